```python
import math
import jax, jax.numpy as jnp
from jax import lax
import numpy as np

D_MODEL = 1024
BATCH = 8
SEQ = 16384
DEPTH = 1

MLA_HEADS = 8
QK_NOPE_DIM = 128
QK_ROPE_DIM = 64
V_HEAD_DIM = 128
Q_LORA_RANK = 384
KV_LORA_RANK = 256
MLA_WIDTH = MLA_HEADS * V_HEAD_DIM
QK_HEAD_DIM = QK_NOPE_DIM + QK_ROPE_DIM
ROPE_THETA = 10000.0
ATTN_BLOCK = 128

SSM_HEAD_DIM = 64
SSM_HEADS = 16
SSM_WIDTH = SSM_HEADS * SSM_HEAD_DIM
SSM_GROUPS = 2
SSM_STATE = 128
CONV_WIDTH = 4
CHUNK = 128
CONV_CH = SSM_WIDTH + 2 * SSM_GROUPS * SSM_STATE

MIX_WIDTH = MLA_WIDTH + SSM_WIDTH

IN_SPLITS = (Q_LORA_RANK, KV_LORA_RANK + QK_ROPE_DIM, MLA_WIDTH, CONV_CH, SSM_HEADS, SSM_WIDTH)
IN_WIDTH = sum(IN_SPLITS)

DEEPNORM_ALPHA = (2.0 * DEPTH) ** 0.25
DEEPNORM_BETA = (8.0 * DEPTH) ** -0.25
RMS_EPS = 1e-6
LN_EPS = 1e-5

kernel_name = "hybrid_mla_ssd_parallel_heads_deepnorm_adaln"


def split_last(x, sizes):
    out, start = [], 0
    for s in sizes:
        out.append(x[..., start:start + s])
        start += s
    return out


def rms_norm(x, g):
    xf = x.astype(jnp.float32)
    y = xf * lax.rsqrt(jnp.mean(xf * xf, axis=-1, keepdims=True) + RMS_EPS)
    return (y * g.astype(jnp.float32)).astype(x.dtype)


def layer_norm(x, g, b):
    xf = x.astype(jnp.float32)
    mu = jnp.mean(xf, axis=-1, keepdims=True)
    var = jnp.mean(jnp.square(xf - mu), axis=-1, keepdims=True)
    y = (xf - mu) * lax.rsqrt(var + LN_EPS)
    return (y * g.astype(jnp.float32) + b.astype(jnp.float32)).astype(x.dtype)


def apply_rope(x, positions):
    half = QK_ROPE_DIM // 2
    inv_freq = 1.0 / (ROPE_THETA ** (jnp.arange(half, dtype=jnp.float32) / half))
    ang = positions.astype(jnp.float32)[..., None] * inv_freq
    cos = jnp.cos(ang)[:, :, None, :]
    sin = jnp.sin(ang)[:, :, None, :]
    xf = x.astype(jnp.float32)
    x1, x2 = xf[..., :half], xf[..., half:]
    out = jnp.concatenate([x1 * cos - x2 * sin, x2 * cos + x1 * sin], axis=-1)
    return out.astype(x.dtype)


def causal_block_attention(q, k, v, scale):
    b, s, h, dq = q.shape
    nb = s // ATTN_BLOCK
    qb = q.reshape(b, nb, ATTN_BLOCK, h, dq).transpose(1, 0, 2, 3, 4)
    key_pos = jnp.arange(s)

    def one_block(args):
        q_blk, i = args
        sc = jnp.einsum('bqhd,bkhd->bhqk', q_blk, k,
                        preferred_element_type=jnp.float32) * scale
        q_pos = i * ATTN_BLOCK + jnp.arange(ATTN_BLOCK)
        mask = key_pos[None, :] <= q_pos[:, None]
        sc = jnp.where(mask[None, None], sc, -jnp.inf)
        p = jax.nn.softmax(sc, axis=-1).astype(v.dtype)
        return jnp.einsum('bhqk,bkhd->bqhd', p, v)

    out = lax.map(one_block, (qb, jnp.arange(nb)))
    return out.transpose(1, 0, 2, 3, 4).reshape(b, s, h, v.shape[-1])


def mla_branch(q_lat, kv_lat, positions, q_norm_g, w_qb, kv_norm_g, w_kvb):
    b, s, _ = q_lat.shape
    q = (rms_norm(q_lat, q_norm_g) @ w_qb).reshape(b, s, MLA_HEADS, QK_HEAD_DIM)
    q_nope, q_rope = q[..., :QK_NOPE_DIM], q[..., QK_NOPE_DIM:]
    c_kv, k_rope = kv_lat[..., :KV_LORA_RANK], kv_lat[..., KV_LORA_RANK:]
    kv = (rms_norm(c_kv, kv_norm_g) @ w_kvb).reshape(b, s, MLA_HEADS, QK_NOPE_DIM + V_HEAD_DIM)
    k_nope, v = kv[..., :QK_NOPE_DIM], kv[..., QK_NOPE_DIM:]
    q_rope = apply_rope(q_rope, positions)
    k_rope = apply_rope(k_rope[:, :, None, :], positions)
    q = jnp.concatenate([q_nope, q_rope], axis=-1)
    k = jnp.concatenate([k_nope, jnp.broadcast_to(k_rope, (b, s, MLA_HEADS, QK_ROPE_DIM))], axis=-1)
    o = causal_block_attention(q, k, v, QK_HEAD_DIM ** -0.5)
    return o.reshape(b, s, MLA_WIDTH)


def causal_depthwise_conv(x, w, bias):
    out = lax.conv_general_dilated(
        x, w[:, None, :], window_strides=(1,), padding=[(CONV_WIDTH - 1, 0)],
        dimension_numbers=('NWC', 'WIO', 'NWC'), feature_group_count=x.shape[-1])
    return out + bias


def ssd_chunked(x, da, bm, cm):
    out_dtype = x.dtype
    b, s, h, p = x.shape
    nc, r = s // CHUNK, h // SSM_GROUPS
    f32 = jnp.float32
    X = x.astype(f32).reshape(b, nc, CHUNK, SSM_GROUPS, r, p)
    A = da.astype(f32).reshape(b, nc, CHUNK, SSM_GROUPS, r)
    Bc = bm.astype(f32).reshape(b, nc, CHUNK, SSM_GROUPS, SSM_STATE)
    Cc = cm.astype(f32).reshape(b, nc, CHUNK, SSM_GROUPS, SSM_STATE)
    a_cum = jnp.cumsum(A, axis=2)
    seg = a_cum[:, :, :, None] - a_cum[:, :, None, :]
    tri = jnp.tril(jnp.ones((CHUNK, CHUNK), dtype=bool))[:, :, None, None]
    decay = jnp.exp(jnp.where(tri, seg, -jnp.inf))
    cb = jnp.einsum('bclgn,bcsgn->bclsg', Cc, Bc)
    y_diag = jnp.einsum('bclsgr,bcsgrp->bclgrp', cb[..., None] * decay, X)
    decay_to_end = jnp.exp(a_cum[:, :, -1:] - a_cum)
    states = jnp.einsum('bclgn,bclgr,bclgrp->bcgrpn', Bc, decay_to_end, X)
    chunk_decay = jnp.exp(a_cum[:, :, -1])

    def step(hc, inp):
        st, dc = inp
        return hc * dc[..., None, None] + st, hc

    h0 = jnp.zeros((b, SSM_GROUPS, r, p, SSM_STATE), f32)
    _, prev = lax.scan(step, h0, (states.transpose(1, 0, 2, 3, 4, 5),
                                   chunk_decay.transpose(1, 0, 2, 3)))
    prev = prev.transpose(1, 0, 2, 3, 4, 5)
    y_off = jnp.einsum('bclgn,bcgrpn,bclgr->bclgrp', Cc, prev, jnp.exp(a_cum))
    return (y_diag + y_off).reshape(b, s, h, p).astype(out_dtype)


def ssd_branch(xbc_raw, dt_raw, z, conv_w, conv_b, dt_bias, a_log, d_skip, ssm_norm_g):
    b, s, _ = xbc_raw.shape
    xbc = jax.nn.silu(causal_depthwise_conv(xbc_raw, conv_w, conv_b))
    xs, bm, cm = split_last(xbc, (SSM_WIDTH, SSM_GROUPS * SSM_STATE, SSM_GROUPS * SSM_STATE))
    xs = xs.reshape(b, s, SSM_HEADS, SSM_HEAD_DIM)
    bm = bm.reshape(b, s, SSM_GROUPS, SSM_STATE)
    cm = cm.reshape(b, s, SSM_GROUPS, SSM_STATE)
    dt = jax.nn.softplus(dt_raw.astype(jnp.float32) + dt_bias.astype(jnp.float32))
    a = -jnp.exp(a_log.astype(jnp.float32))
    y = ssd_chunked(xs * dt[..., None].astype(xs.dtype), dt * a, bm, cm)
    y = y + xs * d_skip[:, None]
    hf = (y.reshape(b, s, SSM_WIDTH).astype(jnp.float32)
          * jax.nn.silu(z.astype(jnp.float32))).reshape(b, s, SSM_GROUPS, -1)
    hf = hf * lax.rsqrt(jnp.mean(hf * hf, axis=-1, keepdims=True) + RMS_EPS)
    return (hf.reshape(b, s, SSM_WIDTH) * ssm_norm_g.astype(jnp.float32)).astype(xbc_raw.dtype)


def hybrid_layer(x, c, positions, w_ada, b_ada, w_in, q_norm_g, w_qb, kv_norm_g, w_kvb,
                 conv_w, conv_b, dt_bias, a_log, d_skip, ssm_norm_g, w_out, ln_g, ln_b):
    mod = c @ w_ada + b_ada
    shift, scale, gate = mod[:, :D_MODEL], mod[:, D_MODEL:2 * D_MODEL], mod[:, 2 * D_MODEL:]
    u = x * (1.0 + scale[:, None, :]) + shift[:, None, :]
    proj = u @ w_in
    q_lat, kv_lat, z_attn, xbc, dt_raw, z_ssm = split_last(proj, IN_SPLITS)
    o_attn = mla_branch(q_lat, kv_lat, positions, q_norm_g, w_qb, kv_norm_g, w_kvb) * jax.nn.silu(z_attn)
    o_ssm = ssd_branch(xbc, dt_raw, z_ssm, conv_w, conv_b, dt_bias, a_log, d_skip, ssm_norm_g)
    mixed = jnp.concatenate([o_attn, o_ssm], axis=-1) @ w_out
    return layer_norm(DEEPNORM_ALPHA * x + gate[:, None, :] * mixed, ln_g, ln_b)


def _fwd_setup_inputs(seed: int = 0) -> dict:
    key = jax.random.key(seed)
    ks = jax.random.split(key, 20)
    f32 = jnp.float32
    n = lambda k, shape, s: jax.random.normal(k, shape, f32) * s
    L = DEPTH
    dt0 = jnp.exp(jax.random.uniform(ks[12], (L, SSM_HEADS), f32, math.log(1e-3), math.log(1e-1)))
    return {
        "x": jax.random.normal(ks[0], (BATCH, SEQ, D_MODEL), f32),
        "c": jax.random.normal(ks[1], (BATCH, D_MODEL), f32),
        "positions": jnp.broadcast_to(jnp.arange(SEQ, dtype=jnp.int32), (BATCH, SEQ)),
        "w_ada": n(ks[2], (L, D_MODEL, 3 * D_MODEL), 0.5 * D_MODEL ** -0.5),
        "b_ada": n(ks[3], (L, 3 * D_MODEL), 0.02),
        "w_in": n(ks[4], (L, D_MODEL, IN_WIDTH), D_MODEL ** -0.5),
        "q_norm_g": 1.0 + n(ks[5], (L, Q_LORA_RANK), 0.05),
        "w_qb": n(ks[6], (L, Q_LORA_RANK, MLA_HEADS * QK_HEAD_DIM), Q_LORA_RANK ** -0.5),
        "kv_norm_g": 1.0 + n(ks[7], (L, KV_LORA_RANK), 0.05),
        "w_kvb": n(ks[8], (L, KV_LORA_RANK, MLA_HEADS * (QK_NOPE_DIM + V_HEAD_DIM)), KV_LORA_RANK ** -0.5),
        "conv_w": n(ks[9], (L, CONV_WIDTH, CONV_CH), CONV_WIDTH ** -0.5),
        "conv_b": n(ks[10], (L, CONV_CH), 0.02),
        "dt_bias": dt0 + jnp.log(-jnp.expm1(-dt0)),
        "a_log": jnp.log(jax.random.uniform(ks[13], (L, SSM_HEADS), f32, 1.0, 16.0)),
        "d_skip": 1.0 + n(ks[14], (L, SSM_HEADS), 0.1),
        "ssm_norm_g": 1.0 + n(ks[15], (L, SSM_WIDTH), 0.05),
        "w_out": n(ks[16], (L, MIX_WIDTH, D_MODEL), DEEPNORM_BETA * MIX_WIDTH ** -0.5),
        "ln_g": 1.0 + n(ks[17], (L, D_MODEL), 0.05),
        "ln_b": n(ks[18], (L, D_MODEL), 0.02),
    }


def _fwd_reference(x, c, positions, w_ada, b_ada, w_in, q_norm_g, w_qb, kv_norm_g, w_kvb,
              conv_w, conv_b, dt_bias, a_log, d_skip, ssm_norm_g, w_out, ln_g, ln_b):
    h = x
    for l in range(DEPTH):
        h = hybrid_layer(h, c, positions, w_ada[l], b_ada[l], w_in[l], q_norm_g[l], w_qb[l],
                         kv_norm_g[l], w_kvb[l], conv_w[l], conv_b[l], dt_bias[l], a_log[l],
                         d_skip[l], ssm_norm_g[l], w_out[l], ln_g[l], ln_b[l])
    return h


import jax as _jax
import jax.numpy as _jnp

TWIN_FORMAT = 'train_step'
FWD_PARAMS = ['x', 'c', 'positions', 'w_ada', 'b_ada', 'w_in', 'q_norm_g', 'w_qb', 'kv_norm_g', 'w_kvb', 'conv_w', 'conv_b', 'dt_bias', 'a_log', 'd_skip', 'ssm_norm_g', 'w_out', 'ln_g', 'ln_b']
TWIN_WEIGHTS = ['w_ada', 'b_ada', 'w_in', 'q_norm_g', 'w_qb', 'kv_norm_g', 'w_kvb', 'conv_w', 'conv_b', 'dt_bias', 'a_log', 'd_skip', 'ssm_norm_g', 'w_out', 'ln_g', 'ln_b']
TWIN_DIFF_INPUT = 'x'
TWIN_INPUTS = ['x', 'c', 'positions', 'w_ada', 'b_ada', 'w_in', 'q_norm_g', 'w_qb', 'kv_norm_g', 'w_kvb', 'conv_w', 'conv_b', 'dt_bias', 'a_log', 'd_skip', 'ssm_norm_g', 'w_out', 'ln_g', 'ln_b', 'loss_target', 'm_w_ada', 'm_b_ada', 'm_w_in', 'm_q_norm_g', 'm_w_qb', 'm_kv_norm_g', 'm_w_kvb', 'm_conv_w', 'm_conv_b', 'm_dt_bias', 'm_a_log', 'm_d_skip', 'm_ssm_norm_g', 'm_w_out', 'm_ln_g', 'm_ln_b', 'v_w_ada', 'v_b_ada', 'v_w_in', 'v_q_norm_g', 'v_w_qb', 'v_kv_norm_g', 'v_w_kvb', 'v_conv_w', 'v_conv_b', 'v_dt_bias', 'v_a_log', 'v_d_skip', 'v_ssm_norm_g', 'v_w_out', 'v_ln_g', 'v_ln_b']
TWIN_OUTPUTS = ['loss', 'grad_x', 'grad_w_ada', 'grad_b_ada', 'grad_w_in', 'grad_q_norm_g', 'grad_w_qb', 'grad_kv_norm_g', 'grad_w_kvb', 'grad_conv_w', 'grad_conv_b', 'grad_dt_bias', 'grad_a_log', 'grad_d_skip', 'grad_ssm_norm_g', 'grad_w_out', 'grad_ln_g', 'grad_ln_b', 'delta_w_ada', 'delta_b_ada', 'delta_w_in', 'delta_q_norm_g', 'delta_w_qb', 'delta_kv_norm_g', 'delta_w_kvb', 'delta_conv_w', 'delta_conv_b', 'delta_dt_bias', 'delta_a_log', 'delta_d_skip', 'delta_ssm_norm_g', 'delta_w_out', 'delta_ln_g', 'delta_ln_b', 'new_m_w_ada', 'new_m_b_ada', 'new_m_w_in', 'new_m_q_norm_g', 'new_m_w_qb', 'new_m_kv_norm_g', 'new_m_w_kvb', 'new_m_conv_w', 'new_m_conv_b', 'new_m_dt_bias', 'new_m_a_log', 'new_m_d_skip', 'new_m_ssm_norm_g', 'new_m_w_out', 'new_m_ln_g', 'new_m_ln_b', 'new_v_w_ada', 'new_v_b_ada', 'new_v_w_in', 'new_v_q_norm_g', 'new_v_w_qb', 'new_v_kv_norm_g', 'new_v_w_kvb', 'new_v_conv_w', 'new_v_conv_b', 'new_v_dt_bias', 'new_v_a_log', 'new_v_d_skip', 'new_v_ssm_norm_g', 'new_v_w_out', 'new_v_ln_g', 'new_v_ln_b']
TWIN_LEAF_KINDS = {'loss': 'loss', 'grad_x': 'grad_x', 'grad_w_ada': 'grad_w', 'grad_b_ada': 'grad_w', 'grad_w_in': 'grad_w', 'grad_q_norm_g': 'grad_w', 'grad_w_qb': 'grad_w', 'grad_kv_norm_g': 'grad_w', 'grad_w_kvb': 'grad_w', 'grad_conv_w': 'grad_w', 'grad_conv_b': 'grad_w', 'grad_dt_bias': 'grad_w', 'grad_a_log': 'grad_w', 'grad_d_skip': 'grad_w', 'grad_ssm_norm_g': 'grad_w', 'grad_w_out': 'grad_w', 'grad_ln_g': 'grad_w', 'grad_ln_b': 'grad_w', 'delta_w_ada': 'delta_w', 'delta_b_ada': 'delta_w', 'delta_w_in': 'delta_w', 'delta_q_norm_g': 'delta_w', 'delta_w_qb': 'delta_w', 'delta_kv_norm_g': 'delta_w', 'delta_w_kvb': 'delta_w', 'delta_conv_w': 'delta_w', 'delta_conv_b': 'delta_w', 'delta_dt_bias': 'delta_w', 'delta_a_log': 'delta_w', 'delta_d_skip': 'delta_w', 'delta_ssm_norm_g': 'delta_w', 'delta_w_out': 'delta_w', 'delta_ln_g': 'delta_w', 'delta_ln_b': 'delta_w', 'new_m_w_ada': 'new_m', 'new_m_b_ada': 'new_m', 'new_m_w_in': 'new_m', 'new_m_q_norm_g': 'new_m', 'new_m_w_qb': 'new_m', 'new_m_kv_norm_g': 'new_m', 'new_m_w_kvb': 'new_m', 'new_m_conv_w': 'new_m', 'new_m_conv_b': 'new_m', 'new_m_dt_bias': 'new_m', 'new_m_a_log': 'new_m', 'new_m_d_skip': 'new_m', 'new_m_ssm_norm_g': 'new_m', 'new_m_w_out': 'new_m', 'new_m_ln_g': 'new_m', 'new_m_ln_b': 'new_m', 'new_v_w_ada': 'new_v', 'new_v_b_ada': 'new_v', 'new_v_w_in': 'new_v', 'new_v_q_norm_g': 'new_v', 'new_v_w_qb': 'new_v', 'new_v_kv_norm_g': 'new_v', 'new_v_w_kvb': 'new_v', 'new_v_conv_w': 'new_v', 'new_v_conv_b': 'new_v', 'new_v_dt_bias': 'new_v', 'new_v_a_log': 'new_v', 'new_v_d_skip': 'new_v', 'new_v_ssm_norm_g': 'new_v', 'new_v_w_out': 'new_v', 'new_v_ln_g': 'new_v', 'new_v_ln_b': 'new_v'}


def _forward(args):
    return _fwd_reference(*[args[k] for k in FWD_PARAMS])


def _output_shape():
    def fwd():
        inp = _fwd_setup_inputs(0)
        return _fwd_reference(*[inp[k] for k in FWD_PARAMS])
    out = _jax.eval_shape(fwd)
    return out.shape, out.dtype

N_MICROBATCH = 1
ADAM_LR = 0.001
ADAM_B1 = 0.9
ADAM_B2 = 0.999
ADAM_EPS = 1e-08
ADAM_WD = 0.01
ADAM_STEP = 10
PER_EXAMPLE_BATCH_AXIS = {'x': 0, 'c': 0, 'positions': 0, 'loss_target': 0}
SHARED_INPUTS = []
_WEIGHT_DTYPES = {'w_ada': _jnp.float32, 'b_ada': _jnp.float32, 'w_in': _jnp.float32, 'q_norm_g': _jnp.float32, 'w_qb': _jnp.float32, 'kv_norm_g': _jnp.float32, 'w_kvb': _jnp.float32, 'conv_w': _jnp.float32, 'conv_b': _jnp.float32, 'dt_bias': _jnp.float32, 'a_log': _jnp.float32, 'd_skip': _jnp.float32, 'ssm_norm_g': _jnp.float32, 'w_out': _jnp.float32, 'ln_g': _jnp.float32, 'ln_b': _jnp.float32}
MOMENT_SCALE = {'w_ada': 1.411301e-01, 'b_ada': 1.489333e-01, 'w_in': 5.011078e-02, 'q_norm_g': 1.263755e-02, 'w_qb': 6.494573e-03, 'kv_norm_g': 4.610695e-02, 'w_kvb': 1.586027e-02, 'conv_w': 5.390069e-02, 'conv_b': 6.860434e-02, 'dt_bias': 1.722261e-01, 'a_log': 6.570173e-01, 'd_skip': 2.695072e-01, 'ssm_norm_g': 8.371070e-02, 'w_out': 1.220237e-01, 'ln_g': 1.283237e+02, 'ln_b': 2.614840e+00}


def _to_microbatches(a, axis):
    t = _jnp.moveaxis(a, axis, 0)
    t = t.reshape((N_MICROBATCH, t.shape[0] // N_MICROBATCH) + t.shape[1:])
    return _jnp.moveaxis(t, 1, axis + 1)


def setup_inputs(seed: int = 0) -> dict:
    inp = _fwd_setup_inputs(seed)
    key = _jax.random.fold_in(_jax.random.key(seed), 7919)
    shape, _ = _output_shape()
    out = dict(inp)
    out["loss_target"] = _jax.random.normal(_jax.random.fold_in(key, 0), shape, _jnp.float32)
    for i, name in enumerate(TWIN_WEIGHTS):
        w = inp[name].astype(_jnp.float32)
        if MOMENT_SCALE is None:
            s = _jnp.sqrt(_jnp.mean(_jnp.square(w)) + 1e-30)
        else:
            s = MOMENT_SCALE[name]
        km, kv = _jax.random.split(_jax.random.fold_in(key, i + 1))
        out[name] = w
        out["m_" + name] = s * _jax.random.normal(km, w.shape, _jnp.float32)
        out["v_" + name] = (s * s) * _jax.random.uniform(kv, w.shape, _jnp.float32, 0.5, 1.5)
    if N_MICROBATCH > 1:
        for name, axis in PER_EXAMPLE_BATCH_AXIS.items():
            out[name] = _to_microbatches(out[name], axis)
    return {'x': out['x'], 'c': out['c'], 'positions': out['positions'], 'w_ada': out['w_ada'], 'b_ada': out['b_ada'], 'w_in': out['w_in'], 'q_norm_g': out['q_norm_g'], 'w_qb': out['w_qb'], 'kv_norm_g': out['kv_norm_g'], 'w_kvb': out['w_kvb'], 'conv_w': out['conv_w'], 'conv_b': out['conv_b'], 'dt_bias': out['dt_bias'], 'a_log': out['a_log'], 'd_skip': out['d_skip'], 'ssm_norm_g': out['ssm_norm_g'], 'w_out': out['w_out'], 'ln_g': out['ln_g'], 'ln_b': out['ln_b'], 'loss_target': out['loss_target'], 'm_w_ada': out['m_w_ada'], 'm_b_ada': out['m_b_ada'], 'm_w_in': out['m_w_in'], 'm_q_norm_g': out['m_q_norm_g'], 'm_w_qb': out['m_w_qb'], 'm_kv_norm_g': out['m_kv_norm_g'], 'm_w_kvb': out['m_w_kvb'], 'm_conv_w': out['m_conv_w'], 'm_conv_b': out['m_conv_b'], 'm_dt_bias': out['m_dt_bias'], 'm_a_log': out['m_a_log'], 'm_d_skip': out['m_d_skip'], 'm_ssm_norm_g': out['m_ssm_norm_g'], 'm_w_out': out['m_w_out'], 'm_ln_g': out['m_ln_g'], 'm_ln_b': out['m_ln_b'], 'v_w_ada': out['v_w_ada'], 'v_b_ada': out['v_b_ada'], 'v_w_in': out['v_w_in'], 'v_q_norm_g': out['v_q_norm_g'], 'v_w_qb': out['v_w_qb'], 'v_kv_norm_g': out['v_kv_norm_g'], 'v_w_kvb': out['v_w_kvb'], 'v_conv_w': out['v_conv_w'], 'v_conv_b': out['v_conv_b'], 'v_dt_bias': out['v_dt_bias'], 'v_a_log': out['v_a_log'], 'v_d_skip': out['v_d_skip'], 'v_ssm_norm_g': out['v_ssm_norm_g'], 'v_w_out': out['v_w_out'], 'v_ln_g': out['v_ln_g'], 'v_ln_b': out['v_ln_b']}


def _loss(weights, diff, rest, loss_target):
    with _jax.named_scope("forward"):
        args = {**rest, TWIN_DIFF_INPUT: diff, **{k: w.astype(_WEIGHT_DTYPES[k]) for k, w in weights.items()}}
        y = _forward(args)
    with _jax.named_scope("loss_head"):
        err = _jnp.square(y.astype(_jnp.float32) - loss_target)
        return 0.5 * _jnp.sum(_jnp.mean(err, axis=-1)) if err.ndim else 0.5 * err


def _adamw(w, g, m, v):
    m = ADAM_B1 * m + (1.0 - ADAM_B1) * g
    v = ADAM_B2 * v + (1.0 - ADAM_B2) * _jnp.square(g)
    m_hat = m / (1.0 - ADAM_B1 ** ADAM_STEP)
    v_hat = v / (1.0 - ADAM_B2 ** ADAM_STEP)
    delta = -ADAM_LR * (m_hat / (_jnp.sqrt(v_hat) + ADAM_EPS) + ADAM_WD * w)
    return delta, m, v


def reference(x, c, positions, w_ada, b_ada, w_in, q_norm_g, w_qb, kv_norm_g, w_kvb, conv_w, conv_b, dt_bias, a_log, d_skip, ssm_norm_g, w_out, ln_g, ln_b, loss_target, m_w_ada, m_b_ada, m_w_in, m_q_norm_g, m_w_qb, m_kv_norm_g, m_w_kvb, m_conv_w, m_conv_b, m_dt_bias, m_a_log, m_d_skip, m_ssm_norm_g, m_w_out, m_ln_g, m_ln_b, v_w_ada, v_b_ada, v_w_in, v_q_norm_g, v_w_qb, v_kv_norm_g, v_w_kvb, v_conv_w, v_conv_b, v_dt_bias, v_a_log, v_d_skip, v_ssm_norm_g, v_w_out, v_ln_g, v_ln_b):
    given = dict(x=x, c=c, positions=positions, w_ada=w_ada, b_ada=b_ada, w_in=w_in, q_norm_g=q_norm_g, w_qb=w_qb, kv_norm_g=kv_norm_g, w_kvb=w_kvb, conv_w=conv_w, conv_b=conv_b, dt_bias=dt_bias, a_log=a_log, d_skip=d_skip, ssm_norm_g=ssm_norm_g, w_out=w_out, ln_g=ln_g, ln_b=ln_b, loss_target=loss_target, m_w_ada=m_w_ada, m_b_ada=m_b_ada, m_w_in=m_w_in, m_q_norm_g=m_q_norm_g, m_w_qb=m_w_qb, m_kv_norm_g=m_kv_norm_g, m_w_kvb=m_w_kvb, m_conv_w=m_conv_w, m_conv_b=m_conv_b, m_dt_bias=m_dt_bias, m_a_log=m_a_log, m_d_skip=m_d_skip, m_ssm_norm_g=m_ssm_norm_g, m_w_out=m_w_out, m_ln_g=m_ln_g, m_ln_b=m_ln_b, v_w_ada=v_w_ada, v_b_ada=v_b_ada, v_w_in=v_w_in, v_q_norm_g=v_q_norm_g, v_w_qb=v_w_qb, v_kv_norm_g=v_kv_norm_g, v_w_kvb=v_w_kvb, v_conv_w=v_conv_w, v_conv_b=v_conv_b, v_dt_bias=v_dt_bias, v_a_log=v_a_log, v_d_skip=v_d_skip, v_ssm_norm_g=v_ssm_norm_g, v_w_out=v_w_out, v_ln_g=v_ln_g, v_ln_b=v_ln_b)
    weights = {n: given[n] for n in TWIN_WEIGHTS}
    shared = {n: given[n] for n in SHARED_INPUTS}
    per_example = {n: given[n] for n in ['x', 'c', 'positions']}
    grad_fn = _jax.value_and_grad(_loss, argnums=(0, 1))

    def one_microbatch(ex, loss_target):
        ex = dict(ex)
        diff = ex.pop(TWIN_DIFF_INPUT)
        return grad_fn(weights, diff, {**shared, **ex}, loss_target)

    if N_MICROBATCH == 1:
        loss, (grad_w, grad_x) = one_microbatch(per_example, given["loss_target"])
    else:
        def body(carry, xs):
            loss_sum, grad_sum = carry
            l_k, (gw_k, gx_k) = one_microbatch(xs[0], xs[1])
            with _jax.named_scope("update"):
                return (loss_sum + l_k, _jax.tree.map(_jnp.add, grad_sum, gw_k)), gx_k

        init = (_jnp.zeros((), _jnp.float32), _jax.tree.map(_jnp.zeros_like, weights))
        (loss, grad_w), grad_x = _jax.lax.scan(body, init, (per_example, given["loss_target"]))
    with _jax.named_scope("update"):
        delta_w, new_m, new_v = {}, {}, {}
        for n in TWIN_WEIGHTS:
            delta_w[n], new_m[n], new_v[n] = _adamw(weights[n], grad_w[n], given["m_" + n], given["v_" + n])
    return (loss, grad_x, *[grad_w[n] for n in TWIN_WEIGHTS], *[delta_w[n] for n in TWIN_WEIGHTS],
            *[new_m[n] for n in TWIN_WEIGHTS], *[new_v[n] for n in TWIN_WEIGHTS])
```

```python
import functools
import math

import numpy as np
import jax
import jax.numpy as jnp
from jax import lax
from jax.experimental import pallas as pl
from jax.experimental.pallas import tpu as pltpu

F32 = jnp.float32
BF16 = jnp.bfloat16
HIGHEST = lax.Precision.HIGHEST
MESH_ID = pl.DeviceIdType.MESH

D = 1024
NH = 8
NOPE = 128
ROPE = 64
VD = 128
QKD = NOPE + ROPE
HP = 256
QL = 384
KVL = 256
ROPE_THETA = 10000.0
SH = 16
SP = 64
SG = 2
SN = 128
CW = 4
CH = 128
SW = SH * SP
CC = SW + 2 * SG * SN
GW = SW // SG
MIX = 2 * D
IN_W = 4304
ALPHA = 2.0 ** 0.25
RMS_EPS = 1e-6
LN_EPS = 1e-5
SCALE = QKD ** -0.5
LR, B1, B2, EPS, WD, STEP = 0.001, 0.9, 0.999, 1e-08, 0.01, 10

P_Q = (0, 384)
P_KV = (384, 640)
P_ZA = (640, 1664)
P_XBC = (1664, 3200)
P_ZS = (3200, 4224)
P_KR = (4224, 4352)
P_DT = (4352, 4480)
IN_P = 4480

R_ADA, R_IN, R_QB, R_KVB, R_OUT = 768, 1076, 144, 128, 512
R_BIG = 2640
R_TILE = 240
R_SMALL = 16

TM = 256
TQ = 512
VMEM_LIMIT = 56 * 1024 * 1024


def _cp(sem=None):
    return pltpu.CompilerParams(dimension_semantics=sem, vmem_limit_bytes=VMEM_LIMIT)


def _mm(a, b):
    return jnp.dot(a, b, preferred_element_type=F32)


def _nt(a, b):
    return lax.dot_general(a, b, (((1,), (1,)), ((), ())), preferred_element_type=F32)


def _tn(a, b):
    return lax.dot_general(a, b, (((0,), (0,)), ((), ())), preferred_element_type=F32)


def _mm_hi(a, b):
    return jnp.dot(a, b, precision=HIGHEST, preferred_element_type=F32)


def _nt_hi(a, b):
    return lax.dot_general(a, b, (((1,), (1,)), ((), ())), precision=HIGHEST, preferred_element_type=F32)


def _sigmoid(z):
    return 1.0 / (1.0 + jnp.exp(-z))


def _softplus(z):
    return jnp.maximum(z, 0.0) + jnp.log1p(jnp.exp(-jnp.abs(z)))


def _rope(t, cs, sn):
    lane = lax.broadcasted_iota(jnp.int32, t.shape, 1)
    rot = jnp.where(lane < ROPE // 2, -pltpu.roll(t, 128 - ROPE // 2, 1), pltpu.roll(t, ROPE // 2, 1))
    return t * cs + rot * sn


def _rope_t(t, cs, sn):
    lane = lax.broadcasted_iota(jnp.int32, t.shape, 1)
    y = t * sn
    rot = jnp.where(lane < ROPE // 2, -pltpu.roll(y, 128 - ROPE // 2, 1), pltpu.roll(y, ROPE // 2, 1))
    return t * cs - rot


def _full(shape):
    n = len(shape)
    return pl.BlockSpec(shape, lambda *_: (0,) * n)


def _const(shape):
    n = len(shape)
    return pl.BlockSpec(shape, lambda *_: (0,) * n, pipeline_mode=pl.Buffered(1))


def _gather_weights(wb, ws):
    def body(wb_ref, ws_ref, ob_ref, os_ref, send_sems, recv_sems, local_sems):
        x, y, c = lax.axis_index("x"), lax.axis_index("y"), lax.axis_index("c")
        me = 2 * x + y
        chips = [(1 - x, y), (x, 1 - y), (1 - x, 1 - y)]
        pairs = ((wb_ref, ob_ref), (ws_ref, os_ref))

        def copy(a, j, slot, to):
            src, dst = pairs[a]
            return pltpu.make_async_remote_copy(
                src_ref=src, dst_ref=dst.at[slot], send_sem=send_sems.at[a, j], recv_sem=recv_sems.at[a, j],
                device_id=to, device_id_type=MESH_ID)

        local = [pltpu.make_async_copy(pairs[a][0], pairs[a][1].at[me], local_sems.at[a]) for a in range(2)]
        for cp in local:
            cp.start()
        sends = [copy(a, j, me, (px, py, c)) for a in range(2) for j, (px, py) in enumerate(chips)]
        for cp in sends:
            cp.start()
        for a in range(2):
            for j, (px, py) in enumerate(chips):
                copy(a, j, 2 * px + py, (x, y, c)).wait_recv()
        for cp in sends:
            cp.wait_send()
        for cp in local:
            cp.wait()

    hbm = pl.BlockSpec(memory_space=pltpu.HBM)
    return pl.pallas_call(
        body, name="gather_weights",
        out_shape=(jax.ShapeDtypeStruct((4,) + wb.shape, wb.dtype), jax.ShapeDtypeStruct((4,) + ws.shape, ws.dtype)),
        in_specs=[hbm, hbm], out_specs=(hbm, hbm),
        scratch_shapes=[pltpu.SemaphoreType.DMA((2, 3)), pltpu.SemaphoreType.DMA((2, 3)), pltpu.SemaphoreType.DMA((2,))],
    )(wb, ws)


def _exchange_grads(gb, gs):
    def body(gb_ref, gs_ref, rb_ref, rs_ref, send_sems, recv_sems, local_sems):
        x, y, c = lax.axis_index("x"), lax.axis_index("y"), lax.axis_index("c")
        chip = 2 * x + y
        sibling = (x, y, 1 - c)
        chips = [(1 - x, y), (x, 1 - y), (1 - x, 1 - y)]
        pairs = ((gb_ref, rb_ref), (gs_ref, rs_ref))

        def slot(px, py, pc):
            return 4 * px + 2 * py + pc

        def copy(a, k, src, s, to):
            return pltpu.make_async_remote_copy(
                src_ref=src, dst_ref=pairs[a][1].at[s], send_sem=send_sems.at[a, k], recv_sem=recv_sems.at[a, k],
                device_id=to, device_id_type=MESH_ID)

        mine = slot(x, y, c)
        local = [pltpu.make_async_copy(pairs[a][0].at[chip], pairs[a][1].at[mine], local_sems.at[a]) for a in range(2)]
        for cp in local:
            cp.start()
        first = []
        for a in range(2):
            g_ref, r_ref = pairs[a]
            first.append(copy(a, 0, g_ref.at[chip], mine, sibling))
            for j, (px, py) in enumerate(chips):
                first.append(copy(a, 1 + j, g_ref.at[2 * px + py], mine, (px, py, c)))
        for cp in first:
            cp.start()
        passed = []
        for a in range(2):
            g_ref, r_ref = pairs[a]
            for j, (px, py) in enumerate(chips):
                s = slot(px, py, c)
                copy(a, 1 + j, g_ref.at[chip], s, (x, y, c)).wait_recv()
                fwd = copy(a, 4 + j, r_ref.at[s], s, sibling)
                fwd.start()
                passed.append(fwd)
        for a in range(2):
            g_ref, r_ref = pairs[a]
            copy(a, 0, g_ref.at[chip], slot(x, y, 1 - c), (x, y, c)).wait_recv()
            for j, (px, py) in enumerate(chips):
                copy(a, 4 + j, g_ref.at[chip], slot(px, py, 1 - c), (x, y, c)).wait_recv()
        for cp in first + passed:
            cp.wait_send()
        for cp in local:
            cp.wait()

    hbm = pl.BlockSpec(memory_space=pltpu.HBM)
    return pl.pallas_call(
        body, name="exchange_grads",
        out_shape=(jax.ShapeDtypeStruct((8,) + gb.shape[1:], gb.dtype), jax.ShapeDtypeStruct((8,) + gs.shape[1:], gs.dtype)),
        in_specs=[hbm, hbm], out_specs=(hbm, hbm),
        scratch_shapes=[pltpu.SemaphoreType.DMA((2, 7)), pltpu.SemaphoreType.DMA((2, 7)), pltpu.SemaphoreType.DMA((2,))],
    )(gb, gs)


def _ada(c8, w_ada, b_ada):
    def body(c_ref, w_ref, b_ref, o_ref):
        o_ref[...] = _mm(c_ref[...].astype(BF16), w_ref[...]) + b_ref[...]

    return pl.pallas_call(body, name="ada", out_shape=jax.ShapeDtypeStruct((8, 3 * D), F32),
                          compiler_params=_cp())(c8, w_ada, b_ada)


def _inproj(x, mod, win, qg, kvg, wq, wk, wv, pos, invf):
    S = x.shape[0]
    tm = min(TM, S)

    def body(x_ref, mod_ref, win_ref, qg_ref, kvg_ref, wq_ref, wk_ref, wv_ref, pos_ref, invf_ref,
             qlat_ref, ckv_ref, za_ref, xbc_ref, zs_ref, dt_ref, q_ref, k_ref, v_ref):
        shift = mod_ref[0:1, 0:D]
        scale = mod_ref[0:1, D:2 * D]
        u = (x_ref[...] * (1.0 + scale) + shift).astype(BF16)

        def proj(p):
            return _mm(u, win_ref[:, p[0]:p[1]])

        ql = proj(P_Q)
        ckv = proj(P_KV)
        qlat_ref[...] = ql
        ckv_ref[...] = ckv
        za_ref[...] = proj(P_ZA)
        xbc_ref[...] = proj(P_XBC)
        zs_ref[...] = proj(P_ZS)
        dt_ref[...] = proj(P_DT)
        kr = proj(P_KR)

        ang = pos_ref[...].astype(F32) * invf_ref[...]
        cs = jnp.cos(ang)
        sn = jnp.sin(ang)

        rq = lax.rsqrt(jnp.mean(ql * ql, axis=-1, keepdims=True) + RMS_EPS)
        qn = (ql * rq * qg_ref[...]).astype(BF16)
        for h in range(NH):
            qh = _mm(qn, wq_ref[:, h * HP:(h + 1) * HP])
            q_ref[h, :, 0:NOPE] = (qh[:, 0:NOPE] * SCALE).astype(BF16)
            q_ref[h, :, NOPE:HP] = (_rope(qh[:, NOPE:HP], cs, sn) * SCALE).astype(BF16)

        rk = lax.rsqrt(jnp.mean(ckv * ckv, axis=-1, keepdims=True) + RMS_EPS)
        kn = (ckv * rk * kvg_ref[...]).astype(BF16)
        knope = _mm(kn, wk_ref[...])
        vall = _mm(kn, wv_ref[...])
        krr = _rope(kr, cs, sn).astype(BF16)
        for h in range(NH):
            k_ref[h, :, 0:NOPE] = knope[:, h * NOPE:(h + 1) * NOPE].astype(BF16)
            k_ref[h, :, NOPE:HP] = krr
            v_ref[h] = vall[:, h * VD:(h + 1) * VD].astype(BF16)

    row = lambda n: pl.BlockSpec((tm, n), lambda i: (i, 0))
    heads = lambda n: pl.BlockSpec((NH, tm, n), lambda i: (0, i, 0))
    sd = lambda n: jax.ShapeDtypeStruct((S, n), F32)
    hd = lambda n: jax.ShapeDtypeStruct((NH, S, n), BF16)
    return pl.pallas_call(
        body, name="inproj", grid=(S // tm,),
        in_specs=[row(D), _const((8, 3 * D)), _const((D, IN_P)), _const((1, QL)), _const((1, KVL)),
                  _const((QL, NH * HP)), _const((KVL, NH * NOPE)), _const((KVL, NH * VD)),
                  row(1), _const((1, 128))],
        out_specs=[row(QL), row(KVL), row(D), row(CC), row(D), row(128), heads(HP), heads(HP), heads(VD)],
        out_shape=[sd(QL), sd(KVL), sd(D), sd(CC), sd(D), sd(128), hd(HP), hd(HP), hd(VD)],
        compiler_params=_cp(("arbitrary",)),
    )(x, mod, win, qg, kvg, wq, wk, wv, pos, invf)


def _attn_fwd(q, k, v):
    _, S, _ = q.shape
    tq = min(TQ, S)
    nq = S // tq

    def body(q_ref, k_ref, v_ref, o_ref, lse_ref):
        i = pl.program_id(1)
        qb = q_ref[...]

        def step(j, carry, masked):
            m, l, acc = carry
            off = pl.multiple_of(j * tq, tq)
            kb = k_ref[pl.ds(off, tq), :]
            vb = v_ref[pl.ds(off, tq), :]
            s = _nt(qb, kb)
            if masked:
                r = lax.broadcasted_iota(jnp.int32, s.shape, 0)
                cidx = lax.broadcasted_iota(jnp.int32, s.shape, 1)
                s = jnp.where(r >= cidx, s, -1e30)
            m_new = jnp.maximum(m, jnp.max(s, axis=1, keepdims=True))
            a = jnp.exp(m - m_new)
            p = jnp.exp(s - m_new)
            l = a * l + jnp.sum(p, axis=1, keepdims=True)
            acc = a * acc + _mm(p.astype(BF16), vb)
            return m_new, l, acc

        init = (jnp.full((tq, 1), -1e30, F32), jnp.zeros((tq, 1), F32), jnp.zeros((tq, VD), F32))
        carry = lax.fori_loop(0, i, lambda j, cr: step(j, cr, False), init)
        m, l, acc = step(i, carry, True)
        o_ref[...] = acc / l
        lse_ref[...] = m + jnp.log(l)

    return pl.pallas_call(
        body, name="attn_fwd", grid=(NH, nq),
        in_specs=[pl.BlockSpec((None, tq, HP), lambda h, i: (h, i, 0)),
                  pl.BlockSpec((None, S, HP), lambda h, i: (h, 0, 0)),
                  pl.BlockSpec((None, S, VD), lambda h, i: (h, 0, 0))],
        out_specs=[pl.BlockSpec((tq, VD), lambda h, i: (i, h)),
                   pl.BlockSpec((None, tq, 1), lambda h, i: (h, i, 0))],
        out_shape=[jax.ShapeDtypeStruct((S, NH * VD), F32), jax.ShapeDtypeStruct((NH, S, 1), F32)],
        compiler_params=_cp(("arbitrary", "arbitrary")),
    )(q, k, v)


def _ssd_consts():
    tri = np.tril(np.ones((CH, CH), np.float32))
    e16 = np.zeros((128, SW), np.float32)
    for h in range(SH):
        e16[h, h * SP:(h + 1) * SP] = 1.0
    return jnp.asarray(tri), jnp.asarray(tri.T.copy()), jnp.asarray(e16)


def _ssd_chunk_fwd_common(xbc_ref, halo_ref, dtraw_ref, cw_ref, cb_ref, dtb_ref, alog_ref, tri_ref, e16_ref, ext, first):
    ext[0:8, :] = jnp.where(first, 0.0, halo_ref[...])
    ext[8:8 + CH, :] = xbc_ref[...]
    cw = cw_ref[...]
    xc = cb_ref[...] + cw[0:1, :] * ext[5:5 + CH, :]
    for kk in range(1, CW):
        xc = xc + cw[kk:kk + 1, :] * ext[5 + kk:5 + kk + CH, :]
    sact = _sigmoid(xc)
    act = xc * sact
    lane = lax.broadcasted_iota(jnp.int32, (1, 128), 1)
    arow = jnp.where(lane < SH, -jnp.exp(alog_ref[...]), 0.0)
    dtpre = dtraw_ref[...] + dtb_ref[...]
    dt = _softplus(dtpre)
    a = dt * arow
    cum = _mm_hi(tri_ref[...], a)
    cumx = _mm_hi(cum, e16_ref[...])
    dtx = _mm_hi(dt, e16_ref[...])
    return xc, sact, act, arow, dtpre, dt, cum, cumx, dtx


def _ssd_fwd(xbc, dtraw, zs, conv_w, conv_b, dtb, alog, dskx, gssm):
    S = xbc.shape[0]
    nc = S // CH
    tri, _, e16 = _ssd_consts()

    def body(xbc_ref, halo_ref, dtraw_ref, zs_ref, cw_ref, cb_ref, dtb_ref, alog_ref, dsk_ref, g_ref, tri_ref, e16_ref,
             y_ref, htp_ref, o_ref, ht, ext):
        i = pl.program_id(0)

        @pl.when(i == 0)
        def _():
            ht[...] = jnp.zeros_like(ht)

        xc, sact, act, arow, dtpre, dt, cum, cumx, dtx = _ssd_chunk_fwd_common(
            xbc_ref, halo_ref, dtraw_ref, cw_ref, cb_ref, dtb_ref, alog_ref, tri_ref, e16_ref, ext, i == 0)
        cum_t = cum.T
        xs = act[:, 0:SW]
        lastx = cumx[CH - 1:CH, :]
        xh = xs * dtx
        eexp = jnp.exp(cumx)
        dte = jnp.exp(lastx - cumx)
        cdx = jnp.exp(lastx)
        htp = ht[...]
        htp_ref[...] = htp
        xw = (xh * dte).astype(BF16)
        xb = xh.astype(BF16)
        trim = tri_ref[...] > 0.5
        lane = lax.broadcasted_iota(jnp.int32, (CH, 128), 1)
        parts = []
        for g in range(SG):
            gl = slice(g * GW, (g + 1) * GW)
            bg = act[:, SW + g * SN:SW + (g + 1) * SN].astype(BF16)
            cg = act[:, SW + SG * SN + g * SN:SW + SG * SN + (g + 1) * SN].astype(BF16)
            cbm = _nt(cg, bg)
            yoff = eexp[:, gl] * _mm(cg, htp[:, gl].astype(BF16))
            ht[:, gl] = htp[:, gl] * cdx[:, gl] + _tn(bg, xw[:, gl])
            for pr in range(GW // 128):
                h0 = g * (SH // SG) + 2 * pr
                lo = g * GW + pr * 128
                xp = xb[:, lo:lo + 128]
                res = []
                for hh in (h0, h0 + 1):
                    seg = cum[:, hh:hh + 1] - cum_t[hh:hh + 1, :]
                    mh = jnp.where(trim, cbm * jnp.exp(seg), 0.0).astype(BF16)
                    res.append(_mm(mh, xp))
                parts.append(jnp.where(lane < SP, res[0], res[1]) + yoff[:, pr * 128:(pr + 1) * 128])
        y = jnp.concatenate(parts, axis=1) + xs * dsk_ref[...]
        y_ref[...] = y
        z = zs_ref[...]
        hf = y * (z * _sigmoid(z))
        outs = []
        for g in range(SG):
            hg = hf[:, g * GW:(g + 1) * GW]
            rs = lax.rsqrt(jnp.mean(hg * hg, axis=-1, keepdims=True) + RMS_EPS)
            outs.append(hg * rs)
        o_ref[...] = (jnp.concatenate(outs, axis=1) * g_ref[...]).astype(BF16)

    row = lambda n: pl.BlockSpec((CH, n), lambda i: (i, 0))
    return pl.pallas_call(
        body, name="ssd_fwd", grid=(nc,),
        in_specs=[row(CC), pl.BlockSpec((8, CC), lambda i: (jnp.maximum(i * (CH // 8) - 1, 0), 0)), row(128), row(SW),
                  _const((CW, CC)), _const((1, CC)), _const((1, 128)), _const((1, 128)), _const((1, SW)), _const((1, SW)),
                  _const((CH, CH)), _const((128, SW))],
        out_specs=[row(SW), pl.BlockSpec((None, SN, SW), lambda i: (i, 0, 0)), row(SW)],
        out_shape=[jax.ShapeDtypeStruct((S, SW), F32), jax.ShapeDtypeStruct((nc, SN, SW), F32),
                   jax.ShapeDtypeStruct((S, SW), BF16)],
        scratch_shapes=[pltpu.VMEM((SN, SW), F32), pltpu.VMEM((8 + CH, CC), F32)],
        compiler_params=_cp(("arbitrary",)),
    )(xbc, xbc, dtraw, zs, conv_w, conv_b, dtb, alog, dskx, gssm, tri, e16)


def _outproj(o, za, ossm, x, tgt, wout, mod, ln_g, ln_b):
    S = x.shape[0]
    tm = min(TM, S)

    def body(o_ref, za_ref, os_ref, x_ref, t_ref, w_ref, mod_ref, g_ref, b_ref,
             gx_ref, do_ref, dza_ref, dos_ref, delta_ref, dw_ref, vec_ref):
        i = pl.program_id(0)

        @pl.when(i == 0)
        def _():
            dw_ref[...] = jnp.zeros_like(dw_ref)
            vec_ref[...] = jnp.zeros_like(vec_ref)

        gate = mod_ref[0:1, 2 * D:3 * D]
        ov = o_ref[...]
        z = za_ref[...]
        sz = _sigmoid(z)
        silz = z * sz
        a = (ov * silz).astype(BF16)
        osb = os_ref[...]
        mixed = _mm(a, w_ref[0:D, :]) + _mm(osb, w_ref[D:MIX, :])
        xv = x_ref[...]
        hres = ALPHA * xv + gate * mixed
        mu = jnp.mean(hres, axis=-1, keepdims=True)
        hc = hres - mu
        var = jnp.mean(hc * hc, axis=-1, keepdims=True)
        rstd = lax.rsqrt(var + LN_EPS)
        xhat = hc * rstd
        g = g_ref[...]
        yv = xhat * g + b_ref[...]
        err = yv - t_ref[...]
        dy = err * (1.0 / D)
        vec_ref[0:1, :] += jnp.sum(err * err, axis=0, keepdims=True)
        vec_ref[1:2, :] += jnp.sum(dy * xhat, axis=0, keepdims=True)
        vec_ref[2:3, :] += jnp.sum(dy, axis=0, keepdims=True)
        dxh = dy * g
        dh = rstd * (dxh - jnp.mean(dxh, axis=-1, keepdims=True) - xhat * jnp.mean(dxh * xhat, axis=-1, keepdims=True))
        gx_ref[...] = ALPHA * dh
        vec_ref[3:4, :] += jnp.sum(dh * mixed, axis=0, keepdims=True)
        dmixed = (gate * dh).astype(BF16)
        dw_ref[0:D, :] += _tn(a, dmixed)
        dw_ref[D:MIX, :] += _tn(osb, dmixed)
        da = _nt(dmixed, w_ref[0:D, :])
        dos_ref[...] = _nt(dmixed, w_ref[D:MIX, :])
        dov = da * silz
        do_ref[...] = dov.astype(BF16)
        dza_ref[...] = (da * ov * (sz * (1.0 + z * (1.0 - sz)))).astype(BF16)
        pr = dov * ov
        for h in range(NH):
            delta_ref[h] = jnp.sum(pr[:, h * VD:(h + 1) * VD], axis=-1, keepdims=True)

    row = lambda n: pl.BlockSpec((tm, n), lambda i: (i, 0))
    return pl.pallas_call(
        body, name="outproj", grid=(S // tm,),
        in_specs=[row(D), row(D), row(D), row(D), row(D), _const((MIX, D)), _const((8, 3 * D)), _const((1, D)), _const((1, D))],
        out_specs=[row(D), row(D), row(D), row(D), pl.BlockSpec((NH, tm, 1), lambda i: (0, i, 0)),
                   _full((MIX, D)), _full((8, D))],
        out_shape=[jax.ShapeDtypeStruct((S, D), F32), jax.ShapeDtypeStruct((S, D), BF16), jax.ShapeDtypeStruct((S, D), BF16),
                   jax.ShapeDtypeStruct((S, D), F32), jax.ShapeDtypeStruct((NH, S, 1), F32),
                   jax.ShapeDtypeStruct((MIX, D), F32), jax.ShapeDtypeStruct((8, D), F32)],
        compiler_params=_cp(("arbitrary",)),
    )(o, za, ossm, x, tgt, wout, mod, ln_g, ln_b)


def _attn_dq(q, k, v, do, lse, delta):
    _, S, _ = q.shape
    tq = min(TQ, S)
    nq = S // tq

    def body(q_ref, k_ref, v_ref, do_ref, lse_ref, dl_ref, dq_ref):
        i = pl.program_id(1)
        qb = q_ref[...]
        dob = do_ref[...]
        lse_c = lse_ref[...]
        dl_c = dl_ref[...]

        def step(j, dq, masked):
            off = pl.multiple_of(j * tq, tq)
            kb = k_ref[pl.ds(off, tq), :]
            vb = v_ref[pl.ds(off, tq), :]
            p = jnp.exp(_nt(qb, kb) - lse_c)
            if masked:
                r = lax.broadcasted_iota(jnp.int32, p.shape, 0)
                cidx = lax.broadcasted_iota(jnp.int32, p.shape, 1)
                p = jnp.where(r >= cidx, p, 0.0)
            ds = p * (_nt(dob, vb) - dl_c)
            return dq + _mm(ds.astype(BF16), kb)

        dq = lax.fori_loop(0, i, lambda j, cr: step(j, cr, False), jnp.zeros((tq, HP), F32))
        dq_ref[...] = step(i, dq, True)

    return pl.pallas_call(
        body, name="attn_dq", grid=(NH, nq),
        in_specs=[pl.BlockSpec((None, tq, HP), lambda h, i: (h, i, 0)),
                  pl.BlockSpec((None, S, HP), lambda h, i: (h, 0, 0)),
                  pl.BlockSpec((None, S, VD), lambda h, i: (h, 0, 0)),
                  pl.BlockSpec((tq, VD), lambda h, i: (i, h)),
                  pl.BlockSpec((None, tq, 1), lambda h, i: (h, i, 0)),
                  pl.BlockSpec((None, tq, 1), lambda h, i: (h, i, 0))],
        out_specs=pl.BlockSpec((None, tq, HP), lambda h, i: (h, i, 0)),
        out_shape=jax.ShapeDtypeStruct((NH, S, HP), F32),
        compiler_params=_cp(("arbitrary", "arbitrary")),
    )(q, k, v, do, lse, delta)


def _attn_dkv(q, k, v, do, lse_r, delta_r):
    _, S, _ = q.shape
    tq = min(TQ, S)
    nq = S // tq

    def body(k_ref, v_ref, q_ref, do_ref, lse_ref, dl_ref, dk_ref, dv_ref):
        j = pl.program_id(1)
        kb = k_ref[...]
        vb = v_ref[...]

        def step(i, carry, masked):
            dk, dv = carry
            off = pl.multiple_of(i * tq, tq)
            qb = q_ref[pl.ds(off, tq), :]
            dob = do_ref[pl.ds(off, tq), :]
            pt = jnp.exp(_nt(kb, qb) - lse_ref[i])
            if masked:
                r = lax.broadcasted_iota(jnp.int32, pt.shape, 0)
                cidx = lax.broadcasted_iota(jnp.int32, pt.shape, 1)
                pt = jnp.where(cidx >= r, pt, 0.0)
            dv = dv + _mm(pt.astype(BF16), dob)
            dst = pt * (_nt(vb, dob) - dl_ref[i])
            dk = dk + _mm(dst.astype(BF16), qb)
            return dk, dv

        carry = step(j, (jnp.zeros((tq, HP), F32), jnp.zeros((tq, VD), F32)), True)
        dk, dv = lax.fori_loop(j + 1, nq, lambda i, cr: step(i, cr, False), carry)
        dk_ref[...] = dk
        dv_ref[...] = dv

    return pl.pallas_call(
        body, name="attn_dkv", grid=(NH, nq),
        in_specs=[pl.BlockSpec((None, tq, HP), lambda h, j: (h, j, 0)),
                  pl.BlockSpec((None, tq, VD), lambda h, j: (h, j, 0)),
                  pl.BlockSpec((None, S, HP), lambda h, j: (h, 0, 0)),
                  pl.BlockSpec((S, VD), lambda h, j: (0, h)),
                  pl.BlockSpec((None, nq, 1, tq), lambda h, j: (h, 0, 0, 0)),
                  pl.BlockSpec((None, nq, 1, tq), lambda h, j: (h, 0, 0, 0))],
        out_specs=[pl.BlockSpec((None, tq, HP), lambda h, j: (h, j, 0)),
                   pl.BlockSpec((None, tq, VD), lambda h, j: (h, j, 0))],
        out_shape=[jax.ShapeDtypeStruct((NH, S, HP), F32), jax.ShapeDtypeStruct((NH, S, VD), F32)],
        compiler_params=_cp(("arbitrary", "arbitrary")),
    )(k, v, q, do, lse_r, delta_r)


def _ssd_bwd(xbc, dtraw, zs, y, htp, dossm, conv_w, conv_b, dtb, alog, dskx, gssm):
    S = xbc.shape[0]
    nc = S // CH
    tri, triu, e16 = _ssd_consts()

    def body(xbc_ref, halo_ref, dtraw_ref, zs_ref, y_ref, htp_ref, dos_ref,
             cw_ref, cb_ref, dtb_ref, alog_ref, dsk_ref, g_ref, tri_ref, triu_ref, e16_ref,
             dxbc_ref, ddt_ref, dzs_ref, dcw_ref, dcb_ref, dvec_ref, dg_ref,
             dht, ext, dext, dskacc):
        r = pl.program_id(0)
        i = nc - 1 - r

        @pl.when(r == 0)
        def _():
            dht[...] = jnp.zeros_like(dht)
            dext[CH:CH + 8, :] = jnp.zeros((8, CC), F32)
            dskacc[...] = jnp.zeros_like(dskacc)
            dcw_ref[...] = jnp.zeros_like(dcw_ref)
            dcb_ref[...] = jnp.zeros_like(dcb_ref)
            dvec_ref[...] = jnp.zeros_like(dvec_ref)
            dg_ref[...] = jnp.zeros_like(dg_ref)

        xc, sact, act, arow, dtpre, dt, cum, cumx, dtx = _ssd_chunk_fwd_common(
            xbc_ref, halo_ref, dtraw_ref, cw_ref, cb_ref, dtb_ref, alog_ref, tri_ref, e16_ref, ext, i == 0)
        cum_t = cum.T
        xs = act[:, 0:SW]
        lastx = cumx[CH - 1:CH, :]
        xh = xs * dtx
        eexp = jnp.exp(cumx)
        dte = jnp.exp(lastx - cumx)
        cdx = jnp.exp(lastx)
        trim = tri_ref[...] > 0.5
        lane = lax.broadcasted_iota(jnp.int32, (CH, 128), 1)
        rowi = lax.broadcasted_iota(jnp.int32, (CH, 128), 0)

        yv = y_ref[...]
        z = zs_ref[...]
        sz = _sigmoid(z)
        silz = z * sz
        hf = yv * silz
        dn = dos_ref[...] * g_ref[...]
        dhf_parts, nrm_parts = [], []
        for g in range(SG):
            gl = slice(g * GW, (g + 1) * GW)
            hg = hf[:, gl]
            rs = lax.rsqrt(jnp.mean(hg * hg, axis=-1, keepdims=True) + RMS_EPS)
            ng = hg * rs
            dng = dn[:, gl]
            dhf_parts.append(rs * (dng - ng * jnp.mean(dng * ng, axis=-1, keepdims=True)))
            nrm_parts.append(ng)
        nrm = jnp.concatenate(nrm_parts, axis=1)
        dhf = jnp.concatenate(dhf_parts, axis=1)
        dg_ref[...] += jnp.sum(dos_ref[...] * nrm, axis=0, keepdims=True)
        dyv = dhf * silz
        dzs_ref[...] = (dhf * yv * (sz * (1.0 + z * (1.0 - sz)))).astype(BF16)
        dskacc[...] += jnp.sum(dyv * xs, axis=0, keepdims=True)
        dxs_skip = dyv * dsk_ref[...]

        dhtn = dht[...]
        hp = htp_ref[...]
        dlastx = jnp.sum(dhtn * hp, axis=0, keepdims=True) * cdx
        xb = xh.astype(BF16)
        xwf = xh * dte
        dcum = jnp.zeros((CH, 128), F32)
        dcum_t = jnp.zeros((128, CH), F32)
        dxh_parts, dcumx_parts, dlast_parts, db_parts, dc_parts = [], [], [], [], []
        for g in range(SG):
            gl = slice(g * GW, (g + 1) * GW)
            bg = act[:, SW + g * SN:SW + (g + 1) * SN].astype(BF16)
            cg = act[:, SW + SG * SN + g * SN:SW + SG * SN + (g + 1) * SN].astype(BF16)
            hpg = hp[:, gl].astype(BF16)
            dhn = dhtn[:, gl].astype(BF16)
            dyg = dyv[:, gl]
            yoff = eexp[:, gl] * _mm(cg, hpg)
            dz = (dyg * eexp[:, gl]).astype(BF16)
            dcg = _nt(dz, hpg)
            dht[:, gl] = dhtn[:, gl] * cdx[:, gl] + _tn(cg, dz)
            dcumx_g = dyg * yoff
            dbg = _nt(xwf[:, gl].astype(BF16), dhn)
            dxw = _mm(bg, dhn)
            ddte = dxw * xwf[:, gl]
            dcumx_parts.append(dcumx_g - ddte)
            dlast_parts.append(jnp.sum(ddte, axis=0, keepdims=True))
            dxh_g = dxw * dte[:, gl]
            cbm = _nt(cg, bg)
            dcb = jnp.zeros((CH, CH), F32)
            dxp_parts = []
            for pr in range(GW // 128):
                h0 = g * (SH // SG) + 2 * pr
                lo = g * GW + pr * 128
                xp = xb[:, lo:lo + 128]
                dyp = dyv[:, lo:lo + 128]
                dxp = jnp.zeros((CH, 128), F32)
                for idx, hh in enumerate((h0, h0 + 1)):
                    decay = jnp.where(trim, jnp.exp(cum[:, hh:hh + 1] - cum_t[hh:hh + 1, :]), 0.0)
                    mh = cbm * decay
                    keep = (lane < SP) if idx == 0 else (lane >= SP)
                    dym = jnp.where(keep, dyp, 0.0).astype(BF16)
                    dm = _nt(dym, xp)
                    dxp = dxp + _tn(mh.astype(BF16), dym)
                    gm = dm * mh
                    dcum = dcum + jnp.where(lane == hh, jnp.sum(gm, axis=1, keepdims=True), 0.0)
                    dcum_t = dcum_t - jnp.where(rowi == hh, jnp.sum(gm, axis=0, keepdims=True), 0.0)
                    dcb = dcb + dm * decay
                dxp_parts.append(dxp)
            dxh_parts.append(dxh_g + jnp.concatenate(dxp_parts, axis=1))
            dcbb = dcb.astype(BF16)
            dc_parts.append(dcg + _mm(dcbb, bg))
            db_parts.append(dbg + _tn(dcbb, cg))
        dxh = jnp.concatenate(dxh_parts, axis=1)
        dcumx = jnp.concatenate(dcumx_parts, axis=1)
        dlastx = dlastx + jnp.concatenate(dlast_parts, axis=1)
        e16 = e16_ref[...]
        dlast128 = _nt_hi(jnp.broadcast_to(dlastx, (8, SW)), e16)[0:1, :]
        dcum = dcum + dcum_t.T + _nt_hi(dcumx, e16) + jnp.where(rowi == CH - 1, dlast128, 0.0)
        da = _mm_hi(triu_ref[...], dcum)
        ddt = da * arow + _nt_hi(dxh * xs, e16)
        dvec_ref[1:2, :] += jnp.sum(da * dt, axis=0, keepdims=True)
        ddtraw = jnp.where(lane < SH, ddt * _sigmoid(dtpre), 0.0)
        dvec_ref[0:1, :] += jnp.sum(ddtraw, axis=0, keepdims=True)
        ddt_ref[...] = ddtraw.astype(BF16)
        dxs = dxs_skip + dxh * dtx
        dact = jnp.concatenate([dxs] + db_parts + dc_parts, axis=1)
        dxc = dact * (sact * (1.0 + xc * (1.0 - sact)))

        dcb_ref[...] += jnp.sum(dxc, axis=0, keepdims=True)
        for kk in range(CW):
            dcw_ref[kk:kk + 1, :] += jnp.sum(dxc * ext[5 + kk:5 + kk + CH, :], axis=0, keepdims=True)
        dext[0:CH, :] = dxc
        cw = cw_ref[...]
        dxr = cw[CW - 1:CW, :] * dxc
        for kk in range(CW - 1):
            dxr = dxr + cw[kk:kk + 1, :] * dext[CW - 1 - kk:CW - 1 - kk + CH, :]
        dxbc_ref[...] = dxr.astype(BF16)
        dext[CH:CH + 8, :] = dxc[0:8, :]

        @pl.when(r == nc - 1)
        def _():
            dvec_ref[1:2, :] = dvec_ref[1:2, :] * arow
            dvec_ref[2:3, :] = _nt_hi(jnp.broadcast_to(dskacc[...], (8, SW)), e16)[0:1, :]

    rev = lambda n: pl.BlockSpec((CH, n), lambda r: (nc - 1 - r, 0))
    return pl.pallas_call(
        body, name="ssd_bwd", grid=(nc,),
        in_specs=[rev(CC), pl.BlockSpec((8, CC), lambda r: (jnp.maximum((nc - 1 - r) * (CH // 8) - 1, 0), 0)),
                  rev(128), rev(SW), rev(SW), pl.BlockSpec((None, SN, SW), lambda r: (nc - 1 - r, 0, 0)), rev(SW),
                  _const((CW, CC)), _const((1, CC)), _const((1, 128)), _const((1, 128)), _const((1, SW)), _const((1, SW)),
                  _const((CH, CH)), _const((CH, CH)), _const((128, SW))],
        out_specs=[rev(CC), rev(128), rev(SW), _full((CW, CC)), _full((1, CC)), _full((8, 128)), _full((1, SW))],
        out_shape=[jax.ShapeDtypeStruct((S, CC), BF16), jax.ShapeDtypeStruct((S, 128), BF16), jax.ShapeDtypeStruct((S, SW), BF16),
                   jax.ShapeDtypeStruct((CW, CC), F32), jax.ShapeDtypeStruct((1, CC), F32),
                   jax.ShapeDtypeStruct((8, 128), F32), jax.ShapeDtypeStruct((1, SW), F32)],
        scratch_shapes=[pltpu.VMEM((SN, SW), F32), pltpu.VMEM((8 + CH, CC), F32), pltpu.VMEM((CH + 8, CC), F32),
                        pltpu.VMEM((1, SW), F32)],
        compiler_params=_cp(("arbitrary",)),
    )(xbc, xbc, dtraw, zs, y, htp, dossm, conv_w, conv_b, dtb, alog, dskx, gssm, tri, triu, e16)


def _mla_bwd(dq, dk, dv, qlat, ckv, qg, kvg, wq, wk, wv, pos, invf):
    S = qlat.shape[0]
    tm = min(TM, S)

    def body(dq_ref, dk_ref, dv_ref, ql_ref, ckv_ref, qg_ref, kvg_ref, wq_ref, wk_ref, wv_ref, pos_ref, invf_ref,
             dql_ref, dckv_ref, dkr_ref, dwq_ref, dwk_ref, dwv_ref, dqg_ref, dkvg_ref):
        i = pl.program_id(0)

        @pl.when(i == 0)
        def _():
            dwq_ref[...] = jnp.zeros_like(dwq_ref)
            dwk_ref[...] = jnp.zeros_like(dwk_ref)
            dwv_ref[...] = jnp.zeros_like(dwv_ref)
            dqg_ref[...] = jnp.zeros_like(dqg_ref)
            dkvg_ref[...] = jnp.zeros_like(dkvg_ref)

        ang = pos_ref[...].astype(F32) * invf_ref[...]
        cs = jnp.cos(ang)
        sn = jnp.sin(ang)

        def rms_bwd(v, g, dn, dg_ref):
            r = lax.rsqrt(jnp.mean(v * v, axis=-1, keepdims=True) + RMS_EPS)
            vh = v * r
            dg_ref[...] += jnp.sum(dn * vh, axis=0, keepdims=True)
            dvh = dn * g
            return vh, r * (dvh - vh * jnp.mean(dvh * vh, axis=-1, keepdims=True))

        pieces = []
        for h in range(NH):
            pieces.append(dq_ref[h, :, 0:NOPE] * SCALE)
            pieces.append(_rope_t(dq_ref[h, :, NOPE:HP], cs, sn) * SCALE)
        dqf = jnp.concatenate(pieces, axis=1).astype(BF16)
        ql = ql_ref[...]
        g = qg_ref[...]
        dqn = _nt(dqf, wq_ref[...])
        qh, dql = rms_bwd(ql, g, dqn, dqg_ref)
        dwq_ref[...] += _tn((qh * g).astype(BF16), dqf)
        dql_ref[...] = dql.astype(BF16)

        dkn_p = jnp.concatenate([dk_ref[h, :, 0:NOPE] for h in range(NH)], axis=1).astype(BF16)
        dvf = jnp.concatenate([dv_ref[h] for h in range(NH)], axis=1).astype(BF16)
        dkr = dk_ref[0, :, NOPE:HP]
        for h in range(1, NH):
            dkr = dkr + dk_ref[h, :, NOPE:HP]
        lane = lax.broadcasted_iota(jnp.int32, dkr.shape, 1)
        dkr_ref[...] = jnp.where(lane < ROPE, _rope_t(dkr, cs, sn), 0.0).astype(BF16)
        cv = ckv_ref[...]
        gk = kvg_ref[...]
        dkn = _nt(dkn_p, wk_ref[...]) + _nt(dvf, wv_ref[...])
        kh, dckv = rms_bwd(cv, gk, dkn, dkvg_ref)
        knb = (kh * gk).astype(BF16)
        dwk_ref[...] += _tn(knb, dkn_p)
        dwv_ref[...] += _tn(knb, dvf)
        dckv_ref[...] = dckv.astype(BF16)

    row = lambda n: pl.BlockSpec((tm, n), lambda i: (i, 0))
    heads = lambda n: pl.BlockSpec((NH, tm, n), lambda i: (0, i, 0))
    return pl.pallas_call(
        body, name="mla_bwd", grid=(S // tm,),
        in_specs=[heads(HP), heads(HP), heads(VD), row(QL), row(KVL), _const((1, QL)), _const((1, KVL)),
                  _const((QL, NH * HP)), _const((KVL, NH * NOPE)), _const((KVL, NH * VD)), row(1), _const((1, 128))],
        out_specs=[row(QL), row(KVL), row(128), _full((QL, NH * HP)), _full((KVL, NH * NOPE)), _full((KVL, NH * VD)),
                   _full((1, QL)), _full((1, KVL))],
        out_shape=[jax.ShapeDtypeStruct((S, QL), BF16), jax.ShapeDtypeStruct((S, KVL), BF16), jax.ShapeDtypeStruct((S, 128), BF16),
                   jax.ShapeDtypeStruct((QL, NH * HP), F32), jax.ShapeDtypeStruct((KVL, NH * NOPE), F32),
                   jax.ShapeDtypeStruct((KVL, NH * VD), F32), jax.ShapeDtypeStruct((1, QL), F32), jax.ShapeDtypeStruct((1, KVL), F32)],
        compiler_params=_cp(("arbitrary",)),
    )(dq, dk, dv, qlat, ckv, qg, kvg, wq, wk, wv, pos, invf)


def _inproj_bwd(x, gx1, mod, win, dql, dckv, dza, dxbc, dzs, dkr, ddt):
    S = x.shape[0]
    tm = min(TM, S)

    def body(x_ref, gx1_ref, mod_ref, win_ref, dql_ref, dckv_ref, dza_ref, dxbc_ref, dzs_ref, dkr_ref, ddt_ref,
             gx_ref, dw_ref, vec_ref):
        i = pl.program_id(0)

        @pl.when(i == 0)
        def _():
            dw_ref[...] = jnp.zeros_like(dw_ref)
            vec_ref[...] = jnp.zeros_like(vec_ref)

        shift = mod_ref[0:1, 0:D]
        scale = mod_ref[0:1, D:2 * D]
        xv = x_ref[...]
        ut = (xv * (1.0 + scale) + shift).T.astype(BF16)
        pieces = (dql_ref, dckv_ref, dza_ref, dxbc_ref, dzs_ref, dkr_ref, ddt_ref)
        du = jnp.zeros((tm, D), F32)
        lo = 0
        for p_ref in pieces:
            n = p_ref.shape[1]
            dp = p_ref[...]
            du = du + _nt(dp, win_ref[:, lo:lo + n])
            dw_ref[:, lo:lo + n] += _mm(ut, dp)
            lo += n
        vec_ref[0:1, :] += jnp.sum(du, axis=0, keepdims=True)
        vec_ref[1:2, :] += jnp.sum(du * xv, axis=0, keepdims=True)
        gx_ref[...] = gx1_ref[...] + du * (1.0 + scale)

    row = lambda n: pl.BlockSpec((tm, n), lambda i: (i, 0))
    return pl.pallas_call(
        body, name="inproj_bwd", grid=(S // tm,),
        in_specs=[row(D), row(D), _const((8, 3 * D)), _const((D, IN_P)), row(QL), row(KVL), row(D), row(CC), row(D),
                  row(128), row(128)],
        out_specs=[row(D), pl.BlockSpec((D, IN_P), lambda i: (0, 0), pipeline_mode=pl.Buffered(1)), _full((8, D))],
        out_shape=[jax.ShapeDtypeStruct((S, D), F32), jax.ShapeDtypeStruct((D, IN_P), F32), jax.ShapeDtypeStruct((8, D), F32)],
        compiler_params=_cp(("arbitrary",)),
    )(x, gx1, mod, win, dql, dckv, dza, dxbc, dzs, dkr, ddt)


def _ada_bwd(ccol, dmod):
    w = 3 * D // 4

    def body(c_ref, d_ref, o_ref):
        o_ref[...] = c_ref[...] * d_ref[...]

    return pl.pallas_call(
        body, name="ada_bwd", grid=(4,),
        in_specs=[_full((D, 1)), pl.BlockSpec((1, w), lambda k: (0, k))],
        out_specs=pl.BlockSpec((None, D, w), lambda k: (k, 0, 0)),
        out_shape=jax.ShapeDtypeStruct((4, D, w), F32),
        compiler_params=_cp(("arbitrary",)),
    )(ccol, dmod)


def _adamw(parts, w, m, v, rows_tile):
    rows = w.shape[0]

    def body(p_ref, w_ref, m_ref, v_ref, g_ref, d_ref, nm_ref, nv_ref):
        g = p_ref[0].astype(F32)
        for s in range(1, 8):
            g = g + p_ref[s].astype(F32)
        g_ref[...] = g
        nm = B1 * m_ref[...] + (1.0 - B1) * g
        nv = B2 * v_ref[...] + (1.0 - B2) * (g * g)
        nm_ref[...] = nm
        nv_ref[...] = nv
        m_hat = nm / (1.0 - B1 ** STEP)
        v_hat = nv / (1.0 - B2 ** STEP)
        d_ref[...] = -LR * (m_hat / (jnp.sqrt(v_hat) + EPS) + WD * w_ref[...])

    row = pl.BlockSpec((rows_tile, 1024), lambda i: (i, 0))
    sd = jax.ShapeDtypeStruct((rows, 1024), F32)
    return pl.pallas_call(
        body, name="adamw_" + str(rows), grid=(rows // rows_tile,),
        in_specs=[pl.BlockSpec((8, rows_tile, 1024), lambda i: (0, i, 0)), row, row, row],
        out_specs=[row, row, row, row], out_shape=[sd, sd, sd, sd],
        compiler_params=_cp(("arbitrary",)),
    )(parts, w, m, v)


_SMALL = (("b_ada", 3 * D), ("conv_w", CW * CC // 4), ("conv_b", CC), ("ssm_norm_g", SW), ("ln_g", D), ("ln_b", D),
          ("q_norm_g", QL), ("kv_norm_g", KVL), ("dt_bias", SH), ("a_log", SH), ("d_skip", SH))


def _pack_big(ada, win, qb, kvb, out, lead):
    flat = [a.reshape(a.shape[:lead] + (-1, 1024)) for a in (ada, win, qb, kvb, out)]
    used = sum(f.shape[lead] for f in flat)
    pad = jnp.zeros(flat[0].shape[:lead] + (R_BIG - used, 1024), flat[0].dtype)
    return jnp.concatenate(flat + [pad], axis=lead)


def _unpack_big(p):
    out, r = [], 0
    for n in (R_ADA, R_IN, R_QB, R_KVB, R_OUT):
        out.append(p[..., r:r + n, :])
        r += n
    return out


def _pack_small(d, lead):
    flat = [d[name].reshape(d[name].shape[:lead] + (-1,)) for name, _ in _SMALL]
    used = sum(f.shape[lead] for f in flat)
    pad = jnp.zeros(flat[0].shape[:lead] + (R_SMALL * 1024 - used,), F32)
    return jnp.concatenate(flat + [pad], axis=lead).reshape(flat[0].shape[:lead] + (R_SMALL, 1024))


def _unpack_small(p):
    flat = p.reshape(-1)
    out, r = {}, 0
    for name, n in _SMALL:
        out[name] = flat[r:r + n]
        r += n
    return out


def _in_to_padded(w):
    z = lambda n: jnp.zeros((w.shape[0], n), w.dtype)
    return jnp.concatenate([w[:, 0:384], w[:, 384:640], w[:, 704:1728], w[:, 1728:3264], w[:, 3280:4304],
                            w[:, 640:704], z(64), w[:, 3264:3280], z(112)], axis=1)


def _in_from_padded(g):
    return jnp.concatenate([g[:, 0:384], g[:, 384:640], g[:, P_KR[0]:P_KR[0] + 64], g[:, 640:1664], g[:, 1664:3200],
                            g[:, P_DT[0]:P_DT[0] + 16], g[:, 3200:4224]], axis=1)


def kernel(x, c, positions, w_ada, b_ada, w_in, q_norm_g, w_qb, kv_norm_g, w_kvb, conv_w, conv_b, dt_bias, a_log, d_skip, ssm_norm_g, w_out, ln_g, ln_b, loss_target, m_w_ada, m_b_ada, m_w_in, m_q_norm_g, m_w_qb, m_kv_norm_g, m_w_kvb, m_conv_w, m_conv_b, m_dt_bias, m_a_log, m_d_skip, m_ssm_norm_g, m_w_out, m_ln_g, m_ln_b, v_w_ada, v_b_ada, v_w_in, v_q_norm_g, v_w_qb, v_kv_norm_g, v_w_kvb, v_conv_w, v_conv_b, v_dt_bias, v_a_log, v_d_skip, v_ssm_norm_g, v_w_out, v_ln_g, v_ln_b):
    S = x.shape[1]
    xv = x[0]
    tgt = loss_target[0]

    wb = _pack_big(w_ada[0], w_in[0], w_qb[0], w_kvb[0], w_out[0], 0).astype(BF16)
    ws = jnp.zeros((8, 512), F32).at[0:CW, 0:CC // 4].set(conv_w[0])
    gb, gsm = _gather_weights(wb, ws)
    f_ada, f_in, f_qb, f_kvb, f_out = _unpack_big(gb)
    cat1 = lambda f, r, n: jnp.concatenate([f[k].reshape(r, n) for k in range(4)], axis=1)
    wada = cat1(f_ada, D, 3 * D // 4)
    win = _in_to_padded(cat1(f_in, D, IN_W // 4))
    wqb = cat1(f_qb, QL, NH * QKD // 4).reshape(QL, NH, QKD)
    wq = jnp.concatenate([wqb, jnp.zeros((QL, NH, HP - QKD), BF16)], axis=2).reshape(QL, NH * HP)
    wkvb = cat1(f_kvb, KVL, NH * (NOPE + VD) // 4).reshape(KVL, NH, NOPE + VD)
    wk = wkvb[:, :, 0:NOPE].reshape(KVL, NH * NOPE)
    wv = wkvb[:, :, NOPE:].reshape(KVL, NH * VD)
    wout = f_out.reshape(MIX, D)
    cwf = jnp.concatenate([gsm[k, 0:CW, 0:CC // 4] for k in range(4)], axis=1)

    half = ROPE // 2
    invf = 1.0 / (ROPE_THETA ** (jnp.arange(half, dtype=F32) / half))
    invf = jnp.concatenate([invf, invf, jnp.zeros((128 - ROPE,), F32)]).reshape(1, 128)
    pos = positions.reshape(S, 1)
    pad128 = lambda a: jnp.concatenate([a.reshape(1, SH), jnp.zeros((1, 128 - SH), F32)], axis=1)
    dtb, alog = pad128(dt_bias), pad128(a_log)
    dskx = jnp.repeat(d_skip.reshape(SH), SP).reshape(1, SW)

    mod = _ada(jnp.broadcast_to(c, (8, D)), wada, b_ada)
    qlat, ckv, za, xbc, zs, dtraw, q, k, v = _inproj(xv, mod, win, q_norm_g, kv_norm_g, wq, wk, wv, pos, invf)
    o, lse = _attn_fwd(q, k, v)
    y, htp, ossm = _ssd_fwd(xbc, dtraw, zs, cwf, conv_b, dtb, alog, dskx, ssm_norm_g)
    gx1, do, dza, dossm, delta, dwout, vec_o = _outproj(o, za, ossm, xv, tgt, wout, mod, ln_g, ln_b)
    loss = lax.psum(0.5 / D * jnp.sum(vec_o[0]), ("x", "y", "c"))

    tq = min(TQ, S)
    dq = _attn_dq(q, k, v, do, lse, delta)
    dk, dv = _attn_dkv(q, k, v, do, lse.reshape(NH, S // tq, 1, tq), delta.reshape(NH, S // tq, 1, tq))
    dxbc, ddt, dzs, dcw, dcb, dvec, dgssm = _ssd_bwd(xbc, dtraw, zs, y, htp, dossm, cwf, conv_b, dtb, alog, dskx, ssm_norm_g)
    dql, dckv, dkr, dwq, dwk, dwv, dqg, dkvg = _mla_bwd(dq, dk, dv, qlat, ckv, q_norm_g, kv_norm_g, wq, wk, wv, pos, invf)
    gx, dwin, vec_i = _inproj_bwd(xv, gx1, mod, win, dql, dckv, dza, dxbc, dzs, dkr, ddt)
    dmod = jnp.concatenate([vec_i[0:1], vec_i[1:2], vec_o[3:4]], axis=1)
    g_ada = _ada_bwd(c.reshape(D, 1), dmod)

    cols = lambda g: g.reshape(g.shape[0], 4, g.shape[1] // 4).transpose(1, 0, 2)
    g_in = cols(_in_from_padded(dwin))
    g_qb = cols(dwq.reshape(QL, NH, HP)[:, :, 0:QKD].reshape(QL, NH * QKD))
    g_kvb = cols(jnp.concatenate([dwk.reshape(KVL, NH, NOPE), dwv.reshape(KVL, NH, VD)], axis=2).reshape(KVL, NH * (NOPE + VD)))
    g_out = dwout.reshape(4, MIX // 4, D)
    gbig = _pack_big(g_ada, g_in, g_qb, g_kvb, g_out, 1).astype(BF16)
    small = {"b_ada": dmod, "conv_b": dcb, "ssm_norm_g": dgssm, "ln_g": vec_o[1:2], "ln_b": vec_o[2:3],
             "q_norm_g": dqg, "kv_norm_g": dkvg, "dt_bias": dvec[0:1, 0:SH], "a_log": dvec[1:2, 0:SH], "d_skip": dvec[2:3, 0:SH]}
    small = {n: jnp.broadcast_to(a.reshape(1, -1), (4, a.size)) for n, a in small.items()}
    small["conv_w"] = cols(dcw).reshape(4, CW * CC // 4)
    gsmall = _pack_small(small, 1)

    rb, rs = _exchange_grads(gbig, gsmall)
    wbig = _pack_big(w_ada[0], w_in[0], w_qb[0], w_kvb[0], w_out[0], 0)
    mbig = _pack_big(m_w_ada[0], m_w_in[0], m_w_qb[0], m_w_kvb[0], m_w_out[0], 0)
    vbig = _pack_big(v_w_ada[0], v_w_in[0], v_w_qb[0], v_w_kvb[0], v_w_out[0], 0)
    big = _adamw(rb, wbig, mbig, vbig, R_TILE)
    wsm = _pack_small(dict(b_ada=b_ada, conv_w=conv_w, conv_b=conv_b, ssm_norm_g=ssm_norm_g, ln_g=ln_g, ln_b=ln_b,
                           q_norm_g=q_norm_g, kv_norm_g=kv_norm_g, dt_bias=dt_bias, a_log=a_log, d_skip=d_skip), 0)
    msm = _pack_small(dict(b_ada=m_b_ada, conv_w=m_conv_w, conv_b=m_conv_b, ssm_norm_g=m_ssm_norm_g, ln_g=m_ln_g, ln_b=m_ln_b,
                           q_norm_g=m_q_norm_g, kv_norm_g=m_kv_norm_g, dt_bias=m_dt_bias, a_log=m_a_log, d_skip=m_d_skip), 0)
    vsm = _pack_small(dict(b_ada=v_b_ada, conv_w=v_conv_w, conv_b=v_conv_b, ssm_norm_g=v_ssm_norm_g, ln_g=v_ln_g, ln_b=v_ln_b,
                           q_norm_g=v_q_norm_g, kv_norm_g=v_kv_norm_g, dt_bias=v_dt_bias, a_log=v_a_log, d_skip=v_d_skip), 0)
    sm = _adamw(rs, wsm, msm, vsm, R_SMALL)

    order = ["w_ada", "b_ada", "w_in", "q_norm_g", "w_qb", "kv_norm_g", "w_kvb", "conv_w", "conv_b", "dt_bias", "a_log",
             "d_skip", "ssm_norm_g", "w_out", "ln_g", "ln_b"]
    shapes = dict(w_ada=w_ada.shape, b_ada=b_ada.shape, w_in=w_in.shape, q_norm_g=q_norm_g.shape, w_qb=w_qb.shape,
                  kv_norm_g=kv_norm_g.shape, w_kvb=w_kvb.shape, conv_w=conv_w.shape, conv_b=conv_b.shape, dt_bias=dt_bias.shape,
                  a_log=a_log.shape, d_skip=d_skip.shape, ssm_norm_g=ssm_norm_g.shape, w_out=w_out.shape, ln_g=ln_g.shape,
                  ln_b=ln_b.shape)
    outs = []
    for kind in range(4):
        b_ada_, b_in_, b_qb_, b_kvb_, b_out_ = _unpack_big(big[kind])
        d = _unpack_small(sm[kind])
        d.update(w_ada=b_ada_, w_in=b_in_, w_qb=b_qb_, w_kvb=b_kvb_, w_out=b_out_)
        outs.extend(d[n].reshape(shapes[n]) for n in order)
    return (loss, gx.reshape(x.shape), *outs)
```

```python
import functools
import math

import numpy as np
import jax
import jax.numpy as jnp
from jax import lax
from jax.experimental import pallas as pl
from jax.experimental.pallas import tpu as pltpu

F32 = jnp.float32
BF16 = jnp.bfloat16
HIGHEST = lax.Precision.HIGHEST
MESH_ID = pl.DeviceIdType.MESH

D = 1024
NH = 8
NOPE = 128
ROPE = 64
VD = 128
QKD = NOPE + ROPE
HP = 256
QL = 384
KVL = 256
ROPE_THETA = 10000.0
SH = 16
SP = 64
SG = 2
SN = 128
CW = 4
CH = 128
SW = SH * SP
CC = SW + 2 * SG * SN
GW = SW // SG
MIX = 2 * D
IN_W = 4304
ALPHA = 2.0 ** 0.25
RMS_EPS = 1e-6
LN_EPS = 1e-5
SCALE = QKD ** -0.5
LN2 = math.log(2.0)
QSCALE = SCALE / LN2
LR, B1, B2, EPS, WD, STEP = 0.001, 0.9, 0.999, 1e-08, 0.01, 10

P_Q = (0, 384)
P_KV = (384, 640)
P_ZA = (640, 1664)
P_XBC = (1664, 3200)
P_ZS = (3200, 4224)
P_KR = (4224, 4352)
P_DT = (4352, 4480)
IN_P = 4480

R_ADA, R_IN, R_QB, R_KVB, R_OUT = 768, 1076, 144, 128, 512
R_BIG = 2640
R_TILE = 240
R_SMALL = 16

TM = 256
TQ = 512
VMEM_LIMIT = 56 * 1024 * 1024


def _cp(sem=None):
    return pltpu.CompilerParams(dimension_semantics=sem, vmem_limit_bytes=VMEM_LIMIT)


def _mm(a, b):
    return jnp.dot(a, b, preferred_element_type=F32)


def _nt(a, b):
    return lax.dot_general(a, b, (((1,), (1,)), ((), ())), preferred_element_type=F32)


def _tn(a, b):
    return lax.dot_general(a, b, (((0,), (0,)), ((), ())), preferred_element_type=F32)


def _mm_hi(a, b):
    return jnp.dot(a, b, precision=HIGHEST, preferred_element_type=F32)


def _nt_hi(a, b):
    return lax.dot_general(a, b, (((1,), (1,)), ((), ())), precision=HIGHEST, preferred_element_type=F32)


def _sigmoid(z):
    return 1.0 / (1.0 + jnp.exp(-z))


def _softplus(z):
    return jnp.maximum(z, 0.0) + jnp.log1p(jnp.exp(-jnp.abs(z)))


def _rope(t, cs, sn):
    lane = lax.broadcasted_iota(jnp.int32, t.shape, 1)
    rot = jnp.where(lane < ROPE // 2, -pltpu.roll(t, 128 - ROPE // 2, 1), pltpu.roll(t, ROPE // 2, 1))
    return t * cs + rot * sn


def _rope_t(t, cs, sn):
    lane = lax.broadcasted_iota(jnp.int32, t.shape, 1)
    y = t * sn
    rot = jnp.where(lane < ROPE // 2, -pltpu.roll(y, 128 - ROPE // 2, 1), pltpu.roll(y, ROPE // 2, 1))
    return t * cs - rot


def _full(shape):
    n = len(shape)
    return pl.BlockSpec(shape, lambda *_: (0,) * n)


def _const(shape):
    n = len(shape)
    return pl.BlockSpec(shape, lambda *_: (0,) * n, pipeline_mode=pl.Buffered(1))


def _gather_weights(wb, ws):
    def body(wb_ref, ws_ref, ob_ref, os_ref, send_sems, recv_sems, local_sems):
        x, y, c = lax.axis_index("x"), lax.axis_index("y"), lax.axis_index("c")
        me = 2 * x + y
        chips = [(1 - x, y), (x, 1 - y), (1 - x, 1 - y)]
        pairs = ((wb_ref, ob_ref), (ws_ref, os_ref))

        def copy(a, j, slot, to):
            src, dst = pairs[a]
            return pltpu.make_async_remote_copy(
                src_ref=src, dst_ref=dst.at[slot], send_sem=send_sems.at[a, j], recv_sem=recv_sems.at[a, j],
                device_id=to, device_id_type=MESH_ID)

        local = [pltpu.make_async_copy(pairs[a][0], pairs[a][1].at[me], local_sems.at[a]) for a in range(2)]
        for cp in local:
            cp.start()
        sends = [copy(a, j, me, (px, py, c)) for a in range(2) for j, (px, py) in enumerate(chips)]
        for cp in sends:
            cp.start()
        for a in range(2):
            for j, (px, py) in enumerate(chips):
                copy(a, j, 2 * px + py, (x, y, c)).wait_recv()
        for cp in sends:
            cp.wait_send()
        for cp in local:
            cp.wait()

    hbm = pl.BlockSpec(memory_space=pltpu.HBM)
    return pl.pallas_call(
        body, name="gather_weights",
        out_shape=(jax.ShapeDtypeStruct((4,) + wb.shape, wb.dtype), jax.ShapeDtypeStruct((4,) + ws.shape, ws.dtype)),
        in_specs=[hbm, hbm], out_specs=(hbm, hbm),
        scratch_shapes=[pltpu.SemaphoreType.DMA((2, 3)), pltpu.SemaphoreType.DMA((2, 3)), pltpu.SemaphoreType.DMA((2,))],
    )(wb, ws)


def _exchange_grads(gb, gs):
    def body(gb_ref, gs_ref, rb_ref, rs_ref, send_sems, recv_sems, local_sems):
        x, y, c = lax.axis_index("x"), lax.axis_index("y"), lax.axis_index("c")
        chip = 2 * x + y
        sibling = (x, y, 1 - c)
        chips = [(1 - x, y), (x, 1 - y), (1 - x, 1 - y)]
        pairs = ((gb_ref, rb_ref), (gs_ref, rs_ref))

        def slot(px, py, pc):
            return 4 * px + 2 * py + pc

        def copy(a, k, src, s, to):
            return pltpu.make_async_remote_copy(
                src_ref=src, dst_ref=pairs[a][1].at[s], send_sem=send_sems.at[a, k], recv_sem=recv_sems.at[a, k],
                device_id=to, device_id_type=MESH_ID)

        mine = slot(x, y, c)
        local = [pltpu.make_async_copy(pairs[a][0].at[chip], pairs[a][1].at[mine], local_sems.at[a]) for a in range(2)]
        for cp in local:
            cp.start()
        first = []
        for a in range(2):
            g_ref, r_ref = pairs[a]
            first.append(copy(a, 0, g_ref.at[chip], mine, sibling))
            for j, (px, py) in enumerate(chips):
                first.append(copy(a, 1 + j, g_ref.at[2 * px + py], mine, (px, py, c)))
        for cp in first:
            cp.start()
        passed = []
        for a in range(2):
            g_ref, r_ref = pairs[a]
            for j, (px, py) in enumerate(chips):
                s = slot(px, py, c)
                copy(a, 1 + j, g_ref.at[chip], s, (x, y, c)).wait_recv()
                fwd = copy(a, 4 + j, r_ref.at[s], s, sibling)
                fwd.start()
                passed.append(fwd)
        for a in range(2):
            g_ref, r_ref = pairs[a]
            copy(a, 0, g_ref.at[chip], slot(x, y, 1 - c), (x, y, c)).wait_recv()
            for j, (px, py) in enumerate(chips):
                copy(a, 4 + j, g_ref.at[chip], slot(px, py, 1 - c), (x, y, c)).wait_recv()
        for cp in first + passed:
            cp.wait_send()
        for cp in local:
            cp.wait()

    hbm = pl.BlockSpec(memory_space=pltpu.HBM)
    return pl.pallas_call(
        body, name="exchange_grads",
        out_shape=(jax.ShapeDtypeStruct((8,) + gb.shape[1:], gb.dtype), jax.ShapeDtypeStruct((8,) + gs.shape[1:], gs.dtype)),
        in_specs=[hbm, hbm], out_specs=(hbm, hbm),
        scratch_shapes=[pltpu.SemaphoreType.DMA((2, 7)), pltpu.SemaphoreType.DMA((2, 7)), pltpu.SemaphoreType.DMA((2,))],
    )(gb, gs)


def _ada(c8, w_ada, b_ada):
    def body(c_ref, w_ref, b_ref, o_ref):
        o_ref[...] = _mm(c_ref[...].astype(BF16), w_ref[...]) + b_ref[...]

    return pl.pallas_call(body, name="ada", out_shape=jax.ShapeDtypeStruct((8, 3 * D), F32),
                          compiler_params=_cp())(c8, w_ada, b_ada)


def _inproj(x, mod, win, qg, kvg, wq, wk, wv, wkt, wvt, pos, invf):
    S = x.shape[0]
    tm = min(TM, S)

    def body(x_ref, mod_ref, win_ref, qg_ref, kvg_ref, wq_ref, wk_ref, wv_ref, wkt_ref, wvt_ref, pos_ref, invf_ref,
             qlat_ref, ckv_ref, za_ref, xbc_ref, zs_ref, dt_ref, q_ref, k_ref, v_ref, kt_ref, vt_ref):
        shift = mod_ref[0:1, 0:D]
        scale = mod_ref[0:1, D:2 * D]
        u = (x_ref[...] * (1.0 + scale) + shift).astype(BF16)

        def proj(p):
            return _mm(u, win_ref[:, p[0]:p[1]])

        ql = proj(P_Q)
        ckv = proj(P_KV)
        qlat_ref[...] = ql
        ckv_ref[...] = ckv
        za_ref[...] = proj(P_ZA)
        xbc_ref[...] = proj(P_XBC)
        zs_ref[...] = proj(P_ZS)
        dt_ref[...] = proj(P_DT)
        kr = proj(P_KR)

        ang = pos_ref[...].astype(F32) * invf_ref[...]
        cs = jnp.cos(ang)
        sn = jnp.sin(ang)

        rq = lax.rsqrt(jnp.mean(ql * ql, axis=-1, keepdims=True) + RMS_EPS)
        qn = (ql * rq * qg_ref[...]).astype(BF16)
        for h in range(NH):
            qh = _mm(qn, wq_ref[:, h * HP:(h + 1) * HP])
            q_ref[h, :, 0:NOPE] = (qh[:, 0:NOPE] * QSCALE).astype(BF16)
            q_ref[h, :, NOPE:HP] = (_rope(qh[:, NOPE:HP], cs, sn) * QSCALE).astype(BF16)

        rk = lax.rsqrt(jnp.mean(ckv * ckv, axis=-1, keepdims=True) + RMS_EPS)
        kn = (ckv * rk * kvg_ref[...]).astype(BF16)
        knope = _mm(kn, wk_ref[...])
        vall = _mm(kn, wv_ref[...])
        krf = _rope(kr, cs, sn)
        krr = krf.astype(BF16)
        krt = krf.T.astype(BF16)
        for h in range(NH):
            k_ref[h, :, 0:NOPE] = knope[:, h * NOPE:(h + 1) * NOPE].astype(BF16)
            k_ref[h, :, NOPE:HP] = krr
            v_ref[h] = vall[:, h * VD:(h + 1) * VD].astype(BF16)
            kt_ref[h, 0:NOPE, :] = _nt(wkt_ref[h * NOPE:(h + 1) * NOPE, :], kn).astype(BF16)
            kt_ref[h, NOPE:HP, :] = krt
            vt_ref[h] = _nt(wvt_ref[h * VD:(h + 1) * VD, :], kn).astype(BF16)

    row = lambda n: pl.BlockSpec((tm, n), lambda i: (i, 0))
    heads = lambda n: pl.BlockSpec((NH, tm, n), lambda i: (0, i, 0))
    heads_t = lambda n: pl.BlockSpec((NH, None, n, tm), lambda i: (0, i, 0, 0))
    sd = lambda n: jax.ShapeDtypeStruct((S, n), F32)
    hd = lambda n: jax.ShapeDtypeStruct((NH, S, n), BF16)
    ht = lambda n: jax.ShapeDtypeStruct((NH, S // tm, n, tm), BF16)
    return pl.pallas_call(
        body, name="inproj", grid=(S // tm,),
        in_specs=[row(D), _const((8, 3 * D)), _const((D, IN_P)), _const((1, QL)), _const((1, KVL)),
                  _const((QL, NH * HP)), _const((KVL, NH * NOPE)), _const((KVL, NH * VD)),
                  _const((NH * NOPE, KVL)), _const((NH * VD, KVL)), row(1), _const((1, 128))],
        out_specs=[row(QL), row(KVL), row(D), row(CC), row(D), row(128), heads(HP), heads(HP), heads(VD),
                   heads_t(HP), heads_t(VD)],
        out_shape=[sd(QL), sd(KVL), sd(D), sd(CC), sd(D), sd(128), hd(HP), hd(HP), hd(VD), ht(HP), ht(VD)],
        compiler_params=_cp(("arbitrary",)),
    )(x, mod, win, qg, kvg, wq, wk, wv, wkt, wvt, pos, invf)


def _attn_fwd(q, k, vt):
    _, S, _ = q.shape
    tq = min(TQ, S)
    nq = S // tq
    tb = vt.shape[3]
    nsb = tq // tb

    def body(q_ref, k_ref, vt_ref, o_ref, lse_ref):
        i = pl.program_id(1)
        qb = q_ref[...]

        def scores(j):
            off = pl.multiple_of(j * tq, tq)
            return _nt(k_ref[pl.ds(off, tq), :], qb)

        def update(j, s, carry):
            m, l, acc = carry
            m_new = jnp.maximum(m, jnp.max(s, axis=0, keepdims=True))
            a = jnp.exp2(m - m_new)
            p = jnp.exp2(s - m_new)
            l = a * l + jnp.sum(p, axis=0, keepdims=True)
            pb = p.astype(BF16)
            acc = a * acc
            for sb in range(nsb):
                acc = acc + _mm(vt_ref[j * nsb + sb], pb[sb * tb:(sb + 1) * tb, :])
            return m_new, l, acc

        def step(j, carry):
            s, rest = carry[0], carry[1:]
            s_next = scores(j + 1)
            return (s_next,) + update(j, s, rest)

        init = (scores(0), jnp.full((1, tq), -1e30, F32), jnp.zeros((1, tq), F32), jnp.zeros((VD, tq), F32))
        s, m, l, acc = lax.fori_loop(0, i, step, init)
        r = lax.broadcasted_iota(jnp.int32, s.shape, 0)
        cidx = lax.broadcasted_iota(jnp.int32, s.shape, 1)
        m, l, acc = update(i, jnp.where(cidx >= r, s, -1e30), (m, l, acc))
        o_ref[...] = (acc / l).T
        lse_ref[...] = m + jnp.log2(l)

    return pl.pallas_call(
        body, name="attn_fwd", grid=(NH, nq),
        in_specs=[pl.BlockSpec((None, tq, HP), lambda h, i: (h, i, 0)),
                  pl.BlockSpec((None, S, HP), lambda h, i: (h, 0, 0)),
                  pl.BlockSpec((None, S // tb, VD, tb), lambda h, i: (h, 0, 0, 0))],
        out_specs=[pl.BlockSpec((tq, VD), lambda h, i: (i, h)),
                   pl.BlockSpec((None, None, 1, tq), lambda h, i: (h, i, 0, 0))],
        out_shape=[jax.ShapeDtypeStruct((S, NH * VD), F32), jax.ShapeDtypeStruct((NH, nq, 1, tq), F32)],
        compiler_params=_cp(("arbitrary", "arbitrary")),
    )(q, k, vt)


def _ssd_consts():
    tri = np.tril(np.ones((CH, CH), np.float32))
    e16 = np.zeros((128, SW), np.float32)
    for h in range(SH):
        e16[h, h * SP:(h + 1) * SP] = 1.0
    return jnp.asarray(tri), jnp.asarray(tri.T.copy()), jnp.asarray(e16)


def _ssd_chunk_fwd_common(xbc_ref, halo_ref, dtraw_ref, cw_ref, cb_ref, dtb_ref, alog_ref, tri_ref, e16_ref, ext, first):
    ext[0:8, :] = jnp.where(first, 0.0, halo_ref[...])
    ext[8:8 + CH, :] = xbc_ref[...]
    cw = cw_ref[...]
    xc = cb_ref[...] + cw[0:1, :] * ext[5:5 + CH, :]
    for kk in range(1, CW):
        xc = xc + cw[kk:kk + 1, :] * ext[5 + kk:5 + kk + CH, :]
    sact = _sigmoid(xc)
    act = xc * sact
    lane = lax.broadcasted_iota(jnp.int32, (1, 128), 1)
    arow = jnp.where(lane < SH, -jnp.exp(alog_ref[...]), 0.0)
    dtpre = dtraw_ref[...] + dtb_ref[...]
    dt = _softplus(dtpre)
    a = dt * arow
    cum = _mm_hi(tri_ref[...], a)
    cumx = _mm_hi(cum, e16_ref[...])
    dtx = _mm_hi(dt, e16_ref[...])
    return xc, sact, act, arow, dtpre, dt, cum, cumx, dtx


def _ssd_fwd(xbc, dtraw, zs, conv_w, conv_b, dtb, alog, dskx, gssm):
    S = xbc.shape[0]
    nc = S // CH
    tri, _, e16 = _ssd_consts()

    def body(xbc_ref, halo_ref, dtraw_ref, zs_ref, cw_ref, cb_ref, dtb_ref, alog_ref, dsk_ref, g_ref, tri_ref, e16_ref,
             y_ref, htp_ref, o_ref, ht, ext):
        i = pl.program_id(0)

        @pl.when(i == 0)
        def _():
            ht[...] = jnp.zeros_like(ht)

        xc, sact, act, arow, dtpre, dt, cum, cumx, dtx = _ssd_chunk_fwd_common(
            xbc_ref, halo_ref, dtraw_ref, cw_ref, cb_ref, dtb_ref, alog_ref, tri_ref, e16_ref, ext, i == 0)
        cum_t = cum.T
        xs = act[:, 0:SW]
        lastx = cumx[CH - 1:CH, :]
        xh = xs * dtx
        eexp = jnp.exp(cumx)
        dte = jnp.exp(lastx - cumx)
        cdx = jnp.exp(lastx)
        htp = ht[...]
        htp_ref[...] = htp
        xw = (xh * dte).astype(BF16)
        xb = xh.astype(BF16)
        trim = tri_ref[...] > 0.5
        lane = lax.broadcasted_iota(jnp.int32, (CH, 128), 1)
        parts = []
        for g in range(SG):
            gl = slice(g * GW, (g + 1) * GW)
            bg = act[:, SW + g * SN:SW + (g + 1) * SN].astype(BF16)
            cg = act[:, SW + SG * SN + g * SN:SW + SG * SN + (g + 1) * SN].astype(BF16)
            cbm = _nt(cg, bg)
            yoff = eexp[:, gl] * _mm(cg, htp[:, gl].astype(BF16))
            ht[:, gl] = htp[:, gl] * cdx[:, gl] + _tn(bg, xw[:, gl])
            for pr in range(GW // 128):
                h0 = g * (SH // SG) + 2 * pr
                lo = g * GW + pr * 128
                xp = xb[:, lo:lo + 128]
                res = []
                for hh in (h0, h0 + 1):
                    seg = cum[:, hh:hh + 1] - cum_t[hh:hh + 1, :]
                    mh = jnp.where(trim, cbm * jnp.exp(seg), 0.0).astype(BF16)
                    res.append(_mm(mh, xp))
                parts.append(jnp.where(lane < SP, res[0], res[1]) + yoff[:, pr * 128:(pr + 1) * 128])
        y = jnp.concatenate(parts, axis=1) + xs * dsk_ref[...]
        y_ref[...] = y
        z = zs_ref[...]
        hf = y * (z * _sigmoid(z))
        outs = []
        for g in range(SG):
            hg = hf[:, g * GW:(g + 1) * GW]
            rs = lax.rsqrt(jnp.mean(hg * hg, axis=-1, keepdims=True) + RMS_EPS)
            outs.append(hg * rs)
        o_ref[...] = (jnp.concatenate(outs, axis=1) * g_ref[...]).astype(BF16)

    row = lambda n: pl.BlockSpec((CH, n), lambda i: (i, 0))
    return pl.pallas_call(
        body, name="ssd_fwd", grid=(nc,),
        in_specs=[row(CC), pl.BlockSpec((8, CC), lambda i: (jnp.maximum(i * (CH // 8) - 1, 0), 0)), row(128), row(SW),
                  _const((CW, CC)), _const((1, CC)), _const((1, 128)), _const((1, 128)), _const((1, SW)), _const((1, SW)),
                  _const((CH, CH)), _const((128, SW))],
        out_specs=[row(SW), pl.BlockSpec((None, SN, SW), lambda i: (i, 0, 0)), row(SW)],
        out_shape=[jax.ShapeDtypeStruct((S, SW), F32), jax.ShapeDtypeStruct((nc, SN, SW), F32),
                   jax.ShapeDtypeStruct((S, SW), BF16)],
        scratch_shapes=[pltpu.VMEM((SN, SW), F32), pltpu.VMEM((8 + CH, CC), F32)],
        compiler_params=_cp(("arbitrary",)),
    )(xbc, xbc, dtraw, zs, conv_w, conv_b, dtb, alog, dskx, gssm, tri, e16)


def _outproj(o, za, ossm, x, tgt, wout, mod, ln_g, ln_b):
    S = x.shape[0]
    tm = min(TM, S)

    def body(o_ref, za_ref, os_ref, x_ref, t_ref, w_ref, mod_ref, g_ref, b_ref,
             gx_ref, do_ref, dza_ref, dos_ref, delta_ref, dw_ref, vec_ref):
        i = pl.program_id(0)

        @pl.when(i == 0)
        def _():
            dw_ref[...] = jnp.zeros_like(dw_ref)
            vec_ref[...] = jnp.zeros_like(vec_ref)

        gate = mod_ref[0:1, 2 * D:3 * D]
        ov = o_ref[...]
        z = za_ref[...]
        sz = _sigmoid(z)
        silz = z * sz
        a = (ov * silz).astype(BF16)
        osb = os_ref[...]
        mixed = _mm(a, w_ref[0:D, :]) + _mm(osb, w_ref[D:MIX, :])
        xv = x_ref[...]
        hres = ALPHA * xv + gate * mixed
        mu = jnp.mean(hres, axis=-1, keepdims=True)
        hc = hres - mu
        var = jnp.mean(hc * hc, axis=-1, keepdims=True)
        rstd = lax.rsqrt(var + LN_EPS)
        xhat = hc * rstd
        g = g_ref[...]
        yv = xhat * g + b_ref[...]
        err = yv - t_ref[...]
        dy = err * (1.0 / D)
        vec_ref[0:1, :] += jnp.sum(err * err, axis=0, keepdims=True)
        vec_ref[1:2, :] += jnp.sum(dy * xhat, axis=0, keepdims=True)
        vec_ref[2:3, :] += jnp.sum(dy, axis=0, keepdims=True)
        dxh = dy * g
        dh = rstd * (dxh - jnp.mean(dxh, axis=-1, keepdims=True) - xhat * jnp.mean(dxh * xhat, axis=-1, keepdims=True))
        gx_ref[...] = ALPHA * dh
        vec_ref[3:4, :] += jnp.sum(dh * mixed, axis=0, keepdims=True)
        dmixed = (gate * dh).astype(BF16)
        dw_ref[0:D, :] += _tn(a, dmixed)
        dw_ref[D:MIX, :] += _tn(osb, dmixed)
        da = _nt(dmixed, w_ref[0:D, :])
        dos_ref[...] = _nt(dmixed, w_ref[D:MIX, :])
        dov = da * silz
        do_ref[...] = dov.astype(BF16)
        dza_ref[...] = (da * ov * (sz * (1.0 + z * (1.0 - sz)))).astype(BF16)
        pr = dov * ov
        for h in range(NH):
            delta_ref[h] = jnp.sum(pr[:, h * VD:(h + 1) * VD], axis=-1, keepdims=True)

    row = lambda n: pl.BlockSpec((tm, n), lambda i: (i, 0))
    return pl.pallas_call(
        body, name="outproj", grid=(S // tm,),
        in_specs=[row(D), row(D), row(D), row(D), row(D), _const((MIX, D)), _const((8, 3 * D)), _const((1, D)), _const((1, D))],
        out_specs=[row(D), row(D), row(D), row(D), pl.BlockSpec((NH, tm, 1), lambda i: (0, i, 0)),
                   _full((MIX, D)), _full((8, D))],
        out_shape=[jax.ShapeDtypeStruct((S, D), F32), jax.ShapeDtypeStruct((S, D), BF16), jax.ShapeDtypeStruct((S, D), BF16),
                   jax.ShapeDtypeStruct((S, D), F32), jax.ShapeDtypeStruct((NH, S, 1), F32),
                   jax.ShapeDtypeStruct((MIX, D), F32), jax.ShapeDtypeStruct((8, D), F32)],
        compiler_params=_cp(("arbitrary",)),
    )(o, za, ossm, x, tgt, wout, mod, ln_g, ln_b)


def _attn_bwd(q, k, kt, v, do, lse, delta):
    _, S, _ = q.shape
    tq = min(TQ, S)
    nq = S // tq
    tb = kt.shape[3]
    nsb = tq // tb

    def body(k_ref, kt_ref, v_ref, q_ref, do_ref, lse_ref, dl_ref, dk_ref, dv_ref, dqt_ref):
        j = pl.program_id(1)
        kb = k_ref[...]
        vb = v_ref[...]

        @pl.when(j == 0)
        def _():
            dqt_ref[...] = jnp.zeros_like(dqt_ref)

        dk_ref[...] = jnp.zeros_like(dk_ref)
        dv_ref[...] = jnp.zeros_like(dv_ref)

        def step(i, masked):
            off = pl.multiple_of(i * tq, tq)
            qb = q_ref[pl.ds(off, tq), :]
            dob = do_ref[pl.ds(off, tq), :]
            pt = jnp.exp2(_nt(kb, qb) - lse_ref[i])
            if masked:
                r = lax.broadcasted_iota(jnp.int32, pt.shape, 0)
                cidx = lax.broadcasted_iota(jnp.int32, pt.shape, 1)
                pt = jnp.where(cidx >= r, pt, 0.0)
            dv_ref[...] += _mm(pt.astype(BF16), dob)
            dsb = (pt * (_nt(vb, dob) - dl_ref[i])).astype(BF16)
            dk_ref[...] += _mm(dsb, qb)
            acc = dqt_ref[i]
            for sb in range(nsb):
                acc = acc + _mm(kt_ref[sb], dsb[sb * tb:(sb + 1) * tb, :])
            dqt_ref[i] = acc

        step(j, True)

        def loop_body(i, carry):
            step(i, False)
            return carry

        lax.fori_loop(j + 1, nq, loop_body, 0)
        dk_ref[...] = dk_ref[...] * LN2

    return pl.pallas_call(
        body, name="attn_bwd", grid=(NH, nq),
        in_specs=[pl.BlockSpec((None, tq, HP), lambda h, j: (h, j, 0)),
                  pl.BlockSpec((None, nsb, HP, tb), lambda h, j: (h, j, 0, 0)),
                  pl.BlockSpec((None, tq, VD), lambda h, j: (h, j, 0)),
                  pl.BlockSpec((None, S, HP), lambda h, j: (h, 0, 0), pipeline_mode=pl.Buffered(1)),
                  pl.BlockSpec((S, VD), lambda h, j: (0, h), pipeline_mode=pl.Buffered(1)),
                  pl.BlockSpec((None, nq, 1, tq), lambda h, j: (h, 0, 0, 0)),
                  pl.BlockSpec((None, nq, 1, tq), lambda h, j: (h, 0, 0, 0))],
        out_specs=[pl.BlockSpec((None, tq, HP), lambda h, j: (h, j, 0)),
                   pl.BlockSpec((None, tq, VD), lambda h, j: (h, j, 0)),
                   pl.BlockSpec((None, nq, HP, tq), lambda h, j: (h, 0, 0, 0), pipeline_mode=pl.Buffered(1))],
        out_shape=[jax.ShapeDtypeStruct((NH, S, HP), F32), jax.ShapeDtypeStruct((NH, S, VD), F32),
                   jax.ShapeDtypeStruct((NH, nq, HP, tq), F32)],
        compiler_params=_cp(("arbitrary", "arbitrary")),
    )(k, kt, v, q, do, lse, delta)


def _ssd_bwd(xbc, dtraw, zs, y, htp, dossm, conv_w, conv_b, dtb, alog, dskx, gssm):
    S = xbc.shape[0]
    nc = S // CH
    tri, triu, e16 = _ssd_consts()

    def body(xbc_ref, halo_ref, dtraw_ref, zs_ref, y_ref, htp_ref, dos_ref,
             cw_ref, cb_ref, dtb_ref, alog_ref, dsk_ref, g_ref, tri_ref, triu_ref, e16_ref,
             dxbc_ref, ddt_ref, dzs_ref, dcw_ref, dcb_ref, dvec_ref, dg_ref,
             dht, ext, dext, dskacc):
        r = pl.program_id(0)
        i = nc - 1 - r

        @pl.when(r == 0)
        def _():
            dht[...] = jnp.zeros_like(dht)
            dext[CH:CH + 8, :] = jnp.zeros((8, CC), F32)
            dskacc[...] = jnp.zeros_like(dskacc)
            dcw_ref[...] = jnp.zeros_like(dcw_ref)
            dcb_ref[...] = jnp.zeros_like(dcb_ref)
            dvec_ref[...] = jnp.zeros_like(dvec_ref)
            dg_ref[...] = jnp.zeros_like(dg_ref)

        xc, sact, act, arow, dtpre, dt, cum, cumx, dtx = _ssd_chunk_fwd_common(
            xbc_ref, halo_ref, dtraw_ref, cw_ref, cb_ref, dtb_ref, alog_ref, tri_ref, e16_ref, ext, i == 0)
        cum_t = cum.T
        xs = act[:, 0:SW]
        lastx = cumx[CH - 1:CH, :]
        xh = xs * dtx
        eexp = jnp.exp(cumx)
        dte = jnp.exp(lastx - cumx)
        cdx = jnp.exp(lastx)
        trim = tri_ref[...] > 0.5
        lane = lax.broadcasted_iota(jnp.int32, (CH, 128), 1)
        rowi = lax.broadcasted_iota(jnp.int32, (CH, 128), 0)

        yv = y_ref[...]
        z = zs_ref[...]
        sz = _sigmoid(z)
        silz = z * sz
        hf = yv * silz
        dn = dos_ref[...] * g_ref[...]
        dhf_parts, nrm_parts = [], []
        for g in range(SG):
            gl = slice(g * GW, (g + 1) * GW)
            hg = hf[:, gl]
            rs = lax.rsqrt(jnp.mean(hg * hg, axis=-1, keepdims=True) + RMS_EPS)
            ng = hg * rs
            dng = dn[:, gl]
            dhf_parts.append(rs * (dng - ng * jnp.mean(dng * ng, axis=-1, keepdims=True)))
            nrm_parts.append(ng)
        nrm = jnp.concatenate(nrm_parts, axis=1)
        dhf = jnp.concatenate(dhf_parts, axis=1)
        dg_ref[...] += jnp.sum(dos_ref[...] * nrm, axis=0, keepdims=True)
        dyv = dhf * silz
        dzs_ref[...] = (dhf * yv * (sz * (1.0 + z * (1.0 - sz)))).astype(BF16)
        dskacc[...] += jnp.sum(dyv * xs, axis=0, keepdims=True)
        dxs_skip = dyv * dsk_ref[...]

        dhtn = dht[...]
        hp = htp_ref[...]
        dlastx = jnp.sum(dhtn * hp, axis=0, keepdims=True) * cdx
        xb = xh.astype(BF16)
        xwf = xh * dte
        dcum = jnp.zeros((CH, 128), F32)
        dcum_t = jnp.zeros((128, CH), F32)
        dxh_parts, dcumx_parts, dlast_parts, db_parts, dc_parts = [], [], [], [], []
        for g in range(SG):
            gl = slice(g * GW, (g + 1) * GW)
            bg = act[:, SW + g * SN:SW + (g + 1) * SN].astype(BF16)
            cg = act[:, SW + SG * SN + g * SN:SW + SG * SN + (g + 1) * SN].astype(BF16)
            hpg = hp[:, gl].astype(BF16)
            dhn = dhtn[:, gl].astype(BF16)
            dyg = dyv[:, gl]
            yoff = eexp[:, gl] * _mm(cg, hpg)
            dz = (dyg * eexp[:, gl]).astype(BF16)
            dcg = _nt(dz, hpg)
            dht[:, gl] = dhtn[:, gl] * cdx[:, gl] + _tn(cg, dz)
            dcumx_g = dyg * yoff
            dbg = _nt(xwf[:, gl].astype(BF16), dhn)
            dxw = _mm(bg, dhn)
            ddte = dxw * xwf[:, gl]
            dcumx_parts.append(dcumx_g - ddte)
            dlast_parts.append(jnp.sum(ddte, axis=0, keepdims=True))
            dxh_g = dxw * dte[:, gl]
            cbm = _nt(cg, bg)
            dcb = jnp.zeros((CH, CH), F32)
            dxp_parts = []
            for pr in range(GW // 128):
                h0 = g * (SH // SG) + 2 * pr
                lo = g * GW + pr * 128
                xp = xb[:, lo:lo + 128]
                dyp = dyv[:, lo:lo + 128]
                dxp = jnp.zeros((CH, 128), F32)
                for idx, hh in enumerate((h0, h0 + 1)):
                    decay = jnp.where(trim, jnp.exp(cum[:, hh:hh + 1] - cum_t[hh:hh + 1, :]), 0.0)
                    mh = cbm * decay
                    keep = (lane < SP) if idx == 0 else (lane >= SP)
                    dym = jnp.where(keep, dyp, 0.0).astype(BF16)
                    dm = _nt(dym, xp)
                    dxp = dxp + _tn(mh.astype(BF16), dym)
                    gm = dm * mh
                    dcum = dcum + jnp.where(lane == hh, jnp.sum(gm, axis=1, keepdims=True), 0.0)
                    dcum_t = dcum_t - jnp.where(rowi == hh, jnp.sum(gm, axis=0, keepdims=True), 0.0)
                    dcb = dcb + dm * decay
                dxp_parts.append(dxp)
            dxh_parts.append(dxh_g + jnp.concatenate(dxp_parts, axis=1))
            dcbb = dcb.astype(BF16)
            dc_parts.append(dcg + _mm(dcbb, bg))
            db_parts.append(dbg + _tn(dcbb, cg))
        dxh = jnp.concatenate(dxh_parts, axis=1)
        dcumx = jnp.concatenate(dcumx_parts, axis=1)
        dlastx = dlastx + jnp.concatenate(dlast_parts, axis=1)
        e16 = e16_ref[...]
        dlast128 = _nt_hi(jnp.broadcast_to(dlastx, (8, SW)), e16)[0:1, :]
        dcum = dcum + dcum_t.T + _nt_hi(dcumx, e16) + jnp.where(rowi == CH - 1, dlast128, 0.0)
        da = _mm_hi(triu_ref[...], dcum)
        ddt = da * arow + _nt_hi(dxh * xs, e16)
        dvec_ref[1:2, :] += jnp.sum(da * dt, axis=0, keepdims=True)
        ddtraw = jnp.where(lane < SH, ddt * _sigmoid(dtpre), 0.0)
        dvec_ref[0:1, :] += jnp.sum(ddtraw, axis=0, keepdims=True)
        ddt_ref[...] = ddtraw.astype(BF16)
        dxs = dxs_skip + dxh * dtx
        dact = jnp.concatenate([dxs] + db_parts + dc_parts, axis=1)
        dxc = dact * (sact * (1.0 + xc * (1.0 - sact)))

        dcb_ref[...] += jnp.sum(dxc, axis=0, keepdims=True)
        for kk in range(CW):
            dcw_ref[kk:kk + 1, :] += jnp.sum(dxc * ext[5 + kk:5 + kk + CH, :], axis=0, keepdims=True)
        dext[0:CH, :] = dxc
        cw = cw_ref[...]
        dxr = cw[CW - 1:CW, :] * dxc
        for kk in range(CW - 1):
            dxr = dxr + cw[kk:kk + 1, :] * dext[CW - 1 - kk:CW - 1 - kk + CH, :]
        dxbc_ref[...] = dxr.astype(BF16)
        dext[CH:CH + 8, :] = dxc[0:8, :]

        @pl.when(r == nc - 1)
        def _():
            dvec_ref[1:2, :] = dvec_ref[1:2, :] * arow
            dvec_ref[2:3, :] = _nt_hi(jnp.broadcast_to(dskacc[...], (8, SW)), e16)[0:1, :]

    rev = lambda n: pl.BlockSpec((CH, n), lambda r: (nc - 1 - r, 0))
    return pl.pallas_call(
        body, name="ssd_bwd", grid=(nc,),
        in_specs=[rev(CC), pl.BlockSpec((8, CC), lambda r: (jnp.maximum((nc - 1 - r) * (CH // 8) - 1, 0), 0)),
                  rev(128), rev(SW), rev(SW), pl.BlockSpec((None, SN, SW), lambda r: (nc - 1 - r, 0, 0)), rev(SW),
                  _const((CW, CC)), _const((1, CC)), _const((1, 128)), _const((1, 128)), _const((1, SW)), _const((1, SW)),
                  _const((CH, CH)), _const((CH, CH)), _const((128, SW))],
        out_specs=[rev(CC), rev(128), rev(SW), _full((CW, CC)), _full((1, CC)), _full((8, 128)), _full((1, SW))],
        out_shape=[jax.ShapeDtypeStruct((S, CC), BF16), jax.ShapeDtypeStruct((S, 128), BF16), jax.ShapeDtypeStruct((S, SW), BF16),
                   jax.ShapeDtypeStruct((CW, CC), F32), jax.ShapeDtypeStruct((1, CC), F32),
                   jax.ShapeDtypeStruct((8, 128), F32), jax.ShapeDtypeStruct((1, SW), F32)],
        scratch_shapes=[pltpu.VMEM((SN, SW), F32), pltpu.VMEM((8 + CH, CC), F32), pltpu.VMEM((CH + 8, CC), F32),
                        pltpu.VMEM((1, SW), F32)],
        compiler_params=_cp(("arbitrary",)),
    )(xbc, xbc, dtraw, zs, y, htp, dossm, conv_w, conv_b, dtb, alog, dskx, gssm, tri, triu, e16)


def _mla_bwd(dqt, dk, dv, qlat, ckv, qg, kvg, wq, wk, wv, pos, invf):
    S = qlat.shape[0]
    tm = dqt.shape[3]

    def body(dq_ref, dk_ref, dv_ref, ql_ref, ckv_ref, qg_ref, kvg_ref, wq_ref, wk_ref, wv_ref, pos_ref, invf_ref,
             dql_ref, dckv_ref, dkr_ref, dwq_ref, dwk_ref, dwv_ref, dqg_ref, dkvg_ref):
        i = pl.program_id(0)

        @pl.when(i == 0)
        def _():
            dwq_ref[...] = jnp.zeros_like(dwq_ref)
            dwk_ref[...] = jnp.zeros_like(dwk_ref)
            dwv_ref[...] = jnp.zeros_like(dwv_ref)
            dqg_ref[...] = jnp.zeros_like(dqg_ref)
            dkvg_ref[...] = jnp.zeros_like(dkvg_ref)

        ang = pos_ref[...].astype(F32) * invf_ref[...]
        cs = jnp.cos(ang)
        sn = jnp.sin(ang)

        def rms_bwd(v, g, dn, dg_ref):
            r = lax.rsqrt(jnp.mean(v * v, axis=-1, keepdims=True) + RMS_EPS)
            vh = v * r
            dg_ref[...] += jnp.sum(dn * vh, axis=0, keepdims=True)
            dvh = dn * g
            return vh, r * (dvh - vh * jnp.mean(dvh * vh, axis=-1, keepdims=True))

        pieces = []
        for h in range(NH):
            dqh = dq_ref[h].T
            pieces.append(dqh[:, 0:NOPE] * SCALE)
            pieces.append(_rope_t(dqh[:, NOPE:HP], cs, sn) * SCALE)
        dqf = jnp.concatenate(pieces, axis=1).astype(BF16)
        ql = ql_ref[...]
        g = qg_ref[...]
        dqn = _nt(dqf, wq_ref[...])
        qh, dql = rms_bwd(ql, g, dqn, dqg_ref)
        dwq_ref[...] += _tn((qh * g).astype(BF16), dqf)
        dql_ref[...] = dql.astype(BF16)

        dkn_p = jnp.concatenate([dk_ref[h, :, 0:NOPE] for h in range(NH)], axis=1).astype(BF16)
        dvf = jnp.concatenate([dv_ref[h] for h in range(NH)], axis=1).astype(BF16)
        dkr = dk_ref[0, :, NOPE:HP]
        for h in range(1, NH):
            dkr = dkr + dk_ref[h, :, NOPE:HP]
        lane = lax.broadcasted_iota(jnp.int32, dkr.shape, 1)
        dkr_ref[...] = jnp.where(lane < ROPE, _rope_t(dkr, cs, sn), 0.0).astype(BF16)
        cv = ckv_ref[...]
        gk = kvg_ref[...]
        dkn = _nt(dkn_p, wk_ref[...]) + _nt(dvf, wv_ref[...])
        kh, dckv = rms_bwd(cv, gk, dkn, dkvg_ref)
        knb = (kh * gk).astype(BF16)
        dwk_ref[...] += _tn(knb, dkn_p)
        dwv_ref[...] += _tn(knb, dvf)
        dckv_ref[...] = dckv.astype(BF16)

    row = lambda n: pl.BlockSpec((tm, n), lambda i: (i, 0))
    heads = lambda n: pl.BlockSpec((NH, tm, n), lambda i: (0, i, 0))
    return pl.pallas_call(
        body, name="mla_bwd", grid=(S // tm,),
        in_specs=[pl.BlockSpec((NH, None, HP, tm), lambda i: (0, i, 0, 0)), heads(HP), heads(VD), row(QL), row(KVL), _const((1, QL)), _const((1, KVL)),
                  _const((QL, NH * HP)), _const((KVL, NH * NOPE)), _const((KVL, NH * VD)), row(1), _const((1, 128))],
        out_specs=[row(QL), row(KVL), row(128), _full((QL, NH * HP)), _full((KVL, NH * NOPE)), _full((KVL, NH * VD)),
                   _full((1, QL)), _full((1, KVL))],
        out_shape=[jax.ShapeDtypeStruct((S, QL), BF16), jax.ShapeDtypeStruct((S, KVL), BF16), jax.ShapeDtypeStruct((S, 128), BF16),
                   jax.ShapeDtypeStruct((QL, NH * HP), F32), jax.ShapeDtypeStruct((KVL, NH * NOPE), F32),
                   jax.ShapeDtypeStruct((KVL, NH * VD), F32), jax.ShapeDtypeStruct((1, QL), F32), jax.ShapeDtypeStruct((1, KVL), F32)],
        compiler_params=_cp(("arbitrary",)),
    )(dqt, dk, dv, qlat, ckv, qg, kvg, wq, wk, wv, pos, invf)


def _inproj_bwd(x, gx1, mod, win, dql, dckv, dza, dxbc, dzs, dkr, ddt):
    S = x.shape[0]
    tm = min(TM, S)

    def body(x_ref, gx1_ref, mod_ref, win_ref, dql_ref, dckv_ref, dza_ref, dxbc_ref, dzs_ref, dkr_ref, ddt_ref,
             gx_ref, dw_ref, vec_ref):
        i = pl.program_id(0)

        @pl.when(i == 0)
        def _():
            dw_ref[...] = jnp.zeros_like(dw_ref)
            vec_ref[...] = jnp.zeros_like(vec_ref)

        shift = mod_ref[0:1, 0:D]
        scale = mod_ref[0:1, D:2 * D]
        xv = x_ref[...]
        ut = (xv * (1.0 + scale) + shift).T.astype(BF16)
        pieces = (dql_ref, dckv_ref, dza_ref, dxbc_ref, dzs_ref, dkr_ref, ddt_ref)
        du = jnp.zeros((tm, D), F32)
        lo = 0
        for p_ref in pieces:
            n = p_ref.shape[1]
            dp = p_ref[...]
            du = du + _nt(dp, win_ref[:, lo:lo + n])
            dw_ref[:, lo:lo + n] += _mm(ut, dp)
            lo += n
        vec_ref[0:1, :] += jnp.sum(du, axis=0, keepdims=True)
        vec_ref[1:2, :] += jnp.sum(du * xv, axis=0, keepdims=True)
        gx_ref[...] = gx1_ref[...] + du * (1.0 + scale)

    row = lambda n: pl.BlockSpec((tm, n), lambda i: (i, 0))
    return pl.pallas_call(
        body, name="inproj_bwd", grid=(S // tm,),
        in_specs=[row(D), row(D), _const((8, 3 * D)), _const((D, IN_P)), row(QL), row(KVL), row(D), row(CC), row(D),
                  row(128), row(128)],
        out_specs=[row(D), pl.BlockSpec((D, IN_P), lambda i: (0, 0), pipeline_mode=pl.Buffered(1)), _full((8, D))],
        out_shape=[jax.ShapeDtypeStruct((S, D), F32), jax.ShapeDtypeStruct((D, IN_P), F32), jax.ShapeDtypeStruct((8, D), F32)],
        compiler_params=_cp(("arbitrary",)),
    )(x, gx1, mod, win, dql, dckv, dza, dxbc, dzs, dkr, ddt)


def _ada_bwd(ccol, dmod):
    w = 3 * D // 4

    def body(c_ref, d_ref, o_ref):
        o_ref[...] = c_ref[...] * d_ref[...]

    return pl.pallas_call(
        body, name="ada_bwd", grid=(4,),
        in_specs=[_full((D, 1)), pl.BlockSpec((1, w), lambda k: (0, k))],
        out_specs=pl.BlockSpec((None, D, w), lambda k: (k, 0, 0)),
        out_shape=jax.ShapeDtypeStruct((4, D, w), F32),
        compiler_params=_cp(("arbitrary",)),
    )(ccol, dmod)


def _adamw(parts, w, m, v, rows_tile):
    rows = w.shape[0]

    def body(p_ref, w_ref, m_ref, v_ref, g_ref, d_ref, nm_ref, nv_ref):
        g = p_ref[0].astype(F32)
        for s in range(1, 8):
            g = g + p_ref[s].astype(F32)
        g_ref[...] = g
        nm = B1 * m_ref[...] + (1.0 - B1) * g
        nv = B2 * v_ref[...] + (1.0 - B2) * (g * g)
        nm_ref[...] = nm
        nv_ref[...] = nv
        m_hat = nm / (1.0 - B1 ** STEP)
        v_hat = nv / (1.0 - B2 ** STEP)
        d_ref[...] = -LR * (m_hat / (jnp.sqrt(v_hat) + EPS) + WD * w_ref[...])

    row = pl.BlockSpec((rows_tile, 1024), lambda i: (i, 0))
    sd = jax.ShapeDtypeStruct((rows, 1024), F32)
    return pl.pallas_call(
        body, name="adamw_" + str(rows), grid=(rows // rows_tile,),
        in_specs=[pl.BlockSpec((8, rows_tile, 1024), lambda i: (0, i, 0)), row, row, row],
        out_specs=[row, row, row, row], out_shape=[sd, sd, sd, sd],
        compiler_params=_cp(("arbitrary",)),
    )(parts, w, m, v)


_SMALL = (("b_ada", 3 * D), ("conv_w", CW * CC // 4), ("conv_b", CC), ("ssm_norm_g", SW), ("ln_g", D), ("ln_b", D),
          ("q_norm_g", QL), ("kv_norm_g", KVL), ("dt_bias", SH), ("a_log", SH), ("d_skip", SH))


def _pack_big(ada, win, qb, kvb, out, lead):
    flat = [a.reshape(a.shape[:lead] + (-1, 1024)) for a in (ada, win, qb, kvb, out)]
    used = sum(f.shape[lead] for f in flat)
    pad = jnp.zeros(flat[0].shape[:lead] + (R_BIG - used, 1024), flat[0].dtype)
    return jnp.concatenate(flat + [pad], axis=lead)


def _unpack_big(p):
    out, r = [], 0
    for n in (R_ADA, R_IN, R_QB, R_KVB, R_OUT):
        out.append(p[..., r:r + n, :])
        r += n
    return out


def _pack_small(d, lead):
    flat = [d[name].reshape(d[name].shape[:lead] + (-1,)) for name, _ in _SMALL]
    used = sum(f.shape[lead] for f in flat)
    pad = jnp.zeros(flat[0].shape[:lead] + (R_SMALL * 1024 - used,), F32)
    return jnp.concatenate(flat + [pad], axis=lead).reshape(flat[0].shape[:lead] + (R_SMALL, 1024))


def _unpack_small(p):
    flat = p.reshape(-1)
    out, r = {}, 0
    for name, n in _SMALL:
        out[name] = flat[r:r + n]
        r += n
    return out


def _in_to_padded(w):
    z = lambda n: jnp.zeros((w.shape[0], n), w.dtype)
    return jnp.concatenate([w[:, 0:384], w[:, 384:640], w[:, 704:1728], w[:, 1728:3264], w[:, 3280:4304],
                            w[:, 640:704], z(64), w[:, 3264:3280], z(112)], axis=1)


def _in_from_padded(g):
    return jnp.concatenate([g[:, 0:384], g[:, 384:640], g[:, P_KR[0]:P_KR[0] + 64], g[:, 640:1664], g[:, 1664:3200],
                            g[:, P_DT[0]:P_DT[0] + 16], g[:, 3200:4224]], axis=1)


def kernel(x, c, positions, w_ada, b_ada, w_in, q_norm_g, w_qb, kv_norm_g, w_kvb, conv_w, conv_b, dt_bias, a_log, d_skip, ssm_norm_g, w_out, ln_g, ln_b, loss_target, m_w_ada, m_b_ada, m_w_in, m_q_norm_g, m_w_qb, m_kv_norm_g, m_w_kvb, m_conv_w, m_conv_b, m_dt_bias, m_a_log, m_d_skip, m_ssm_norm_g, m_w_out, m_ln_g, m_ln_b, v_w_ada, v_b_ada, v_w_in, v_q_norm_g, v_w_qb, v_kv_norm_g, v_w_kvb, v_conv_w, v_conv_b, v_dt_bias, v_a_log, v_d_skip, v_ssm_norm_g, v_w_out, v_ln_g, v_ln_b):
    S = x.shape[1]
    xv = x[0]
    tgt = loss_target[0]

    wb = _pack_big(w_ada[0], w_in[0], w_qb[0], w_kvb[0], w_out[0], 0).astype(BF16)
    ws = jnp.zeros((8, 512), F32).at[0:CW, 0:CC // 4].set(conv_w[0])
    gb, gsm = _gather_weights(wb, ws)
    f_ada, f_in, f_qb, f_kvb, f_out = _unpack_big(gb)
    cat1 = lambda f, r, n: jnp.concatenate([f[k].reshape(r, n) for k in range(4)], axis=1)
    wada = cat1(f_ada, D, 3 * D // 4)
    win = _in_to_padded(cat1(f_in, D, IN_W // 4))
    wqb = cat1(f_qb, QL, NH * QKD // 4).reshape(QL, NH, QKD)
    wq = jnp.concatenate([wqb, jnp.zeros((QL, NH, HP - QKD), BF16)], axis=2).reshape(QL, NH * HP)
    wkvb = cat1(f_kvb, KVL, NH * (NOPE + VD) // 4).reshape(KVL, NH, NOPE + VD)
    wk = wkvb[:, :, 0:NOPE].reshape(KVL, NH * NOPE)
    wv = wkvb[:, :, NOPE:].reshape(KVL, NH * VD)
    wout = f_out.reshape(MIX, D)
    cwf = jnp.concatenate([gsm[k, 0:CW, 0:CC // 4] for k in range(4)], axis=1)

    half = ROPE // 2
    invf = 1.0 / (ROPE_THETA ** (jnp.arange(half, dtype=F32) / half))
    invf = jnp.concatenate([invf, invf, jnp.zeros((128 - ROPE,), F32)]).reshape(1, 128)
    pos = positions.reshape(S, 1)
    pad128 = lambda a: jnp.concatenate([a.reshape(1, SH), jnp.zeros((1, 128 - SH), F32)], axis=1)
    dtb, alog = pad128(dt_bias), pad128(a_log)
    dskx = jnp.repeat(d_skip.reshape(SH), SP).reshape(1, SW)

    mod = _ada(jnp.broadcast_to(c, (8, D)), wada, b_ada)
    qlat, ckv, za, xbc, zs, dtraw, q, k, v, kt, vt = _inproj(xv, mod, win, q_norm_g, kv_norm_g, wq, wk, wv, wk.T, wv.T, pos, invf)
    o, lse = _attn_fwd(q, k, vt)
    y, htp, ossm = _ssd_fwd(xbc, dtraw, zs, cwf, conv_b, dtb, alog, dskx, ssm_norm_g)
    gx1, do, dza, dossm, delta, dwout, vec_o = _outproj(o, za, ossm, xv, tgt, wout, mod, ln_g, ln_b)
    loss = lax.psum(0.5 / D * jnp.sum(vec_o[0]), ("x", "y", "c"))

    tq = min(TQ, S)
    dk, dv, dq = _attn_bwd(q, k, kt, v, do, lse, delta.reshape(NH, S // tq, 1, tq))
    dxbc, ddt, dzs, dcw, dcb, dvec, dgssm = _ssd_bwd(xbc, dtraw, zs, y, htp, dossm, cwf, conv_b, dtb, alog, dskx, ssm_norm_g)
    dql, dckv, dkr, dwq, dwk, dwv, dqg, dkvg = _mla_bwd(dq, dk, dv, qlat, ckv, q_norm_g, kv_norm_g, wq, wk, wv, pos, invf)
    gx, dwin, vec_i = _inproj_bwd(xv, gx1, mod, win, dql, dckv, dza, dxbc, dzs, dkr, ddt)
    dmod = jnp.concatenate([vec_i[0:1], vec_i[1:2], vec_o[3:4]], axis=1)
    g_ada = _ada_bwd(c.reshape(D, 1), dmod)

    cols = lambda g: g.reshape(g.shape[0], 4, g.shape[1] // 4).transpose(1, 0, 2)
    g_in = cols(_in_from_padded(dwin))
    g_qb = cols(dwq.reshape(QL, NH, HP)[:, :, 0:QKD].reshape(QL, NH * QKD))
    g_kvb = cols(jnp.concatenate([dwk.reshape(KVL, NH, NOPE), dwv.reshape(KVL, NH, VD)], axis=2).reshape(KVL, NH * (NOPE + VD)))
    g_out = dwout.reshape(4, MIX // 4, D)
    gbig = _pack_big(g_ada, g_in, g_qb, g_kvb, g_out, 1).astype(BF16)
    small = {"b_ada": dmod, "conv_b": dcb, "ssm_norm_g": dgssm, "ln_g": vec_o[1:2], "ln_b": vec_o[2:3],
             "q_norm_g": dqg, "kv_norm_g": dkvg, "dt_bias": dvec[0:1, 0:SH], "a_log": dvec[1:2, 0:SH], "d_skip": dvec[2:3, 0:SH]}
    small = {n: jnp.broadcast_to(a.reshape(1, -1), (4, a.size)) for n, a in small.items()}
    small["conv_w"] = cols(dcw).reshape(4, CW * CC // 4)
    gsmall = _pack_small(small, 1)

    rb, rs = _exchange_grads(gbig, gsmall)
    wbig = _pack_big(w_ada[0], w_in[0], w_qb[0], w_kvb[0], w_out[0], 0)
    mbig = _pack_big(m_w_ada[0], m_w_in[0], m_w_qb[0], m_w_kvb[0], m_w_out[0], 0)
    vbig = _pack_big(v_w_ada[0], v_w_in[0], v_w_qb[0], v_w_kvb[0], v_w_out[0], 0)
    big = _adamw(rb, wbig, mbig, vbig, R_TILE)
    wsm = _pack_small(dict(b_ada=b_ada, conv_w=conv_w, conv_b=conv_b, ssm_norm_g=ssm_norm_g, ln_g=ln_g, ln_b=ln_b,
                           q_norm_g=q_norm_g, kv_norm_g=kv_norm_g, dt_bias=dt_bias, a_log=a_log, d_skip=d_skip), 0)
    msm = _pack_small(dict(b_ada=m_b_ada, conv_w=m_conv_w, conv_b=m_conv_b, ssm_norm_g=m_ssm_norm_g, ln_g=m_ln_g, ln_b=m_ln_b,
                           q_norm_g=m_q_norm_g, kv_norm_g=m_kv_norm_g, dt_bias=m_dt_bias, a_log=m_a_log, d_skip=m_d_skip), 0)
    vsm = _pack_small(dict(b_ada=v_b_ada, conv_w=v_conv_w, conv_b=v_conv_b, ssm_norm_g=v_ssm_norm_g, ln_g=v_ln_g, ln_b=v_ln_b,
                           q_norm_g=v_q_norm_g, kv_norm_g=v_kv_norm_g, dt_bias=v_dt_bias, a_log=v_a_log, d_skip=v_d_skip), 0)
    sm = _adamw(rs, wsm, msm, vsm, R_SMALL)

    order = ["w_ada", "b_ada", "w_in", "q_norm_g", "w_qb", "kv_norm_g", "w_kvb", "conv_w", "conv_b", "dt_bias", "a_log",
             "d_skip", "ssm_norm_g", "w_out", "ln_g", "ln_b"]
    shapes = dict(w_ada=w_ada.shape, b_ada=b_ada.shape, w_in=w_in.shape, q_norm_g=q_norm_g.shape, w_qb=w_qb.shape,
                  kv_norm_g=kv_norm_g.shape, w_kvb=w_kvb.shape, conv_w=conv_w.shape, conv_b=conv_b.shape, dt_bias=dt_bias.shape,
                  a_log=a_log.shape, d_skip=d_skip.shape, ssm_norm_g=ssm_norm_g.shape, w_out=w_out.shape, ln_g=ln_g.shape,
                  ln_b=ln_b.shape)
    outs = []
    for kind in range(4):
        b_ada_, b_in_, b_qb_, b_kvb_, b_out_ = _unpack_big(big[kind])
        d = _unpack_small(sm[kind])
        d.update(w_ada=b_ada_, w_in=b_in_, w_qb=b_qb_, w_kvb=b_kvb_, w_out=b_out_)
        outs.extend(d[n].reshape(shapes[n]) for n in order)
    return (loss, gx.reshape(x.shape), *outs)
```

```python
import functools
import math

import numpy as np
import jax
import jax.numpy as jnp
from jax import lax
from jax.experimental import pallas as pl
from jax.experimental.pallas import tpu as pltpu

F32 = jnp.float32
BF16 = jnp.bfloat16
HIGHEST = lax.Precision.HIGHEST
MESH_ID = pl.DeviceIdType.MESH

D = 1024
NH = 8
NOPE = 128
ROPE = 64
VD = 128
QKD = NOPE + ROPE
HP = 256
QL = 384
KVL = 256
ROPE_THETA = 10000.0
SH = 16
SP = 64
SG = 2
SN = 128
CW = 4
CH = 128
SW = SH * SP
CC = SW + 2 * SG * SN
GW = SW // SG
MIX = 2 * D
IN_W = 4304
ALPHA = 2.0 ** 0.25
RMS_EPS = 1e-6
LN_EPS = 1e-5
SCALE = QKD ** -0.5
LN2 = math.log(2.0)
QSCALE = SCALE / LN2
LR, B1, B2, EPS, WD, STEP = 0.001, 0.9, 0.999, 1e-08, 0.01, 10

P_Q = (0, 384)
P_KV = (384, 640)
P_ZA = (640, 1664)
P_XBC = (1664, 3200)
P_ZS = (3200, 4224)
P_KR = (4224, 4352)
P_DT = (4352, 4480)
IN_P = 4480

R_ADA, R_IN, R_QB, R_KVB, R_OUT = 768, 1076, 144, 128, 512
R_BIG = 2640
R_TILE = 240
R_SMALL = 16

TM = 256
TQ = 512
TQF = 1024
VMEM_LIMIT = 56 * 1024 * 1024


def _cp(sem=None):
    return pltpu.CompilerParams(dimension_semantics=sem, vmem_limit_bytes=VMEM_LIMIT)


def _mm(a, b):
    return jnp.dot(a, b, preferred_element_type=F32)


def _nt(a, b):
    return lax.dot_general(a, b, (((1,), (1,)), ((), ())), preferred_element_type=F32)


def _tn(a, b):
    return lax.dot_general(a, b, (((0,), (0,)), ((), ())), preferred_element_type=F32)


def _mm_hi(a, b):
    return jnp.dot(a, b, precision=HIGHEST, preferred_element_type=F32)


def _nt_hi(a, b):
    return lax.dot_general(a, b, (((1,), (1,)), ((), ())), precision=HIGHEST, preferred_element_type=F32)


def _sigmoid(z):
    return 1.0 / (1.0 + jnp.exp(-z))


def _softplus(z):
    return jnp.maximum(z, 0.0) + jnp.log1p(jnp.exp(-jnp.abs(z)))


def _rope(t, cs, sn):
    lane = lax.broadcasted_iota(jnp.int32, t.shape, 1)
    rot = jnp.where(lane < ROPE // 2, -pltpu.roll(t, 128 - ROPE // 2, 1), pltpu.roll(t, ROPE // 2, 1))
    return t * cs + rot * sn


def _rope_t(t, cs, sn):
    lane = lax.broadcasted_iota(jnp.int32, t.shape, 1)
    y = t * sn
    rot = jnp.where(lane < ROPE // 2, -pltpu.roll(y, 128 - ROPE // 2, 1), pltpu.roll(y, ROPE // 2, 1))
    return t * cs - rot


def _full(shape):
    n = len(shape)
    return pl.BlockSpec(shape, lambda *_: (0,) * n)


def _const(shape):
    n = len(shape)
    return pl.BlockSpec(shape, lambda *_: (0,) * n, pipeline_mode=pl.Buffered(1))


def _gather_weights(wb, ws):
    def body(wb_ref, ws_ref, ob_ref, os_ref, send_sems, recv_sems, local_sems):
        x, y, c = lax.axis_index("x"), lax.axis_index("y"), lax.axis_index("c")
        me = 2 * x + y
        chips = [(1 - x, y), (x, 1 - y), (1 - x, 1 - y)]
        pairs = ((wb_ref, ob_ref), (ws_ref, os_ref))

        def copy(a, j, slot, to):
            src, dst = pairs[a]
            return pltpu.make_async_remote_copy(
                src_ref=src, dst_ref=dst.at[slot], send_sem=send_sems.at[a, j], recv_sem=recv_sems.at[a, j],
                device_id=to, device_id_type=MESH_ID)

        local = [pltpu.make_async_copy(pairs[a][0], pairs[a][1].at[me], local_sems.at[a]) for a in range(2)]
        for cp in local:
            cp.start()
        sends = [copy(a, j, me, (px, py, c)) for a in range(2) for j, (px, py) in enumerate(chips)]
        for cp in sends:
            cp.start()
        for a in range(2):
            for j, (px, py) in enumerate(chips):
                copy(a, j, 2 * px + py, (x, y, c)).wait_recv()
        for cp in sends:
            cp.wait_send()
        for cp in local:
            cp.wait()

    hbm = pl.BlockSpec(memory_space=pltpu.HBM)
    return pl.pallas_call(
        body, name="gather_weights",
        out_shape=(jax.ShapeDtypeStruct((4,) + wb.shape, wb.dtype), jax.ShapeDtypeStruct((4,) + ws.shape, ws.dtype)),
        in_specs=[hbm, hbm], out_specs=(hbm, hbm),
        scratch_shapes=[pltpu.SemaphoreType.DMA((2, 3)), pltpu.SemaphoreType.DMA((2, 3)), pltpu.SemaphoreType.DMA((2,))],
    )(wb, ws)


def _exchange_grads(gb, gs):
    def body(gb_ref, gs_ref, rb_ref, rs_ref, send_sems, recv_sems, local_sems):
        x, y, c = lax.axis_index("x"), lax.axis_index("y"), lax.axis_index("c")
        chip = 2 * x + y
        sibling = (x, y, 1 - c)
        chips = [(1 - x, y), (x, 1 - y), (1 - x, 1 - y)]
        pairs = ((gb_ref, rb_ref), (gs_ref, rs_ref))

        def slot(px, py, pc):
            return 4 * px + 2 * py + pc

        def copy(a, k, src, s, to):
            return pltpu.make_async_remote_copy(
                src_ref=src, dst_ref=pairs[a][1].at[s], send_sem=send_sems.at[a, k], recv_sem=recv_sems.at[a, k],
                device_id=to, device_id_type=MESH_ID)

        mine = slot(x, y, c)
        local = [pltpu.make_async_copy(pairs[a][0].at[chip], pairs[a][1].at[mine], local_sems.at[a]) for a in range(2)]
        for cp in local:
            cp.start()
        first = []
        for a in range(2):
            g_ref, r_ref = pairs[a]
            first.append(copy(a, 0, g_ref.at[chip], mine, sibling))
            for j, (px, py) in enumerate(chips):
                first.append(copy(a, 1 + j, g_ref.at[2 * px + py], mine, (px, py, c)))
        for cp in first:
            cp.start()
        passed = []
        for a in range(2):
            g_ref, r_ref = pairs[a]
            for j, (px, py) in enumerate(chips):
                s = slot(px, py, c)
                copy(a, 1 + j, g_ref.at[chip], s, (x, y, c)).wait_recv()
                fwd = copy(a, 4 + j, r_ref.at[s], s, sibling)
                fwd.start()
                passed.append(fwd)
        for a in range(2):
            g_ref, r_ref = pairs[a]
            copy(a, 0, g_ref.at[chip], slot(x, y, 1 - c), (x, y, c)).wait_recv()
            for j, (px, py) in enumerate(chips):
                copy(a, 4 + j, g_ref.at[chip], slot(px, py, 1 - c), (x, y, c)).wait_recv()
        for cp in first + passed:
            cp.wait_send()
        for cp in local:
            cp.wait()

    hbm = pl.BlockSpec(memory_space=pltpu.HBM)
    return pl.pallas_call(
        body, name="exchange_grads",
        out_shape=(jax.ShapeDtypeStruct((8,) + gb.shape[1:], gb.dtype), jax.ShapeDtypeStruct((8,) + gs.shape[1:], gs.dtype)),
        in_specs=[hbm, hbm], out_specs=(hbm, hbm),
        scratch_shapes=[pltpu.SemaphoreType.DMA((2, 7)), pltpu.SemaphoreType.DMA((2, 7)), pltpu.SemaphoreType.DMA((2,))],
    )(gb, gs)


def _ada(c8, w_ada, b_ada):
    def body(c_ref, w_ref, b_ref, o_ref):
        o_ref[...] = _mm(c_ref[...].astype(BF16), w_ref[...]) + b_ref[...]

    return pl.pallas_call(body, name="ada", out_shape=jax.ShapeDtypeStruct((8, 3 * D), F32),
                          compiler_params=_cp())(c8, w_ada, b_ada)


def _inproj(x, mod, win, qg, kvg, wq, wk, wv, wkt, wvt, pos, invf):
    S = x.shape[0]
    tm = min(TM, S)

    def body(x_ref, mod_ref, win_ref, qg_ref, kvg_ref, wq_ref, wk_ref, wv_ref, wkt_ref, wvt_ref, pos_ref, invf_ref,
             qlat_ref, ckv_ref, za_ref, xbc_ref, zs_ref, dt_ref, q_ref, k_ref, v_ref, kt_ref, vt_ref):
        shift = mod_ref[0:1, 0:D]
        scale = mod_ref[0:1, D:2 * D]
        u = (x_ref[...] * (1.0 + scale) + shift).astype(BF16)

        def proj(p):
            return _mm(u, win_ref[:, p[0]:p[1]])

        ql = proj(P_Q)
        ckv = proj(P_KV)
        qlat_ref[...] = ql
        ckv_ref[...] = ckv
        za_ref[...] = proj(P_ZA)
        xbc_ref[...] = proj(P_XBC)
        zs_ref[...] = proj(P_ZS)
        dt_ref[...] = proj(P_DT)
        kr = proj(P_KR)

        ang = pos_ref[...].astype(F32) * invf_ref[...]
        cs = jnp.cos(ang)
        sn = jnp.sin(ang)

        rq = lax.rsqrt(jnp.mean(ql * ql, axis=-1, keepdims=True) + RMS_EPS)
        qn = (ql * rq * qg_ref[...]).astype(BF16)
        for h in range(NH):
            qh = _mm(qn, wq_ref[:, h * HP:(h + 1) * HP])
            q_ref[h, :, 0:NOPE] = (qh[:, 0:NOPE] * QSCALE).astype(BF16)
            q_ref[h, :, NOPE:HP] = (_rope(qh[:, NOPE:HP], cs, sn) * QSCALE).astype(BF16)

        rk = lax.rsqrt(jnp.mean(ckv * ckv, axis=-1, keepdims=True) + RMS_EPS)
        kn = (ckv * rk * kvg_ref[...]).astype(BF16)
        knope = _mm(kn, wk_ref[...])
        vall = _mm(kn, wv_ref[...])
        krf = _rope(kr, cs, sn)
        krr = krf.astype(BF16)
        krt = krf.T.astype(BF16)
        for h in range(NH):
            k_ref[h, :, 0:NOPE] = knope[:, h * NOPE:(h + 1) * NOPE].astype(BF16)
            k_ref[h, :, NOPE:HP] = krr
            v_ref[h] = vall[:, h * VD:(h + 1) * VD].astype(BF16)
            kt_ref[h, 0:NOPE, :] = _nt(wkt_ref[h * NOPE:(h + 1) * NOPE, :], kn).astype(BF16)
            kt_ref[h, NOPE:HP, :] = krt
            vt_ref[h] = _nt(wvt_ref[h * VD:(h + 1) * VD, :], kn).astype(BF16)

    row = lambda n: pl.BlockSpec((tm, n), lambda i: (i, 0))
    heads = lambda n: pl.BlockSpec((NH, tm, n), lambda i: (0, i, 0))
    heads_t = lambda n: pl.BlockSpec((NH, None, n, tm), lambda i: (0, i, 0, 0))
    sd = lambda n: jax.ShapeDtypeStruct((S, n), F32)
    hd = lambda n: jax.ShapeDtypeStruct((NH, S, n), BF16)
    ht = lambda n: jax.ShapeDtypeStruct((NH, S // tm, n, tm), BF16)
    return pl.pallas_call(
        body, name="inproj", grid=(S // tm,),
        in_specs=[row(D), _const((8, 3 * D)), _const((D, IN_P)), _const((1, QL)), _const((1, KVL)),
                  _const((QL, NH * HP)), _const((KVL, NH * NOPE)), _const((KVL, NH * VD)),
                  _const((NH * NOPE, KVL)), _const((NH * VD, KVL)), row(1), _const((1, 128))],
        out_specs=[row(QL), row(KVL), row(D), row(CC), row(D), row(128), heads(HP), heads(HP), heads(VD),
                   heads_t(HP), heads_t(VD)],
        out_shape=[sd(QL), sd(KVL), sd(D), sd(CC), sd(D), sd(128), hd(HP), hd(HP), hd(VD), ht(HP), ht(VD)],
        compiler_params=_cp(("arbitrary",)),
    )(x, mod, win, qg, kvg, wq, wk, wv, wkt, wvt, pos, invf)


def _attn_fwd(q, k, vt):
    _, S, _ = q.shape
    tq = min(TQF, S)
    nq = S // tq
    half = tq // 2
    tb = vt.shape[3]
    nsb = half // tb

    def body(q_ref, k_ref, vt_ref, o_ref, lse_ref):
        i = pl.program_id(1)
        qb = q_ref[...]

        def scores(j, hb):
            off = pl.multiple_of(j * tq + hb * half, half)
            return _nt(k_ref[pl.ds(off, half), :], qb)

        def update(j, hb, s, carry):
            m, l, acc = carry
            m_new = jnp.maximum(m, jnp.max(s, axis=0, keepdims=True))
            a = jnp.exp2(m - m_new)
            p = jnp.exp2(s - m_new)
            l = a * l + jnp.sum(p, axis=0, keepdims=True)
            pb = p.astype(BF16)
            acc = a * acc
            for sb in range(nsb):
                acc = acc + _mm(vt_ref[(2 * j + hb) * nsb + sb], pb[sb * tb:(sb + 1) * tb, :])
            return m_new, l, acc

        def trip(j, carry, masked):
            s = [scores(j, hb) for hb in range(2)]
            if masked:
                r = lax.broadcasted_iota(jnp.int32, s[0].shape, 0)
                cidx = lax.broadcasted_iota(jnp.int32, s[0].shape, 1)
                s = [jnp.where(cidx >= r + hb * half, s[hb], -1e30) for hb in range(2)]
            for hb in range(2):
                carry = update(j, hb, s[hb], carry)
            return carry

        init = (jnp.full((1, tq), -1e30, F32), jnp.zeros((1, tq), F32), jnp.zeros((VD, tq), F32))
        carry = lax.fori_loop(0, i, lambda j, cr: trip(j, cr, False), init)
        m, l, acc = trip(i, carry, True)
        o_ref[...] = (acc / l).T
        lse_ref[...] = m + jnp.log2(l)

    return pl.pallas_call(
        body, name="attn_fwd", grid=(NH, nq),
        in_specs=[pl.BlockSpec((None, tq, HP), lambda h, i: (h, i, 0)),
                  pl.BlockSpec((None, S, HP), lambda h, i: (h, 0, 0)),
                  pl.BlockSpec((None, S // tb, VD, tb), lambda h, i: (h, 0, 0, 0))],
        out_specs=[pl.BlockSpec((tq, VD), lambda h, i: (i, h)),
                   pl.BlockSpec((None, None, 1, tq), lambda h, i: (h, i, 0, 0))],
        out_shape=[jax.ShapeDtypeStruct((S, NH * VD), F32), jax.ShapeDtypeStruct((NH, nq, 1, tq), F32)],
        compiler_params=_cp(("arbitrary", "arbitrary")),
    )(q, k, vt)


def _ssd_consts():
    tri = np.tril(np.ones((CH, CH), np.float32))
    e16 = np.zeros((128, SW), np.float32)
    for h in range(SH):
        e16[h, h * SP:(h + 1) * SP] = 1.0
    return jnp.asarray(tri), jnp.asarray(tri.T.copy()), jnp.asarray(e16)


def _ssd_chunk_fwd_common(xbc_ref, halo_ref, dtraw_ref, cw_ref, cb_ref, dtb_ref, alog_ref, tri_ref, e16_ref, ext, first):
    ext[0:8, :] = jnp.where(first, 0.0, halo_ref[...])
    ext[8:8 + CH, :] = xbc_ref[...]
    cw = cw_ref[...]
    xc = cb_ref[...] + cw[0:1, :] * ext[5:5 + CH, :]
    for kk in range(1, CW):
        xc = xc + cw[kk:kk + 1, :] * ext[5 + kk:5 + kk + CH, :]
    sact = _sigmoid(xc)
    act = xc * sact
    lane = lax.broadcasted_iota(jnp.int32, (1, 128), 1)
    arow = jnp.where(lane < SH, -jnp.exp(alog_ref[...]), 0.0)
    dtpre = dtraw_ref[...] + dtb_ref[...]
    dt = _softplus(dtpre)
    a = dt * arow
    cum = _mm_hi(tri_ref[...], a)
    cumx = _mm_hi(cum, e16_ref[...])
    dtx = _mm_hi(dt, e16_ref[...])
    return xc, sact, act, arow, dtpre, dt, cum, cumx, dtx


def _ssd_fwd(xbc, dtraw, zs, conv_w, conv_b, dtb, alog, dskx, gssm):
    S = xbc.shape[0]
    nc = S // CH
    tri, _, e16 = _ssd_consts()

    def body(xbc_ref, halo_ref, dtraw_ref, zs_ref, cw_ref, cb_ref, dtb_ref, alog_ref, dsk_ref, g_ref, tri_ref, e16_ref,
             y_ref, htp_ref, o_ref, ht, ext):
        i = pl.program_id(0)

        @pl.when(i == 0)
        def _():
            ht[...] = jnp.zeros_like(ht)

        xc, sact, act, arow, dtpre, dt, cum, cumx, dtx = _ssd_chunk_fwd_common(
            xbc_ref, halo_ref, dtraw_ref, cw_ref, cb_ref, dtb_ref, alog_ref, tri_ref, e16_ref, ext, i == 0)
        cum_t = cum.T
        xs = act[:, 0:SW]
        lastx = cumx[CH - 1:CH, :]
        xh = xs * dtx
        eexp = jnp.exp(cumx)
        dte = jnp.exp(lastx - cumx)
        cdx = jnp.exp(lastx)
        htp = ht[...]
        htp_ref[...] = htp
        xw = (xh * dte).astype(BF16)
        xb = xh.astype(BF16)
        trim = tri_ref[...] > 0.5
        lane = lax.broadcasted_iota(jnp.int32, (CH, 128), 1)
        parts = []
        for g in range(SG):
            gl = slice(g * GW, (g + 1) * GW)
            bg = act[:, SW + g * SN:SW + (g + 1) * SN].astype(BF16)
            cg = act[:, SW + SG * SN + g * SN:SW + SG * SN + (g + 1) * SN].astype(BF16)
            cbm = _nt(cg, bg)
            yoff = eexp[:, gl] * _mm(cg, htp[:, gl].astype(BF16))
            ht[:, gl] = htp[:, gl] * cdx[:, gl] + _tn(bg, xw[:, gl])
            for pr in range(GW // 128):
                h0 = g * (SH // SG) + 2 * pr
                lo = g * GW + pr * 128
                xp = xb[:, lo:lo + 128]
                res = []
                for hh in (h0, h0 + 1):
                    seg = cum[:, hh:hh + 1] - cum_t[hh:hh + 1, :]
                    mh = jnp.where(trim, cbm * jnp.exp(seg), 0.0).astype(BF16)
                    res.append(_mm(mh, xp))
                parts.append(jnp.where(lane < SP, res[0], res[1]) + yoff[:, pr * 128:(pr + 1) * 128])
        y = jnp.concatenate(parts, axis=1) + xs * dsk_ref[...]
        y_ref[...] = y
        z = zs_ref[...]
        hf = y * (z * _sigmoid(z))
        outs = []
        for g in range(SG):
            hg = hf[:, g * GW:(g + 1) * GW]
            rs = lax.rsqrt(jnp.mean(hg * hg, axis=-1, keepdims=True) + RMS_EPS)
            outs.append(hg * rs)
        o_ref[...] = (jnp.concatenate(outs, axis=1) * g_ref[...]).astype(BF16)

    row = lambda n: pl.BlockSpec((CH, n), lambda i: (i, 0))
    return pl.pallas_call(
        body, name="ssd_fwd", grid=(nc,),
        in_specs=[row(CC), pl.BlockSpec((8, CC), lambda i: (jnp.maximum(i * (CH // 8) - 1, 0), 0)), row(128), row(SW),
                  _const((CW, CC)), _const((1, CC)), _const((1, 128)), _const((1, 128)), _const((1, SW)), _const((1, SW)),
                  _const((CH, CH)), _const((128, SW))],
        out_specs=[row(SW), pl.BlockSpec((None, SN, SW), lambda i: (i, 0, 0)), row(SW)],
        out_shape=[jax.ShapeDtypeStruct((S, SW), F32), jax.ShapeDtypeStruct((nc, SN, SW), F32),
                   jax.ShapeDtypeStruct((S, SW), BF16)],
        scratch_shapes=[pltpu.VMEM((SN, SW), F32), pltpu.VMEM((8 + CH, CC), F32)],
        compiler_params=_cp(("arbitrary",)),
    )(xbc, xbc, dtraw, zs, conv_w, conv_b, dtb, alog, dskx, gssm, tri, e16)


def _outproj(o, za, ossm, x, tgt, wout, mod, ln_g, ln_b):
    S = x.shape[0]
    tm = min(TM, S)

    def body(o_ref, za_ref, os_ref, x_ref, t_ref, w_ref, mod_ref, g_ref, b_ref,
             gx_ref, do_ref, dza_ref, dos_ref, delta_ref, dw_ref, vec_ref):
        i = pl.program_id(0)

        @pl.when(i == 0)
        def _():
            dw_ref[...] = jnp.zeros_like(dw_ref)
            vec_ref[...] = jnp.zeros_like(vec_ref)

        gate = mod_ref[0:1, 2 * D:3 * D]
        ov = o_ref[...]
        z = za_ref[...]
        sz = _sigmoid(z)
        silz = z * sz
        a = (ov * silz).astype(BF16)
        osb = os_ref[...]
        mixed = _mm(a, w_ref[0:D, :]) + _mm(osb, w_ref[D:MIX, :])
        xv = x_ref[...]
        hres = ALPHA * xv + gate * mixed
        mu = jnp.mean(hres, axis=-1, keepdims=True)
        hc = hres - mu
        var = jnp.mean(hc * hc, axis=-1, keepdims=True)
        rstd = lax.rsqrt(var + LN_EPS)
        xhat = hc * rstd
        g = g_ref[...]
        yv = xhat * g + b_ref[...]
        err = yv - t_ref[...]
        dy = err * (1.0 / D)
        vec_ref[0:1, :] += jnp.sum(err * err, axis=0, keepdims=True)
        vec_ref[1:2, :] += jnp.sum(dy * xhat, axis=0, keepdims=True)
        vec_ref[2:3, :] += jnp.sum(dy, axis=0, keepdims=True)
        dxh = dy * g
        dh = rstd * (dxh - jnp.mean(dxh, axis=-1, keepdims=True) - xhat * jnp.mean(dxh * xhat, axis=-1, keepdims=True))
        gx_ref[...] = ALPHA * dh
        vec_ref[3:4, :] += jnp.sum(dh * mixed, axis=0, keepdims=True)
        dmixed = (gate * dh).astype(BF16)
        dw_ref[0:D, :] += _tn(a, dmixed)
        dw_ref[D:MIX, :] += _tn(osb, dmixed)
        da = _nt(dmixed, w_ref[0:D, :])
        dos_ref[...] = _nt(dmixed, w_ref[D:MIX, :])
        dov = da * silz
        do_ref[...] = dov.astype(BF16)
        dza_ref[...] = (da * ov * (sz * (1.0 + z * (1.0 - sz)))).astype(BF16)
        pr = dov * ov
        for h in range(NH):
            delta_ref[h] = jnp.sum(pr[:, h * VD:(h + 1) * VD], axis=-1, keepdims=True)

    row = lambda n: pl.BlockSpec((tm, n), lambda i: (i, 0))
    return pl.pallas_call(
        body, name="outproj", grid=(S // tm,),
        in_specs=[row(D), row(D), row(D), row(D), row(D), _const((MIX, D)), _const((8, 3 * D)), _const((1, D)), _const((1, D))],
        out_specs=[row(D), row(D), row(D), row(D), pl.BlockSpec((NH, tm, 1), lambda i: (0, i, 0)),
                   _full((MIX, D)), _full((8, D))],
        out_shape=[jax.ShapeDtypeStruct((S, D), F32), jax.ShapeDtypeStruct((S, D), BF16), jax.ShapeDtypeStruct((S, D), BF16),
                   jax.ShapeDtypeStruct((S, D), F32), jax.ShapeDtypeStruct((NH, S, 1), F32),
                   jax.ShapeDtypeStruct((MIX, D), F32), jax.ShapeDtypeStruct((8, D), F32)],
        compiler_params=_cp(("arbitrary",)),
    )(o, za, ossm, x, tgt, wout, mod, ln_g, ln_b)


def _attn_bwd(q, k, kt, v, do, lse, delta):
    _, S, _ = q.shape
    tk = min(TQ, S // 2)
    nk = S // tk
    tq = 2 * tk
    nq = S // tq
    tb = kt.shape[3]
    nsb = tk // tb

    def body(k_ref, kt_ref, v_ref, q_ref, do_ref, lse_ref, dl_ref, dk_ref, dv_ref, dqt_ref):
        j = pl.program_id(1)
        kb = k_ref[...]
        vb = v_ref[...]

        @pl.when(j == 0)
        def _():
            dqt_ref[...] = jnp.zeros_like(dqt_ref)

        dk_ref[...] = jnp.zeros_like(dk_ref)
        dv_ref[...] = jnp.zeros_like(dv_ref)

        def step(i, masked):
            off = pl.multiple_of(i * tq, tq)
            qb = q_ref[pl.ds(off, tq), :]
            dob = do_ref[pl.ds(off, tq), :]
            pt = jnp.exp2(_nt(kb, qb) - lse_ref[i])
            if masked:
                r = lax.broadcasted_iota(jnp.int32, pt.shape, 0)
                cidx = lax.broadcasted_iota(jnp.int32, pt.shape, 1)
                pt = jnp.where(i * tq + cidx >= j * tk + r, pt, 0.0)
            dv_ref[...] += _mm(pt.astype(BF16), dob)
            dsb = (pt * (_nt(vb, dob) - dl_ref[i])).astype(BF16)
            dk_ref[...] += _mm(dsb, qb)
            acc = dqt_ref[i]
            for sb in range(nsb):
                acc = acc + _mm(kt_ref[sb], dsb[sb * tb:(sb + 1) * tb, :])
            dqt_ref[i] = acc

        first = j // 2
        step(first, True)

        def loop_body(i, carry):
            step(i, False)
            return carry

        lax.fori_loop(first + 1, nq, loop_body, 0)
        dk_ref[...] = dk_ref[...] * LN2

    return pl.pallas_call(
        body, name="attn_bwd", grid=(NH, nk),
        in_specs=[pl.BlockSpec((None, tk, HP), lambda h, j: (h, j, 0)),
                  pl.BlockSpec((None, nsb, HP, tb), lambda h, j: (h, j, 0, 0)),
                  pl.BlockSpec((None, tk, VD), lambda h, j: (h, j, 0)),
                  pl.BlockSpec((None, S, HP), lambda h, j: (h, 0, 0), pipeline_mode=pl.Buffered(1)),
                  pl.BlockSpec((S, VD), lambda h, j: (0, h), pipeline_mode=pl.Buffered(1)),
                  pl.BlockSpec((None, nq, 1, tq), lambda h, j: (h, 0, 0, 0)),
                  pl.BlockSpec((None, nq, 1, tq), lambda h, j: (h, 0, 0, 0))],
        out_specs=[pl.BlockSpec((None, tk, HP), lambda h, j: (h, j, 0)),
                   pl.BlockSpec((None, tk, VD), lambda h, j: (h, j, 0)),
                   pl.BlockSpec((None, nq, HP, tq), lambda h, j: (h, 0, 0, 0), pipeline_mode=pl.Buffered(1))],
        out_shape=[jax.ShapeDtypeStruct((NH, S, HP), F32), jax.ShapeDtypeStruct((NH, S, VD), F32),
                   jax.ShapeDtypeStruct((NH, nq, HP, tq), F32)],
        compiler_params=_cp(("arbitrary", "arbitrary")),
    )(k, kt, v, q, do, lse.reshape(NH, nq, 1, tq), delta.reshape(NH, nq, 1, tq))


def _ssd_bwd(xbc, dtraw, zs, y, htp, dossm, conv_w, conv_b, dtb, alog, dskx, gssm):
    S = xbc.shape[0]
    nc = S // CH
    tri, triu, e16 = _ssd_consts()

    def body(xbc_ref, halo_ref, dtraw_ref, zs_ref, y_ref, htp_ref, dos_ref,
             cw_ref, cb_ref, dtb_ref, alog_ref, dsk_ref, g_ref, tri_ref, triu_ref, e16_ref,
             dxbc_ref, ddt_ref, dzs_ref, dcw_ref, dcb_ref, dvec_ref, dg_ref,
             dht, ext, dext, dskacc):
        r = pl.program_id(0)
        i = nc - 1 - r

        @pl.when(r == 0)
        def _():
            dht[...] = jnp.zeros_like(dht)
            dext[CH:CH + 8, :] = jnp.zeros((8, CC), F32)
            dskacc[...] = jnp.zeros_like(dskacc)
            dcw_ref[...] = jnp.zeros_like(dcw_ref)
            dcb_ref[...] = jnp.zeros_like(dcb_ref)
            dvec_ref[...] = jnp.zeros_like(dvec_ref)
            dg_ref[...] = jnp.zeros_like(dg_ref)

        xc, sact, act, arow, dtpre, dt, cum, cumx, dtx = _ssd_chunk_fwd_common(
            xbc_ref, halo_ref, dtraw_ref, cw_ref, cb_ref, dtb_ref, alog_ref, tri_ref, e16_ref, ext, i == 0)
        cum_t = cum.T
        xs = act[:, 0:SW]
        lastx = cumx[CH - 1:CH, :]
        xh = xs * dtx
        eexp = jnp.exp(cumx)
        dte = jnp.exp(lastx - cumx)
        cdx = jnp.exp(lastx)
        trim = tri_ref[...] > 0.5
        lane = lax.broadcasted_iota(jnp.int32, (CH, 128), 1)
        rowi = lax.broadcasted_iota(jnp.int32, (CH, 128), 0)

        yv = y_ref[...]
        z = zs_ref[...]
        sz = _sigmoid(z)
        silz = z * sz
        hf = yv * silz
        dn = dos_ref[...] * g_ref[...]
        dhf_parts, nrm_parts = [], []
        for g in range(SG):
            gl = slice(g * GW, (g + 1) * GW)
            hg = hf[:, gl]
            rs = lax.rsqrt(jnp.mean(hg * hg, axis=-1, keepdims=True) + RMS_EPS)
            ng = hg * rs
            dng = dn[:, gl]
            dhf_parts.append(rs * (dng - ng * jnp.mean(dng * ng, axis=-1, keepdims=True)))
            nrm_parts.append(ng)
        nrm = jnp.concatenate(nrm_parts, axis=1)
        dhf = jnp.concatenate(dhf_parts, axis=1)
        dg_ref[...] += jnp.sum(dos_ref[...] * nrm, axis=0, keepdims=True)
        dyv = dhf * silz
        dzs_ref[...] = (dhf * yv * (sz * (1.0 + z * (1.0 - sz)))).astype(BF16)
        dskacc[...] += jnp.sum(dyv * xs, axis=0, keepdims=True)
        dxs_skip = dyv * dsk_ref[...]

        dhtn = dht[...]
        hp = htp_ref[...]
        dlastx = jnp.sum(dhtn * hp, axis=0, keepdims=True) * cdx
        xb = xh.astype(BF16)
        xwf = xh * dte
        dcum = jnp.zeros((CH, 128), F32)
        dcum_t = jnp.zeros((128, CH), F32)
        dxh_parts, dcumx_parts, dlast_parts, db_parts, dc_parts = [], [], [], [], []
        for g in range(SG):
            gl = slice(g * GW, (g + 1) * GW)
            bg = act[:, SW + g * SN:SW + (g + 1) * SN].astype(BF16)
            cg = act[:, SW + SG * SN + g * SN:SW + SG * SN + (g + 1) * SN].astype(BF16)
            hpg = hp[:, gl].astype(BF16)
            dhn = dhtn[:, gl].astype(BF16)
            dyg = dyv[:, gl]
            yoff = eexp[:, gl] * _mm(cg, hpg)
            dz = (dyg * eexp[:, gl]).astype(BF16)
            dcg = _nt(dz, hpg)
            dht[:, gl] = dhtn[:, gl] * cdx[:, gl] + _tn(cg, dz)
            dcumx_g = dyg * yoff
            dbg = _nt(xwf[:, gl].astype(BF16), dhn)
            dxw = _mm(bg, dhn)
            ddte = dxw * xwf[:, gl]
            dcumx_parts.append(dcumx_g - ddte)
            dlast_parts.append(jnp.sum(ddte, axis=0, keepdims=True))
            dxh_g = dxw * dte[:, gl]
            cbm = _nt(cg, bg)
            dcb = jnp.zeros((CH, CH), F32)
            dxp_parts = []
            for pr in range(GW // 128):
                h0 = g * (SH // SG) + 2 * pr
                lo = g * GW + pr * 128
                xp = xb[:, lo:lo + 128]
                dyp = dyv[:, lo:lo + 128]
                dxp = jnp.zeros((CH, 128), F32)
                for idx, hh in enumerate((h0, h0 + 1)):
                    decay = jnp.where(trim, jnp.exp(cum[:, hh:hh + 1] - cum_t[hh:hh + 1, :]), 0.0)
                    mh = cbm * decay
                    keep = (lane < SP) if idx == 0 else (lane >= SP)
                    dym = jnp.where(keep, dyp, 0.0).astype(BF16)
                    dm = _nt(dym, xp)
                    dxp = dxp + _tn(mh.astype(BF16), dym)
                    gm = dm * mh
                    dcum = dcum + jnp.where(lane == hh, jnp.sum(gm, axis=1, keepdims=True), 0.0)
                    dcum_t = dcum_t - jnp.where(rowi == hh, jnp.sum(gm, axis=0, keepdims=True), 0.0)
                    dcb = dcb + dm * decay
                dxp_parts.append(dxp)
            dxh_parts.append(dxh_g + jnp.concatenate(dxp_parts, axis=1))
            dcbb = dcb.astype(BF16)
            dc_parts.append(dcg + _mm(dcbb, bg))
            db_parts.append(dbg + _tn(dcbb, cg))
        dxh = jnp.concatenate(dxh_parts, axis=1)
        dcumx = jnp.concatenate(dcumx_parts, axis=1)
        dlastx = dlastx + jnp.concatenate(dlast_parts, axis=1)
        e16 = e16_ref[...]
        dlast128 = _nt_hi(jnp.broadcast_to(dlastx, (8, SW)), e16)[0:1, :]
        dcum = dcum + dcum_t.T + _nt_hi(dcumx, e16) + jnp.where(rowi == CH - 1, dlast128, 0.0)
        da = _mm_hi(triu_ref[...], dcum)
        ddt = da * arow + _nt_hi(dxh * xs, e16)
        dvec_ref[1:2, :] += jnp.sum(da * dt, axis=0, keepdims=True)
        ddtraw = jnp.where(lane < SH, ddt * _sigmoid(dtpre), 0.0)
        dvec_ref[0:1, :] += jnp.sum(ddtraw, axis=0, keepdims=True)
        ddt_ref[...] = ddtraw.astype(BF16)
        dxs = dxs_skip + dxh * dtx
        dact = jnp.concatenate([dxs] + db_parts + dc_parts, axis=1)
        dxc = dact * (sact * (1.0 + xc * (1.0 - sact)))

        dcb_ref[...] += jnp.sum(dxc, axis=0, keepdims=True)
        for kk in range(CW):
            dcw_ref[kk:kk + 1, :] += jnp.sum(dxc * ext[5 + kk:5 + kk + CH, :], axis=0, keepdims=True)
        dext[0:CH, :] = dxc
        cw = cw_ref[...]
        dxr = cw[CW - 1:CW, :] * dxc
        for kk in range(CW - 1):
            dxr = dxr + cw[kk:kk + 1, :] * dext[CW - 1 - kk:CW - 1 - kk + CH, :]
        dxbc_ref[...] = dxr.astype(BF16)
        dext[CH:CH + 8, :] = dxc[0:8, :]

        @pl.when(r == nc - 1)
        def _():
            dvec_ref[1:2, :] = dvec_ref[1:2, :] * arow
            dvec_ref[2:3, :] = _nt_hi(jnp.broadcast_to(dskacc[...], (8, SW)), e16)[0:1, :]

    rev = lambda n: pl.BlockSpec((CH, n), lambda r: (nc - 1 - r, 0))
    return pl.pallas_call(
        body, name="ssd_bwd", grid=(nc,),
        in_specs=[rev(CC), pl.BlockSpec((8, CC), lambda r: (jnp.maximum((nc - 1 - r) * (CH // 8) - 1, 0), 0)),
                  rev(128), rev(SW), rev(SW), pl.BlockSpec((None, SN, SW), lambda r: (nc - 1 - r, 0, 0)), rev(SW),
                  _const((CW, CC)), _const((1, CC)), _const((1, 128)), _const((1, 128)), _const((1, SW)), _const((1, SW)),
                  _const((CH, CH)), _const((CH, CH)), _const((128, SW))],
        out_specs=[rev(CC), rev(128), rev(SW), _full((CW, CC)), _full((1, CC)), _full((8, 128)), _full((1, SW))],
        out_shape=[jax.ShapeDtypeStruct((S, CC), BF16), jax.ShapeDtypeStruct((S, 128), BF16), jax.ShapeDtypeStruct((S, SW), BF16),
                   jax.ShapeDtypeStruct((CW, CC), F32), jax.ShapeDtypeStruct((1, CC), F32),
                   jax.ShapeDtypeStruct((8, 128), F32), jax.ShapeDtypeStruct((1, SW), F32)],
        scratch_shapes=[pltpu.VMEM((SN, SW), F32), pltpu.VMEM((8 + CH, CC), F32), pltpu.VMEM((CH + 8, CC), F32),
                        pltpu.VMEM((1, SW), F32)],
        compiler_params=_cp(("arbitrary",)),
    )(xbc, xbc, dtraw, zs, y, htp, dossm, conv_w, conv_b, dtb, alog, dskx, gssm, tri, triu, e16)


def _mla_bwd(dqt, dk, dv, qlat, ckv, qg, kvg, wq, wk, wv, pos, invf):
    S = qlat.shape[0]
    tm = min(TQ, dqt.shape[3])
    per = dqt.shape[3] // tm

    def body(dq_ref, dk_ref, dv_ref, ql_ref, ckv_ref, qg_ref, kvg_ref, wq_ref, wk_ref, wv_ref, pos_ref, invf_ref,
             dql_ref, dckv_ref, dkr_ref, dwq_ref, dwk_ref, dwv_ref, dqg_ref, dkvg_ref):
        i = pl.program_id(0)

        @pl.when(i == 0)
        def _():
            dwq_ref[...] = jnp.zeros_like(dwq_ref)
            dwk_ref[...] = jnp.zeros_like(dwk_ref)
            dwv_ref[...] = jnp.zeros_like(dwv_ref)
            dqg_ref[...] = jnp.zeros_like(dqg_ref)
            dkvg_ref[...] = jnp.zeros_like(dkvg_ref)

        ang = pos_ref[...].astype(F32) * invf_ref[...]
        cs = jnp.cos(ang)
        sn = jnp.sin(ang)

        def rms_bwd(v, g, dn, dg_ref):
            r = lax.rsqrt(jnp.mean(v * v, axis=-1, keepdims=True) + RMS_EPS)
            vh = v * r
            dg_ref[...] += jnp.sum(dn * vh, axis=0, keepdims=True)
            dvh = dn * g
            return vh, r * (dvh - vh * jnp.mean(dvh * vh, axis=-1, keepdims=True))

        pieces = []
        for h in range(NH):
            dqh = dq_ref[h].T
            pieces.append(dqh[:, 0:NOPE] * SCALE)
            pieces.append(_rope_t(dqh[:, NOPE:HP], cs, sn) * SCALE)
        dqf = jnp.concatenate(pieces, axis=1).astype(BF16)
        ql = ql_ref[...]
        g = qg_ref[...]
        dqn = _nt(dqf, wq_ref[...])
        qh, dql = rms_bwd(ql, g, dqn, dqg_ref)
        dwq_ref[...] += _tn((qh * g).astype(BF16), dqf)
        dql_ref[...] = dql.astype(BF16)

        dkn_p = jnp.concatenate([dk_ref[h, :, 0:NOPE] for h in range(NH)], axis=1).astype(BF16)
        dvf = jnp.concatenate([dv_ref[h] for h in range(NH)], axis=1).astype(BF16)
        dkr = dk_ref[0, :, NOPE:HP]
        for h in range(1, NH):
            dkr = dkr + dk_ref[h, :, NOPE:HP]
        lane = lax.broadcasted_iota(jnp.int32, dkr.shape, 1)
        dkr_ref[...] = jnp.where(lane < ROPE, _rope_t(dkr, cs, sn), 0.0).astype(BF16)
        cv = ckv_ref[...]
        gk = kvg_ref[...]
        dkn = _nt(dkn_p, wk_ref[...]) + _nt(dvf, wv_ref[...])
        kh, dckv = rms_bwd(cv, gk, dkn, dkvg_ref)
        knb = (kh * gk).astype(BF16)
        dwk_ref[...] += _tn(knb, dkn_p)
        dwv_ref[...] += _tn(knb, dvf)
        dckv_ref[...] = dckv.astype(BF16)

    row = lambda n: pl.BlockSpec((tm, n), lambda i: (i, 0))
    heads = lambda n: pl.BlockSpec((NH, tm, n), lambda i: (0, i, 0))
    return pl.pallas_call(
        body, name="mla_bwd", grid=(S // tm,),
        in_specs=[pl.BlockSpec((NH, None, HP, tm), lambda i: (0, i // per, 0, i % per)), heads(HP), heads(VD), row(QL), row(KVL), _const((1, QL)), _const((1, KVL)),
                  _const((QL, NH * HP)), _const((KVL, NH * NOPE)), _const((KVL, NH * VD)), row(1), _const((1, 128))],
        out_specs=[row(QL), row(KVL), row(128), _full((QL, NH * HP)), _full((KVL, NH * NOPE)), _full((KVL, NH * VD)),
                   _full((1, QL)), _full((1, KVL))],
        out_shape=[jax.ShapeDtypeStruct((S, QL), BF16), jax.ShapeDtypeStruct((S, KVL), BF16), jax.ShapeDtypeStruct((S, 128), BF16),
                   jax.ShapeDtypeStruct((QL, NH * HP), F32), jax.ShapeDtypeStruct((KVL, NH * NOPE), F32),
                   jax.ShapeDtypeStruct((KVL, NH * VD), F32), jax.ShapeDtypeStruct((1, QL), F32), jax.ShapeDtypeStruct((1, KVL), F32)],
        compiler_params=_cp(("arbitrary",)),
    )(dqt, dk, dv, qlat, ckv, qg, kvg, wq, wk, wv, pos, invf)


def _inproj_bwd(x, gx1, mod, win, dql, dckv, dza, dxbc, dzs, dkr, ddt):
    S = x.shape[0]
    tm = min(TM, S)

    def body(x_ref, gx1_ref, mod_ref, win_ref, dql_ref, dckv_ref, dza_ref, dxbc_ref, dzs_ref, dkr_ref, ddt_ref,
             gx_ref, dw_ref, vec_ref):
        i = pl.program_id(0)

        @pl.when(i == 0)
        def _():
            dw_ref[...] = jnp.zeros_like(dw_ref)
            vec_ref[...] = jnp.zeros_like(vec_ref)

        shift = mod_ref[0:1, 0:D]
        scale = mod_ref[0:1, D:2 * D]
        xv = x_ref[...]
        ut = (xv * (1.0 + scale) + shift).T.astype(BF16)
        pieces = (dql_ref, dckv_ref, dza_ref, dxbc_ref, dzs_ref, dkr_ref, ddt_ref)
        du = jnp.zeros((tm, D), F32)
        lo = 0
        for p_ref in pieces:
            n = p_ref.shape[1]
            dp = p_ref[...]
            du = du + _nt(dp, win_ref[:, lo:lo + n])
            dw_ref[:, lo:lo + n] += _mm(ut, dp)
            lo += n
        vec_ref[0:1, :] += jnp.sum(du, axis=0, keepdims=True)
        vec_ref[1:2, :] += jnp.sum(du * xv, axis=0, keepdims=True)
        gx_ref[...] = gx1_ref[...] + du * (1.0 + scale)

    row = lambda n: pl.BlockSpec((tm, n), lambda i: (i, 0))
    return pl.pallas_call(
        body, name="inproj_bwd", grid=(S // tm,),
        in_specs=[row(D), row(D), _const((8, 3 * D)), _const((D, IN_P)), row(QL), row(KVL), row(D), row(CC), row(D),
                  row(128), row(128)],
        out_specs=[row(D), pl.BlockSpec((D, IN_P), lambda i: (0, 0), pipeline_mode=pl.Buffered(1)), _full((8, D))],
        out_shape=[jax.ShapeDtypeStruct((S, D), F32), jax.ShapeDtypeStruct((D, IN_P), F32), jax.ShapeDtypeStruct((8, D), F32)],
        compiler_params=_cp(("arbitrary",)),
    )(x, gx1, mod, win, dql, dckv, dza, dxbc, dzs, dkr, ddt)


def _ada_bwd(ccol, dmod):
    w = 3 * D // 4

    def body(c_ref, d_ref, o_ref):
        o_ref[...] = c_ref[...] * d_ref[...]

    return pl.pallas_call(
        body, name="ada_bwd", grid=(4,),
        in_specs=[_full((D, 1)), pl.BlockSpec((1, w), lambda k: (0, k))],
        out_specs=pl.BlockSpec((None, D, w), lambda k: (k, 0, 0)),
        out_shape=jax.ShapeDtypeStruct((4, D, w), F32),
        compiler_params=_cp(("arbitrary",)),
    )(ccol, dmod)


def _adamw(parts, w, m, v, rows_tile):
    rows = w.shape[0]

    def body(p_ref, w_ref, m_ref, v_ref, g_ref, d_ref, nm_ref, nv_ref):
        g = p_ref[0].astype(F32)
        for s in range(1, 8):
            g = g + p_ref[s].astype(F32)
        g_ref[...] = g
        nm = B1 * m_ref[...] + (1.0 - B1) * g
        nv = B2 * v_ref[...] + (1.0 - B2) * (g * g)
        nm_ref[...] = nm
        nv_ref[...] = nv
        m_hat = nm / (1.0 - B1 ** STEP)
        v_hat = nv / (1.0 - B2 ** STEP)
        d_ref[...] = -LR * (m_hat / (jnp.sqrt(v_hat) + EPS) + WD * w_ref[...])

    row = pl.BlockSpec((rows_tile, 1024), lambda i: (i, 0))
    sd = jax.ShapeDtypeStruct((rows, 1024), F32)
    return pl.pallas_call(
        body, name="adamw_" + str(rows), grid=(rows // rows_tile,),
        in_specs=[pl.BlockSpec((8, rows_tile, 1024), lambda i: (0, i, 0)), row, row, row],
        out_specs=[row, row, row, row], out_shape=[sd, sd, sd, sd],
        compiler_params=_cp(("arbitrary",)),
    )(parts, w, m, v)


_SMALL = (("b_ada", 3 * D), ("conv_w", CW * CC // 4), ("conv_b", CC), ("ssm_norm_g", SW), ("ln_g", D), ("ln_b", D),
          ("q_norm_g", QL), ("kv_norm_g", KVL), ("dt_bias", SH), ("a_log", SH), ("d_skip", SH))


def _pack_big(ada, win, qb, kvb, out, lead):
    flat = [a.reshape(a.shape[:lead] + (-1, 1024)) for a in (ada, win, qb, kvb, out)]
    used = sum(f.shape[lead] for f in flat)
    pad = jnp.zeros(flat[0].shape[:lead] + (R_BIG - used, 1024), flat[0].dtype)
    return jnp.concatenate(flat + [pad], axis=lead)


def _unpack_big(p):
    out, r = [], 0
    for n in (R_ADA, R_IN, R_QB, R_KVB, R_OUT):
        out.append(p[..., r:r + n, :])
        r += n
    return out


def _pack_small(d, lead):
    flat = [d[name].reshape(d[name].shape[:lead] + (-1,)) for name, _ in _SMALL]
    used = sum(f.shape[lead] for f in flat)
    pad = jnp.zeros(flat[0].shape[:lead] + (R_SMALL * 1024 - used,), F32)
    return jnp.concatenate(flat + [pad], axis=lead).reshape(flat[0].shape[:lead] + (R_SMALL, 1024))


def _unpack_small(p):
    flat = p.reshape(-1)
    out, r = {}, 0
    for name, n in _SMALL:
        out[name] = flat[r:r + n]
        r += n
    return out


def _in_to_padded(w):
    z = lambda n: jnp.zeros((w.shape[0], n), w.dtype)
    return jnp.concatenate([w[:, 0:384], w[:, 384:640], w[:, 704:1728], w[:, 1728:3264], w[:, 3280:4304],
                            w[:, 640:704], z(64), w[:, 3264:3280], z(112)], axis=1)


def _in_from_padded(g):
    return jnp.concatenate([g[:, 0:384], g[:, 384:640], g[:, P_KR[0]:P_KR[0] + 64], g[:, 640:1664], g[:, 1664:3200],
                            g[:, P_DT[0]:P_DT[0] + 16], g[:, 3200:4224]], axis=1)


def kernel(x, c, positions, w_ada, b_ada, w_in, q_norm_g, w_qb, kv_norm_g, w_kvb, conv_w, conv_b, dt_bias, a_log, d_skip, ssm_norm_g, w_out, ln_g, ln_b, loss_target, m_w_ada, m_b_ada, m_w_in, m_q_norm_g, m_w_qb, m_kv_norm_g, m_w_kvb, m_conv_w, m_conv_b, m_dt_bias, m_a_log, m_d_skip, m_ssm_norm_g, m_w_out, m_ln_g, m_ln_b, v_w_ada, v_b_ada, v_w_in, v_q_norm_g, v_w_qb, v_kv_norm_g, v_w_kvb, v_conv_w, v_conv_b, v_dt_bias, v_a_log, v_d_skip, v_ssm_norm_g, v_w_out, v_ln_g, v_ln_b):
    S = x.shape[1]
    xv = x[0]
    tgt = loss_target[0]

    wb = _pack_big(w_ada[0], w_in[0], w_qb[0], w_kvb[0], w_out[0], 0).astype(BF16)
    ws = jnp.zeros((8, 512), F32).at[0:CW, 0:CC // 4].set(conv_w[0])
    gb, gsm = _gather_weights(wb, ws)
    f_ada, f_in, f_qb, f_kvb, f_out = _unpack_big(gb)
    cat1 = lambda f, r, n: jnp.concatenate([f[k].reshape(r, n) for k in range(4)], axis=1)
    wada = cat1(f_ada, D, 3 * D // 4)
    win = _in_to_padded(cat1(f_in, D, IN_W // 4))
    wqb = cat1(f_qb, QL, NH * QKD // 4).reshape(QL, NH, QKD)
    wq = jnp.concatenate([wqb, jnp.zeros((QL, NH, HP - QKD), BF16)], axis=2).reshape(QL, NH * HP)
    wkvb = cat1(f_kvb, KVL, NH * (NOPE + VD) // 4).reshape(KVL, NH, NOPE + VD)
    wk = wkvb[:, :, 0:NOPE].reshape(KVL, NH * NOPE)
    wv = wkvb[:, :, NOPE:].reshape(KVL, NH * VD)
    wout = f_out.reshape(MIX, D)
    cwf = jnp.concatenate([gsm[k, 0:CW, 0:CC // 4] for k in range(4)], axis=1)

    half = ROPE // 2
    invf = 1.0 / (ROPE_THETA ** (jnp.arange(half, dtype=F32) / half))
    invf = jnp.concatenate([invf, invf, jnp.zeros((128 - ROPE,), F32)]).reshape(1, 128)
    pos = positions.reshape(S, 1)
    pad128 = lambda a: jnp.concatenate([a.reshape(1, SH), jnp.zeros((1, 128 - SH), F32)], axis=1)
    dtb, alog = pad128(dt_bias), pad128(a_log)
    dskx = jnp.repeat(d_skip.reshape(SH), SP).reshape(1, SW)

    mod = _ada(jnp.broadcast_to(c, (8, D)), wada, b_ada)
    qlat, ckv, za, xbc, zs, dtraw, q, k, v, kt, vt = _inproj(xv, mod, win, q_norm_g, kv_norm_g, wq, wk, wv, wk.T, wv.T, pos, invf)
    o, lse = _attn_fwd(q, k, vt)
    y, htp, ossm = _ssd_fwd(xbc, dtraw, zs, cwf, conv_b, dtb, alog, dskx, ssm_norm_g)
    gx1, do, dza, dossm, delta, dwout, vec_o = _outproj(o, za, ossm, xv, tgt, wout, mod, ln_g, ln_b)
    loss = lax.psum(0.5 / D * jnp.sum(vec_o[0]), ("x", "y", "c"))

    dk, dv, dq = _attn_bwd(q, k, kt, v, do, lse, delta)
    dxbc, ddt, dzs, dcw, dcb, dvec, dgssm = _ssd_bwd(xbc, dtraw, zs, y, htp, dossm, cwf, conv_b, dtb, alog, dskx, ssm_norm_g)
    dql, dckv, dkr, dwq, dwk, dwv, dqg, dkvg = _mla_bwd(dq, dk, dv, qlat, ckv, q_norm_g, kv_norm_g, wq, wk, wv, pos, invf)
    gx, dwin, vec_i = _inproj_bwd(xv, gx1, mod, win, dql, dckv, dza, dxbc, dzs, dkr, ddt)
    dmod = jnp.concatenate([vec_i[0:1], vec_i[1:2], vec_o[3:4]], axis=1)
    g_ada = _ada_bwd(c.reshape(D, 1), dmod)

    cols = lambda g: g.reshape(g.shape[0], 4, g.shape[1] // 4).transpose(1, 0, 2)
    g_in = cols(_in_from_padded(dwin))
    g_qb = cols(dwq.reshape(QL, NH, HP)[:, :, 0:QKD].reshape(QL, NH * QKD))
    g_kvb = cols(jnp.concatenate([dwk.reshape(KVL, NH, NOPE), dwv.reshape(KVL, NH, VD)], axis=2).reshape(KVL, NH * (NOPE + VD)))
    g_out = dwout.reshape(4, MIX // 4, D)
    gbig = _pack_big(g_ada, g_in, g_qb, g_kvb, g_out, 1).astype(BF16)
    small = {"b_ada": dmod, "conv_b": dcb, "ssm_norm_g": dgssm, "ln_g": vec_o[1:2], "ln_b": vec_o[2:3],
             "q_norm_g": dqg, "kv_norm_g": dkvg, "dt_bias": dvec[0:1, 0:SH], "a_log": dvec[1:2, 0:SH], "d_skip": dvec[2:3, 0:SH]}
    small = {n: jnp.broadcast_to(a.reshape(1, -1), (4, a.size)) for n, a in small.items()}
    small["conv_w"] = cols(dcw).reshape(4, CW * CC // 4)
    gsmall = _pack_small(small, 1)

    rb, rs = _exchange_grads(gbig, gsmall)
    wbig = _pack_big(w_ada[0], w_in[0], w_qb[0], w_kvb[0], w_out[0], 0)
    mbig = _pack_big(m_w_ada[0], m_w_in[0], m_w_qb[0], m_w_kvb[0], m_w_out[0], 0)
    vbig = _pack_big(v_w_ada[0], v_w_in[0], v_w_qb[0], v_w_kvb[0], v_w_out[0], 0)
    big = _adamw(rb, wbig, mbig, vbig, R_TILE)
    wsm = _pack_small(dict(b_ada=b_ada, conv_w=conv_w, conv_b=conv_b, ssm_norm_g=ssm_norm_g, ln_g=ln_g, ln_b=ln_b,
                           q_norm_g=q_norm_g, kv_norm_g=kv_norm_g, dt_bias=dt_bias, a_log=a_log, d_skip=d_skip), 0)
    msm = _pack_small(dict(b_ada=m_b_ada, conv_w=m_conv_w, conv_b=m_conv_b, ssm_norm_g=m_ssm_norm_g, ln_g=m_ln_g, ln_b=m_ln_b,
                           q_norm_g=m_q_norm_g, kv_norm_g=m_kv_norm_g, dt_bias=m_dt_bias, a_log=m_a_log, d_skip=m_d_skip), 0)
    vsm = _pack_small(dict(b_ada=v_b_ada, conv_w=v_conv_w, conv_b=v_conv_b, ssm_norm_g=v_ssm_norm_g, ln_g=v_ln_g, ln_b=v_ln_b,
                           q_norm_g=v_q_norm_g, kv_norm_g=v_kv_norm_g, dt_bias=v_dt_bias, a_log=v_a_log, d_skip=v_d_skip), 0)
    sm = _adamw(rs, wsm, msm, vsm, R_SMALL)

    order = ["w_ada", "b_ada", "w_in", "q_norm_g", "w_qb", "kv_norm_g", "w_kvb", "conv_w", "conv_b", "dt_bias", "a_log",
             "d_skip", "ssm_norm_g", "w_out", "ln_g", "ln_b"]
    shapes = dict(w_ada=w_ada.shape, b_ada=b_ada.shape, w_in=w_in.shape, q_norm_g=q_norm_g.shape, w_qb=w_qb.shape,
                  kv_norm_g=kv_norm_g.shape, w_kvb=w_kvb.shape, conv_w=conv_w.shape, conv_b=conv_b.shape, dt_bias=dt_bias.shape,
                  a_log=a_log.shape, d_skip=d_skip.shape, ssm_norm_g=ssm_norm_g.shape, w_out=w_out.shape, ln_g=ln_g.shape,
                  ln_b=ln_b.shape)
    outs = []
    for kind in range(4):
        b_ada_, b_in_, b_qb_, b_kvb_, b_out_ = _unpack_big(big[kind])
        d = _unpack_small(sm[kind])
        d.update(w_ada=b_ada_, w_in=b_in_, w_qb=b_qb_, w_kvb=b_kvb_, w_out=b_out_)
        outs.extend(d[n].reshape(shapes[n]) for n in order)
    return (loss, gx.reshape(x.shape), *outs)
```

```python
import functools
import math

import numpy as np
import jax
import jax.numpy as jnp
from jax import lax
from jax.experimental import pallas as pl
from jax.experimental.pallas import tpu as pltpu

F32 = jnp.float32
BF16 = jnp.bfloat16
HIGHEST = lax.Precision.HIGHEST
MESH_ID = pl.DeviceIdType.MESH

D = 1024
NH = 8
NOPE = 128
ROPE = 64
VD = 128
VDP = 144
QKD = NOPE + ROPE
HP = 256
QL = 384
KVL = 256
ROPE_THETA = 10000.0
SH = 16
SP = 64
SG = 2
SN = 128
CW = 4
CH = 128
SW = SH * SP
CC = SW + 2 * SG * SN
GW = SW // SG
MIX = 2 * D
IN_W = 4304
ALPHA = 2.0 ** 0.25
RMS_EPS = 1e-6
LN_EPS = 1e-5
SCALE = QKD ** -0.5
LN2 = math.log(2.0)
QSCALE = SCALE / LN2
LR, B1, B2, EPS, WD, STEP = 0.001, 0.9, 0.999, 1e-08, 0.01, 10

P_Q = (0, 384)
P_KV = (384, 640)
P_ZA = (640, 1664)
P_XBC = (1664, 3200)
P_ZS = (3200, 4224)
P_KR = (4224, 4352)
P_DT = (4352, 4480)
IN_P = 4480

R_SMALL = 16

TM = 256
TQ = 512
TQF = 1024
VMEM_LIMIT = 56 * 1024 * 1024


def _cp(sem=None):
    return pltpu.CompilerParams(dimension_semantics=sem, vmem_limit_bytes=VMEM_LIMIT)


def _mm(a, b):
    return jnp.dot(a, b, preferred_element_type=F32)


def _nt(a, b):
    return lax.dot_general(a, b, (((1,), (1,)), ((), ())), preferred_element_type=F32)


def _tn(a, b):
    return lax.dot_general(a, b, (((0,), (0,)), ((), ())), preferred_element_type=F32)


def _mm_hi(a, b):
    return jnp.dot(a, b, precision=HIGHEST, preferred_element_type=F32)


def _nt_hi(a, b):
    return lax.dot_general(a, b, (((1,), (1,)), ((), ())), precision=HIGHEST, preferred_element_type=F32)


def _sigmoid(z):
    return 1.0 / (1.0 + jnp.exp(-z))


def _softplus(z):
    return jnp.maximum(z, 0.0) + jnp.log1p(jnp.exp(-jnp.abs(z)))


def _rope(t, cs, sn):
    lane = lax.broadcasted_iota(jnp.int32, t.shape, 1)
    rot = jnp.where(lane < ROPE // 2, -pltpu.roll(t, 128 - ROPE // 2, 1), pltpu.roll(t, ROPE // 2, 1))
    return t * cs + rot * sn


def _rope_t(t, cs, sn):
    lane = lax.broadcasted_iota(jnp.int32, t.shape, 1)
    y = t * sn
    rot = jnp.where(lane < ROPE // 2, -pltpu.roll(y, 128 - ROPE // 2, 1), pltpu.roll(y, ROPE // 2, 1))
    return t * cs - rot


def _full(shape):
    n = len(shape)
    return pl.BlockSpec(shape, lambda *_: (0,) * n)


def _const(shape):
    n = len(shape)
    return pl.BlockSpec(shape, lambda *_: (0,) * n, pipeline_mode=pl.Buffered(1))


def _gather_weights(shards):
    n = len(shards)

    def body(*refs):
        srcs, dsts = refs[:n], refs[n:2 * n]
        send_sems, recv_sems, local_sems = refs[2 * n:]
        x, y, c = lax.axis_index("x"), lax.axis_index("y"), lax.axis_index("c")
        me = 2 * x + y
        chips = [(1 - x, y), (x, 1 - y), (1 - x, 1 - y)]

        def copy(a, j, slot, to):
            return pltpu.make_async_remote_copy(
                src_ref=srcs[a], dst_ref=dsts[a].at[slot], send_sem=send_sems.at[a, j], recv_sem=recv_sems.at[a, j],
                device_id=to, device_id_type=MESH_ID)

        local = [pltpu.make_async_copy(srcs[a], dsts[a].at[me], local_sems.at[a]) for a in range(n)]
        for cp in local:
            cp.start()
        sends = [copy(a, j, me, (px, py, c)) for a in range(n) for j, (px, py) in enumerate(chips)]
        for cp in sends:
            cp.start()
        for a in range(n):
            for j, (px, py) in enumerate(chips):
                copy(a, j, 2 * px + py, (x, y, c)).wait_recv()
        for cp in sends:
            cp.wait_send()
        for cp in local:
            cp.wait()

    hbm = pl.BlockSpec(memory_space=pltpu.HBM)
    return pl.pallas_call(
        body, name="gather_weights",
        out_shape=tuple(jax.ShapeDtypeStruct((4,) + a.shape, a.dtype) for a in shards),
        in_specs=[hbm] * n, out_specs=tuple([hbm] * n),
        scratch_shapes=[pltpu.SemaphoreType.DMA((n, 3)), pltpu.SemaphoreType.DMA((n, 3)), pltpu.SemaphoreType.DMA((n,))],
    )(*shards)


def _exchange_grads(slabs):
    n = len(slabs)

    def body(*refs):
        srcs, dsts = refs[:n], refs[n:2 * n]
        send_sems, recv_sems, local_sems = refs[2 * n:]
        x, y, c = lax.axis_index("x"), lax.axis_index("y"), lax.axis_index("c")
        chip = 2 * x + y
        sibling = (x, y, 1 - c)
        chips = [(1 - x, y), (x, 1 - y), (1 - x, 1 - y)]

        def slot(px, py, pc):
            return 4 * px + 2 * py + pc

        def copy(a, k, src, s, to):
            return pltpu.make_async_remote_copy(
                src_ref=src, dst_ref=dsts[a].at[s], send_sem=send_sems.at[a, k], recv_sem=recv_sems.at[a, k],
                device_id=to, device_id_type=MESH_ID)

        mine = slot(x, y, c)
        local = [pltpu.make_async_copy(srcs[a].at[chip], dsts[a].at[mine], local_sems.at[a]) for a in range(n)]
        for cp in local:
            cp.start()
        first = []
        for a in range(n):
            first.append(copy(a, 0, srcs[a].at[chip], mine, sibling))
            for j, (px, py) in enumerate(chips):
                first.append(copy(a, 1 + j, srcs[a].at[2 * px + py], mine, (px, py, c)))
        for cp in first:
            cp.start()
        passed = []
        for a in range(n):
            for j, (px, py) in enumerate(chips):
                s = slot(px, py, c)
                copy(a, 1 + j, srcs[a].at[chip], s, (x, y, c)).wait_recv()
                fwd = copy(a, 4 + j, dsts[a].at[s], s, sibling)
                fwd.start()
                passed.append(fwd)
        for a in range(n):
            copy(a, 0, srcs[a].at[chip], slot(x, y, 1 - c), (x, y, c)).wait_recv()
            for j, (px, py) in enumerate(chips):
                copy(a, 4 + j, srcs[a].at[chip], slot(px, py, 1 - c), (x, y, c)).wait_recv()
        for cp in first + passed:
            cp.wait_send()
        for cp in local:
            cp.wait()

    hbm = pl.BlockSpec(memory_space=pltpu.HBM)
    return pl.pallas_call(
        body, name="exchange_grads",
        out_shape=tuple(jax.ShapeDtypeStruct((8,) + a.shape[1:], a.dtype) for a in slabs),
        in_specs=[hbm] * n, out_specs=tuple([hbm] * n),
        scratch_shapes=[pltpu.SemaphoreType.DMA((n, 7)), pltpu.SemaphoreType.DMA((n, 7)), pltpu.SemaphoreType.DMA((n,))],
    )(*slabs)


def _ada(c8, w_ada, b_ada):
    def body(c_ref, w_ref, b_ref, o_ref):
        o_ref[...] = _mm(c_ref[...].astype(BF16), w_ref[...]) + b_ref[...]

    return pl.pallas_call(body, name="ada", out_shape=jax.ShapeDtypeStruct((8, 3 * D), F32),
                          compiler_params=_cp())(c8, w_ada, b_ada)


def _inproj(x, mod, win, qg, kvg, wq, wk, wv, wkt, wvt, pos, invf):
    S = x.shape[0]
    tm = min(TM, S)

    def body(x_ref, mod_ref, win_ref, qg_ref, kvg_ref, wq_ref, wk_ref, wv_ref, wkt_ref, wvt_ref, pos_ref, invf_ref,
             qlat_ref, ckv_ref, za_ref, xbc_ref, zs_ref, dt_ref, q_ref, k_ref, v_ref, kt_ref, vt_ref):
        shift = mod_ref[0:1, 0:D]
        scale = mod_ref[0:1, D:2 * D]
        u = (x_ref[...] * (1.0 + scale) + shift).astype(BF16)

        def proj(p):
            return _mm(u, win_ref[:, p[0]:p[1]])

        ql = proj(P_Q)
        ckv = proj(P_KV)
        qlat_ref[...] = ql
        ckv_ref[...] = ckv
        za_ref[...] = proj(P_ZA)
        xbc_ref[...] = proj(P_XBC)
        zs_ref[...] = proj(P_ZS)
        dt_ref[...] = proj(P_DT)
        kr = proj(P_KR)

        ang = pos_ref[...].astype(F32) * invf_ref[...]
        cs = jnp.cos(ang)
        sn = jnp.sin(ang)

        rq = lax.rsqrt(jnp.mean(ql * ql, axis=-1, keepdims=True) + RMS_EPS)
        qn = (ql * rq * qg_ref[...]).astype(BF16)
        for h in range(NH):
            qh = _mm(qn, wq_ref[:, h * HP:(h + 1) * HP])
            q_ref[h, :, 0:NOPE] = (qh[:, 0:NOPE] * QSCALE).astype(BF16)
            q_ref[h, :, NOPE:HP] = (_rope(qh[:, NOPE:HP], cs, sn) * QSCALE).astype(BF16)

        rk = lax.rsqrt(jnp.mean(ckv * ckv, axis=-1, keepdims=True) + RMS_EPS)
        kn = (ckv * rk * kvg_ref[...]).astype(BF16)
        knope = _mm(kn, wk_ref[...])
        vall = _mm(kn, wv_ref[...])
        krf = _rope(kr, cs, sn)
        krr = krf.astype(BF16)
        krt = krf.T.astype(BF16)
        ones_rows = jnp.where(lax.broadcasted_iota(jnp.int32, (VDP - VD, tm), 0) == 0, 1.0, 0.0).astype(BF16)
        for h in range(NH):
            k_ref[h, :, 0:NOPE] = knope[:, h * NOPE:(h + 1) * NOPE].astype(BF16)
            k_ref[h, :, NOPE:HP] = krr
            v_ref[h] = vall[:, h * VD:(h + 1) * VD].astype(BF16)
            kt_ref[h, 0:NOPE, :] = _nt(wkt_ref[h * NOPE:(h + 1) * NOPE, :], kn).astype(BF16)
            kt_ref[h, NOPE:HP, :] = krt
            vt_ref[h, 0:VD, :] = _nt(wvt_ref[h * VD:(h + 1) * VD, :], kn).astype(BF16)
            vt_ref[h, VD:VDP, :] = ones_rows

    row = lambda n: pl.BlockSpec((tm, n), lambda i: (i, 0))
    heads = lambda n: pl.BlockSpec((NH, tm, n), lambda i: (0, i, 0))
    heads_t = lambda n: pl.BlockSpec((NH, None, n, tm), lambda i: (0, i, 0, 0))
    sd = lambda n: jax.ShapeDtypeStruct((S, n), F32)
    hd = lambda n: jax.ShapeDtypeStruct((NH, S, n), BF16)
    ht = lambda n: jax.ShapeDtypeStruct((NH, S // tm, n, tm), BF16)
    return pl.pallas_call(
        body, name="inproj", grid=(S // tm,),
        in_specs=[row(D), _const((8, 3 * D)), _const((D, IN_P)), _const((1, QL)), _const((1, KVL)),
                  _const((QL, NH * HP)), _const((KVL, NH * NOPE)), _const((KVL, NH * VD)),
                  _const((NH * NOPE, KVL)), _const((NH * VD, KVL)), row(1), _const((1, 128))],
        out_specs=[row(QL), row(KVL), row(D), row(CC), row(D), row(128), heads(HP), heads(HP), heads(VD),
                   heads_t(HP), heads_t(VDP)],
        out_shape=[sd(QL), sd(KVL), sd(D), sd(CC), sd(D), sd(128), hd(HP), hd(HP), hd(VD), ht(HP), ht(VDP)],
        compiler_params=_cp(("arbitrary",)),
    )(x, mod, win, qg, kvg, wq, wk, wv, wkt, wvt, pos, invf)


def _attn_fwd(q, k, vt):
    _, S, _ = q.shape
    tq = min(TQF, S)
    nq = S // tq
    half = tq // 2
    tb = vt.shape[3]
    nsb = half // tb

    def body(q_ref, k_ref, vt_ref, o_ref, lse_ref):
        i = pl.program_id(1)
        qb = q_ref[...]

        def scores(j, hb):
            off = pl.multiple_of(j * tq + hb * half, half)
            return _nt(k_ref[pl.ds(off, half), :], qb)

        def update(j, hb, s, carry):
            m, acc = carry
            m_new = jnp.maximum(m, jnp.max(s, axis=0, keepdims=True))
            a = jnp.exp2(m - m_new)
            pb = jnp.exp2(s - m_new).astype(BF16)
            acc = a * acc
            for sb in range(nsb):
                acc = acc + _mm(vt_ref[(2 * j + hb) * nsb + sb], pb[sb * tb:(sb + 1) * tb, :])
            return m_new, acc

        def trip(j, carry, masked):
            s = [scores(j, hb) for hb in range(2)]
            if masked:
                r = lax.broadcasted_iota(jnp.int32, s[0].shape, 0)
                cidx = lax.broadcasted_iota(jnp.int32, s[0].shape, 1)
                s = [jnp.where(cidx >= r + hb * half, s[hb], -1e30) for hb in range(2)]
            for hb in range(2):
                carry = update(j, hb, s[hb], carry)
            return carry

        init = (jnp.full((1, tq), -1e30, F32), jnp.zeros((VDP, tq), F32))
        carry = lax.fori_loop(0, i, lambda j, cr: trip(j, cr, False), init)
        m, acc = trip(i, carry, True)
        l = acc[VD:VD + 1, :]
        o_ref[...] = (acc[0:VD, :] / l).T
        lse_ref[...] = m + jnp.log2(l)

    return pl.pallas_call(
        body, name="attn_fwd", grid=(NH, nq),
        in_specs=[pl.BlockSpec((None, tq, HP), lambda h, i: (h, i, 0)),
                  pl.BlockSpec((None, S, HP), lambda h, i: (h, 0, 0)),
                  pl.BlockSpec((None, S // tb, VDP, tb), lambda h, i: (h, 0, 0, 0))],
        out_specs=[pl.BlockSpec((tq, VD), lambda h, i: (i, h)),
                   pl.BlockSpec((None, None, 1, tq), lambda h, i: (h, i, 0, 0))],
        out_shape=[jax.ShapeDtypeStruct((S, NH * VD), F32), jax.ShapeDtypeStruct((NH, nq, 1, tq), F32)],
        compiler_params=_cp(("arbitrary", "arbitrary")),
    )(q, k, vt)


def _ssd_consts():
    tri = np.tril(np.ones((CH, CH), np.float32))
    e16 = np.zeros((128, SW), np.float32)
    for h in range(SH):
        e16[h, h * SP:(h + 1) * SP] = 1.0
    return jnp.asarray(tri), jnp.asarray(tri.T.copy()), jnp.asarray(e16)


def _ssd_chunk_fwd_common(xbc_ref, halo_ref, dtraw_ref, cw_ref, cb_ref, dtb_ref, alog_ref, tri_ref, e16_ref, ext, first):
    ext[0:8, :] = jnp.where(first, 0.0, halo_ref[...])
    ext[8:8 + CH, :] = xbc_ref[...]
    cw = cw_ref[...]
    xc = cb_ref[...] + cw[0:1, :] * ext[5:5 + CH, :]
    for kk in range(1, CW):
        xc = xc + cw[kk:kk + 1, :] * ext[5 + kk:5 + kk + CH, :]
    sact = _sigmoid(xc)
    act = xc * sact
    lane = lax.broadcasted_iota(jnp.int32, (1, 128), 1)
    arow = jnp.where(lane < SH, -jnp.exp(alog_ref[...]), 0.0)
    dtpre = dtraw_ref[...] + dtb_ref[...]
    dt = _softplus(dtpre)
    a = dt * arow
    cum = _mm_hi(tri_ref[...], a)
    cumx = _mm_hi(cum, e16_ref[...])
    dtx = _mm_hi(dt, e16_ref[...])
    return xc, sact, act, arow, dtpre, dt, cum, cumx, dtx


def _ssd_fwd(xbc, dtraw, zs, conv_w, conv_b, dtb, alog, dskx, gssm):
    S = xbc.shape[0]
    nc = S // CH
    tri, _, e16 = _ssd_consts()

    def body(xbc_ref, halo_ref, dtraw_ref, zs_ref, cw_ref, cb_ref, dtb_ref, alog_ref, dsk_ref, g_ref, tri_ref, e16_ref,
             y_ref, htp_ref, o_ref, ht, ext):
        i = pl.program_id(0)

        @pl.when(i == 0)
        def _():
            ht[...] = jnp.zeros_like(ht)

        xc, sact, act, arow, dtpre, dt, cum, cumx, dtx = _ssd_chunk_fwd_common(
            xbc_ref, halo_ref, dtraw_ref, cw_ref, cb_ref, dtb_ref, alog_ref, tri_ref, e16_ref, ext, i == 0)
        cum_t = cum.T
        xs = act[:, 0:SW]
        lastx = cumx[CH - 1:CH, :]
        xh = xs * dtx
        eexp = jnp.exp(cumx)
        dte = jnp.exp(lastx - cumx)
        cdx = jnp.exp(lastx)
        htp = ht[...]
        htp_ref[...] = htp
        xw = (xh * dte).astype(BF16)
        xb = xh.astype(BF16)
        trim = tri_ref[...] > 0.5
        lane = lax.broadcasted_iota(jnp.int32, (CH, 128), 1)
        parts = []
        for g in range(SG):
            gl = slice(g * GW, (g + 1) * GW)
            bg = act[:, SW + g * SN:SW + (g + 1) * SN].astype(BF16)
            cg = act[:, SW + SG * SN + g * SN:SW + SG * SN + (g + 1) * SN].astype(BF16)
            cbm = _nt(cg, bg)
            yoff = eexp[:, gl] * _mm(cg, htp[:, gl].astype(BF16))
            ht[:, gl] = htp[:, gl] * cdx[:, gl] + _tn(bg, xw[:, gl])
            for pr in range(GW // 128):
                h0 = g * (SH // SG) + 2 * pr
                lo = g * GW + pr * 128
                xp = xb[:, lo:lo + 128]
                res = []
                for hh in (h0, h0 + 1):
                    seg = cum[:, hh:hh + 1] - cum_t[hh:hh + 1, :]
                    mh = jnp.where(trim, cbm * jnp.exp(seg), 0.0).astype(BF16)
                    res.append(_mm(mh, xp))
                parts.append(jnp.where(lane < SP, res[0], res[1]) + yoff[:, pr * 128:(pr + 1) * 128])
        y = jnp.concatenate(parts, axis=1) + xs * dsk_ref[...]
        y_ref[...] = y
        z = zs_ref[...]
        hf = y * (z * _sigmoid(z))
        outs = []
        for g in range(SG):
            hg = hf[:, g * GW:(g + 1) * GW]
            rs = lax.rsqrt(jnp.mean(hg * hg, axis=-1, keepdims=True) + RMS_EPS)
            outs.append(hg * rs)
        o_ref[...] = (jnp.concatenate(outs, axis=1) * g_ref[...]).astype(BF16)

    row = lambda n: pl.BlockSpec((CH, n), lambda i: (i, 0))
    return pl.pallas_call(
        body, name="ssd_fwd", grid=(nc,),
        in_specs=[row(CC), pl.BlockSpec((8, CC), lambda i: (jnp.maximum(i * (CH // 8) - 1, 0), 0)), row(128), row(SW),
                  _const((CW, CC)), _const((1, CC)), _const((1, 128)), _const((1, 128)), _const((1, SW)), _const((1, SW)),
                  _const((CH, CH)), _const((128, SW))],
        out_specs=[row(SW), pl.BlockSpec((None, SN, SW), lambda i: (i, 0, 0)), row(SW)],
        out_shape=[jax.ShapeDtypeStruct((S, SW), F32), jax.ShapeDtypeStruct((nc, SN, SW), F32),
                   jax.ShapeDtypeStruct((S, SW), BF16)],
        scratch_shapes=[pltpu.VMEM((SN, SW), F32), pltpu.VMEM((8 + CH, CC), F32)],
        compiler_params=_cp(("arbitrary",)),
    )(xbc, xbc, dtraw, zs, conv_w, conv_b, dtb, alog, dskx, gssm, tri, e16)


def _outproj(o, za, ossm, x, tgt, wout, mod, ln_g, ln_b):
    S = x.shape[0]
    tm = min(TM, S)

    def body(o_ref, za_ref, os_ref, x_ref, t_ref, w_ref, mod_ref, g_ref, b_ref,
             gx_ref, do_ref, dza_ref, dos_ref, delta_ref, dw_ref, vec_ref):
        i = pl.program_id(0)

        @pl.when(i == 0)
        def _():
            dw_ref[...] = jnp.zeros_like(dw_ref)
            vec_ref[...] = jnp.zeros_like(vec_ref)

        gate = mod_ref[0:1, 2 * D:3 * D]
        ov = o_ref[...]
        z = za_ref[...]
        sz = _sigmoid(z)
        silz = z * sz
        a = (ov * silz).astype(BF16)
        osb = os_ref[...]
        mixed = _mm(a, w_ref[0:D, :]) + _mm(osb, w_ref[D:MIX, :])
        xv = x_ref[...]
        hres = ALPHA * xv + gate * mixed
        mu = jnp.mean(hres, axis=-1, keepdims=True)
        hc = hres - mu
        var = jnp.mean(hc * hc, axis=-1, keepdims=True)
        rstd = lax.rsqrt(var + LN_EPS)
        xhat = hc * rstd
        g = g_ref[...]
        yv = xhat * g + b_ref[...]
        err = yv - t_ref[...]
        dy = err * (1.0 / D)
        vec_ref[0:1, :] += jnp.sum(err * err, axis=0, keepdims=True)
        vec_ref[1:2, :] += jnp.sum(dy * xhat, axis=0, keepdims=True)
        vec_ref[2:3, :] += jnp.sum(dy, axis=0, keepdims=True)
        dxh = dy * g
        dh = rstd * (dxh - jnp.mean(dxh, axis=-1, keepdims=True) - xhat * jnp.mean(dxh * xhat, axis=-1, keepdims=True))
        gx_ref[...] = ALPHA * dh
        vec_ref[3:4, :] += jnp.sum(dh * mixed, axis=0, keepdims=True)
        dmixed = (gate * dh).astype(BF16)
        dw_ref[0:D, :] += _tn(a, dmixed)
        dw_ref[D:MIX, :] += _tn(osb, dmixed)
        da = _nt(dmixed, w_ref[0:D, :])
        dos_ref[...] = _nt(dmixed, w_ref[D:MIX, :])
        dov = da * silz
        do_ref[...] = dov.astype(BF16)
        dza_ref[...] = (da * ov * (sz * (1.0 + z * (1.0 - sz)))).astype(BF16)
        pr = dov * ov
        for h in range(NH):
            delta_ref[h] = jnp.sum(pr[:, h * VD:(h + 1) * VD], axis=-1, keepdims=True)

    row = lambda n: pl.BlockSpec((tm, n), lambda i: (i, 0))
    return pl.pallas_call(
        body, name="outproj", grid=(S // tm,),
        in_specs=[row(D), row(D), row(D), row(D), row(D), _const((MIX, D)), _const((8, 3 * D)), _const((1, D)), _const((1, D))],
        out_specs=[row(D), row(D), row(D), row(D), pl.BlockSpec((NH, tm, 1), lambda i: (0, i, 0)),
                   _full((MIX, D)), _full((8, D))],
        out_shape=[jax.ShapeDtypeStruct((S, D), F32), jax.ShapeDtypeStruct((S, D), BF16), jax.ShapeDtypeStruct((S, D), BF16),
                   jax.ShapeDtypeStruct((S, D), F32), jax.ShapeDtypeStruct((NH, S, 1), F32),
                   jax.ShapeDtypeStruct((MIX, D), F32), jax.ShapeDtypeStruct((8, D), F32)],
        compiler_params=_cp(("arbitrary",)),
    )(o, za, ossm, x, tgt, wout, mod, ln_g, ln_b)


def _attn_bwd(q, k, kt, v, do, lse, delta):
    _, S, _ = q.shape
    tk = min(TQ, S // 2)
    nk = S // tk
    tq = 2 * tk
    nq = S // tq
    tb = kt.shape[3]
    nsb = tk // tb

    def body(k_ref, kt_ref, v_ref, q_ref, do_ref, lse_ref, dl_ref, dk_ref, dv_ref, dqt_ref):
        j = pl.program_id(1)
        kb = k_ref[...]
        vb = v_ref[...]

        @pl.when(j == 0)
        def _():
            dqt_ref[...] = jnp.zeros_like(dqt_ref)

        dk_ref[...] = jnp.zeros_like(dk_ref)
        dv_ref[...] = jnp.zeros_like(dv_ref)

        def step(i, masked):
            off = pl.multiple_of(i * tq, tq)
            qb = q_ref[pl.ds(off, tq), :]
            dob = do_ref[pl.ds(off, tq), :]
            pt = jnp.exp2(_nt(kb, qb) - lse_ref[i])
            if masked:
                r = lax.broadcasted_iota(jnp.int32, pt.shape, 0)
                cidx = lax.broadcasted_iota(jnp.int32, pt.shape, 1)
                pt = jnp.where(i * tq + cidx >= j * tk + r, pt, 0.0)
            dv_ref[...] += _mm(pt.astype(BF16), dob)
            dsb = (pt * (_nt(vb, dob) - dl_ref[i])).astype(BF16)
            dk_ref[...] += _mm(dsb, qb)
            acc = dqt_ref[i]
            for sb in range(nsb):
                acc = acc + _mm(kt_ref[sb], dsb[sb * tb:(sb + 1) * tb, :])
            dqt_ref[i] = acc

        first = j // 2
        step(first, True)

        def loop_body(i, carry):
            step(i, False)
            return carry

        lax.fori_loop(first + 1, nq, loop_body, 0)
        dk_ref[...] = dk_ref[...] * LN2

    return pl.pallas_call(
        body, name="attn_bwd", grid=(NH, nk),
        in_specs=[pl.BlockSpec((None, tk, HP), lambda h, j: (h, j, 0)),
                  pl.BlockSpec((None, nsb, HP, tb), lambda h, j: (h, j, 0, 0)),
                  pl.BlockSpec((None, tk, VD), lambda h, j: (h, j, 0)),
                  pl.BlockSpec((None, S, HP), lambda h, j: (h, 0, 0), pipeline_mode=pl.Buffered(1)),
                  pl.BlockSpec((S, VD), lambda h, j: (0, h), pipeline_mode=pl.Buffered(1)),
                  pl.BlockSpec((None, nq, 1, tq), lambda h, j: (h, 0, 0, 0)),
                  pl.BlockSpec((None, nq, 1, tq), lambda h, j: (h, 0, 0, 0))],
        out_specs=[pl.BlockSpec((None, tk, HP), lambda h, j: (h, j, 0)),
                   pl.BlockSpec((None, tk, VD), lambda h, j: (h, j, 0)),
                   pl.BlockSpec((None, nq, HP, tq), lambda h, j: (h, 0, 0, 0), pipeline_mode=pl.Buffered(1))],
        out_shape=[jax.ShapeDtypeStruct((NH, S, HP), F32), jax.ShapeDtypeStruct((NH, S, VD), F32),
                   jax.ShapeDtypeStruct((NH, nq, HP, tq), F32)],
        compiler_params=_cp(("arbitrary", "arbitrary")),
    )(k, kt, v, q, do, lse.reshape(NH, nq, 1, tq), delta.reshape(NH, nq, 1, tq))


def _ssd_bwd(xbc, dtraw, zs, y, htp, dossm, conv_w, conv_b, dtb, alog, dskx, gssm):
    S = xbc.shape[0]
    nc = S // CH
    tri, triu, e16 = _ssd_consts()

    def body(xbc_ref, halo_ref, dtraw_ref, zs_ref, y_ref, htp_ref, dos_ref,
             cw_ref, cb_ref, dtb_ref, alog_ref, dsk_ref, g_ref, tri_ref, triu_ref, e16_ref,
             dxbc_ref, ddt_ref, dzs_ref, dcw_ref, dcb_ref, dvec_ref, dg_ref,
             dht, ext, dext, dskacc):
        r = pl.program_id(0)
        i = nc - 1 - r

        @pl.when(r == 0)
        def _():
            dht[...] = jnp.zeros_like(dht)
            dext[CH:CH + 8, :] = jnp.zeros((8, CC), F32)
            dskacc[...] = jnp.zeros_like(dskacc)
            dcw_ref[...] = jnp.zeros_like(dcw_ref)
            dcb_ref[...] = jnp.zeros_like(dcb_ref)
            dvec_ref[...] = jnp.zeros_like(dvec_ref)
            dg_ref[...] = jnp.zeros_like(dg_ref)

        xc, sact, act, arow, dtpre, dt, cum, cumx, dtx = _ssd_chunk_fwd_common(
            xbc_ref, halo_ref, dtraw_ref, cw_ref, cb_ref, dtb_ref, alog_ref, tri_ref, e16_ref, ext, i == 0)
        cum_t = cum.T
        xs = act[:, 0:SW]
        lastx = cumx[CH - 1:CH, :]
        xh = xs * dtx
        eexp = jnp.exp(cumx)
        dte = jnp.exp(lastx - cumx)
        cdx = jnp.exp(lastx)
        trim = tri_ref[...] > 0.5
        lane = lax.broadcasted_iota(jnp.int32, (CH, 128), 1)
        rowi = lax.broadcasted_iota(jnp.int32, (CH, 128), 0)

        yv = y_ref[...]
        z = zs_ref[...]
        sz = _sigmoid(z)
        silz = z * sz
        hf = yv * silz
        dn = dos_ref[...] * g_ref[...]
        dhf_parts, nrm_parts = [], []
        for g in range(SG):
            gl = slice(g * GW, (g + 1) * GW)
            hg = hf[:, gl]
            rs = lax.rsqrt(jnp.mean(hg * hg, axis=-1, keepdims=True) + RMS_EPS)
            ng = hg * rs
            dng = dn[:, gl]
            dhf_parts.append(rs * (dng - ng * jnp.mean(dng * ng, axis=-1, keepdims=True)))
            nrm_parts.append(ng)
        nrm = jnp.concatenate(nrm_parts, axis=1)
        dhf = jnp.concatenate(dhf_parts, axis=1)
        dg_ref[...] += jnp.sum(dos_ref[...] * nrm, axis=0, keepdims=True)
        dyv = dhf * silz
        dzs_ref[...] = (dhf * yv * (sz * (1.0 + z * (1.0 - sz)))).astype(BF16)
        dskacc[...] += jnp.sum(dyv * xs, axis=0, keepdims=True)
        dxs_skip = dyv * dsk_ref[...]

        dhtn = dht[...]
        hp = htp_ref[...]
        dlastx = jnp.sum(dhtn * hp, axis=0, keepdims=True) * cdx
        xb = xh.astype(BF16)
        xwf = xh * dte
        dcum = jnp.zeros((CH, 128), F32)
        dcum_t = jnp.zeros((128, CH), F32)
        dxh_parts, dcumx_parts, dlast_parts, db_parts, dc_parts = [], [], [], [], []
        for g in range(SG):
            gl = slice(g * GW, (g + 1) * GW)
            bg = act[:, SW + g * SN:SW + (g + 1) * SN].astype(BF16)
            cg = act[:, SW + SG * SN + g * SN:SW + SG * SN + (g + 1) * SN].astype(BF16)
            hpg = hp[:, gl].astype(BF16)
            dhn = dhtn[:, gl].astype(BF16)
            dyg = dyv[:, gl]
            yoff = eexp[:, gl] * _mm(cg, hpg)
            dz = (dyg * eexp[:, gl]).astype(BF16)
            dcg = _nt(dz, hpg)
            dht[:, gl] = dhtn[:, gl] * cdx[:, gl] + _tn(cg, dz)
            dcumx_g = dyg * yoff
            dbg = _nt(xwf[:, gl].astype(BF16), dhn)
            dxw = _mm(bg, dhn)
            ddte = dxw * xwf[:, gl]
            dcumx_parts.append(dcumx_g - ddte)
            dlast_parts.append(jnp.sum(ddte, axis=0, keepdims=True))
            dxh_g = dxw * dte[:, gl]
            cbm = _nt(cg, bg)
            dcb = jnp.zeros((CH, CH), F32)
            dxp_parts = []
            for pr in range(GW // 128):
                h0 = g * (SH // SG) + 2 * pr
                lo = g * GW + pr * 128
                xp = xb[:, lo:lo + 128]
                dyp = dyv[:, lo:lo + 128]
                dxp = jnp.zeros((CH, 128), F32)
                for idx, hh in enumerate((h0, h0 + 1)):
                    decay = jnp.where(trim, jnp.exp(cum[:, hh:hh + 1] - cum_t[hh:hh + 1, :]), 0.0)
                    mh = cbm * decay
                    keep = (lane < SP) if idx == 0 else (lane >= SP)
                    dym = jnp.where(keep, dyp, 0.0).astype(BF16)
                    dm = _nt(dym, xp)
                    dxp = dxp + _tn(mh.astype(BF16), dym)
                    gm = dm * mh
                    dcum = dcum + jnp.where(lane == hh, jnp.sum(gm, axis=1, keepdims=True), 0.0)
                    dcum_t = dcum_t - jnp.where(rowi == hh, jnp.sum(gm, axis=0, keepdims=True), 0.0)
                    dcb = dcb + dm * decay
                dxp_parts.append(dxp)
            dxh_parts.append(dxh_g + jnp.concatenate(dxp_parts, axis=1))
            dcbb = dcb.astype(BF16)
            dc_parts.append(dcg + _mm(dcbb, bg))
            db_parts.append(dbg + _tn(dcbb, cg))
        dxh = jnp.concatenate(dxh_parts, axis=1)
        dcumx = jnp.concatenate(dcumx_parts, axis=1)
        dlastx = dlastx + jnp.concatenate(dlast_parts, axis=1)
        e16 = e16_ref[...]
        dlast128 = _nt_hi(jnp.broadcast_to(dlastx, (8, SW)), e16)[0:1, :]
        dcum = dcum + dcum_t.T + _nt_hi(dcumx, e16) + jnp.where(rowi == CH - 1, dlast128, 0.0)
        da = _mm_hi(triu_ref[...], dcum)
        ddt = da * arow + _nt_hi(dxh * xs, e16)
        dvec_ref[1:2, :] += jnp.sum(da * dt, axis=0, keepdims=True)
        ddtraw = jnp.where(lane < SH, ddt * _sigmoid(dtpre), 0.0)
        dvec_ref[0:1, :] += jnp.sum(ddtraw, axis=0, keepdims=True)
        ddt_ref[...] = ddtraw.astype(BF16)
        dxs = dxs_skip + dxh * dtx
        dact = jnp.concatenate([dxs] + db_parts + dc_parts, axis=1)
        dxc = dact * (sact * (1.0 + xc * (1.0 - sact)))

        dcb_ref[...] += jnp.sum(dxc, axis=0, keepdims=True)
        for kk in range(CW):
            dcw_ref[kk:kk + 1, :] += jnp.sum(dxc * ext[5 + kk:5 + kk + CH, :], axis=0, keepdims=True)
        dext[0:CH, :] = dxc
        cw = cw_ref[...]
        dxr = cw[CW - 1:CW, :] * dxc
        for kk in range(CW - 1):
            dxr = dxr + cw[kk:kk + 1, :] * dext[CW - 1 - kk:CW - 1 - kk + CH, :]
        dxbc_ref[...] = dxr.astype(BF16)
        dext[CH:CH + 8, :] = dxc[0:8, :]

        @pl.when(r == nc - 1)
        def _():
            dvec_ref[1:2, :] = dvec_ref[1:2, :] * arow
            dvec_ref[2:3, :] = _nt_hi(jnp.broadcast_to(dskacc[...], (8, SW)), e16)[0:1, :]

    rev = lambda n: pl.BlockSpec((CH, n), lambda r: (nc - 1 - r, 0))
    return pl.pallas_call(
        body, name="ssd_bwd", grid=(nc,),
        in_specs=[rev(CC), pl.BlockSpec((8, CC), lambda r: (jnp.maximum((nc - 1 - r) * (CH // 8) - 1, 0), 0)),
                  rev(128), rev(SW), rev(SW), pl.BlockSpec((None, SN, SW), lambda r: (nc - 1 - r, 0, 0)), rev(SW),
                  _const((CW, CC)), _const((1, CC)), _const((1, 128)), _const((1, 128)), _const((1, SW)), _const((1, SW)),
                  _const((CH, CH)), _const((CH, CH)), _const((128, SW))],
        out_specs=[rev(CC), rev(128), rev(SW), _full((CW, CC)), _full((1, CC)), _full((8, 128)), _full((1, SW))],
        out_shape=[jax.ShapeDtypeStruct((S, CC), BF16), jax.ShapeDtypeStruct((S, 128), BF16), jax.ShapeDtypeStruct((S, SW), BF16),
                   jax.ShapeDtypeStruct((CW, CC), F32), jax.ShapeDtypeStruct((1, CC), F32),
                   jax.ShapeDtypeStruct((8, 128), F32), jax.ShapeDtypeStruct((1, SW), F32)],
        scratch_shapes=[pltpu.VMEM((SN, SW), F32), pltpu.VMEM((8 + CH, CC), F32), pltpu.VMEM((CH + 8, CC), F32),
                        pltpu.VMEM((1, SW), F32)],
        compiler_params=_cp(("arbitrary",)),
    )(xbc, xbc, dtraw, zs, y, htp, dossm, conv_w, conv_b, dtb, alog, dskx, gssm, tri, triu, e16)


def _mla_bwd(dqt, dk, dv, qlat, ckv, qg, kvg, wq, wk, wv, pos, invf):
    S = qlat.shape[0]
    tm = min(TQ, dqt.shape[3])
    per = dqt.shape[3] // tm

    def body(dq_ref, dk_ref, dv_ref, ql_ref, ckv_ref, qg_ref, kvg_ref, wq_ref, wk_ref, wv_ref, pos_ref, invf_ref,
             dql_ref, dckv_ref, dkr_ref, dwq_ref, dwk_ref, dwv_ref, dqg_ref, dkvg_ref):
        i = pl.program_id(0)

        @pl.when(i == 0)
        def _():
            dwq_ref[...] = jnp.zeros_like(dwq_ref)
            dwk_ref[...] = jnp.zeros_like(dwk_ref)
            dwv_ref[...] = jnp.zeros_like(dwv_ref)
            dqg_ref[...] = jnp.zeros_like(dqg_ref)
            dkvg_ref[...] = jnp.zeros_like(dkvg_ref)

        ang = pos_ref[...].astype(F32) * invf_ref[...]
        cs = jnp.cos(ang)
        sn = jnp.sin(ang)

        def rms_bwd(v, g, dn, dg_ref):
            r = lax.rsqrt(jnp.mean(v * v, axis=-1, keepdims=True) + RMS_EPS)
            vh = v * r
            dg_ref[...] += jnp.sum(dn * vh, axis=0, keepdims=True)
            dvh = dn * g
            return vh, r * (dvh - vh * jnp.mean(dvh * vh, axis=-1, keepdims=True))

        pieces = []
        for h in range(NH):
            dqh = dq_ref[h].T
            pieces.append(dqh[:, 0:NOPE] * SCALE)
            pieces.append(_rope_t(dqh[:, NOPE:HP], cs, sn) * SCALE)
        dqf = jnp.concatenate(pieces, axis=1).astype(BF16)
        ql = ql_ref[...]
        g = qg_ref[...]
        dqn = _nt(dqf, wq_ref[...])
        qh, dql = rms_bwd(ql, g, dqn, dqg_ref)
        dwq_ref[...] += _tn((qh * g).astype(BF16), dqf)
        dql_ref[...] = dql.astype(BF16)

        dkn_p = jnp.concatenate([dk_ref[h, :, 0:NOPE] for h in range(NH)], axis=1).astype(BF16)
        dvf = jnp.concatenate([dv_ref[h] for h in range(NH)], axis=1).astype(BF16)
        dkr = dk_ref[0, :, NOPE:HP]
        for h in range(1, NH):
            dkr = dkr + dk_ref[h, :, NOPE:HP]
        lane = lax.broadcasted_iota(jnp.int32, dkr.shape, 1)
        dkr_ref[...] = jnp.where(lane < ROPE, _rope_t(dkr, cs, sn), 0.0).astype(BF16)
        cv = ckv_ref[...]
        gk = kvg_ref[...]
        dkn = _nt(dkn_p, wk_ref[...]) + _nt(dvf, wv_ref[...])
        kh, dckv = rms_bwd(cv, gk, dkn, dkvg_ref)
        knb = (kh * gk).astype(BF16)
        dwk_ref[...] += _tn(knb, dkn_p)
        dwv_ref[...] += _tn(knb, dvf)
        dckv_ref[...] = dckv.astype(BF16)

    row = lambda n: pl.BlockSpec((tm, n), lambda i: (i, 0))
    heads = lambda n: pl.BlockSpec((NH, tm, n), lambda i: (0, i, 0))
    return pl.pallas_call(
        body, name="mla_bwd", grid=(S // tm,),
        in_specs=[pl.BlockSpec((NH, None, HP, tm), lambda i: (0, i // per, 0, i % per)), heads(HP), heads(VD), row(QL), row(KVL), _const((1, QL)), _const((1, KVL)),
                  _const((QL, NH * HP)), _const((KVL, NH * NOPE)), _const((KVL, NH * VD)), row(1), _const((1, 128))],
        out_specs=[row(QL), row(KVL), row(128), _full((QL, NH * HP)), _full((KVL, NH * NOPE)), _full((KVL, NH * VD)),
                   _full((1, QL)), _full((1, KVL))],
        out_shape=[jax.ShapeDtypeStruct((S, QL), BF16), jax.ShapeDtypeStruct((S, KVL), BF16), jax.ShapeDtypeStruct((S, 128), BF16),
                   jax.ShapeDtypeStruct((QL, NH * HP), F32), jax.ShapeDtypeStruct((KVL, NH * NOPE), F32),
                   jax.ShapeDtypeStruct((KVL, NH * VD), F32), jax.ShapeDtypeStruct((1, QL), F32), jax.ShapeDtypeStruct((1, KVL), F32)],
        compiler_params=_cp(("arbitrary",)),
    )(dqt, dk, dv, qlat, ckv, qg, kvg, wq, wk, wv, pos, invf)


def _inproj_bwd(x, gx1, mod, win, dql, dckv, dza, dxbc, dzs, dkr, ddt):
    S = x.shape[0]
    tm = min(TM, S)

    def body(x_ref, gx1_ref, mod_ref, win_ref, dql_ref, dckv_ref, dza_ref, dxbc_ref, dzs_ref, dkr_ref, ddt_ref,
             gx_ref, dw_ref, vec_ref):
        i = pl.program_id(0)

        @pl.when(i == 0)
        def _():
            dw_ref[...] = jnp.zeros_like(dw_ref)
            vec_ref[...] = jnp.zeros_like(vec_ref)

        shift = mod_ref[0:1, 0:D]
        scale = mod_ref[0:1, D:2 * D]
        xv = x_ref[...]
        ut = (xv * (1.0 + scale) + shift).T.astype(BF16)
        pieces = (dql_ref, dckv_ref, dza_ref, dxbc_ref, dzs_ref, dkr_ref, ddt_ref)
        du = jnp.zeros((tm, D), F32)
        lo = 0
        for p_ref in pieces:
            n = p_ref.shape[1]
            dp = p_ref[...]
            du = du + _nt(dp, win_ref[:, lo:lo + n])
            dw_ref[:, lo:lo + n] += _mm(ut, dp)
            lo += n
        vec_ref[0:1, :] += jnp.sum(du, axis=0, keepdims=True)
        vec_ref[1:2, :] += jnp.sum(du * xv, axis=0, keepdims=True)
        gx_ref[...] = gx1_ref[...] + du * (1.0 + scale)

    row = lambda n: pl.BlockSpec((tm, n), lambda i: (i, 0))
    return pl.pallas_call(
        body, name="inproj_bwd", grid=(S // tm,),
        in_specs=[row(D), row(D), _const((8, 3 * D)), _const((D, IN_P)), row(QL), row(KVL), row(D), row(CC), row(D),
                  row(128), row(128)],
        out_specs=[row(D), pl.BlockSpec((D, IN_P), lambda i: (0, 0), pipeline_mode=pl.Buffered(1)), _full((8, D))],
        out_shape=[jax.ShapeDtypeStruct((S, D), F32), jax.ShapeDtypeStruct((D, IN_P), F32), jax.ShapeDtypeStruct((8, D), F32)],
        compiler_params=_cp(("arbitrary",)),
    )(x, gx1, mod, win, dql, dckv, dza, dxbc, dzs, dkr, ddt)


def _ada_bwd(ccol, dmod):
    w = 3 * D // 4

    def body(c_ref, d_ref, o_ref):
        o_ref[...] = (c_ref[...] * d_ref[...]).astype(BF16)

    return pl.pallas_call(
        body, name="ada_bwd", grid=(4,),
        in_specs=[_full((D, 1)), pl.BlockSpec((1, w), lambda k: (0, k))],
        out_specs=pl.BlockSpec((None, D, w), lambda k: (k, 0, 0)),
        out_shape=jax.ShapeDtypeStruct((4, D, w), BF16),
        compiler_params=_cp(("arbitrary",)),
    )(ccol, dmod)


def _adamw(name, parts, w, m, v):
    rows, ncol = w.shape
    tr = min(rows, 128)

    def body(p_ref, w_ref, m_ref, v_ref, g_ref, d_ref, nm_ref, nv_ref):
        g = p_ref[0].astype(F32)
        for s in range(1, 8):
            g = g + p_ref[s].astype(F32)
        g_ref[...] = g
        nm = B1 * m_ref[...] + (1.0 - B1) * g
        nv = B2 * v_ref[...] + (1.0 - B2) * (g * g)
        nm_ref[...] = nm
        nv_ref[...] = nv
        m_hat = nm / (1.0 - B1 ** STEP)
        v_hat = nv / (1.0 - B2 ** STEP)
        d_ref[...] = -LR * (m_hat / (jnp.sqrt(v_hat) + EPS) + WD * w_ref[...])

    row = pl.BlockSpec((tr, ncol), lambda i: (i, 0))
    sd = jax.ShapeDtypeStruct((rows, ncol), F32)
    return pl.pallas_call(
        body, name="adamw_" + name, grid=(rows // tr,),
        in_specs=[pl.BlockSpec((8, tr, ncol), lambda i: (0, i, 0)), row, row, row],
        out_specs=[row, row, row, row], out_shape=[sd, sd, sd, sd],
        compiler_params=_cp(("arbitrary",)),
    )(parts, w, m, v)


_SMALL = (("b_ada", 3 * D), ("conv_w", CW * CC // 4), ("conv_b", CC), ("ssm_norm_g", SW), ("ln_g", D), ("ln_b", D),
          ("q_norm_g", QL), ("kv_norm_g", KVL), ("dt_bias", SH), ("a_log", SH), ("d_skip", SH))


def _pack_small(d, lead):
    flat = [d[name].reshape(d[name].shape[:lead] + (-1,)) for name, _ in _SMALL]
    used = sum(f.shape[lead] for f in flat)
    pad = jnp.zeros(flat[0].shape[:lead] + (R_SMALL * 1024 - used,), F32)
    return jnp.concatenate(flat + [pad], axis=lead).reshape(flat[0].shape[:lead] + (R_SMALL, 1024))


def _unpack_small(p):
    flat = p.reshape(-1)
    out, r = {}, 0
    for name, n in _SMALL:
        out[name] = flat[r:r + n]
        r += n
    return out


def _in_to_padded(w):
    z = lambda n: jnp.zeros((w.shape[0], n), w.dtype)
    return jnp.concatenate([w[:, 0:384], w[:, 384:640], w[:, 704:1728], w[:, 1728:3264], w[:, 3280:4304],
                            w[:, 640:704], z(64), w[:, 3264:3280], z(112)], axis=1)


def _in_from_padded(g):
    return jnp.concatenate([g[:, 0:384], g[:, 384:640], g[:, P_KR[0]:P_KR[0] + 64], g[:, 640:1664], g[:, 1664:3200],
                            g[:, P_DT[0]:P_DT[0] + 16], g[:, 3200:4224]], axis=1)


def kernel(x, c, positions, w_ada, b_ada, w_in, q_norm_g, w_qb, kv_norm_g, w_kvb, conv_w, conv_b, dt_bias, a_log, d_skip, ssm_norm_g, w_out, ln_g, ln_b, loss_target, m_w_ada, m_b_ada, m_w_in, m_q_norm_g, m_w_qb, m_kv_norm_g, m_w_kvb, m_conv_w, m_conv_b, m_dt_bias, m_a_log, m_d_skip, m_ssm_norm_g, m_w_out, m_ln_g, m_ln_b, v_w_ada, v_b_ada, v_w_in, v_q_norm_g, v_w_qb, v_kv_norm_g, v_w_kvb, v_conv_w, v_conv_b, v_dt_bias, v_a_log, v_d_skip, v_ssm_norm_g, v_w_out, v_ln_g, v_ln_b):
    S = x.shape[1]
    xv = x[0]
    tgt = loss_target[0]

    f_ada, f_in, f_qb, f_kvb, f_out, f_cw = _gather_weights(
        [w_ada[0].astype(BF16), w_in[0].astype(BF16), w_qb[0].astype(BF16), w_kvb[0].astype(BF16), w_out[0].astype(BF16),
         conv_w[0]])
    cat1 = lambda f: f.transpose(1, 0, 2).reshape(f.shape[1], 4 * f.shape[2])
    wada = cat1(f_ada)
    win = _in_to_padded(cat1(f_in))
    wqb = cat1(f_qb).reshape(QL, NH, QKD)
    wq = jnp.concatenate([wqb, jnp.zeros((QL, NH, HP - QKD), BF16)], axis=2).reshape(QL, NH * HP)
    wkvb = cat1(f_kvb).reshape(KVL, NH, NOPE + VD)
    wk = wkvb[:, :, 0:NOPE].reshape(KVL, NH * NOPE)
    wv = wkvb[:, :, NOPE:].reshape(KVL, NH * VD)
    wout = f_out.reshape(MIX, D)
    cwf = cat1(f_cw)

    half = ROPE // 2
    invf = 1.0 / (ROPE_THETA ** (jnp.arange(half, dtype=F32) / half))
    invf = jnp.concatenate([invf, invf, jnp.zeros((128 - ROPE,), F32)]).reshape(1, 128)
    pos = positions.reshape(S, 1)
    pad128 = lambda a: jnp.concatenate([a.reshape(1, SH), jnp.zeros((1, 128 - SH), F32)], axis=1)
    dtb, alog = pad128(dt_bias), pad128(a_log)
    dskx = jnp.repeat(d_skip.reshape(SH), SP).reshape(1, SW)

    mod = _ada(jnp.broadcast_to(c, (8, D)), wada, b_ada)
    qlat, ckv, za, xbc, zs, dtraw, q, k, v, kt, vt = _inproj(xv, mod, win, q_norm_g, kv_norm_g, wq, wk, wv, wk.T, wv.T, pos, invf)
    o, lse = _attn_fwd(q, k, vt)
    y, htp, ossm = _ssd_fwd(xbc, dtraw, zs, cwf, conv_b, dtb, alog, dskx, ssm_norm_g)
    gx1, do, dza, dossm, delta, dwout, vec_o = _outproj(o, za, ossm, xv, tgt, wout, mod, ln_g, ln_b)
    loss = lax.psum(0.5 / D * jnp.sum(vec_o[0]), ("x", "y", "c"))

    dk, dv, dq = _attn_bwd(q, k, kt, v, do, lse, delta)
    dxbc, ddt, dzs, dcw, dcb, dvec, dgssm = _ssd_bwd(xbc, dtraw, zs, y, htp, dossm, cwf, conv_b, dtb, alog, dskx, ssm_norm_g)
    dql, dckv, dkr, dwq, dwk, dwv, dqg, dkvg = _mla_bwd(dq, dk, dv, qlat, ckv, q_norm_g, kv_norm_g, wq, wk, wv, pos, invf)
    gx, dwin, vec_i = _inproj_bwd(xv, gx1, mod, win, dql, dckv, dza, dxbc, dzs, dkr, ddt)
    dmod = jnp.concatenate([vec_i[0:1], vec_i[1:2], vec_o[3:4]], axis=1)
    g_ada = _ada_bwd(c.reshape(D, 1), dmod)

    cols = lambda g: g.reshape(g.shape[0], 4, g.shape[1] // 4).transpose(1, 0, 2)
    g_in = cols(_in_from_padded(dwin)).astype(BF16)
    g_qb = cols(dwq.reshape(QL, NH, HP)[:, :, 0:QKD].reshape(QL, NH * QKD)).astype(BF16)
    g_kvb = cols(jnp.concatenate([dwk.reshape(KVL, NH, NOPE), dwv.reshape(KVL, NH, VD)], axis=2)
                 .reshape(KVL, NH * (NOPE + VD))).astype(BF16)
    g_out = dwout.reshape(4, MIX // 4, D).astype(BF16)
    small = {"b_ada": dmod, "conv_b": dcb, "ssm_norm_g": dgssm, "ln_g": vec_o[1:2], "ln_b": vec_o[2:3],
             "q_norm_g": dqg, "kv_norm_g": dkvg, "dt_bias": dvec[0:1, 0:SH], "a_log": dvec[1:2, 0:SH], "d_skip": dvec[2:3, 0:SH]}
    small = {n: jnp.broadcast_to(a.reshape(1, -1), (4, a.size)) for n, a in small.items()}
    small["conv_w"] = cols(dcw).reshape(4, CW * CC // 4)
    gsmall = _pack_small(small, 1)

    r_ada, r_in, r_qb, r_kvb, r_out, rs = _exchange_grads([g_ada, g_in, g_qb, g_kvb, g_out, gsmall])
    res = dict(w_ada=_adamw("w_ada", r_ada, w_ada[0], m_w_ada[0], v_w_ada[0]),
               w_in=_adamw("w_in", r_in, w_in[0], m_w_in[0], v_w_in[0]),
               w_qb=_adamw("w_qb", r_qb, w_qb[0], m_w_qb[0], v_w_qb[0]),
               w_kvb=_adamw("w_kvb", r_kvb, w_kvb[0], m_w_kvb[0], v_w_kvb[0]),
               w_out=_adamw("w_out", r_out, w_out[0], m_w_out[0], v_w_out[0]))
    wsm = _pack_small(dict(b_ada=b_ada, conv_w=conv_w, conv_b=conv_b, ssm_norm_g=ssm_norm_g, ln_g=ln_g, ln_b=ln_b,
                           q_norm_g=q_norm_g, kv_norm_g=kv_norm_g, dt_bias=dt_bias, a_log=a_log, d_skip=d_skip), 0)
    msm = _pack_small(dict(b_ada=m_b_ada, conv_w=m_conv_w, conv_b=m_conv_b, ssm_norm_g=m_ssm_norm_g, ln_g=m_ln_g, ln_b=m_ln_b,
                           q_norm_g=m_q_norm_g, kv_norm_g=m_kv_norm_g, dt_bias=m_dt_bias, a_log=m_a_log, d_skip=m_d_skip), 0)
    vsm = _pack_small(dict(b_ada=v_b_ada, conv_w=v_conv_w, conv_b=v_conv_b, ssm_norm_g=v_ssm_norm_g, ln_g=v_ln_g, ln_b=v_ln_b,
                           q_norm_g=v_q_norm_g, kv_norm_g=v_kv_norm_g, dt_bias=v_dt_bias, a_log=v_a_log, d_skip=v_d_skip), 0)
    sm = _adamw("small", rs, wsm, msm, vsm)

    order = ["w_ada", "b_ada", "w_in", "q_norm_g", "w_qb", "kv_norm_g", "w_kvb", "conv_w", "conv_b", "dt_bias", "a_log",
             "d_skip", "ssm_norm_g", "w_out", "ln_g", "ln_b"]
    shapes = dict(w_ada=w_ada.shape, b_ada=b_ada.shape, w_in=w_in.shape, q_norm_g=q_norm_g.shape, w_qb=w_qb.shape,
                  kv_norm_g=kv_norm_g.shape, w_kvb=w_kvb.shape, conv_w=conv_w.shape, conv_b=conv_b.shape, dt_bias=dt_bias.shape,
                  a_log=a_log.shape, d_skip=d_skip.shape, ssm_norm_g=ssm_norm_g.shape, w_out=w_out.shape, ln_g=ln_g.shape,
                  ln_b=ln_b.shape)
    outs = []
    for kind in range(4):
        d = _unpack_small(sm[kind])
        d.update({n: r[kind] for n, r in res.items()})
        outs.extend(d[n].reshape(shapes[n]) for n in order)
    return (loss, gx.reshape(x.shape), *outs)
```

```python
import functools
import math

import numpy as np
import jax
import jax.numpy as jnp
from jax import lax
from jax.experimental import pallas as pl
from jax.experimental.pallas import tpu as pltpu

F32 = jnp.float32
BF16 = jnp.bfloat16
HIGHEST = lax.Precision.HIGHEST
MESH_ID = pl.DeviceIdType.MESH

D = 1024
NH = 8
NOPE = 128
ROPE = 64
VD = 128
VDP = 144
QKD = NOPE + ROPE
HP = 256
QL = 384
KVL = 256
ROPE_THETA = 10000.0
SH = 16
SP = 64
SG = 2
SN = 128
CW = 4
CH = 128
SW = SH * SP
CC = SW + 2 * SG * SN
GW = SW // SG
MIX = 2 * D
IN_W = 4304
ALPHA = 2.0 ** 0.25
RMS_EPS = 1e-6
LN_EPS = 1e-5
SCALE = QKD ** -0.5
LN2 = math.log(2.0)
QSCALE = SCALE / LN2
LR, B1, B2, EPS, WD, STEP = 0.001, 0.9, 0.999, 1e-08, 0.01, 10

P_Q = (0, 384)
P_KV = (384, 640)
P_ZA = (640, 1664)
P_XBC = (1664, 3200)
P_ZS = (3200, 4224)
P_KR = (4224, 4352)
P_DT = (4352, 4480)
IN_P = 4480

R_SMALL = 16

TM = 256
TQ = 512
TQF = 1024
VMEM_LIMIT = 56 * 1024 * 1024


def _cp(sem=None):
    return pltpu.CompilerParams(dimension_semantics=sem, vmem_limit_bytes=VMEM_LIMIT)


def _mm(a, b):
    return jnp.dot(a, b, preferred_element_type=F32)


def _nt(a, b):
    return lax.dot_general(a, b, (((1,), (1,)), ((), ())), preferred_element_type=F32)


def _tn(a, b):
    return lax.dot_general(a, b, (((0,), (0,)), ((), ())), preferred_element_type=F32)


def _mm_hi(a, b):
    return jnp.dot(a, b, precision=HIGHEST, preferred_element_type=F32)


def _nt_hi(a, b):
    return lax.dot_general(a, b, (((1,), (1,)), ((), ())), precision=HIGHEST, preferred_element_type=F32)


def _sigmoid(z):
    return 1.0 / (1.0 + jnp.exp(-z))


def _softplus(z):
    return jnp.maximum(z, 0.0) + jnp.log1p(jnp.exp(-jnp.abs(z)))


def _rope(t, cs, sn):
    lane = lax.broadcasted_iota(jnp.int32, t.shape, 1)
    rot = jnp.where(lane < ROPE // 2, -pltpu.roll(t, 128 - ROPE // 2, 1), pltpu.roll(t, ROPE // 2, 1))
    return t * cs + rot * sn


def _rope_t(t, cs, sn):
    lane = lax.broadcasted_iota(jnp.int32, t.shape, 1)
    y = t * sn
    rot = jnp.where(lane < ROPE // 2, -pltpu.roll(y, 128 - ROPE // 2, 1), pltpu.roll(y, ROPE // 2, 1))
    return t * cs - rot


def _full(shape):
    n = len(shape)
    return pl.BlockSpec(shape, lambda *_: (0,) * n)


def _const(shape):
    n = len(shape)
    return pl.BlockSpec(shape, lambda *_: (0,) * n, pipeline_mode=pl.Buffered(1))


def _gather_weights(shards):
    n = len(shards)
    halves = [a.shape[0] // 2 for a in shards]

    def body(*refs):
        srcs, dsts = refs[:n], refs[n:2 * n]
        send_sems, recv_sems, local_sems = refs[2 * n:]
        x, y, c = lax.axis_index("x"), lax.axis_index("y"), lax.axis_index("c")
        me = 2 * x + y
        sibling = (x, y, 1 - c)
        chips = [(1 - x, y), (x, 1 - y), (1 - x, 1 - y)]

        def rows(a, pc):
            return pl.ds(pl.multiple_of(pc * halves[a], halves[a]), halves[a])

        def copy(a, k, src, slot, pc, to):
            return pltpu.make_async_remote_copy(
                src_ref=src, dst_ref=dsts[a].at[slot, rows(a, pc)], send_sem=send_sems.at[a, k],
                recv_sem=recv_sems.at[a, k], device_id=to, device_id_type=MESH_ID)

        local = [pltpu.make_async_copy(srcs[a], dsts[a].at[me], local_sems.at[a]) for a in range(n)]
        for cp in local:
            cp.start()
        sends = [copy(a, j, srcs[a].at[rows(a, c)], me, c, (px, py, c)) for a in range(n) for j, (px, py) in enumerate(chips)]
        for cp in sends:
            cp.start()
        passed = []
        for a in range(n):
            for j, (px, py) in enumerate(chips):
                k = 2 * px + py
                copy(a, j, srcs[a].at[rows(a, c)], k, c, (x, y, c)).wait_recv()
                fwd = copy(a, 3 + j, dsts[a].at[k, rows(a, c)], k, c, sibling)
                fwd.start()
                passed.append(fwd)
        for a in range(n):
            for j, (px, py) in enumerate(chips):
                copy(a, 3 + j, srcs[a].at[rows(a, c)], 2 * px + py, 1 - c, (x, y, c)).wait_recv()
        for cp in sends + passed:
            cp.wait_send()
        for cp in local:
            cp.wait()

    hbm = pl.BlockSpec(memory_space=pltpu.HBM)
    return pl.pallas_call(
        body, name="gather_weights",
        out_shape=tuple(jax.ShapeDtypeStruct((4,) + a.shape, a.dtype) for a in shards),
        in_specs=[hbm] * n, out_specs=tuple([hbm] * n),
        scratch_shapes=[pltpu.SemaphoreType.DMA((n, 6)), pltpu.SemaphoreType.DMA((n, 6)), pltpu.SemaphoreType.DMA((n,))],
    )(*shards)


def _exchange(name, slabs):
    n = len(slabs)

    def body(*refs):
        srcs, dsts = refs[:n], refs[n:2 * n]
        send_sems, recv_sems, local_sems = refs[2 * n:]
        x, y, c = lax.axis_index("x"), lax.axis_index("y"), lax.axis_index("c")
        chip = 2 * x + y
        sibling = (x, y, 1 - c)
        chips = [(1 - x, y), (x, 1 - y), (1 - x, 1 - y)]

        def slot(px, py, pc):
            return 4 * px + 2 * py + pc

        def copy(a, k, src, s, to):
            return pltpu.make_async_remote_copy(
                src_ref=src, dst_ref=dsts[a].at[s], send_sem=send_sems.at[a, k], recv_sem=recv_sems.at[a, k],
                device_id=to, device_id_type=MESH_ID)

        mine = slot(x, y, c)
        local = [pltpu.make_async_copy(srcs[a].at[chip], dsts[a].at[mine], local_sems.at[a]) for a in range(n)]
        for cp in local:
            cp.start()
        first = []
        for a in range(n):
            first.append(copy(a, 0, srcs[a].at[chip], mine, sibling))
            for j, (px, py) in enumerate(chips):
                first.append(copy(a, 1 + j, srcs[a].at[2 * px + py], mine, (px, py, c)))
        for cp in first:
            cp.start()
        passed = []
        for a in range(n):
            for j, (px, py) in enumerate(chips):
                s = slot(px, py, c)
                copy(a, 1 + j, srcs[a].at[chip], s, (x, y, c)).wait_recv()
                fwd = copy(a, 4 + j, dsts[a].at[s], s, sibling)
                fwd.start()
                passed.append(fwd)
        for a in range(n):
            copy(a, 0, srcs[a].at[chip], slot(x, y, 1 - c), (x, y, c)).wait_recv()
            for j, (px, py) in enumerate(chips):
                copy(a, 4 + j, srcs[a].at[chip], slot(px, py, 1 - c), (x, y, c)).wait_recv()
        for cp in first + passed:
            cp.wait_send()
        for cp in local:
            cp.wait()

    hbm = pl.BlockSpec(memory_space=pltpu.HBM)
    return pl.pallas_call(
        body, name=name,
        out_shape=tuple(jax.ShapeDtypeStruct((8,) + a.shape[1:], a.dtype) for a in slabs),
        in_specs=[hbm] * n, out_specs=tuple([hbm] * n),
        scratch_shapes=[pltpu.SemaphoreType.DMA((n, 7)), pltpu.SemaphoreType.DMA((n, 7)), pltpu.SemaphoreType.DMA((n,))],
    )(*slabs)


def _ada(call, w_shard):
    def body(c_ref, w_ref, o_ref):
        o_ref[...] = _mm(c_ref[...].astype(BF16), w_ref[...].astype(BF16))

    return pl.pallas_call(body, name="ada", out_shape=jax.ShapeDtypeStruct((8, w_shard.shape[1]), F32),
                          compiler_params=_cp())(call, w_shard)


def _inproj(x, mod, win, qg, kvg, wq, wk, wv, wkt, wvt, pos, invf):
    S = x.shape[0]
    tm = min(TM, S)

    def body(x_ref, mod_ref, win_ref, qg_ref, kvg_ref, wq_ref, wk_ref, wv_ref, wkt_ref, wvt_ref, pos_ref, invf_ref,
             qlat_ref, ckv_ref, za_ref, xbc_ref, zs_ref, dt_ref, q_ref, k_ref, v_ref, kt_ref, vt_ref):
        shift = mod_ref[0:1, 0:D]
        scale = mod_ref[0:1, D:2 * D]
        u = (x_ref[...] * (1.0 + scale) + shift).astype(BF16)

        def proj(p):
            return _mm(u, win_ref[:, p[0]:p[1]])

        ql = proj(P_Q)
        ckv = proj(P_KV)
        qlat_ref[...] = ql
        ckv_ref[...] = ckv
        za_ref[...] = proj(P_ZA)
        xbc_ref[...] = proj(P_XBC)
        zs_ref[...] = proj(P_ZS)
        dt_ref[...] = proj(P_DT)
        kr = proj(P_KR)

        ang = pos_ref[...].astype(F32) * invf_ref[...]
        cs = jnp.cos(ang)
        sn = jnp.sin(ang)

        rq = lax.rsqrt(jnp.mean(ql * ql, axis=-1, keepdims=True) + RMS_EPS)
        qn = (ql * rq * qg_ref[...]).astype(BF16)
        for h in range(NH):
            qh = _mm(qn, wq_ref[:, h * HP:(h + 1) * HP])
            q_ref[h, :, 0:NOPE] = (qh[:, 0:NOPE] * QSCALE).astype(BF16)
            q_ref[h, :, NOPE:HP] = (_rope(qh[:, NOPE:HP], cs, sn) * QSCALE).astype(BF16)

        rk = lax.rsqrt(jnp.mean(ckv * ckv, axis=-1, keepdims=True) + RMS_EPS)
        kn = (ckv * rk * kvg_ref[...]).astype(BF16)
        knope = _mm(kn, wk_ref[...])
        vall = _mm(kn, wv_ref[...])
        krf = _rope(kr, cs, sn)
        krr = krf.astype(BF16)
        krt = krf.T.astype(BF16)
        ones_rows = jnp.where(lax.broadcasted_iota(jnp.int32, (VDP - VD, tm), 0) == 0, 1.0, 0.0).astype(BF16)
        for h in range(NH):
            k_ref[h, :, 0:NOPE] = knope[:, h * NOPE:(h + 1) * NOPE].astype(BF16)
            k_ref[h, :, NOPE:HP] = krr
            v_ref[h] = vall[:, h * VD:(h + 1) * VD].astype(BF16)
            kt_ref[h, 0:NOPE, :] = _nt(wkt_ref[h * NOPE:(h + 1) * NOPE, :], kn).astype(BF16)
            kt_ref[h, NOPE:HP, :] = krt
            vt_ref[h, 0:VD, :] = _nt(wvt_ref[h * VD:(h + 1) * VD, :], kn).astype(BF16)
            vt_ref[h, VD:VDP, :] = ones_rows

    row = lambda n: pl.BlockSpec((tm, n), lambda i: (i, 0))
    heads = lambda n: pl.BlockSpec((NH, tm, n), lambda i: (0, i, 0))
    heads_t = lambda n: pl.BlockSpec((NH, None, n, tm), lambda i: (0, i, 0, 0))
    sd = lambda n: jax.ShapeDtypeStruct((S, n), F32)
    hd = lambda n: jax.ShapeDtypeStruct((NH, S, n), BF16)
    ht = lambda n: jax.ShapeDtypeStruct((NH, S // tm, n, tm), BF16)
    return pl.pallas_call(
        body, name="inproj", grid=(S // tm,),
        in_specs=[row(D), _const((8, 3 * D)), _const((D, IN_P)), _const((1, QL)), _const((1, KVL)),
                  _const((QL, NH * HP)), _const((KVL, NH * NOPE)), _const((KVL, NH * VD)),
                  _const((NH * NOPE, KVL)), _const((NH * VD, KVL)), row(1), _const((1, 128))],
        out_specs=[row(QL), row(KVL), row(D), row(CC), row(D), row(128), heads(HP), heads(HP), heads(VD),
                   heads_t(HP), heads_t(VDP)],
        out_shape=[sd(QL), sd(KVL), sd(D), sd(CC), sd(D), sd(128), hd(HP), hd(HP), hd(VD), ht(HP), ht(VDP)],
        compiler_params=_cp(("arbitrary",)),
    )(x, mod, win, qg, kvg, wq, wk, wv, wkt, wvt, pos, invf)


def _attn_fwd(q, k, vt):
    _, S, _ = q.shape
    tq = min(TQF, S)
    nq = S // tq
    half = tq // 2
    tb = vt.shape[3]
    nsb = half // tb

    def body(q_ref, k_ref, vt_ref, o_ref, lse_ref):
        i = pl.program_id(1)
        qb = q_ref[...]

        def scores(j, hb):
            off = pl.multiple_of(j * tq + hb * half, half)
            return _nt(k_ref[pl.ds(off, half), :], qb)

        def update(j, hb, s, carry):
            m, acc = carry
            m_new = jnp.maximum(m, jnp.max(s, axis=0, keepdims=True))
            a = jnp.exp2(m - m_new)
            pb = jnp.exp2(s - m_new).astype(BF16)
            acc = a * acc
            for sb in range(nsb):
                acc = acc + _mm(vt_ref[(2 * j + hb) * nsb + sb], pb[sb * tb:(sb + 1) * tb, :])
            return m_new, acc

        def trip(j, carry, masked):
            s = [scores(j, hb) for hb in range(2)]
            if masked:
                r = lax.broadcasted_iota(jnp.int32, s[0].shape, 0)
                cidx = lax.broadcasted_iota(jnp.int32, s[0].shape, 1)
                s = [jnp.where(cidx >= r + hb * half, s[hb], -1e30) for hb in range(2)]
            for hb in range(2):
                carry = update(j, hb, s[hb], carry)
            return carry

        init = (jnp.full((1, tq), -1e30, F32), jnp.zeros((VDP, tq), F32))
        carry = lax.fori_loop(0, i, lambda j, cr: trip(j, cr, False), init)
        m, acc = trip(i, carry, True)
        l = acc[VD:VD + 1, :]
        o_ref[...] = (acc[0:VD, :] / l).T
        lse_ref[...] = m + jnp.log2(l)

    return pl.pallas_call(
        body, name="attn_fwd", grid=(NH, nq),
        in_specs=[pl.BlockSpec((None, tq, HP), lambda h, i: (h, i, 0)),
                  pl.BlockSpec((None, S, HP), lambda h, i: (h, 0, 0)),
                  pl.BlockSpec((None, S // tb, VDP, tb), lambda h, i: (h, 0, 0, 0))],
        out_specs=[pl.BlockSpec((tq, VD), lambda h, i: (i, h)),
                   pl.BlockSpec((None, None, 1, tq), lambda h, i: (h, i, 0, 0))],
        out_shape=[jax.ShapeDtypeStruct((S, NH * VD), F32), jax.ShapeDtypeStruct((NH, nq, 1, tq), F32)],
        compiler_params=_cp(("arbitrary", "arbitrary")),
    )(q, k, vt)


def _ssd_consts():
    tri = np.tril(np.ones((CH, CH), np.float32))
    e16 = np.zeros((128, SW), np.float32)
    for h in range(SH):
        e16[h, h * SP:(h + 1) * SP] = 1.0
    return jnp.asarray(tri), jnp.asarray(tri.T.copy()), jnp.asarray(e16)


def _ssd_chunk_fwd_common(xbc_ref, halo_ref, dtraw_ref, cw_ref, cb_ref, dtb_ref, alog_ref, tri_ref, e16_ref, ext, first):
    ext[0:8, :] = jnp.where(first, 0.0, halo_ref[...])
    ext[8:8 + CH, :] = xbc_ref[...]
    cw = cw_ref[...]
    xc = cb_ref[...] + cw[0:1, :] * ext[5:5 + CH, :]
    for kk in range(1, CW):
        xc = xc + cw[kk:kk + 1, :] * ext[5 + kk:5 + kk + CH, :]
    sact = _sigmoid(xc)
    act = xc * sact
    lane = lax.broadcasted_iota(jnp.int32, (1, 128), 1)
    arow = jnp.where(lane < SH, -jnp.exp(alog_ref[...]), 0.0)
    dtpre = dtraw_ref[...] + dtb_ref[...]
    dt = _softplus(dtpre)
    a = dt * arow
    cum = _mm_hi(tri_ref[...], a)
    cumx = _mm_hi(cum, e16_ref[...])
    dtx = _mm_hi(dt, e16_ref[...])
    return xc, sact, act, arow, dtpre, dt, cum, cumx, dtx


def _ssd_fwd(xbc, dtraw, zs, conv_w, conv_b, dtb, alog, dskx, gssm):
    S = xbc.shape[0]
    nc = S // CH
    tri, _, e16 = _ssd_consts()

    def body(xbc_ref, halo_ref, dtraw_ref, zs_ref, cw_ref, cb_ref, dtb_ref, alog_ref, dsk_ref, g_ref, tri_ref, e16_ref,
             y_ref, htp_ref, o_ref, ht, ext):
        i = pl.program_id(0)

        @pl.when(i == 0)
        def _():
            ht[...] = jnp.zeros_like(ht)

        xc, sact, act, arow, dtpre, dt, cum, cumx, dtx = _ssd_chunk_fwd_common(
            xbc_ref, halo_ref, dtraw_ref, cw_ref, cb_ref, dtb_ref, alog_ref, tri_ref, e16_ref, ext, i == 0)
        cum_t = cum.T
        xs = act[:, 0:SW]
        lastx = cumx[CH - 1:CH, :]
        xh = xs * dtx
        eexp = jnp.exp(cumx)
        dte = jnp.exp(lastx - cumx)
        cdx = jnp.exp(lastx)
        htp = ht[...]
        htp_ref[...] = htp
        xw = (xh * dte).astype(BF16)
        xb = xh.astype(BF16)
        trim = tri_ref[...] > 0.5
        lane = lax.broadcasted_iota(jnp.int32, (CH, 128), 1)
        parts = []
        for g in range(SG):
            gl = slice(g * GW, (g + 1) * GW)
            bg = act[:, SW + g * SN:SW + (g + 1) * SN].astype(BF16)
            cg = act[:, SW + SG * SN + g * SN:SW + SG * SN + (g + 1) * SN].astype(BF16)
            cbm = _nt(cg, bg)
            yoff = eexp[:, gl] * _mm(cg, htp[:, gl].astype(BF16))
            ht[:, gl] = htp[:, gl] * cdx[:, gl] + _tn(bg, xw[:, gl])
            for pr in range(GW // 128):
                h0 = g * (SH // SG) + 2 * pr
                lo = g * GW + pr * 128
                xp = xb[:, lo:lo + 128]
                res = []
                for hh in (h0, h0 + 1):
                    seg = cum[:, hh:hh + 1] - cum_t[hh:hh + 1, :]
                    mh = jnp.where(trim, cbm * jnp.exp(seg), 0.0).astype(BF16)
                    res.append(_mm(mh, xp))
                parts.append(jnp.where(lane < SP, res[0], res[1]) + yoff[:, pr * 128:(pr + 1) * 128])
        y = jnp.concatenate(parts, axis=1) + xs * dsk_ref[...]
        y_ref[...] = y
        z = zs_ref[...]
        hf = y * (z * _sigmoid(z))
        outs = []
        for g in range(SG):
            hg = hf[:, g * GW:(g + 1) * GW]
            rs = lax.rsqrt(jnp.mean(hg * hg, axis=-1, keepdims=True) + RMS_EPS)
            outs.append(hg * rs)
        o_ref[...] = (jnp.concatenate(outs, axis=1) * g_ref[...]).astype(BF16)

    row = lambda n: pl.BlockSpec((CH, n), lambda i: (i, 0))
    return pl.pallas_call(
        body, name="ssd_fwd", grid=(nc,),
        in_specs=[row(CC), pl.BlockSpec((8, CC), lambda i: (jnp.maximum(i * (CH // 8) - 1, 0), 0)), row(128), row(SW),
                  _const((CW, CC)), _const((1, CC)), _const((1, 128)), _const((1, 128)), _const((1, SW)), _const((1, SW)),
                  _const((CH, CH)), _const((128, SW))],
        out_specs=[row(SW), pl.BlockSpec((None, SN, SW), lambda i: (i, 0, 0)), row(SW)],
        out_shape=[jax.ShapeDtypeStruct((S, SW), F32), jax.ShapeDtypeStruct((nc, SN, SW), F32),
                   jax.ShapeDtypeStruct((S, SW), BF16)],
        scratch_shapes=[pltpu.VMEM((SN, SW), F32), pltpu.VMEM((8 + CH, CC), F32)],
        compiler_params=_cp(("arbitrary",)),
    )(xbc, xbc, dtraw, zs, conv_w, conv_b, dtb, alog, dskx, gssm, tri, e16)


def _outproj(o, za, ossm, x, tgt, wout, mod, ln_g, ln_b):
    S = x.shape[0]
    tm = min(TM, S)

    def body(o_ref, za_ref, os_ref, x_ref, t_ref, w_ref, mod_ref, g_ref, b_ref,
             gx_ref, do_ref, dza_ref, dos_ref, delta_ref, dw_ref, vec_ref):
        i = pl.program_id(0)

        @pl.when(i == 0)
        def _():
            dw_ref[...] = jnp.zeros_like(dw_ref)
            vec_ref[...] = jnp.zeros_like(vec_ref)

        gate = mod_ref[0:1, 2 * D:3 * D]
        ov = o_ref[...]
        z = za_ref[...]
        sz = _sigmoid(z)
        silz = z * sz
        a = (ov * silz).astype(BF16)
        osb = os_ref[...]
        mixed = _mm(a, w_ref[0:D, :]) + _mm(osb, w_ref[D:MIX, :])
        xv = x_ref[...]
        hres = ALPHA * xv + gate * mixed
        mu = jnp.mean(hres, axis=-1, keepdims=True)
        hc = hres - mu
        var = jnp.mean(hc * hc, axis=-1, keepdims=True)
        rstd = lax.rsqrt(var + LN_EPS)
        xhat = hc * rstd
        g = g_ref[...]
        yv = xhat * g + b_ref[...]
        err = yv - t_ref[...]
        dy = err * (1.0 / D)
        vec_ref[0:1, :] += jnp.sum(err * err, axis=0, keepdims=True)
        vec_ref[1:2, :] += jnp.sum(dy * xhat, axis=0, keepdims=True)
        vec_ref[2:3, :] += jnp.sum(dy, axis=0, keepdims=True)
        dxh = dy * g
        dh = rstd * (dxh - jnp.mean(dxh, axis=-1, keepdims=True) - xhat * jnp.mean(dxh * xhat, axis=-1, keepdims=True))
        gx_ref[...] = ALPHA * dh
        vec_ref[3:4, :] += jnp.sum(dh * mixed, axis=0, keepdims=True)
        dmixed = (gate * dh).astype(BF16)
        dw_ref[0:D, :] += _tn(a, dmixed)
        dw_ref[D:MIX, :] += _tn(osb, dmixed)
        da = _nt(dmixed, w_ref[0:D, :])
        dos_ref[...] = _nt(dmixed, w_ref[D:MIX, :])
        dov = da * silz
        do_ref[...] = dov.astype(BF16)
        dza_ref[...] = (da * ov * (sz * (1.0 + z * (1.0 - sz)))).astype(BF16)
        pr = dov * ov
        for h in range(NH):
            delta_ref[h] = jnp.sum(pr[:, h * VD:(h + 1) * VD], axis=-1, keepdims=True)

    row = lambda n: pl.BlockSpec((tm, n), lambda i: (i, 0))
    return pl.pallas_call(
        body, name="outproj", grid=(S // tm,),
        in_specs=[row(D), row(D), row(D), row(D), row(D), _const((MIX, D)), _const((8, 3 * D)), _const((1, D)), _const((1, D))],
        out_specs=[row(D), row(D), row(D), row(D), pl.BlockSpec((NH, tm, 1), lambda i: (0, i, 0)),
                   _full((MIX, D)), _full((8, D))],
        out_shape=[jax.ShapeDtypeStruct((S, D), F32), jax.ShapeDtypeStruct((S, D), BF16), jax.ShapeDtypeStruct((S, D), BF16),
                   jax.ShapeDtypeStruct((S, D), F32), jax.ShapeDtypeStruct((NH, S, 1), F32),
                   jax.ShapeDtypeStruct((MIX, D), F32), jax.ShapeDtypeStruct((8, D), F32)],
        compiler_params=_cp(("arbitrary",)),
    )(o, za, ossm, x, tgt, wout, mod, ln_g, ln_b)


def _attn_bwd(q, k, kt, v, do, lse, delta):
    _, S, _ = q.shape
    tk = min(TQ, S // 2)
    nk = S // tk
    tq = 2 * tk
    nq = S // tq
    tb = kt.shape[3]
    nsb = tk // tb

    def body(k_ref, kt_ref, v_ref, q_ref, do_ref, lse_ref, dl_ref, dk_ref, dv_ref, dqt_ref):
        j = pl.program_id(1)
        kb = k_ref[...]
        vb = v_ref[...]

        @pl.when(j == 0)
        def _():
            dqt_ref[...] = jnp.zeros_like(dqt_ref)

        dk_ref[...] = jnp.zeros_like(dk_ref)
        dv_ref[...] = jnp.zeros_like(dv_ref)

        def step(i, masked):
            off = pl.multiple_of(i * tq, tq)
            qb = q_ref[pl.ds(off, tq), :]
            dob = do_ref[pl.ds(off, tq), :]
            pt = jnp.exp2(_nt(kb, qb) - lse_ref[i])
            if masked:
                r = lax.broadcasted_iota(jnp.int32, pt.shape, 0)
                cidx = lax.broadcasted_iota(jnp.int32, pt.shape, 1)
                pt = jnp.where(i * tq + cidx >= j * tk + r, pt, 0.0)
            dv_ref[...] += _mm(pt.astype(BF16), dob)
            dsb = (pt * (_nt(vb, dob) - dl_ref[i])).astype(BF16)
            dk_ref[...] += _mm(dsb, qb)
            acc = dqt_ref[i]
            for sb in range(nsb):
                acc = acc + _mm(kt_ref[sb], dsb[sb * tb:(sb + 1) * tb, :])
            dqt_ref[i] = acc

        first = j // 2
        step(first, True)

        def loop_body(i, carry):
            step(i, False)
            return carry

        lax.fori_loop(first + 1, nq, loop_body, 0)
        dk_ref[...] = dk_ref[...] * LN2

    return pl.pallas_call(
        body, name="attn_bwd", grid=(NH, nk),
        in_specs=[pl.BlockSpec((None, tk, HP), lambda h, j: (h, j, 0)),
                  pl.BlockSpec((None, nsb, HP, tb), lambda h, j: (h, j, 0, 0)),
                  pl.BlockSpec((None, tk, VD), lambda h, j: (h, j, 0)),
                  pl.BlockSpec((None, S, HP), lambda h, j: (h, 0, 0), pipeline_mode=pl.Buffered(1)),
                  pl.BlockSpec((S, VD), lambda h, j: (0, h), pipeline_mode=pl.Buffered(1)),
                  pl.BlockSpec((None, nq, 1, tq), lambda h, j: (h, 0, 0, 0)),
                  pl.BlockSpec((None, nq, 1, tq), lambda h, j: (h, 0, 0, 0))],
        out_specs=[pl.BlockSpec((None, tk, HP), lambda h, j: (h, j, 0)),
                   pl.BlockSpec((None, tk, VD), lambda h, j: (h, j, 0)),
                   pl.BlockSpec((None, nq, HP, tq), lambda h, j: (h, 0, 0, 0), pipeline_mode=pl.Buffered(1))],
        out_shape=[jax.ShapeDtypeStruct((NH, S, HP), F32), jax.ShapeDtypeStruct((NH, S, VD), F32),
                   jax.ShapeDtypeStruct((NH, nq, HP, tq), F32)],
        compiler_params=_cp(("arbitrary", "arbitrary")),
    )(k, kt, v, q, do, lse.reshape(NH, nq, 1, tq), delta.reshape(NH, nq, 1, tq))


def _ssd_bwd(xbc, dtraw, zs, y, htp, dossm, conv_w, conv_b, dtb, alog, dskx, gssm):
    S = xbc.shape[0]
    nc = S // CH
    tri, triu, e16 = _ssd_consts()

    def body(xbc_ref, halo_ref, dtraw_ref, zs_ref, y_ref, htp_ref, dos_ref,
             cw_ref, cb_ref, dtb_ref, alog_ref, dsk_ref, g_ref, tri_ref, triu_ref, e16_ref,
             dxbc_ref, ddt_ref, dzs_ref, dcw_ref, dcb_ref, dvec_ref, dg_ref,
             dht, ext, dext, dskacc):
        r = pl.program_id(0)
        i = nc - 1 - r

        @pl.when(r == 0)
        def _():
            dht[...] = jnp.zeros_like(dht)
            dext[CH:CH + 8, :] = jnp.zeros((8, CC), F32)
            dskacc[...] = jnp.zeros_like(dskacc)
            dcw_ref[...] = jnp.zeros_like(dcw_ref)
            dcb_ref[...] = jnp.zeros_like(dcb_ref)
            dvec_ref[...] = jnp.zeros_like(dvec_ref)
            dg_ref[...] = jnp.zeros_like(dg_ref)

        xc, sact, act, arow, dtpre, dt, cum, cumx, dtx = _ssd_chunk_fwd_common(
            xbc_ref, halo_ref, dtraw_ref, cw_ref, cb_ref, dtb_ref, alog_ref, tri_ref, e16_ref, ext, i == 0)
        cum_t = cum.T
        xs = act[:, 0:SW]
        lastx = cumx[CH - 1:CH, :]
        xh = xs * dtx
        eexp = jnp.exp(cumx)
        dte = jnp.exp(lastx - cumx)
        cdx = jnp.exp(lastx)
        trim = tri_ref[...] > 0.5
        lane = lax.broadcasted_iota(jnp.int32, (CH, 128), 1)
        rowi = lax.broadcasted_iota(jnp.int32, (CH, 128), 0)

        yv = y_ref[...]
        z = zs_ref[...]
        sz = _sigmoid(z)
        silz = z * sz
        hf = yv * silz
        dn = dos_ref[...] * g_ref[...]
        dhf_parts, nrm_parts = [], []
        for g in range(SG):
            gl = slice(g * GW, (g + 1) * GW)
            hg = hf[:, gl]
            rs = lax.rsqrt(jnp.mean(hg * hg, axis=-1, keepdims=True) + RMS_EPS)
            ng = hg * rs
            dng = dn[:, gl]
            dhf_parts.append(rs * (dng - ng * jnp.mean(dng * ng, axis=-1, keepdims=True)))
            nrm_parts.append(ng)
        nrm = jnp.concatenate(nrm_parts, axis=1)
        dhf = jnp.concatenate(dhf_parts, axis=1)
        dg_ref[...] += jnp.sum(dos_ref[...] * nrm, axis=0, keepdims=True)
        dyv = dhf * silz
        dzs_ref[...] = (dhf * yv * (sz * (1.0 + z * (1.0 - sz)))).astype(BF16)
        dskacc[...] += jnp.sum(dyv * xs, axis=0, keepdims=True)
        dxs_skip = dyv * dsk_ref[...]

        dhtn = dht[...]
        hp = htp_ref[...]
        dlastx = jnp.sum(dhtn * hp, axis=0, keepdims=True) * cdx
        xb = xh.astype(BF16)
        xwf = xh * dte
        dcum = jnp.zeros((CH, 128), F32)
        dcum_t = jnp.zeros((128, CH), F32)
        dxh_parts, dcumx_parts, dlast_parts, db_parts, dc_parts = [], [], [], [], []
        for g in range(SG):
            gl = slice(g * GW, (g + 1) * GW)
            bg = act[:, SW + g * SN:SW + (g + 1) * SN].astype(BF16)
            cg = act[:, SW + SG * SN + g * SN:SW + SG * SN + (g + 1) * SN].astype(BF16)
            hpg = hp[:, gl].astype(BF16)
            dhn = dhtn[:, gl].astype(BF16)
            dyg = dyv[:, gl]
            yoff = eexp[:, gl] * _mm(cg, hpg)
            dz = (dyg * eexp[:, gl]).astype(BF16)
            dcg = _nt(dz, hpg)
            dht[:, gl] = dhtn[:, gl] * cdx[:, gl] + _tn(cg, dz)
            dcumx_g = dyg * yoff
            dbg = _nt(xwf[:, gl].astype(BF16), dhn)
            dxw = _mm(bg, dhn)
            ddte = dxw * xwf[:, gl]
            dcumx_parts.append(dcumx_g - ddte)
            dlast_parts.append(jnp.sum(ddte, axis=0, keepdims=True))
            dxh_g = dxw * dte[:, gl]
            cbm = _nt(cg, bg)
            dcb = jnp.zeros((CH, CH), F32)
            dxp_parts = []
            for pr in range(GW // 128):
                h0 = g * (SH // SG) + 2 * pr
                lo = g * GW + pr * 128
                xp = xb[:, lo:lo + 128]
                dyp = dyv[:, lo:lo + 128]
                dxp = jnp.zeros((CH, 128), F32)
                for idx, hh in enumerate((h0, h0 + 1)):
                    decay = jnp.where(trim, jnp.exp(cum[:, hh:hh + 1] - cum_t[hh:hh + 1, :]), 0.0)
                    mh = cbm * decay
                    keep = (lane < SP) if idx == 0 else (lane >= SP)
                    dym = jnp.where(keep, dyp, 0.0).astype(BF16)
                    dm = _nt(dym, xp)
                    dxp = dxp + _tn(mh.astype(BF16), dym)
                    gm = dm * mh
                    dcum = dcum + jnp.where(lane == hh, jnp.sum(gm, axis=1, keepdims=True), 0.0)
                    dcum_t = dcum_t - jnp.where(rowi == hh, jnp.sum(gm, axis=0, keepdims=True), 0.0)
                    dcb = dcb + dm * decay
                dxp_parts.append(dxp)
            dxh_parts.append(dxh_g + jnp.concatenate(dxp_parts, axis=1))
            dcbb = dcb.astype(BF16)
            dc_parts.append(dcg + _mm(dcbb, bg))
            db_parts.append(dbg + _tn(dcbb, cg))
        dxh = jnp.concatenate(dxh_parts, axis=1)
        dcumx = jnp.concatenate(dcumx_parts, axis=1)
        dlastx = dlastx + jnp.concatenate(dlast_parts, axis=1)
        e16 = e16_ref[...]
        dlast128 = _nt_hi(jnp.broadcast_to(dlastx, (8, SW)), e16)[0:1, :]
        dcum = dcum + dcum_t.T + _nt_hi(dcumx, e16) + jnp.where(rowi == CH - 1, dlast128, 0.0)
        da = _mm_hi(triu_ref[...], dcum)
        ddt = da * arow + _nt_hi(dxh * xs, e16)
        dvec_ref[1:2, :] += jnp.sum(da * dt, axis=0, keepdims=True)
        ddtraw = jnp.where(lane < SH, ddt * _sigmoid(dtpre), 0.0)
        dvec_ref[0:1, :] += jnp.sum(ddtraw, axis=0, keepdims=True)
        ddt_ref[...] = ddtraw.astype(BF16)
        dxs = dxs_skip + dxh * dtx
        dact = jnp.concatenate([dxs] + db_parts + dc_parts, axis=1)
        dxc = dact * (sact * (1.0 + xc * (1.0 - sact)))

        dcb_ref[...] += jnp.sum(dxc, axis=0, keepdims=True)
        for kk in range(CW):
            dcw_ref[kk:kk + 1, :] += jnp.sum(dxc * ext[5 + kk:5 + kk + CH, :], axis=0, keepdims=True)
        dext[0:CH, :] = dxc
        cw = cw_ref[...]
        dxr = cw[CW - 1:CW, :] * dxc
        for kk in range(CW - 1):
            dxr = dxr + cw[kk:kk + 1, :] * dext[CW - 1 - kk:CW - 1 - kk + CH, :]
        dxbc_ref[...] = dxr.astype(BF16)
        dext[CH:CH + 8, :] = dxc[0:8, :]

        @pl.when(r == nc - 1)
        def _():
            dvec_ref[1:2, :] = dvec_ref[1:2, :] * arow
            dvec_ref[2:3, :] = _nt_hi(jnp.broadcast_to(dskacc[...], (8, SW)), e16)[0:1, :]

    rev = lambda n: pl.BlockSpec((CH, n), lambda r: (nc - 1 - r, 0))
    return pl.pallas_call(
        body, name="ssd_bwd", grid=(nc,),
        in_specs=[rev(CC), pl.BlockSpec((8, CC), lambda r: (jnp.maximum((nc - 1 - r) * (CH // 8) - 1, 0), 0)),
                  rev(128), rev(SW), rev(SW), pl.BlockSpec((None, SN, SW), lambda r: (nc - 1 - r, 0, 0)), rev(SW),
                  _const((CW, CC)), _const((1, CC)), _const((1, 128)), _const((1, 128)), _const((1, SW)), _const((1, SW)),
                  _const((CH, CH)), _const((CH, CH)), _const((128, SW))],
        out_specs=[rev(CC), rev(128), rev(SW), _full((CW, CC)), _full((1, CC)), _full((8, 128)), _full((1, SW))],
        out_shape=[jax.ShapeDtypeStruct((S, CC), BF16), jax.ShapeDtypeStruct((S, 128), BF16), jax.ShapeDtypeStruct((S, SW), BF16),
                   jax.ShapeDtypeStruct((CW, CC), F32), jax.ShapeDtypeStruct((1, CC), F32),
                   jax.ShapeDtypeStruct((8, 128), F32), jax.ShapeDtypeStruct((1, SW), F32)],
        scratch_shapes=[pltpu.VMEM((SN, SW), F32), pltpu.VMEM((8 + CH, CC), F32), pltpu.VMEM((CH + 8, CC), F32),
                        pltpu.VMEM((1, SW), F32)],
        compiler_params=_cp(("arbitrary",)),
    )(xbc, xbc, dtraw, zs, y, htp, dossm, conv_w, conv_b, dtb, alog, dskx, gssm, tri, triu, e16)


def _mla_bwd(dqt, dk, dv, qlat, ckv, qg, kvg, wq, wk, wv, pos, invf):
    S = qlat.shape[0]
    tm = min(TQ, dqt.shape[3])
    per = dqt.shape[3] // tm

    def body(dq_ref, dk_ref, dv_ref, ql_ref, ckv_ref, qg_ref, kvg_ref, wq_ref, wk_ref, wv_ref, pos_ref, invf_ref,
             dql_ref, dckv_ref, dkr_ref, dwq_ref, dwk_ref, dwv_ref, dqg_ref, dkvg_ref):
        i = pl.program_id(0)

        @pl.when(i == 0)
        def _():
            dwq_ref[...] = jnp.zeros_like(dwq_ref)
            dwk_ref[...] = jnp.zeros_like(dwk_ref)
            dwv_ref[...] = jnp.zeros_like(dwv_ref)
            dqg_ref[...] = jnp.zeros_like(dqg_ref)
            dkvg_ref[...] = jnp.zeros_like(dkvg_ref)

        ang = pos_ref[...].astype(F32) * invf_ref[...]
        cs = jnp.cos(ang)
        sn = jnp.sin(ang)

        def rms_bwd(v, g, dn, dg_ref):
            r = lax.rsqrt(jnp.mean(v * v, axis=-1, keepdims=True) + RMS_EPS)
            vh = v * r
            dg_ref[...] += jnp.sum(dn * vh, axis=0, keepdims=True)
            dvh = dn * g
            return vh, r * (dvh - vh * jnp.mean(dvh * vh, axis=-1, keepdims=True))

        pieces = []
        for h in range(NH):
            dqh = dq_ref[h].T
            pieces.append(dqh[:, 0:NOPE] * SCALE)
            pieces.append(_rope_t(dqh[:, NOPE:HP], cs, sn) * SCALE)
        dqf = jnp.concatenate(pieces, axis=1).astype(BF16)
        ql = ql_ref[...]
        g = qg_ref[...]
        dqn = _nt(dqf, wq_ref[...])
        qh, dql = rms_bwd(ql, g, dqn, dqg_ref)
        dwq_ref[...] += _tn((qh * g).astype(BF16), dqf)
        dql_ref[...] = dql.astype(BF16)

        dkn_p = jnp.concatenate([dk_ref[h, :, 0:NOPE] for h in range(NH)], axis=1).astype(BF16)
        dvf = jnp.concatenate([dv_ref[h] for h in range(NH)], axis=1).astype(BF16)
        dkr = dk_ref[0, :, NOPE:HP]
        for h in range(1, NH):
            dkr = dkr + dk_ref[h, :, NOPE:HP]
        lane = lax.broadcasted_iota(jnp.int32, dkr.shape, 1)
        dkr_ref[...] = jnp.where(lane < ROPE, _rope_t(dkr, cs, sn), 0.0).astype(BF16)
        cv = ckv_ref[...]
        gk = kvg_ref[...]
        dkn = _nt(dkn_p, wk_ref[...]) + _nt(dvf, wv_ref[...])
        kh, dckv = rms_bwd(cv, gk, dkn, dkvg_ref)
        knb = (kh * gk).astype(BF16)
        dwk_ref[...] += _tn(knb, dkn_p)
        dwv_ref[...] += _tn(knb, dvf)
        dckv_ref[...] = dckv.astype(BF16)

    row = lambda n: pl.BlockSpec((tm, n), lambda i: (i, 0))
    heads = lambda n: pl.BlockSpec((NH, tm, n), lambda i: (0, i, 0))
    return pl.pallas_call(
        body, name="mla_bwd", grid=(S // tm,),
        in_specs=[pl.BlockSpec((NH, None, HP, tm), lambda i: (0, i // per, 0, i % per)), heads(HP), heads(VD), row(QL), row(KVL), _const((1, QL)), _const((1, KVL)),
                  _const((QL, NH * HP)), _const((KVL, NH * NOPE)), _const((KVL, NH * VD)), row(1), _const((1, 128))],
        out_specs=[row(QL), row(KVL), row(128), _full((QL, NH * HP)), _full((KVL, NH * NOPE)), _full((KVL, NH * VD)),
                   _full((1, QL)), _full((1, KVL))],
        out_shape=[jax.ShapeDtypeStruct((S, QL), BF16), jax.ShapeDtypeStruct((S, KVL), BF16), jax.ShapeDtypeStruct((S, 128), BF16),
                   jax.ShapeDtypeStruct((QL, NH * HP), F32), jax.ShapeDtypeStruct((KVL, NH * NOPE), F32),
                   jax.ShapeDtypeStruct((KVL, NH * VD), F32), jax.ShapeDtypeStruct((1, QL), F32), jax.ShapeDtypeStruct((1, KVL), F32)],
        compiler_params=_cp(("arbitrary",)),
    )(dqt, dk, dv, qlat, ckv, qg, kvg, wq, wk, wv, pos, invf)


def _inproj_bwd(x, gx1, mod, win, dql, dckv, dza, dxbc, dzs, dkr, ddt):
    S = x.shape[0]
    tm = min(TM, S)

    def body(x_ref, gx1_ref, mod_ref, win_ref, dql_ref, dckv_ref, dza_ref, dxbc_ref, dzs_ref, dkr_ref, ddt_ref,
             gx_ref, dw_ref, vec_ref):
        i = pl.program_id(0)

        @pl.when(i == 0)
        def _():
            dw_ref[...] = jnp.zeros_like(dw_ref)
            vec_ref[...] = jnp.zeros_like(vec_ref)

        shift = mod_ref[0:1, 0:D]
        scale = mod_ref[0:1, D:2 * D]
        xv = x_ref[...]
        ut = (xv * (1.0 + scale) + shift).T.astype(BF16)
        pieces = (dql_ref, dckv_ref, dza_ref, dxbc_ref, dzs_ref, dkr_ref, ddt_ref)
        du = jnp.zeros((tm, D), F32)
        lo = 0
        for p_ref in pieces:
            n = p_ref.shape[1]
            dp = p_ref[...]
            du = du + _nt(dp, win_ref[:, lo:lo + n])
            dw_ref[:, lo:lo + n] += _mm(ut, dp)
            lo += n
        vec_ref[0:1, :] += jnp.sum(du, axis=0, keepdims=True)
        vec_ref[1:2, :] += jnp.sum(du * xv, axis=0, keepdims=True)
        gx_ref[...] = gx1_ref[...] + du * (1.0 + scale)

    row = lambda n: pl.BlockSpec((tm, n), lambda i: (i, 0))
    return pl.pallas_call(
        body, name="inproj_bwd", grid=(S // tm,),
        in_specs=[row(D), row(D), _const((8, 3 * D)), _const((D, IN_P)), row(QL), row(KVL), row(D), row(CC), row(D),
                  row(128), row(128)],
        out_specs=[row(D), pl.BlockSpec((D, IN_P), lambda i: (0, 0), pipeline_mode=pl.Buffered(1)), _full((8, D))],
        out_shape=[jax.ShapeDtypeStruct((S, D), F32), jax.ShapeDtypeStruct((D, IN_P), F32), jax.ShapeDtypeStruct((8, D), F32)],
        compiler_params=_cp(("arbitrary",)),
    )(x, gx1, mod, win, dql, dckv, dza, dxbc, dzs, dkr, ddt)


def _ada_bwd(callt, dmods):
    w = dmods.shape[1]

    def body(c_ref, d_ref, o_ref):
        acc = c_ref[:, 0:1] * d_ref[0:1, :]
        for s in range(1, 8):
            acc = acc + c_ref[:, s:s + 1] * d_ref[s:s + 1, :]
        o_ref[0] = acc

    return pl.pallas_call(body, name="ada_bwd", out_shape=jax.ShapeDtypeStruct((1, D, w), F32),
                          compiler_params=_cp())(callt, dmods)


def _adamw(name, parts, w, m, v):
    rows, ncol = w.shape
    tr = min(rows, 128)
    nparts = parts.shape[0]

    def body(p_ref, w_ref, m_ref, v_ref, g_ref, d_ref, nm_ref, nv_ref):
        g = p_ref[0].astype(F32)
        for s in range(1, nparts):
            g = g + p_ref[s].astype(F32)
        g_ref[...] = g
        nm = B1 * m_ref[...] + (1.0 - B1) * g
        nv = B2 * v_ref[...] + (1.0 - B2) * (g * g)
        nm_ref[...] = nm
        nv_ref[...] = nv
        m_hat = nm / (1.0 - B1 ** STEP)
        v_hat = nv / (1.0 - B2 ** STEP)
        d_ref[...] = -LR * (m_hat / (jnp.sqrt(v_hat) + EPS) + WD * w_ref[...])

    row = pl.BlockSpec((tr, ncol), lambda i: (i, 0))
    sd = jax.ShapeDtypeStruct((rows, ncol), F32)
    return pl.pallas_call(
        body, name="adamw_" + name, grid=(rows // tr,),
        in_specs=[pl.BlockSpec((nparts, tr, ncol), lambda i: (0, i, 0)), row, row, row],
        out_specs=[row, row, row, row], out_shape=[sd, sd, sd, sd],
        compiler_params=_cp(("arbitrary",)),
    )(parts, w, m, v)


_SMALL = (("b_ada", 3 * D), ("conv_w", CW * CC // 4), ("conv_b", CC), ("ssm_norm_g", SW), ("ln_g", D), ("ln_b", D),
          ("q_norm_g", QL), ("kv_norm_g", KVL), ("dt_bias", SH), ("a_log", SH), ("d_skip", SH))


def _pack_small(d, lead):
    flat = [d[name].reshape(d[name].shape[:lead] + (-1,)) for name, _ in _SMALL]
    used = sum(f.shape[lead] for f in flat)
    pad = jnp.zeros(flat[0].shape[:lead] + (R_SMALL * 1024 - used,), F32)
    return jnp.concatenate(flat + [pad], axis=lead).reshape(flat[0].shape[:lead] + (R_SMALL, 1024))


def _unpack_small(p):
    flat = p.reshape(-1)
    out, r = {}, 0
    for name, n in _SMALL:
        out[name] = flat[r:r + n]
        r += n
    return out


def _in_to_padded(w):
    z = lambda n: jnp.zeros((w.shape[0], n), w.dtype)
    return jnp.concatenate([w[:, 0:384], w[:, 384:640], w[:, 704:1728], w[:, 1728:3264], w[:, 3280:4304],
                            w[:, 640:704], z(64), w[:, 3264:3280], z(112)], axis=1)


def _in_from_padded(g):
    return jnp.concatenate([g[:, 0:384], g[:, 384:640], g[:, P_KR[0]:P_KR[0] + 64], g[:, 640:1664], g[:, 1664:3200],
                            g[:, P_DT[0]:P_DT[0] + 16], g[:, 3200:4224]], axis=1)


def kernel(x, c, positions, w_ada, b_ada, w_in, q_norm_g, w_qb, kv_norm_g, w_kvb, conv_w, conv_b, dt_bias, a_log, d_skip, ssm_norm_g, w_out, ln_g, ln_b, loss_target, m_w_ada, m_b_ada, m_w_in, m_q_norm_g, m_w_qb, m_kv_norm_g, m_w_kvb, m_conv_w, m_conv_b, m_dt_bias, m_a_log, m_d_skip, m_ssm_norm_g, m_w_out, m_ln_g, m_ln_b, v_w_ada, v_b_ada, v_w_in, v_q_norm_g, v_w_qb, v_kv_norm_g, v_w_kvb, v_conv_w, v_conv_b, v_dt_bias, v_a_log, v_d_skip, v_ssm_norm_g, v_w_out, v_ln_g, v_ln_b):
    S = x.shape[1]
    xv = x[0]
    tgt = loss_target[0]

    cw16 = jnp.concatenate([conv_w[0], jnp.zeros((16 - CW, CC // 4), F32)], axis=0)
    f_in, f_qb, f_kvb, f_out, f_cw = _gather_weights(
        [w_in[0].astype(BF16), w_qb[0].astype(BF16), w_kvb[0].astype(BF16), w_out[0].astype(BF16), cw16])
    cat1 = lambda f: f.transpose(1, 0, 2).reshape(f.shape[1], 4 * f.shape[2])
    win = _in_to_padded(cat1(f_in))
    wqb = cat1(f_qb).reshape(QL, NH, QKD)
    wq = jnp.concatenate([wqb, jnp.zeros((QL, NH, HP - QKD), BF16)], axis=2).reshape(QL, NH * HP)
    wkvb = cat1(f_kvb).reshape(KVL, NH, NOPE + VD)
    wk = wkvb[:, :, 0:NOPE].reshape(KVL, NH * NOPE)
    wv = wkvb[:, :, NOPE:].reshape(KVL, NH * VD)
    wout = f_out.reshape(MIX, D)
    cwf = cat1(f_cw[:, 0:CW, :])

    half = ROPE // 2
    invf = 1.0 / (ROPE_THETA ** (jnp.arange(half, dtype=F32) / half))
    invf = jnp.concatenate([invf, invf, jnp.zeros((128 - ROPE,), F32)]).reshape(1, 128)
    pos = positions.reshape(S, 1)
    pad128 = lambda a: jnp.concatenate([a.reshape(1, SH), jnp.zeros((1, 128 - SH), F32)], axis=1)
    dtb, alog = pad128(dt_bias), pad128(a_log)
    dskx = jnp.repeat(d_skip.reshape(SH), SP).reshape(1, SW)

    my_c = lax.axis_index("c")
    (call,) = _exchange("gather_c", [jnp.broadcast_to(c.reshape(1, 1, D), (4, 1, D))])
    call = call.reshape(8, D)
    mods = _ada(call, w_ada[0])
    (mrows,) = _exchange("scatter_mod", [mods.reshape(4, 2, 3 * D // 4)])
    mine = lax.dynamic_index_in_dim(mrows.reshape(4, 2, 2, 3 * D // 4)[:, 0], my_c, axis=1, keepdims=False)
    mod = jnp.broadcast_to(mine.reshape(1, 3 * D) + b_ada, (8, 3 * D))
    qlat, ckv, za, xbc, zs, dtraw, q, k, v, kt, vt = _inproj(xv, mod, win, q_norm_g, kv_norm_g, wq, wk, wv, wk.T, wv.T, pos, invf)
    o, lse = _attn_fwd(q, k, vt)
    y, htp, ossm = _ssd_fwd(xbc, dtraw, zs, cwf, conv_b, dtb, alog, dskx, ssm_norm_g)
    gx1, do, dza, dossm, delta, dwout, vec_o = _outproj(o, za, ossm, xv, tgt, wout, mod, ln_g, ln_b)
    loss = lax.psum(0.5 / D * jnp.sum(vec_o[0]), ("x", "y", "c"))

    dk, dv, dq = _attn_bwd(q, k, kt, v, do, lse, delta)
    dxbc, ddt, dzs, dcw, dcb, dvec, dgssm = _ssd_bwd(xbc, dtraw, zs, y, htp, dossm, cwf, conv_b, dtb, alog, dskx, ssm_norm_g)
    dql, dckv, dkr, dwq, dwk, dwv, dqg, dkvg = _mla_bwd(dq, dk, dv, qlat, ckv, q_norm_g, kv_norm_g, wq, wk, wv, pos, invf)
    gx, dwin, vec_i = _inproj_bwd(xv, gx1, mod, win, dql, dckv, dza, dxbc, dzs, dkr, ddt)
    dmod = jnp.concatenate([vec_i[0:1], vec_i[1:2], vec_o[3:4]], axis=1)

    cols = lambda g: g.reshape(g.shape[0], 4, g.shape[1] // 4).transpose(1, 0, 2)
    g_in = cols(_in_from_padded(dwin)).astype(BF16)
    g_qb = cols(dwq.reshape(QL, NH, HP)[:, :, 0:QKD].reshape(QL, NH * QKD)).astype(BF16)
    g_kvb = cols(jnp.concatenate([dwk.reshape(KVL, NH, NOPE), dwv.reshape(KVL, NH, VD)], axis=2)
                 .reshape(KVL, NH * (NOPE + VD))).astype(BF16)
    g_out = dwout.reshape(4, MIX // 4, D).astype(BF16)
    small = {"b_ada": dmod, "conv_b": dcb, "ssm_norm_g": dgssm, "ln_g": vec_o[1:2], "ln_b": vec_o[2:3],
             "q_norm_g": dqg, "kv_norm_g": dkvg, "dt_bias": dvec[0:1, 0:SH], "a_log": dvec[1:2, 0:SH], "d_skip": dvec[2:3, 0:SH]}
    small = {n: jnp.broadcast_to(a.reshape(1, -1), (4, a.size)) for n, a in small.items()}
    small["conv_w"] = cols(dcw).reshape(4, CW * CC // 4)
    gsmall = _pack_small(small, 1)

    r_in, r_qb, r_kvb, r_out, rs, dmods = _exchange(
        "exchange_grads", [g_in, g_qb, g_kvb, g_out, gsmall, jnp.broadcast_to(dmod.reshape(1, 1, 3 * D), (4, 1, 3 * D))])
    chip = 2 * lax.axis_index("x") + lax.axis_index("y")
    dmods = lax.dynamic_slice_in_dim(dmods.reshape(8, 3 * D), chip * (3 * D // 4), 3 * D // 4, axis=1)
    g_ada = _ada_bwd(call.T, dmods)
    res = dict(w_ada=_adamw("w_ada", g_ada, w_ada[0], m_w_ada[0], v_w_ada[0]),
               w_in=_adamw("w_in", r_in, w_in[0], m_w_in[0], v_w_in[0]),
               w_qb=_adamw("w_qb", r_qb, w_qb[0], m_w_qb[0], v_w_qb[0]),
               w_kvb=_adamw("w_kvb", r_kvb, w_kvb[0], m_w_kvb[0], v_w_kvb[0]),
               w_out=_adamw("w_out", r_out, w_out[0], m_w_out[0], v_w_out[0]))
    wsm = _pack_small(dict(b_ada=b_ada, conv_w=conv_w, conv_b=conv_b, ssm_norm_g=ssm_norm_g, ln_g=ln_g, ln_b=ln_b,
                           q_norm_g=q_norm_g, kv_norm_g=kv_norm_g, dt_bias=dt_bias, a_log=a_log, d_skip=d_skip), 0)
    msm = _pack_small(dict(b_ada=m_b_ada, conv_w=m_conv_w, conv_b=m_conv_b, ssm_norm_g=m_ssm_norm_g, ln_g=m_ln_g, ln_b=m_ln_b,
                           q_norm_g=m_q_norm_g, kv_norm_g=m_kv_norm_g, dt_bias=m_dt_bias, a_log=m_a_log, d_skip=m_d_skip), 0)
    vsm = _pack_small(dict(b_ada=v_b_ada, conv_w=v_conv_w, conv_b=v_conv_b, ssm_norm_g=v_ssm_norm_g, ln_g=v_ln_g, ln_b=v_ln_b,
                           q_norm_g=v_q_norm_g, kv_norm_g=v_kv_norm_g, dt_bias=v_dt_bias, a_log=v_a_log, d_skip=v_d_skip), 0)
    sm = _adamw("small", rs, wsm, msm, vsm)

    order = ["w_ada", "b_ada", "w_in", "q_norm_g", "w_qb", "kv_norm_g", "w_kvb", "conv_w", "conv_b", "dt_bias", "a_log",
             "d_skip", "ssm_norm_g", "w_out", "ln_g", "ln_b"]
    shapes = dict(w_ada=w_ada.shape, b_ada=b_ada.shape, w_in=w_in.shape, q_norm_g=q_norm_g.shape, w_qb=w_qb.shape,
                  kv_norm_g=kv_norm_g.shape, w_kvb=w_kvb.shape, conv_w=conv_w.shape, conv_b=conv_b.shape, dt_bias=dt_bias.shape,
                  a_log=a_log.shape, d_skip=d_skip.shape, ssm_norm_g=ssm_norm_g.shape, w_out=w_out.shape, ln_g=ln_g.shape,
                  ln_b=ln_b.shape)
    outs = []
    for kind in range(4):
        d = _unpack_small(sm[kind])
        d.update({n: r[kind] for n, r in res.items()})
        outs.extend(d[n].reshape(shapes[n]) for n in order)
    return (loss, gx.reshape(x.shape), *outs)
```

```python
import functools
import math

import numpy as np
import jax
import jax.numpy as jnp
from jax import lax
from jax.experimental import pallas as pl
from jax.experimental.pallas import tpu as pltpu

F32 = jnp.float32
BF16 = jnp.bfloat16
HIGHEST = lax.Precision.HIGHEST
MESH_ID = pl.DeviceIdType.MESH

D = 1024
NH = 8
NOPE = 128
ROPE = 64
VD = 128
VDP = 144
QKD = NOPE + ROPE
HP = 256
QL = 384
KVL = 256
ROPE_THETA = 10000.0
SH = 16
SP = 64
SG = 2
SN = 128
CW = 4
CH = 128
SW = SH * SP
CC = SW + 2 * SG * SN
GW = SW // SG
MIX = 2 * D
IN_W = 4304
ALPHA = 2.0 ** 0.25
RMS_EPS = 1e-6
LN_EPS = 1e-5
SCALE = QKD ** -0.5
LN2 = math.log(2.0)
QSCALE = SCALE / LN2
LR, B1, B2, EPS, WD, STEP = 0.001, 0.9, 0.999, 1e-08, 0.01, 10

P_Q = (0, 384)
P_KV = (384, 640)
P_ZA = (640, 1664)
P_XBC = (1664, 3200)
P_ZS = (3200, 4224)
P_KR = (4224, 4352)
P_DT = (4352, 4480)
IN_P = 4480

R_SMALL = 16

TM = 256
TQ = 512
TQF = 1024
VMEM_LIMIT = 56 * 1024 * 1024


def _cp(sem=None):
    return pltpu.CompilerParams(dimension_semantics=sem, vmem_limit_bytes=VMEM_LIMIT)


def _mm(a, b):
    return jnp.dot(a, b, preferred_element_type=F32)


def _nt(a, b):
    return lax.dot_general(a, b, (((1,), (1,)), ((), ())), preferred_element_type=F32)


def _tn(a, b):
    return lax.dot_general(a, b, (((0,), (0,)), ((), ())), preferred_element_type=F32)


def _mm_hi(a, b):
    return jnp.dot(a, b, precision=HIGHEST, preferred_element_type=F32)


def _nt_hi(a, b):
    return lax.dot_general(a, b, (((1,), (1,)), ((), ())), precision=HIGHEST, preferred_element_type=F32)


def _sigmoid(z):
    return 1.0 / (1.0 + jnp.exp(-z))


def _softplus(z):
    return jnp.maximum(z, 0.0) + jnp.log1p(jnp.exp(-jnp.abs(z)))


def _rope(t, cs, sn):
    lane = lax.broadcasted_iota(jnp.int32, t.shape, 1)
    rot = jnp.where(lane < ROPE // 2, -pltpu.roll(t, 128 - ROPE // 2, 1), pltpu.roll(t, ROPE // 2, 1))
    return t * cs + rot * sn


def _rope_t(t, cs, sn):
    lane = lax.broadcasted_iota(jnp.int32, t.shape, 1)
    y = t * sn
    rot = jnp.where(lane < ROPE // 2, -pltpu.roll(y, 128 - ROPE // 2, 1), pltpu.roll(y, ROPE // 2, 1))
    return t * cs - rot


def _full(shape):
    n = len(shape)
    return pl.BlockSpec(shape, lambda *_: (0,) * n)


def _const(shape):
    n = len(shape)
    return pl.BlockSpec(shape, lambda *_: (0,) * n, pipeline_mode=pl.Buffered(1))


def _gather_weights(shards):
    n = len(shards)
    halves = [a.shape[0] // 2 for a in shards]

    def body(*refs):
        srcs, dsts = refs[:n], refs[n:2 * n]
        send_sems, recv_sems, local_sems = refs[2 * n:]
        x, y, c = lax.axis_index("x"), lax.axis_index("y"), lax.axis_index("c")
        me = 2 * x + y
        sibling = (x, y, 1 - c)
        chips = [(1 - x, y), (x, 1 - y), (1 - x, 1 - y)]

        def rows(a, pc):
            return pl.ds(pl.multiple_of(pc * halves[a], halves[a]), halves[a])

        def copy(a, k, src, slot, pc, to):
            return pltpu.make_async_remote_copy(
                src_ref=src, dst_ref=dsts[a].at[slot, rows(a, pc)], send_sem=send_sems.at[a, k],
                recv_sem=recv_sems.at[a, k], device_id=to, device_id_type=MESH_ID)

        local = [pltpu.make_async_copy(srcs[a], dsts[a].at[me], local_sems.at[a]) for a in range(n)]
        for cp in local:
            cp.start()
        sends = [copy(a, j, srcs[a].at[rows(a, c)], me, c, (px, py, c)) for a in range(n) for j, (px, py) in enumerate(chips)]
        for cp in sends:
            cp.start()
        passed = []
        for a in range(n):
            for j, (px, py) in enumerate(chips):
                k = 2 * px + py
                copy(a, j, srcs[a].at[rows(a, c)], k, c, (x, y, c)).wait_recv()
                fwd = copy(a, 3 + j, dsts[a].at[k, rows(a, c)], k, c, sibling)
                fwd.start()
                passed.append(fwd)
        for a in range(n):
            for j, (px, py) in enumerate(chips):
                copy(a, 3 + j, srcs[a].at[rows(a, c)], 2 * px + py, 1 - c, (x, y, c)).wait_recv()
        for cp in sends + passed:
            cp.wait_send()
        for cp in local:
            cp.wait()

    hbm = pl.BlockSpec(memory_space=pltpu.HBM)
    return pl.pallas_call(
        body, name="gather_weights",
        out_shape=tuple(jax.ShapeDtypeStruct((4,) + a.shape, a.dtype) for a in shards),
        in_specs=[hbm] * n, out_specs=tuple([hbm] * n),
        scratch_shapes=[pltpu.SemaphoreType.DMA((n, 6)), pltpu.SemaphoreType.DMA((n, 6)), pltpu.SemaphoreType.DMA((n,))],
    )(*shards)


def _exchange(name, slabs):
    n = len(slabs)

    def body(*refs):
        srcs, dsts = refs[:n], refs[n:2 * n]
        send_sems, recv_sems, local_sems = refs[2 * n:]
        x, y, c = lax.axis_index("x"), lax.axis_index("y"), lax.axis_index("c")
        chip = 2 * x + y
        sibling = (x, y, 1 - c)
        chips = [(1 - x, y), (x, 1 - y), (1 - x, 1 - y)]

        def slot(px, py, pc):
            return 4 * px + 2 * py + pc

        def copy(a, k, src, s, to):
            return pltpu.make_async_remote_copy(
                src_ref=src, dst_ref=dsts[a].at[s], send_sem=send_sems.at[a, k], recv_sem=recv_sems.at[a, k],
                device_id=to, device_id_type=MESH_ID)

        mine = slot(x, y, c)
        local = [pltpu.make_async_copy(srcs[a].at[chip], dsts[a].at[mine], local_sems.at[a]) for a in range(n)]
        for cp in local:
            cp.start()
        first = []
        for a in range(n):
            first.append(copy(a, 0, srcs[a].at[chip], mine, sibling))
            for j, (px, py) in enumerate(chips):
                first.append(copy(a, 1 + j, srcs[a].at[2 * px + py], mine, (px, py, c)))
        for cp in first:
            cp.start()
        passed = []
        for a in range(n):
            for j, (px, py) in enumerate(chips):
                s = slot(px, py, c)
                copy(a, 1 + j, srcs[a].at[chip], s, (x, y, c)).wait_recv()
                fwd = copy(a, 4 + j, dsts[a].at[s], s, sibling)
                fwd.start()
                passed.append(fwd)
        for a in range(n):
            copy(a, 0, srcs[a].at[chip], slot(x, y, 1 - c), (x, y, c)).wait_recv()
            for j, (px, py) in enumerate(chips):
                copy(a, 4 + j, srcs[a].at[chip], slot(px, py, 1 - c), (x, y, c)).wait_recv()
        for cp in first + passed:
            cp.wait_send()
        for cp in local:
            cp.wait()

    hbm = pl.BlockSpec(memory_space=pltpu.HBM)
    return pl.pallas_call(
        body, name=name,
        out_shape=tuple(jax.ShapeDtypeStruct((8,) + a.shape[1:], a.dtype) for a in slabs),
        in_specs=[hbm] * n, out_specs=tuple([hbm] * n),
        scratch_shapes=[pltpu.SemaphoreType.DMA((n, 7)), pltpu.SemaphoreType.DMA((n, 7)), pltpu.SemaphoreType.DMA((n,))],
    )(*slabs)


def _ada(call, w_shard):
    def body(c_ref, w_ref, o_ref):
        o_ref[...] = _mm(c_ref[...].astype(BF16), w_ref[...].astype(BF16))

    return pl.pallas_call(body, name="ada", out_shape=jax.ShapeDtypeStruct((8, w_shard.shape[1]), F32),
                          compiler_params=_cp())(call, w_shard)


def _inproj(x, mod, win, qg, kvg, wq, wk, wv, wkt, wvt, pos, invf):
    S = x.shape[0]
    tm = min(TM, S)

    def body(x_ref, mod_ref, win_ref, qg_ref, kvg_ref, wq_ref, wk_ref, wv_ref, wkt_ref, wvt_ref, pos_ref, invf_ref,
             qlat_ref, ckv_ref, za_ref, xbc_ref, zs_ref, dt_ref, q_ref, k_ref, v_ref, kt_ref, vt_ref):
        shift = mod_ref[0:1, 0:D]
        scale = mod_ref[0:1, D:2 * D]
        u = (x_ref[...] * (1.0 + scale) + shift).astype(BF16)

        def proj(p):
            return _mm(u, win_ref[:, p[0]:p[1]])

        ql = proj(P_Q)
        ckv = proj(P_KV)
        qlat_ref[...] = ql
        ckv_ref[...] = ckv
        za_ref[...] = proj(P_ZA)
        xbc_ref[...] = proj(P_XBC)
        zs_ref[...] = proj(P_ZS)
        dt_ref[...] = proj(P_DT)
        kr = proj(P_KR)

        ang = pos_ref[...].astype(F32) * invf_ref[...]
        cs = jnp.cos(ang)
        sn = jnp.sin(ang)

        rq = lax.rsqrt(jnp.mean(ql * ql, axis=-1, keepdims=True) + RMS_EPS)
        qn = (ql * rq * qg_ref[...]).astype(BF16)
        for h in range(NH):
            qh = _mm(qn, wq_ref[:, h * HP:(h + 1) * HP])
            q_ref[h, :, 0:NOPE] = (qh[:, 0:NOPE] * QSCALE).astype(BF16)
            q_ref[h, :, NOPE:HP] = (_rope(qh[:, NOPE:HP], cs, sn) * QSCALE).astype(BF16)

        rk = lax.rsqrt(jnp.mean(ckv * ckv, axis=-1, keepdims=True) + RMS_EPS)
        kn = (ckv * rk * kvg_ref[...]).astype(BF16)
        knope = _mm(kn, wk_ref[...])
        vall = _mm(kn, wv_ref[...])
        krf = _rope(kr, cs, sn)
        krr = krf.astype(BF16)
        krt = krf.T.astype(BF16)
        ones_rows = jnp.where(lax.broadcasted_iota(jnp.int32, (VDP - VD, tm), 0) == 0, 1.0, 0.0).astype(BF16)
        for h in range(NH):
            k_ref[h, :, 0:NOPE] = knope[:, h * NOPE:(h + 1) * NOPE].astype(BF16)
            k_ref[h, :, NOPE:HP] = krr
            v_ref[h] = vall[:, h * VD:(h + 1) * VD].astype(BF16)
            kt_ref[h, 0:NOPE, :] = _nt(wkt_ref[h * NOPE:(h + 1) * NOPE, :], kn).astype(BF16)
            kt_ref[h, NOPE:HP, :] = krt
            vt_ref[h, 0:VD, :] = _nt(wvt_ref[h * VD:(h + 1) * VD, :], kn).astype(BF16)
            vt_ref[h, VD:VDP, :] = ones_rows

    row = lambda n: pl.BlockSpec((tm, n), lambda i: (i, 0))
    heads = lambda n: pl.BlockSpec((NH, tm, n), lambda i: (0, i, 0))
    heads_t = lambda n: pl.BlockSpec((NH, None, n, tm), lambda i: (0, i, 0, 0))
    sd = lambda n: jax.ShapeDtypeStruct((S, n), F32)
    hd = lambda n: jax.ShapeDtypeStruct((NH, S, n), BF16)
    ht = lambda n: jax.ShapeDtypeStruct((NH, S // tm, n, tm), BF16)
    return pl.pallas_call(
        body, name="inproj", grid=(S // tm,),
        in_specs=[row(D), _const((8, 3 * D)), _const((D, IN_P)), _const((1, QL)), _const((1, KVL)),
                  _const((QL, NH * HP)), _const((KVL, NH * NOPE)), _const((KVL, NH * VD)),
                  _const((NH * NOPE, KVL)), _const((NH * VD, KVL)), row(1), _const((1, 128))],
        out_specs=[row(QL), row(KVL), row(D), row(CC), row(D), row(128), heads(HP), heads(HP), heads(VD),
                   heads_t(HP), heads_t(VDP)],
        out_shape=[sd(QL), sd(KVL), sd(D), sd(CC), sd(D), sd(128), hd(HP), hd(HP), hd(VD), ht(HP), ht(VDP)],
        compiler_params=_cp(("arbitrary",)),
    )(x, mod, win, qg, kvg, wq, wk, wv, wkt, wvt, pos, invf)


def _attn_fwd(q, k, vt):
    _, S, _ = q.shape
    tq = min(TQF, S)
    nq = S // tq
    half = tq // 2
    tb = vt.shape[3]
    nsb = half // tb

    def body(q_ref, k_ref, vt_ref, o_ref, lse_ref):
        i = pl.program_id(1)
        qb = q_ref[...]

        def scores(j, hb):
            off = pl.multiple_of(j * tq + hb * half, half)
            return _nt(k_ref[pl.ds(off, half), :], qb)

        def update(j, hb, s, carry):
            m, acc = carry
            m_new = jnp.maximum(m, jnp.max(s, axis=0, keepdims=True))
            a = jnp.exp2(m - m_new)
            pb = jnp.exp2(s - m_new).astype(BF16)
            acc = a * acc
            for sb in range(nsb):
                acc = acc + _mm(vt_ref[(2 * j + hb) * nsb + sb], pb[sb * tb:(sb + 1) * tb, :])
            return m_new, acc

        def trip(j, carry, masked):
            s = [scores(j, hb) for hb in range(2)]
            if masked:
                r = lax.broadcasted_iota(jnp.int32, s[0].shape, 0)
                cidx = lax.broadcasted_iota(jnp.int32, s[0].shape, 1)
                s = [jnp.where(cidx >= r + hb * half, s[hb], -1e30) for hb in range(2)]
            for hb in range(2):
                carry = update(j, hb, s[hb], carry)
            return carry

        init = (jnp.full((1, tq), -1e30, F32), jnp.zeros((VDP, tq), F32))
        carry = lax.fori_loop(0, i, lambda j, cr: trip(j, cr, False), init)
        m, acc = trip(i, carry, True)
        l = acc[VD:VD + 1, :]
        o_ref[...] = (acc[0:VD, :] / l).T
        lse_ref[...] = m + jnp.log2(l)

    return pl.pallas_call(
        body, name="attn_fwd", grid=(NH, nq),
        in_specs=[pl.BlockSpec((None, tq, HP), lambda h, i: (h, i, 0)),
                  pl.BlockSpec((None, S, HP), lambda h, i: (h, 0, 0)),
                  pl.BlockSpec((None, S // tb, VDP, tb), lambda h, i: (h, 0, 0, 0))],
        out_specs=[pl.BlockSpec((tq, VD), lambda h, i: (i, h)),
                   pl.BlockSpec((None, None, 1, tq), lambda h, i: (h, i, 0, 0))],
        out_shape=[jax.ShapeDtypeStruct((S, NH * VD), F32), jax.ShapeDtypeStruct((NH, nq, 1, tq), F32)],
        compiler_params=_cp(("arbitrary", "arbitrary")),
    )(q, k, vt)


def _ssd_consts():
    tri = np.tril(np.ones((CH, CH), np.float32))
    e16 = np.zeros((128, SW), np.float32)
    for h in range(SH):
        e16[h, h * SP:(h + 1) * SP] = 1.0
    return jnp.asarray(tri), jnp.asarray(tri.T.copy()), jnp.asarray(e16)


def _ssd_chunk_fwd_common(xbc_ref, halo_ref, dtraw_ref, cw_ref, cb_ref, dtb_ref, alog_ref, tri_ref, e16_ref, ext, first):
    ext[0:8, :] = jnp.where(first, 0.0, halo_ref[...])
    ext[8:8 + CH, :] = xbc_ref[...]
    cw = cw_ref[...]
    xc = cb_ref[...] + cw[0:1, :] * ext[5:5 + CH, :]
    for kk in range(1, CW):
        xc = xc + cw[kk:kk + 1, :] * ext[5 + kk:5 + kk + CH, :]
    sact = _sigmoid(xc)
    act = xc * sact
    lane = lax.broadcasted_iota(jnp.int32, (1, 128), 1)
    arow = jnp.where(lane < SH, -jnp.exp(alog_ref[...]), 0.0)
    dtpre = dtraw_ref[...] + dtb_ref[...]
    dt = _softplus(dtpre)
    a = dt * arow
    cum = _mm_hi(tri_ref[...], a)
    cumx = _mm_hi(cum, e16_ref[...])
    dtx = _mm_hi(dt, e16_ref[...])
    return xc, sact, act, arow, dtpre, dt, cum, cumx, dtx


def _ssd_fwd(xbc, dtraw, zs, conv_w, conv_b, dtb, alog, dskx, gssm):
    S = xbc.shape[0]
    nc = S // CH
    tri, _, e16 = _ssd_consts()

    def body(xbc_ref, halo_ref, dtraw_ref, zs_ref, cw_ref, cb_ref, dtb_ref, alog_ref, dsk_ref, g_ref, tri_ref, e16_ref,
             y_ref, htp_ref, o_ref, ht, ext):
        i = pl.program_id(0)

        @pl.when(i == 0)
        def _():
            ht[...] = jnp.zeros_like(ht)

        xc, sact, act, arow, dtpre, dt, cum, cumx, dtx = _ssd_chunk_fwd_common(
            xbc_ref, halo_ref, dtraw_ref, cw_ref, cb_ref, dtb_ref, alog_ref, tri_ref, e16_ref, ext, i == 0)
        cum_t = cum.T
        xs = act[:, 0:SW]
        lastx = cumx[CH - 1:CH, :]
        xh = xs * dtx
        eexp = jnp.exp(cumx)
        dte = jnp.exp(lastx - cumx)
        cdx = jnp.exp(lastx)
        htp = ht[...]
        htp_ref[...] = htp
        xw = (xh * dte).astype(BF16)
        xb = xh.astype(BF16)
        trim = tri_ref[...] > 0.5
        lane = lax.broadcasted_iota(jnp.int32, (CH, 128), 1)
        parts = []
        for g in range(SG):
            gl = slice(g * GW, (g + 1) * GW)
            bg = act[:, SW + g * SN:SW + (g + 1) * SN].astype(BF16)
            cg = act[:, SW + SG * SN + g * SN:SW + SG * SN + (g + 1) * SN].astype(BF16)
            cbm = _nt(cg, bg)
            yoff = eexp[:, gl] * _mm(cg, htp[:, gl].astype(BF16))
            ht[:, gl] = htp[:, gl] * cdx[:, gl] + _tn(bg, xw[:, gl])
            for pr in range(GW // 128):
                h0 = g * (SH // SG) + 2 * pr
                lo = g * GW + pr * 128
                xp = xb[:, lo:lo + 128]
                res = []
                for hh in (h0, h0 + 1):
                    seg = cum[:, hh:hh + 1] - cum_t[hh:hh + 1, :]
                    mh = jnp.where(trim, cbm * jnp.exp(seg), 0.0).astype(BF16)
                    res.append(_mm(mh, xp))
                parts.append(jnp.where(lane < SP, res[0], res[1]) + yoff[:, pr * 128:(pr + 1) * 128])
        y = jnp.concatenate(parts, axis=1) + xs * dsk_ref[...]
        y_ref[...] = y
        z = zs_ref[...]
        hf = y * (z * _sigmoid(z))
        outs = []
        for g in range(SG):
            hg = hf[:, g * GW:(g + 1) * GW]
            rs = lax.rsqrt(jnp.mean(hg * hg, axis=-1, keepdims=True) + RMS_EPS)
            outs.append(hg * rs)
        o_ref[...] = (jnp.concatenate(outs, axis=1) * g_ref[...]).astype(BF16)

    row = lambda n: pl.BlockSpec((CH, n), lambda i: (i, 0))
    return pl.pallas_call(
        body, name="ssd_fwd", grid=(nc,),
        in_specs=[row(CC), pl.BlockSpec((8, CC), lambda i: (jnp.maximum(i * (CH // 8) - 1, 0), 0)), row(128), row(SW),
                  _const((CW, CC)), _const((1, CC)), _const((1, 128)), _const((1, 128)), _const((1, SW)), _const((1, SW)),
                  _const((CH, CH)), _const((128, SW))],
        out_specs=[row(SW), pl.BlockSpec((None, SN, SW), lambda i: (i, 0, 0)), row(SW)],
        out_shape=[jax.ShapeDtypeStruct((S, SW), F32), jax.ShapeDtypeStruct((nc, SN, SW), F32),
                   jax.ShapeDtypeStruct((S, SW), BF16)],
        scratch_shapes=[pltpu.VMEM((SN, SW), F32), pltpu.VMEM((8 + CH, CC), F32)],
        compiler_params=_cp(("arbitrary",)),
    )(xbc, xbc, dtraw, zs, conv_w, conv_b, dtb, alog, dskx, gssm, tri, e16)


def _outproj(o, za, ossm, x, tgt, wout, mod, ln_g, ln_b):
    S = x.shape[0]
    tm = min(TM, S)

    def body(o_ref, za_ref, os_ref, x_ref, t_ref, w_ref, mod_ref, g_ref, b_ref,
             gx_ref, do_ref, dza_ref, dos_ref, delta_ref, dw_ref, vec_ref):
        i = pl.program_id(0)

        @pl.when(i == 0)
        def _():
            dw_ref[...] = jnp.zeros_like(dw_ref)
            vec_ref[...] = jnp.zeros_like(vec_ref)

        gate = mod_ref[0:1, 2 * D:3 * D]
        ov = o_ref[...]
        z = za_ref[...]
        sz = _sigmoid(z)
        silz = z * sz
        a = (ov * silz).astype(BF16)
        osb = os_ref[...]
        mixed = _mm(a, w_ref[0:D, :]) + _mm(osb, w_ref[D:MIX, :])
        xv = x_ref[...]
        hres = ALPHA * xv + gate * mixed
        mu = jnp.mean(hres, axis=-1, keepdims=True)
        hc = hres - mu
        var = jnp.mean(hc * hc, axis=-1, keepdims=True)
        rstd = lax.rsqrt(var + LN_EPS)
        xhat = hc * rstd
        g = g_ref[...]
        yv = xhat * g + b_ref[...]
        err = yv - t_ref[...]
        dy = err * (1.0 / D)
        vec_ref[0:1, :] += jnp.sum(err * err, axis=0, keepdims=True)
        vec_ref[1:2, :] += jnp.sum(dy * xhat, axis=0, keepdims=True)
        vec_ref[2:3, :] += jnp.sum(dy, axis=0, keepdims=True)
        dxh = dy * g
        dh = rstd * (dxh - jnp.mean(dxh, axis=-1, keepdims=True) - xhat * jnp.mean(dxh * xhat, axis=-1, keepdims=True))
        gx_ref[...] = ALPHA * dh
        vec_ref[3:4, :] += jnp.sum(dh * mixed, axis=0, keepdims=True)
        dmixed = (gate * dh).astype(BF16)
        dw_ref[0:D, :] += _tn(a, dmixed)
        dw_ref[D:MIX, :] += _tn(osb, dmixed)
        da = _nt(dmixed, w_ref[0:D, :])
        dos_ref[...] = _nt(dmixed, w_ref[D:MIX, :])
        dov = da * silz
        do_ref[...] = dov.astype(BF16)
        dza_ref[...] = (da * ov * (sz * (1.0 + z * (1.0 - sz)))).astype(BF16)
        pr = dov * ov
        for h in range(NH):
            delta_ref[h] = jnp.sum(pr[:, h * VD:(h + 1) * VD], axis=-1, keepdims=True)

    row = lambda n: pl.BlockSpec((tm, n), lambda i: (i, 0))
    return pl.pallas_call(
        body, name="outproj", grid=(S // tm,),
        in_specs=[row(D), row(D), row(D), row(D), row(D), _const((MIX, D)), _const((8, 3 * D)), _const((1, D)), _const((1, D))],
        out_specs=[row(D), row(D), row(D), row(D), pl.BlockSpec((NH, tm, 1), lambda i: (0, i, 0)),
                   _full((MIX, D)), _full((8, D))],
        out_shape=[jax.ShapeDtypeStruct((S, D), F32), jax.ShapeDtypeStruct((S, D), BF16), jax.ShapeDtypeStruct((S, D), BF16),
                   jax.ShapeDtypeStruct((S, D), F32), jax.ShapeDtypeStruct((NH, S, 1), F32),
                   jax.ShapeDtypeStruct((MIX, D), F32), jax.ShapeDtypeStruct((8, D), F32)],
        compiler_params=_cp(("arbitrary",)),
    )(o, za, ossm, x, tgt, wout, mod, ln_g, ln_b)


def _attn_bwd(q, k, kt, v, do, lse, delta):
    _, S, _ = q.shape
    tk = min(TQ, S // 2)
    nk = S // tk
    tq = 2 * tk
    nq = S // tq
    tb = kt.shape[3]
    nsb = tk // tb

    def body(k_ref, kt_ref, v_ref, q_ref, do_ref, lse_ref, dl_ref, dk_ref, dv_ref, dqt_ref):
        j = pl.program_id(1)
        kb = k_ref[...]
        vb = v_ref[...]

        @pl.when(j == 0)
        def _():
            dqt_ref[...] = jnp.zeros_like(dqt_ref)

        dk_ref[...] = jnp.zeros_like(dk_ref)
        dv_ref[...] = jnp.zeros_like(dv_ref)

        def step(i, masked, lo=0):
            off = pl.multiple_of(i * tq + lo, tk)
            qb = q_ref[pl.ds(off, tq - lo), :]
            dob = do_ref[pl.ds(off, tq - lo), :]
            pt = jnp.exp2(_nt(kb, qb) - lse_ref[i][:, lo:tq])
            if masked:
                r = lax.broadcasted_iota(jnp.int32, pt.shape, 0)
                cidx = lax.broadcasted_iota(jnp.int32, pt.shape, 1)
                pt = jnp.where(i * tq + lo + cidx >= j * tk + r, pt, 0.0)
            dv_ref[...] += _mm(pt.astype(BF16), dob)
            dsb = (pt * (_nt(vb, dob) - dl_ref[i][:, lo:tq])).astype(BF16)
            dk_ref[...] += _mm(dsb, qb)
            acc = dqt_ref[i, :, lo:tq]
            for sb in range(nsb):
                acc = acc + _mm(kt_ref[sb], dsb[sb * tb:(sb + 1) * tb, :])
            dqt_ref[i, :, lo:tq] = acc

        first = j >> 1

        @pl.when((j & 1) == 0)
        def _():
            step(first, True)

        @pl.when((j & 1) == 1)
        def _():
            step(first, True, tk)

        def loop_body(i, carry):
            step(i, False)
            return carry

        lax.fori_loop(first + 1, nq, loop_body, 0)
        dk_ref[...] = dk_ref[...] * LN2

    return pl.pallas_call(
        body, name="attn_bwd", grid=(NH, nk),
        in_specs=[pl.BlockSpec((None, tk, HP), lambda h, j: (h, j, 0)),
                  pl.BlockSpec((None, nsb, HP, tb), lambda h, j: (h, j, 0, 0)),
                  pl.BlockSpec((None, tk, VD), lambda h, j: (h, j, 0)),
                  pl.BlockSpec((None, S, HP), lambda h, j: (h, 0, 0)),
                  pl.BlockSpec((S, VD), lambda h, j: (0, h)),
                  pl.BlockSpec((None, nq, 1, tq), lambda h, j: (h, 0, 0, 0)),
                  pl.BlockSpec((None, nq, 1, tq), lambda h, j: (h, 0, 0, 0))],
        out_specs=[pl.BlockSpec((None, tk, HP), lambda h, j: (h, j, 0)),
                   pl.BlockSpec((None, tk, VD), lambda h, j: (h, j, 0)),
                   pl.BlockSpec((None, nq, HP, tq), lambda h, j: (h, 0, 0, 0), pipeline_mode=pl.Buffered(1))],
        out_shape=[jax.ShapeDtypeStruct((NH, S, HP), F32), jax.ShapeDtypeStruct((NH, S, VD), F32),
                   jax.ShapeDtypeStruct((NH, nq, HP, tq), F32)],
        compiler_params=_cp(("arbitrary", "arbitrary")),
    )(k, kt, v, q, do, lse.reshape(NH, nq, 1, tq), delta.reshape(NH, nq, 1, tq))


def _ssd_bwd(xbc, dtraw, zs, y, htp, dossm, conv_w, conv_b, dtb, alog, dskx, gssm):
    S = xbc.shape[0]
    nc = S // CH
    tri, triu, e16 = _ssd_consts()

    def body(xbc_ref, halo_ref, dtraw_ref, zs_ref, y_ref, htp_ref, dos_ref,
             cw_ref, cb_ref, dtb_ref, alog_ref, dsk_ref, g_ref, tri_ref, triu_ref, e16_ref,
             dxbc_ref, ddt_ref, dzs_ref, dcw_ref, dcb_ref, dvec_ref, dg_ref,
             dht, ext, dext, dskacc):
        r = pl.program_id(0)
        i = nc - 1 - r

        @pl.when(r == 0)
        def _():
            dht[...] = jnp.zeros_like(dht)
            dext[CH:CH + 8, :] = jnp.zeros((8, CC), F32)
            dskacc[...] = jnp.zeros_like(dskacc)
            dcw_ref[...] = jnp.zeros_like(dcw_ref)
            dcb_ref[...] = jnp.zeros_like(dcb_ref)
            dvec_ref[...] = jnp.zeros_like(dvec_ref)
            dg_ref[...] = jnp.zeros_like(dg_ref)

        xc, sact, act, arow, dtpre, dt, cum, cumx, dtx = _ssd_chunk_fwd_common(
            xbc_ref, halo_ref, dtraw_ref, cw_ref, cb_ref, dtb_ref, alog_ref, tri_ref, e16_ref, ext, i == 0)
        cum_t = cum.T
        xs = act[:, 0:SW]
        lastx = cumx[CH - 1:CH, :]
        xh = xs * dtx
        eexp = jnp.exp(cumx)
        dte = jnp.exp(lastx - cumx)
        cdx = jnp.exp(lastx)
        trim = tri_ref[...] > 0.5
        lane = lax.broadcasted_iota(jnp.int32, (CH, 128), 1)
        rowi = lax.broadcasted_iota(jnp.int32, (CH, 128), 0)

        yv = y_ref[...]
        z = zs_ref[...]
        sz = _sigmoid(z)
        silz = z * sz
        hf = yv * silz
        dn = dos_ref[...] * g_ref[...]
        dhf_parts, nrm_parts = [], []
        for g in range(SG):
            gl = slice(g * GW, (g + 1) * GW)
            hg = hf[:, gl]
            rs = lax.rsqrt(jnp.mean(hg * hg, axis=-1, keepdims=True) + RMS_EPS)
            ng = hg * rs
            dng = dn[:, gl]
            dhf_parts.append(rs * (dng - ng * jnp.mean(dng * ng, axis=-1, keepdims=True)))
            nrm_parts.append(ng)
        nrm = jnp.concatenate(nrm_parts, axis=1)
        dhf = jnp.concatenate(dhf_parts, axis=1)
        dg_ref[...] += jnp.sum(dos_ref[...] * nrm, axis=0, keepdims=True)
        dyv = dhf * silz
        dzs_ref[...] = (dhf * yv * (sz * (1.0 + z * (1.0 - sz)))).astype(BF16)
        dskacc[...] += jnp.sum(dyv * xs, axis=0, keepdims=True)
        dxs_skip = dyv * dsk_ref[...]

        dhtn = dht[...]
        hp = htp_ref[...]
        dlastx = jnp.sum(dhtn * hp, axis=0, keepdims=True) * cdx
        xb = xh.astype(BF16)
        xwf = xh * dte
        dcum = jnp.zeros((CH, 128), F32)
        dcum_t = jnp.zeros((128, CH), F32)
        dxh_parts, dcumx_parts, dlast_parts, db_parts, dc_parts = [], [], [], [], []
        for g in range(SG):
            gl = slice(g * GW, (g + 1) * GW)
            bg = act[:, SW + g * SN:SW + (g + 1) * SN].astype(BF16)
            cg = act[:, SW + SG * SN + g * SN:SW + SG * SN + (g + 1) * SN].astype(BF16)
            hpg = hp[:, gl].astype(BF16)
            dhn = dhtn[:, gl].astype(BF16)
            dyg = dyv[:, gl]
            yoff = eexp[:, gl] * _mm(cg, hpg)
            dz = (dyg * eexp[:, gl]).astype(BF16)
            dcg = _nt(dz, hpg)
            dht[:, gl] = dhtn[:, gl] * cdx[:, gl] + _tn(cg, dz)
            dcumx_g = dyg * yoff
            dbg = _nt(xwf[:, gl].astype(BF16), dhn)
            dxw = _mm(bg, dhn)
            ddte = dxw * xwf[:, gl]
            dcumx_parts.append(dcumx_g - ddte)
            dlast_parts.append(jnp.sum(ddte, axis=0, keepdims=True))
            dxh_g = dxw * dte[:, gl]
            cbm = _nt(cg, bg)
            dcb = jnp.zeros((CH, CH), F32)
            dxp_parts = []
            for pr in range(GW // 128):
                h0 = g * (SH // SG) + 2 * pr
                lo = g * GW + pr * 128
                xp = xb[:, lo:lo + 128]
                dyp = dyv[:, lo:lo + 128]
                dxp = jnp.zeros((CH, 128), F32)
                for idx, hh in enumerate((h0, h0 + 1)):
                    decay = jnp.where(trim, jnp.exp(cum[:, hh:hh + 1] - cum_t[hh:hh + 1, :]), 0.0)
                    mh = cbm * decay
                    keep = (lane < SP) if idx == 0 else (lane >= SP)
                    dym = jnp.where(keep, dyp, 0.0).astype(BF16)
                    dm = _nt(dym, xp)
                    dxp = dxp + _tn(mh.astype(BF16), dym)
                    gm = dm * mh
                    dcum = dcum + jnp.where(lane == hh, jnp.sum(gm, axis=1, keepdims=True), 0.0)
                    dcum_t = dcum_t - jnp.where(rowi == hh, jnp.sum(gm, axis=0, keepdims=True), 0.0)
                    dcb = dcb + dm * decay
                dxp_parts.append(dxp)
            dxh_parts.append(dxh_g + jnp.concatenate(dxp_parts, axis=1))
            dcbb = dcb.astype(BF16)
            dc_parts.append(dcg + _mm(dcbb, bg))
            db_parts.append(dbg + _tn(dcbb, cg))
        dxh = jnp.concatenate(dxh_parts, axis=1)
        dcumx = jnp.concatenate(dcumx_parts, axis=1)
        dlastx = dlastx + jnp.concatenate(dlast_parts, axis=1)
        e16 = e16_ref[...]
        dlast128 = _nt_hi(jnp.broadcast_to(dlastx, (8, SW)), e16)[0:1, :]
        dcum = dcum + dcum_t.T + _nt_hi(dcumx, e16) + jnp.where(rowi == CH - 1, dlast128, 0.0)
        da = _mm_hi(triu_ref[...], dcum)
        ddt = da * arow + _nt_hi(dxh * xs, e16)
        dvec_ref[1:2, :] += jnp.sum(da * dt, axis=0, keepdims=True)
        ddtraw = jnp.where(lane < SH, ddt * _sigmoid(dtpre), 0.0)
        dvec_ref[0:1, :] += jnp.sum(ddtraw, axis=0, keepdims=True)
        ddt_ref[...] = ddtraw.astype(BF16)
        dxs = dxs_skip + dxh * dtx
        dact = jnp.concatenate([dxs] + db_parts + dc_parts, axis=1)
        dxc = dact * (sact * (1.0 + xc * (1.0 - sact)))

        dcb_ref[...] += jnp.sum(dxc, axis=0, keepdims=True)
        for kk in range(CW):
            dcw_ref[kk:kk + 1, :] += jnp.sum(dxc * ext[5 + kk:5 + kk + CH, :], axis=0, keepdims=True)
        dext[0:CH, :] = dxc
        cw = cw_ref[...]
        dxr = cw[CW - 1:CW, :] * dxc
        for kk in range(CW - 1):
            dxr = dxr + cw[kk:kk + 1, :] * dext[CW - 1 - kk:CW - 1 - kk + CH, :]
        dxbc_ref[...] = dxr.astype(BF16)
        dext[CH:CH + 8, :] = dxc[0:8, :]

        @pl.when(r == nc - 1)
        def _():
            dvec_ref[1:2, :] = dvec_ref[1:2, :] * arow
            dvec_ref[2:3, :] = _nt_hi(jnp.broadcast_to(dskacc[...], (8, SW)), e16)[0:1, :]

    rev = lambda n: pl.BlockSpec((CH, n), lambda r: (nc - 1 - r, 0))
    return pl.pallas_call(
        body, name="ssd_bwd", grid=(nc,),
        in_specs=[rev(CC), pl.BlockSpec((8, CC), lambda r: (jnp.maximum((nc - 1 - r) * (CH // 8) - 1, 0), 0)),
                  rev(128), rev(SW), rev(SW), pl.BlockSpec((None, SN, SW), lambda r: (nc - 1 - r, 0, 0)), rev(SW),
                  _const((CW, CC)), _const((1, CC)), _const((1, 128)), _const((1, 128)), _const((1, SW)), _const((1, SW)),
                  _const((CH, CH)), _const((CH, CH)), _const((128, SW))],
        out_specs=[rev(CC), rev(128), rev(SW), _full((CW, CC)), _full((1, CC)), _full((8, 128)), _full((1, SW))],
        out_shape=[jax.ShapeDtypeStruct((S, CC), BF16), jax.ShapeDtypeStruct((S, 128), BF16), jax.ShapeDtypeStruct((S, SW), BF16),
                   jax.ShapeDtypeStruct((CW, CC), F32), jax.ShapeDtypeStruct((1, CC), F32),
                   jax.ShapeDtypeStruct((8, 128), F32), jax.ShapeDtypeStruct((1, SW), F32)],
        scratch_shapes=[pltpu.VMEM((SN, SW), F32), pltpu.VMEM((8 + CH, CC), F32), pltpu.VMEM((CH + 8, CC), F32),
                        pltpu.VMEM((1, SW), F32)],
        compiler_params=_cp(("arbitrary",)),
    )(xbc, xbc, dtraw, zs, y, htp, dossm, conv_w, conv_b, dtb, alog, dskx, gssm, tri, triu, e16)


def _mla_bwd(dqt, dk, dv, qlat, ckv, qg, kvg, wq, wk, wv, pos, invf):
    S = qlat.shape[0]
    tm = min(TQ, dqt.shape[3])
    per = dqt.shape[3] // tm

    def body(dq_ref, dk_ref, dv_ref, ql_ref, ckv_ref, qg_ref, kvg_ref, wq_ref, wk_ref, wv_ref, pos_ref, invf_ref,
             dql_ref, dckv_ref, dkr_ref, dwq_ref, dwk_ref, dwv_ref, dqg_ref, dkvg_ref):
        i = pl.program_id(0)

        @pl.when(i == 0)
        def _():
            dwq_ref[...] = jnp.zeros_like(dwq_ref)
            dwk_ref[...] = jnp.zeros_like(dwk_ref)
            dwv_ref[...] = jnp.zeros_like(dwv_ref)
            dqg_ref[...] = jnp.zeros_like(dqg_ref)
            dkvg_ref[...] = jnp.zeros_like(dkvg_ref)

        ang = pos_ref[...].astype(F32) * invf_ref[...]
        cs = jnp.cos(ang)
        sn = jnp.sin(ang)

        def rms_bwd(v, g, dn, dg_ref):
            r = lax.rsqrt(jnp.mean(v * v, axis=-1, keepdims=True) + RMS_EPS)
            vh = v * r
            dg_ref[...] += jnp.sum(dn * vh, axis=0, keepdims=True)
            dvh = dn * g
            return vh, r * (dvh - vh * jnp.mean(dvh * vh, axis=-1, keepdims=True))

        pieces = []
        for h in range(NH):
            dqh = dq_ref[h].T
            pieces.append(dqh[:, 0:NOPE] * SCALE)
            pieces.append(_rope_t(dqh[:, NOPE:HP], cs, sn) * SCALE)
        dqf = jnp.concatenate(pieces, axis=1).astype(BF16)
        ql = ql_ref[...]
        g = qg_ref[...]
        dqn = _nt(dqf, wq_ref[...])
        qh, dql = rms_bwd(ql, g, dqn, dqg_ref)
        dwq_ref[...] += _tn((qh * g).astype(BF16), dqf)
        dql_ref[...] = dql.astype(BF16)

        dkn_p = jnp.concatenate([dk_ref[h, :, 0:NOPE] for h in range(NH)], axis=1).astype(BF16)
        dvf = jnp.concatenate([dv_ref[h] for h in range(NH)], axis=1).astype(BF16)
        dkr = dk_ref[0, :, NOPE:HP]
        for h in range(1, NH):
            dkr = dkr + dk_ref[h, :, NOPE:HP]
        lane = lax.broadcasted_iota(jnp.int32, dkr.shape, 1)
        dkr_ref[...] = jnp.where(lane < ROPE, _rope_t(dkr, cs, sn), 0.0).astype(BF16)
        cv = ckv_ref[...]
        gk = kvg_ref[...]
        dkn = _nt(dkn_p, wk_ref[...]) + _nt(dvf, wv_ref[...])
        kh, dckv = rms_bwd(cv, gk, dkn, dkvg_ref)
        knb = (kh * gk).astype(BF16)
        dwk_ref[...] += _tn(knb, dkn_p)
        dwv_ref[...] += _tn(knb, dvf)
        dckv_ref[...] = dckv.astype(BF16)

    row = lambda n: pl.BlockSpec((tm, n), lambda i: (i, 0))
    heads = lambda n: pl.BlockSpec((NH, tm, n), lambda i: (0, i, 0))
    return pl.pallas_call(
        body, name="mla_bwd", grid=(S // tm,),
        in_specs=[pl.BlockSpec((NH, None, HP, tm), lambda i: (0, i // per, 0, i % per)), heads(HP), heads(VD), row(QL), row(KVL), _const((1, QL)), _const((1, KVL)),
                  _const((QL, NH * HP)), _const((KVL, NH * NOPE)), _const((KVL, NH * VD)), row(1), _const((1, 128))],
        out_specs=[row(QL), row(KVL), row(128), _full((QL, NH * HP)), _full((KVL, NH * NOPE)), _full((KVL, NH * VD)),
                   _full((1, QL)), _full((1, KVL))],
        out_shape=[jax.ShapeDtypeStruct((S, QL), BF16), jax.ShapeDtypeStruct((S, KVL), BF16), jax.ShapeDtypeStruct((S, 128), BF16),
                   jax.ShapeDtypeStruct((QL, NH * HP), F32), jax.ShapeDtypeStruct((KVL, NH * NOPE), F32),
                   jax.ShapeDtypeStruct((KVL, NH * VD), F32), jax.ShapeDtypeStruct((1, QL), F32), jax.ShapeDtypeStruct((1, KVL), F32)],
        compiler_params=_cp(("arbitrary",)),
    )(dqt, dk, dv, qlat, ckv, qg, kvg, wq, wk, wv, pos, invf)


def _inproj_bwd(x, gx1, mod, win, dql, dckv, dza, dxbc, dzs, dkr, ddt):
    S = x.shape[0]
    tm = min(TM, S)

    def body(x_ref, gx1_ref, mod_ref, win_ref, dql_ref, dckv_ref, dza_ref, dxbc_ref, dzs_ref, dkr_ref, ddt_ref,
             gx_ref, dw_ref, vec_ref):
        i = pl.program_id(0)

        @pl.when(i == 0)
        def _():
            dw_ref[...] = jnp.zeros_like(dw_ref)
            vec_ref[...] = jnp.zeros_like(vec_ref)

        shift = mod_ref[0:1, 0:D]
        scale = mod_ref[0:1, D:2 * D]
        xv = x_ref[...]
        ut = (xv * (1.0 + scale) + shift).T.astype(BF16)
        pieces = (dql_ref, dckv_ref, dza_ref, dxbc_ref, dzs_ref, dkr_ref, ddt_ref)
        du = jnp.zeros((tm, D), F32)
        lo = 0
        for p_ref in pieces:
            n = p_ref.shape[1]
            dp = p_ref[...]
            du = du + _nt(dp, win_ref[:, lo:lo + n])
            dw_ref[:, lo:lo + n] += _mm(ut, dp)
            lo += n
        vec_ref[0:1, :] += jnp.sum(du, axis=0, keepdims=True)
        vec_ref[1:2, :] += jnp.sum(du * xv, axis=0, keepdims=True)
        gx_ref[...] = gx1_ref[...] + du * (1.0 + scale)

    row = lambda n: pl.BlockSpec((tm, n), lambda i: (i, 0))
    return pl.pallas_call(
        body, name="inproj_bwd", grid=(S // tm,),
        in_specs=[row(D), row(D), _const((8, 3 * D)), _const((D, IN_P)), row(QL), row(KVL), row(D), row(CC), row(D),
                  row(128), row(128)],
        out_specs=[row(D), pl.BlockSpec((D, IN_P), lambda i: (0, 0), pipeline_mode=pl.Buffered(1)), _full((8, D))],
        out_shape=[jax.ShapeDtypeStruct((S, D), F32), jax.ShapeDtypeStruct((D, IN_P), F32), jax.ShapeDtypeStruct((8, D), F32)],
        compiler_params=_cp(("arbitrary",)),
    )(x, gx1, mod, win, dql, dckv, dza, dxbc, dzs, dkr, ddt)


def _ada_bwd(callt, dmods):
    w = dmods.shape[1]

    def body(c_ref, d_ref, o_ref):
        acc = c_ref[:, 0:1] * d_ref[0:1, :]
        for s in range(1, 8):
            acc = acc + c_ref[:, s:s + 1] * d_ref[s:s + 1, :]
        o_ref[0] = acc

    return pl.pallas_call(body, name="ada_bwd", out_shape=jax.ShapeDtypeStruct((1, D, w), F32),
                          compiler_params=_cp())(callt, dmods)


def _adamw(name, parts, w, m, v):
    rows, ncol = w.shape
    tr = min(rows, 128)
    nparts = parts.shape[0]

    def body(p_ref, w_ref, m_ref, v_ref, g_ref, d_ref, nm_ref, nv_ref):
        g = p_ref[0].astype(F32)
        for s in range(1, nparts):
            g = g + p_ref[s].astype(F32)
        g_ref[...] = g
        nm = B1 * m_ref[...] + (1.0 - B1) * g
        nv = B2 * v_ref[...] + (1.0 - B2) * (g * g)
        nm_ref[...] = nm
        nv_ref[...] = nv
        m_hat = nm / (1.0 - B1 ** STEP)
        v_hat = nv / (1.0 - B2 ** STEP)
        d_ref[...] = -LR * (m_hat / (jnp.sqrt(v_hat) + EPS) + WD * w_ref[...])

    row = pl.BlockSpec((tr, ncol), lambda i: (i, 0))
    sd = jax.ShapeDtypeStruct((rows, ncol), F32)
    return pl.pallas_call(
        body, name="adamw_" + name, grid=(rows // tr,),
        in_specs=[pl.BlockSpec((nparts, tr, ncol), lambda i: (0, i, 0)), row, row, row],
        out_specs=[row, row, row, row], out_shape=[sd, sd, sd, sd],
        compiler_params=_cp(("arbitrary",)),
    )(parts, w, m, v)


_SMALL = (("b_ada", 3 * D), ("conv_w", CW * CC // 4), ("conv_b", CC), ("ssm_norm_g", SW), ("ln_g", D), ("ln_b", D),
          ("q_norm_g", QL), ("kv_norm_g", KVL), ("dt_bias", SH), ("a_log", SH), ("d_skip", SH), ("loss", 128))


def _pack_small(d, lead):
    flat = [d[name].reshape(d[name].shape[:lead] + (-1,)) for name, _ in _SMALL]
    used = sum(f.shape[lead] for f in flat)
    pad = jnp.zeros(flat[0].shape[:lead] + (R_SMALL * 1024 - used,), F32)
    return jnp.concatenate(flat + [pad], axis=lead).reshape(flat[0].shape[:lead] + (R_SMALL, 1024))


def _unpack_small(p):
    flat = p.reshape(-1)
    out, r = {}, 0
    for name, n in _SMALL:
        out[name] = flat[r:r + n]
        r += n
    return out


def _in_to_padded(w):
    z = lambda n: jnp.zeros((w.shape[0], n), w.dtype)
    return jnp.concatenate([w[:, 0:384], w[:, 384:640], w[:, 704:1728], w[:, 1728:3264], w[:, 3280:4304],
                            w[:, 640:704], z(64), w[:, 3264:3280], z(112)], axis=1)


def _in_from_padded(g):
    return jnp.concatenate([g[:, 0:384], g[:, 384:640], g[:, P_KR[0]:P_KR[0] + 64], g[:, 640:1664], g[:, 1664:3200],
                            g[:, P_DT[0]:P_DT[0] + 16], g[:, 3200:4224]], axis=1)


def kernel(x, c, positions, w_ada, b_ada, w_in, q_norm_g, w_qb, kv_norm_g, w_kvb, conv_w, conv_b, dt_bias, a_log, d_skip, ssm_norm_g, w_out, ln_g, ln_b, loss_target, m_w_ada, m_b_ada, m_w_in, m_q_norm_g, m_w_qb, m_kv_norm_g, m_w_kvb, m_conv_w, m_conv_b, m_dt_bias, m_a_log, m_d_skip, m_ssm_norm_g, m_w_out, m_ln_g, m_ln_b, v_w_ada, v_b_ada, v_w_in, v_q_norm_g, v_w_qb, v_kv_norm_g, v_w_kvb, v_conv_w, v_conv_b, v_dt_bias, v_a_log, v_d_skip, v_ssm_norm_g, v_w_out, v_ln_g, v_ln_b):
    S = x.shape[1]
    xv = x[0]
    tgt = loss_target[0]

    cw16 = jnp.concatenate([conv_w[0], jnp.zeros((16 - CW, CC // 4), F32)], axis=0)
    f_in, f_qb, f_kvb, f_out, f_cw = _gather_weights(
        [w_in[0].astype(BF16), w_qb[0].astype(BF16), w_kvb[0].astype(BF16), w_out[0].astype(BF16), cw16])
    cat1 = lambda f: f.transpose(1, 0, 2).reshape(f.shape[1], 4 * f.shape[2])
    win = _in_to_padded(cat1(f_in))
    wqb = cat1(f_qb).reshape(QL, NH, QKD)
    wq = jnp.concatenate([wqb, jnp.zeros((QL, NH, HP - QKD), BF16)], axis=2).reshape(QL, NH * HP)
    wkvb = cat1(f_kvb).reshape(KVL, NH, NOPE + VD)
    wk = wkvb[:, :, 0:NOPE].reshape(KVL, NH * NOPE)
    wv = wkvb[:, :, NOPE:].reshape(KVL, NH * VD)
    wout = f_out.reshape(MIX, D)
    cwf = cat1(f_cw[:, 0:CW, :])

    half = ROPE // 2
    invf = 1.0 / (ROPE_THETA ** (jnp.arange(half, dtype=F32) / half))
    invf = jnp.concatenate([invf, invf, jnp.zeros((128 - ROPE,), F32)]).reshape(1, 128)
    pos = positions.reshape(S, 1)
    pad128 = lambda a: jnp.concatenate([a.reshape(1, SH), jnp.zeros((1, 128 - SH), F32)], axis=1)
    dtb, alog = pad128(dt_bias), pad128(a_log)
    dskx = jnp.repeat(d_skip.reshape(SH), SP).reshape(1, SW)

    my_c = lax.axis_index("c")
    (call,) = _exchange("gather_c", [jnp.broadcast_to(c.reshape(1, 1, D), (4, 1, D))])
    call = call.reshape(8, D)
    mods = _ada(call, w_ada[0])
    (mrows,) = _exchange("scatter_mod", [mods.reshape(4, 2, 3 * D // 4)])
    mine = lax.dynamic_index_in_dim(mrows.reshape(4, 2, 2, 3 * D // 4)[:, 0], my_c, axis=1, keepdims=False)
    mod = jnp.broadcast_to(mine.reshape(1, 3 * D) + b_ada, (8, 3 * D))
    qlat, ckv, za, xbc, zs, dtraw, q, k, v, kt, vt = _inproj(xv, mod, win, q_norm_g, kv_norm_g, wq, wk, wv, wk.T, wv.T, pos, invf)
    o, lse = _attn_fwd(q, k, vt)
    y, htp, ossm = _ssd_fwd(xbc, dtraw, zs, cwf, conv_b, dtb, alog, dskx, ssm_norm_g)
    gx1, do, dza, dossm, delta, dwout, vec_o = _outproj(o, za, ossm, xv, tgt, wout, mod, ln_g, ln_b)
    loss_part = jnp.zeros((128,), F32).at[0].set(0.5 / D * jnp.sum(vec_o[0]))

    dk, dv, dq = _attn_bwd(q, k, kt, v, do, lse, delta)
    dxbc, ddt, dzs, dcw, dcb, dvec, dgssm = _ssd_bwd(xbc, dtraw, zs, y, htp, dossm, cwf, conv_b, dtb, alog, dskx, ssm_norm_g)
    dql, dckv, dkr, dwq, dwk, dwv, dqg, dkvg = _mla_bwd(dq, dk, dv, qlat, ckv, q_norm_g, kv_norm_g, wq, wk, wv, pos, invf)
    gx, dwin, vec_i = _inproj_bwd(xv, gx1, mod, win, dql, dckv, dza, dxbc, dzs, dkr, ddt)
    dmod = jnp.concatenate([vec_i[0:1], vec_i[1:2], vec_o[3:4]], axis=1)

    cols = lambda g: g.reshape(g.shape[0], 4, g.shape[1] // 4).transpose(1, 0, 2)
    g_in = cols(_in_from_padded(dwin)).astype(BF16)
    g_qb = cols(dwq.reshape(QL, NH, HP)[:, :, 0:QKD].reshape(QL, NH * QKD)).astype(BF16)
    g_kvb = cols(jnp.concatenate([dwk.reshape(KVL, NH, NOPE), dwv.reshape(KVL, NH, VD)], axis=2)
                 .reshape(KVL, NH * (NOPE + VD))).astype(BF16)
    g_out = dwout.reshape(4, MIX // 4, D).astype(BF16)
    small = {"b_ada": dmod, "conv_b": dcb, "ssm_norm_g": dgssm, "ln_g": vec_o[1:2], "ln_b": vec_o[2:3],
             "q_norm_g": dqg, "kv_norm_g": dkvg, "dt_bias": dvec[0:1, 0:SH], "a_log": dvec[1:2, 0:SH], "d_skip": dvec[2:3, 0:SH],
             "loss": loss_part}
    small = {n: jnp.broadcast_to(a.reshape(1, -1), (4, a.size)) for n, a in small.items()}
    small["conv_w"] = cols(dcw).reshape(4, CW * CC // 4)
    gsmall = _pack_small(small, 1)

    r_in, r_qb, r_kvb, r_out, rs, dmods = _exchange(
        "exchange_grads", [g_in, g_qb, g_kvb, g_out, gsmall, jnp.broadcast_to(dmod.reshape(1, 1, 3 * D), (4, 1, 3 * D))])
    chip = 2 * lax.axis_index("x") + lax.axis_index("y")
    dmods = lax.dynamic_slice_in_dim(dmods.reshape(8, 3 * D), chip * (3 * D // 4), 3 * D // 4, axis=1)
    g_ada = _ada_bwd(call.T, dmods)
    res = dict(w_ada=_adamw("w_ada", g_ada, w_ada[0], m_w_ada[0], v_w_ada[0]),
               w_in=_adamw("w_in", r_in, w_in[0], m_w_in[0], v_w_in[0]),
               w_qb=_adamw("w_qb", r_qb, w_qb[0], m_w_qb[0], v_w_qb[0]),
               w_kvb=_adamw("w_kvb", r_kvb, w_kvb[0], m_w_kvb[0], v_w_kvb[0]),
               w_out=_adamw("w_out", r_out, w_out[0], m_w_out[0], v_w_out[0]))
    wsm = _pack_small(dict(b_ada=b_ada, conv_w=conv_w, conv_b=conv_b, ssm_norm_g=ssm_norm_g, ln_g=ln_g, ln_b=ln_b,
                           q_norm_g=q_norm_g, kv_norm_g=kv_norm_g, dt_bias=dt_bias, a_log=a_log, d_skip=d_skip, loss=jnp.zeros((128,), F32)), 0)
    msm = _pack_small(dict(b_ada=m_b_ada, conv_w=m_conv_w, conv_b=m_conv_b, ssm_norm_g=m_ssm_norm_g, ln_g=m_ln_g, ln_b=m_ln_b,
                           q_norm_g=m_q_norm_g, kv_norm_g=m_kv_norm_g, dt_bias=m_dt_bias, a_log=m_a_log, d_skip=m_d_skip, loss=jnp.zeros((128,), F32)), 0)
    vsm = _pack_small(dict(b_ada=v_b_ada, conv_w=v_conv_w, conv_b=v_conv_b, ssm_norm_g=v_ssm_norm_g, ln_g=v_ln_g, ln_b=v_ln_b,
                           q_norm_g=v_q_norm_g, kv_norm_g=v_kv_norm_g, dt_bias=v_dt_bias, a_log=v_a_log, d_skip=v_d_skip, loss=jnp.zeros((128,), F32)), 0)
    sm = _adamw("small", rs, wsm, msm, vsm)

    order = ["w_ada", "b_ada", "w_in", "q_norm_g", "w_qb", "kv_norm_g", "w_kvb", "conv_w", "conv_b", "dt_bias", "a_log",
             "d_skip", "ssm_norm_g", "w_out", "ln_g", "ln_b"]
    shapes = dict(w_ada=w_ada.shape, b_ada=b_ada.shape, w_in=w_in.shape, q_norm_g=q_norm_g.shape, w_qb=w_qb.shape,
                  kv_norm_g=kv_norm_g.shape, w_kvb=w_kvb.shape, conv_w=conv_w.shape, conv_b=conv_b.shape, dt_bias=dt_bias.shape,
                  a_log=a_log.shape, d_skip=d_skip.shape, ssm_norm_g=ssm_norm_g.shape, w_out=w_out.shape, ln_g=ln_g.shape,
                  ln_b=ln_b.shape)
    outs = []
    for kind in range(4):
        d = _unpack_small(sm[kind])
        d.update({n: r[kind] for n, r in res.items()})
        outs.extend(d[n].reshape(shapes[n]) for n in order)
    loss = _unpack_small(sm[0])["loss"][0]
    return (loss, gx.reshape(x.shape), *outs)
```

```python
import functools
import math

import numpy as np
import jax
import jax.numpy as jnp
from jax import lax
from jax.experimental import pallas as pl
from jax.experimental.pallas import tpu as pltpu

F32 = jnp.float32
BF16 = jnp.bfloat16
HIGHEST = lax.Precision.HIGHEST
MESH_ID = pl.DeviceIdType.MESH

D = 1024
NH = 8
NOPE = 128
ROPE = 64
VD = 128
VDP = 144
QKD = NOPE + ROPE
HP = 256
QL = 384
KVL = 256
ROPE_THETA = 10000.0
SH = 16
SP = 64
SG = 2
SN = 128
CW = 4
CH = 128
SW = SH * SP
CC = SW + 2 * SG * SN
GW = SW // SG
MIX = 2 * D
IN_W = 4304
ALPHA = 2.0 ** 0.25
RMS_EPS = 1e-6
LN_EPS = 1e-5
SCALE = QKD ** -0.5
LN2 = math.log(2.0)
QSCALE = SCALE / LN2
LR, B1, B2, EPS, WD, STEP = 0.001, 0.9, 0.999, 1e-08, 0.01, 10

P_Q = (0, 384)
P_KV = (384, 640)
P_ZA = (640, 1664)
P_XBC = (1664, 3200)
P_ZS = (3200, 4224)
P_KR = (4224, 4352)
P_DT = (4352, 4480)
IN_P = 4480

R_SMALL = 16

TM = 256
TQ = 512
TQF = 1024
VMEM_LIMIT = 56 * 1024 * 1024


def _cp(sem=None):
    return pltpu.CompilerParams(dimension_semantics=sem, vmem_limit_bytes=VMEM_LIMIT)


def _mm(a, b):
    return jnp.dot(a, b, preferred_element_type=F32)


def _nt(a, b):
    return lax.dot_general(a, b, (((1,), (1,)), ((), ())), preferred_element_type=F32)


def _tn(a, b):
    return lax.dot_general(a, b, (((0,), (0,)), ((), ())), preferred_element_type=F32)


def _mm_hi(a, b):
    return jnp.dot(a, b, precision=HIGHEST, preferred_element_type=F32)


def _nt_hi(a, b):
    return lax.dot_general(a, b, (((1,), (1,)), ((), ())), precision=HIGHEST, preferred_element_type=F32)


def _sigmoid(z):
    return 1.0 / (1.0 + jnp.exp(-z))


def _softplus(z):
    return jnp.maximum(z, 0.0) + jnp.log1p(jnp.exp(-jnp.abs(z)))


def _rope(t, cs, sn):
    lane = lax.broadcasted_iota(jnp.int32, t.shape, 1)
    rot = jnp.where(lane < ROPE // 2, -pltpu.roll(t, 128 - ROPE // 2, 1), pltpu.roll(t, ROPE // 2, 1))
    return t * cs + rot * sn


def _rope_t(t, cs, sn):
    lane = lax.broadcasted_iota(jnp.int32, t.shape, 1)
    y = t * sn
    rot = jnp.where(lane < ROPE // 2, -pltpu.roll(y, 128 - ROPE // 2, 1), pltpu.roll(y, ROPE // 2, 1))
    return t * cs - rot


def _full(shape):
    n = len(shape)
    return pl.BlockSpec(shape, lambda *_: (0,) * n)


def _const(shape):
    n = len(shape)
    return pl.BlockSpec(shape, lambda *_: (0,) * n, pipeline_mode=pl.Buffered(1))


def _gather_weights(shards):
    n = len(shards)
    halves = [a.shape[0] // 2 for a in shards]

    def body(*refs):
        srcs, dsts = refs[:n], refs[n:2 * n]
        send_sems, recv_sems, local_sems = refs[2 * n:]
        x, y, c = lax.axis_index("x"), lax.axis_index("y"), lax.axis_index("c")
        me = 2 * x + y
        sibling = (x, y, 1 - c)
        chips = [(1 - x, y), (x, 1 - y), (1 - x, 1 - y)]

        def rows(a, pc):
            return pl.ds(pl.multiple_of(pc * halves[a], halves[a]), halves[a])

        def copy(a, k, src, slot, pc, to):
            return pltpu.make_async_remote_copy(
                src_ref=src, dst_ref=dsts[a].at[slot, rows(a, pc)], send_sem=send_sems.at[a, k],
                recv_sem=recv_sems.at[a, k], device_id=to, device_id_type=MESH_ID)

        local = [pltpu.make_async_copy(srcs[a], dsts[a].at[me], local_sems.at[a]) for a in range(n)]
        for cp in local:
            cp.start()
        sends = [copy(a, j, srcs[a].at[rows(a, c)], me, c, (px, py, c)) for a in range(n) for j, (px, py) in enumerate(chips)]
        for cp in sends:
            cp.start()
        passed = []
        for a in range(n):
            for j, (px, py) in enumerate(chips):
                k = 2 * px + py
                copy(a, j, srcs[a].at[rows(a, c)], k, c, (x, y, c)).wait_recv()
                fwd = copy(a, 3 + j, dsts[a].at[k, rows(a, c)], k, c, sibling)
                fwd.start()
                passed.append(fwd)
        for a in range(n):
            for j, (px, py) in enumerate(chips):
                copy(a, 3 + j, srcs[a].at[rows(a, c)], 2 * px + py, 1 - c, (x, y, c)).wait_recv()
        for cp in sends + passed:
            cp.wait_send()
        for cp in local:
            cp.wait()

    hbm = pl.BlockSpec(memory_space=pltpu.HBM)
    return pl.pallas_call(
        body, name="gather_weights",
        out_shape=tuple(jax.ShapeDtypeStruct((4,) + a.shape, a.dtype) for a in shards),
        in_specs=[hbm] * n, out_specs=tuple([hbm] * n),
        scratch_shapes=[pltpu.SemaphoreType.DMA((n, 6)), pltpu.SemaphoreType.DMA((n, 6)), pltpu.SemaphoreType.DMA((n,))],
    )(*shards)


def _exchange(name, slabs):
    n = len(slabs)

    def body(*refs):
        srcs, dsts = refs[:n], refs[n:2 * n]
        send_sems, recv_sems, local_sems = refs[2 * n:]
        x, y, c = lax.axis_index("x"), lax.axis_index("y"), lax.axis_index("c")
        chip = 2 * x + y
        sibling = (x, y, 1 - c)
        chips = [(1 - x, y), (x, 1 - y), (1 - x, 1 - y)]

        def slot(px, py, pc):
            return 4 * px + 2 * py + pc

        def copy(a, k, src, s, to):
            return pltpu.make_async_remote_copy(
                src_ref=src, dst_ref=dsts[a].at[s], send_sem=send_sems.at[a, k], recv_sem=recv_sems.at[a, k],
                device_id=to, device_id_type=MESH_ID)

        mine = slot(x, y, c)
        local = [pltpu.make_async_copy(srcs[a].at[chip], dsts[a].at[mine], local_sems.at[a]) for a in range(n)]
        for cp in local:
            cp.start()
        first = []
        for a in range(n):
            first.append(copy(a, 0, srcs[a].at[chip], mine, sibling))
            for j, (px, py) in enumerate(chips):
                first.append(copy(a, 1 + j, srcs[a].at[2 * px + py], mine, (px, py, c)))
        for cp in first:
            cp.start()
        passed = []
        for a in range(n):
            for j, (px, py) in enumerate(chips):
                s = slot(px, py, c)
                copy(a, 1 + j, srcs[a].at[chip], s, (x, y, c)).wait_recv()
                fwd = copy(a, 4 + j, dsts[a].at[s], s, sibling)
                fwd.start()
                passed.append(fwd)
        for a in range(n):
            copy(a, 0, srcs[a].at[chip], slot(x, y, 1 - c), (x, y, c)).wait_recv()
            for j, (px, py) in enumerate(chips):
                copy(a, 4 + j, srcs[a].at[chip], slot(px, py, 1 - c), (x, y, c)).wait_recv()
        for cp in first + passed:
            cp.wait_send()
        for cp in local:
            cp.wait()

    hbm = pl.BlockSpec(memory_space=pltpu.HBM)
    return pl.pallas_call(
        body, name=name,
        out_shape=tuple(jax.ShapeDtypeStruct((8,) + a.shape[1:], a.dtype) for a in slabs),
        in_specs=[hbm] * n, out_specs=tuple([hbm] * n),
        scratch_shapes=[pltpu.SemaphoreType.DMA((n, 7)), pltpu.SemaphoreType.DMA((n, 7)), pltpu.SemaphoreType.DMA((n,))],
    )(*slabs)


def _ada(call, w_shard):
    def body(c_ref, w_ref, o_ref):
        o_ref[...] = _mm(c_ref[...].astype(BF16), w_ref[...].astype(BF16))

    return pl.pallas_call(body, name="ada", out_shape=jax.ShapeDtypeStruct((8, w_shard.shape[1]), F32),
                          compiler_params=_cp())(call, w_shard)


def _inproj(x, mod, win, qg, kvg, wq, wk, wv, wkt, wvt, pos, invf):
    S = x.shape[0]
    tm = min(TM, S)

    def body(x_ref, mod_ref, win_ref, qg_ref, kvg_ref, wq_ref, wk_ref, wv_ref, wkt_ref, wvt_ref, pos_ref, invf_ref,
             qlat_ref, ckv_ref, za_ref, xbc_ref, zs_ref, dt_ref, q_ref, k_ref, v_ref, kt_ref, vt_ref):
        shift = mod_ref[0:1, 0:D]
        scale = mod_ref[0:1, D:2 * D]
        u = (x_ref[...] * (1.0 + scale) + shift).astype(BF16)

        def proj(p):
            return _mm(u, win_ref[:, p[0]:p[1]])

        ql = proj(P_Q)
        ckv = proj(P_KV)
        qlat_ref[...] = ql
        ckv_ref[...] = ckv
        za_ref[...] = proj(P_ZA)
        xbc_ref[...] = proj(P_XBC)
        zs_ref[...] = proj(P_ZS)
        dt_ref[...] = proj(P_DT)
        kr = proj(P_KR)

        ang = pos_ref[...].astype(F32) * invf_ref[...]
        cs = jnp.cos(ang)
        sn = jnp.sin(ang)

        rq = lax.rsqrt(jnp.mean(ql * ql, axis=-1, keepdims=True) + RMS_EPS)
        qn = (ql * rq * qg_ref[...]).astype(BF16)
        for h in range(NH):
            qh = _mm(qn, wq_ref[:, h * HP:(h + 1) * HP])
            q_ref[h, :, 0:NOPE] = (qh[:, 0:NOPE] * QSCALE).astype(BF16)
            q_ref[h, :, NOPE:HP] = (_rope(qh[:, NOPE:HP], cs, sn) * QSCALE).astype(BF16)

        rk = lax.rsqrt(jnp.mean(ckv * ckv, axis=-1, keepdims=True) + RMS_EPS)
        kn = (ckv * rk * kvg_ref[...]).astype(BF16)
        knope = _mm(kn, wk_ref[...])
        vall = _mm(kn, wv_ref[...])
        krf = _rope(kr, cs, sn)
        krr = krf.astype(BF16)
        krt = krf.T.astype(BF16)
        ones_rows = jnp.where(lax.broadcasted_iota(jnp.int32, (VDP - VD, tm), 0) == 0, 1.0, 0.0).astype(BF16)
        for h in range(NH):
            k_ref[h, :, 0:NOPE] = knope[:, h * NOPE:(h + 1) * NOPE].astype(BF16)
            k_ref[h, :, NOPE:HP] = krr
            v_ref[h] = vall[:, h * VD:(h + 1) * VD].astype(BF16)
            kt_ref[h, 0:NOPE, :] = _nt(wkt_ref[h * NOPE:(h + 1) * NOPE, :], kn).astype(BF16)
            kt_ref[h, NOPE:HP, :] = krt
            vt_ref[h, 0:VD, :] = _nt(wvt_ref[h * VD:(h + 1) * VD, :], kn).astype(BF16)
            vt_ref[h, VD:VDP, :] = ones_rows

    row = lambda n: pl.BlockSpec((tm, n), lambda i: (i, 0))
    heads = lambda n: pl.BlockSpec((NH, tm, n), lambda i: (0, i, 0))
    heads_t = lambda n: pl.BlockSpec((NH, None, n, tm), lambda i: (0, i, 0, 0))
    sd = lambda n: jax.ShapeDtypeStruct((S, n), F32)
    hd = lambda n: jax.ShapeDtypeStruct((NH, S, n), BF16)
    ht = lambda n: jax.ShapeDtypeStruct((NH, S // tm, n, tm), BF16)
    return pl.pallas_call(
        body, name="inproj", grid=(S // tm,),
        in_specs=[row(D), _const((8, 3 * D)), _const((D, IN_P)), _const((1, QL)), _const((1, KVL)),
                  _const((QL, NH * HP)), _const((KVL, NH * NOPE)), _const((KVL, NH * VD)),
                  _const((NH * NOPE, KVL)), _const((NH * VD, KVL)), row(1), _const((1, 128))],
        out_specs=[row(QL), row(KVL), row(D), row(CC), row(D), row(128), heads(HP), heads(HP), heads(VD),
                   heads_t(HP), heads_t(VDP)],
        out_shape=[sd(QL), sd(KVL), sd(D), sd(CC), sd(D), sd(128), hd(HP), hd(HP), hd(VD), ht(HP), ht(VDP)],
        compiler_params=_cp(("arbitrary",)),
    )(x, mod, win, qg, kvg, wq, wk, wv, wkt, wvt, pos, invf)


def _attn_fwd(q, k, vt):
    _, S, _ = q.shape
    tq = min(TQF, S)
    nq = S // tq
    half = tq // 2
    tb = vt.shape[3]
    nsb = half // tb

    def body(q_ref, k_ref, vt_ref, o_ref, lse_ref):
        i = pl.program_id(1)
        qb = q_ref[...]

        def scores(j, hb):
            off = pl.multiple_of(j * tq + hb * half, half)
            return _nt(k_ref[pl.ds(off, half), :], qb)

        def update(j, hb, s, carry):
            m, acc = carry
            m_new = jnp.maximum(m, jnp.max(s, axis=0, keepdims=True))
            a = jnp.exp2(m - m_new)
            pb = jnp.exp2(s - m_new).astype(BF16)
            acc = a * acc
            for sb in range(nsb):
                acc = acc + _mm(vt_ref[(2 * j + hb) * nsb + sb], pb[sb * tb:(sb + 1) * tb, :])
            return m_new, acc

        def trip(j, carry, masked):
            s = [scores(j, hb) for hb in range(2)]
            if masked:
                r = lax.broadcasted_iota(jnp.int32, s[0].shape, 0)
                cidx = lax.broadcasted_iota(jnp.int32, s[0].shape, 1)
                s = [jnp.where(cidx >= r + hb * half, s[hb], -1e30) for hb in range(2)]
            for hb in range(2):
                carry = update(j, hb, s[hb], carry)
            return carry

        def finish(carry):
            m, acc = carry
            l = acc[VD:VD + 1, :]
            o_ref[...] = (acc[0:VD, :] / l).T
            lse_ref[...] = m + jnp.log2(l)

        init = (jnp.full((1, tq), -1e30, F32), jnp.zeros((VDP, tq), F32))
        carry = lax.fori_loop(0, i >> 1, lambda t, cr: trip(2 * t + 1, trip(2 * t, cr, False), False), init)

        @pl.when((i & 1) == 0)
        def _():
            finish(trip(i, carry, True))

        @pl.when((i & 1) == 1)
        def _():
            finish(trip(i, trip(i - 1, carry, False), True))

    return pl.pallas_call(
        body, name="attn_fwd", grid=(NH, nq),
        in_specs=[pl.BlockSpec((None, tq, HP), lambda h, i: (h, i, 0)),
                  pl.BlockSpec((None, S, HP), lambda h, i: (h, 0, 0)),
                  pl.BlockSpec((None, S // tb, VDP, tb), lambda h, i: (h, 0, 0, 0))],
        out_specs=[pl.BlockSpec((tq, VD), lambda h, i: (i, h)),
                   pl.BlockSpec((None, None, 1, tq), lambda h, i: (h, i, 0, 0))],
        out_shape=[jax.ShapeDtypeStruct((S, NH * VD), F32), jax.ShapeDtypeStruct((NH, nq, 1, tq), F32)],
        compiler_params=_cp(("arbitrary", "arbitrary")),
    )(q, k, vt)


def _ssd_consts():
    tri = np.tril(np.ones((CH, CH), np.float32))
    e16 = np.zeros((128, SW), np.float32)
    for h in range(SH):
        e16[h, h * SP:(h + 1) * SP] = 1.0
    return jnp.asarray(tri), jnp.asarray(tri.T.copy()), jnp.asarray(e16)


def _ssd_chunk_fwd_common(xbc_ref, halo_ref, dtraw_ref, cw_ref, cb_ref, dtb_ref, alog_ref, tri_ref, e16_ref, ext, first):
    ext[0:8, :] = jnp.where(first, 0.0, halo_ref[...])
    ext[8:8 + CH, :] = xbc_ref[...]
    cw = cw_ref[...]
    xc = cb_ref[...] + cw[0:1, :] * ext[5:5 + CH, :]
    for kk in range(1, CW):
        xc = xc + cw[kk:kk + 1, :] * ext[5 + kk:5 + kk + CH, :]
    sact = _sigmoid(xc)
    act = xc * sact
    lane = lax.broadcasted_iota(jnp.int32, (1, 128), 1)
    arow = jnp.where(lane < SH, -jnp.exp(alog_ref[...]), 0.0)
    dtpre = dtraw_ref[...] + dtb_ref[...]
    dt = _softplus(dtpre)
    a = dt * arow
    cum = _mm_hi(tri_ref[...], a)
    cumx = _mm_hi(cum, e16_ref[...])
    dtx = _mm_hi(dt, e16_ref[...])
    return xc, sact, act, arow, dtpre, dt, cum, cumx, dtx


def _ssd_fwd(xbc, dtraw, zs, conv_w, conv_b, dtb, alog, dskx, gssm):
    S = xbc.shape[0]
    nc = S // CH
    tri, _, e16 = _ssd_consts()

    def body(xbc_ref, halo_ref, dtraw_ref, zs_ref, cw_ref, cb_ref, dtb_ref, alog_ref, dsk_ref, g_ref, tri_ref, e16_ref,
             y_ref, htp_ref, o_ref, ht, ext):
        i = pl.program_id(0)

        @pl.when(i == 0)
        def _():
            ht[...] = jnp.zeros_like(ht)

        xc, sact, act, arow, dtpre, dt, cum, cumx, dtx = _ssd_chunk_fwd_common(
            xbc_ref, halo_ref, dtraw_ref, cw_ref, cb_ref, dtb_ref, alog_ref, tri_ref, e16_ref, ext, i == 0)
        cum_t = cum.T
        xs = act[:, 0:SW]
        lastx = cumx[CH - 1:CH, :]
        xh = xs * dtx
        eexp = jnp.exp(cumx)
        dte = jnp.exp(lastx - cumx)
        cdx = jnp.exp(lastx)
        htp = ht[...]
        htp_ref[...] = htp
        xw = (xh * dte).astype(BF16)
        xb = xh.astype(BF16)
        trim = tri_ref[...] > 0.5
        lane = lax.broadcasted_iota(jnp.int32, (CH, 128), 1)
        parts = []
        for g in range(SG):
            gl = slice(g * GW, (g + 1) * GW)
            bg = act[:, SW + g * SN:SW + (g + 1) * SN].astype(BF16)
            cg = act[:, SW + SG * SN + g * SN:SW + SG * SN + (g + 1) * SN].astype(BF16)
            cbm = _nt(cg, bg)
            yoff = eexp[:, gl] * _mm(cg, htp[:, gl].astype(BF16))
            ht[:, gl] = htp[:, gl] * cdx[:, gl] + _tn(bg, xw[:, gl])
            for pr in range(GW // 128):
                h0 = g * (SH // SG) + 2 * pr
                lo = g * GW + pr * 128
                xp = xb[:, lo:lo + 128]
                res = []
                for hh in (h0, h0 + 1):
                    seg = cum[:, hh:hh + 1] - cum_t[hh:hh + 1, :]
                    mh = jnp.where(trim, cbm * jnp.exp(seg), 0.0).astype(BF16)
                    res.append(_mm(mh, xp))
                parts.append(jnp.where(lane < SP, res[0], res[1]) + yoff[:, pr * 128:(pr + 1) * 128])
        y = jnp.concatenate(parts, axis=1) + xs * dsk_ref[...]
        y_ref[...] = y
        z = zs_ref[...]
        hf = y * (z * _sigmoid(z))
        outs = []
        for g in range(SG):
            hg = hf[:, g * GW:(g + 1) * GW]
            rs = lax.rsqrt(jnp.mean(hg * hg, axis=-1, keepdims=True) + RMS_EPS)
            outs.append(hg * rs)
        o_ref[...] = (jnp.concatenate(outs, axis=1) * g_ref[...]).astype(BF16)

    row = lambda n: pl.BlockSpec((CH, n), lambda i: (i, 0))
    return pl.pallas_call(
        body, name="ssd_fwd", grid=(nc,),
        in_specs=[row(CC), pl.BlockSpec((8, CC), lambda i: (jnp.maximum(i * (CH // 8) - 1, 0), 0)), row(128), row(SW),
                  _const((CW, CC)), _const((1, CC)), _const((1, 128)), _const((1, 128)), _const((1, SW)), _const((1, SW)),
                  _const((CH, CH)), _const((128, SW))],
        out_specs=[row(SW), pl.BlockSpec((None, SN, SW), lambda i: (i, 0, 0)), row(SW)],
        out_shape=[jax.ShapeDtypeStruct((S, SW), F32), jax.ShapeDtypeStruct((nc, SN, SW), F32),
                   jax.ShapeDtypeStruct((S, SW), BF16)],
        scratch_shapes=[pltpu.VMEM((SN, SW), F32), pltpu.VMEM((8 + CH, CC), F32)],
        compiler_params=_cp(("arbitrary",)),
    )(xbc, xbc, dtraw, zs, conv_w, conv_b, dtb, alog, dskx, gssm, tri, e16)


def _outproj(o, za, ossm, x, tgt, wout, mod, ln_g, ln_b):
    S = x.shape[0]
    tm = min(TM, S)

    def body(o_ref, za_ref, os_ref, x_ref, t_ref, w_ref, mod_ref, g_ref, b_ref,
             gx_ref, do_ref, dza_ref, dos_ref, delta_ref, dw_ref, vec_ref):
        i = pl.program_id(0)

        @pl.when(i == 0)
        def _():
            dw_ref[...] = jnp.zeros_like(dw_ref)
            vec_ref[...] = jnp.zeros_like(vec_ref)

        gate = mod_ref[0:1, 2 * D:3 * D]
        ov = o_ref[...]
        z = za_ref[...]
        sz = _sigmoid(z)
        silz = z * sz
        a = (ov * silz).astype(BF16)
        osb = os_ref[...]
        mixed = _mm(a, w_ref[0:D, :]) + _mm(osb, w_ref[D:MIX, :])
        xv = x_ref[...]
        hres = ALPHA * xv + gate * mixed
        mu = jnp.mean(hres, axis=-1, keepdims=True)
        hc = hres - mu
        var = jnp.mean(hc * hc, axis=-1, keepdims=True)
        rstd = lax.rsqrt(var + LN_EPS)
        xhat = hc * rstd
        g = g_ref[...]
        yv = xhat * g + b_ref[...]
        err = yv - t_ref[...]
        dy = err * (1.0 / D)
        vec_ref[0:1, :] += jnp.sum(err * err, axis=0, keepdims=True)
        vec_ref[1:2, :] += jnp.sum(dy * xhat, axis=0, keepdims=True)
        vec_ref[2:3, :] += jnp.sum(dy, axis=0, keepdims=True)
        dxh = dy * g
        dh = rstd * (dxh - jnp.mean(dxh, axis=-1, keepdims=True) - xhat * jnp.mean(dxh * xhat, axis=-1, keepdims=True))
        gx_ref[...] = ALPHA * dh
        vec_ref[3:4, :] += jnp.sum(dh * mixed, axis=0, keepdims=True)
        dmixed = (gate * dh).astype(BF16)
        dw_ref[0:D, :] += _tn(a, dmixed)
        dw_ref[D:MIX, :] += _tn(osb, dmixed)
        da = _nt(dmixed, w_ref[0:D, :])
        dos_ref[...] = _nt(dmixed, w_ref[D:MIX, :])
        dov = da * silz
        do_ref[...] = dov.astype(BF16)
        dza_ref[...] = (da * ov * (sz * (1.0 + z * (1.0 - sz)))).astype(BF16)
        pr = dov * ov
        for h in range(NH):
            delta_ref[h] = jnp.sum(pr[:, h * VD:(h + 1) * VD], axis=-1, keepdims=True)

    row = lambda n: pl.BlockSpec((tm, n), lambda i: (i, 0))
    return pl.pallas_call(
        body, name="outproj", grid=(S // tm,),
        in_specs=[row(D), row(D), row(D), row(D), row(D), _const((MIX, D)), _const((8, 3 * D)), _const((1, D)), _const((1, D))],
        out_specs=[row(D), row(D), row(D), row(D), pl.BlockSpec((NH, tm, 1), lambda i: (0, i, 0)),
                   _full((MIX, D)), _full((8, D))],
        out_shape=[jax.ShapeDtypeStruct((S, D), F32), jax.ShapeDtypeStruct((S, D), BF16), jax.ShapeDtypeStruct((S, D), BF16),
                   jax.ShapeDtypeStruct((S, D), F32), jax.ShapeDtypeStruct((NH, S, 1), F32),
                   jax.ShapeDtypeStruct((MIX, D), F32), jax.ShapeDtypeStruct((8, D), F32)],
        compiler_params=_cp(("arbitrary",)),
    )(o, za, ossm, x, tgt, wout, mod, ln_g, ln_b)


def _attn_bwd(q, k, kt, v, do, lse, delta):
    _, S, _ = q.shape
    tk = min(TQ, S // 2)
    nk = S // tk
    tq = 2 * tk
    nq = S // tq
    tb = kt.shape[3]
    nsb = tk // tb

    def body(k_ref, kt_ref, v_ref, q_ref, do_ref, lse_ref, dl_ref, dk_ref, dv_ref, dqt_ref):
        j = pl.program_id(1)
        kb = k_ref[...]
        vb = v_ref[...]

        @pl.when(j == 0)
        def _():
            dqt_ref[...] = jnp.zeros_like(dqt_ref)

        dk_ref[...] = jnp.zeros_like(dk_ref)
        dv_ref[...] = jnp.zeros_like(dv_ref)

        def step(i, masked, lo=0):
            off = pl.multiple_of(i * tq + lo, tk)
            qb = q_ref[pl.ds(off, tq - lo), :]
            dob = do_ref[pl.ds(off, tq - lo), :]
            pt = jnp.exp2(_nt(kb, qb) - lse_ref[i][:, lo:tq])
            if masked:
                r = lax.broadcasted_iota(jnp.int32, pt.shape, 0)
                cidx = lax.broadcasted_iota(jnp.int32, pt.shape, 1)
                pt = jnp.where(i * tq + lo + cidx >= j * tk + r, pt, 0.0)
            dv_ref[...] += _mm(pt.astype(BF16), dob)
            dsb = (pt * (_nt(vb, dob) - dl_ref[i][:, lo:tq])).astype(BF16)
            dk_ref[...] += _mm(dsb, qb)
            acc = dqt_ref[i, :, lo:tq]
            for sb in range(nsb):
                acc = acc + _mm(kt_ref[sb], dsb[sb * tb:(sb + 1) * tb, :])
            dqt_ref[i, :, lo:tq] = acc

        first = j >> 1

        @pl.when((j & 1) == 0)
        def _():
            step(first, True)

        @pl.when((j & 1) == 1)
        def _():
            step(first, True, tk)

        def loop_body(i, carry):
            step(i, False)
            return carry

        lax.fori_loop(first + 1, nq, loop_body, 0)
        dk_ref[...] = dk_ref[...] * LN2

    return pl.pallas_call(
        body, name="attn_bwd", grid=(NH, nk),
        in_specs=[pl.BlockSpec((None, tk, HP), lambda h, j: (h, j, 0)),
                  pl.BlockSpec((None, nsb, HP, tb), lambda h, j: (h, j, 0, 0)),
                  pl.BlockSpec((None, tk, VD), lambda h, j: (h, j, 0)),
                  pl.BlockSpec((None, S, HP), lambda h, j: (h, 0, 0)),
                  pl.BlockSpec((S, VD), lambda h, j: (0, h)),
                  pl.BlockSpec((None, nq, 1, tq), lambda h, j: (h, 0, 0, 0)),
                  pl.BlockSpec((None, nq, 1, tq), lambda h, j: (h, 0, 0, 0))],
        out_specs=[pl.BlockSpec((None, tk, HP), lambda h, j: (h, j, 0)),
                   pl.BlockSpec((None, tk, VD), lambda h, j: (h, j, 0)),
                   pl.BlockSpec((None, nq, HP, tq), lambda h, j: (h, 0, 0, 0), pipeline_mode=pl.Buffered(1))],
        out_shape=[jax.ShapeDtypeStruct((NH, S, HP), F32), jax.ShapeDtypeStruct((NH, S, VD), F32),
                   jax.ShapeDtypeStruct((NH, nq, HP, tq), F32)],
        compiler_params=_cp(("arbitrary", "arbitrary")),
    )(k, kt, v, q, do, lse.reshape(NH, nq, 1, tq), delta.reshape(NH, nq, 1, tq))


def _ssd_bwd(xbc, dtraw, zs, y, htp, dossm, conv_w, conv_b, dtb, alog, dskx, gssm):
    S = xbc.shape[0]
    nc = S // CH
    tri, triu, e16 = _ssd_consts()

    def body(xbc_ref, halo_ref, dtraw_ref, zs_ref, y_ref, htp_ref, dos_ref,
             cw_ref, cb_ref, dtb_ref, alog_ref, dsk_ref, g_ref, tri_ref, triu_ref, e16_ref,
             dxbc_ref, ddt_ref, dzs_ref, dcw_ref, dcb_ref, dvec_ref, dg_ref,
             dht, ext, dext, dskacc):
        r = pl.program_id(0)
        i = nc - 1 - r

        @pl.when(r == 0)
        def _():
            dht[...] = jnp.zeros_like(dht)
            dext[CH:CH + 8, :] = jnp.zeros((8, CC), F32)
            dskacc[...] = jnp.zeros_like(dskacc)
            dcw_ref[...] = jnp.zeros_like(dcw_ref)
            dcb_ref[...] = jnp.zeros_like(dcb_ref)
            dvec_ref[...] = jnp.zeros_like(dvec_ref)
            dg_ref[...] = jnp.zeros_like(dg_ref)

        xc, sact, act, arow, dtpre, dt, cum, cumx, dtx = _ssd_chunk_fwd_common(
            xbc_ref, halo_ref, dtraw_ref, cw_ref, cb_ref, dtb_ref, alog_ref, tri_ref, e16_ref, ext, i == 0)
        cum_t = cum.T
        xs = act[:, 0:SW]
        lastx = cumx[CH - 1:CH, :]
        xh = xs * dtx
        eexp = jnp.exp(cumx)
        dte = jnp.exp(lastx - cumx)
        cdx = jnp.exp(lastx)
        trim = tri_ref[...] > 0.5
        lane = lax.broadcasted_iota(jnp.int32, (CH, 128), 1)
        rowi = lax.broadcasted_iota(jnp.int32, (CH, 128), 0)

        yv = y_ref[...]
        z = zs_ref[...]
        sz = _sigmoid(z)
        silz = z * sz
        hf = yv * silz
        dn = dos_ref[...] * g_ref[...]
        dhf_parts, nrm_parts = [], []
        for g in range(SG):
            gl = slice(g * GW, (g + 1) * GW)
            hg = hf[:, gl]
            rs = lax.rsqrt(jnp.mean(hg * hg, axis=-1, keepdims=True) + RMS_EPS)
            ng = hg * rs
            dng = dn[:, gl]
            dhf_parts.append(rs * (dng - ng * jnp.mean(dng * ng, axis=-1, keepdims=True)))
            nrm_parts.append(ng)
        nrm = jnp.concatenate(nrm_parts, axis=1)
        dhf = jnp.concatenate(dhf_parts, axis=1)
        dg_ref[...] += jnp.sum(dos_ref[...] * nrm, axis=0, keepdims=True)
        dyv = dhf * silz
        dzs_ref[...] = (dhf * yv * (sz * (1.0 + z * (1.0 - sz)))).astype(BF16)
        dskacc[...] += jnp.sum(dyv * xs, axis=0, keepdims=True)
        dxs_skip = dyv * dsk_ref[...]

        dhtn = dht[...]
        hp = htp_ref[...]
        dlastx = jnp.sum(dhtn * hp, axis=0, keepdims=True) * cdx
        xb = xh.astype(BF16)
        xwf = xh * dte
        dcum = jnp.zeros((CH, 128), F32)
        dcum_t = jnp.zeros((128, CH), F32)
        dxh_parts, dcumx_parts, dlast_parts, db_parts, dc_parts = [], [], [], [], []
        for g in range(SG):
            gl = slice(g * GW, (g + 1) * GW)
            bg = act[:, SW + g * SN:SW + (g + 1) * SN].astype(BF16)
            cg = act[:, SW + SG * SN + g * SN:SW + SG * SN + (g + 1) * SN].astype(BF16)
            hpg = hp[:, gl].astype(BF16)
            dhn = dhtn[:, gl].astype(BF16)
            dyg = dyv[:, gl]
            yoff = eexp[:, gl] * _mm(cg, hpg)
            dz = (dyg * eexp[:, gl]).astype(BF16)
            dcg = _nt(dz, hpg)
            dht[:, gl] = dhtn[:, gl] * cdx[:, gl] + _tn(cg, dz)
            dcumx_g = dyg * yoff
            dbg = _nt(xwf[:, gl].astype(BF16), dhn)
            dxw = _mm(bg, dhn)
            ddte = dxw * xwf[:, gl]
            dcumx_parts.append(dcumx_g - ddte)
            dlast_parts.append(jnp.sum(ddte, axis=0, keepdims=True))
            dxh_g = dxw * dte[:, gl]
            cbm = _nt(cg, bg)
            dcb = jnp.zeros((CH, CH), F32)
            dxp_parts = []
            for pr in range(GW // 128):
                h0 = g * (SH // SG) + 2 * pr
                lo = g * GW + pr * 128
                xp = xb[:, lo:lo + 128]
                dyp = dyv[:, lo:lo + 128]
                dxp = jnp.zeros((CH, 128), F32)
                for idx, hh in enumerate((h0, h0 + 1)):
                    decay = jnp.where(trim, jnp.exp(cum[:, hh:hh + 1] - cum_t[hh:hh + 1, :]), 0.0)
                    mh = cbm * decay
                    keep = (lane < SP) if idx == 0 else (lane >= SP)
                    dym = jnp.where(keep, dyp, 0.0).astype(BF16)
                    dm = _nt(dym, xp)
                    dxp = dxp + _tn(mh.astype(BF16), dym)
                    gm = dm * mh
                    dcum = dcum + jnp.where(lane == hh, jnp.sum(gm, axis=1, keepdims=True), 0.0)
                    dcum_t = dcum_t - jnp.where(rowi == hh, jnp.sum(gm, axis=0, keepdims=True), 0.0)
                    dcb = dcb + dm * decay
                dxp_parts.append(dxp)
            dxh_parts.append(dxh_g + jnp.concatenate(dxp_parts, axis=1))
            dcbb = dcb.astype(BF16)
            dc_parts.append(dcg + _mm(dcbb, bg))
            db_parts.append(dbg + _tn(dcbb, cg))
        dxh = jnp.concatenate(dxh_parts, axis=1)
        dcumx = jnp.concatenate(dcumx_parts, axis=1)
        dlastx = dlastx + jnp.concatenate(dlast_parts, axis=1)
        e16 = e16_ref[...]
        dlast128 = _nt_hi(jnp.broadcast_to(dlastx, (8, SW)), e16)[0:1, :]
        dcum = dcum + dcum_t.T + _nt_hi(dcumx, e16) + jnp.where(rowi == CH - 1, dlast128, 0.0)
        da = _mm_hi(triu_ref[...], dcum)
        ddt = da * arow + _nt_hi(dxh * xs, e16)
        dvec_ref[1:2, :] += jnp.sum(da * dt, axis=0, keepdims=True)
        ddtraw = jnp.where(lane < SH, ddt * _sigmoid(dtpre), 0.0)
        dvec_ref[0:1, :] += jnp.sum(ddtraw, axis=0, keepdims=True)
        ddt_ref[...] = ddtraw.astype(BF16)
        dxs = dxs_skip + dxh * dtx
        dact = jnp.concatenate([dxs] + db_parts + dc_parts, axis=1)
        dxc = dact * (sact * (1.0 + xc * (1.0 - sact)))

        dcb_ref[...] += jnp.sum(dxc, axis=0, keepdims=True)
        for kk in range(CW):
            dcw_ref[kk:kk + 1, :] += jnp.sum(dxc * ext[5 + kk:5 + kk + CH, :], axis=0, keepdims=True)
        dext[0:CH, :] = dxc
        cw = cw_ref[...]
        dxr = cw[CW - 1:CW, :] * dxc
        for kk in range(CW - 1):
            dxr = dxr + cw[kk:kk + 1, :] * dext[CW - 1 - kk:CW - 1 - kk + CH, :]
        dxbc_ref[...] = dxr.astype(BF16)
        dext[CH:CH + 8, :] = dxc[0:8, :]

        @pl.when(r == nc - 1)
        def _():
            dvec_ref[1:2, :] = dvec_ref[1:2, :] * arow
            dvec_ref[2:3, :] = _nt_hi(jnp.broadcast_to(dskacc[...], (8, SW)), e16)[0:1, :]

    rev = lambda n: pl.BlockSpec((CH, n), lambda r: (nc - 1 - r, 0))
    return pl.pallas_call(
        body, name="ssd_bwd", grid=(nc,),
        in_specs=[rev(CC), pl.BlockSpec((8, CC), lambda r: (jnp.maximum((nc - 1 - r) * (CH // 8) - 1, 0), 0)),
                  rev(128), rev(SW), rev(SW), pl.BlockSpec((None, SN, SW), lambda r: (nc - 1 - r, 0, 0)), rev(SW),
                  _const((CW, CC)), _const((1, CC)), _const((1, 128)), _const((1, 128)), _const((1, SW)), _const((1, SW)),
                  _const((CH, CH)), _const((CH, CH)), _const((128, SW))],
        out_specs=[rev(CC), rev(128), rev(SW), _full((CW, CC)), _full((1, CC)), _full((8, 128)), _full((1, SW))],
        out_shape=[jax.ShapeDtypeStruct((S, CC), BF16), jax.ShapeDtypeStruct((S, 128), BF16), jax.ShapeDtypeStruct((S, SW), BF16),
                   jax.ShapeDtypeStruct((CW, CC), F32), jax.ShapeDtypeStruct((1, CC), F32),
                   jax.ShapeDtypeStruct((8, 128), F32), jax.ShapeDtypeStruct((1, SW), F32)],
        scratch_shapes=[pltpu.VMEM((SN, SW), F32), pltpu.VMEM((8 + CH, CC), F32), pltpu.VMEM((CH + 8, CC), F32),
                        pltpu.VMEM((1, SW), F32)],
        compiler_params=_cp(("arbitrary",)),
    )(xbc, xbc, dtraw, zs, y, htp, dossm, conv_w, conv_b, dtb, alog, dskx, gssm, tri, triu, e16)


def _mla_bwd(dqt, dk, dv, qlat, ckv, qg, kvg, wq, wk, wv, pos, invf):
    S = qlat.shape[0]
    tm = min(TQ, dqt.shape[3])
    per = dqt.shape[3] // tm

    def body(dq_ref, dk_ref, dv_ref, ql_ref, ckv_ref, qg_ref, kvg_ref, wq_ref, wk_ref, wv_ref, pos_ref, invf_ref,
             dql_ref, dckv_ref, dkr_ref, dwq_ref, dwk_ref, dwv_ref, dqg_ref, dkvg_ref):
        i = pl.program_id(0)

        @pl.when(i == 0)
        def _():
            dwq_ref[...] = jnp.zeros_like(dwq_ref)
            dwk_ref[...] = jnp.zeros_like(dwk_ref)
            dwv_ref[...] = jnp.zeros_like(dwv_ref)
            dqg_ref[...] = jnp.zeros_like(dqg_ref)
            dkvg_ref[...] = jnp.zeros_like(dkvg_ref)

        ang = pos_ref[...].astype(F32) * invf_ref[...]
        cs = jnp.cos(ang)
        sn = jnp.sin(ang)

        def rms_bwd(v, g, dn, dg_ref):
            r = lax.rsqrt(jnp.mean(v * v, axis=-1, keepdims=True) + RMS_EPS)
            vh = v * r
            dg_ref[...] += jnp.sum(dn * vh, axis=0, keepdims=True)
            dvh = dn * g
            return vh, r * (dvh - vh * jnp.mean(dvh * vh, axis=-1, keepdims=True))

        pieces = []
        for h in range(NH):
            dqh = dq_ref[h].T
            pieces.append(dqh[:, 0:NOPE] * SCALE)
            pieces.append(_rope_t(dqh[:, NOPE:HP], cs, sn) * SCALE)
        dqf = jnp.concatenate(pieces, axis=1).astype(BF16)
        ql = ql_ref[...]
        g = qg_ref[...]
        dqn = _nt(dqf, wq_ref[...])
        qh, dql = rms_bwd(ql, g, dqn, dqg_ref)
        dwq_ref[...] += _tn((qh * g).astype(BF16), dqf)
        dql_ref[...] = dql.astype(BF16)

        dkn_p = jnp.concatenate([dk_ref[h, :, 0:NOPE] for h in range(NH)], axis=1).astype(BF16)
        dvf = jnp.concatenate([dv_ref[h] for h in range(NH)], axis=1).astype(BF16)
        dkr = dk_ref[0, :, NOPE:HP]
        for h in range(1, NH):
            dkr = dkr + dk_ref[h, :, NOPE:HP]
        lane = lax.broadcasted_iota(jnp.int32, dkr.shape, 1)
        dkr_ref[...] = jnp.where(lane < ROPE, _rope_t(dkr, cs, sn), 0.0).astype(BF16)
        cv = ckv_ref[...]
        gk = kvg_ref[...]
        dkn = _nt(dkn_p, wk_ref[...]) + _nt(dvf, wv_ref[...])
        kh, dckv = rms_bwd(cv, gk, dkn, dkvg_ref)
        knb = (kh * gk).astype(BF16)
        dwk_ref[...] += _tn(knb, dkn_p)
        dwv_ref[...] += _tn(knb, dvf)
        dckv_ref[...] = dckv.astype(BF16)

    row = lambda n: pl.BlockSpec((tm, n), lambda i: (i, 0))
    heads = lambda n: pl.BlockSpec((NH, tm, n), lambda i: (0, i, 0))
    return pl.pallas_call(
        body, name="mla_bwd", grid=(S // tm,),
        in_specs=[pl.BlockSpec((NH, None, HP, tm), lambda i: (0, i // per, 0, i % per)), heads(HP), heads(VD), row(QL), row(KVL), _const((1, QL)), _const((1, KVL)),
                  _const((QL, NH * HP)), _const((KVL, NH * NOPE)), _const((KVL, NH * VD)), row(1), _const((1, 128))],
        out_specs=[row(QL), row(KVL), row(128), _full((QL, NH * HP)), _full((KVL, NH * NOPE)), _full((KVL, NH * VD)),
                   _full((1, QL)), _full((1, KVL))],
        out_shape=[jax.ShapeDtypeStruct((S, QL), BF16), jax.ShapeDtypeStruct((S, KVL), BF16), jax.ShapeDtypeStruct((S, 128), BF16),
                   jax.ShapeDtypeStruct((QL, NH * HP), F32), jax.ShapeDtypeStruct((KVL, NH * NOPE), F32),
                   jax.ShapeDtypeStruct((KVL, NH * VD), F32), jax.ShapeDtypeStruct((1, QL), F32), jax.ShapeDtypeStruct((1, KVL), F32)],
        compiler_params=_cp(("arbitrary",)),
    )(dqt, dk, dv, qlat, ckv, qg, kvg, wq, wk, wv, pos, invf)


def _inproj_bwd(x, gx1, mod, win, dql, dckv, dza, dxbc, dzs, dkr, ddt):
    S = x.shape[0]
    tm = min(TM, S)

    def body(x_ref, gx1_ref, mod_ref, win_ref, dql_ref, dckv_ref, dza_ref, dxbc_ref, dzs_ref, dkr_ref, ddt_ref,
             gx_ref, dw_ref, vec_ref):
        i = pl.program_id(0)

        @pl.when(i == 0)
        def _():
            dw_ref[...] = jnp.zeros_like(dw_ref)
            vec_ref[...] = jnp.zeros_like(vec_ref)

        shift = mod_ref[0:1, 0:D]
        scale = mod_ref[0:1, D:2 * D]
        xv = x_ref[...]
        ut = (xv * (1.0 + scale) + shift).T.astype(BF16)
        pieces = (dql_ref, dckv_ref, dza_ref, dxbc_ref, dzs_ref, dkr_ref, ddt_ref)
        du = jnp.zeros((tm, D), F32)
        lo = 0
        for p_ref in pieces:
            n = p_ref.shape[1]
            dp = p_ref[...]
            du = du + _nt(dp, win_ref[:, lo:lo + n])
            dw_ref[:, lo:lo + n] += _mm(ut, dp)
            lo += n
        vec_ref[0:1, :] += jnp.sum(du, axis=0, keepdims=True)
        vec_ref[1:2, :] += jnp.sum(du * xv, axis=0, keepdims=True)
        gx_ref[...] = gx1_ref[...] + du * (1.0 + scale)

    row = lambda n: pl.BlockSpec((tm, n), lambda i: (i, 0))
    return pl.pallas_call(
        body, name="inproj_bwd", grid=(S // tm,),
        in_specs=[row(D), row(D), _const((8, 3 * D)), _const((D, IN_P)), row(QL), row(KVL), row(D), row(CC), row(D),
                  row(128), row(128)],
        out_specs=[row(D), pl.BlockSpec((D, IN_P), lambda i: (0, 0), pipeline_mode=pl.Buffered(1)), _full((8, D))],
        out_shape=[jax.ShapeDtypeStruct((S, D), F32), jax.ShapeDtypeStruct((D, IN_P), F32), jax.ShapeDtypeStruct((8, D), F32)],
        compiler_params=_cp(("arbitrary",)),
    )(x, gx1, mod, win, dql, dckv, dza, dxbc, dzs, dkr, ddt)


def _ada_bwd(callt, dmods):
    w = dmods.shape[1]

    def body(c_ref, d_ref, o_ref):
        acc = c_ref[:, 0:1] * d_ref[0:1, :]
        for s in range(1, 8):
            acc = acc + c_ref[:, s:s + 1] * d_ref[s:s + 1, :]
        o_ref[0] = acc

    return pl.pallas_call(body, name="ada_bwd", out_shape=jax.ShapeDtypeStruct((1, D, w), F32),
                          compiler_params=_cp())(callt, dmods)


def _adamw(name, parts, w, m, v):
    rows, ncol = w.shape
    tr = min(rows, 128)
    nparts = parts.shape[0]

    def body(p_ref, w_ref, m_ref, v_ref, g_ref, d_ref, nm_ref, nv_ref):
        g = p_ref[0].astype(F32)
        for s in range(1, nparts):
            g = g + p_ref[s].astype(F32)
        g_ref[...] = g
        nm = B1 * m_ref[...] + (1.0 - B1) * g
        nv = B2 * v_ref[...] + (1.0 - B2) * (g * g)
        nm_ref[...] = nm
        nv_ref[...] = nv
        m_hat = nm / (1.0 - B1 ** STEP)
        v_hat = nv / (1.0 - B2 ** STEP)
        d_ref[...] = -LR * (m_hat / (jnp.sqrt(v_hat) + EPS) + WD * w_ref[...])

    row = pl.BlockSpec((tr, ncol), lambda i: (i, 0))
    sd = jax.ShapeDtypeStruct((rows, ncol), F32)
    return pl.pallas_call(
        body, name="adamw_" + name, grid=(rows // tr,),
        in_specs=[pl.BlockSpec((nparts, tr, ncol), lambda i: (0, i, 0)), row, row, row],
        out_specs=[row, row, row, row], out_shape=[sd, sd, sd, sd],
        compiler_params=_cp(("arbitrary",)),
    )(parts, w, m, v)


_SMALL = (("b_ada", 3 * D), ("conv_w", CW * CC // 4), ("conv_b", CC), ("ssm_norm_g", SW), ("ln_g", D), ("ln_b", D),
          ("q_norm_g", QL), ("kv_norm_g", KVL), ("dt_bias", SH), ("a_log", SH), ("d_skip", SH), ("loss", 128))


def _pack_small(d, lead):
    flat = [d[name].reshape(d[name].shape[:lead] + (-1,)) for name, _ in _SMALL]
    used = sum(f.shape[lead] for f in flat)
    pad = jnp.zeros(flat[0].shape[:lead] + (R_SMALL * 1024 - used,), F32)
    return jnp.concatenate(flat + [pad], axis=lead).reshape(flat[0].shape[:lead] + (R_SMALL, 1024))


def _unpack_small(p):
    flat = p.reshape(-1)
    out, r = {}, 0
    for name, n in _SMALL:
        out[name] = flat[r:r + n]
        r += n
    return out


def _in_to_padded(w):
    z = lambda n: jnp.zeros((w.shape[0], n), w.dtype)
    return jnp.concatenate([w[:, 0:384], w[:, 384:640], w[:, 704:1728], w[:, 1728:3264], w[:, 3280:4304],
                            w[:, 640:704], z(64), w[:, 3264:3280], z(112)], axis=1)


def _in_from_padded(g):
    return jnp.concatenate([g[:, 0:384], g[:, 384:640], g[:, P_KR[0]:P_KR[0] + 64], g[:, 640:1664], g[:, 1664:3200],
                            g[:, P_DT[0]:P_DT[0] + 16], g[:, 3200:4224]], axis=1)


def kernel(x, c, positions, w_ada, b_ada, w_in, q_norm_g, w_qb, kv_norm_g, w_kvb, conv_w, conv_b, dt_bias, a_log, d_skip, ssm_norm_g, w_out, ln_g, ln_b, loss_target, m_w_ada, m_b_ada, m_w_in, m_q_norm_g, m_w_qb, m_kv_norm_g, m_w_kvb, m_conv_w, m_conv_b, m_dt_bias, m_a_log, m_d_skip, m_ssm_norm_g, m_w_out, m_ln_g, m_ln_b, v_w_ada, v_b_ada, v_w_in, v_q_norm_g, v_w_qb, v_kv_norm_g, v_w_kvb, v_conv_w, v_conv_b, v_dt_bias, v_a_log, v_d_skip, v_ssm_norm_g, v_w_out, v_ln_g, v_ln_b):
    S = x.shape[1]
    xv = x[0]
    tgt = loss_target[0]

    cw16 = jnp.concatenate([conv_w[0], jnp.zeros((16 - CW, CC // 4), F32)], axis=0)
    f_in, f_qb, f_kvb, f_out, f_cw = _gather_weights(
        [w_in[0].astype(BF16), w_qb[0].astype(BF16), w_kvb[0].astype(BF16), w_out[0].astype(BF16), cw16])
    cat1 = lambda f: f.transpose(1, 0, 2).reshape(f.shape[1], 4 * f.shape[2])
    win = _in_to_padded(cat1(f_in))
    wqb = cat1(f_qb).reshape(QL, NH, QKD)
    wq = jnp.concatenate([wqb, jnp.zeros((QL, NH, HP - QKD), BF16)], axis=2).reshape(QL, NH * HP)
    wkvb = cat1(f_kvb).reshape(KVL, NH, NOPE + VD)
    wk = wkvb[:, :, 0:NOPE].reshape(KVL, NH * NOPE)
    wv = wkvb[:, :, NOPE:].reshape(KVL, NH * VD)
    wout = f_out.reshape(MIX, D)
    cwf = cat1(f_cw[:, 0:CW, :])

    half = ROPE // 2
    invf = 1.0 / (ROPE_THETA ** (jnp.arange(half, dtype=F32) / half))
    invf = jnp.concatenate([invf, invf, jnp.zeros((128 - ROPE,), F32)]).reshape(1, 128)
    pos = positions.reshape(S, 1)
    pad128 = lambda a: jnp.concatenate([a.reshape(1, SH), jnp.zeros((1, 128 - SH), F32)], axis=1)
    dtb, alog = pad128(dt_bias), pad128(a_log)
    dskx = jnp.repeat(d_skip.reshape(SH), SP).reshape(1, SW)

    my_c = lax.axis_index("c")
    (call,) = _exchange("gather_c", [jnp.broadcast_to(c.reshape(1, 1, D), (4, 1, D))])
    call = call.reshape(8, D)
    mods = _ada(call, w_ada[0])
    (mrows,) = _exchange("scatter_mod", [mods.reshape(4, 2, 3 * D // 4)])
    mine = lax.dynamic_index_in_dim(mrows.reshape(4, 2, 2, 3 * D // 4)[:, 0], my_c, axis=1, keepdims=False)
    mod = jnp.broadcast_to(mine.reshape(1, 3 * D) + b_ada, (8, 3 * D))
    qlat, ckv, za, xbc, zs, dtraw, q, k, v, kt, vt = _inproj(xv, mod, win, q_norm_g, kv_norm_g, wq, wk, wv, wk.T, wv.T, pos, invf)
    o, lse = _attn_fwd(q, k, vt)
    y, htp, ossm = _ssd_fwd(xbc, dtraw, zs, cwf, conv_b, dtb, alog, dskx, ssm_norm_g)
    gx1, do, dza, dossm, delta, dwout, vec_o = _outproj(o, za, ossm, xv, tgt, wout, mod, ln_g, ln_b)
    loss_part = jnp.zeros((128,), F32).at[0].set(0.5 / D * jnp.sum(vec_o[0]))

    dk, dv, dq = _attn_bwd(q, k, kt, v, do, lse, delta)
    dxbc, ddt, dzs, dcw, dcb, dvec, dgssm = _ssd_bwd(xbc, dtraw, zs, y, htp, dossm, cwf, conv_b, dtb, alog, dskx, ssm_norm_g)
    dql, dckv, dkr, dwq, dwk, dwv, dqg, dkvg = _mla_bwd(dq, dk, dv, qlat, ckv, q_norm_g, kv_norm_g, wq, wk, wv, pos, invf)
    gx, dwin, vec_i = _inproj_bwd(xv, gx1, mod, win, dql, dckv, dza, dxbc, dzs, dkr, ddt)
    dmod = jnp.concatenate([vec_i[0:1], vec_i[1:2], vec_o[3:4]], axis=1)

    cols = lambda g: g.reshape(g.shape[0], 4, g.shape[1] // 4).transpose(1, 0, 2)
    g_in = cols(_in_from_padded(dwin)).astype(BF16)
    g_qb = cols(dwq.reshape(QL, NH, HP)[:, :, 0:QKD].reshape(QL, NH * QKD)).astype(BF16)
    g_kvb = cols(jnp.concatenate([dwk.reshape(KVL, NH, NOPE), dwv.reshape(KVL, NH, VD)], axis=2)
                 .reshape(KVL, NH * (NOPE + VD))).astype(BF16)
    g_out = dwout.reshape(4, MIX // 4, D).astype(BF16)
    small = {"b_ada": dmod, "conv_b": dcb, "ssm_norm_g": dgssm, "ln_g": vec_o[1:2], "ln_b": vec_o[2:3],
             "q_norm_g": dqg, "kv_norm_g": dkvg, "dt_bias": dvec[0:1, 0:SH], "a_log": dvec[1:2, 0:SH], "d_skip": dvec[2:3, 0:SH],
             "loss": loss_part}
    small = {n: jnp.broadcast_to(a.reshape(1, -1), (4, a.size)) for n, a in small.items()}
    small["conv_w"] = cols(dcw).reshape(4, CW * CC // 4)
    gsmall = _pack_small(small, 1)

    r_in, r_qb, r_kvb, r_out, rs, dmods = _exchange(
        "exchange_grads", [g_in, g_qb, g_kvb, g_out, gsmall, jnp.broadcast_to(dmod.reshape(1, 1, 3 * D), (4, 1, 3 * D))])
    chip = 2 * lax.axis_index("x") + lax.axis_index("y")
    dmods = lax.dynamic_slice_in_dim(dmods.reshape(8, 3 * D), chip * (3 * D // 4), 3 * D // 4, axis=1)
    g_ada = _ada_bwd(call.T, dmods)
    res = dict(w_ada=_adamw("w_ada", g_ada, w_ada[0], m_w_ada[0], v_w_ada[0]),
               w_in=_adamw("w_in", r_in, w_in[0], m_w_in[0], v_w_in[0]),
               w_qb=_adamw("w_qb", r_qb, w_qb[0], m_w_qb[0], v_w_qb[0]),
               w_kvb=_adamw("w_kvb", r_kvb, w_kvb[0], m_w_kvb[0], v_w_kvb[0]),
               w_out=_adamw("w_out", r_out, w_out[0], m_w_out[0], v_w_out[0]))
    wsm = _pack_small(dict(b_ada=b_ada, conv_w=conv_w, conv_b=conv_b, ssm_norm_g=ssm_norm_g, ln_g=ln_g, ln_b=ln_b,
                           q_norm_g=q_norm_g, kv_norm_g=kv_norm_g, dt_bias=dt_bias, a_log=a_log, d_skip=d_skip, loss=jnp.zeros((128,), F32)), 0)
    msm = _pack_small(dict(b_ada=m_b_ada, conv_w=m_conv_w, conv_b=m_conv_b, ssm_norm_g=m_ssm_norm_g, ln_g=m_ln_g, ln_b=m_ln_b,
                           q_norm_g=m_q_norm_g, kv_norm_g=m_kv_norm_g, dt_bias=m_dt_bias, a_log=m_a_log, d_skip=m_d_skip, loss=jnp.zeros((128,), F32)), 0)
    vsm = _pack_small(dict(b_ada=v_b_ada, conv_w=v_conv_w, conv_b=v_conv_b, ssm_norm_g=v_ssm_norm_g, ln_g=v_ln_g, ln_b=v_ln_b,
                           q_norm_g=v_q_norm_g, kv_norm_g=v_kv_norm_g, dt_bias=v_dt_bias, a_log=v_a_log, d_skip=v_d_skip, loss=jnp.zeros((128,), F32)), 0)
    sm = _adamw("small", rs, wsm, msm, vsm)

    order = ["w_ada", "b_ada", "w_in", "q_norm_g", "w_qb", "kv_norm_g", "w_kvb", "conv_w", "conv_b", "dt_bias", "a_log",
             "d_skip", "ssm_norm_g", "w_out", "ln_g", "ln_b"]
    shapes = dict(w_ada=w_ada.shape, b_ada=b_ada.shape, w_in=w_in.shape, q_norm_g=q_norm_g.shape, w_qb=w_qb.shape,
                  kv_norm_g=kv_norm_g.shape, w_kvb=w_kvb.shape, conv_w=conv_w.shape, conv_b=conv_b.shape, dt_bias=dt_bias.shape,
                  a_log=a_log.shape, d_skip=d_skip.shape, ssm_norm_g=ssm_norm_g.shape, w_out=w_out.shape, ln_g=ln_g.shape,
                  ln_b=ln_b.shape)
    outs = []
    for kind in range(4):
        d = _unpack_small(sm[kind])
        d.update({n: r[kind] for n, r in res.items()})
        outs.extend(d[n].reshape(shapes[n]) for n in order)
    loss = _unpack_small(sm[0])["loss"][0]
    return (loss, gx.reshape(x.shape), *outs)
```

```python
import functools
import math

import numpy as np
import jax
import jax.numpy as jnp
from jax import lax
from jax.experimental import pallas as pl
from jax.experimental.pallas import tpu as pltpu

F32 = jnp.float32
BF16 = jnp.bfloat16
HIGHEST = lax.Precision.HIGHEST
MESH_ID = pl.DeviceIdType.MESH

D = 1024
NH = 8
NOPE = 128
ROPE = 64
VD = 128
VDP = 144
QKD = NOPE + ROPE
HP = 256
QL = 384
KVL = 256
ROPE_THETA = 10000.0
SH = 16
SP = 64
SG = 2
SN = 128
CW = 4
CH = 128
SW = SH * SP
CC = SW + 2 * SG * SN
GW = SW // SG
MIX = 2 * D
IN_W = 4304
ALPHA = 2.0 ** 0.25
RMS_EPS = 1e-6
LN_EPS = 1e-5
SCALE = QKD ** -0.5
LN2 = math.log(2.0)
QSCALE = SCALE / LN2
LR, B1, B2, EPS, WD, STEP = 0.001, 0.9, 0.999, 1e-08, 0.01, 10

P_Q = (0, 384)
P_KV = (384, 640)
P_ZA = (640, 1664)
P_XBC = (1664, 3200)
P_ZS = (3200, 4224)
P_KR = (4224, 4352)
P_DT = (4352, 4480)
IN_P = 4480

R_SMALL = 16

TM = 256
TQ = 512
TQF = 1024
VMEM_LIMIT = 56 * 1024 * 1024


def _cp(sem=None):
    return pltpu.CompilerParams(dimension_semantics=sem, vmem_limit_bytes=VMEM_LIMIT)


def _mm(a, b):
    return jnp.dot(a, b, preferred_element_type=F32)


def _nt(a, b):
    return lax.dot_general(a, b, (((1,), (1,)), ((), ())), preferred_element_type=F32)


def _tn(a, b):
    return lax.dot_general(a, b, (((0,), (0,)), ((), ())), preferred_element_type=F32)


def _mm_hi(a, b):
    return jnp.dot(a, b, precision=HIGHEST, preferred_element_type=F32)


def _nt_hi(a, b):
    return lax.dot_general(a, b, (((1,), (1,)), ((), ())), precision=HIGHEST, preferred_element_type=F32)


def _sigmoid(z):
    return 1.0 / (1.0 + jnp.exp(-z))


def _softplus(z):
    return jnp.maximum(z, 0.0) + jnp.log1p(jnp.exp(-jnp.abs(z)))


def _rope(t, cs, sn):
    lane = lax.broadcasted_iota(jnp.int32, t.shape, 1)
    rot = jnp.where(lane < ROPE // 2, -pltpu.roll(t, 128 - ROPE // 2, 1), pltpu.roll(t, ROPE // 2, 1))
    return t * cs + rot * sn


def _rope_t(t, cs, sn):
    lane = lax.broadcasted_iota(jnp.int32, t.shape, 1)
    y = t * sn
    rot = jnp.where(lane < ROPE // 2, -pltpu.roll(y, 128 - ROPE // 2, 1), pltpu.roll(y, ROPE // 2, 1))
    return t * cs - rot


def _full(shape):
    n = len(shape)
    return pl.BlockSpec(shape, lambda *_: (0,) * n)


def _const(shape):
    n = len(shape)
    return pl.BlockSpec(shape, lambda *_: (0,) * n, pipeline_mode=pl.Buffered(1))


def _gather_weights(shards):
    n = len(shards)
    halves = [a.shape[0] // 2 for a in shards]

    def body(*refs):
        srcs, dsts = refs[:n], refs[n:2 * n]
        send_sems, recv_sems, local_sems = refs[2 * n:]
        x, y, c = lax.axis_index("x"), lax.axis_index("y"), lax.axis_index("c")
        me = 2 * x + y
        sibling = (x, y, 1 - c)
        chips = [(1 - x, y), (x, 1 - y), (1 - x, 1 - y)]

        def rows(a, pc):
            return pl.ds(pl.multiple_of(pc * halves[a], halves[a]), halves[a])

        def copy(a, k, src, slot, pc, to):
            return pltpu.make_async_remote_copy(
                src_ref=src, dst_ref=dsts[a].at[slot, rows(a, pc)], send_sem=send_sems.at[a, k],
                recv_sem=recv_sems.at[a, k], device_id=to, device_id_type=MESH_ID)

        local = [pltpu.make_async_copy(srcs[a], dsts[a].at[me], local_sems.at[a]) for a in range(n)]
        for cp in local:
            cp.start()
        sends = [copy(a, j, srcs[a].at[rows(a, c)], me, c, (px, py, c)) for a in range(n) for j, (px, py) in enumerate(chips)]
        for cp in sends:
            cp.start()
        passed = []
        for a in range(n):
            for j, (px, py) in enumerate(chips):
                k = 2 * px + py
                copy(a, j, srcs[a].at[rows(a, c)], k, c, (x, y, c)).wait_recv()
                fwd = copy(a, 3 + j, dsts[a].at[k, rows(a, c)], k, c, sibling)
                fwd.start()
                passed.append(fwd)
        for a in range(n):
            for j, (px, py) in enumerate(chips):
                copy(a, 3 + j, srcs[a].at[rows(a, c)], 2 * px + py, 1 - c, (x, y, c)).wait_recv()
        for cp in sends + passed:
            cp.wait_send()
        for cp in local:
            cp.wait()

    hbm = pl.BlockSpec(memory_space=pltpu.HBM)
    return pl.pallas_call(
        body, name="gather_weights",
        out_shape=tuple(jax.ShapeDtypeStruct((4,) + a.shape, a.dtype) for a in shards),
        in_specs=[hbm] * n, out_specs=tuple([hbm] * n),
        scratch_shapes=[pltpu.SemaphoreType.DMA((n, 6)), pltpu.SemaphoreType.DMA((n, 6)), pltpu.SemaphoreType.DMA((n,))],
    )(*shards)


def _exchange(name, slabs):
    n = len(slabs)

    def body(*refs):
        srcs, dsts = refs[:n], refs[n:2 * n]
        send_sems, recv_sems, local_sems = refs[2 * n:]
        x, y, c = lax.axis_index("x"), lax.axis_index("y"), lax.axis_index("c")
        chip = 2 * x + y
        sibling = (x, y, 1 - c)
        chips = [(1 - x, y), (x, 1 - y), (1 - x, 1 - y)]

        def slot(px, py, pc):
            return 4 * px + 2 * py + pc

        def copy(a, k, src, s, to):
            return pltpu.make_async_remote_copy(
                src_ref=src, dst_ref=dsts[a].at[s], send_sem=send_sems.at[a, k], recv_sem=recv_sems.at[a, k],
                device_id=to, device_id_type=MESH_ID)

        mine = slot(x, y, c)
        local = [pltpu.make_async_copy(srcs[a].at[chip], dsts[a].at[mine], local_sems.at[a]) for a in range(n)]
        for cp in local:
            cp.start()
        first = []
        for a in range(n):
            first.append(copy(a, 0, srcs[a].at[chip], mine, sibling))
            for j, (px, py) in enumerate(chips):
                first.append(copy(a, 1 + j, srcs[a].at[2 * px + py], mine, (px, py, c)))
        for cp in first:
            cp.start()
        passed = []
        for a in range(n):
            for j, (px, py) in enumerate(chips):
                s = slot(px, py, c)
                copy(a, 1 + j, srcs[a].at[chip], s, (x, y, c)).wait_recv()
                fwd = copy(a, 4 + j, dsts[a].at[s], s, sibling)
                fwd.start()
                passed.append(fwd)
        for a in range(n):
            copy(a, 0, srcs[a].at[chip], slot(x, y, 1 - c), (x, y, c)).wait_recv()
            for j, (px, py) in enumerate(chips):
                copy(a, 4 + j, srcs[a].at[chip], slot(px, py, 1 - c), (x, y, c)).wait_recv()
        for cp in first + passed:
            cp.wait_send()
        for cp in local:
            cp.wait()

    hbm = pl.BlockSpec(memory_space=pltpu.HBM)
    return pl.pallas_call(
        body, name=name,
        out_shape=tuple(jax.ShapeDtypeStruct((8,) + a.shape[1:], a.dtype) for a in slabs),
        in_specs=[hbm] * n, out_specs=tuple([hbm] * n),
        scratch_shapes=[pltpu.SemaphoreType.DMA((n, 7)), pltpu.SemaphoreType.DMA((n, 7)), pltpu.SemaphoreType.DMA((n,))],
    )(*slabs)


def _ada(call, w_shard):
    def body(c_ref, w_ref, o_ref):
        o_ref[...] = _mm(c_ref[...].astype(BF16), w_ref[...].astype(BF16))

    return pl.pallas_call(body, name="ada", out_shape=jax.ShapeDtypeStruct((8, w_shard.shape[1]), F32),
                          compiler_params=_cp())(call, w_shard)


def _inproj(x, mod, win, qg, kvg, wq, wk, wv, wkt, wvt, pos, invf):
    S = x.shape[0]
    tm = min(TM, S)

    def body(x_ref, mod_ref, win_ref, qg_ref, kvg_ref, wq_ref, wk_ref, wv_ref, wkt_ref, wvt_ref, pos_ref, invf_ref,
             qlat_ref, ckv_ref, za_ref, xbc_ref, zs_ref, dt_ref, q_ref, k_ref, v_ref, kt_ref, vt_ref):
        shift = mod_ref[0:1, 0:D]
        scale = mod_ref[0:1, D:2 * D]
        u = (x_ref[...] * (1.0 + scale) + shift).astype(BF16)

        def proj(p):
            return _mm(u, win_ref[:, p[0]:p[1]])

        ql = proj(P_Q)
        ckv = proj(P_KV)
        qlat_ref[...] = ql
        ckv_ref[...] = ckv
        za_ref[...] = proj(P_ZA)
        xbc_ref[...] = proj(P_XBC)
        zs_ref[...] = proj(P_ZS)
        dt_ref[...] = proj(P_DT)
        kr = proj(P_KR)

        ang = pos_ref[...].astype(F32) * invf_ref[...]
        cs = jnp.cos(ang)
        sn = jnp.sin(ang)

        rq = lax.rsqrt(jnp.mean(ql * ql, axis=-1, keepdims=True) + RMS_EPS)
        qn = (ql * rq * qg_ref[...]).astype(BF16)
        for h in range(NH):
            qh = _mm(qn, wq_ref[:, h * HP:(h + 1) * HP])
            q_ref[h, :, 0:NOPE] = (qh[:, 0:NOPE] * QSCALE).astype(BF16)
            q_ref[h, :, NOPE:HP] = (_rope(qh[:, NOPE:HP], cs, sn) * QSCALE).astype(BF16)

        rk = lax.rsqrt(jnp.mean(ckv * ckv, axis=-1, keepdims=True) + RMS_EPS)
        kn = (ckv * rk * kvg_ref[...]).astype(BF16)
        knope = _mm(kn, wk_ref[...])
        vall = _mm(kn, wv_ref[...])
        krf = _rope(kr, cs, sn)
        krr = krf.astype(BF16)
        krt = krf.T.astype(BF16)
        ones_rows = jnp.where(lax.broadcasted_iota(jnp.int32, (VDP - VD, tm), 0) == 0, 1.0, 0.0).astype(BF16)
        for h in range(NH):
            k_ref[h, :, 0:NOPE] = knope[:, h * NOPE:(h + 1) * NOPE].astype(BF16)
            k_ref[h, :, NOPE:HP] = krr
            v_ref[h] = vall[:, h * VD:(h + 1) * VD].astype(BF16)
            kt_ref[h, 0:NOPE, :] = _nt(wkt_ref[h * NOPE:(h + 1) * NOPE, :], kn).astype(BF16)
            kt_ref[h, NOPE:HP, :] = krt
            vt_ref[h, 0:VD, :] = _nt(wvt_ref[h * VD:(h + 1) * VD, :], kn).astype(BF16)
            vt_ref[h, VD:VDP, :] = ones_rows

    row = lambda n: pl.BlockSpec((tm, n), lambda i: (i, 0))
    heads = lambda n: pl.BlockSpec((NH, tm, n), lambda i: (0, i, 0))
    heads_t = lambda n: pl.BlockSpec((NH, None, n, tm), lambda i: (0, i, 0, 0))
    sd = lambda n: jax.ShapeDtypeStruct((S, n), F32)
    hd = lambda n: jax.ShapeDtypeStruct((NH, S, n), BF16)
    ht = lambda n: jax.ShapeDtypeStruct((NH, S // tm, n, tm), BF16)
    return pl.pallas_call(
        body, name="inproj", grid=(S // tm,),
        in_specs=[row(D), _const((8, 3 * D)), _const((D, IN_P)), _const((1, QL)), _const((1, KVL)),
                  _const((QL, NH * HP)), _const((KVL, NH * NOPE)), _const((KVL, NH * VD)),
                  _const((NH * NOPE, KVL)), _const((NH * VD, KVL)), row(1), _const((1, 128))],
        out_specs=[row(QL), row(KVL), row(D), row(CC), row(D), row(128), heads(HP), heads(HP), heads(VD),
                   heads_t(HP), heads_t(VDP)],
        out_shape=[sd(QL), sd(KVL), sd(D), sd(CC), sd(D), sd(128), hd(HP), hd(HP), hd(VD), ht(HP), ht(VDP)],
        compiler_params=_cp(("arbitrary",)),
    )(x, mod, win, qg, kvg, wq, wk, wv, wkt, wvt, pos, invf)


def _attn_fwd(q, k, vt):
    _, S, _ = q.shape
    tq = min(TQF, S)
    nq = S // tq
    half = tq // 2
    tb = vt.shape[3]
    nsb = half // tb
    nblk = nq * (nq + 1)

    def body(q_ref, k_ref, vt_ref, o_ref, lse_ref, mblk_ref, p_hbm, pbuf, sems):
        h = pl.program_id(0)
        i = pl.program_id(1)
        qb = q_ref[...]
        base = i * (i + 1)

        def scores(j, hb):
            off = pl.multiple_of(j * tq + hb * half, half)
            return _nt(k_ref[pl.ds(off, half), :], qb)

        def p_copy(j, par, hb):
            return pltpu.make_async_copy(pbuf.at[2 * par + hb], p_hbm.at[h, base + 2 * j + hb], sems.at[2 * par + hb])

        def update(j, par, hb, s, carry):
            m, acc = carry
            m_new = jnp.maximum(m, jnp.max(s, axis=0, keepdims=True))
            a = jnp.exp2(m - m_new)
            pbuf[2 * par + hb] = jnp.exp2(s - m_new).astype(BF16)
            mblk_ref[2 * j + hb] = m_new
            acc = a * acc
            for sb in range(nsb):
                acc = acc + _mm(vt_ref[(2 * j + hb) * nsb + sb], pbuf[2 * par + hb, sb * tb:(sb + 1) * tb, :])
            return m_new, acc

        def trip(j, par, carry, masked):
            @pl.when(j >= 2)
            def _():
                for hb in range(2):
                    p_copy(j - 2, par, hb).wait()

            s = [scores(j, hb) for hb in range(2)]
            if masked:
                r = lax.broadcasted_iota(jnp.int32, s[0].shape, 0)
                cidx = lax.broadcasted_iota(jnp.int32, s[0].shape, 1)
                s = [jnp.where(cidx >= r + hb * half, s[hb], -1e30) for hb in range(2)]
            for hb in range(2):
                carry = update(j, par, hb, s[hb], carry)
            for hb in range(2):
                p_copy(j, par, hb).start()
            return carry

        def finish(carry):
            m, acc = carry
            l = acc[VD:VD + 1, :]
            o_ref[...] = (acc[0:VD, :] / l).T
            lse_ref[...] = m + jnp.log2(l)

        init = (jnp.full((1, tq), -1e30, F32), jnp.zeros((VDP, tq), F32))
        carry = lax.fori_loop(0, i >> 1, lambda t, cr: trip(2 * t + 1, 1, trip(2 * t, 0, cr, False), False), init)

        @pl.when((i & 1) == 0)
        def _():
            finish(trip(i, 0, carry, True))
            for hb in range(2):
                p_copy(i, 0, hb).wait()

            @pl.when(i >= 1)
            def _():
                for hb in range(2):
                    p_copy(i - 1, 1, hb).wait()

        @pl.when((i & 1) == 1)
        def _():
            finish(trip(i, 1, trip(i - 1, 0, carry, False), True))
            for hb in range(2):
                p_copy(i - 1, 0, hb).wait()
                p_copy(i, 1, hb).wait()

    return pl.pallas_call(
        body, name="attn_fwd", grid=(NH, nq),
        in_specs=[pl.BlockSpec((None, tq, HP), lambda h, i: (h, i, 0)),
                  pl.BlockSpec((None, S, HP), lambda h, i: (h, 0, 0)),
                  pl.BlockSpec((None, S // tb, VDP, tb), lambda h, i: (h, 0, 0, 0))],
        out_specs=[pl.BlockSpec((tq, VD), lambda h, i: (i, h)),
                   pl.BlockSpec((None, None, 1, tq), lambda h, i: (h, i, 0, 0)),
                   pl.BlockSpec((None, None, 2 * nq, 1, tq), lambda h, i: (h, i, 0, 0, 0)),
                   pl.BlockSpec(memory_space=pltpu.HBM)],
        out_shape=[jax.ShapeDtypeStruct((S, NH * VD), F32), jax.ShapeDtypeStruct((NH, nq, 1, tq), F32),
                   jax.ShapeDtypeStruct((NH, nq, 2 * nq, 1, tq), F32),
                   jax.ShapeDtypeStruct((NH, nblk, half, tq), BF16)],
        scratch_shapes=[pltpu.VMEM((4, half, tq), BF16), pltpu.SemaphoreType.DMA((4,))],
        compiler_params=_cp(("arbitrary", "arbitrary")),
    )(q, k, vt)


def _ssd_consts():
    tri = np.tril(np.ones((CH, CH), np.float32))
    e16 = np.zeros((128, SW), np.float32)
    for h in range(SH):
        e16[h, h * SP:(h + 1) * SP] = 1.0
    return jnp.asarray(tri), jnp.asarray(tri.T.copy()), jnp.asarray(e16)


def _ssd_chunk_fwd_common(xbc_ref, halo_ref, dtraw_ref, cw_ref, cb_ref, dtb_ref, alog_ref, tri_ref, e16_ref, ext, first):
    ext[0:8, :] = jnp.where(first, 0.0, halo_ref[...])
    ext[8:8 + CH, :] = xbc_ref[...]
    cw = cw_ref[...]
    xc = cb_ref[...] + cw[0:1, :] * ext[5:5 + CH, :]
    for kk in range(1, CW):
        xc = xc + cw[kk:kk + 1, :] * ext[5 + kk:5 + kk + CH, :]
    sact = _sigmoid(xc)
    act = xc * sact
    lane = lax.broadcasted_iota(jnp.int32, (1, 128), 1)
    arow = jnp.where(lane < SH, -jnp.exp(alog_ref[...]), 0.0)
    dtpre = dtraw_ref[...] + dtb_ref[...]
    dt = _softplus(dtpre)
    a = dt * arow
    cum = _mm_hi(tri_ref[...], a)
    cumx = _mm_hi(cum, e16_ref[...])
    dtx = _mm_hi(dt, e16_ref[...])
    return xc, sact, act, arow, dtpre, dt, cum, cumx, dtx


def _ssd_fwd(xbc, dtraw, zs, conv_w, conv_b, dtb, alog, dskx, gssm):
    S = xbc.shape[0]
    nc = S // CH
    tri, _, e16 = _ssd_consts()

    def body(xbc_ref, halo_ref, dtraw_ref, zs_ref, cw_ref, cb_ref, dtb_ref, alog_ref, dsk_ref, g_ref, tri_ref, e16_ref,
             y_ref, htp_ref, o_ref, ht, ext):
        i = pl.program_id(0)

        @pl.when(i == 0)
        def _():
            ht[...] = jnp.zeros_like(ht)

        xc, sact, act, arow, dtpre, dt, cum, cumx, dtx = _ssd_chunk_fwd_common(
            xbc_ref, halo_ref, dtraw_ref, cw_ref, cb_ref, dtb_ref, alog_ref, tri_ref, e16_ref, ext, i == 0)
        cum_t = cum.T
        xs = act[:, 0:SW]
        lastx = cumx[CH - 1:CH, :]
        xh = xs * dtx
        eexp = jnp.exp(cumx)
        dte = jnp.exp(lastx - cumx)
        cdx = jnp.exp(lastx)
        htp = ht[...]
        htp_ref[...] = htp
        xw = (xh * dte).astype(BF16)
        xb = xh.astype(BF16)
        trim = tri_ref[...] > 0.5
        lane = lax.broadcasted_iota(jnp.int32, (CH, 128), 1)
        parts = []
        for g in range(SG):
            gl = slice(g * GW, (g + 1) * GW)
            bg = act[:, SW + g * SN:SW + (g + 1) * SN].astype(BF16)
            cg = act[:, SW + SG * SN + g * SN:SW + SG * SN + (g + 1) * SN].astype(BF16)
            cbm = _nt(cg, bg)
            yoff = eexp[:, gl] * _mm(cg, htp[:, gl].astype(BF16))
            ht[:, gl] = htp[:, gl] * cdx[:, gl] + _tn(bg, xw[:, gl])
            for pr in range(GW // 128):
                h0 = g * (SH // SG) + 2 * pr
                lo = g * GW + pr * 128
                xp = xb[:, lo:lo + 128]
                res = []
                for hh in (h0, h0 + 1):
                    seg = cum[:, hh:hh + 1] - cum_t[hh:hh + 1, :]
                    mh = jnp.where(trim, cbm * jnp.exp(seg), 0.0).astype(BF16)
                    res.append(_mm(mh, xp))
                parts.append(jnp.where(lane < SP, res[0], res[1]) + yoff[:, pr * 128:(pr + 1) * 128])
        y = jnp.concatenate(parts, axis=1) + xs * dsk_ref[...]
        y_ref[...] = y
        z = zs_ref[...]
        hf = y * (z * _sigmoid(z))
        outs = []
        for g in range(SG):
            hg = hf[:, g * GW:(g + 1) * GW]
            rs = lax.rsqrt(jnp.mean(hg * hg, axis=-1, keepdims=True) + RMS_EPS)
            outs.append(hg * rs)
        o_ref[...] = (jnp.concatenate(outs, axis=1) * g_ref[...]).astype(BF16)

    row = lambda n: pl.BlockSpec((CH, n), lambda i: (i, 0))
    return pl.pallas_call(
        body, name="ssd_fwd", grid=(nc,),
        in_specs=[row(CC), pl.BlockSpec((8, CC), lambda i: (jnp.maximum(i * (CH // 8) - 1, 0), 0)), row(128), row(SW),
                  _const((CW, CC)), _const((1, CC)), _const((1, 128)), _const((1, 128)), _const((1, SW)), _const((1, SW)),
                  _const((CH, CH)), _const((128, SW))],
        out_specs=[row(SW), pl.BlockSpec((None, SN, SW), lambda i: (i, 0, 0)), row(SW)],
        out_shape=[jax.ShapeDtypeStruct((S, SW), F32), jax.ShapeDtypeStruct((nc, SN, SW), F32),
                   jax.ShapeDtypeStruct((S, SW), BF16)],
        scratch_shapes=[pltpu.VMEM((SN, SW), F32), pltpu.VMEM((8 + CH, CC), F32)],
        compiler_params=_cp(("arbitrary",)),
    )(xbc, xbc, dtraw, zs, conv_w, conv_b, dtb, alog, dskx, gssm, tri, e16)


def _outproj(o, za, ossm, x, tgt, wout, mod, ln_g, ln_b):
    S = x.shape[0]
    tm = min(TM, S)

    def body(o_ref, za_ref, os_ref, x_ref, t_ref, w_ref, mod_ref, g_ref, b_ref,
             gx_ref, do_ref, dza_ref, dos_ref, delta_ref, dw_ref, vec_ref):
        i = pl.program_id(0)

        @pl.when(i == 0)
        def _():
            dw_ref[...] = jnp.zeros_like(dw_ref)
            vec_ref[...] = jnp.zeros_like(vec_ref)

        gate = mod_ref[0:1, 2 * D:3 * D]
        ov = o_ref[...]
        z = za_ref[...]
        sz = _sigmoid(z)
        silz = z * sz
        a = (ov * silz).astype(BF16)
        osb = os_ref[...]
        mixed = _mm(a, w_ref[0:D, :]) + _mm(osb, w_ref[D:MIX, :])
        xv = x_ref[...]
        hres = ALPHA * xv + gate * mixed
        mu = jnp.mean(hres, axis=-1, keepdims=True)
        hc = hres - mu
        var = jnp.mean(hc * hc, axis=-1, keepdims=True)
        rstd = lax.rsqrt(var + LN_EPS)
        xhat = hc * rstd
        g = g_ref[...]
        yv = xhat * g + b_ref[...]
        err = yv - t_ref[...]
        dy = err * (1.0 / D)
        vec_ref[0:1, :] += jnp.sum(err * err, axis=0, keepdims=True)
        vec_ref[1:2, :] += jnp.sum(dy * xhat, axis=0, keepdims=True)
        vec_ref[2:3, :] += jnp.sum(dy, axis=0, keepdims=True)
        dxh = dy * g
        dh = rstd * (dxh - jnp.mean(dxh, axis=-1, keepdims=True) - xhat * jnp.mean(dxh * xhat, axis=-1, keepdims=True))
        gx_ref[...] = ALPHA * dh
        vec_ref[3:4, :] += jnp.sum(dh * mixed, axis=0, keepdims=True)
        dmixed = (gate * dh).astype(BF16)
        dw_ref[0:D, :] += _tn(a, dmixed)
        dw_ref[D:MIX, :] += _tn(osb, dmixed)
        da = _nt(dmixed, w_ref[0:D, :])
        dos_ref[...] = _nt(dmixed, w_ref[D:MIX, :])
        dov = da * silz
        do_ref[...] = dov.astype(BF16)
        dza_ref[...] = (da * ov * (sz * (1.0 + z * (1.0 - sz)))).astype(BF16)
        pr = dov * ov
        for h in range(NH):
            delta_ref[h] = jnp.sum(pr[:, h * VD:(h + 1) * VD], axis=-1, keepdims=True)

    row = lambda n: pl.BlockSpec((tm, n), lambda i: (i, 0))
    return pl.pallas_call(
        body, name="outproj", grid=(S // tm,),
        in_specs=[row(D), row(D), row(D), row(D), row(D), _const((MIX, D)), _const((8, 3 * D)), _const((1, D)), _const((1, D))],
        out_specs=[row(D), row(D), row(D), row(D), pl.BlockSpec((NH, tm, 1), lambda i: (0, i, 0)),
                   _full((MIX, D)), _full((8, D))],
        out_shape=[jax.ShapeDtypeStruct((S, D), F32), jax.ShapeDtypeStruct((S, D), BF16), jax.ShapeDtypeStruct((S, D), BF16),
                   jax.ShapeDtypeStruct((S, D), F32), jax.ShapeDtypeStruct((NH, S, 1), F32),
                   jax.ShapeDtypeStruct((MIX, D), F32), jax.ShapeDtypeStruct((8, D), F32)],
        compiler_params=_cp(("arbitrary",)),
    )(o, za, ossm, x, tgt, wout, mod, ln_g, ln_b)


def _attn_bwd(q, k, kt, v, do, lse, delta, mblk, pst):
    _, S, _ = q.shape
    tk = min(TQ, S // 2)
    nk = S // tk
    tq = 2 * tk
    nq = S // tq
    tb = kt.shape[3]
    nsb = tk // tb

    assert pst.shape[2:] == (tk, tq)

    def body(k_ref, kt_ref, v_ref, q_ref, do_ref, lse_ref, dl_ref, mblk_ref, p_hbm, dk_ref, dv_ref, dqt_ref, pin, sems):
        h = pl.program_id(0)
        j = pl.program_id(1)
        kb = k_ref[...]
        vb = v_ref[...]
        first = j >> 1

        @pl.when(j == 0)
        def _():
            dqt_ref[...] = jnp.zeros_like(dqt_ref)

        dk_ref[...] = jnp.zeros_like(dk_ref)
        dv_ref[...] = jnp.zeros_like(dv_ref)

        def fetch(i, slot):
            return pltpu.make_async_copy(p_hbm.at[h, i * (i + 1) + j], pin.at[slot], sems.at[slot])

        def step(i, slot, lo=0):
            @pl.when(i + 1 < nq)
            def _():
                fetch(i + 1, 1 - slot).start()

            fetch(i, slot).wait()
            off = pl.multiple_of(i * tq + lo, tk)
            qb = q_ref[pl.ds(off, tq - lo), :]
            dob = do_ref[pl.ds(off, tq - lo), :]
            sc = jnp.exp2(mblk_ref[i] - lse_ref[i])[:, lo:tq]
            pt = pin[slot, :, lo:tq].astype(F32) * sc
            dv_ref[...] += _mm(pt.astype(BF16), dob)
            dsb = (pt * (_nt(vb, dob) - dl_ref[i][:, lo:tq])).astype(BF16)
            dk_ref[...] += _mm(dsb, qb)
            acc = dqt_ref[i, :, lo:tq]
            for sb in range(nsb):
                acc = acc + _mm(kt_ref[sb], dsb[sb * tb:(sb + 1) * tb, :])
            dqt_ref[i, :, lo:tq] = acc

        fetch(first, 0).start()

        @pl.when((j & 1) == 0)
        def _():
            step(first, 0)

        @pl.when((j & 1) == 1)
        def _():
            step(first, 0, tk)

        def loop_body(i, carry):
            step(i, (i - first) & 1)
            return carry

        lax.fori_loop(first + 1, nq, loop_body, 0)
        dk_ref[...] = dk_ref[...] * LN2

    return pl.pallas_call(
        body, name="attn_bwd", grid=(NH, nk),
        in_specs=[pl.BlockSpec((None, tk, HP), lambda h, j: (h, j, 0)),
                  pl.BlockSpec((None, nsb, HP, tb), lambda h, j: (h, j, 0, 0)),
                  pl.BlockSpec((None, tk, VD), lambda h, j: (h, j, 0)),
                  pl.BlockSpec((None, S, HP), lambda h, j: (h, 0, 0)),
                  pl.BlockSpec((S, VD), lambda h, j: (0, h)),
                  pl.BlockSpec((None, nq, 1, tq), lambda h, j: (h, 0, 0, 0)),
                  pl.BlockSpec((None, nq, 1, tq), lambda h, j: (h, 0, 0, 0)),
                  pl.BlockSpec((None, nq, None, 1, tq), lambda h, j: (h, 0, j, 0, 0)),
                  pl.BlockSpec(memory_space=pltpu.HBM)],
        out_specs=[pl.BlockSpec((None, tk, HP), lambda h, j: (h, j, 0)),
                   pl.BlockSpec((None, tk, VD), lambda h, j: (h, j, 0)),
                   pl.BlockSpec((None, nq, HP, tq), lambda h, j: (h, 0, 0, 0), pipeline_mode=pl.Buffered(1))],
        out_shape=[jax.ShapeDtypeStruct((NH, S, HP), F32), jax.ShapeDtypeStruct((NH, S, VD), F32),
                   jax.ShapeDtypeStruct((NH, nq, HP, tq), F32)],
        scratch_shapes=[pltpu.VMEM((2, tk, tq), BF16), pltpu.SemaphoreType.DMA((2,))],
        compiler_params=_cp(("arbitrary", "arbitrary")),
    )(k, kt, v, q, do, lse.reshape(NH, nq, 1, tq), delta.reshape(NH, nq, 1, tq), mblk, pst)


def _ssd_bwd(xbc, dtraw, zs, y, htp, dossm, conv_w, conv_b, dtb, alog, dskx, gssm):
    S = xbc.shape[0]
    nc = S // CH
    tri, triu, e16 = _ssd_consts()

    def body(xbc_ref, halo_ref, dtraw_ref, zs_ref, y_ref, htp_ref, dos_ref,
             cw_ref, cb_ref, dtb_ref, alog_ref, dsk_ref, g_ref, tri_ref, triu_ref, e16_ref,
             dxbc_ref, ddt_ref, dzs_ref, dcw_ref, dcb_ref, dvec_ref, dg_ref,
             dht, ext, dext, dskacc):
        r = pl.program_id(0)
        i = nc - 1 - r

        @pl.when(r == 0)
        def _():
            dht[...] = jnp.zeros_like(dht)
            dext[CH:CH + 8, :] = jnp.zeros((8, CC), F32)
            dskacc[...] = jnp.zeros_like(dskacc)
            dcw_ref[...] = jnp.zeros_like(dcw_ref)
            dcb_ref[...] = jnp.zeros_like(dcb_ref)
            dvec_ref[...] = jnp.zeros_like(dvec_ref)
            dg_ref[...] = jnp.zeros_like(dg_ref)

        xc, sact, act, arow, dtpre, dt, cum, cumx, dtx = _ssd_chunk_fwd_common(
            xbc_ref, halo_ref, dtraw_ref, cw_ref, cb_ref, dtb_ref, alog_ref, tri_ref, e16_ref, ext, i == 0)
        cum_t = cum.T
        xs = act[:, 0:SW]
        lastx = cumx[CH - 1:CH, :]
        xh = xs * dtx
        eexp = jnp.exp(cumx)
        dte = jnp.exp(lastx - cumx)
        cdx = jnp.exp(lastx)
        trim = tri_ref[...] > 0.5
        lane = lax.broadcasted_iota(jnp.int32, (CH, 128), 1)
        rowi = lax.broadcasted_iota(jnp.int32, (CH, 128), 0)

        yv = y_ref[...]
        z = zs_ref[...]
        sz = _sigmoid(z)
        silz = z * sz
        hf = yv * silz
        dn = dos_ref[...] * g_ref[...]
        dhf_parts, nrm_parts = [], []
        for g in range(SG):
            gl = slice(g * GW, (g + 1) * GW)
            hg = hf[:, gl]
            rs = lax.rsqrt(jnp.mean(hg * hg, axis=-1, keepdims=True) + RMS_EPS)
            ng = hg * rs
            dng = dn[:, gl]
            dhf_parts.append(rs * (dng - ng * jnp.mean(dng * ng, axis=-1, keepdims=True)))
            nrm_parts.append(ng)
        nrm = jnp.concatenate(nrm_parts, axis=1)
        dhf = jnp.concatenate(dhf_parts, axis=1)
        dg_ref[...] += jnp.sum(dos_ref[...] * nrm, axis=0, keepdims=True)
        dyv = dhf * silz
        dzs_ref[...] = (dhf * yv * (sz * (1.0 + z * (1.0 - sz)))).astype(BF16)
        dskacc[...] += jnp.sum(dyv * xs, axis=0, keepdims=True)
        dxs_skip = dyv * dsk_ref[...]

        dhtn = dht[...]
        hp = htp_ref[...]
        dlastx = jnp.sum(dhtn * hp, axis=0, keepdims=True) * cdx
        xb = xh.astype(BF16)
        xwf = xh * dte
        dcum = jnp.zeros((CH, 128), F32)
        dcum_t = jnp.zeros((128, CH), F32)
        dxh_parts, dcumx_parts, dlast_parts, db_parts, dc_parts = [], [], [], [], []
        for g in range(SG):
            gl = slice(g * GW, (g + 1) * GW)
            bg = act[:, SW + g * SN:SW + (g + 1) * SN].astype(BF16)
            cg = act[:, SW + SG * SN + g * SN:SW + SG * SN + (g + 1) * SN].astype(BF16)
            hpg = hp[:, gl].astype(BF16)
            dhn = dhtn[:, gl].astype(BF16)
            dyg = dyv[:, gl]
            yoff = eexp[:, gl] * _mm(cg, hpg)
            dz = (dyg * eexp[:, gl]).astype(BF16)
            dcg = _nt(dz, hpg)
            dht[:, gl] = dhtn[:, gl] * cdx[:, gl] + _tn(cg, dz)
            dcumx_g = dyg * yoff
            dbg = _nt(xwf[:, gl].astype(BF16), dhn)
            dxw = _mm(bg, dhn)
            ddte = dxw * xwf[:, gl]
            dcumx_parts.append(dcumx_g - ddte)
            dlast_parts.append(jnp.sum(ddte, axis=0, keepdims=True))
            dxh_g = dxw * dte[:, gl]
            cbm = _nt(cg, bg)
            dcb = jnp.zeros((CH, CH), F32)
            dxp_parts = []
            for pr in range(GW // 128):
                h0 = g * (SH // SG) + 2 * pr
                lo = g * GW + pr * 128
                xp = xb[:, lo:lo + 128]
                dyp = dyv[:, lo:lo + 128]
                dxp = jnp.zeros((CH, 128), F32)
                for idx, hh in enumerate((h0, h0 + 1)):
                    decay = jnp.where(trim, jnp.exp(cum[:, hh:hh + 1] - cum_t[hh:hh + 1, :]), 0.0)
                    mh = cbm * decay
                    keep = (lane < SP) if idx == 0 else (lane >= SP)
                    dym = jnp.where(keep, dyp, 0.0).astype(BF16)
                    dm = _nt(dym, xp)
                    dxp = dxp + _tn(mh.astype(BF16), dym)
                    gm = dm * mh
                    dcum = dcum + jnp.where(lane == hh, jnp.sum(gm, axis=1, keepdims=True), 0.0)
                    dcum_t = dcum_t - jnp.where(rowi == hh, jnp.sum(gm, axis=0, keepdims=True), 0.0)
                    dcb = dcb + dm * decay
                dxp_parts.append(dxp)
            dxh_parts.append(dxh_g + jnp.concatenate(dxp_parts, axis=1))
            dcbb = dcb.astype(BF16)
            dc_parts.append(dcg + _mm(dcbb, bg))
            db_parts.append(dbg + _tn(dcbb, cg))
        dxh = jnp.concatenate(dxh_parts, axis=1)
        dcumx = jnp.concatenate(dcumx_parts, axis=1)
        dlastx = dlastx + jnp.concatenate(dlast_parts, axis=1)
        e16 = e16_ref[...]
        dlast128 = _nt_hi(jnp.broadcast_to(dlastx, (8, SW)), e16)[0:1, :]
        dcum = dcum + dcum_t.T + _nt_hi(dcumx, e16) + jnp.where(rowi == CH - 1, dlast128, 0.0)
        da = _mm_hi(triu_ref[...], dcum)
        ddt = da * arow + _nt_hi(dxh * xs, e16)
        dvec_ref[1:2, :] += jnp.sum(da * dt, axis=0, keepdims=True)
        ddtraw = jnp.where(lane < SH, ddt * _sigmoid(dtpre), 0.0)
        dvec_ref[0:1, :] += jnp.sum(ddtraw, axis=0, keepdims=True)
        ddt_ref[...] = ddtraw.astype(BF16)
        dxs = dxs_skip + dxh * dtx
        dact = jnp.concatenate([dxs] + db_parts + dc_parts, axis=1)
        dxc = dact * (sact * (1.0 + xc * (1.0 - sact)))

        dcb_ref[...] += jnp.sum(dxc, axis=0, keepdims=True)
        for kk in range(CW):
            dcw_ref[kk:kk + 1, :] += jnp.sum(dxc * ext[5 + kk:5 + kk + CH, :], axis=0, keepdims=True)
        dext[0:CH, :] = dxc
        cw = cw_ref[...]
        dxr = cw[CW - 1:CW, :] * dxc
        for kk in range(CW - 1):
            dxr = dxr + cw[kk:kk + 1, :] * dext[CW - 1 - kk:CW - 1 - kk + CH, :]
        dxbc_ref[...] = dxr.astype(BF16)
        dext[CH:CH + 8, :] = dxc[0:8, :]

        @pl.when(r == nc - 1)
        def _():
            dvec_ref[1:2, :] = dvec_ref[1:2, :] * arow
            dvec_ref[2:3, :] = _nt_hi(jnp.broadcast_to(dskacc[...], (8, SW)), e16)[0:1, :]

    rev = lambda n: pl.BlockSpec((CH, n), lambda r: (nc - 1 - r, 0))
    return pl.pallas_call(
        body, name="ssd_bwd", grid=(nc,),
        in_specs=[rev(CC), pl.BlockSpec((8, CC), lambda r: (jnp.maximum((nc - 1 - r) * (CH // 8) - 1, 0), 0)),
                  rev(128), rev(SW), rev(SW), pl.BlockSpec((None, SN, SW), lambda r: (nc - 1 - r, 0, 0)), rev(SW),
                  _const((CW, CC)), _const((1, CC)), _const((1, 128)), _const((1, 128)), _const((1, SW)), _const((1, SW)),
                  _const((CH, CH)), _const((CH, CH)), _const((128, SW))],
        out_specs=[rev(CC), rev(128), rev(SW), _full((CW, CC)), _full((1, CC)), _full((8, 128)), _full((1, SW))],
        out_shape=[jax.ShapeDtypeStruct((S, CC), BF16), jax.ShapeDtypeStruct((S, 128), BF16), jax.ShapeDtypeStruct((S, SW), BF16),
                   jax.ShapeDtypeStruct((CW, CC), F32), jax.ShapeDtypeStruct((1, CC), F32),
                   jax.ShapeDtypeStruct((8, 128), F32), jax.ShapeDtypeStruct((1, SW), F32)],
        scratch_shapes=[pltpu.VMEM((SN, SW), F32), pltpu.VMEM((8 + CH, CC), F32), pltpu.VMEM((CH + 8, CC), F32),
                        pltpu.VMEM((1, SW), F32)],
        compiler_params=_cp(("arbitrary",)),
    )(xbc, xbc, dtraw, zs, y, htp, dossm, conv_w, conv_b, dtb, alog, dskx, gssm, tri, triu, e16)


def _mla_bwd(dqt, dk, dv, qlat, ckv, qg, kvg, wq, wk, wv, pos, invf):
    S = qlat.shape[0]
    tm = min(TQ, dqt.shape[3])
    per = dqt.shape[3] // tm

    def body(dq_ref, dk_ref, dv_ref, ql_ref, ckv_ref, qg_ref, kvg_ref, wq_ref, wk_ref, wv_ref, pos_ref, invf_ref,
             dql_ref, dckv_ref, dkr_ref, dwq_ref, dwk_ref, dwv_ref, dqg_ref, dkvg_ref):
        i = pl.program_id(0)

        @pl.when(i == 0)
        def _():
            dwq_ref[...] = jnp.zeros_like(dwq_ref)
            dwk_ref[...] = jnp.zeros_like(dwk_ref)
            dwv_ref[...] = jnp.zeros_like(dwv_ref)
            dqg_ref[...] = jnp.zeros_like(dqg_ref)
            dkvg_ref[...] = jnp.zeros_like(dkvg_ref)

        ang = pos_ref[...].astype(F32) * invf_ref[...]
        cs = jnp.cos(ang)
        sn = jnp.sin(ang)

        def rms_bwd(v, g, dn, dg_ref):
            r = lax.rsqrt(jnp.mean(v * v, axis=-1, keepdims=True) + RMS_EPS)
            vh = v * r
            dg_ref[...] += jnp.sum(dn * vh, axis=0, keepdims=True)
            dvh = dn * g
            return vh, r * (dvh - vh * jnp.mean(dvh * vh, axis=-1, keepdims=True))

        pieces = []
        for h in range(NH):
            dqh = dq_ref[h].T
            pieces.append(dqh[:, 0:NOPE] * SCALE)
            pieces.append(_rope_t(dqh[:, NOPE:HP], cs, sn) * SCALE)
        dqf = jnp.concatenate(pieces, axis=1).astype(BF16)
        ql = ql_ref[...]
        g = qg_ref[...]
        dqn = _nt(dqf, wq_ref[...])
        qh, dql = rms_bwd(ql, g, dqn, dqg_ref)
        dwq_ref[...] += _tn((qh * g).astype(BF16), dqf)
        dql_ref[...] = dql.astype(BF16)

        dkn_p = jnp.concatenate([dk_ref[h, :, 0:NOPE] for h in range(NH)], axis=1).astype(BF16)
        dvf = jnp.concatenate([dv_ref[h] for h in range(NH)], axis=1).astype(BF16)
        dkr = dk_ref[0, :, NOPE:HP]
        for h in range(1, NH):
            dkr = dkr + dk_ref[h, :, NOPE:HP]
        lane = lax.broadcasted_iota(jnp.int32, dkr.shape, 1)
        dkr_ref[...] = jnp.where(lane < ROPE, _rope_t(dkr, cs, sn), 0.0).astype(BF16)
        cv = ckv_ref[...]
        gk = kvg_ref[...]
        dkn = _nt(dkn_p, wk_ref[...]) + _nt(dvf, wv_ref[...])
        kh, dckv = rms_bwd(cv, gk, dkn, dkvg_ref)
        knb = (kh * gk).astype(BF16)
        dwk_ref[...] += _tn(knb, dkn_p)
        dwv_ref[...] += _tn(knb, dvf)
        dckv_ref[...] = dckv.astype(BF16)

    row = lambda n: pl.BlockSpec((tm, n), lambda i: (i, 0))
    heads = lambda n: pl.BlockSpec((NH, tm, n), lambda i: (0, i, 0))
    return pl.pallas_call(
        body, name="mla_bwd", grid=(S // tm,),
        in_specs=[pl.BlockSpec((NH, None, HP, tm), lambda i: (0, i // per, 0, i % per)), heads(HP), heads(VD), row(QL), row(KVL), _const((1, QL)), _const((1, KVL)),
                  _const((QL, NH * HP)), _const((KVL, NH * NOPE)), _const((KVL, NH * VD)), row(1), _const((1, 128))],
        out_specs=[row(QL), row(KVL), row(128), _full((QL, NH * HP)), _full((KVL, NH * NOPE)), _full((KVL, NH * VD)),
                   _full((1, QL)), _full((1, KVL))],
        out_shape=[jax.ShapeDtypeStruct((S, QL), BF16), jax.ShapeDtypeStruct((S, KVL), BF16), jax.ShapeDtypeStruct((S, 128), BF16),
                   jax.ShapeDtypeStruct((QL, NH * HP), F32), jax.ShapeDtypeStruct((KVL, NH * NOPE), F32),
                   jax.ShapeDtypeStruct((KVL, NH * VD), F32), jax.ShapeDtypeStruct((1, QL), F32), jax.ShapeDtypeStruct((1, KVL), F32)],
        compiler_params=_cp(("arbitrary",)),
    )(dqt, dk, dv, qlat, ckv, qg, kvg, wq, wk, wv, pos, invf)


def _inproj_bwd(x, gx1, mod, win, dql, dckv, dza, dxbc, dzs, dkr, ddt):
    S = x.shape[0]
    tm = min(TM, S)

    def body(x_ref, gx1_ref, mod_ref, win_ref, dql_ref, dckv_ref, dza_ref, dxbc_ref, dzs_ref, dkr_ref, ddt_ref,
             gx_ref, dw_ref, vec_ref):
        i = pl.program_id(0)

        @pl.when(i == 0)
        def _():
            dw_ref[...] = jnp.zeros_like(dw_ref)
            vec_ref[...] = jnp.zeros_like(vec_ref)

        shift = mod_ref[0:1, 0:D]
        scale = mod_ref[0:1, D:2 * D]
        xv = x_ref[...]
        ut = (xv * (1.0 + scale) + shift).T.astype(BF16)
        pieces = (dql_ref, dckv_ref, dza_ref, dxbc_ref, dzs_ref, dkr_ref, ddt_ref)
        du = jnp.zeros((tm, D), F32)
        lo = 0
        for p_ref in pieces:
            n = p_ref.shape[1]
            dp = p_ref[...]
            du = du + _nt(dp, win_ref[:, lo:lo + n])
            dw_ref[:, lo:lo + n] += _mm(ut, dp)
            lo += n
        vec_ref[0:1, :] += jnp.sum(du, axis=0, keepdims=True)
        vec_ref[1:2, :] += jnp.sum(du * xv, axis=0, keepdims=True)
        gx_ref[...] = gx1_ref[...] + du * (1.0 + scale)

    row = lambda n: pl.BlockSpec((tm, n), lambda i: (i, 0))
    return pl.pallas_call(
        body, name="inproj_bwd", grid=(S // tm,),
        in_specs=[row(D), row(D), _const((8, 3 * D)), _const((D, IN_P)), row(QL), row(KVL), row(D), row(CC), row(D),
                  row(128), row(128)],
        out_specs=[row(D), pl.BlockSpec((D, IN_P), lambda i: (0, 0), pipeline_mode=pl.Buffered(1)), _full((8, D))],
        out_shape=[jax.ShapeDtypeStruct((S, D), F32), jax.ShapeDtypeStruct((D, IN_P), F32), jax.ShapeDtypeStruct((8, D), F32)],
        compiler_params=_cp(("arbitrary",)),
    )(x, gx1, mod, win, dql, dckv, dza, dxbc, dzs, dkr, ddt)


def _ada_bwd(callt, dmods):
    w = dmods.shape[1]

    def body(c_ref, d_ref, o_ref):
        acc = c_ref[:, 0:1] * d_ref[0:1, :]
        for s in range(1, 8):
            acc = acc + c_ref[:, s:s + 1] * d_ref[s:s + 1, :]
        o_ref[0] = acc

    return pl.pallas_call(body, name="ada_bwd", out_shape=jax.ShapeDtypeStruct((1, D, w), F32),
                          compiler_params=_cp())(callt, dmods)


def _adamw(name, parts, w, m, v):
    rows, ncol = w.shape
    tr = min(rows, 128)
    nparts = parts.shape[0]

    def body(p_ref, w_ref, m_ref, v_ref, g_ref, d_ref, nm_ref, nv_ref):
        g = p_ref[0].astype(F32)
        for s in range(1, nparts):
            g = g + p_ref[s].astype(F32)
        g_ref[...] = g
        nm = B1 * m_ref[...] + (1.0 - B1) * g
        nv = B2 * v_ref[...] + (1.0 - B2) * (g * g)
        nm_ref[...] = nm
        nv_ref[...] = nv
        m_hat = nm / (1.0 - B1 ** STEP)
        v_hat = nv / (1.0 - B2 ** STEP)
        d_ref[...] = -LR * (m_hat / (jnp.sqrt(v_hat) + EPS) + WD * w_ref[...])

    row = pl.BlockSpec((tr, ncol), lambda i: (i, 0))
    sd = jax.ShapeDtypeStruct((rows, ncol), F32)
    return pl.pallas_call(
        body, name="adamw_" + name, grid=(rows // tr,),
        in_specs=[pl.BlockSpec((nparts, tr, ncol), lambda i: (0, i, 0)), row, row, row],
        out_specs=[row, row, row, row], out_shape=[sd, sd, sd, sd],
        compiler_params=_cp(("arbitrary",)),
    )(parts, w, m, v)


_SMALL = (("b_ada", 3 * D), ("conv_w", CW * CC // 4), ("conv_b", CC), ("ssm_norm_g", SW), ("ln_g", D), ("ln_b", D),
          ("q_norm_g", QL), ("kv_norm_g", KVL), ("dt_bias", SH), ("a_log", SH), ("d_skip", SH), ("loss", 128))


def _pack_small(d, lead):
    flat = [d[name].reshape(d[name].shape[:lead] + (-1,)) for name, _ in _SMALL]
    used = sum(f.shape[lead] for f in flat)
    pad = jnp.zeros(flat[0].shape[:lead] + (R_SMALL * 1024 - used,), F32)
    return jnp.concatenate(flat + [pad], axis=lead).reshape(flat[0].shape[:lead] + (R_SMALL, 1024))


def _unpack_small(p):
    flat = p.reshape(-1)
    out, r = {}, 0
    for name, n in _SMALL:
        out[name] = flat[r:r + n]
        r += n
    return out


def _in_to_padded(w):
    z = lambda n: jnp.zeros((w.shape[0], n), w.dtype)
    return jnp.concatenate([w[:, 0:384], w[:, 384:640], w[:, 704:1728], w[:, 1728:3264], w[:, 3280:4304],
                            w[:, 640:704], z(64), w[:, 3264:3280], z(112)], axis=1)


def _in_from_padded(g):
    return jnp.concatenate([g[:, 0:384], g[:, 384:640], g[:, P_KR[0]:P_KR[0] + 64], g[:, 640:1664], g[:, 1664:3200],
                            g[:, P_DT[0]:P_DT[0] + 16], g[:, 3200:4224]], axis=1)


def kernel(x, c, positions, w_ada, b_ada, w_in, q_norm_g, w_qb, kv_norm_g, w_kvb, conv_w, conv_b, dt_bias, a_log, d_skip, ssm_norm_g, w_out, ln_g, ln_b, loss_target, m_w_ada, m_b_ada, m_w_in, m_q_norm_g, m_w_qb, m_kv_norm_g, m_w_kvb, m_conv_w, m_conv_b, m_dt_bias, m_a_log, m_d_skip, m_ssm_norm_g, m_w_out, m_ln_g, m_ln_b, v_w_ada, v_b_ada, v_w_in, v_q_norm_g, v_w_qb, v_kv_norm_g, v_w_kvb, v_conv_w, v_conv_b, v_dt_bias, v_a_log, v_d_skip, v_ssm_norm_g, v_w_out, v_ln_g, v_ln_b):
    S = x.shape[1]
    xv = x[0]
    tgt = loss_target[0]

    cw16 = jnp.concatenate([conv_w[0], jnp.zeros((16 - CW, CC // 4), F32)], axis=0)
    f_in, f_qb, f_kvb, f_out, f_cw = _gather_weights(
        [w_in[0].astype(BF16), w_qb[0].astype(BF16), w_kvb[0].astype(BF16), w_out[0].astype(BF16), cw16])
    cat1 = lambda f: f.transpose(1, 0, 2).reshape(f.shape[1], 4 * f.shape[2])
    win = _in_to_padded(cat1(f_in))
    wqb = cat1(f_qb).reshape(QL, NH, QKD)
    wq = jnp.concatenate([wqb, jnp.zeros((QL, NH, HP - QKD), BF16)], axis=2).reshape(QL, NH * HP)
    wkvb = cat1(f_kvb).reshape(KVL, NH, NOPE + VD)
    wk = wkvb[:, :, 0:NOPE].reshape(KVL, NH * NOPE)
    wv = wkvb[:, :, NOPE:].reshape(KVL, NH * VD)
    wout = f_out.reshape(MIX, D)
    cwf = cat1(f_cw[:, 0:CW, :])

    half = ROPE // 2
    invf = 1.0 / (ROPE_THETA ** (jnp.arange(half, dtype=F32) / half))
    invf = jnp.concatenate([invf, invf, jnp.zeros((128 - ROPE,), F32)]).reshape(1, 128)
    pos = positions.reshape(S, 1)
    pad128 = lambda a: jnp.concatenate([a.reshape(1, SH), jnp.zeros((1, 128 - SH), F32)], axis=1)
    dtb, alog = pad128(dt_bias), pad128(a_log)
    dskx = jnp.repeat(d_skip.reshape(SH), SP).reshape(1, SW)

    my_c = lax.axis_index("c")
    (call,) = _exchange("gather_c", [jnp.broadcast_to(c.reshape(1, 1, D), (4, 1, D))])
    call = call.reshape(8, D)
    mods = _ada(call, w_ada[0])
    (mrows,) = _exchange("scatter_mod", [mods.reshape(4, 2, 3 * D // 4)])
    mine = lax.dynamic_index_in_dim(mrows.reshape(4, 2, 2, 3 * D // 4)[:, 0], my_c, axis=1, keepdims=False)
    mod = jnp.broadcast_to(mine.reshape(1, 3 * D) + b_ada, (8, 3 * D))
    qlat, ckv, za, xbc, zs, dtraw, q, k, v, kt, vt = _inproj(xv, mod, win, q_norm_g, kv_norm_g, wq, wk, wv, wk.T, wv.T, pos, invf)
    o, lse, mblk, pst = _attn_fwd(q, k, vt)
    y, htp, ossm = _ssd_fwd(xbc, dtraw, zs, cwf, conv_b, dtb, alog, dskx, ssm_norm_g)
    gx1, do, dza, dossm, delta, dwout, vec_o = _outproj(o, za, ossm, xv, tgt, wout, mod, ln_g, ln_b)
    loss_part = jnp.zeros((128,), F32).at[0].set(0.5 / D * jnp.sum(vec_o[0]))

    dk, dv, dq = _attn_bwd(q, k, kt, v, do, lse, delta, mblk, pst)
    dxbc, ddt, dzs, dcw, dcb, dvec, dgssm = _ssd_bwd(xbc, dtraw, zs, y, htp, dossm, cwf, conv_b, dtb, alog, dskx, ssm_norm_g)
    dql, dckv, dkr, dwq, dwk, dwv, dqg, dkvg = _mla_bwd(dq, dk, dv, qlat, ckv, q_norm_g, kv_norm_g, wq, wk, wv, pos, invf)
    gx, dwin, vec_i = _inproj_bwd(xv, gx1, mod, win, dql, dckv, dza, dxbc, dzs, dkr, ddt)
    dmod = jnp.concatenate([vec_i[0:1], vec_i[1:2], vec_o[3:4]], axis=1)

    cols = lambda g: g.reshape(g.shape[0], 4, g.shape[1] // 4).transpose(1, 0, 2)
    g_in = cols(_in_from_padded(dwin)).astype(BF16)
    g_qb = cols(dwq.reshape(QL, NH, HP)[:, :, 0:QKD].reshape(QL, NH * QKD)).astype(BF16)
    g_kvb = cols(jnp.concatenate([dwk.reshape(KVL, NH, NOPE), dwv.reshape(KVL, NH, VD)], axis=2)
                 .reshape(KVL, NH * (NOPE + VD))).astype(BF16)
    g_out = dwout.reshape(4, MIX // 4, D).astype(BF16)
    small = {"b_ada": dmod, "conv_b": dcb, "ssm_norm_g": dgssm, "ln_g": vec_o[1:2], "ln_b": vec_o[2:3],
             "q_norm_g": dqg, "kv_norm_g": dkvg, "dt_bias": dvec[0:1, 0:SH], "a_log": dvec[1:2, 0:SH], "d_skip": dvec[2:3, 0:SH],
             "loss": loss_part}
    small = {n: jnp.broadcast_to(a.reshape(1, -1), (4, a.size)) for n, a in small.items()}
    small["conv_w"] = cols(dcw).reshape(4, CW * CC // 4)
    gsmall = _pack_small(small, 1)

    r_in, r_qb, r_kvb, r_out, rs, dmods = _exchange(
        "exchange_grads", [g_in, g_qb, g_kvb, g_out, gsmall, jnp.broadcast_to(dmod.reshape(1, 1, 3 * D), (4, 1, 3 * D))])
    chip = 2 * lax.axis_index("x") + lax.axis_index("y")
    dmods = lax.dynamic_slice_in_dim(dmods.reshape(8, 3 * D), chip * (3 * D // 4), 3 * D // 4, axis=1)
    g_ada = _ada_bwd(call.T, dmods)
    res = dict(w_ada=_adamw("w_ada", g_ada, w_ada[0], m_w_ada[0], v_w_ada[0]),
               w_in=_adamw("w_in", r_in, w_in[0], m_w_in[0], v_w_in[0]),
               w_qb=_adamw("w_qb", r_qb, w_qb[0], m_w_qb[0], v_w_qb[0]),
               w_kvb=_adamw("w_kvb", r_kvb, w_kvb[0], m_w_kvb[0], v_w_kvb[0]),
               w_out=_adamw("w_out", r_out, w_out[0], m_w_out[0], v_w_out[0]))
    wsm = _pack_small(dict(b_ada=b_ada, conv_w=conv_w, conv_b=conv_b, ssm_norm_g=ssm_norm_g, ln_g=ln_g, ln_b=ln_b,
                           q_norm_g=q_norm_g, kv_norm_g=kv_norm_g, dt_bias=dt_bias, a_log=a_log, d_skip=d_skip, loss=jnp.zeros((128,), F32)), 0)
    msm = _pack_small(dict(b_ada=m_b_ada, conv_w=m_conv_w, conv_b=m_conv_b, ssm_norm_g=m_ssm_norm_g, ln_g=m_ln_g, ln_b=m_ln_b,
                           q_norm_g=m_q_norm_g, kv_norm_g=m_kv_norm_g, dt_bias=m_dt_bias, a_log=m_a_log, d_skip=m_d_skip, loss=jnp.zeros((128,), F32)), 0)
    vsm = _pack_small(dict(b_ada=v_b_ada, conv_w=v_conv_w, conv_b=v_conv_b, ssm_norm_g=v_ssm_norm_g, ln_g=v_ln_g, ln_b=v_ln_b,
                           q_norm_g=v_q_norm_g, kv_norm_g=v_kv_norm_g, dt_bias=v_dt_bias, a_log=v_a_log, d_skip=v_d_skip, loss=jnp.zeros((128,), F32)), 0)
    sm = _adamw("small", rs, wsm, msm, vsm)

    order = ["w_ada", "b_ada", "w_in", "q_norm_g", "w_qb", "kv_norm_g", "w_kvb", "conv_w", "conv_b", "dt_bias", "a_log",
             "d_skip", "ssm_norm_g", "w_out", "ln_g", "ln_b"]
    shapes = dict(w_ada=w_ada.shape, b_ada=b_ada.shape, w_in=w_in.shape, q_norm_g=q_norm_g.shape, w_qb=w_qb.shape,
                  kv_norm_g=kv_norm_g.shape, w_kvb=w_kvb.shape, conv_w=conv_w.shape, conv_b=conv_b.shape, dt_bias=dt_bias.shape,
                  a_log=a_log.shape, d_skip=d_skip.shape, ssm_norm_g=ssm_norm_g.shape, w_out=w_out.shape, ln_g=ln_g.shape,
                  ln_b=ln_b.shape)
    outs = []
    for kind in range(4):
        d = _unpack_small(sm[kind])
        d.update({n: r[kind] for n, r in res.items()})
        outs.extend(d[n].reshape(shapes[n]) for n in order)
    loss = _unpack_small(sm[0])["loss"][0]
    return (loss, gx.reshape(x.shape), *outs)
```

```python
import functools
import math

import numpy as np
import jax
import jax.numpy as jnp
from jax import lax
from jax.experimental import pallas as pl
from jax.experimental.pallas import tpu as pltpu

F32 = jnp.float32
BF16 = jnp.bfloat16
HIGHEST = lax.Precision.HIGHEST
MESH_ID = pl.DeviceIdType.MESH

D = 1024
NH = 8
NOPE = 128
ROPE = 64
VD = 128
VDP = 144
QKD = NOPE + ROPE
HP = 256
QL = 384
KVL = 256
ROPE_THETA = 10000.0
SH = 16
SP = 64
SG = 2
SN = 128
CW = 4
CH = 128
SW = SH * SP
CC = SW + 2 * SG * SN
GW = SW // SG
MIX = 2 * D
IN_W = 4304
ALPHA = 2.0 ** 0.25
RMS_EPS = 1e-6
LN_EPS = 1e-5
SCALE = QKD ** -0.5
LN2 = math.log(2.0)
QSCALE = SCALE / LN2
LR, B1, B2, EPS, WD, STEP = 0.001, 0.9, 0.999, 1e-08, 0.01, 10

P_Q = (0, 384)
P_KV = (384, 640)
P_ZA = (640, 1664)
P_XBC = (1664, 3200)
P_ZS = (3200, 4224)
P_KR = (4224, 4352)
P_DT = (4352, 4480)
IN_P = 4480

R_SMALL = 16

TM = 256
TQ = 512
NDMA = 8
TQF = 1024
VMEM_LIMIT = 56 * 1024 * 1024


def _cp(sem=None):
    return pltpu.CompilerParams(dimension_semantics=sem, vmem_limit_bytes=VMEM_LIMIT)


def _mm(a, b):
    return jnp.dot(a, b, preferred_element_type=F32)


def _nt(a, b):
    return lax.dot_general(a, b, (((1,), (1,)), ((), ())), preferred_element_type=F32)


def _tn(a, b):
    return lax.dot_general(a, b, (((0,), (0,)), ((), ())), preferred_element_type=F32)


def _mm_hi(a, b):
    return jnp.dot(a, b, precision=HIGHEST, preferred_element_type=F32)


def _nt_hi(a, b):
    return lax.dot_general(a, b, (((1,), (1,)), ((), ())), precision=HIGHEST, preferred_element_type=F32)


def _sigmoid(z):
    return 1.0 / (1.0 + jnp.exp(-z))


def _softplus(z):
    return jnp.maximum(z, 0.0) + jnp.log1p(jnp.exp(-jnp.abs(z)))


def _rope(t, cs, sn):
    lane = lax.broadcasted_iota(jnp.int32, t.shape, 1)
    rot = jnp.where(lane < ROPE // 2, -pltpu.roll(t, 128 - ROPE // 2, 1), pltpu.roll(t, ROPE // 2, 1))
    return t * cs + rot * sn


def _rope_t(t, cs, sn):
    lane = lax.broadcasted_iota(jnp.int32, t.shape, 1)
    y = t * sn
    rot = jnp.where(lane < ROPE // 2, -pltpu.roll(y, 128 - ROPE // 2, 1), pltpu.roll(y, ROPE // 2, 1))
    return t * cs - rot


def _full(shape):
    n = len(shape)
    return pl.BlockSpec(shape, lambda *_: (0,) * n)


def _const(shape):
    n = len(shape)
    return pl.BlockSpec(shape, lambda *_: (0,) * n, pipeline_mode=pl.Buffered(1))


def _gather_weights(shards):
    n = len(shards)
    halves = [a.shape[0] // 2 for a in shards]

    def body(*refs):
        srcs, dsts = refs[:n], refs[n:2 * n]
        send_sems, recv_sems, local_sems = refs[2 * n:]
        x, y, c = lax.axis_index("x"), lax.axis_index("y"), lax.axis_index("c")
        me = 2 * x + y
        sibling = (x, y, 1 - c)
        chips = [(1 - x, y), (x, 1 - y), (1 - x, 1 - y)]

        def rows(a, pc):
            return pl.ds(pl.multiple_of(pc * halves[a], halves[a]), halves[a])

        def copy(a, k, src, slot, pc, to):
            return pltpu.make_async_remote_copy(
                src_ref=src, dst_ref=dsts[a].at[slot, rows(a, pc)], send_sem=send_sems.at[a, k],
                recv_sem=recv_sems.at[a, k], device_id=to, device_id_type=MESH_ID)

        local = [pltpu.make_async_copy(srcs[a], dsts[a].at[me], local_sems.at[a]) for a in range(n)]
        for cp in local:
            cp.start()
        sends = [copy(a, j, srcs[a].at[rows(a, c)], me, c, (px, py, c)) for a in range(n) for j, (px, py) in enumerate(chips)]
        for cp in sends:
            cp.start()
        passed = []
        for a in range(n):
            for j, (px, py) in enumerate(chips):
                k = 2 * px + py
                copy(a, j, srcs[a].at[rows(a, c)], k, c, (x, y, c)).wait_recv()
                fwd = copy(a, 3 + j, dsts[a].at[k, rows(a, c)], k, c, sibling)
                fwd.start()
                passed.append(fwd)
        for a in range(n):
            for j, (px, py) in enumerate(chips):
                copy(a, 3 + j, srcs[a].at[rows(a, c)], 2 * px + py, 1 - c, (x, y, c)).wait_recv()
        for cp in sends + passed:
            cp.wait_send()
        for cp in local:
            cp.wait()

    hbm = pl.BlockSpec(memory_space=pltpu.HBM)
    return pl.pallas_call(
        body, name="gather_weights",
        out_shape=tuple(jax.ShapeDtypeStruct((4,) + a.shape, a.dtype) for a in shards),
        in_specs=[hbm] * n, out_specs=tuple([hbm] * n),
        scratch_shapes=[pltpu.SemaphoreType.DMA((n, 6)), pltpu.SemaphoreType.DMA((n, 6)), pltpu.SemaphoreType.DMA((n,))],
    )(*shards)


def _exchange(name, slabs):
    n = len(slabs)

    def body(*refs):
        srcs, dsts = refs[:n], refs[n:2 * n]
        send_sems, recv_sems, local_sems = refs[2 * n:]
        x, y, c = lax.axis_index("x"), lax.axis_index("y"), lax.axis_index("c")
        chip = 2 * x + y
        sibling = (x, y, 1 - c)
        chips = [(1 - x, y), (x, 1 - y), (1 - x, 1 - y)]

        def slot(px, py, pc):
            return 4 * px + 2 * py + pc

        def copy(a, k, src, s, to):
            return pltpu.make_async_remote_copy(
                src_ref=src, dst_ref=dsts[a].at[s], send_sem=send_sems.at[a, k], recv_sem=recv_sems.at[a, k],
                device_id=to, device_id_type=MESH_ID)

        mine = slot(x, y, c)
        local = [pltpu.make_async_copy(srcs[a].at[chip], dsts[a].at[mine], local_sems.at[a]) for a in range(n)]
        for cp in local:
            cp.start()
        first = []
        for a in range(n):
            first.append(copy(a, 0, srcs[a].at[chip], mine, sibling))
            for j, (px, py) in enumerate(chips):
                first.append(copy(a, 1 + j, srcs[a].at[2 * px + py], mine, (px, py, c)))
        for cp in first:
            cp.start()
        passed = []
        for a in range(n):
            for j, (px, py) in enumerate(chips):
                s = slot(px, py, c)
                copy(a, 1 + j, srcs[a].at[chip], s, (x, y, c)).wait_recv()
                fwd = copy(a, 4 + j, dsts[a].at[s], s, sibling)
                fwd.start()
                passed.append(fwd)
        for a in range(n):
            copy(a, 0, srcs[a].at[chip], slot(x, y, 1 - c), (x, y, c)).wait_recv()
            for j, (px, py) in enumerate(chips):
                copy(a, 4 + j, srcs[a].at[chip], slot(px, py, 1 - c), (x, y, c)).wait_recv()
        for cp in first + passed:
            cp.wait_send()
        for cp in local:
            cp.wait()

    hbm = pl.BlockSpec(memory_space=pltpu.HBM)
    return pl.pallas_call(
        body, name=name,
        out_shape=tuple(jax.ShapeDtypeStruct((8,) + a.shape[1:], a.dtype) for a in slabs),
        in_specs=[hbm] * n, out_specs=tuple([hbm] * n),
        scratch_shapes=[pltpu.SemaphoreType.DMA((n, 7)), pltpu.SemaphoreType.DMA((n, 7)), pltpu.SemaphoreType.DMA((n,))],
    )(*slabs)


def _ada(call, w_shard):
    def body(c_ref, w_ref, o_ref):
        o_ref[...] = _mm(c_ref[...].astype(BF16), w_ref[...].astype(BF16))

    return pl.pallas_call(body, name="ada", out_shape=jax.ShapeDtypeStruct((8, w_shard.shape[1]), F32),
                          compiler_params=_cp())(call, w_shard)


def _inproj(x, mod, win, qg, kvg, wq, wk, wv, wkt, wvt, pos, invf):
    S = x.shape[0]
    tm = min(TM, S)

    def body(x_ref, mod_ref, win_ref, qg_ref, kvg_ref, wq_ref, wk_ref, wv_ref, wkt_ref, wvt_ref, pos_ref, invf_ref,
             qlat_ref, ckv_ref, za_ref, xbc_ref, zs_ref, dt_ref, q_ref, k_ref, v_ref, kt_ref, vt_ref):
        shift = mod_ref[0:1, 0:D]
        scale = mod_ref[0:1, D:2 * D]
        u = (x_ref[...] * (1.0 + scale) + shift).astype(BF16)

        def proj(p):
            return _mm(u, win_ref[:, p[0]:p[1]])

        ql = proj(P_Q)
        ckv = proj(P_KV)
        qlat_ref[...] = ql
        ckv_ref[...] = ckv
        za_ref[...] = proj(P_ZA)
        xbc_ref[...] = proj(P_XBC)
        zs_ref[...] = proj(P_ZS)
        dt_ref[...] = proj(P_DT)
        kr = proj(P_KR)

        ang = pos_ref[...].astype(F32) * invf_ref[...]
        cs = jnp.cos(ang)
        sn = jnp.sin(ang)

        rq = lax.rsqrt(jnp.mean(ql * ql, axis=-1, keepdims=True) + RMS_EPS)
        qn = (ql * rq * qg_ref[...]).astype(BF16)
        for h in range(NH):
            qh = _mm(qn, wq_ref[:, h * HP:(h + 1) * HP])
            q_ref[h, :, 0:NOPE] = (qh[:, 0:NOPE] * QSCALE).astype(BF16)
            q_ref[h, :, NOPE:HP] = (_rope(qh[:, NOPE:HP], cs, sn) * QSCALE).astype(BF16)

        rk = lax.rsqrt(jnp.mean(ckv * ckv, axis=-1, keepdims=True) + RMS_EPS)
        kn = (ckv * rk * kvg_ref[...]).astype(BF16)
        knope = _mm(kn, wk_ref[...])
        vall = _mm(kn, wv_ref[...])
        krf = _rope(kr, cs, sn)
        krr = krf.astype(BF16)
        krt = krf.T.astype(BF16)
        ones_rows = jnp.where(lax.broadcasted_iota(jnp.int32, (VDP - VD, tm), 0) == 0, 1.0, 0.0).astype(BF16)
        for h in range(NH):
            k_ref[h, :, 0:NOPE] = knope[:, h * NOPE:(h + 1) * NOPE].astype(BF16)
            k_ref[h, :, NOPE:HP] = krr
            v_ref[h] = vall[:, h * VD:(h + 1) * VD].astype(BF16)
            kt_ref[h, 0:NOPE, :] = _nt(wkt_ref[h * NOPE:(h + 1) * NOPE, :], kn).astype(BF16)
            kt_ref[h, NOPE:HP, :] = krt
            vt_ref[h, 0:VD, :] = _nt(wvt_ref[h * VD:(h + 1) * VD, :], kn).astype(BF16)
            vt_ref[h, VD:VDP, :] = ones_rows

    row = lambda n: pl.BlockSpec((tm, n), lambda i: (i, 0))
    heads = lambda n: pl.BlockSpec((NH, tm, n), lambda i: (0, i, 0))
    heads_t = lambda n: pl.BlockSpec((NH, None, n, tm), lambda i: (0, i, 0, 0))
    sd = lambda n: jax.ShapeDtypeStruct((S, n), F32)
    hd = lambda n: jax.ShapeDtypeStruct((NH, S, n), BF16)
    ht = lambda n: jax.ShapeDtypeStruct((NH, S // tm, n, tm), BF16)
    return pl.pallas_call(
        body, name="inproj", grid=(S // tm,),
        in_specs=[row(D), _const((8, 3 * D)), _const((D, IN_P)), _const((1, QL)), _const((1, KVL)),
                  _const((QL, NH * HP)), _const((KVL, NH * NOPE)), _const((KVL, NH * VD)),
                  _const((NH * NOPE, KVL)), _const((NH * VD, KVL)), row(1), _const((1, 128))],
        out_specs=[row(QL), row(KVL), row(D), row(CC), row(D), row(128), heads(HP), heads(HP), heads(VD),
                   heads_t(HP), heads_t(VDP)],
        out_shape=[sd(QL), sd(KVL), sd(D), sd(CC), sd(D), sd(128), hd(HP), hd(HP), hd(VD), ht(HP), ht(VDP)],
        compiler_params=_cp(("arbitrary",)),
    )(x, mod, win, qg, kvg, wq, wk, wv, wkt, wvt, pos, invf)


def _attn_fwd(q, k, vt):
    _, S, _ = q.shape
    tq = min(TQF, S)
    nq = S // tq
    half = tq // 2
    tb = vt.shape[3]
    nsb = half // tb
    nblk = nq * (nq + 1)

    def body(q_ref, k_ref, vt_ref, o_ref, lse_ref, mblk_ref, p_hbm, pbuf, sems):
        h = pl.program_id(0)
        i = pl.program_id(1)
        qb = q_ref[...]
        base = i * (i + 1)

        def scores(j, hb):
            off = pl.multiple_of(j * tq + hb * half, half)
            return _nt(k_ref[pl.ds(off, half), :], qb)

        def p_copy(j, par, hb, rows=None):
            src_ref, dst_ref = pbuf.at[2 * par + hb], p_hbm.at[h, base + 2 * j + hb]
            if rows is not None:
                src_ref, dst_ref = src_ref.at[rows], dst_ref.at[rows]
            return pltpu.make_async_copy(src_ref, dst_ref, sems.at[2 * par + hb])

        def p_start(j, par, hb):
            for ch in range(NDMA):
                p_copy(j, par, hb, pl.ds(ch * (half // NDMA), half // NDMA)).start()

        def update(j, par, hb, s, carry):
            m, acc = carry
            m_new = jnp.maximum(m, jnp.max(s, axis=0, keepdims=True))
            a = jnp.exp2(m - m_new)
            pbuf[2 * par + hb] = jnp.exp2(s - m_new).astype(BF16)
            mblk_ref[2 * j + hb] = m_new
            acc = a * acc
            for sb in range(nsb):
                acc = acc + _mm(vt_ref[(2 * j + hb) * nsb + sb], pbuf[2 * par + hb, sb * tb:(sb + 1) * tb, :])
            return m_new, acc

        def trip(j, par, carry, masked):
            @pl.when(j >= 2)
            def _():
                for hb in range(2):
                    p_copy(j - 2, par, hb).wait()

            s = [scores(j, hb) for hb in range(2)]
            if masked:
                r = lax.broadcasted_iota(jnp.int32, s[0].shape, 0)
                cidx = lax.broadcasted_iota(jnp.int32, s[0].shape, 1)
                s = [jnp.where(cidx >= r + hb * half, s[hb], -1e30) for hb in range(2)]
            for hb in range(2):
                carry = update(j, par, hb, s[hb], carry)
            for hb in range(2):
                p_start(j, par, hb)
            return carry

        def finish(carry):
            m, acc = carry
            l = acc[VD:VD + 1, :]
            o_ref[...] = (acc[0:VD, :] / l).T
            lse_ref[...] = m + jnp.log2(l)

        init = (jnp.full((1, tq), -1e30, F32), jnp.zeros((VDP, tq), F32))
        carry = lax.fori_loop(0, i >> 1, lambda t, cr: trip(2 * t + 1, 1, trip(2 * t, 0, cr, False), False), init)

        @pl.when((i & 1) == 0)
        def _():
            finish(trip(i, 0, carry, True))
            for hb in range(2):
                p_copy(i, 0, hb).wait()

            @pl.when(i >= 1)
            def _():
                for hb in range(2):
                    p_copy(i - 1, 1, hb).wait()

        @pl.when((i & 1) == 1)
        def _():
            finish(trip(i, 1, trip(i - 1, 0, carry, False), True))
            for hb in range(2):
                p_copy(i - 1, 0, hb).wait()
                p_copy(i, 1, hb).wait()

    return pl.pallas_call(
        body, name="attn_fwd", grid=(NH, nq),
        in_specs=[pl.BlockSpec((None, tq, HP), lambda h, i: (h, i, 0)),
                  pl.BlockSpec((None, S, HP), lambda h, i: (h, 0, 0)),
                  pl.BlockSpec((None, S // tb, VDP, tb), lambda h, i: (h, 0, 0, 0))],
        out_specs=[pl.BlockSpec((tq, VD), lambda h, i: (i, h)),
                   pl.BlockSpec((None, None, 1, tq), lambda h, i: (h, i, 0, 0)),
                   pl.BlockSpec((None, None, 2 * nq, 1, tq), lambda h, i: (h, i, 0, 0, 0)),
                   pl.BlockSpec(memory_space=pltpu.HBM)],
        out_shape=[jax.ShapeDtypeStruct((S, NH * VD), F32), jax.ShapeDtypeStruct((NH, nq, 1, tq), F32),
                   jax.ShapeDtypeStruct((NH, nq, 2 * nq, 1, tq), F32),
                   jax.ShapeDtypeStruct((NH, nblk, half, tq), BF16)],
        scratch_shapes=[pltpu.VMEM((4, half, tq), BF16), pltpu.SemaphoreType.DMA((4,))],
        compiler_params=_cp(("arbitrary", "arbitrary")),
    )(q, k, vt)


def _ssd_consts():
    tri = np.tril(np.ones((CH, CH), np.float32))
    e16 = np.zeros((128, SW), np.float32)
    for h in range(SH):
        e16[h, h * SP:(h + 1) * SP] = 1.0
    return jnp.asarray(tri), jnp.asarray(tri.T.copy()), jnp.asarray(e16)


def _ssd_chunk_fwd_common(xbc_ref, halo_ref, dtraw_ref, cw_ref, cb_ref, dtb_ref, alog_ref, tri_ref, e16_ref, ext, first):
    ext[0:8, :] = jnp.where(first, 0.0, halo_ref[...])
    ext[8:8 + CH, :] = xbc_ref[...]
    cw = cw_ref[...]
    xc = cb_ref[...] + cw[0:1, :] * ext[5:5 + CH, :]
    for kk in range(1, CW):
        xc = xc + cw[kk:kk + 1, :] * ext[5 + kk:5 + kk + CH, :]
    sact = _sigmoid(xc)
    act = xc * sact
    lane = lax.broadcasted_iota(jnp.int32, (1, 128), 1)
    arow = jnp.where(lane < SH, -jnp.exp(alog_ref[...]), 0.0)
    dtpre = dtraw_ref[...] + dtb_ref[...]
    dt = _softplus(dtpre)
    a = dt * arow
    cum = _mm_hi(tri_ref[...], a)
    cumx = _mm_hi(cum, e16_ref[...])
    dtx = _mm_hi(dt, e16_ref[...])
    return xc, sact, act, arow, dtpre, dt, cum, cumx, dtx


def _ssd_fwd(xbc, dtraw, zs, conv_w, conv_b, dtb, alog, dskx, gssm):
    S = xbc.shape[0]
    nc = S // CH
    tri, _, e16 = _ssd_consts()

    def body(xbc_ref, halo_ref, dtraw_ref, zs_ref, cw_ref, cb_ref, dtb_ref, alog_ref, dsk_ref, g_ref, tri_ref, e16_ref,
             y_ref, htp_ref, o_ref, ht, ext):
        i = pl.program_id(0)

        @pl.when(i == 0)
        def _():
            ht[...] = jnp.zeros_like(ht)

        xc, sact, act, arow, dtpre, dt, cum, cumx, dtx = _ssd_chunk_fwd_common(
            xbc_ref, halo_ref, dtraw_ref, cw_ref, cb_ref, dtb_ref, alog_ref, tri_ref, e16_ref, ext, i == 0)
        cum_t = cum.T
        xs = act[:, 0:SW]
        lastx = cumx[CH - 1:CH, :]
        xh = xs * dtx
        eexp = jnp.exp(cumx)
        dte = jnp.exp(lastx - cumx)
        cdx = jnp.exp(lastx)
        htp = ht[...]
        htp_ref[...] = htp
        xw = (xh * dte).astype(BF16)
        xb = xh.astype(BF16)
        trim = tri_ref[...] > 0.5
        lane = lax.broadcasted_iota(jnp.int32, (CH, 128), 1)
        parts = []
        for g in range(SG):
            gl = slice(g * GW, (g + 1) * GW)
            bg = act[:, SW + g * SN:SW + (g + 1) * SN].astype(BF16)
            cg = act[:, SW + SG * SN + g * SN:SW + SG * SN + (g + 1) * SN].astype(BF16)
            cbm = _nt(cg, bg)
            yoff = eexp[:, gl] * _mm(cg, htp[:, gl].astype(BF16))
            ht[:, gl] = htp[:, gl] * cdx[:, gl] + _tn(bg, xw[:, gl])
            for pr in range(GW // 128):
                h0 = g * (SH // SG) + 2 * pr
                lo = g * GW + pr * 128
                xp = xb[:, lo:lo + 128]
                res = []
                for hh in (h0, h0 + 1):
                    seg = cum[:, hh:hh + 1] - cum_t[hh:hh + 1, :]
                    mh = jnp.where(trim, cbm * jnp.exp(seg), 0.0).astype(BF16)
                    res.append(_mm(mh, xp))
                parts.append(jnp.where(lane < SP, res[0], res[1]) + yoff[:, pr * 128:(pr + 1) * 128])
        y = jnp.concatenate(parts, axis=1) + xs * dsk_ref[...]
        y_ref[...] = y
        z = zs_ref[...]
        hf = y * (z * _sigmoid(z))
        outs = []
        for g in range(SG):
            hg = hf[:, g * GW:(g + 1) * GW]
            rs = lax.rsqrt(jnp.mean(hg * hg, axis=-1, keepdims=True) + RMS_EPS)
            outs.append(hg * rs)
        o_ref[...] = (jnp.concatenate(outs, axis=1) * g_ref[...]).astype(BF16)

    row = lambda n: pl.BlockSpec((CH, n), lambda i: (i, 0))
    return pl.pallas_call(
        body, name="ssd_fwd", grid=(nc,),
        in_specs=[row(CC), pl.BlockSpec((8, CC), lambda i: (jnp.maximum(i * (CH // 8) - 1, 0), 0)), row(128), row(SW),
                  _const((CW, CC)), _const((1, CC)), _const((1, 128)), _const((1, 128)), _const((1, SW)), _const((1, SW)),
                  _const((CH, CH)), _const((128, SW))],
        out_specs=[row(SW), pl.BlockSpec((None, SN, SW), lambda i: (i, 0, 0)), row(SW)],
        out_shape=[jax.ShapeDtypeStruct((S, SW), F32), jax.ShapeDtypeStruct((nc, SN, SW), F32),
                   jax.ShapeDtypeStruct((S, SW), BF16)],
        scratch_shapes=[pltpu.VMEM((SN, SW), F32), pltpu.VMEM((8 + CH, CC), F32)],
        compiler_params=_cp(("arbitrary",)),
    )(xbc, xbc, dtraw, zs, conv_w, conv_b, dtb, alog, dskx, gssm, tri, e16)


def _outproj(o, za, ossm, x, tgt, wout, mod, ln_g, ln_b):
    S = x.shape[0]
    tm = min(TM, S)

    def body(o_ref, za_ref, os_ref, x_ref, t_ref, w_ref, mod_ref, g_ref, b_ref,
             gx_ref, do_ref, dza_ref, dos_ref, delta_ref, dw_ref, vec_ref):
        i = pl.program_id(0)

        @pl.when(i == 0)
        def _():
            dw_ref[...] = jnp.zeros_like(dw_ref)
            vec_ref[...] = jnp.zeros_like(vec_ref)

        gate = mod_ref[0:1, 2 * D:3 * D]
        ov = o_ref[...]
        z = za_ref[...]
        sz = _sigmoid(z)
        silz = z * sz
        a = (ov * silz).astype(BF16)
        osb = os_ref[...]
        mixed = _mm(a, w_ref[0:D, :]) + _mm(osb, w_ref[D:MIX, :])
        xv = x_ref[...]
        hres = ALPHA * xv + gate * mixed
        mu = jnp.mean(hres, axis=-1, keepdims=True)
        hc = hres - mu
        var = jnp.mean(hc * hc, axis=-1, keepdims=True)
        rstd = lax.rsqrt(var + LN_EPS)
        xhat = hc * rstd
        g = g_ref[...]
        yv = xhat * g + b_ref[...]
        err = yv - t_ref[...]
        dy = err * (1.0 / D)
        vec_ref[0:1, :] += jnp.sum(err * err, axis=0, keepdims=True)
        vec_ref[1:2, :] += jnp.sum(dy * xhat, axis=0, keepdims=True)
        vec_ref[2:3, :] += jnp.sum(dy, axis=0, keepdims=True)
        dxh = dy * g
        dh = rstd * (dxh - jnp.mean(dxh, axis=-1, keepdims=True) - xhat * jnp.mean(dxh * xhat, axis=-1, keepdims=True))
        gx_ref[...] = ALPHA * dh
        vec_ref[3:4, :] += jnp.sum(dh * mixed, axis=0, keepdims=True)
        dmixed = (gate * dh).astype(BF16)
        dw_ref[0:D, :] += _tn(a, dmixed)
        dw_ref[D:MIX, :] += _tn(osb, dmixed)
        da = _nt(dmixed, w_ref[0:D, :])
        dos_ref[...] = _nt(dmixed, w_ref[D:MIX, :])
        dov = da * silz
        do_ref[...] = dov.astype(BF16)
        dza_ref[...] = (da * ov * (sz * (1.0 + z * (1.0 - sz)))).astype(BF16)
        pr = dov * ov
        for h in range(NH):
            delta_ref[h] = jnp.sum(pr[:, h * VD:(h + 1) * VD], axis=-1, keepdims=True)

    row = lambda n: pl.BlockSpec((tm, n), lambda i: (i, 0))
    return pl.pallas_call(
        body, name="outproj", grid=(S // tm,),
        in_specs=[row(D), row(D), row(D), row(D), row(D), _const((MIX, D)), _const((8, 3 * D)), _const((1, D)), _const((1, D))],
        out_specs=[row(D), row(D), row(D), row(D), pl.BlockSpec((NH, tm, 1), lambda i: (0, i, 0)),
                   _full((MIX, D)), _full((8, D))],
        out_shape=[jax.ShapeDtypeStruct((S, D), F32), jax.ShapeDtypeStruct((S, D), BF16), jax.ShapeDtypeStruct((S, D), BF16),
                   jax.ShapeDtypeStruct((S, D), F32), jax.ShapeDtypeStruct((NH, S, 1), F32),
                   jax.ShapeDtypeStruct((MIX, D), F32), jax.ShapeDtypeStruct((8, D), F32)],
        compiler_params=_cp(("arbitrary",)),
    )(o, za, ossm, x, tgt, wout, mod, ln_g, ln_b)


def _attn_bwd(q, k, kt, v, do, lse, delta, mblk, pst):
    _, S, _ = q.shape
    tk = min(TQ, S // 2)
    nk = S // tk
    tq = 2 * tk
    nq = S // tq
    tb = kt.shape[3]
    nsb = tk // tb

    assert pst.shape[2:] == (tk, tq)

    def body(k_ref, kt_ref, v_ref, q_ref, do_ref, lse_ref, dl_ref, mblk_ref, p_hbm, dk_ref, dv_ref, dqt_ref, pin, sems):
        h = pl.program_id(0)
        j = pl.program_id(1)
        kb = k_ref[...]
        vb = v_ref[...]
        first = j >> 1

        @pl.when(j == 0)
        def _():
            dqt_ref[...] = jnp.zeros_like(dqt_ref)

        dk_ref[...] = jnp.zeros_like(dk_ref)
        dv_ref[...] = jnp.zeros_like(dv_ref)

        def fetch(i, slot, rows=None):
            src_ref, dst_ref = p_hbm.at[h, i * (i + 1) + j], pin.at[slot]
            if rows is not None:
                src_ref, dst_ref = src_ref.at[rows], dst_ref.at[rows]
            return pltpu.make_async_copy(src_ref, dst_ref, sems.at[slot])

        def fetch_start(i, slot):
            for ch in range(NDMA):
                fetch(i, slot, pl.ds(ch * (tk // NDMA), tk // NDMA)).start()

        def step(i, slot, lo=0):
            @pl.when(i + 1 < nq)
            def _():
                fetch_start(i + 1, 1 - slot)

            fetch(i, slot).wait()
            off = pl.multiple_of(i * tq + lo, tk)
            qb = q_ref[pl.ds(off, tq - lo), :]
            dob = do_ref[pl.ds(off, tq - lo), :]
            sc = jnp.exp2(mblk_ref[i] - lse_ref[i])[:, lo:tq]
            pt = pin[slot, :, lo:tq].astype(F32) * sc
            dv_ref[...] += _mm(pt.astype(BF16), dob)
            dsb = (pt * (_nt(vb, dob) - dl_ref[i][:, lo:tq])).astype(BF16)
            dk_ref[...] += _mm(dsb, qb)
            acc = dqt_ref[i, :, lo:tq]
            for sb in range(nsb):
                acc = acc + _mm(kt_ref[sb], dsb[sb * tb:(sb + 1) * tb, :])
            dqt_ref[i, :, lo:tq] = acc

        fetch_start(first, 0)

        @pl.when((j & 1) == 0)
        def _():
            step(first, 0)

        @pl.when((j & 1) == 1)
        def _():
            step(first, 0, tk)

        def loop_body(i, carry):
            step(i, (i - first) & 1)
            return carry

        lax.fori_loop(first + 1, nq, loop_body, 0)
        dk_ref[...] = dk_ref[...] * LN2

    return pl.pallas_call(
        body, name="attn_bwd", grid=(NH, nk),
        in_specs=[pl.BlockSpec((None, tk, HP), lambda h, j: (h, j, 0)),
                  pl.BlockSpec((None, nsb, HP, tb), lambda h, j: (h, j, 0, 0)),
                  pl.BlockSpec((None, tk, VD), lambda h, j: (h, j, 0)),
                  pl.BlockSpec((None, S, HP), lambda h, j: (h, 0, 0)),
                  pl.BlockSpec((S, VD), lambda h, j: (0, h)),
                  pl.BlockSpec((None, nq, 1, tq), lambda h, j: (h, 0, 0, 0)),
                  pl.BlockSpec((None, nq, 1, tq), lambda h, j: (h, 0, 0, 0)),
                  pl.BlockSpec((None, nq, None, 1, tq), lambda h, j: (h, 0, j, 0, 0)),
                  pl.BlockSpec(memory_space=pltpu.HBM)],
        out_specs=[pl.BlockSpec((None, tk, HP), lambda h, j: (h, j, 0)),
                   pl.BlockSpec((None, tk, VD), lambda h, j: (h, j, 0)),
                   pl.BlockSpec((None, nq, HP, tq), lambda h, j: (h, 0, 0, 0), pipeline_mode=pl.Buffered(1))],
        out_shape=[jax.ShapeDtypeStruct((NH, S, HP), F32), jax.ShapeDtypeStruct((NH, S, VD), F32),
                   jax.ShapeDtypeStruct((NH, nq, HP, tq), F32)],
        scratch_shapes=[pltpu.VMEM((2, tk, tq), BF16), pltpu.SemaphoreType.DMA((2,))],
        compiler_params=_cp(("arbitrary", "arbitrary")),
    )(k, kt, v, q, do, lse.reshape(NH, nq, 1, tq), delta.reshape(NH, nq, 1, tq), mblk, pst)


def _ssd_bwd(xbc, dtraw, zs, y, htp, dossm, conv_w, conv_b, dtb, alog, dskx, gssm):
    S = xbc.shape[0]
    nc = S // CH
    tri, triu, e16 = _ssd_consts()

    def body(xbc_ref, halo_ref, dtraw_ref, zs_ref, y_ref, htp_ref, dos_ref,
             cw_ref, cb_ref, dtb_ref, alog_ref, dsk_ref, g_ref, tri_ref, triu_ref, e16_ref,
             dxbc_ref, ddt_ref, dzs_ref, dcw_ref, dcb_ref, dvec_ref, dg_ref,
             dht, ext, dext, dskacc):
        r = pl.program_id(0)
        i = nc - 1 - r

        @pl.when(r == 0)
        def _():
            dht[...] = jnp.zeros_like(dht)
            dext[CH:CH + 8, :] = jnp.zeros((8, CC), F32)
            dskacc[...] = jnp.zeros_like(dskacc)
            dcw_ref[...] = jnp.zeros_like(dcw_ref)
            dcb_ref[...] = jnp.zeros_like(dcb_ref)
            dvec_ref[...] = jnp.zeros_like(dvec_ref)
            dg_ref[...] = jnp.zeros_like(dg_ref)

        xc, sact, act, arow, dtpre, dt, cum, cumx, dtx = _ssd_chunk_fwd_common(
            xbc_ref, halo_ref, dtraw_ref, cw_ref, cb_ref, dtb_ref, alog_ref, tri_ref, e16_ref, ext, i == 0)
        cum_t = cum.T
        xs = act[:, 0:SW]
        lastx = cumx[CH - 1:CH, :]
        xh = xs * dtx
        eexp = jnp.exp(cumx)
        dte = jnp.exp(lastx - cumx)
        cdx = jnp.exp(lastx)
        trim = tri_ref[...] > 0.5
        lane = lax.broadcasted_iota(jnp.int32, (CH, 128), 1)
        rowi = lax.broadcasted_iota(jnp.int32, (CH, 128), 0)

        yv = y_ref[...]
        z = zs_ref[...]
        sz = _sigmoid(z)
        silz = z * sz
        hf = yv * silz
        dn = dos_ref[...] * g_ref[...]
        dhf_parts, nrm_parts = [], []
        for g in range(SG):
            gl = slice(g * GW, (g + 1) * GW)
            hg = hf[:, gl]
            rs = lax.rsqrt(jnp.mean(hg * hg, axis=-1, keepdims=True) + RMS_EPS)
            ng = hg * rs
            dng = dn[:, gl]
            dhf_parts.append(rs * (dng - ng * jnp.mean(dng * ng, axis=-1, keepdims=True)))
            nrm_parts.append(ng)
        nrm = jnp.concatenate(nrm_parts, axis=1)
        dhf = jnp.concatenate(dhf_parts, axis=1)
        dg_ref[...] += jnp.sum(dos_ref[...] * nrm, axis=0, keepdims=True)
        dyv = dhf * silz
        dzs_ref[...] = (dhf * yv * (sz * (1.0 + z * (1.0 - sz)))).astype(BF16)
        dskacc[...] += jnp.sum(dyv * xs, axis=0, keepdims=True)
        dxs_skip = dyv * dsk_ref[...]

        dhtn = dht[...]
        hp = htp_ref[...]
        dlastx = jnp.sum(dhtn * hp, axis=0, keepdims=True) * cdx
        xb = xh.astype(BF16)
        xwf = xh * dte
        dcum = jnp.zeros((CH, 128), F32)
        dcum_t = jnp.zeros((128, CH), F32)
        dxh_parts, dcumx_parts, dlast_parts, db_parts, dc_parts = [], [], [], [], []
        for g in range(SG):
            gl = slice(g * GW, (g + 1) * GW)
            bg = act[:, SW + g * SN:SW + (g + 1) * SN].astype(BF16)
            cg = act[:, SW + SG * SN + g * SN:SW + SG * SN + (g + 1) * SN].astype(BF16)
            hpg = hp[:, gl].astype(BF16)
            dhn = dhtn[:, gl].astype(BF16)
            dyg = dyv[:, gl]
            yoff = eexp[:, gl] * _mm(cg, hpg)
            dz = (dyg * eexp[:, gl]).astype(BF16)
            dcg = _nt(dz, hpg)
            dht[:, gl] = dhtn[:, gl] * cdx[:, gl] + _tn(cg, dz)
            dcumx_g = dyg * yoff
            dbg = _nt(xwf[:, gl].astype(BF16), dhn)
            dxw = _mm(bg, dhn)
            ddte = dxw * xwf[:, gl]
            dcumx_parts.append(dcumx_g - ddte)
            dlast_parts.append(jnp.sum(ddte, axis=0, keepdims=True))
            dxh_g = dxw * dte[:, gl]
            cbm = _nt(cg, bg)
            dcb = jnp.zeros((CH, CH), F32)
            dxp_parts = []
            for pr in range(GW // 128):
                h0 = g * (SH // SG) + 2 * pr
                lo = g * GW + pr * 128
                xp = xb[:, lo:lo + 128]
                dyp = dyv[:, lo:lo + 128]
                dxp = jnp.zeros((CH, 128), F32)
                for idx, hh in enumerate((h0, h0 + 1)):
                    decay = jnp.where(trim, jnp.exp(cum[:, hh:hh + 1] - cum_t[hh:hh + 1, :]), 0.0)
                    mh = cbm * decay
                    keep = (lane < SP) if idx == 0 else (lane >= SP)
                    dym = jnp.where(keep, dyp, 0.0).astype(BF16)
                    dm = _nt(dym, xp)
                    dxp = dxp + _tn(mh.astype(BF16), dym)
                    gm = dm * mh
                    dcum = dcum + jnp.where(lane == hh, jnp.sum(gm, axis=1, keepdims=True), 0.0)
                    dcum_t = dcum_t - jnp.where(rowi == hh, jnp.sum(gm, axis=0, keepdims=True), 0.0)
                    dcb = dcb + dm * decay
                dxp_parts.append(dxp)
            dxh_parts.append(dxh_g + jnp.concatenate(dxp_parts, axis=1))
            dcbb = dcb.astype(BF16)
            dc_parts.append(dcg + _mm(dcbb, bg))
            db_parts.append(dbg + _tn(dcbb, cg))
        dxh = jnp.concatenate(dxh_parts, axis=1)
        dcumx = jnp.concatenate(dcumx_parts, axis=1)
        dlastx = dlastx + jnp.concatenate(dlast_parts, axis=1)
        e16 = e16_ref[...]
        dlast128 = _nt_hi(jnp.broadcast_to(dlastx, (8, SW)), e16)[0:1, :]
        dcum = dcum + dcum_t.T + _nt_hi(dcumx, e16) + jnp.where(rowi == CH - 1, dlast128, 0.0)
        da = _mm_hi(triu_ref[...], dcum)
        ddt = da * arow + _nt_hi(dxh * xs, e16)
        dvec_ref[1:2, :] += jnp.sum(da * dt, axis=0, keepdims=True)
        ddtraw = jnp.where(lane < SH, ddt * _sigmoid(dtpre), 0.0)
        dvec_ref[0:1, :] += jnp.sum(ddtraw, axis=0, keepdims=True)
        ddt_ref[...] = ddtraw.astype(BF16)
        dxs = dxs_skip + dxh * dtx
        dact = jnp.concatenate([dxs] + db_parts + dc_parts, axis=1)
        dxc = dact * (sact * (1.0 + xc * (1.0 - sact)))

        dcb_ref[...] += jnp.sum(dxc, axis=0, keepdims=True)
        for kk in range(CW):
            dcw_ref[kk:kk + 1, :] += jnp.sum(dxc * ext[5 + kk:5 + kk + CH, :], axis=0, keepdims=True)
        dext[0:CH, :] = dxc
        cw = cw_ref[...]
        dxr = cw[CW - 1:CW, :] * dxc
        for kk in range(CW - 1):
            dxr = dxr + cw[kk:kk + 1, :] * dext[CW - 1 - kk:CW - 1 - kk + CH, :]
        dxbc_ref[...] = dxr.astype(BF16)
        dext[CH:CH + 8, :] = dxc[0:8, :]

        @pl.when(r == nc - 1)
        def _():
            dvec_ref[1:2, :] = dvec_ref[1:2, :] * arow
            dvec_ref[2:3, :] = _nt_hi(jnp.broadcast_to(dskacc[...], (8, SW)), e16)[0:1, :]

    rev = lambda n: pl.BlockSpec((CH, n), lambda r: (nc - 1 - r, 0))
    return pl.pallas_call(
        body, name="ssd_bwd", grid=(nc,),
        in_specs=[rev(CC), pl.BlockSpec((8, CC), lambda r: (jnp.maximum((nc - 1 - r) * (CH // 8) - 1, 0), 0)),
                  rev(128), rev(SW), rev(SW), pl.BlockSpec((None, SN, SW), lambda r: (nc - 1 - r, 0, 0)), rev(SW),
                  _const((CW, CC)), _const((1, CC)), _const((1, 128)), _const((1, 128)), _const((1, SW)), _const((1, SW)),
                  _const((CH, CH)), _const((CH, CH)), _const((128, SW))],
        out_specs=[rev(CC), rev(128), rev(SW), _full((CW, CC)), _full((1, CC)), _full((8, 128)), _full((1, SW))],
        out_shape=[jax.ShapeDtypeStruct((S, CC), BF16), jax.ShapeDtypeStruct((S, 128), BF16), jax.ShapeDtypeStruct((S, SW), BF16),
                   jax.ShapeDtypeStruct((CW, CC), F32), jax.ShapeDtypeStruct((1, CC), F32),
                   jax.ShapeDtypeStruct((8, 128), F32), jax.ShapeDtypeStruct((1, SW), F32)],
        scratch_shapes=[pltpu.VMEM((SN, SW), F32), pltpu.VMEM((8 + CH, CC), F32), pltpu.VMEM((CH + 8, CC), F32),
                        pltpu.VMEM((1, SW), F32)],
        compiler_params=_cp(("arbitrary",)),
    )(xbc, xbc, dtraw, zs, y, htp, dossm, conv_w, conv_b, dtb, alog, dskx, gssm, tri, triu, e16)


def _mla_bwd(dqt, dk, dv, qlat, ckv, qg, kvg, wq, wk, wv, pos, invf):
    S = qlat.shape[0]
    tm = min(TQ, dqt.shape[3])
    per = dqt.shape[3] // tm

    def body(dq_ref, dk_ref, dv_ref, ql_ref, ckv_ref, qg_ref, kvg_ref, wq_ref, wk_ref, wv_ref, pos_ref, invf_ref,
             dql_ref, dckv_ref, dkr_ref, dwq_ref, dwk_ref, dwv_ref, dqg_ref, dkvg_ref):
        i = pl.program_id(0)

        @pl.when(i == 0)
        def _():
            dwq_ref[...] = jnp.zeros_like(dwq_ref)
            dwk_ref[...] = jnp.zeros_like(dwk_ref)
            dwv_ref[...] = jnp.zeros_like(dwv_ref)
            dqg_ref[...] = jnp.zeros_like(dqg_ref)
            dkvg_ref[...] = jnp.zeros_like(dkvg_ref)

        ang = pos_ref[...].astype(F32) * invf_ref[...]
        cs = jnp.cos(ang)
        sn = jnp.sin(ang)

        def rms_bwd(v, g, dn, dg_ref):
            r = lax.rsqrt(jnp.mean(v * v, axis=-1, keepdims=True) + RMS_EPS)
            vh = v * r
            dg_ref[...] += jnp.sum(dn * vh, axis=0, keepdims=True)
            dvh = dn * g
            return vh, r * (dvh - vh * jnp.mean(dvh * vh, axis=-1, keepdims=True))

        pieces = []
        for h in range(NH):
            dqh = dq_ref[h].T
            pieces.append(dqh[:, 0:NOPE] * SCALE)
            pieces.append(_rope_t(dqh[:, NOPE:HP], cs, sn) * SCALE)
        dqf = jnp.concatenate(pieces, axis=1).astype(BF16)
        ql = ql_ref[...]
        g = qg_ref[...]
        dqn = _nt(dqf, wq_ref[...])
        qh, dql = rms_bwd(ql, g, dqn, dqg_ref)
        dwq_ref[...] += _tn((qh * g).astype(BF16), dqf)
        dql_ref[...] = dql.astype(BF16)

        dkn_p = jnp.concatenate([dk_ref[h, :, 0:NOPE] for h in range(NH)], axis=1).astype(BF16)
        dvf = jnp.concatenate([dv_ref[h] for h in range(NH)], axis=1).astype(BF16)
        dkr = dk_ref[0, :, NOPE:HP]
        for h in range(1, NH):
            dkr = dkr + dk_ref[h, :, NOPE:HP]
        lane = lax.broadcasted_iota(jnp.int32, dkr.shape, 1)
        dkr_ref[...] = jnp.where(lane < ROPE, _rope_t(dkr, cs, sn), 0.0).astype(BF16)
        cv = ckv_ref[...]
        gk = kvg_ref[...]
        dkn = _nt(dkn_p, wk_ref[...]) + _nt(dvf, wv_ref[...])
        kh, dckv = rms_bwd(cv, gk, dkn, dkvg_ref)
        knb = (kh * gk).astype(BF16)
        dwk_ref[...] += _tn(knb, dkn_p)
        dwv_ref[...] += _tn(knb, dvf)
        dckv_ref[...] = dckv.astype(BF16)

    row = lambda n: pl.BlockSpec((tm, n), lambda i: (i, 0))
    heads = lambda n: pl.BlockSpec((NH, tm, n), lambda i: (0, i, 0))
    return pl.pallas_call(
        body, name="mla_bwd", grid=(S // tm,),
        in_specs=[pl.BlockSpec((NH, None, HP, tm), lambda i: (0, i // per, 0, i % per)), heads(HP), heads(VD), row(QL), row(KVL), _const((1, QL)), _const((1, KVL)),
                  _const((QL, NH * HP)), _const((KVL, NH * NOPE)), _const((KVL, NH * VD)), row(1), _const((1, 128))],
        out_specs=[row(QL), row(KVL), row(128), _full((QL, NH * HP)), _full((KVL, NH * NOPE)), _full((KVL, NH * VD)),
                   _full((1, QL)), _full((1, KVL))],
        out_shape=[jax.ShapeDtypeStruct((S, QL), BF16), jax.ShapeDtypeStruct((S, KVL), BF16), jax.ShapeDtypeStruct((S, 128), BF16),
                   jax.ShapeDtypeStruct((QL, NH * HP), F32), jax.ShapeDtypeStruct((KVL, NH * NOPE), F32),
                   jax.ShapeDtypeStruct((KVL, NH * VD), F32), jax.ShapeDtypeStruct((1, QL), F32), jax.ShapeDtypeStruct((1, KVL), F32)],
        compiler_params=_cp(("arbitrary",)),
    )(dqt, dk, dv, qlat, ckv, qg, kvg, wq, wk, wv, pos, invf)


def _inproj_bwd(x, gx1, mod, win, dql, dckv, dza, dxbc, dzs, dkr, ddt):
    S = x.shape[0]
    tm = min(TM, S)

    def body(x_ref, gx1_ref, mod_ref, win_ref, dql_ref, dckv_ref, dza_ref, dxbc_ref, dzs_ref, dkr_ref, ddt_ref,
             gx_ref, dw_ref, vec_ref):
        i = pl.program_id(0)

        @pl.when(i == 0)
        def _():
            dw_ref[...] = jnp.zeros_like(dw_ref)
            vec_ref[...] = jnp.zeros_like(vec_ref)

        shift = mod_ref[0:1, 0:D]
        scale = mod_ref[0:1, D:2 * D]
        xv = x_ref[...]
        ut = (xv * (1.0 + scale) + shift).T.astype(BF16)
        pieces = (dql_ref, dckv_ref, dza_ref, dxbc_ref, dzs_ref, dkr_ref, ddt_ref)
        du = jnp.zeros((tm, D), F32)
        lo = 0
        for p_ref in pieces:
            n = p_ref.shape[1]
            dp = p_ref[...]
            du = du + _nt(dp, win_ref[:, lo:lo + n])
            dw_ref[:, lo:lo + n] += _mm(ut, dp)
            lo += n
        vec_ref[0:1, :] += jnp.sum(du, axis=0, keepdims=True)
        vec_ref[1:2, :] += jnp.sum(du * xv, axis=0, keepdims=True)
        gx_ref[...] = gx1_ref[...] + du * (1.0 + scale)

    row = lambda n: pl.BlockSpec((tm, n), lambda i: (i, 0))
    return pl.pallas_call(
        body, name="inproj_bwd", grid=(S // tm,),
        in_specs=[row(D), row(D), _const((8, 3 * D)), _const((D, IN_P)), row(QL), row(KVL), row(D), row(CC), row(D),
                  row(128), row(128)],
        out_specs=[row(D), pl.BlockSpec((D, IN_P), lambda i: (0, 0), pipeline_mode=pl.Buffered(1)), _full((8, D))],
        out_shape=[jax.ShapeDtypeStruct((S, D), F32), jax.ShapeDtypeStruct((D, IN_P), F32), jax.ShapeDtypeStruct((8, D), F32)],
        compiler_params=_cp(("arbitrary",)),
    )(x, gx1, mod, win, dql, dckv, dza, dxbc, dzs, dkr, ddt)


def _ada_bwd(callt, dmods):
    w = dmods.shape[1]

    def body(c_ref, d_ref, o_ref):
        acc = c_ref[:, 0:1] * d_ref[0:1, :]
        for s in range(1, 8):
            acc = acc + c_ref[:, s:s + 1] * d_ref[s:s + 1, :]
        o_ref[0] = acc

    return pl.pallas_call(body, name="ada_bwd", out_shape=jax.ShapeDtypeStruct((1, D, w), F32),
                          compiler_params=_cp())(callt, dmods)


def _adamw(name, parts, w, m, v):
    rows, ncol = w.shape
    tr = min(rows, 128)
    nparts = parts.shape[0]

    def body(p_ref, w_ref, m_ref, v_ref, g_ref, d_ref, nm_ref, nv_ref):
        g = p_ref[0].astype(F32)
        for s in range(1, nparts):
            g = g + p_ref[s].astype(F32)
        g_ref[...] = g
        nm = B1 * m_ref[...] + (1.0 - B1) * g
        nv = B2 * v_ref[...] + (1.0 - B2) * (g * g)
        nm_ref[...] = nm
        nv_ref[...] = nv
        m_hat = nm / (1.0 - B1 ** STEP)
        v_hat = nv / (1.0 - B2 ** STEP)
        d_ref[...] = -LR * (m_hat / (jnp.sqrt(v_hat) + EPS) + WD * w_ref[...])

    row = pl.BlockSpec((tr, ncol), lambda i: (i, 0))
    sd = jax.ShapeDtypeStruct((rows, ncol), F32)
    return pl.pallas_call(
        body, name="adamw_" + name, grid=(rows // tr,),
        in_specs=[pl.BlockSpec((nparts, tr, ncol), lambda i: (0, i, 0)), row, row, row],
        out_specs=[row, row, row, row], out_shape=[sd, sd, sd, sd],
        compiler_params=_cp(("arbitrary",)),
    )(parts, w, m, v)


_SMALL = (("b_ada", 3 * D), ("conv_w", CW * CC // 4), ("conv_b", CC), ("ssm_norm_g", SW), ("ln_g", D), ("ln_b", D),
          ("q_norm_g", QL), ("kv_norm_g", KVL), ("dt_bias", SH), ("a_log", SH), ("d_skip", SH), ("loss", 128))


def _pack_small(d, lead):
    flat = [d[name].reshape(d[name].shape[:lead] + (-1,)) for name, _ in _SMALL]
    used = sum(f.shape[lead] for f in flat)
    pad = jnp.zeros(flat[0].shape[:lead] + (R_SMALL * 1024 - used,), F32)
    return jnp.concatenate(flat + [pad], axis=lead).reshape(flat[0].shape[:lead] + (R_SMALL, 1024))


def _unpack_small(p):
    flat = p.reshape(-1)
    out, r = {}, 0
    for name, n in _SMALL:
        out[name] = flat[r:r + n]
        r += n
    return out


def _in_to_padded(w):
    z = lambda n: jnp.zeros((w.shape[0], n), w.dtype)
    return jnp.concatenate([w[:, 0:384], w[:, 384:640], w[:, 704:1728], w[:, 1728:3264], w[:, 3280:4304],
                            w[:, 640:704], z(64), w[:, 3264:3280], z(112)], axis=1)


def _in_from_padded(g):
    return jnp.concatenate([g[:, 0:384], g[:, 384:640], g[:, P_KR[0]:P_KR[0] + 64], g[:, 640:1664], g[:, 1664:3200],
                            g[:, P_DT[0]:P_DT[0] + 16], g[:, 3200:4224]], axis=1)


def kernel(x, c, positions, w_ada, b_ada, w_in, q_norm_g, w_qb, kv_norm_g, w_kvb, conv_w, conv_b, dt_bias, a_log, d_skip, ssm_norm_g, w_out, ln_g, ln_b, loss_target, m_w_ada, m_b_ada, m_w_in, m_q_norm_g, m_w_qb, m_kv_norm_g, m_w_kvb, m_conv_w, m_conv_b, m_dt_bias, m_a_log, m_d_skip, m_ssm_norm_g, m_w_out, m_ln_g, m_ln_b, v_w_ada, v_b_ada, v_w_in, v_q_norm_g, v_w_qb, v_kv_norm_g, v_w_kvb, v_conv_w, v_conv_b, v_dt_bias, v_a_log, v_d_skip, v_ssm_norm_g, v_w_out, v_ln_g, v_ln_b):
    S = x.shape[1]
    xv = x[0]
    tgt = loss_target[0]

    cw16 = jnp.concatenate([conv_w[0], jnp.zeros((16 - CW, CC // 4), F32)], axis=0)
    f_in, f_qb, f_kvb, f_out, f_cw = _gather_weights(
        [w_in[0].astype(BF16), w_qb[0].astype(BF16), w_kvb[0].astype(BF16), w_out[0].astype(BF16), cw16])
    cat1 = lambda f: f.transpose(1, 0, 2).reshape(f.shape[1], 4 * f.shape[2])
    win = _in_to_padded(cat1(f_in))
    wqb = cat1(f_qb).reshape(QL, NH, QKD)
    wq = jnp.concatenate([wqb, jnp.zeros((QL, NH, HP - QKD), BF16)], axis=2).reshape(QL, NH * HP)
    wkvb = cat1(f_kvb).reshape(KVL, NH, NOPE + VD)
    wk = wkvb[:, :, 0:NOPE].reshape(KVL, NH * NOPE)
    wv = wkvb[:, :, NOPE:].reshape(KVL, NH * VD)
    wout = f_out.reshape(MIX, D)
    cwf = cat1(f_cw[:, 0:CW, :])

    half = ROPE // 2
    invf = 1.0 / (ROPE_THETA ** (jnp.arange(half, dtype=F32) / half))
    invf = jnp.concatenate([invf, invf, jnp.zeros((128 - ROPE,), F32)]).reshape(1, 128)
    pos = positions.reshape(S, 1)
    pad128 = lambda a: jnp.concatenate([a.reshape(1, SH), jnp.zeros((1, 128 - SH), F32)], axis=1)
    dtb, alog = pad128(dt_bias), pad128(a_log)
    dskx = jnp.repeat(d_skip.reshape(SH), SP).reshape(1, SW)

    my_c = lax.axis_index("c")
    (call,) = _exchange("gather_c", [jnp.broadcast_to(c.reshape(1, 1, D), (4, 1, D))])
    call = call.reshape(8, D)
    mods = _ada(call, w_ada[0])
    (mrows,) = _exchange("scatter_mod", [mods.reshape(4, 2, 3 * D // 4)])
    mine = lax.dynamic_index_in_dim(mrows.reshape(4, 2, 2, 3 * D // 4)[:, 0], my_c, axis=1, keepdims=False)
    mod = jnp.broadcast_to(mine.reshape(1, 3 * D) + b_ada, (8, 3 * D))
    qlat, ckv, za, xbc, zs, dtraw, q, k, v, kt, vt = _inproj(xv, mod, win, q_norm_g, kv_norm_g, wq, wk, wv, wk.T, wv.T, pos, invf)
    o, lse, mblk, pst = _attn_fwd(q, k, vt)
    y, htp, ossm = _ssd_fwd(xbc, dtraw, zs, cwf, conv_b, dtb, alog, dskx, ssm_norm_g)
    gx1, do, dza, dossm, delta, dwout, vec_o = _outproj(o, za, ossm, xv, tgt, wout, mod, ln_g, ln_b)
    loss_part = jnp.zeros((128,), F32).at[0].set(0.5 / D * jnp.sum(vec_o[0]))

    dk, dv, dq = _attn_bwd(q, k, kt, v, do, lse, delta, mblk, pst)
    dxbc, ddt, dzs, dcw, dcb, dvec, dgssm = _ssd_bwd(xbc, dtraw, zs, y, htp, dossm, cwf, conv_b, dtb, alog, dskx, ssm_norm_g)
    dql, dckv, dkr, dwq, dwk, dwv, dqg, dkvg = _mla_bwd(dq, dk, dv, qlat, ckv, q_norm_g, kv_norm_g, wq, wk, wv, pos, invf)
    gx, dwin, vec_i = _inproj_bwd(xv, gx1, mod, win, dql, dckv, dza, dxbc, dzs, dkr, ddt)
    dmod = jnp.concatenate([vec_i[0:1], vec_i[1:2], vec_o[3:4]], axis=1)

    cols = lambda g: g.reshape(g.shape[0], 4, g.shape[1] // 4).transpose(1, 0, 2)
    g_in = cols(_in_from_padded(dwin)).astype(BF16)
    g_qb = cols(dwq.reshape(QL, NH, HP)[:, :, 0:QKD].reshape(QL, NH * QKD)).astype(BF16)
    g_kvb = cols(jnp.concatenate([dwk.reshape(KVL, NH, NOPE), dwv.reshape(KVL, NH, VD)], axis=2)
                 .reshape(KVL, NH * (NOPE + VD))).astype(BF16)
    g_out = dwout.reshape(4, MIX // 4, D).astype(BF16)
    small = {"b_ada": dmod, "conv_b": dcb, "ssm_norm_g": dgssm, "ln_g": vec_o[1:2], "ln_b": vec_o[2:3],
             "q_norm_g": dqg, "kv_norm_g": dkvg, "dt_bias": dvec[0:1, 0:SH], "a_log": dvec[1:2, 0:SH], "d_skip": dvec[2:3, 0:SH],
             "loss": loss_part}
    small = {n: jnp.broadcast_to(a.reshape(1, -1), (4, a.size)) for n, a in small.items()}
    small["conv_w"] = cols(dcw).reshape(4, CW * CC // 4)
    gsmall = _pack_small(small, 1)

    r_in, r_qb, r_kvb, r_out, rs, dmods = _exchange(
        "exchange_grads", [g_in, g_qb, g_kvb, g_out, gsmall, jnp.broadcast_to(dmod.reshape(1, 1, 3 * D), (4, 1, 3 * D))])
    chip = 2 * lax.axis_index("x") + lax.axis_index("y")
    dmods = lax.dynamic_slice_in_dim(dmods.reshape(8, 3 * D), chip * (3 * D // 4), 3 * D // 4, axis=1)
    g_ada = _ada_bwd(call.T, dmods)
    res = dict(w_ada=_adamw("w_ada", g_ada, w_ada[0], m_w_ada[0], v_w_ada[0]),
               w_in=_adamw("w_in", r_in, w_in[0], m_w_in[0], v_w_in[0]),
               w_qb=_adamw("w_qb", r_qb, w_qb[0], m_w_qb[0], v_w_qb[0]),
               w_kvb=_adamw("w_kvb", r_kvb, w_kvb[0], m_w_kvb[0], v_w_kvb[0]),
               w_out=_adamw("w_out", r_out, w_out[0], m_w_out[0], v_w_out[0]))
    wsm = _pack_small(dict(b_ada=b_ada, conv_w=conv_w, conv_b=conv_b, ssm_norm_g=ssm_norm_g, ln_g=ln_g, ln_b=ln_b,
                           q_norm_g=q_norm_g, kv_norm_g=kv_norm_g, dt_bias=dt_bias, a_log=a_log, d_skip=d_skip, loss=jnp.zeros((128,), F32)), 0)
    msm = _pack_small(dict(b_ada=m_b_ada, conv_w=m_conv_w, conv_b=m_conv_b, ssm_norm_g=m_ssm_norm_g, ln_g=m_ln_g, ln_b=m_ln_b,
                           q_norm_g=m_q_norm_g, kv_norm_g=m_kv_norm_g, dt_bias=m_dt_bias, a_log=m_a_log, d_skip=m_d_skip, loss=jnp.zeros((128,), F32)), 0)
    vsm = _pack_small(dict(b_ada=v_b_ada, conv_w=v_conv_w, conv_b=v_conv_b, ssm_norm_g=v_ssm_norm_g, ln_g=v_ln_g, ln_b=v_ln_b,
                           q_norm_g=v_q_norm_g, kv_norm_g=v_kv_norm_g, dt_bias=v_dt_bias, a_log=v_a_log, d_skip=v_d_skip, loss=jnp.zeros((128,), F32)), 0)
    sm = _adamw("small", rs, wsm, msm, vsm)

    order = ["w_ada", "b_ada", "w_in", "q_norm_g", "w_qb", "kv_norm_g", "w_kvb", "conv_w", "conv_b", "dt_bias", "a_log",
             "d_skip", "ssm_norm_g", "w_out", "ln_g", "ln_b"]
    shapes = dict(w_ada=w_ada.shape, b_ada=b_ada.shape, w_in=w_in.shape, q_norm_g=q_norm_g.shape, w_qb=w_qb.shape,
                  kv_norm_g=kv_norm_g.shape, w_kvb=w_kvb.shape, conv_w=conv_w.shape, conv_b=conv_b.shape, dt_bias=dt_bias.shape,
                  a_log=a_log.shape, d_skip=d_skip.shape, ssm_norm_g=ssm_norm_g.shape, w_out=w_out.shape, ln_g=ln_g.shape,
                  ln_b=ln_b.shape)
    outs = []
    for kind in range(4):
        d = _unpack_small(sm[kind])
        d.update({n: r[kind] for n, r in res.items()})
        outs.extend(d[n].reshape(shapes[n]) for n in order)
    loss = _unpack_small(sm[0])["loss"][0]
    return (loss, gx.reshape(x.shape), *outs)
```

```python
import functools
import math

import numpy as np
import jax
import jax.numpy as jnp
from jax import lax
from jax.experimental import pallas as pl
from jax.experimental.pallas import tpu as pltpu

F32 = jnp.float32
BF16 = jnp.bfloat16
MESH_ID = pl.DeviceIdType.MESH

D = 1024
NH = 8
NOPE = 128
ROPE = 64
VD = 128
VDP = 144
QKD = NOPE + ROPE
HP = 256
QL = 384
KVL = 256
ROPE_THETA = 10000.0
SH = 16
SP = 64
SG = 2
SN = 128
CW = 4
CH = 128
SW = SH * SP
CC = SW + 2 * SG * SN
GW = SW // SG
MIX = 2 * D
IN_W = 4304
ALPHA = 2.0 ** 0.25
RMS_EPS = 1e-6
LN_EPS = 1e-5
SCALE = QKD ** -0.5
LN2 = math.log(2.0)
QSCALE = SCALE / LN2
LR, B1, B2, EPS, WD, STEP = 0.001, 0.9, 0.999, 1e-08, 0.01, 10

P_Q = (0, 384)
P_KV = (384, 640)
P_ZA = (640, 1664)
P_XBC = (1664, 3200)
P_ZS = (3200, 4224)
P_KR = (4224, 4352)
P_DT = (4352, 4480)
IN_P = 4480

R_SMALL = 16

TM = 256
TQ = 512
TQF = 1024
VMEM_LIMIT = 56 * 1024 * 1024


def _cp(sem=None):
    return pltpu.CompilerParams(dimension_semantics=sem, vmem_limit_bytes=VMEM_LIMIT)


def _mm(a, b):
    return jnp.dot(a, b, preferred_element_type=F32)


def _nt(a, b):
    return lax.dot_general(a, b, (((1,), (1,)), ((), ())), preferred_element_type=F32)


def _tn(a, b):
    return lax.dot_general(a, b, (((0,), (0,)), ((), ())), preferred_element_type=F32)


def _split(a, terms):
    parts = []
    for t in range(terms):
        p = a.astype(BF16)
        parts.append(p)
        if t + 1 < terms:
            a = a - p.astype(F32)
    return parts


def _mm_x(a, ones, terms):
    parts = _split(a, terms)
    out = _mm(parts[0], ones)
    for p in parts[1:]:
        out = out + _mm(p, ones)
    return out


def _xmm(ones, a, terms):
    parts = _split(a, terms)
    out = _mm(ones, parts[0])
    for p in parts[1:]:
        out = out + _mm(ones, p)
    return out


def _nt_x(a, ones, terms):
    parts = _split(a, terms)
    out = _nt(parts[0], ones)
    for p in parts[1:]:
        out = out + _nt(p, ones)
    return out


def _sigmoid(z):
    return 1.0 / (1.0 + jnp.exp(-z))


def _softplus(z):
    return jnp.maximum(z, 0.0) + jnp.log1p(jnp.exp(-jnp.abs(z)))


def _rope(t, cs, sn):
    lane = lax.broadcasted_iota(jnp.int32, t.shape, 1)
    rot = jnp.where(lane < ROPE // 2, -pltpu.roll(t, 128 - ROPE // 2, 1), pltpu.roll(t, ROPE // 2, 1))
    return t * cs + rot * sn


def _rope_t(t, cs, sn):
    lane = lax.broadcasted_iota(jnp.int32, t.shape, 1)
    y = t * sn
    rot = jnp.where(lane < ROPE // 2, -pltpu.roll(y, 128 - ROPE // 2, 1), pltpu.roll(y, ROPE // 2, 1))
    return t * cs - rot


def _full(shape):
    n = len(shape)
    return pl.BlockSpec(shape, lambda *_: (0,) * n)


def _const(shape):
    n = len(shape)
    return pl.BlockSpec(shape, lambda *_: (0,) * n, pipeline_mode=pl.Buffered(1))


def _gather_weights(shards):
    n = len(shards)
    halves = [a.shape[0] // 2 for a in shards]

    def body(*refs):
        srcs, dsts = refs[:n], refs[n:2 * n]
        send_sems, recv_sems, local_sems = refs[2 * n:]
        x, y, c = lax.axis_index("x"), lax.axis_index("y"), lax.axis_index("c")
        me = 2 * x + y
        sibling = (x, y, 1 - c)
        chips = [(1 - x, y), (x, 1 - y), (1 - x, 1 - y)]

        def rows(a, pc):
            return pl.ds(pl.multiple_of(pc * halves[a], halves[a]), halves[a])

        def copy(a, k, src, slot, pc, to):
            return pltpu.make_async_remote_copy(
                src_ref=src, dst_ref=dsts[a].at[slot, rows(a, pc)], send_sem=send_sems.at[a, k],
                recv_sem=recv_sems.at[a, k], device_id=to, device_id_type=MESH_ID)

        local = [pltpu.make_async_copy(srcs[a], dsts[a].at[me], local_sems.at[a]) for a in range(n)]
        for cp in local:
            cp.start()
        sends = [copy(a, j, srcs[a].at[rows(a, c)], me, c, (px, py, c)) for a in range(n) for j, (px, py) in enumerate(chips)]
        for cp in sends:
            cp.start()
        passed = []
        for a in range(n):
            for j, (px, py) in enumerate(chips):
                k = 2 * px + py
                copy(a, j, srcs[a].at[rows(a, c)], k, c, (x, y, c)).wait_recv()
                fwd = copy(a, 3 + j, dsts[a].at[k, rows(a, c)], k, c, sibling)
                fwd.start()
                passed.append(fwd)
        for a in range(n):
            for j, (px, py) in enumerate(chips):
                copy(a, 3 + j, srcs[a].at[rows(a, c)], 2 * px + py, 1 - c, (x, y, c)).wait_recv()
        for cp in sends + passed:
            cp.wait_send()
        for cp in local:
            cp.wait()

    hbm = pl.BlockSpec(memory_space=pltpu.HBM)
    return pl.pallas_call(
        body, name="gather_weights",
        out_shape=tuple(jax.ShapeDtypeStruct((4,) + a.shape, a.dtype) for a in shards),
        in_specs=[hbm] * n, out_specs=tuple([hbm] * n),
        scratch_shapes=[pltpu.SemaphoreType.DMA((n, 6)), pltpu.SemaphoreType.DMA((n, 6)), pltpu.SemaphoreType.DMA((n,))],
    )(*shards)


def _exchange(name, slabs):
    n = len(slabs)

    def body(*refs):
        srcs, dsts = refs[:n], refs[n:2 * n]
        send_sems, recv_sems, local_sems = refs[2 * n:]
        x, y, c = lax.axis_index("x"), lax.axis_index("y"), lax.axis_index("c")
        chip = 2 * x + y
        sibling = (x, y, 1 - c)
        chips = [(1 - x, y), (x, 1 - y), (1 - x, 1 - y)]

        def slot(px, py, pc):
            return 4 * px + 2 * py + pc

        def copy(a, k, src, s, to):
            return pltpu.make_async_remote_copy(
                src_ref=src, dst_ref=dsts[a].at[s], send_sem=send_sems.at[a, k], recv_sem=recv_sems.at[a, k],
                device_id=to, device_id_type=MESH_ID)

        mine = slot(x, y, c)
        local = [pltpu.make_async_copy(srcs[a].at[chip], dsts[a].at[mine], local_sems.at[a]) for a in range(n)]
        for cp in local:
            cp.start()
        first = []
        for a in range(n):
            first.append(copy(a, 0, srcs[a].at[chip], mine, sibling))
            for j, (px, py) in enumerate(chips):
                first.append(copy(a, 1 + j, srcs[a].at[2 * px + py], mine, (px, py, c)))
        for cp in first:
            cp.start()
        passed = []
        for a in range(n):
            for j, (px, py) in enumerate(chips):
                s = slot(px, py, c)
                copy(a, 1 + j, srcs[a].at[chip], s, (x, y, c)).wait_recv()
                fwd = copy(a, 4 + j, dsts[a].at[s], s, sibling)
                fwd.start()
                passed.append(fwd)
        for a in range(n):
            copy(a, 0, srcs[a].at[chip], slot(x, y, 1 - c), (x, y, c)).wait_recv()
            for j, (px, py) in enumerate(chips):
                copy(a, 4 + j, srcs[a].at[chip], slot(px, py, 1 - c), (x, y, c)).wait_recv()
        for cp in first + passed:
            cp.wait_send()
        for cp in local:
            cp.wait()

    hbm = pl.BlockSpec(memory_space=pltpu.HBM)
    return pl.pallas_call(
        body, name=name,
        out_shape=tuple(jax.ShapeDtypeStruct((8,) + a.shape[1:], a.dtype) for a in slabs),
        in_specs=[hbm] * n, out_specs=tuple([hbm] * n),
        scratch_shapes=[pltpu.SemaphoreType.DMA((n, 7)), pltpu.SemaphoreType.DMA((n, 7)), pltpu.SemaphoreType.DMA((n,))],
    )(*slabs)


def _ada(call, w_shard):
    def body(c_ref, w_ref, o_ref):
        o_ref[...] = _mm(c_ref[...].astype(BF16), w_ref[...].astype(BF16))

    return pl.pallas_call(body, name="ada", out_shape=jax.ShapeDtypeStruct((8, w_shard.shape[1]), F32),
                          compiler_params=_cp())(call, w_shard)


def _inproj(x, mod, win, qg, kvg, wq, wk, wv, wkt, wvt, pos, invf):
    S = x.shape[0]
    tm = min(TM, S)

    def body(x_ref, mod_ref, win_ref, qg_ref, kvg_ref, wq_ref, wk_ref, wv_ref, wkt_ref, wvt_ref, pos_ref, invf_ref,
             qlat_ref, ckv_ref, za_ref, xbc_ref, zs_ref, dt_ref, q_ref, k_ref, v_ref, kt_ref, vt_ref):
        shift = mod_ref[0:1, 0:D]
        scale = mod_ref[0:1, D:2 * D]
        u = (x_ref[...] * (1.0 + scale) + shift).astype(BF16)

        def proj(p):
            return _mm(u, win_ref[:, p[0]:p[1]])

        ql = proj(P_Q)
        ckv = proj(P_KV)
        qlat_ref[...] = ql
        ckv_ref[...] = ckv
        za_ref[...] = proj(P_ZA)
        xbc_ref[...] = proj(P_XBC)
        zs_ref[...] = proj(P_ZS)
        dt_ref[...] = proj(P_DT)
        kr = proj(P_KR)

        ang = pos_ref[...].astype(F32) * invf_ref[...]
        cs = jnp.cos(ang)
        sn = jnp.sin(ang)

        rq = lax.rsqrt(jnp.mean(ql * ql, axis=-1, keepdims=True) + RMS_EPS)
        qn = (ql * rq * qg_ref[...]).astype(BF16)
        for h in range(NH):
            qh = _mm(qn, wq_ref[:, h * HP:(h + 1) * HP])
            q_ref[h, :, 0:NOPE] = (qh[:, 0:NOPE] * QSCALE).astype(BF16)
            q_ref[h, :, NOPE:HP] = (_rope(qh[:, NOPE:HP], cs, sn) * QSCALE).astype(BF16)

        rk = lax.rsqrt(jnp.mean(ckv * ckv, axis=-1, keepdims=True) + RMS_EPS)
        kn = (ckv * rk * kvg_ref[...]).astype(BF16)
        knope = _mm(kn, wk_ref[...])
        vall = _mm(kn, wv_ref[...])
        krf = _rope(kr, cs, sn)
        krr = krf.astype(BF16)
        krt = krf.T.astype(BF16)
        ones_rows = jnp.where(lax.broadcasted_iota(jnp.int32, (VDP - VD, tm), 0) == 0, 1.0, 0.0).astype(BF16)
        for h in range(NH):
            k_ref[h, :, 0:NOPE] = knope[:, h * NOPE:(h + 1) * NOPE].astype(BF16)
            k_ref[h, :, NOPE:HP] = krr
            v_ref[h] = vall[:, h * VD:(h + 1) * VD].astype(BF16)
            kt_ref[h, 0:NOPE, :] = _nt(wkt_ref[h * NOPE:(h + 1) * NOPE, :], kn).astype(BF16)
            kt_ref[h, NOPE:HP, :] = krt
            vt_ref[h, 0:VD, :] = _nt(wvt_ref[h * VD:(h + 1) * VD, :], kn).astype(BF16)
            vt_ref[h, VD:VDP, :] = ones_rows

    row = lambda n: pl.BlockSpec((tm, n), lambda i: (i, 0))
    heads = lambda n: pl.BlockSpec((NH, tm, n), lambda i: (0, i, 0))
    heads_t = lambda n: pl.BlockSpec((NH, None, n, tm), lambda i: (0, i, 0, 0))
    sd = lambda n: jax.ShapeDtypeStruct((S, n), F32)
    hd = lambda n: jax.ShapeDtypeStruct((NH, S, n), BF16)
    ht = lambda n: jax.ShapeDtypeStruct((NH, S // tm, n, tm), BF16)
    return pl.pallas_call(
        body, name="inproj", grid=(S // tm,),
        in_specs=[row(D), _const((8, 3 * D)), _const((D, IN_P)), _const((1, QL)), _const((1, KVL)),
                  _const((QL, NH * HP)), _const((KVL, NH * NOPE)), _const((KVL, NH * VD)),
                  _const((NH * NOPE, KVL)), _const((NH * VD, KVL)), row(1), _const((1, 128))],
        out_specs=[row(QL), row(KVL), row(D), row(CC), row(D), row(128), heads(HP), heads(HP), heads(VD),
                   heads_t(HP), heads_t(VDP)],
        out_shape=[sd(QL), sd(KVL), sd(D), sd(CC), sd(D), sd(128), hd(HP), hd(HP), hd(VD), ht(HP), ht(VDP)],
        compiler_params=_cp(("arbitrary",)),
    )(x, mod, win, qg, kvg, wq, wk, wv, wkt, wvt, pos, invf)


def _attn_fwd(q, k, vt):
    _, S, _ = q.shape
    tq = min(TQF, S)
    nq = S // tq
    half = tq // 2
    tb = vt.shape[3]
    nsb = half // tb

    def body(q_ref, k_ref, vt_ref, o_ref, lse_ref):
        i = pl.program_id(1)
        qb = q_ref[...]

        def scores(j, hb):
            off = pl.multiple_of(j * tq + hb * half, half)
            return _nt(k_ref[pl.ds(off, half), :], qb)

        def update(j, hb, s, carry):
            m, acc = carry
            m_new = jnp.maximum(m, jnp.max(s, axis=0, keepdims=True))
            a = jnp.exp2(m - m_new)
            pb = jnp.exp2(s - m_new).astype(BF16)
            acc = a * acc
            for sb in range(nsb):
                acc = acc + _mm(vt_ref[(2 * j + hb) * nsb + sb], pb[sb * tb:(sb + 1) * tb, :])
            return m_new, acc

        def trip(j, carry, masked):
            s = [scores(j, hb) for hb in range(2)]
            if masked:
                r = lax.broadcasted_iota(jnp.int32, s[0].shape, 0)
                cidx = lax.broadcasted_iota(jnp.int32, s[0].shape, 1)
                s = [jnp.where(cidx >= r + hb * half, s[hb], -1e30) for hb in range(2)]
            for hb in range(2):
                carry = update(j, hb, s[hb], carry)
            return carry

        def finish(carry):
            m, acc = carry
            l = acc[VD:VD + 1, :]
            o_ref[...] = (acc[0:VD, :] / l).T
            lse_ref[...] = m + jnp.log2(l)

        init = (jnp.full((1, tq), -1e30, F32), jnp.zeros((VDP, tq), F32))
        carry = lax.fori_loop(0, i >> 1, lambda t, cr: trip(2 * t + 1, trip(2 * t, cr, False), False), init)

        @pl.when((i & 1) == 0)
        def _():
            finish(trip(i, carry, True))

        @pl.when((i & 1) == 1)
        def _():
            finish(trip(i, trip(i - 1, carry, False), True))

    return pl.pallas_call(
        body, name="attn_fwd", grid=(NH, nq),
        in_specs=[pl.BlockSpec((None, tq, HP), lambda h, i: (h, i, 0)),
                  pl.BlockSpec((None, S, HP), lambda h, i: (h, 0, 0)),
                  pl.BlockSpec((None, S // tb, VDP, tb), lambda h, i: (h, 0, 0, 0))],
        out_specs=[pl.BlockSpec((tq, VD), lambda h, i: (i, h)),
                   pl.BlockSpec((None, None, 1, tq), lambda h, i: (h, i, 0, 0))],
        out_shape=[jax.ShapeDtypeStruct((S, NH * VD), F32), jax.ShapeDtypeStruct((NH, nq, 1, tq), F32)],
        compiler_params=_cp(("arbitrary", "arbitrary")),
    )(q, k, vt)


def _ssd_consts():
    tri = np.tril(np.ones((CH, CH), np.float32))
    e16 = np.zeros((128, SW), np.float32)
    for h in range(SH):
        e16[h, h * SP:(h + 1) * SP] = 1.0
    return jnp.asarray(tri, BF16), jnp.asarray(tri.T.copy(), BF16), jnp.asarray(e16, BF16)


def _ssd_chunk_fwd_common(xbc_ref, halo_ref, dtraw_ref, cw_ref, cb_ref, dtb_ref, alog_ref, tri_ref, e16_ref, ext, first):
    ext[0:8, :] = jnp.where(first, 0.0, halo_ref[...])
    ext[8:8 + CH, :] = xbc_ref[...]
    cw = cw_ref[...]
    wins = [ext[5 + kk:5 + kk + CH, :] for kk in range(CW)]
    xc = cb_ref[...] + cw[0:1, :] * wins[0]
    for kk in range(1, CW):
        xc = xc + cw[kk:kk + 1, :] * wins[kk]
    sact = _sigmoid(xc)
    act = xc * sact
    lane = lax.broadcasted_iota(jnp.int32, (1, 128), 1)
    arow = jnp.where(lane < SH, -jnp.exp(alog_ref[...]), 0.0)
    dtpre = dtraw_ref[...] + dtb_ref[...]
    dt = _softplus(dtpre)
    a = dt * arow
    cum = _xmm(tri_ref[...], a, 3)
    cumx = _mm_x(cum, e16_ref[...], 3)
    dtx = _mm_x(dt, e16_ref[...], 2)
    return xc, sact, act, arow, dtpre, dt, cum, cumx, dtx, wins


def _ssd_fwd(xbc, dtraw, zs, conv_w, conv_b, dtb, alog, dskx, gssm):
    S = xbc.shape[0]
    nc = S // CH
    tri, _, e16 = _ssd_consts()

    def body(xbc_ref, halo_ref, dtraw_ref, zs_ref, cw_ref, cb_ref, dtb_ref, alog_ref, dsk_ref, g_ref, tri_ref, e16_ref,
             y_ref, htp_ref, o_ref, ht, ext):
        i = pl.program_id(0)

        @pl.when(i == 0)
        def _():
            ht[...] = jnp.zeros_like(ht)

        xc, sact, act, arow, dtpre, dt, cum, cumx, dtx, wins = _ssd_chunk_fwd_common(
            xbc_ref, halo_ref, dtraw_ref, cw_ref, cb_ref, dtb_ref, alog_ref, tri_ref, e16_ref, ext, i == 0)
        cum_t = cum.T
        xs = act[:, 0:SW]
        lastx = cumx[CH - 1:CH, :]
        xh = xs * dtx
        eexp = jnp.exp(cumx)
        dte = jnp.exp(lastx - cumx)
        cdx = jnp.exp(lastx)
        htp = ht[...]
        htp_ref[...] = htp
        xw = (xh * dte).astype(BF16)
        xb = xh.astype(BF16)
        trim = tri_ref[...].astype(F32) > 0.5
        lane = lax.broadcasted_iota(jnp.int32, (CH, 128), 1)
        parts = []
        for g in range(SG):
            gl = slice(g * GW, (g + 1) * GW)
            bg = act[:, SW + g * SN:SW + (g + 1) * SN].astype(BF16)
            cg = act[:, SW + SG * SN + g * SN:SW + SG * SN + (g + 1) * SN].astype(BF16)
            cbm = _nt(cg, bg)
            yoff = eexp[:, gl] * _mm(cg, htp[:, gl].astype(BF16))
            ht[:, gl] = htp[:, gl] * cdx[:, gl] + _tn(bg, xw[:, gl])
            for pr in range(GW // 128):
                h0 = g * (SH // SG) + 2 * pr
                lo = g * GW + pr * 128
                xp = xb[:, lo:lo + 128]
                res = []
                for hh in (h0, h0 + 1):
                    seg = cum[:, hh:hh + 1] - cum_t[hh:hh + 1, :]
                    mh = jnp.where(trim, cbm * jnp.exp(seg), 0.0).astype(BF16)
                    res.append(_mm(mh, xp))
                parts.append(jnp.where(lane < SP, res[0], res[1]) + yoff[:, pr * 128:(pr + 1) * 128])
        y = jnp.concatenate(parts, axis=1) + xs * dsk_ref[...]
        y_ref[...] = y
        z = zs_ref[...]
        hf = y * (z * _sigmoid(z))
        outs = []
        for g in range(SG):
            hg = hf[:, g * GW:(g + 1) * GW]
            rs = lax.rsqrt(jnp.mean(hg * hg, axis=-1, keepdims=True) + RMS_EPS)
            outs.append(hg * rs)
        o_ref[...] = (jnp.concatenate(outs, axis=1) * g_ref[...]).astype(BF16)

    row = lambda n: pl.BlockSpec((CH, n), lambda i: (i, 0))
    return pl.pallas_call(
        body, name="ssd_fwd", grid=(nc,),
        in_specs=[row(CC), pl.BlockSpec((8, CC), lambda i: (jnp.maximum(i * (CH // 8) - 1, 0), 0)), row(128), row(SW),
                  _const((CW, CC)), _const((1, CC)), _const((1, 128)), _const((1, 128)), _const((1, SW)), _const((1, SW)),
                  _const((CH, CH)), _const((128, SW))],
        out_specs=[row(SW), pl.BlockSpec((None, SN, SW), lambda i: (i, 0, 0)), row(SW)],
        out_shape=[jax.ShapeDtypeStruct((S, SW), F32), jax.ShapeDtypeStruct((nc, SN, SW), F32),
                   jax.ShapeDtypeStruct((S, SW), BF16)],
        scratch_shapes=[pltpu.VMEM((SN, SW), F32), pltpu.VMEM((8 + CH, CC), F32)],
        compiler_params=_cp(("arbitrary",)),
    )(xbc, xbc, dtraw, zs, conv_w, conv_b, dtb, alog, dskx, gssm, tri, e16)


def _outproj(o, za, ossm, x, tgt, wout, mod, ln_g, ln_b):
    S = x.shape[0]
    tm = min(TM, S)

    def body(o_ref, za_ref, os_ref, x_ref, t_ref, w_ref, mod_ref, g_ref, b_ref,
             gx_ref, do_ref, dza_ref, dos_ref, delta_ref, dw_ref, vec_ref):
        i = pl.program_id(0)

        @pl.when(i == 0)
        def _():
            dw_ref[...] = jnp.zeros_like(dw_ref)
            vec_ref[...] = jnp.zeros_like(vec_ref)

        gate = mod_ref[0:1, 2 * D:3 * D]
        ov = o_ref[...]
        z = za_ref[...]
        sz = _sigmoid(z)
        silz = z * sz
        a = (ov * silz).astype(BF16)
        osb = os_ref[...]
        mixed = _mm(a, w_ref[0:D, :]) + _mm(osb, w_ref[D:MIX, :])
        xv = x_ref[...]
        hres = ALPHA * xv + gate * mixed
        mu = jnp.mean(hres, axis=-1, keepdims=True)
        hc = hres - mu
        var = jnp.mean(hc * hc, axis=-1, keepdims=True)
        rstd = lax.rsqrt(var + LN_EPS)
        xhat = hc * rstd
        g = g_ref[...]
        yv = xhat * g + b_ref[...]
        err = yv - t_ref[...]
        dy = err * (1.0 / D)
        vec_ref[0:1, :] += jnp.sum(err * err, axis=0, keepdims=True)
        vec_ref[1:2, :] += jnp.sum(dy * xhat, axis=0, keepdims=True)
        vec_ref[2:3, :] += jnp.sum(dy, axis=0, keepdims=True)
        dxh = dy * g
        dh = rstd * (dxh - jnp.mean(dxh, axis=-1, keepdims=True) - xhat * jnp.mean(dxh * xhat, axis=-1, keepdims=True))
        gx_ref[...] = ALPHA * dh
        vec_ref[3:4, :] += jnp.sum(dh * mixed, axis=0, keepdims=True)
        dmixed = (gate * dh).astype(BF16)
        dw_ref[0:D, :] += _tn(a, dmixed)
        dw_ref[D:MIX, :] += _tn(osb, dmixed)
        da = _nt(dmixed, w_ref[0:D, :])
        dos_ref[...] = _nt(dmixed, w_ref[D:MIX, :])
        dov = da * silz
        do_ref[...] = dov.astype(BF16)
        dza_ref[...] = (da * ov * (sz * (1.0 + z * (1.0 - sz)))).astype(BF16)
        pr = dov * ov
        for h in range(NH):
            delta_ref[h] = jnp.sum(pr[:, h * VD:(h + 1) * VD], axis=-1, keepdims=True)

    row = lambda n: pl.BlockSpec((tm, n), lambda i: (i, 0))
    return pl.pallas_call(
        body, name="outproj", grid=(S // tm,),
        in_specs=[row(D), row(D), row(D), row(D), row(D), _const((MIX, D)), _const((8, 3 * D)), _const((1, D)), _const((1, D))],
        out_specs=[row(D), row(D), row(D), row(D), pl.BlockSpec((NH, tm, 1), lambda i: (0, i, 0)),
                   _full((MIX, D)), _full((8, D))],
        out_shape=[jax.ShapeDtypeStruct((S, D), F32), jax.ShapeDtypeStruct((S, D), BF16), jax.ShapeDtypeStruct((S, D), BF16),
                   jax.ShapeDtypeStruct((S, D), F32), jax.ShapeDtypeStruct((NH, S, 1), F32),
                   jax.ShapeDtypeStruct((MIX, D), F32), jax.ShapeDtypeStruct((8, D), F32)],
        compiler_params=_cp(("arbitrary",)),
    )(o, za, ossm, x, tgt, wout, mod, ln_g, ln_b)


def _attn_bwd(q, k, kt, v, do, lse, delta):
    _, S, _ = q.shape
    tk = min(TQ, S // 2)
    nk = S // tk
    tq = 2 * tk
    nq = S // tq
    tb = kt.shape[3]
    nsb = tk // tb

    def body(k_ref, kt_ref, v_ref, q_ref, do_ref, lse_ref, dl_ref, dk_ref, dv_ref, dqt_ref):
        j = pl.program_id(1)
        kb = k_ref[...]
        vb = v_ref[...]

        @pl.when(j == 0)
        def _():
            dqt_ref[...] = jnp.zeros_like(dqt_ref)

        dk_ref[...] = jnp.zeros_like(dk_ref)
        dv_ref[...] = jnp.zeros_like(dv_ref)

        def step(i, masked, lo=0):
            off = pl.multiple_of(i * tq + lo, tk)
            qb = q_ref[pl.ds(off, tq - lo), :]
            dob = do_ref[pl.ds(off, tq - lo), :]
            pt = jnp.exp2(_nt(kb, qb) - lse_ref[i][:, lo:tq])
            if masked:
                r = lax.broadcasted_iota(jnp.int32, pt.shape, 0)
                cidx = lax.broadcasted_iota(jnp.int32, pt.shape, 1)
                pt = jnp.where(i * tq + lo + cidx >= j * tk + r, pt, 0.0)
            dv_ref[...] += _mm(pt.astype(BF16), dob)
            dsb = (pt * (_nt(vb, dob) - dl_ref[i][:, lo:tq])).astype(BF16)
            dk_ref[...] += _mm(dsb, qb)
            acc = dqt_ref[i, :, lo:tq]
            for sb in range(nsb):
                acc = acc + _mm(kt_ref[sb], dsb[sb * tb:(sb + 1) * tb, :])
            dqt_ref[i, :, lo:tq] = acc

        first = j >> 1

        @pl.when((j & 1) == 0)
        def _():
            step(first, True)

        @pl.when((j & 1) == 1)
        def _():
            step(first, True, tk)

        def loop_body(i, carry):
            step(i, False)
            return carry

        lax.fori_loop(first + 1, nq, loop_body, 0)
        dk_ref[...] = dk_ref[...] * LN2

    return pl.pallas_call(
        body, name="attn_bwd", grid=(NH, nk),
        in_specs=[pl.BlockSpec((None, tk, HP), lambda h, j: (h, j, 0)),
                  pl.BlockSpec((None, nsb, HP, tb), lambda h, j: (h, j, 0, 0)),
                  pl.BlockSpec((None, tk, VD), lambda h, j: (h, j, 0)),
                  pl.BlockSpec((None, S, HP), lambda h, j: (h, 0, 0)),
                  pl.BlockSpec((S, VD), lambda h, j: (0, h)),
                  pl.BlockSpec((None, nq, 1, tq), lambda h, j: (h, 0, 0, 0)),
                  pl.BlockSpec((None, nq, 1, tq), lambda h, j: (h, 0, 0, 0))],
        out_specs=[pl.BlockSpec((None, tk, HP), lambda h, j: (h, j, 0)),
                   pl.BlockSpec((None, tk, VD), lambda h, j: (h, j, 0)),
                   pl.BlockSpec((None, nq, HP, tq), lambda h, j: (h, 0, 0, 0), pipeline_mode=pl.Buffered(1))],
        out_shape=[jax.ShapeDtypeStruct((NH, S, HP), F32), jax.ShapeDtypeStruct((NH, S, VD), F32),
                   jax.ShapeDtypeStruct((NH, nq, HP, tq), F32)],
        compiler_params=_cp(("arbitrary", "arbitrary")),
    )(k, kt, v, q, do, lse.reshape(NH, nq, 1, tq), delta.reshape(NH, nq, 1, tq))


def _ssd_bwd(xbc, dtraw, zs, y, htp, dossm, conv_w, conv_b, dtb, alog, dskx, gssm):
    S = xbc.shape[0]
    nc = S // CH
    tri, triu, e16 = _ssd_consts()

    def body(xbc_ref, halo_ref, dtraw_ref, zs_ref, y_ref, htp_ref, dos_ref,
             cw_ref, cb_ref, dtb_ref, alog_ref, dsk_ref, g_ref, tri_ref, triu_ref, e16_ref,
             dxbc_ref, ddt_ref, dzs_ref, dcw_ref, dcb_ref, dvec_ref, dg_ref,
             dht, ext, dext, dskacc):
        r = pl.program_id(0)
        i = nc - 1 - r

        @pl.when(r == 0)
        def _():
            dht[...] = jnp.zeros_like(dht)
            dext[CH:CH + 8, :] = jnp.zeros((8, CC), F32)
            dskacc[...] = jnp.zeros_like(dskacc)
            dcw_ref[...] = jnp.zeros_like(dcw_ref)
            dcb_ref[...] = jnp.zeros_like(dcb_ref)
            dvec_ref[...] = jnp.zeros_like(dvec_ref)
            dg_ref[...] = jnp.zeros_like(dg_ref)

        xc, sact, act, arow, dtpre, dt, cum, cumx, dtx, wins = _ssd_chunk_fwd_common(
            xbc_ref, halo_ref, dtraw_ref, cw_ref, cb_ref, dtb_ref, alog_ref, tri_ref, e16_ref, ext, i == 0)
        cum_t = cum.T
        xs = act[:, 0:SW]
        lastx = cumx[CH - 1:CH, :]
        xh = xs * dtx
        eexp = jnp.exp(cumx)
        dte = jnp.exp(lastx - cumx)
        cdx = jnp.exp(lastx)
        trim = tri_ref[...].astype(F32) > 0.5
        lane = lax.broadcasted_iota(jnp.int32, (CH, 128), 1)
        rowi = lax.broadcasted_iota(jnp.int32, (CH, 128), 0)

        yv = y_ref[...]
        z = zs_ref[...]
        sz = _sigmoid(z)
        silz = z * sz
        hf = yv * silz
        dn = dos_ref[...] * g_ref[...]
        dhf_parts, nrm_parts = [], []
        for g in range(SG):
            gl = slice(g * GW, (g + 1) * GW)
            hg = hf[:, gl]
            rs = lax.rsqrt(jnp.mean(hg * hg, axis=-1, keepdims=True) + RMS_EPS)
            ng = hg * rs
            dng = dn[:, gl]
            dhf_parts.append(rs * (dng - ng * jnp.mean(dng * ng, axis=-1, keepdims=True)))
            nrm_parts.append(ng)
        nrm = jnp.concatenate(nrm_parts, axis=1)
        dhf = jnp.concatenate(dhf_parts, axis=1)
        dg_ref[...] += jnp.sum(dos_ref[...] * nrm, axis=0, keepdims=True)
        dyv = dhf * silz
        dzs_ref[...] = (dhf * yv * (sz * (1.0 + z * (1.0 - sz)))).astype(BF16)
        dskacc[...] += jnp.sum(dyv * xs, axis=0, keepdims=True)
        dxs_skip = dyv * dsk_ref[...]

        dhtn = dht[...]
        hp = htp_ref[...]
        dlastx = jnp.sum(dhtn * hp, axis=0, keepdims=True) * cdx
        xb = xh.astype(BF16)
        xwf = xh * dte
        dcum = jnp.zeros((CH, 128), F32)
        dcum_t = jnp.zeros((128, CH), F32)
        dxh_parts, dcumx_parts, dlast_parts, db_parts, dc_parts = [], [], [], [], []
        for g in range(SG):
            gl = slice(g * GW, (g + 1) * GW)
            bg = act[:, SW + g * SN:SW + (g + 1) * SN].astype(BF16)
            cg = act[:, SW + SG * SN + g * SN:SW + SG * SN + (g + 1) * SN].astype(BF16)
            hpg = hp[:, gl].astype(BF16)
            dhn = dhtn[:, gl].astype(BF16)
            dyg = dyv[:, gl]
            dz = (dyg * eexp[:, gl]).astype(BF16)
            dcg = _nt(dz, hpg)
            dht[:, gl] = dhtn[:, gl] * cdx[:, gl] + _tn(cg, dz)
            yoff = eexp[:, gl] * _mm(cg, hpg)
            dcumx_g = dyg * yoff
            dbg = _nt(xwf[:, gl].astype(BF16), dhn)
            dxw = _mm(bg, dhn)
            ddte = dxw * xwf[:, gl]
            dcumx_parts.append(dcumx_g - ddte)
            dlast_parts.append(jnp.sum(ddte, axis=0, keepdims=True))
            dxh_g = dxw * dte[:, gl]
            cbm = _nt(cg, bg)
            dcb = jnp.zeros((CH, CH), F32)
            dxp_parts = []
            for pr in range(GW // 128):
                h0 = g * (SH // SG) + 2 * pr
                lo = g * GW + pr * 128
                xp = xb[:, lo:lo + 128]
                dyp = dyv[:, lo:lo + 128]
                dxp = jnp.zeros((CH, 128), F32)
                for idx, hh in enumerate((h0, h0 + 1)):
                    decay = jnp.where(trim, jnp.exp(cum[:, hh:hh + 1] - cum_t[hh:hh + 1, :]), 0.0)
                    mh = cbm * decay
                    keep = (lane < SP) if idx == 0 else (lane >= SP)
                    dym = jnp.where(keep, dyp, 0.0).astype(BF16)
                    dm = _nt(dym, xp)
                    dxp = dxp + _tn(mh.astype(BF16), dym)
                    gm = dm * mh
                    dcum = dcum + jnp.where(lane == hh, jnp.sum(gm, axis=1, keepdims=True), 0.0)
                    dcum_t = dcum_t - jnp.where(rowi == hh, jnp.sum(gm, axis=0, keepdims=True), 0.0)
                    dcb = dcb + dm * decay
                dxp_parts.append(dxp)
            dxh_parts.append(dxh_g + jnp.concatenate(dxp_parts, axis=1))
            dcbb = dcb.astype(BF16)
            dc_parts.append(dcg + _mm(dcbb, bg))
            db_parts.append(dbg + _tn(dcbb, cg))
        dxh = jnp.concatenate(dxh_parts, axis=1)
        dcumx = jnp.concatenate(dcumx_parts, axis=1)
        dlastx = dlastx + jnp.concatenate(dlast_parts, axis=1)
        e16 = e16_ref[...]
        dlast128 = _nt_x(jnp.broadcast_to(dlastx, (8, SW)), e16, 2)[0:1, :]
        dcum = dcum + dcum_t.T + _nt_x(dcumx, e16, 2) + jnp.where(rowi == CH - 1, dlast128, 0.0)
        da = _xmm(triu_ref[...], dcum, 2)
        ddt = da * arow + _nt_x(dxh * xs, e16, 2)
        dvec_ref[1:2, :] += jnp.sum(da * dt, axis=0, keepdims=True)
        ddtraw = jnp.where(lane < SH, ddt * _sigmoid(dtpre), 0.0)
        dvec_ref[0:1, :] += jnp.sum(ddtraw, axis=0, keepdims=True)
        ddt_ref[...] = ddtraw.astype(BF16)
        dxs = dxs_skip + dxh * dtx
        dact = jnp.concatenate([dxs] + db_parts + dc_parts, axis=1)
        dxc = dact * (sact * (1.0 + xc * (1.0 - sact)))

        dcb_ref[...] += jnp.sum(dxc, axis=0, keepdims=True)
        for kk in range(CW):
            dcw_ref[kk:kk + 1, :] += jnp.sum(dxc * wins[kk], axis=0, keepdims=True)
        dext[0:CH, :] = dxc
        cw = cw_ref[...]
        dxr = cw[CW - 1:CW, :] * dxc
        for kk in range(CW - 1):
            dxr = dxr + cw[kk:kk + 1, :] * dext[CW - 1 - kk:CW - 1 - kk + CH, :]
        dxbc_ref[...] = dxr.astype(BF16)
        dext[CH:CH + 8, :] = dxc[0:8, :]

        @pl.when(r == nc - 1)
        def _():
            dvec_ref[1:2, :] = dvec_ref[1:2, :] * arow
            dvec_ref[2:3, :] = _nt_x(jnp.broadcast_to(dskacc[...], (8, SW)), e16, 3)[0:1, :]

    rev = lambda n: pl.BlockSpec((CH, n), lambda r: (nc - 1 - r, 0))
    return pl.pallas_call(
        body, name="ssd_bwd", grid=(nc,),
        in_specs=[rev(CC), pl.BlockSpec((8, CC), lambda r: (jnp.maximum((nc - 1 - r) * (CH // 8) - 1, 0), 0)),
                  rev(128), rev(SW), rev(SW), pl.BlockSpec((None, SN, SW), lambda r: (nc - 1 - r, 0, 0)), rev(SW),
                  _const((CW, CC)), _const((1, CC)), _const((1, 128)), _const((1, 128)), _const((1, SW)), _const((1, SW)),
                  _const((CH, CH)), _const((CH, CH)), _const((128, SW))],
        out_specs=[rev(CC), rev(128), rev(SW), _full((CW, CC)), _full((1, CC)), _full((8, 128)), _full((1, SW))],
        out_shape=[jax.ShapeDtypeStruct((S, CC), BF16), jax.ShapeDtypeStruct((S, 128), BF16), jax.ShapeDtypeStruct((S, SW), BF16),
                   jax.ShapeDtypeStruct((CW, CC), F32), jax.ShapeDtypeStruct((1, CC), F32),
                   jax.ShapeDtypeStruct((8, 128), F32), jax.ShapeDtypeStruct((1, SW), F32)],
        scratch_shapes=[pltpu.VMEM((SN, SW), F32), pltpu.VMEM((8 + CH, CC), F32), pltpu.VMEM((CH + 8, CC), F32),
                        pltpu.VMEM((1, SW), F32)],
        compiler_params=_cp(("arbitrary",)),
    )(xbc, xbc, dtraw, zs, y, htp, dossm, conv_w, conv_b, dtb, alog, dskx, gssm, tri, triu, e16)


def _mla_bwd(dqt, dk, dv, qlat, ckv, qg, kvg, wq, wk, wv, pos, invf):
    S = qlat.shape[0]
    tm = min(TQ, dqt.shape[3])
    per = dqt.shape[3] // tm

    def body(dq_ref, dk_ref, dv_ref, ql_ref, ckv_ref, qg_ref, kvg_ref, wq_ref, wk_ref, wv_ref, pos_ref, invf_ref,
             dql_ref, dckv_ref, dkr_ref, dwq_ref, dwk_ref, dwv_ref, dqg_ref, dkvg_ref):
        i = pl.program_id(0)

        @pl.when(i == 0)
        def _():
            dwq_ref[...] = jnp.zeros_like(dwq_ref)
            dwk_ref[...] = jnp.zeros_like(dwk_ref)
            dwv_ref[...] = jnp.zeros_like(dwv_ref)
            dqg_ref[...] = jnp.zeros_like(dqg_ref)
            dkvg_ref[...] = jnp.zeros_like(dkvg_ref)

        ang = pos_ref[...].astype(F32) * invf_ref[...]
        cs = jnp.cos(ang)
        sn = jnp.sin(ang)

        def rms_bwd(v, g, dn, dg_ref):
            r = lax.rsqrt(jnp.mean(v * v, axis=-1, keepdims=True) + RMS_EPS)
            vh = v * r
            dg_ref[...] += jnp.sum(dn * vh, axis=0, keepdims=True)
            dvh = dn * g
            return vh, r * (dvh - vh * jnp.mean(dvh * vh, axis=-1, keepdims=True))

        pieces = []
        for h in range(NH):
            dqh = dq_ref[h].T
            pieces.append(dqh[:, 0:NOPE] * SCALE)
            pieces.append(_rope_t(dqh[:, NOPE:HP], cs, sn) * SCALE)
        dqf = jnp.concatenate(pieces, axis=1).astype(BF16)
        ql = ql_ref[...]
        g = qg_ref[...]
        dqn = _nt(dqf, wq_ref[...])
        qh, dql = rms_bwd(ql, g, dqn, dqg_ref)
        dwq_ref[...] += _tn((qh * g).astype(BF16), dqf)
        dql_ref[...] = dql.astype(BF16)

        dkn_p = jnp.concatenate([dk_ref[h, :, 0:NOPE] for h in range(NH)], axis=1).astype(BF16)
        dvf = jnp.concatenate([dv_ref[h] for h in range(NH)], axis=1).astype(BF16)
        dkr = dk_ref[0, :, NOPE:HP]
        for h in range(1, NH):
            dkr = dkr + dk_ref[h, :, NOPE:HP]
        lane = lax.broadcasted_iota(jnp.int32, dkr.shape, 1)
        dkr_ref[...] = jnp.where(lane < ROPE, _rope_t(dkr, cs, sn), 0.0).astype(BF16)
        cv = ckv_ref[...]
        gk = kvg_ref[...]
        dkn = _nt(dkn_p, wk_ref[...]) + _nt(dvf, wv_ref[...])
        kh, dckv = rms_bwd(cv, gk, dkn, dkvg_ref)
        knb = (kh * gk).astype(BF16)
        dwk_ref[...] += _tn(knb, dkn_p)
        dwv_ref[...] += _tn(knb, dvf)
        dckv_ref[...] = dckv.astype(BF16)

    row = lambda n: pl.BlockSpec((tm, n), lambda i: (i, 0))
    heads = lambda n: pl.BlockSpec((NH, tm, n), lambda i: (0, i, 0))
    return pl.pallas_call(
        body, name="mla_bwd", grid=(S // tm,),
        in_specs=[pl.BlockSpec((NH, None, HP, tm), lambda i: (0, i // per, 0, i % per)), heads(HP), heads(VD), row(QL), row(KVL), _const((1, QL)), _const((1, KVL)),
                  _const((QL, NH * HP)), _const((KVL, NH * NOPE)), _const((KVL, NH * VD)), row(1), _const((1, 128))],
        out_specs=[row(QL), row(KVL), row(128), _full((QL, NH * HP)), _full((KVL, NH * NOPE)), _full((KVL, NH * VD)),
                   _full((1, QL)), _full((1, KVL))],
        out_shape=[jax.ShapeDtypeStruct((S, QL), BF16), jax.ShapeDtypeStruct((S, KVL), BF16), jax.ShapeDtypeStruct((S, 128), BF16),
                   jax.ShapeDtypeStruct((QL, NH * HP), F32), jax.ShapeDtypeStruct((KVL, NH * NOPE), F32),
                   jax.ShapeDtypeStruct((KVL, NH * VD), F32), jax.ShapeDtypeStruct((1, QL), F32), jax.ShapeDtypeStruct((1, KVL), F32)],
        compiler_params=_cp(("arbitrary",)),
    )(dqt, dk, dv, qlat, ckv, qg, kvg, wq, wk, wv, pos, invf)


def _inproj_bwd(x, gx1, mod, win, dql, dckv, dza, dxbc, dzs, dkr, ddt):
    S = x.shape[0]
    tm = min(TM, S)

    def body(x_ref, gx1_ref, mod_ref, win_ref, dql_ref, dckv_ref, dza_ref, dxbc_ref, dzs_ref, dkr_ref, ddt_ref,
             gx_ref, dw_ref, vec_ref):
        i = pl.program_id(0)

        @pl.when(i == 0)
        def _():
            dw_ref[...] = jnp.zeros_like(dw_ref)
            vec_ref[...] = jnp.zeros_like(vec_ref)

        shift = mod_ref[0:1, 0:D]
        scale = mod_ref[0:1, D:2 * D]
        xv = x_ref[...]
        ut = (xv * (1.0 + scale) + shift).T.astype(BF16)
        pieces = (dql_ref, dckv_ref, dza_ref, dxbc_ref, dzs_ref, dkr_ref, ddt_ref)
        du = jnp.zeros((tm, D), F32)
        lo = 0
        for p_ref in pieces:
            n = p_ref.shape[1]
            dp = p_ref[...]
            du = du + _nt(dp, win_ref[:, lo:lo + n])
            dw_ref[:, lo:lo + n] += _mm(ut, dp)
            lo += n
        vec_ref[0:1, :] += jnp.sum(du, axis=0, keepdims=True)
        vec_ref[1:2, :] += jnp.sum(du * xv, axis=0, keepdims=True)
        gx_ref[...] = gx1_ref[...] + du * (1.0 + scale)

    row = lambda n: pl.BlockSpec((tm, n), lambda i: (i, 0))
    return pl.pallas_call(
        body, name="inproj_bwd", grid=(S // tm,),
        in_specs=[row(D), row(D), _const((8, 3 * D)), _const((D, IN_P)), row(QL), row(KVL), row(D), row(CC), row(D),
                  row(128), row(128)],
        out_specs=[row(D), pl.BlockSpec((D, IN_P), lambda i: (0, 0), pipeline_mode=pl.Buffered(1)), _full((8, D))],
        out_shape=[jax.ShapeDtypeStruct((S, D), F32), jax.ShapeDtypeStruct((D, IN_P), F32), jax.ShapeDtypeStruct((8, D), F32)],
        compiler_params=_cp(("arbitrary",)),
    )(x, gx1, mod, win, dql, dckv, dza, dxbc, dzs, dkr, ddt)


def _ada_bwd(callt, dmods):
    w = dmods.shape[1]

    def body(c_ref, d_ref, o_ref):
        acc = c_ref[:, 0:1] * d_ref[0:1, :]
        for s in range(1, 8):
            acc = acc + c_ref[:, s:s + 1] * d_ref[s:s + 1, :]
        o_ref[0] = acc

    return pl.pallas_call(body, name="ada_bwd", out_shape=jax.ShapeDtypeStruct((1, D, w), F32),
                          compiler_params=_cp())(callt, dmods)


def _adamw(name, parts, w, m, v):
    rows, ncol = w.shape
    tr = min(rows, 128)
    nparts = parts.shape[0]

    def body(p_ref, w_ref, m_ref, v_ref, g_ref, d_ref, nm_ref, nv_ref):
        g = p_ref[0].astype(F32)
        for s in range(1, nparts):
            g = g + p_ref[s].astype(F32)
        g_ref[...] = g
        nm = B1 * m_ref[...] + (1.0 - B1) * g
        nv = B2 * v_ref[...] + (1.0 - B2) * (g * g)
        nm_ref[...] = nm
        nv_ref[...] = nv
        m_hat = nm / (1.0 - B1 ** STEP)
        v_hat = nv / (1.0 - B2 ** STEP)
        d_ref[...] = -LR * (m_hat / (jnp.sqrt(v_hat) + EPS) + WD * w_ref[...])

    row = pl.BlockSpec((tr, ncol), lambda i: (i, 0))
    sd = jax.ShapeDtypeStruct((rows, ncol), F32)
    return pl.pallas_call(
        body, name="adamw_" + name, grid=(rows // tr,),
        in_specs=[pl.BlockSpec((nparts, tr, ncol), lambda i: (0, i, 0)), row, row, row],
        out_specs=[row, row, row, row], out_shape=[sd, sd, sd, sd],
        compiler_params=_cp(("arbitrary",)),
    )(parts, w, m, v)


_SMALL = (("b_ada", 3 * D), ("conv_w", CW * CC // 4), ("conv_b", CC), ("ssm_norm_g", SW), ("ln_g", D), ("ln_b", D),
          ("q_norm_g", QL), ("kv_norm_g", KVL), ("dt_bias", SH), ("a_log", SH), ("d_skip", SH), ("loss", 128))


def _pack_small(d, lead):
    flat = [d[name].reshape(d[name].shape[:lead] + (-1,)) for name, _ in _SMALL]
    used = sum(f.shape[lead] for f in flat)
    pad = jnp.zeros(flat[0].shape[:lead] + (R_SMALL * 1024 - used,), F32)
    return jnp.concatenate(flat + [pad], axis=lead).reshape(flat[0].shape[:lead] + (R_SMALL, 1024))


def _unpack_small(p):
    flat = p.reshape(-1)
    out, r = {}, 0
    for name, n in _SMALL:
        out[name] = flat[r:r + n]
        r += n
    return out


def _in_to_padded(w):
    z = lambda n: jnp.zeros((w.shape[0], n), w.dtype)
    return jnp.concatenate([w[:, 0:384], w[:, 384:640], w[:, 704:1728], w[:, 1728:3264], w[:, 3280:4304],
                            w[:, 640:704], z(64), w[:, 3264:3280], z(112)], axis=1)


def _in_from_padded(g):
    return jnp.concatenate([g[:, 0:384], g[:, 384:640], g[:, P_KR[0]:P_KR[0] + 64], g[:, 640:1664], g[:, 1664:3200],
                            g[:, P_DT[0]:P_DT[0] + 16], g[:, 3200:4224]], axis=1)


def kernel(x, c, positions, w_ada, b_ada, w_in, q_norm_g, w_qb, kv_norm_g, w_kvb, conv_w, conv_b, dt_bias, a_log, d_skip, ssm_norm_g, w_out, ln_g, ln_b, loss_target, m_w_ada, m_b_ada, m_w_in, m_q_norm_g, m_w_qb, m_kv_norm_g, m_w_kvb, m_conv_w, m_conv_b, m_dt_bias, m_a_log, m_d_skip, m_ssm_norm_g, m_w_out, m_ln_g, m_ln_b, v_w_ada, v_b_ada, v_w_in, v_q_norm_g, v_w_qb, v_kv_norm_g, v_w_kvb, v_conv_w, v_conv_b, v_dt_bias, v_a_log, v_d_skip, v_ssm_norm_g, v_w_out, v_ln_g, v_ln_b):
    S = x.shape[1]
    xv = x[0]
    tgt = loss_target[0]

    cw16 = jnp.concatenate([conv_w[0], jnp.zeros((16 - CW, CC // 4), F32)], axis=0)
    f_in, f_qb, f_kvb, f_out, f_cw = _gather_weights(
        [w_in[0].astype(BF16), w_qb[0].astype(BF16), w_kvb[0].astype(BF16), w_out[0].astype(BF16), cw16])
    cat1 = lambda f: f.transpose(1, 0, 2).reshape(f.shape[1], 4 * f.shape[2])
    win = _in_to_padded(cat1(f_in))
    wqb = cat1(f_qb).reshape(QL, NH, QKD)
    wq = jnp.concatenate([wqb, jnp.zeros((QL, NH, HP - QKD), BF16)], axis=2).reshape(QL, NH * HP)
    wkvb = cat1(f_kvb).reshape(KVL, NH, NOPE + VD)
    wk = wkvb[:, :, 0:NOPE].reshape(KVL, NH * NOPE)
    wv = wkvb[:, :, NOPE:].reshape(KVL, NH * VD)
    wout = f_out.reshape(MIX, D)
    cwf = cat1(f_cw[:, 0:CW, :])

    half = ROPE // 2
    invf = 1.0 / (ROPE_THETA ** (jnp.arange(half, dtype=F32) / half))
    invf = jnp.concatenate([invf, invf, jnp.zeros((128 - ROPE,), F32)]).reshape(1, 128)
    pos = positions.reshape(S, 1)
    pad128 = lambda a: jnp.concatenate([a.reshape(1, SH), jnp.zeros((1, 128 - SH), F32)], axis=1)
    dtb, alog = pad128(dt_bias), pad128(a_log)
    dskx = jnp.repeat(d_skip.reshape(SH), SP).reshape(1, SW)

    my_c = lax.axis_index("c")
    (call,) = _exchange("gather_c", [jnp.broadcast_to(c.reshape(1, 1, D), (4, 1, D))])
    call = call.reshape(8, D)
    mods = _ada(call, w_ada[0])
    (mrows,) = _exchange("scatter_mod", [mods.reshape(4, 2, 3 * D // 4)])
    mine = lax.dynamic_index_in_dim(mrows.reshape(4, 2, 2, 3 * D // 4)[:, 0], my_c, axis=1, keepdims=False)
    mod = jnp.broadcast_to(mine.reshape(1, 3 * D) + b_ada, (8, 3 * D))
    qlat, ckv, za, xbc, zs, dtraw, q, k, v, kt, vt = _inproj(xv, mod, win, q_norm_g, kv_norm_g, wq, wk, wv, wk.T, wv.T, pos, invf)
    o, lse = _attn_fwd(q, k, vt)
    y, htp, ossm = _ssd_fwd(xbc, dtraw, zs, cwf, conv_b, dtb, alog, dskx, ssm_norm_g)
    gx1, do, dza, dossm, delta, dwout, vec_o = _outproj(o, za, ossm, xv, tgt, wout, mod, ln_g, ln_b)
    loss_part = jnp.zeros((128,), F32).at[0].set(0.5 / D * jnp.sum(vec_o[0]))

    dk, dv, dq = _attn_bwd(q, k, kt, v, do, lse, delta)
    dxbc, ddt, dzs, dcw, dcb, dvec, dgssm = _ssd_bwd(xbc, dtraw, zs, y, htp, dossm, cwf, conv_b, dtb, alog, dskx, ssm_norm_g)
    dql, dckv, dkr, dwq, dwk, dwv, dqg, dkvg = _mla_bwd(dq, dk, dv, qlat, ckv, q_norm_g, kv_norm_g, wq, wk, wv, pos, invf)
    gx, dwin, vec_i = _inproj_bwd(xv, gx1, mod, win, dql, dckv, dza, dxbc, dzs, dkr, ddt)
    dmod = jnp.concatenate([vec_i[0:1], vec_i[1:2], vec_o[3:4]], axis=1)

    cols = lambda g: g.reshape(g.shape[0], 4, g.shape[1] // 4).transpose(1, 0, 2)
    g_in = cols(_in_from_padded(dwin)).astype(BF16)
    g_qb = cols(dwq.reshape(QL, NH, HP)[:, :, 0:QKD].reshape(QL, NH * QKD)).astype(BF16)
    g_kvb = cols(jnp.concatenate([dwk.reshape(KVL, NH, NOPE), dwv.reshape(KVL, NH, VD)], axis=2)
                 .reshape(KVL, NH * (NOPE + VD))).astype(BF16)
    g_out = dwout.reshape(4, MIX // 4, D).astype(BF16)
    small = {"b_ada": dmod, "conv_b": dcb, "ssm_norm_g": dgssm, "ln_g": vec_o[1:2], "ln_b": vec_o[2:3],
             "q_norm_g": dqg, "kv_norm_g": dkvg, "dt_bias": dvec[0:1, 0:SH], "a_log": dvec[1:2, 0:SH], "d_skip": dvec[2:3, 0:SH],
             "loss": loss_part}
    small = {n: jnp.broadcast_to(a.reshape(1, -1), (4, a.size)) for n, a in small.items()}
    small["conv_w"] = cols(dcw).reshape(4, CW * CC // 4)
    gsmall = _pack_small(small, 1)

    r_in, r_qb, r_kvb, r_out, rs, dmods = _exchange(
        "exchange_grads", [g_in, g_qb, g_kvb, g_out, gsmall, jnp.broadcast_to(dmod.reshape(1, 1, 3 * D), (4, 1, 3 * D))])
    chip = 2 * lax.axis_index("x") + lax.axis_index("y")
    dmods = lax.dynamic_slice_in_dim(dmods.reshape(8, 3 * D), chip * (3 * D // 4), 3 * D // 4, axis=1)
    g_ada = _ada_bwd(call.T, dmods)
    res = dict(w_ada=_adamw("w_ada", g_ada, w_ada[0], m_w_ada[0], v_w_ada[0]),
               w_in=_adamw("w_in", r_in, w_in[0], m_w_in[0], v_w_in[0]),
               w_qb=_adamw("w_qb", r_qb, w_qb[0], m_w_qb[0], v_w_qb[0]),
               w_kvb=_adamw("w_kvb", r_kvb, w_kvb[0], m_w_kvb[0], v_w_kvb[0]),
               w_out=_adamw("w_out", r_out, w_out[0], m_w_out[0], v_w_out[0]))
    wsm = _pack_small(dict(b_ada=b_ada, conv_w=conv_w, conv_b=conv_b, ssm_norm_g=ssm_norm_g, ln_g=ln_g, ln_b=ln_b,
                           q_norm_g=q_norm_g, kv_norm_g=kv_norm_g, dt_bias=dt_bias, a_log=a_log, d_skip=d_skip, loss=jnp.zeros((128,), F32)), 0)
    msm = _pack_small(dict(b_ada=m_b_ada, conv_w=m_conv_w, conv_b=m_conv_b, ssm_norm_g=m_ssm_norm_g, ln_g=m_ln_g, ln_b=m_ln_b,
                           q_norm_g=m_q_norm_g, kv_norm_g=m_kv_norm_g, dt_bias=m_dt_bias, a_log=m_a_log, d_skip=m_d_skip, loss=jnp.zeros((128,), F32)), 0)
    vsm = _pack_small(dict(b_ada=v_b_ada, conv_w=v_conv_w, conv_b=v_conv_b, ssm_norm_g=v_ssm_norm_g, ln_g=v_ln_g, ln_b=v_ln_b,
                           q_norm_g=v_q_norm_g, kv_norm_g=v_kv_norm_g, dt_bias=v_dt_bias, a_log=v_a_log, d_skip=v_d_skip, loss=jnp.zeros((128,), F32)), 0)
    sm = _adamw("small", rs, wsm, msm, vsm)

    order = ["w_ada", "b_ada", "w_in", "q_norm_g", "w_qb", "kv_norm_g", "w_kvb", "conv_w", "conv_b", "dt_bias", "a_log",
             "d_skip", "ssm_norm_g", "w_out", "ln_g", "ln_b"]
    shapes = dict(w_ada=w_ada.shape, b_ada=b_ada.shape, w_in=w_in.shape, q_norm_g=q_norm_g.shape, w_qb=w_qb.shape,
                  kv_norm_g=kv_norm_g.shape, w_kvb=w_kvb.shape, conv_w=conv_w.shape, conv_b=conv_b.shape, dt_bias=dt_bias.shape,
                  a_log=a_log.shape, d_skip=d_skip.shape, ssm_norm_g=ssm_norm_g.shape, w_out=w_out.shape, ln_g=ln_g.shape,
                  ln_b=ln_b.shape)
    outs = []
    for kind in range(4):
        d = _unpack_small(sm[kind])
        d.update({n: r[kind] for n, r in res.items()})
        outs.extend(d[n].reshape(shapes[n]) for n in order)
    loss = _unpack_small(sm[0])["loss"][0]
    return (loss, gx.reshape(x.shape), *outs)
```

```python
import functools
import math

import numpy as np
import jax
import jax.numpy as jnp
from jax import lax
from jax.experimental import pallas as pl
from jax.experimental.pallas import tpu as pltpu

F32 = jnp.float32
BF16 = jnp.bfloat16
MESH_ID = pl.DeviceIdType.MESH

D = 1024
NH = 8
NOPE = 128
ROPE = 64
VD = 128
VDP = 144
QKD = NOPE + ROPE
HP = 256
QL = 384
KVL = 256
ROPE_THETA = 10000.0
SH = 16
SP = 64
SG = 2
SN = 128
CW = 4
CH = 128
SW = SH * SP
CC = SW + 2 * SG * SN
GW = SW // SG
MIX = 2 * D
IN_W = 4304
ALPHA = 2.0 ** 0.25
RMS_EPS = 1e-6
LN_EPS = 1e-5
SCALE = QKD ** -0.5
LN2 = math.log(2.0)
QSCALE = SCALE / LN2
LR, B1, B2, EPS, WD, STEP = 0.001, 0.9, 0.999, 1e-08, 0.01, 10

P_Q = (0, 384)
P_KV = (384, 640)
P_ZA = (640, 1664)
P_XBC = (1664, 3200)
P_ZS = (3200, 4224)
P_KR = (4224, 4352)
P_DT = (4352, 4480)
IN_P = 4480

R_SMALL = 16

TM = 256
TQ = 512
TQF = 1024
VMEM_LIMIT = 56 * 1024 * 1024


def _cp(sem=None):
    return pltpu.CompilerParams(dimension_semantics=sem, vmem_limit_bytes=VMEM_LIMIT)


def _mm(a, b):
    return jnp.dot(a, b, preferred_element_type=F32)


def _nt(a, b):
    return lax.dot_general(a, b, (((1,), (1,)), ((), ())), preferred_element_type=F32)


def _tn(a, b):
    return lax.dot_general(a, b, (((0,), (0,)), ((), ())), preferred_element_type=F32)


def _split(a, terms):
    parts = []
    for t in range(terms):
        p = a.astype(BF16)
        parts.append(p)
        if t + 1 < terms:
            a = a - p.astype(F32)
    return parts


def _mm_x(a, ones, terms):
    parts = _split(a, terms)
    out = _mm(parts[0], ones)
    for p in parts[1:]:
        out = out + _mm(p, ones)
    return out


def _xmm(ones, a, terms):
    parts = _split(a, terms)
    out = _mm(ones, parts[0])
    for p in parts[1:]:
        out = out + _mm(ones, p)
    return out


def _nt_x(a, ones, terms):
    parts = _split(a, terms)
    out = _nt(parts[0], ones)
    for p in parts[1:]:
        out = out + _nt(p, ones)
    return out


def _sigmoid(z):
    return 1.0 / (1.0 + jnp.exp(-z))


def _softplus(z):
    return jnp.maximum(z, 0.0) + jnp.log1p(jnp.exp(-jnp.abs(z)))


def _rope(t, cs, sn):
    lane = lax.broadcasted_iota(jnp.int32, t.shape, 1)
    rot = jnp.where(lane < ROPE // 2, -pltpu.roll(t, 128 - ROPE // 2, 1), pltpu.roll(t, ROPE // 2, 1))
    return t * cs + rot * sn


def _rope_t(t, cs, sn):
    lane = lax.broadcasted_iota(jnp.int32, t.shape, 1)
    y = t * sn
    rot = jnp.where(lane < ROPE // 2, -pltpu.roll(y, 128 - ROPE // 2, 1), pltpu.roll(y, ROPE // 2, 1))
    return t * cs - rot


def _full(shape):
    n = len(shape)
    return pl.BlockSpec(shape, lambda *_: (0,) * n)


def _const(shape):
    n = len(shape)
    return pl.BlockSpec(shape, lambda *_: (0,) * n, pipeline_mode=pl.Buffered(1))


def _gather_weights(shards):
    n = len(shards)
    halves = [a.shape[0] // 2 for a in shards]

    def body(*refs):
        srcs, dsts = refs[:n], refs[n:2 * n]
        send_sems, recv_sems, local_sems = refs[2 * n:]
        x, y, c = lax.axis_index("x"), lax.axis_index("y"), lax.axis_index("c")
        me = 2 * x + y
        sibling = (x, y, 1 - c)
        chips = [(1 - x, y), (x, 1 - y), (1 - x, 1 - y)]

        def rows(a, pc):
            return pl.ds(pl.multiple_of(pc * halves[a], halves[a]), halves[a])

        def copy(a, k, src, slot, pc, to):
            return pltpu.make_async_remote_copy(
                src_ref=src, dst_ref=dsts[a].at[slot, rows(a, pc)], send_sem=send_sems.at[a, k],
                recv_sem=recv_sems.at[a, k], device_id=to, device_id_type=MESH_ID)

        local = [pltpu.make_async_copy(srcs[a], dsts[a].at[me], local_sems.at[a]) for a in range(n)]
        for cp in local:
            cp.start()
        sends = [copy(a, j, srcs[a].at[rows(a, c)], me, c, (px, py, c)) for a in range(n) for j, (px, py) in enumerate(chips)]
        for cp in sends:
            cp.start()
        passed = []
        for a in range(n):
            for j, (px, py) in enumerate(chips):
                k = 2 * px + py
                copy(a, j, srcs[a].at[rows(a, c)], k, c, (x, y, c)).wait_recv()
                fwd = copy(a, 3 + j, dsts[a].at[k, rows(a, c)], k, c, sibling)
                fwd.start()
                passed.append(fwd)
        for a in range(n):
            for j, (px, py) in enumerate(chips):
                copy(a, 3 + j, srcs[a].at[rows(a, c)], 2 * px + py, 1 - c, (x, y, c)).wait_recv()
        for cp in sends + passed:
            cp.wait_send()
        for cp in local:
            cp.wait()

    hbm = pl.BlockSpec(memory_space=pltpu.HBM)
    return pl.pallas_call(
        body, name="gather_weights",
        out_shape=tuple(jax.ShapeDtypeStruct((4,) + a.shape, a.dtype) for a in shards),
        in_specs=[hbm] * n, out_specs=tuple([hbm] * n),
        scratch_shapes=[pltpu.SemaphoreType.DMA((n, 6)), pltpu.SemaphoreType.DMA((n, 6)), pltpu.SemaphoreType.DMA((n,))],
    )(*shards)


def _exchange(name, slabs):
    n = len(slabs)

    def body(*refs):
        srcs, dsts = refs[:n], refs[n:2 * n]
        send_sems, recv_sems, local_sems = refs[2 * n:]
        x, y, c = lax.axis_index("x"), lax.axis_index("y"), lax.axis_index("c")
        chip = 2 * x + y
        sibling = (x, y, 1 - c)
        chips = [(1 - x, y), (x, 1 - y), (1 - x, 1 - y)]

        def slot(px, py, pc):
            return 4 * px + 2 * py + pc

        def copy(a, k, src, s, to):
            return pltpu.make_async_remote_copy(
                src_ref=src, dst_ref=dsts[a].at[s], send_sem=send_sems.at[a, k], recv_sem=recv_sems.at[a, k],
                device_id=to, device_id_type=MESH_ID)

        mine = slot(x, y, c)
        local = [pltpu.make_async_copy(srcs[a].at[chip], dsts[a].at[mine], local_sems.at[a]) for a in range(n)]
        for cp in local:
            cp.start()
        first = []
        for a in range(n):
            first.append(copy(a, 0, srcs[a].at[chip], mine, sibling))
            for j, (px, py) in enumerate(chips):
                first.append(copy(a, 1 + j, srcs[a].at[2 * px + py], mine, (px, py, c)))
        for cp in first:
            cp.start()
        passed = []
        for a in range(n):
            for j, (px, py) in enumerate(chips):
                s = slot(px, py, c)
                copy(a, 1 + j, srcs[a].at[chip], s, (x, y, c)).wait_recv()
                fwd = copy(a, 4 + j, dsts[a].at[s], s, sibling)
                fwd.start()
                passed.append(fwd)
        for a in range(n):
            copy(a, 0, srcs[a].at[chip], slot(x, y, 1 - c), (x, y, c)).wait_recv()
            for j, (px, py) in enumerate(chips):
                copy(a, 4 + j, srcs[a].at[chip], slot(px, py, 1 - c), (x, y, c)).wait_recv()
        for cp in first + passed:
            cp.wait_send()
        for cp in local:
            cp.wait()

    hbm = pl.BlockSpec(memory_space=pltpu.HBM)
    return pl.pallas_call(
        body, name=name,
        out_shape=tuple(jax.ShapeDtypeStruct((8,) + a.shape[1:], a.dtype) for a in slabs),
        in_specs=[hbm] * n, out_specs=tuple([hbm] * n),
        scratch_shapes=[pltpu.SemaphoreType.DMA((n, 7)), pltpu.SemaphoreType.DMA((n, 7)), pltpu.SemaphoreType.DMA((n,))],
    )(*slabs)


def _ada(call, w_shard):
    def body(c_ref, w_ref, o_ref):
        o_ref[...] = _mm(c_ref[...].astype(BF16), w_ref[...].astype(BF16))

    return pl.pallas_call(body, name="ada", out_shape=jax.ShapeDtypeStruct((8, w_shard.shape[1]), F32),
                          compiler_params=_cp())(call, w_shard)


def _inproj(x, mod, win, qg, kvg, wq, wk, wv, wkt, wvt, pos, invf):
    S = x.shape[0]
    tm = min(TM, S)

    def body(x_ref, mod_ref, win_ref, qg_ref, kvg_ref, wq_ref, wk_ref, wv_ref, wkt_ref, wvt_ref, pos_ref, invf_ref,
             qlat_ref, ckv_ref, za_ref, xbc_ref, zs_ref, dt_ref, q_ref, k_ref, v_ref, kt_ref, vt_ref):
        shift = mod_ref[0:1, 0:D]
        scale = mod_ref[0:1, D:2 * D]
        u = (x_ref[...] * (1.0 + scale) + shift).astype(BF16)

        def proj(p):
            return _mm(u, win_ref[:, p[0]:p[1]])

        ql = proj(P_Q)
        ckv = proj(P_KV)
        qlat_ref[...] = ql
        ckv_ref[...] = ckv
        za_ref[...] = proj(P_ZA)
        xbc_ref[...] = proj(P_XBC)
        zs_ref[...] = proj(P_ZS)
        dt_ref[...] = proj(P_DT)
        kr = proj(P_KR)

        ang = pos_ref[...].astype(F32) * invf_ref[...]
        cs = jnp.cos(ang)
        sn = jnp.sin(ang)

        rq = lax.rsqrt(jnp.mean(ql * ql, axis=-1, keepdims=True) + RMS_EPS)
        qn = (ql * rq * qg_ref[...]).astype(BF16)
        for h in range(NH):
            qh = _mm(qn, wq_ref[:, h * HP:(h + 1) * HP])
            q_ref[h, :, 0:NOPE] = (qh[:, 0:NOPE] * QSCALE).astype(BF16)
            q_ref[h, :, NOPE:HP] = (_rope(qh[:, NOPE:HP], cs, sn) * QSCALE).astype(BF16)

        rk = lax.rsqrt(jnp.mean(ckv * ckv, axis=-1, keepdims=True) + RMS_EPS)
        kn = (ckv * rk * kvg_ref[...]).astype(BF16)
        knope = _mm(kn, wk_ref[...])
        vall = _mm(kn, wv_ref[...])
        krf = _rope(kr, cs, sn)
        krr = krf.astype(BF16)
        krt = krf.T.astype(BF16)
        ones_rows = jnp.where(lax.broadcasted_iota(jnp.int32, (VDP - VD, tm), 0) == 0, 1.0, 0.0).astype(BF16)
        for h in range(NH):
            k_ref[h, :, 0:NOPE] = knope[:, h * NOPE:(h + 1) * NOPE].astype(BF16)
            k_ref[h, :, NOPE:HP] = krr
            v_ref[h] = vall[:, h * VD:(h + 1) * VD].astype(BF16)
            kt_ref[h, 0:NOPE, :] = _nt(wkt_ref[h * NOPE:(h + 1) * NOPE, :], kn).astype(BF16)
            kt_ref[h, NOPE:HP, :] = krt
            vt_ref[h, 0:VD, :] = _nt(wvt_ref[h * VD:(h + 1) * VD, :], kn).astype(BF16)
            vt_ref[h, VD:VDP, :] = ones_rows

    row = lambda n: pl.BlockSpec((tm, n), lambda i: (i, 0))
    heads = lambda n: pl.BlockSpec((NH, tm, n), lambda i: (0, i, 0))
    heads_t = lambda n: pl.BlockSpec((NH, None, n, tm), lambda i: (0, i, 0, 0))
    sd = lambda n: jax.ShapeDtypeStruct((S, n), F32)
    hd = lambda n: jax.ShapeDtypeStruct((NH, S, n), BF16)
    ht = lambda n: jax.ShapeDtypeStruct((NH, S // tm, n, tm), BF16)
    return pl.pallas_call(
        body, name="inproj", grid=(S // tm,),
        in_specs=[row(D), _const((8, 3 * D)), _const((D, IN_P)), _const((1, QL)), _const((1, KVL)),
                  _const((QL, NH * HP)), _const((KVL, NH * NOPE)), _const((KVL, NH * VD)),
                  _const((NH * NOPE, KVL)), _const((NH * VD, KVL)), row(1), _const((1, 128))],
        out_specs=[row(QL), row(KVL), row(D), row(CC), row(D), row(128), heads(HP), heads(HP), heads(VD),
                   heads_t(HP), heads_t(VDP)],
        out_shape=[sd(QL), sd(KVL), sd(D), sd(CC), sd(D), sd(128), hd(HP), hd(HP), hd(VD), ht(HP), ht(VDP)],
        compiler_params=_cp(("arbitrary",)),
    )(x, mod, win, qg, kvg, wq, wk, wv, wkt, wvt, pos, invf)


def _attn_fwd(q, k, vt):
    _, S, _ = q.shape
    tq = min(TQF, S)
    nq = S // tq
    half = tq // 2
    tb = vt.shape[3]
    nsb = half // tb

    def body(q_ref, k_ref, vt_ref, o_ref, lse_ref):
        i = pl.program_id(1)
        qb = q_ref[...]

        def scores(j, hb):
            off = pl.multiple_of(j * tq + hb * half, half)
            return _nt(k_ref[pl.ds(off, half), :], qb)

        def update(j, hb, s, carry):
            m, acc = carry
            m_new = jnp.maximum(m, jnp.max(s, axis=0, keepdims=True))
            a = jnp.exp2(m - m_new)
            pb = jnp.exp2(s - m_new).astype(BF16)
            acc = a * acc
            for sb in range(nsb):
                acc = acc + _mm(vt_ref[(2 * j + hb) * nsb + sb], pb[sb * tb:(sb + 1) * tb, :])
            return m_new, acc

        def trip(j, carry, masked):
            s = [scores(j, hb) for hb in range(2)]
            if masked:
                r = lax.broadcasted_iota(jnp.int32, s[0].shape, 0)
                cidx = lax.broadcasted_iota(jnp.int32, s[0].shape, 1)
                s = [jnp.where(cidx >= r + hb * half, s[hb], -1e30) for hb in range(2)]
            for hb in range(2):
                carry = update(j, hb, s[hb], carry)
            return carry

        def finish(carry):
            m, acc = carry
            l = acc[VD:VD + 1, :]
            o_ref[...] = (acc[0:VD, :] / l).T
            lse_ref[...] = m + jnp.log2(l)

        init = (jnp.full((1, tq), -1e30, F32), jnp.zeros((VDP, tq), F32))
        carry = lax.fori_loop(0, i >> 1, lambda t, cr: trip(2 * t + 1, trip(2 * t, cr, False), False), init)

        @pl.when((i & 1) == 0)
        def _():
            finish(trip(i, carry, True))

        @pl.when((i & 1) == 1)
        def _():
            finish(trip(i, trip(i - 1, carry, False), True))

    return pl.pallas_call(
        body, name="attn_fwd", grid=(NH, nq),
        in_specs=[pl.BlockSpec((None, tq, HP), lambda h, i: (h, i, 0)),
                  pl.BlockSpec((None, S, HP), lambda h, i: (h, 0, 0)),
                  pl.BlockSpec((None, S // tb, VDP, tb), lambda h, i: (h, 0, 0, 0))],
        out_specs=[pl.BlockSpec((tq, VD), lambda h, i: (i, h)),
                   pl.BlockSpec((None, None, 1, tq), lambda h, i: (h, i, 0, 0))],
        out_shape=[jax.ShapeDtypeStruct((S, NH * VD), F32), jax.ShapeDtypeStruct((NH, nq, 1, tq), F32)],
        compiler_params=_cp(("arbitrary", "arbitrary")),
    )(q, k, vt)


def _ssd_consts():
    tri = np.tril(np.ones((CH, CH), np.float32))
    e16 = np.zeros((128, SW), np.float32)
    for h in range(SH):
        e16[h, h * SP:(h + 1) * SP] = 1.0
    return jnp.asarray(tri, BF16), jnp.asarray(tri.T.copy(), BF16), jnp.asarray(e16, BF16)


def _ssd_chunk_fwd_common(xbc_ref, halo_ref, dtraw_ref, cw_ref, cb_ref, dtb_ref, alog_ref, tri_ref, e16_ref, ext, first):
    ext[0:8, :] = jnp.where(first, 0.0, halo_ref[...])
    ext[8:8 + CH, :] = xbc_ref[...]
    cw = cw_ref[...]
    wins = [ext[5 + kk:5 + kk + CH, :] for kk in range(CW)]
    xc = cb_ref[...] + cw[0:1, :] * wins[0]
    for kk in range(1, CW):
        xc = xc + cw[kk:kk + 1, :] * wins[kk]
    sact = _sigmoid(xc)
    act = xc * sact
    lane = lax.broadcasted_iota(jnp.int32, (1, 128), 1)
    arow = jnp.where(lane < SH, -jnp.exp(alog_ref[...]), 0.0)
    dtpre = dtraw_ref[...] + dtb_ref[...]
    dt = _softplus(dtpre)
    a = dt * arow
    cum = _xmm(tri_ref[...], a, 3)
    cumx = _mm_x(cum, e16_ref[...], 3)
    dtx = _mm_x(dt, e16_ref[...], 2)
    return xc, sact, act, arow, dtpre, dt, cum, cumx, dtx, wins


def _ssd_fwd(xbc, dtraw, zs, conv_w, conv_b, dtb, alog, dskx, gssm):
    S = xbc.shape[0]
    nc = S // CH
    tri, _, e16 = _ssd_consts()

    def body(xbc_ref, halo_ref, dtraw_ref, zs_ref, cw_ref, cb_ref, dtb_ref, alog_ref, dsk_ref, g_ref, tri_ref, e16_ref,
             y_ref, htp_ref, o_ref, ht, ext):
        i = pl.program_id(0)

        @pl.when(i == 0)
        def _():
            ht[...] = jnp.zeros_like(ht)

        xc, sact, act, arow, dtpre, dt, cum, cumx, dtx, wins = _ssd_chunk_fwd_common(
            xbc_ref, halo_ref, dtraw_ref, cw_ref, cb_ref, dtb_ref, alog_ref, tri_ref, e16_ref, ext, i == 0)
        cum_t = cum.T
        xs = act[:, 0:SW]
        lastx = cumx[CH - 1:CH, :]
        xh = xs * dtx
        eexp = jnp.exp(cumx)
        dte = jnp.exp(lastx - cumx)
        cdx = jnp.exp(lastx)
        htp = ht[...]
        htp_ref[...] = htp
        xw = (xh * dte).astype(BF16)
        xb = xh.astype(BF16)
        trim = tri_ref[...].astype(F32) > 0.5
        lane = lax.broadcasted_iota(jnp.int32, (CH, 128), 1)
        parts = []
        for g in range(SG):
            gl = slice(g * GW, (g + 1) * GW)
            bg = act[:, SW + g * SN:SW + (g + 1) * SN].astype(BF16)
            cg = act[:, SW + SG * SN + g * SN:SW + SG * SN + (g + 1) * SN].astype(BF16)
            cbm = _nt(cg, bg)
            yoff = eexp[:, gl] * _mm(cg, htp[:, gl].astype(BF16))
            ht[:, gl] = htp[:, gl] * cdx[:, gl] + _tn(bg, xw[:, gl])
            for pr in range(GW // 128):
                h0 = g * (SH // SG) + 2 * pr
                lo = g * GW + pr * 128
                xp = xb[:, lo:lo + 128]
                res = []
                for hh in (h0, h0 + 1):
                    seg = cum[:, hh:hh + 1] - cum_t[hh:hh + 1, :]
                    mh = jnp.where(trim, cbm * jnp.exp(seg), 0.0).astype(BF16)
                    res.append(_mm(mh, xp))
                parts.append(jnp.where(lane < SP, res[0], res[1]) + yoff[:, pr * 128:(pr + 1) * 128])
        y = jnp.concatenate(parts, axis=1) + xs * dsk_ref[...]
        y_ref[...] = y
        z = zs_ref[...]
        hf = y * (z * _sigmoid(z))
        outs = []
        for g in range(SG):
            hg = hf[:, g * GW:(g + 1) * GW]
            rs = lax.rsqrt(jnp.mean(hg * hg, axis=-1, keepdims=True) + RMS_EPS)
            outs.append(hg * rs)
        o_ref[...] = (jnp.concatenate(outs, axis=1) * g_ref[...]).astype(BF16)

    row = lambda n: pl.BlockSpec((CH, n), lambda i: (i, 0))
    return pl.pallas_call(
        body, name="ssd_fwd", grid=(nc,),
        in_specs=[row(CC), pl.BlockSpec((8, CC), lambda i: (jnp.maximum(i * (CH // 8) - 1, 0), 0)), row(128), row(SW),
                  _const((CW, CC)), _const((1, CC)), _const((1, 128)), _const((1, 128)), _const((1, SW)), _const((1, SW)),
                  _const((CH, CH)), _const((128, SW))],
        out_specs=[row(SW), pl.BlockSpec((None, SN, SW), lambda i: (i, 0, 0)), row(SW)],
        out_shape=[jax.ShapeDtypeStruct((S, SW), F32), jax.ShapeDtypeStruct((nc, SN, SW), F32),
                   jax.ShapeDtypeStruct((S, SW), BF16)],
        scratch_shapes=[pltpu.VMEM((SN, SW), F32), pltpu.VMEM((8 + CH, CC), F32)],
        compiler_params=_cp(("arbitrary",)),
    )(xbc, xbc, dtraw, zs, conv_w, conv_b, dtb, alog, dskx, gssm, tri, e16)


def _outproj(o, za, ossm, x, tgt, wout, mod, ln_g, ln_b):
    S = x.shape[0]
    tm = min(TM, S)

    def body(o_ref, za_ref, os_ref, x_ref, t_ref, w_ref, mod_ref, g_ref, b_ref,
             gx_ref, do_ref, dza_ref, dos_ref, delta_ref, dw_ref, vec_ref):
        i = pl.program_id(0)

        @pl.when(i == 0)
        def _():
            dw_ref[...] = jnp.zeros_like(dw_ref)
            vec_ref[...] = jnp.zeros_like(vec_ref)

        gate = mod_ref[0:1, 2 * D:3 * D]
        ov = o_ref[...]
        z = za_ref[...]
        sz = _sigmoid(z)
        silz = z * sz
        a = (ov * silz).astype(BF16)
        osb = os_ref[...]
        mixed = _mm(a, w_ref[0:D, :]) + _mm(osb, w_ref[D:MIX, :])
        xv = x_ref[...]
        hres = ALPHA * xv + gate * mixed
        mu = jnp.mean(hres, axis=-1, keepdims=True)
        hc = hres - mu
        var = jnp.mean(hc * hc, axis=-1, keepdims=True)
        rstd = lax.rsqrt(var + LN_EPS)
        xhat = hc * rstd
        g = g_ref[...]
        yv = xhat * g + b_ref[...]
        err = yv - t_ref[...]
        dy = err * (1.0 / D)
        vec_ref[0:1, :] += jnp.sum(err * err, axis=0, keepdims=True)
        vec_ref[1:2, :] += jnp.sum(dy * xhat, axis=0, keepdims=True)
        vec_ref[2:3, :] += jnp.sum(dy, axis=0, keepdims=True)
        dxh = dy * g
        dh = rstd * (dxh - jnp.mean(dxh, axis=-1, keepdims=True) - xhat * jnp.mean(dxh * xhat, axis=-1, keepdims=True))
        gx_ref[...] = ALPHA * dh
        vec_ref[3:4, :] += jnp.sum(dh * mixed, axis=0, keepdims=True)
        dmixed = (gate * dh).astype(BF16)
        dw_ref[0:D, :] += _tn(a, dmixed)
        dw_ref[D:MIX, :] += _tn(osb, dmixed)
        da = _nt(dmixed, w_ref[0:D, :])
        dos_ref[...] = _nt(dmixed, w_ref[D:MIX, :])
        dov = da * silz
        do_ref[...] = dov.astype(BF16)
        dza_ref[...] = (da * ov * (sz * (1.0 + z * (1.0 - sz)))).astype(BF16)
        pr = dov * ov
        for h in range(NH):
            delta_ref[h] = jnp.sum(pr[:, h * VD:(h + 1) * VD], axis=-1, keepdims=True)

    row = lambda n: pl.BlockSpec((tm, n), lambda i: (i, 0))
    return pl.pallas_call(
        body, name="outproj", grid=(S // tm,),
        in_specs=[row(D), row(D), row(D), row(D), row(D), _const((MIX, D)), _const((8, 3 * D)), _const((1, D)), _const((1, D))],
        out_specs=[row(D), row(D), row(D), row(D), pl.BlockSpec((NH, tm, 1), lambda i: (0, i, 0)),
                   _full((MIX, D)), _full((8, D))],
        out_shape=[jax.ShapeDtypeStruct((S, D), F32), jax.ShapeDtypeStruct((S, D), BF16), jax.ShapeDtypeStruct((S, D), BF16),
                   jax.ShapeDtypeStruct((S, D), F32), jax.ShapeDtypeStruct((NH, S, 1), F32),
                   jax.ShapeDtypeStruct((MIX, D), F32), jax.ShapeDtypeStruct((8, D), F32)],
        compiler_params=_cp(("arbitrary",)),
    )(o, za, ossm, x, tgt, wout, mod, ln_g, ln_b)


def _attn_bwd(q, k, kt, v, do, lse, delta):
    _, S, _ = q.shape
    tk = min(TQ, S // 2)
    nk = S // tk
    tq = 2 * tk
    nq = S // tq
    tb = kt.shape[3]
    nsb = tk // tb

    def body(k_ref, kt_ref, v_ref, q_ref, do_ref, lse_ref, dl_ref, dk_ref, dv_ref, dq_ref, dqt_ref):
        j = pl.program_id(1)
        kb = k_ref[...]
        vb = v_ref[...]

        @pl.when(j == 0)
        def _():
            dqt_ref[...] = jnp.zeros_like(dqt_ref)

        dk_ref[...] = jnp.zeros_like(dk_ref)
        dv_ref[...] = jnp.zeros_like(dv_ref)

        def step(i, masked, lo=0):
            off = pl.multiple_of(i * tq + lo, tk)
            qb = q_ref[pl.ds(off, tq - lo), :]
            dob = do_ref[pl.ds(off, tq - lo), :]
            pt = jnp.exp2(_nt(kb, qb) - lse_ref[i][:, lo:tq])
            if masked:
                r = lax.broadcasted_iota(jnp.int32, pt.shape, 0)
                cidx = lax.broadcasted_iota(jnp.int32, pt.shape, 1)
                pt = jnp.where(i * tq + lo + cidx >= j * tk + r, pt, 0.0)
            dv_ref[...] += _mm(pt.astype(BF16), dob)
            dsb = (pt * (_nt(vb, dob) - dl_ref[i][:, lo:tq])).astype(BF16)
            dk_ref[...] += _mm(dsb, qb)
            acc = dqt_ref[i, :, lo:tq]
            for sb in range(nsb):
                acc = acc + _mm(kt_ref[sb], dsb[sb * tb:(sb + 1) * tb, :])
            dqt_ref[i, :, lo:tq] = acc

        first = j >> 1

        @pl.when((j & 1) == 0)
        def _():
            step(first, True)

        @pl.when((j & 1) == 1)
        def _():
            step(first, True, tk)
            dq_ref[...] = dqt_ref[first].T

        def loop_body(i, carry):
            step(i, False)
            return carry

        lax.fori_loop(first + 1, nq, loop_body, 0)
        dk_ref[...] = dk_ref[...] * LN2

    return pl.pallas_call(
        body, name="attn_bwd", grid=(NH, nk),
        in_specs=[pl.BlockSpec((None, tk, HP), lambda h, j: (h, j, 0)),
                  pl.BlockSpec((None, nsb, HP, tb), lambda h, j: (h, j, 0, 0)),
                  pl.BlockSpec((None, tk, VD), lambda h, j: (h, j, 0)),
                  pl.BlockSpec((None, S, HP), lambda h, j: (h, 0, 0)),
                  pl.BlockSpec((S, VD), lambda h, j: (0, h)),
                  pl.BlockSpec((None, nq, 1, tq), lambda h, j: (h, 0, 0, 0)),
                  pl.BlockSpec((None, nq, 1, tq), lambda h, j: (h, 0, 0, 0))],
        out_specs=[pl.BlockSpec((None, tk, HP), lambda h, j: (h, j, 0)),
                   pl.BlockSpec((None, tk, VD), lambda h, j: (h, j, 0)),
                   pl.BlockSpec((None, tq, HP), lambda h, j: (h, j >> 1, 0))],
        out_shape=[jax.ShapeDtypeStruct((NH, S, HP), F32), jax.ShapeDtypeStruct((NH, S, VD), F32),
                   jax.ShapeDtypeStruct((NH, S, HP), F32)],
        scratch_shapes=[pltpu.VMEM((nq, HP, tq), F32)],
        compiler_params=_cp(("arbitrary", "arbitrary")),
    )(k, kt, v, q, do, lse.reshape(NH, nq, 1, tq), delta.reshape(NH, nq, 1, tq))


def _ssd_bwd(xbc, dtraw, zs, y, htp, dossm, conv_w, conv_b, dtb, alog, dskx, gssm):
    S = xbc.shape[0]
    nc = S // CH
    tri, triu, e16 = _ssd_consts()

    def body(xbc_ref, halo_ref, dtraw_ref, zs_ref, y_ref, htp_ref, dos_ref,
             cw_ref, cb_ref, dtb_ref, alog_ref, dsk_ref, g_ref, tri_ref, triu_ref, e16_ref,
             dxbc_ref, ddt_ref, dzs_ref, dcw_ref, dcb_ref, dvec_ref, dg_ref,
             dht, ext, dext, dskacc):
        r = pl.program_id(0)
        i = nc - 1 - r

        @pl.when(r == 0)
        def _():
            dht[...] = jnp.zeros_like(dht)
            dext[CH:CH + 8, :] = jnp.zeros((8, CC), F32)
            dskacc[...] = jnp.zeros_like(dskacc)
            dcw_ref[...] = jnp.zeros_like(dcw_ref)
            dcb_ref[...] = jnp.zeros_like(dcb_ref)
            dvec_ref[...] = jnp.zeros_like(dvec_ref)
            dg_ref[...] = jnp.zeros_like(dg_ref)

        xc, sact, act, arow, dtpre, dt, cum, cumx, dtx, wins = _ssd_chunk_fwd_common(
            xbc_ref, halo_ref, dtraw_ref, cw_ref, cb_ref, dtb_ref, alog_ref, tri_ref, e16_ref, ext, i == 0)
        cum_t = cum.T
        xs = act[:, 0:SW]
        lastx = cumx[CH - 1:CH, :]
        xh = xs * dtx
        eexp = jnp.exp(cumx)
        dte = jnp.exp(lastx - cumx)
        cdx = jnp.exp(lastx)
        trim = tri_ref[...].astype(F32) > 0.5
        lane = lax.broadcasted_iota(jnp.int32, (CH, 128), 1)
        rowi = lax.broadcasted_iota(jnp.int32, (CH, 128), 0)

        yv = y_ref[...]
        z = zs_ref[...]
        sz = _sigmoid(z)
        silz = z * sz
        hf = yv * silz
        dn = dos_ref[...] * g_ref[...]
        dhf_parts, nrm_parts = [], []
        for g in range(SG):
            gl = slice(g * GW, (g + 1) * GW)
            hg = hf[:, gl]
            rs = lax.rsqrt(jnp.mean(hg * hg, axis=-1, keepdims=True) + RMS_EPS)
            ng = hg * rs
            dng = dn[:, gl]
            dhf_parts.append(rs * (dng - ng * jnp.mean(dng * ng, axis=-1, keepdims=True)))
            nrm_parts.append(ng)
        nrm = jnp.concatenate(nrm_parts, axis=1)
        dhf = jnp.concatenate(dhf_parts, axis=1)
        dg_ref[...] += jnp.sum(dos_ref[...] * nrm, axis=0, keepdims=True)
        dyv = dhf * silz
        dzs_ref[...] = (dhf * yv * (sz * (1.0 + z * (1.0 - sz)))).astype(BF16)
        dskacc[...] += jnp.sum(dyv * xs, axis=0, keepdims=True)
        dxs_skip = dyv * dsk_ref[...]

        dhtn = dht[...]
        hp = htp_ref[...]
        dlastx = jnp.sum(dhtn * hp, axis=0, keepdims=True) * cdx
        xb = xh.astype(BF16)
        xwf = xh * dte
        dcum = jnp.zeros((CH, 128), F32)
        dcum_t = jnp.zeros((128, CH), F32)
        dxh_parts, dcumx_parts, dlast_parts, db_parts, dc_parts = [], [], [], [], []
        for g in range(SG):
            gl = slice(g * GW, (g + 1) * GW)
            bg = act[:, SW + g * SN:SW + (g + 1) * SN].astype(BF16)
            cg = act[:, SW + SG * SN + g * SN:SW + SG * SN + (g + 1) * SN].astype(BF16)
            hpg = hp[:, gl].astype(BF16)
            dhn = dhtn[:, gl].astype(BF16)
            dyg = dyv[:, gl]
            dz = (dyg * eexp[:, gl]).astype(BF16)
            dcg = _nt(dz, hpg)
            dht[:, gl] = dhtn[:, gl] * cdx[:, gl] + _tn(cg, dz)
            yoff = eexp[:, gl] * _mm(cg, hpg)
            dcumx_g = dyg * yoff
            dbg = _nt(xwf[:, gl].astype(BF16), dhn)
            dxw = _mm(bg, dhn)
            ddte = dxw * xwf[:, gl]
            dcumx_parts.append(dcumx_g - ddte)
            dlast_parts.append(jnp.sum(ddte, axis=0, keepdims=True))
            dxh_g = dxw * dte[:, gl]
            cbm = _nt(cg, bg)
            dcb = jnp.zeros((CH, CH), F32)
            dxp_parts = []
            for pr in range(GW // 128):
                h0 = g * (SH // SG) + 2 * pr
                lo = g * GW + pr * 128
                xp = xb[:, lo:lo + 128]
                dyp = dyv[:, lo:lo + 128]
                dxp = jnp.zeros((CH, 128), F32)
                for idx, hh in enumerate((h0, h0 + 1)):
                    decay = jnp.where(trim, jnp.exp(cum[:, hh:hh + 1] - cum_t[hh:hh + 1, :]), 0.0)
                    mh = cbm * decay
                    keep = (lane < SP) if idx == 0 else (lane >= SP)
                    dym = jnp.where(keep, dyp, 0.0).astype(BF16)
                    dm = _nt(dym, xp)
                    dxp = dxp + _tn(mh.astype(BF16), dym)
                    gm = dm * mh
                    dcum = dcum + jnp.where(lane == hh, jnp.sum(gm, axis=1, keepdims=True), 0.0)
                    dcum_t = dcum_t - jnp.where(rowi == hh, jnp.sum(gm, axis=0, keepdims=True), 0.0)
                    dcb = dcb + dm * decay
                dxp_parts.append(dxp)
            dxh_parts.append(dxh_g + jnp.concatenate(dxp_parts, axis=1))
            dcbb = dcb.astype(BF16)
            dc_parts.append(dcg + _mm(dcbb, bg))
            db_parts.append(dbg + _tn(dcbb, cg))
        dxh = jnp.concatenate(dxh_parts, axis=1)
        dcumx = jnp.concatenate(dcumx_parts, axis=1)
        dlastx = dlastx + jnp.concatenate(dlast_parts, axis=1)
        e16 = e16_ref[...]
        dlast128 = _nt_x(jnp.broadcast_to(dlastx, (8, SW)), e16, 2)[0:1, :]
        dcum = dcum + dcum_t.T + _nt_x(dcumx, e16, 2) + jnp.where(rowi == CH - 1, dlast128, 0.0)
        da = _xmm(triu_ref[...], dcum, 2)
        ddt = da * arow + _nt_x(dxh * xs, e16, 2)
        dvec_ref[1:2, :] += jnp.sum(da * dt, axis=0, keepdims=True)
        ddtraw = jnp.where(lane < SH, ddt * _sigmoid(dtpre), 0.0)
        dvec_ref[0:1, :] += jnp.sum(ddtraw, axis=0, keepdims=True)
        ddt_ref[...] = ddtraw.astype(BF16)
        dxs = dxs_skip + dxh * dtx
        dact = jnp.concatenate([dxs] + db_parts + dc_parts, axis=1)
        dxc = dact * (sact * (1.0 + xc * (1.0 - sact)))

        dcb_ref[...] += jnp.sum(dxc, axis=0, keepdims=True)
        for kk in range(CW):
            dcw_ref[kk:kk + 1, :] += jnp.sum(dxc * wins[kk], axis=0, keepdims=True)
        dext[0:CH, :] = dxc
        cw = cw_ref[...]
        dxr = cw[CW - 1:CW, :] * dxc
        for kk in range(CW - 1):
            dxr = dxr + cw[kk:kk + 1, :] * dext[CW - 1 - kk:CW - 1 - kk + CH, :]
        dxbc_ref[...] = dxr.astype(BF16)
        dext[CH:CH + 8, :] = dxc[0:8, :]

        @pl.when(r == nc - 1)
        def _():
            dvec_ref[1:2, :] = dvec_ref[1:2, :] * arow
            dvec_ref[2:3, :] = _nt_x(jnp.broadcast_to(dskacc[...], (8, SW)), e16, 3)[0:1, :]

    rev = lambda n: pl.BlockSpec((CH, n), lambda r: (nc - 1 - r, 0))
    return pl.pallas_call(
        body, name="ssd_bwd", grid=(nc,),
        in_specs=[rev(CC), pl.BlockSpec((8, CC), lambda r: (jnp.maximum((nc - 1 - r) * (CH // 8) - 1, 0), 0)),
                  rev(128), rev(SW), rev(SW), pl.BlockSpec((None, SN, SW), lambda r: (nc - 1 - r, 0, 0)), rev(SW),
                  _const((CW, CC)), _const((1, CC)), _const((1, 128)), _const((1, 128)), _const((1, SW)), _const((1, SW)),
                  _const((CH, CH)), _const((CH, CH)), _const((128, SW))],
        out_specs=[rev(CC), rev(128), rev(SW), _full((CW, CC)), _full((1, CC)), _full((8, 128)), _full((1, SW))],
        out_shape=[jax.ShapeDtypeStruct((S, CC), BF16), jax.ShapeDtypeStruct((S, 128), BF16), jax.ShapeDtypeStruct((S, SW), BF16),
                   jax.ShapeDtypeStruct((CW, CC), F32), jax.ShapeDtypeStruct((1, CC), F32),
                   jax.ShapeDtypeStruct((8, 128), F32), jax.ShapeDtypeStruct((1, SW), F32)],
        scratch_shapes=[pltpu.VMEM((SN, SW), F32), pltpu.VMEM((8 + CH, CC), F32), pltpu.VMEM((CH + 8, CC), F32),
                        pltpu.VMEM((1, SW), F32)],
        compiler_params=_cp(("arbitrary",)),
    )(xbc, xbc, dtraw, zs, y, htp, dossm, conv_w, conv_b, dtb, alog, dskx, gssm, tri, triu, e16)


def _mla_bwd(dq, dk, dv, qlat, ckv, qg, kvg, wq, wk, wv, pos, invf):
    S = qlat.shape[0]
    tm = min(TQ, S)

    def body(dq_ref, dk_ref, dv_ref, ql_ref, ckv_ref, qg_ref, kvg_ref, wq_ref, wk_ref, wv_ref, pos_ref, invf_ref,
             dql_ref, dckv_ref, dkr_ref, dwq_ref, dwk_ref, dwv_ref, dqg_ref, dkvg_ref):
        i = pl.program_id(0)

        @pl.when(i == 0)
        def _():
            dwq_ref[...] = jnp.zeros_like(dwq_ref)
            dwk_ref[...] = jnp.zeros_like(dwk_ref)
            dwv_ref[...] = jnp.zeros_like(dwv_ref)
            dqg_ref[...] = jnp.zeros_like(dqg_ref)
            dkvg_ref[...] = jnp.zeros_like(dkvg_ref)

        ang = pos_ref[...].astype(F32) * invf_ref[...]
        cs = jnp.cos(ang)
        sn = jnp.sin(ang)

        def rms_bwd(v, g, dn, dg_ref):
            r = lax.rsqrt(jnp.mean(v * v, axis=-1, keepdims=True) + RMS_EPS)
            vh = v * r
            dg_ref[...] += jnp.sum(dn * vh, axis=0, keepdims=True)
            dvh = dn * g
            return vh, r * (dvh - vh * jnp.mean(dvh * vh, axis=-1, keepdims=True))

        pieces = []
        for h in range(NH):
            dqh = dq_ref[h]
            pieces.append(dqh[:, 0:NOPE] * SCALE)
            pieces.append(_rope_t(dqh[:, NOPE:HP], cs, sn) * SCALE)
        dqf = jnp.concatenate(pieces, axis=1).astype(BF16)
        ql = ql_ref[...]
        g = qg_ref[...]
        dqn = _nt(dqf, wq_ref[...])
        qh, dql = rms_bwd(ql, g, dqn, dqg_ref)
        dwq_ref[...] += _tn((qh * g).astype(BF16), dqf)
        dql_ref[...] = dql.astype(BF16)

        dkn_p = jnp.concatenate([dk_ref[h, :, 0:NOPE] for h in range(NH)], axis=1).astype(BF16)
        dvf = jnp.concatenate([dv_ref[h] for h in range(NH)], axis=1).astype(BF16)
        dkr = dk_ref[0, :, NOPE:HP]
        for h in range(1, NH):
            dkr = dkr + dk_ref[h, :, NOPE:HP]
        lane = lax.broadcasted_iota(jnp.int32, dkr.shape, 1)
        dkr_ref[...] = jnp.where(lane < ROPE, _rope_t(dkr, cs, sn), 0.0).astype(BF16)
        cv = ckv_ref[...]
        gk = kvg_ref[...]
        dkn = _nt(dkn_p, wk_ref[...]) + _nt(dvf, wv_ref[...])
        kh, dckv = rms_bwd(cv, gk, dkn, dkvg_ref)
        knb = (kh * gk).astype(BF16)
        dwk_ref[...] += _tn(knb, dkn_p)
        dwv_ref[...] += _tn(knb, dvf)
        dckv_ref[...] = dckv.astype(BF16)

    row = lambda n: pl.BlockSpec((tm, n), lambda i: (i, 0))
    heads = lambda n: pl.BlockSpec((NH, tm, n), lambda i: (0, i, 0))
    return pl.pallas_call(
        body, name="mla_bwd", grid=(S // tm,),
        in_specs=[heads(HP), heads(HP), heads(VD), row(QL), row(KVL), _const((1, QL)), _const((1, KVL)),
                  _const((QL, NH * HP)), _const((KVL, NH * NOPE)), _const((KVL, NH * VD)), row(1), _const((1, 128))],
        out_specs=[row(QL), row(KVL), row(128), _full((QL, NH * HP)), _full((KVL, NH * NOPE)), _full((KVL, NH * VD)),
                   _full((1, QL)), _full((1, KVL))],
        out_shape=[jax.ShapeDtypeStruct((S, QL), BF16), jax.ShapeDtypeStruct((S, KVL), BF16), jax.ShapeDtypeStruct((S, 128), BF16),
                   jax.ShapeDtypeStruct((QL, NH * HP), F32), jax.ShapeDtypeStruct((KVL, NH * NOPE), F32),
                   jax.ShapeDtypeStruct((KVL, NH * VD), F32), jax.ShapeDtypeStruct((1, QL), F32), jax.ShapeDtypeStruct((1, KVL), F32)],
        compiler_params=_cp(("arbitrary",)),
    )(dq, dk, dv, qlat, ckv, qg, kvg, wq, wk, wv, pos, invf)


def _inproj_bwd(x, gx1, mod, win, dql, dckv, dza, dxbc, dzs, dkr, ddt):
    S = x.shape[0]
    tm = min(TM, S)

    def body(x_ref, gx1_ref, mod_ref, win_ref, dql_ref, dckv_ref, dza_ref, dxbc_ref, dzs_ref, dkr_ref, ddt_ref,
             gx_ref, dw_ref, vec_ref):
        i = pl.program_id(0)

        @pl.when(i == 0)
        def _():
            dw_ref[...] = jnp.zeros_like(dw_ref)
            vec_ref[...] = jnp.zeros_like(vec_ref)

        shift = mod_ref[0:1, 0:D]
        scale = mod_ref[0:1, D:2 * D]
        xv = x_ref[...]
        ut = (xv * (1.0 + scale) + shift).T.astype(BF16)
        pieces = (dql_ref, dckv_ref, dza_ref, dxbc_ref, dzs_ref, dkr_ref, ddt_ref)
        du = jnp.zeros((tm, D), F32)
        lo = 0
        for p_ref in pieces:
            n = p_ref.shape[1]
            dp = p_ref[...]
            du = du + _nt(dp, win_ref[:, lo:lo + n])
            dw_ref[:, lo:lo + n] += _mm(ut, dp)
            lo += n
        vec_ref[0:1, :] += jnp.sum(du, axis=0, keepdims=True)
        vec_ref[1:2, :] += jnp.sum(du * xv, axis=0, keepdims=True)
        gx_ref[...] = gx1_ref[...] + du * (1.0 + scale)

    row = lambda n: pl.BlockSpec((tm, n), lambda i: (i, 0))
    return pl.pallas_call(
        body, name="inproj_bwd", grid=(S // tm,),
        in_specs=[row(D), row(D), _const((8, 3 * D)), _const((D, IN_P)), row(QL), row(KVL), row(D), row(CC), row(D),
                  row(128), row(128)],
        out_specs=[row(D), pl.BlockSpec((D, IN_P), lambda i: (0, 0), pipeline_mode=pl.Buffered(1)), _full((8, D))],
        out_shape=[jax.ShapeDtypeStruct((S, D), F32), jax.ShapeDtypeStruct((D, IN_P), F32), jax.ShapeDtypeStruct((8, D), F32)],
        compiler_params=_cp(("arbitrary",)),
    )(x, gx1, mod, win, dql, dckv, dza, dxbc, dzs, dkr, ddt)


def _ada_bwd(callt, dmods):
    w = dmods.shape[1]

    def body(c_ref, d_ref, o_ref):
        acc = c_ref[:, 0:1] * d_ref[0:1, :]
        for s in range(1, 8):
            acc = acc + c_ref[:, s:s + 1] * d_ref[s:s + 1, :]
        o_ref[0] = acc

    return pl.pallas_call(body, name="ada_bwd", out_shape=jax.ShapeDtypeStruct((1, D, w), F32),
                          compiler_params=_cp())(callt, dmods)


def _adamw(name, parts, w, m, v):
    rows, ncol = w.shape
    tr = min(rows, 128)
    nparts = parts.shape[0]

    def body(p_ref, w_ref, m_ref, v_ref, g_ref, d_ref, nm_ref, nv_ref):
        g = p_ref[0].astype(F32)
        for s in range(1, nparts):
            g = g + p_ref[s].astype(F32)
        g_ref[...] = g
        nm = B1 * m_ref[...] + (1.0 - B1) * g
        nv = B2 * v_ref[...] + (1.0 - B2) * (g * g)
        nm_ref[...] = nm
        nv_ref[...] = nv
        m_hat = nm / (1.0 - B1 ** STEP)
        v_hat = nv / (1.0 - B2 ** STEP)
        d_ref[...] = -LR * (m_hat / (jnp.sqrt(v_hat) + EPS) + WD * w_ref[...])

    row = pl.BlockSpec((tr, ncol), lambda i: (i, 0))
    sd = jax.ShapeDtypeStruct((rows, ncol), F32)
    return pl.pallas_call(
        body, name="adamw_" + name, grid=(rows // tr,),
        in_specs=[pl.BlockSpec((nparts, tr, ncol), lambda i: (0, i, 0)), row, row, row],
        out_specs=[row, row, row, row], out_shape=[sd, sd, sd, sd],
        compiler_params=_cp(("arbitrary",)),
    )(parts, w, m, v)


_SMALL = (("b_ada", 3 * D), ("conv_w", CW * CC // 4), ("conv_b", CC), ("ssm_norm_g", SW), ("ln_g", D), ("ln_b", D),
          ("q_norm_g", QL), ("kv_norm_g", KVL), ("dt_bias", SH), ("a_log", SH), ("d_skip", SH), ("loss", 128))


def _pack_small(d, lead):
    flat = [d[name].reshape(d[name].shape[:lead] + (-1,)) for name, _ in _SMALL]
    used = sum(f.shape[lead] for f in flat)
    pad = jnp.zeros(flat[0].shape[:lead] + (R_SMALL * 1024 - used,), F32)
    return jnp.concatenate(flat + [pad], axis=lead).reshape(flat[0].shape[:lead] + (R_SMALL, 1024))


def _unpack_small(p):
    flat = p.reshape(-1)
    out, r = {}, 0
    for name, n in _SMALL:
        out[name] = flat[r:r + n]
        r += n
    return out


def _in_to_padded(w):
    z = lambda n: jnp.zeros((w.shape[0], n), w.dtype)
    return jnp.concatenate([w[:, 0:384], w[:, 384:640], w[:, 704:1728], w[:, 1728:3264], w[:, 3280:4304],
                            w[:, 640:704], z(64), w[:, 3264:3280], z(112)], axis=1)


def _in_from_padded(g):
    return jnp.concatenate([g[:, 0:384], g[:, 384:640], g[:, P_KR[0]:P_KR[0] + 64], g[:, 640:1664], g[:, 1664:3200],
                            g[:, P_DT[0]:P_DT[0] + 16], g[:, 3200:4224]], axis=1)


def kernel(x, c, positions, w_ada, b_ada, w_in, q_norm_g, w_qb, kv_norm_g, w_kvb, conv_w, conv_b, dt_bias, a_log, d_skip, ssm_norm_g, w_out, ln_g, ln_b, loss_target, m_w_ada, m_b_ada, m_w_in, m_q_norm_g, m_w_qb, m_kv_norm_g, m_w_kvb, m_conv_w, m_conv_b, m_dt_bias, m_a_log, m_d_skip, m_ssm_norm_g, m_w_out, m_ln_g, m_ln_b, v_w_ada, v_b_ada, v_w_in, v_q_norm_g, v_w_qb, v_kv_norm_g, v_w_kvb, v_conv_w, v_conv_b, v_dt_bias, v_a_log, v_d_skip, v_ssm_norm_g, v_w_out, v_ln_g, v_ln_b):
    S = x.shape[1]
    xv = x[0]
    tgt = loss_target[0]

    cw16 = jnp.concatenate([conv_w[0], jnp.zeros((16 - CW, CC // 4), F32)], axis=0)
    f_in, f_qb, f_kvb, f_out, f_cw = _gather_weights(
        [w_in[0].astype(BF16), w_qb[0].astype(BF16), w_kvb[0].astype(BF16), w_out[0].astype(BF16), cw16])
    cat1 = lambda f: f.transpose(1, 0, 2).reshape(f.shape[1], 4 * f.shape[2])
    win = _in_to_padded(cat1(f_in))
    wqb = cat1(f_qb).reshape(QL, NH, QKD)
    wq = jnp.concatenate([wqb, jnp.zeros((QL, NH, HP - QKD), BF16)], axis=2).reshape(QL, NH * HP)
    wkvb = cat1(f_kvb).reshape(KVL, NH, NOPE + VD)
    wk = wkvb[:, :, 0:NOPE].reshape(KVL, NH * NOPE)
    wv = wkvb[:, :, NOPE:].reshape(KVL, NH * VD)
    wout = f_out.reshape(MIX, D)
    cwf = cat1(f_cw[:, 0:CW, :])

    half = ROPE // 2
    invf = 1.0 / (ROPE_THETA ** (jnp.arange(half, dtype=F32) / half))
    invf = jnp.concatenate([invf, invf, jnp.zeros((128 - ROPE,), F32)]).reshape(1, 128)
    pos = positions.reshape(S, 1)
    pad128 = lambda a: jnp.concatenate([a.reshape(1, SH), jnp.zeros((1, 128 - SH), F32)], axis=1)
    dtb, alog = pad128(dt_bias), pad128(a_log)
    dskx = jnp.repeat(d_skip.reshape(SH), SP).reshape(1, SW)

    my_c = lax.axis_index("c")
    (call,) = _exchange("gather_c", [jnp.broadcast_to(c.reshape(1, 1, D), (4, 1, D))])
    call = call.reshape(8, D)
    mods = _ada(call, w_ada[0])
    (mrows,) = _exchange("scatter_mod", [mods.reshape(4, 2, 3 * D // 4)])
    mine = lax.dynamic_index_in_dim(mrows.reshape(4, 2, 2, 3 * D // 4)[:, 0], my_c, axis=1, keepdims=False)
    mod = jnp.broadcast_to(mine.reshape(1, 3 * D) + b_ada, (8, 3 * D))
    qlat, ckv, za, xbc, zs, dtraw, q, k, v, kt, vt = _inproj(xv, mod, win, q_norm_g, kv_norm_g, wq, wk, wv, wk.T, wv.T, pos, invf)
    o, lse = _attn_fwd(q, k, vt)
    y, htp, ossm = _ssd_fwd(xbc, dtraw, zs, cwf, conv_b, dtb, alog, dskx, ssm_norm_g)
    gx1, do, dza, dossm, delta, dwout, vec_o = _outproj(o, za, ossm, xv, tgt, wout, mod, ln_g, ln_b)
    loss_part = jnp.zeros((128,), F32).at[0].set(0.5 / D * jnp.sum(vec_o[0]))

    dk, dv, dq = _attn_bwd(q, k, kt, v, do, lse, delta)
    dxbc, ddt, dzs, dcw, dcb, dvec, dgssm = _ssd_bwd(xbc, dtraw, zs, y, htp, dossm, cwf, conv_b, dtb, alog, dskx, ssm_norm_g)
    dql, dckv, dkr, dwq, dwk, dwv, dqg, dkvg = _mla_bwd(dq, dk, dv, qlat, ckv, q_norm_g, kv_norm_g, wq, wk, wv, pos, invf)
    gx, dwin, vec_i = _inproj_bwd(xv, gx1, mod, win, dql, dckv, dza, dxbc, dzs, dkr, ddt)
    dmod = jnp.concatenate([vec_i[0:1], vec_i[1:2], vec_o[3:4]], axis=1)

    cols = lambda g: g.reshape(g.shape[0], 4, g.shape[1] // 4).transpose(1, 0, 2)
    g_in = cols(_in_from_padded(dwin)).astype(BF16)
    g_qb = cols(dwq.reshape(QL, NH, HP)[:, :, 0:QKD].reshape(QL, NH * QKD)).astype(BF16)
    g_kvb = cols(jnp.concatenate([dwk.reshape(KVL, NH, NOPE), dwv.reshape(KVL, NH, VD)], axis=2)
                 .reshape(KVL, NH * (NOPE + VD))).astype(BF16)
    g_out = dwout.reshape(4, MIX // 4, D).astype(BF16)
    small = {"b_ada": dmod, "conv_b": dcb, "ssm_norm_g": dgssm, "ln_g": vec_o[1:2], "ln_b": vec_o[2:3],
             "q_norm_g": dqg, "kv_norm_g": dkvg, "dt_bias": dvec[0:1, 0:SH], "a_log": dvec[1:2, 0:SH], "d_skip": dvec[2:3, 0:SH],
             "loss": loss_part}
    small = {n: jnp.broadcast_to(a.reshape(1, -1), (4, a.size)) for n, a in small.items()}
    small["conv_w"] = cols(dcw).reshape(4, CW * CC // 4)
    gsmall = _pack_small(small, 1)

    r_in, r_qb, r_kvb, r_out, rs, dmods = _exchange(
        "exchange_grads", [g_in, g_qb, g_kvb, g_out, gsmall, jnp.broadcast_to(dmod.reshape(1, 1, 3 * D), (4, 1, 3 * D))])
    chip = 2 * lax.axis_index("x") + lax.axis_index("y")
    dmods = lax.dynamic_slice_in_dim(dmods.reshape(8, 3 * D), chip * (3 * D // 4), 3 * D // 4, axis=1)
    g_ada = _ada_bwd(call.T, dmods)
    res = dict(w_ada=_adamw("w_ada", g_ada, w_ada[0], m_w_ada[0], v_w_ada[0]),
               w_in=_adamw("w_in", r_in, w_in[0], m_w_in[0], v_w_in[0]),
               w_qb=_adamw("w_qb", r_qb, w_qb[0], m_w_qb[0], v_w_qb[0]),
               w_kvb=_adamw("w_kvb", r_kvb, w_kvb[0], m_w_kvb[0], v_w_kvb[0]),
               w_out=_adamw("w_out", r_out, w_out[0], m_w_out[0], v_w_out[0]))
    wsm = _pack_small(dict(b_ada=b_ada, conv_w=conv_w, conv_b=conv_b, ssm_norm_g=ssm_norm_g, ln_g=ln_g, ln_b=ln_b,
                           q_norm_g=q_norm_g, kv_norm_g=kv_norm_g, dt_bias=dt_bias, a_log=a_log, d_skip=d_skip, loss=jnp.zeros((128,), F32)), 0)
    msm = _pack_small(dict(b_ada=m_b_ada, conv_w=m_conv_w, conv_b=m_conv_b, ssm_norm_g=m_ssm_norm_g, ln_g=m_ln_g, ln_b=m_ln_b,
                           q_norm_g=m_q_norm_g, kv_norm_g=m_kv_norm_g, dt_bias=m_dt_bias, a_log=m_a_log, d_skip=m_d_skip, loss=jnp.zeros((128,), F32)), 0)
    vsm = _pack_small(dict(b_ada=v_b_ada, conv_w=v_conv_w, conv_b=v_conv_b, ssm_norm_g=v_ssm_norm_g, ln_g=v_ln_g, ln_b=v_ln_b,
                           q_norm_g=v_q_norm_g, kv_norm_g=v_kv_norm_g, dt_bias=v_dt_bias, a_log=v_a_log, d_skip=v_d_skip, loss=jnp.zeros((128,), F32)), 0)
    sm = _adamw("small", rs, wsm, msm, vsm)

    order = ["w_ada", "b_ada", "w_in", "q_norm_g", "w_qb", "kv_norm_g", "w_kvb", "conv_w", "conv_b", "dt_bias", "a_log",
             "d_skip", "ssm_norm_g", "w_out", "ln_g", "ln_b"]
    shapes = dict(w_ada=w_ada.shape, b_ada=b_ada.shape, w_in=w_in.shape, q_norm_g=q_norm_g.shape, w_qb=w_qb.shape,
                  kv_norm_g=kv_norm_g.shape, w_kvb=w_kvb.shape, conv_w=conv_w.shape, conv_b=conv_b.shape, dt_bias=dt_bias.shape,
                  a_log=a_log.shape, d_skip=d_skip.shape, ssm_norm_g=ssm_norm_g.shape, w_out=w_out.shape, ln_g=ln_g.shape,
                  ln_b=ln_b.shape)
    outs = []
    for kind in range(4):
        d = _unpack_small(sm[kind])
        d.update({n: r[kind] for n, r in res.items()})
        outs.extend(d[n].reshape(shapes[n]) for n in order)
    loss = _unpack_small(sm[0])["loss"][0]
    return (loss, gx.reshape(x.shape), *outs)
```

```python
import functools
import math

import numpy as np
import jax
import jax.numpy as jnp
from jax import lax
from jax.experimental import pallas as pl
from jax.experimental.pallas import tpu as pltpu

F32 = jnp.float32
BF16 = jnp.bfloat16
MESH_ID = pl.DeviceIdType.MESH

D = 1024
NH = 8
NOPE = 128
ROPE = 64
VD = 128
VDP = 144
QKD = NOPE + ROPE
HP = 256
QL = 384
KVL = 256
ROPE_THETA = 10000.0
SH = 16
SP = 64
SG = 2
SN = 128
CW = 4
CH = 128
SW = SH * SP
CC = SW + 2 * SG * SN
GW = SW // SG
MIX = 2 * D
IN_W = 4304
ALPHA = 2.0 ** 0.25
RMS_EPS = 1e-6
LN_EPS = 1e-5
SCALE = QKD ** -0.5
LN2 = math.log(2.0)
QSCALE = SCALE / LN2
LR, B1, B2, EPS, WD, STEP = 0.001, 0.9, 0.999, 1e-08, 0.01, 10

P_Q = (0, 384)
P_KV = (384, 640)
P_ZA = (640, 1664)
P_XBC = (1664, 3200)
P_ZS = (3200, 4224)
P_KR = (4224, 4352)
P_DT = (4352, 4480)
IN_P = 4480

R_SMALL = 16

TM = 256
TQ = 512
NSP = 2
TQF = 1024
VMEM_LIMIT = 56 * 1024 * 1024


def _cp(sem=None):
    return pltpu.CompilerParams(dimension_semantics=sem, vmem_limit_bytes=VMEM_LIMIT)


def _mm(a, b):
    return jnp.dot(a, b, preferred_element_type=F32)


def _nt(a, b):
    return lax.dot_general(a, b, (((1,), (1,)), ((), ())), preferred_element_type=F32)


def _tn(a, b):
    return lax.dot_general(a, b, (((0,), (0,)), ((), ())), preferred_element_type=F32)


def _split(a, terms):
    parts = []
    for t in range(terms):
        p = a.astype(BF16)
        parts.append(p)
        if t + 1 < terms:
            a = a - p.astype(F32)
    return parts


def _mm_x(a, ones, terms):
    parts = _split(a, terms)
    out = _mm(parts[0], ones)
    for p in parts[1:]:
        out = out + _mm(p, ones)
    return out


def _xmm(ones, a, terms):
    parts = _split(a, terms)
    out = _mm(ones, parts[0])
    for p in parts[1:]:
        out = out + _mm(ones, p)
    return out


def _nt_x(a, ones, terms):
    parts = _split(a, terms)
    out = _nt(parts[0], ones)
    for p in parts[1:]:
        out = out + _nt(p, ones)
    return out


def _sigmoid(z):
    return 1.0 / (1.0 + jnp.exp(-z))


def _softplus(z):
    return jnp.maximum(z, 0.0) + jnp.log1p(jnp.exp(-jnp.abs(z)))


def _rope(t, cs, sn):
    lane = lax.broadcasted_iota(jnp.int32, t.shape, 1)
    rot = jnp.where(lane < ROPE // 2, -pltpu.roll(t, 128 - ROPE // 2, 1), pltpu.roll(t, ROPE // 2, 1))
    return t * cs + rot * sn


def _rope_t(t, cs, sn):
    lane = lax.broadcasted_iota(jnp.int32, t.shape, 1)
    y = t * sn
    rot = jnp.where(lane < ROPE // 2, -pltpu.roll(y, 128 - ROPE // 2, 1), pltpu.roll(y, ROPE // 2, 1))
    return t * cs - rot


def _full(shape):
    n = len(shape)
    return pl.BlockSpec(shape, lambda *_: (0,) * n)


def _const(shape):
    n = len(shape)
    return pl.BlockSpec(shape, lambda *_: (0,) * n, pipeline_mode=pl.Buffered(1))


def _gather_weights(shards):
    n = len(shards)
    halves = [a.shape[0] // 2 for a in shards]

    def body(*refs):
        srcs, dsts = refs[:n], refs[n:2 * n]
        send_sems, recv_sems, local_sems = refs[2 * n:]
        x, y, c = lax.axis_index("x"), lax.axis_index("y"), lax.axis_index("c")
        me = 2 * x + y
        sibling = (x, y, 1 - c)
        chips = [(1 - x, y), (x, 1 - y), (1 - x, 1 - y)]

        def rows(a, pc):
            return pl.ds(pl.multiple_of(pc * halves[a], halves[a]), halves[a])

        def copy(a, k, src, slot, pc, to):
            return pltpu.make_async_remote_copy(
                src_ref=src, dst_ref=dsts[a].at[slot, rows(a, pc)], send_sem=send_sems.at[a, k],
                recv_sem=recv_sems.at[a, k], device_id=to, device_id_type=MESH_ID)

        local = [pltpu.make_async_copy(srcs[a], dsts[a].at[me], local_sems.at[a]) for a in range(n)]
        for cp in local:
            cp.start()
        sends = [copy(a, j, srcs[a].at[rows(a, c)], me, c, (px, py, c)) for a in range(n) for j, (px, py) in enumerate(chips)]
        for cp in sends:
            cp.start()
        passed = []
        for a in range(n):
            for j, (px, py) in enumerate(chips):
                k = 2 * px + py
                copy(a, j, srcs[a].at[rows(a, c)], k, c, (x, y, c)).wait_recv()
                fwd = copy(a, 3 + j, dsts[a].at[k, rows(a, c)], k, c, sibling)
                fwd.start()
                passed.append(fwd)
        for a in range(n):
            for j, (px, py) in enumerate(chips):
                copy(a, 3 + j, srcs[a].at[rows(a, c)], 2 * px + py, 1 - c, (x, y, c)).wait_recv()
        for cp in sends + passed:
            cp.wait_send()
        for cp in local:
            cp.wait()

    hbm = pl.BlockSpec(memory_space=pltpu.HBM)
    return pl.pallas_call(
        body, name="gather_weights",
        out_shape=tuple(jax.ShapeDtypeStruct((4,) + a.shape, a.dtype) for a in shards),
        in_specs=[hbm] * n, out_specs=tuple([hbm] * n),
        scratch_shapes=[pltpu.SemaphoreType.DMA((n, 6)), pltpu.SemaphoreType.DMA((n, 6)), pltpu.SemaphoreType.DMA((n,))],
    )(*shards)


def _exchange(name, slabs):
    n = len(slabs)

    def body(*refs):
        srcs, dsts = refs[:n], refs[n:2 * n]
        send_sems, recv_sems, local_sems = refs[2 * n:]
        x, y, c = lax.axis_index("x"), lax.axis_index("y"), lax.axis_index("c")
        chip = 2 * x + y
        sibling = (x, y, 1 - c)
        chips = [(1 - x, y), (x, 1 - y), (1 - x, 1 - y)]

        def slot(px, py, pc):
            return 4 * px + 2 * py + pc

        def copy(a, k, src, s, to):
            return pltpu.make_async_remote_copy(
                src_ref=src, dst_ref=dsts[a].at[s], send_sem=send_sems.at[a, k], recv_sem=recv_sems.at[a, k],
                device_id=to, device_id_type=MESH_ID)

        mine = slot(x, y, c)
        local = [pltpu.make_async_copy(srcs[a].at[chip], dsts[a].at[mine], local_sems.at[a]) for a in range(n)]
        for cp in local:
            cp.start()
        first = []
        for a in range(n):
            first.append(copy(a, 0, srcs[a].at[chip], mine, sibling))
            for j, (px, py) in enumerate(chips):
                first.append(copy(a, 1 + j, srcs[a].at[2 * px + py], mine, (px, py, c)))
        for cp in first:
            cp.start()
        passed = []
        for a in range(n):
            for j, (px, py) in enumerate(chips):
                s = slot(px, py, c)
                copy(a, 1 + j, srcs[a].at[chip], s, (x, y, c)).wait_recv()
                fwd = copy(a, 4 + j, dsts[a].at[s], s, sibling)
                fwd.start()
                passed.append(fwd)
        for a in range(n):
            copy(a, 0, srcs[a].at[chip], slot(x, y, 1 - c), (x, y, c)).wait_recv()
            for j, (px, py) in enumerate(chips):
                copy(a, 4 + j, srcs[a].at[chip], slot(px, py, 1 - c), (x, y, c)).wait_recv()
        for cp in first + passed:
            cp.wait_send()
        for cp in local:
            cp.wait()

    hbm = pl.BlockSpec(memory_space=pltpu.HBM)
    return pl.pallas_call(
        body, name=name,
        out_shape=tuple(jax.ShapeDtypeStruct((8,) + a.shape[1:], a.dtype) for a in slabs),
        in_specs=[hbm] * n, out_specs=tuple([hbm] * n),
        scratch_shapes=[pltpu.SemaphoreType.DMA((n, 7)), pltpu.SemaphoreType.DMA((n, 7)), pltpu.SemaphoreType.DMA((n,))],
    )(*slabs)


def _ada(call, w_shard):
    def body(c_ref, w_ref, o_ref):
        o_ref[...] = _mm(c_ref[...].astype(BF16), w_ref[...].astype(BF16))

    return pl.pallas_call(body, name="ada", out_shape=jax.ShapeDtypeStruct((8, w_shard.shape[1]), F32),
                          compiler_params=_cp())(call, w_shard)


def _inproj(x, mod, win, qg, kvg, wq, wk, wv, wkt, wvt, pos, invf):
    S = x.shape[0]
    tm = min(TM, S)

    def body(x_ref, mod_ref, win_ref, qg_ref, kvg_ref, wq_ref, wk_ref, wv_ref, wkt_ref, wvt_ref, pos_ref, invf_ref,
             qlat_ref, ckv_ref, za_ref, xbc_ref, zs_ref, dt_ref, q_ref, k_ref, v_ref, kt_ref, vt_ref):
        shift = mod_ref[0:1, 0:D]
        scale = mod_ref[0:1, D:2 * D]
        u = (x_ref[...] * (1.0 + scale) + shift).astype(BF16)

        def proj(p):
            return _mm(u, win_ref[:, p[0]:p[1]])

        ql = proj(P_Q)
        ckv = proj(P_KV)
        qlat_ref[...] = ql
        ckv_ref[...] = ckv
        za_ref[...] = proj(P_ZA)
        xbc_ref[...] = proj(P_XBC)
        zs_ref[...] = proj(P_ZS)
        dt_ref[...] = proj(P_DT)
        kr = proj(P_KR)

        ang = pos_ref[...].astype(F32) * invf_ref[...]
        cs = jnp.cos(ang)
        sn = jnp.sin(ang)

        rq = lax.rsqrt(jnp.mean(ql * ql, axis=-1, keepdims=True) + RMS_EPS)
        qn = (ql * rq * qg_ref[...]).astype(BF16)
        for h in range(NH):
            qh = _mm(qn, wq_ref[:, h * HP:(h + 1) * HP])
            q_ref[h, :, 0:NOPE] = (qh[:, 0:NOPE] * QSCALE).astype(BF16)
            q_ref[h, :, NOPE:HP] = (_rope(qh[:, NOPE:HP], cs, sn) * QSCALE).astype(BF16)

        rk = lax.rsqrt(jnp.mean(ckv * ckv, axis=-1, keepdims=True) + RMS_EPS)
        kn = (ckv * rk * kvg_ref[...]).astype(BF16)
        knope = _mm(kn, wk_ref[...])
        vall = _mm(kn, wv_ref[...])
        krf = _rope(kr, cs, sn)
        krr = krf.astype(BF16)
        krt = krf.T.astype(BF16)
        ones_rows = jnp.where(lax.broadcasted_iota(jnp.int32, (VDP - VD, tm), 0) == 0, 1.0, 0.0).astype(BF16)
        for h in range(NH):
            k_ref[h, :, 0:NOPE] = knope[:, h * NOPE:(h + 1) * NOPE].astype(BF16)
            k_ref[h, :, NOPE:HP] = krr
            v_ref[h] = vall[:, h * VD:(h + 1) * VD].astype(BF16)
            kt_ref[h, 0:NOPE, :] = _nt(wkt_ref[h * NOPE:(h + 1) * NOPE, :], kn).astype(BF16)
            kt_ref[h, NOPE:HP, :] = krt
            vt_ref[h, 0:VD, :] = _nt(wvt_ref[h * VD:(h + 1) * VD, :], kn).astype(BF16)
            vt_ref[h, VD:VDP, :] = ones_rows

    row = lambda n: pl.BlockSpec((tm, n), lambda i: (i, 0))
    heads = lambda n: pl.BlockSpec((NH, tm, n), lambda i: (0, i, 0))
    heads_t = lambda n: pl.BlockSpec((NH, None, n, tm), lambda i: (0, i, 0, 0))
    sd = lambda n: jax.ShapeDtypeStruct((S, n), F32)
    hd = lambda n: jax.ShapeDtypeStruct((NH, S, n), BF16)
    ht = lambda n: jax.ShapeDtypeStruct((NH, S // tm, n, tm), BF16)
    return pl.pallas_call(
        body, name="inproj", grid=(S // tm,),
        in_specs=[row(D), _const((8, 3 * D)), _const((D, IN_P)), _const((1, QL)), _const((1, KVL)),
                  _const((QL, NH * HP)), _const((KVL, NH * NOPE)), _const((KVL, NH * VD)),
                  _const((NH * NOPE, KVL)), _const((NH * VD, KVL)), row(1), _const((1, 128))],
        out_specs=[row(QL), row(KVL), row(D), row(CC), row(D), row(128), heads(HP), heads(HP), heads(VD),
                   heads_t(HP), heads_t(VDP)],
        out_shape=[sd(QL), sd(KVL), sd(D), sd(CC), sd(D), sd(128), hd(HP), hd(HP), hd(VD), ht(HP), ht(VDP)],
        compiler_params=_cp(("arbitrary",)),
    )(x, mod, win, qg, kvg, wq, wk, wv, wkt, wvt, pos, invf)


def _attn_fwd(q, k, vt):
    _, S, _ = q.shape
    tq = min(TQF, S)
    nq = S // tq
    half = tq // NSP
    tb = vt.shape[3]
    nsb = half // tb

    def body(q_ref, k_ref, vt_ref, o_ref, lse_ref):
        i = pl.program_id(1)
        qb = q_ref[...]

        def scores(j, hb):
            off = pl.multiple_of(j * tq + hb * half, half)
            return _nt(k_ref[pl.ds(off, half), :], qb)

        def update(j, hb, s, carry):
            m, acc = carry
            m_new = jnp.maximum(m, jnp.max(s, axis=0, keepdims=True))
            a = jnp.exp2(m - m_new)
            pb = jnp.exp2(s - m_new).astype(BF16)
            acc = a * acc
            for sb in range(nsb):
                acc = acc + _mm(vt_ref[(NSP * j + hb) * nsb + sb], pb[sb * tb:(sb + 1) * tb, :])
            return m_new, acc

        def trip(j, carry, masked):
            s = [scores(j, hb) for hb in range(NSP)]
            if masked:
                r = lax.broadcasted_iota(jnp.int32, s[0].shape, 0)
                cidx = lax.broadcasted_iota(jnp.int32, s[0].shape, 1)
                s = [jnp.where(cidx >= r + hb * half, s[hb], -1e30) for hb in range(NSP)]
            for hb in range(NSP):
                carry = update(j, hb, s[hb], carry)
            return carry

        def finish(carry):
            m, acc = carry
            l = acc[VD:VD + 1, :]
            o_ref[...] = (acc[0:VD, :] / l).T
            lse_ref[...] = m + jnp.log2(l)

        init = (jnp.full((1, tq), -1e30, F32), jnp.zeros((VDP, tq), F32))
        carry = lax.fori_loop(0, i >> 1, lambda t, cr: trip(2 * t + 1, trip(2 * t, cr, False), False), init)

        @pl.when((i & 1) == 0)
        def _():
            finish(trip(i, carry, True))

        @pl.when((i & 1) == 1)
        def _():
            finish(trip(i, trip(i - 1, carry, False), True))

    return pl.pallas_call(
        body, name="attn_fwd", grid=(NH, nq),
        in_specs=[pl.BlockSpec((None, tq, HP), lambda h, i: (h, i, 0)),
                  pl.BlockSpec((None, S, HP), lambda h, i: (h, 0, 0)),
                  pl.BlockSpec((None, S // tb, VDP, tb), lambda h, i: (h, 0, 0, 0))],
        out_specs=[pl.BlockSpec((tq, VD), lambda h, i: (i, h)),
                   pl.BlockSpec((None, None, 1, tq), lambda h, i: (h, i, 0, 0))],
        out_shape=[jax.ShapeDtypeStruct((S, NH * VD), F32), jax.ShapeDtypeStruct((NH, nq, 1, tq), F32)],
        compiler_params=_cp(("arbitrary", "arbitrary")),
    )(q, k, vt)


def _ssd_consts():
    tri = np.tril(np.ones((CH, CH), np.float32))
    e16 = np.zeros((128, SW), np.float32)
    for h in range(SH):
        e16[h, h * SP:(h + 1) * SP] = 1.0
    return jnp.asarray(tri, BF16), jnp.asarray(tri.T.copy(), BF16), jnp.asarray(e16, BF16)


def _ssd_chunk_fwd_common(xbc_ref, halo_ref, dtraw_ref, cw_ref, cb_ref, dtb_ref, alog_ref, tri_ref, e16_ref, ext, first):
    ext[0:8, :] = jnp.where(first, 0.0, halo_ref[...])
    ext[8:8 + CH, :] = xbc_ref[...]
    cw = cw_ref[...]
    wins = [ext[5 + kk:5 + kk + CH, :] for kk in range(CW)]
    xc = cb_ref[...] + cw[0:1, :] * wins[0]
    for kk in range(1, CW):
        xc = xc + cw[kk:kk + 1, :] * wins[kk]
    sact = _sigmoid(xc)
    act = xc * sact
    lane = lax.broadcasted_iota(jnp.int32, (1, 128), 1)
    arow = jnp.where(lane < SH, -jnp.exp(alog_ref[...]), 0.0)
    dtpre = dtraw_ref[...] + dtb_ref[...]
    dt = _softplus(dtpre)
    a = dt * arow
    cum = _xmm(tri_ref[...], a, 3)
    cumx = _mm_x(cum, e16_ref[...], 3)
    dtx = _mm_x(dt, e16_ref[...], 2)
    return xc, sact, act, arow, dtpre, dt, cum, cumx, dtx, wins


def _ssd_fwd(xbc, dtraw, zs, conv_w, conv_b, dtb, alog, dskx, gssm):
    S = xbc.shape[0]
    nc = S // CH
    tri, _, e16 = _ssd_consts()

    def body(xbc_ref, halo_ref, dtraw_ref, zs_ref, cw_ref, cb_ref, dtb_ref, alog_ref, dsk_ref, g_ref, tri_ref, e16_ref,
             y_ref, htp_ref, o_ref, ht, ext):
        i = pl.program_id(0)

        @pl.when(i == 0)
        def _():
            ht[...] = jnp.zeros_like(ht)

        xc, sact, act, arow, dtpre, dt, cum, cumx, dtx, wins = _ssd_chunk_fwd_common(
            xbc_ref, halo_ref, dtraw_ref, cw_ref, cb_ref, dtb_ref, alog_ref, tri_ref, e16_ref, ext, i == 0)
        cum_t = cum.T
        xs = act[:, 0:SW]
        lastx = cumx[CH - 1:CH, :]
        xh = xs * dtx
        eexp = jnp.exp(cumx)
        dte = jnp.exp(lastx - cumx)
        cdx = jnp.exp(lastx)
        htp = ht[...]
        htp_ref[...] = htp
        xw = (xh * dte).astype(BF16)
        xb = xh.astype(BF16)
        trim = tri_ref[...].astype(F32) > 0.5
        lane = lax.broadcasted_iota(jnp.int32, (CH, 128), 1)
        parts = []
        for g in range(SG):
            gl = slice(g * GW, (g + 1) * GW)
            bg = act[:, SW + g * SN:SW + (g + 1) * SN].astype(BF16)
            cg = act[:, SW + SG * SN + g * SN:SW + SG * SN + (g + 1) * SN].astype(BF16)
            cbm = _nt(cg, bg)
            yoff = eexp[:, gl] * _mm(cg, htp[:, gl].astype(BF16))
            ht[:, gl] = htp[:, gl] * cdx[:, gl] + _tn(bg, xw[:, gl])
            for pr in range(GW // 128):
                h0 = g * (SH // SG) + 2 * pr
                lo = g * GW + pr * 128
                xp = xb[:, lo:lo + 128]
                res = []
                for hh in (h0, h0 + 1):
                    seg = cum[:, hh:hh + 1] - cum_t[hh:hh + 1, :]
                    mh = jnp.where(trim, cbm * jnp.exp(seg), 0.0).astype(BF16)
                    res.append(_mm(mh, xp))
                parts.append(jnp.where(lane < SP, res[0], res[1]) + yoff[:, pr * 128:(pr + 1) * 128])
        y = jnp.concatenate(parts, axis=1) + xs * dsk_ref[...]
        y_ref[...] = y
        z = zs_ref[...]
        hf = y * (z * _sigmoid(z))
        outs = []
        for g in range(SG):
            hg = hf[:, g * GW:(g + 1) * GW]
            rs = lax.rsqrt(jnp.mean(hg * hg, axis=-1, keepdims=True) + RMS_EPS)
            outs.append(hg * rs)
        o_ref[...] = (jnp.concatenate(outs, axis=1) * g_ref[...]).astype(BF16)

    row = lambda n: pl.BlockSpec((CH, n), lambda i: (i, 0))
    return pl.pallas_call(
        body, name="ssd_fwd", grid=(nc,),
        in_specs=[row(CC), pl.BlockSpec((8, CC), lambda i: (jnp.maximum(i * (CH // 8) - 1, 0), 0)), row(128), row(SW),
                  _const((CW, CC)), _const((1, CC)), _const((1, 128)), _const((1, 128)), _const((1, SW)), _const((1, SW)),
                  _const((CH, CH)), _const((128, SW))],
        out_specs=[row(SW), pl.BlockSpec((None, SN, SW), lambda i: (i, 0, 0)), row(SW)],
        out_shape=[jax.ShapeDtypeStruct((S, SW), F32), jax.ShapeDtypeStruct((nc, SN, SW), F32),
                   jax.ShapeDtypeStruct((S, SW), BF16)],
        scratch_shapes=[pltpu.VMEM((SN, SW), F32), pltpu.VMEM((8 + CH, CC), F32)],
        compiler_params=_cp(("arbitrary",)),
    )(xbc, xbc, dtraw, zs, conv_w, conv_b, dtb, alog, dskx, gssm, tri, e16)


def _outproj(o, za, ossm, x, tgt, wout, mod, ln_g, ln_b):
    S = x.shape[0]
    tm = min(TM, S)

    e8 = np.zeros((D, 128), np.float32)
    for h in range(NH):
        e8[h * VD:(h + 1) * VD, h] = 1.0
    e8 = jnp.asarray(e8, BF16)

    def body(o_ref, za_ref, os_ref, x_ref, t_ref, w_ref, mod_ref, g_ref, b_ref, e8_ref,
             gx_ref, do_ref, dza_ref, dos_ref, delta_ref, dw_ref, vec_ref):
        i = pl.program_id(0)

        @pl.when(i == 0)
        def _():
            dw_ref[...] = jnp.zeros_like(dw_ref)
            vec_ref[...] = jnp.zeros_like(vec_ref)

        gate = mod_ref[0:1, 2 * D:3 * D]
        ov = o_ref[...]
        z = za_ref[...]
        sz = _sigmoid(z)
        silz = z * sz
        a = (ov * silz).astype(BF16)
        osb = os_ref[...]
        mixed = _mm(a, w_ref[0:D, :]) + _mm(osb, w_ref[D:MIX, :])
        xv = x_ref[...]
        hres = ALPHA * xv + gate * mixed
        mu = jnp.mean(hres, axis=-1, keepdims=True)
        hc = hres - mu
        var = jnp.mean(hc * hc, axis=-1, keepdims=True)
        rstd = lax.rsqrt(var + LN_EPS)
        xhat = hc * rstd
        g = g_ref[...]
        yv = xhat * g + b_ref[...]
        err = yv - t_ref[...]
        dy = err * (1.0 / D)
        vec_ref[0:1, :] += jnp.sum(err * err, axis=0, keepdims=True)
        vec_ref[1:2, :] += jnp.sum(dy * xhat, axis=0, keepdims=True)
        vec_ref[2:3, :] += jnp.sum(dy, axis=0, keepdims=True)
        dxh = dy * g
        dh = rstd * (dxh - jnp.mean(dxh, axis=-1, keepdims=True) - xhat * jnp.mean(dxh * xhat, axis=-1, keepdims=True))
        gx_ref[...] = ALPHA * dh
        vec_ref[3:4, :] += jnp.sum(dh * mixed, axis=0, keepdims=True)
        dmixed = (gate * dh).astype(BF16)
        dw_ref[0:D, :] += _tn(a, dmixed)
        dw_ref[D:MIX, :] += _tn(osb, dmixed)
        da = _nt(dmixed, w_ref[0:D, :])
        dos_ref[...] = _nt(dmixed, w_ref[D:MIX, :])
        dov = da * silz
        do_ref[...] = dov.astype(BF16)
        dza_ref[...] = (da * ov * (sz * (1.0 + z * (1.0 - sz)))).astype(BF16)
        delta_ref[:, 0, :] = _mm_x(dov * ov, e8_ref[...], 3).T[0:NH, :]

    row = lambda n: pl.BlockSpec((tm, n), lambda i: (i, 0))
    return pl.pallas_call(
        body, name="outproj", grid=(S // tm,),
        in_specs=[row(D), row(D), row(D), row(D), row(D), _const((MIX, D)), _const((8, 3 * D)), _const((1, D)), _const((1, D)),
                  _const((D, 128))],
        out_specs=[row(D), row(D), row(D), row(D), pl.BlockSpec((NH, None, 1, tm), lambda i: (0, i, 0, 0)),
                   _full((MIX, D)), _full((8, D))],
        out_shape=[jax.ShapeDtypeStruct((S, D), F32), jax.ShapeDtypeStruct((S, D), BF16), jax.ShapeDtypeStruct((S, D), BF16),
                   jax.ShapeDtypeStruct((S, D), F32), jax.ShapeDtypeStruct((NH, S // tm, 1, tm), F32),
                   jax.ShapeDtypeStruct((MIX, D), F32), jax.ShapeDtypeStruct((8, D), F32)],
        compiler_params=_cp(("arbitrary",)),
    )(o, za, ossm, x, tgt, wout, mod, ln_g, ln_b, e8)


def _attn_bwd(q, k, kt, v, do, lse, delta):
    _, S, _ = q.shape
    tk = min(TQ, S // 2)
    nk = S // tk
    tq = 2 * tk
    nq = S // tq
    tb = kt.shape[3]
    nsb = tk // tb

    def body(k_ref, kt_ref, v_ref, q_ref, do_ref, lse_ref, dl_ref, dk_ref, dv_ref, dq_ref, dqt_ref):
        j = pl.program_id(1)
        kb = k_ref[...]
        vb = v_ref[...]

        @pl.when(j == 0)
        def _():
            dqt_ref[...] = jnp.zeros_like(dqt_ref)

        dk_ref[...] = jnp.zeros_like(dk_ref)
        dv_ref[...] = jnp.zeros_like(dv_ref)

        def step(i, masked, lo=0):
            off = pl.multiple_of(i * tq + lo, tk)
            qb = q_ref[pl.ds(off, tq - lo), :]
            dob = do_ref[pl.ds(off, tq - lo), :]
            pt = jnp.exp2(_nt(kb, qb) - lse_ref[i][:, lo:tq])
            if masked:
                r = lax.broadcasted_iota(jnp.int32, pt.shape, 0)
                cidx = lax.broadcasted_iota(jnp.int32, pt.shape, 1)
                pt = jnp.where(i * tq + lo + cidx >= j * tk + r, pt, 0.0)
            dv_ref[...] += _mm(pt.astype(BF16), dob)
            dsb = (pt * (_nt(vb, dob) - dl_ref[i][:, lo:tq])).astype(BF16)
            dk_ref[...] += _mm(dsb, qb)
            acc = dqt_ref[i, :, lo:tq]
            for sb in range(nsb):
                acc = acc + _mm(kt_ref[sb], dsb[sb * tb:(sb + 1) * tb, :])
            dqt_ref[i, :, lo:tq] = acc

        first = j >> 1

        @pl.when((j & 1) == 0)
        def _():
            step(first, True)

        @pl.when((j & 1) == 1)
        def _():
            step(first, True, tk)
            dq_ref[...] = dqt_ref[first].T

        def loop_body(i, carry):
            step(i, False)
            return carry

        lax.fori_loop(first + 1, nq, loop_body, 0)
        dk_ref[...] = dk_ref[...] * LN2

    return pl.pallas_call(
        body, name="attn_bwd", grid=(NH, nk),
        in_specs=[pl.BlockSpec((None, tk, HP), lambda h, j: (h, j, 0)),
                  pl.BlockSpec((None, nsb, HP, tb), lambda h, j: (h, j, 0, 0)),
                  pl.BlockSpec((None, tk, VD), lambda h, j: (h, j, 0)),
                  pl.BlockSpec((None, S, HP), lambda h, j: (h, 0, 0)),
                  pl.BlockSpec((S, VD), lambda h, j: (0, h)),
                  pl.BlockSpec((None, nq, 1, tq), lambda h, j: (h, 0, 0, 0)),
                  pl.BlockSpec((None, nq, 1, tq), lambda h, j: (h, 0, 0, 0))],
        out_specs=[pl.BlockSpec((None, tk, HP), lambda h, j: (h, j, 0)),
                   pl.BlockSpec((None, tk, VD), lambda h, j: (h, j, 0)),
                   pl.BlockSpec((None, tq, HP), lambda h, j: (h, j >> 1, 0))],
        out_shape=[jax.ShapeDtypeStruct((NH, S, HP), F32), jax.ShapeDtypeStruct((NH, S, VD), F32),
                   jax.ShapeDtypeStruct((NH, S, HP), F32)],
        scratch_shapes=[pltpu.VMEM((nq, HP, tq), F32)],
        compiler_params=_cp(("arbitrary", "arbitrary")),
    )(k, kt, v, q, do, lse.reshape(NH, nq, 1, tq), delta.reshape(NH, nq, 1, tq))


def _ssd_bwd(xbc, dtraw, zs, y, htp, dossm, conv_w, conv_b, dtb, alog, dskx, gssm):
    S = xbc.shape[0]
    nc = S // CH
    tri, triu, e16 = _ssd_consts()

    def body(xbc_ref, halo_ref, dtraw_ref, zs_ref, y_ref, htp_ref, dos_ref,
             cw_ref, cb_ref, dtb_ref, alog_ref, dsk_ref, g_ref, tri_ref, triu_ref, e16_ref,
             dxbc_ref, ddt_ref, dzs_ref, dcw_ref, dcb_ref, dvec_ref, dg_ref,
             dht, ext, dext, dskacc):
        r = pl.program_id(0)
        i = nc - 1 - r

        @pl.when(r == 0)
        def _():
            dht[...] = jnp.zeros_like(dht)
            dext[CH:CH + 8, :] = jnp.zeros((8, CC), F32)
            dskacc[...] = jnp.zeros_like(dskacc)
            dcw_ref[...] = jnp.zeros_like(dcw_ref)
            dcb_ref[...] = jnp.zeros_like(dcb_ref)
            dvec_ref[...] = jnp.zeros_like(dvec_ref)
            dg_ref[...] = jnp.zeros_like(dg_ref)

        xc, sact, act, arow, dtpre, dt, cum, cumx, dtx, wins = _ssd_chunk_fwd_common(
            xbc_ref, halo_ref, dtraw_ref, cw_ref, cb_ref, dtb_ref, alog_ref, tri_ref, e16_ref, ext, i == 0)
        cum_t = cum.T
        xs = act[:, 0:SW]
        lastx = cumx[CH - 1:CH, :]
        xh = xs * dtx
        eexp = jnp.exp(cumx)
        dte = jnp.exp(lastx - cumx)
        cdx = jnp.exp(lastx)
        trim = tri_ref[...].astype(F32) > 0.5
        lane = lax.broadcasted_iota(jnp.int32, (CH, 128), 1)
        rowi = lax.broadcasted_iota(jnp.int32, (CH, 128), 0)

        yv = y_ref[...]
        z = zs_ref[...]
        sz = _sigmoid(z)
        silz = z * sz
        hf = yv * silz
        dn = dos_ref[...] * g_ref[...]
        dhf_parts, nrm_parts = [], []
        for g in range(SG):
            gl = slice(g * GW, (g + 1) * GW)
            hg = hf[:, gl]
            rs = lax.rsqrt(jnp.mean(hg * hg, axis=-1, keepdims=True) + RMS_EPS)
            ng = hg * rs
            dng = dn[:, gl]
            dhf_parts.append(rs * (dng - ng * jnp.mean(dng * ng, axis=-1, keepdims=True)))
            nrm_parts.append(ng)
        nrm = jnp.concatenate(nrm_parts, axis=1)
        dhf = jnp.concatenate(dhf_parts, axis=1)
        dg_ref[...] += jnp.sum(dos_ref[...] * nrm, axis=0, keepdims=True)
        dyv = dhf * silz
        dzs_ref[...] = (dhf * yv * (sz * (1.0 + z * (1.0 - sz)))).astype(BF16)
        dskacc[...] += jnp.sum(dyv * xs, axis=0, keepdims=True)
        dxs_skip = dyv * dsk_ref[...]

        dhtn = dht[...]
        hp = htp_ref[...]
        dlastx = jnp.sum(dhtn * hp, axis=0, keepdims=True) * cdx
        xb = xh.astype(BF16)
        xwf = xh * dte
        dcum = jnp.zeros((CH, 128), F32)
        dcum_t = jnp.zeros((128, CH), F32)
        dxh_parts, dcumx_parts, dlast_parts, db_parts, dc_parts = [], [], [], [], []
        for g in range(SG):
            gl = slice(g * GW, (g + 1) * GW)
            bg = act[:, SW + g * SN:SW + (g + 1) * SN].astype(BF16)
            cg = act[:, SW + SG * SN + g * SN:SW + SG * SN + (g + 1) * SN].astype(BF16)
            hpg = hp[:, gl].astype(BF16)
            dhn = dhtn[:, gl].astype(BF16)
            dyg = dyv[:, gl]
            dz = (dyg * eexp[:, gl]).astype(BF16)
            dcg = _nt(dz, hpg)
            dht[:, gl] = dhtn[:, gl] * cdx[:, gl] + _tn(cg, dz)
            yoff = eexp[:, gl] * _mm(cg, hpg)
            dcumx_g = dyg * yoff
            dbg = _nt(xwf[:, gl].astype(BF16), dhn)
            dxw = _mm(bg, dhn)
            ddte = dxw * xwf[:, gl]
            dcumx_parts.append(dcumx_g - ddte)
            dlast_parts.append(jnp.sum(ddte, axis=0, keepdims=True))
            dxh_g = dxw * dte[:, gl]
            cbm = _nt(cg, bg)
            dcb = jnp.zeros((CH, CH), F32)
            dxp_parts = []
            for pr in range(GW // 128):
                h0 = g * (SH // SG) + 2 * pr
                lo = g * GW + pr * 128
                xp = xb[:, lo:lo + 128]
                dyp = dyv[:, lo:lo + 128]
                dxp = jnp.zeros((CH, 128), F32)
                for idx, hh in enumerate((h0, h0 + 1)):
                    decay = jnp.where(trim, jnp.exp(cum[:, hh:hh + 1] - cum_t[hh:hh + 1, :]), 0.0)
                    mh = cbm * decay
                    keep = (lane < SP) if idx == 0 else (lane >= SP)
                    dym = jnp.where(keep, dyp, 0.0).astype(BF16)
                    dm = _nt(dym, xp)
                    dxp = dxp + _tn(mh.astype(BF16), dym)
                    gm = dm * mh
                    dcum = dcum + jnp.where(lane == hh, jnp.sum(gm, axis=1, keepdims=True), 0.0)
                    dcum_t = dcum_t - jnp.where(rowi == hh, jnp.sum(gm, axis=0, keepdims=True), 0.0)
                    dcb = dcb + dm * decay
                dxp_parts.append(dxp)
            dxh_parts.append(dxh_g + jnp.concatenate(dxp_parts, axis=1))
            dcbb = dcb.astype(BF16)
            dc_parts.append(dcg + _mm(dcbb, bg))
            db_parts.append(dbg + _tn(dcbb, cg))
        dxh = jnp.concatenate(dxh_parts, axis=1)
        dcumx = jnp.concatenate(dcumx_parts, axis=1)
        dlastx = dlastx + jnp.concatenate(dlast_parts, axis=1)
        e16 = e16_ref[...]
        dlast128 = _nt_x(jnp.broadcast_to(dlastx, (8, SW)), e16, 2)[0:1, :]
        dcum = dcum + dcum_t.T + _nt_x(dcumx, e16, 2) + jnp.where(rowi == CH - 1, dlast128, 0.0)
        da = _xmm(triu_ref[...], dcum, 2)
        ddt = da * arow + _nt_x(dxh * xs, e16, 2)
        dvec_ref[1:2, :] += jnp.sum(da * dt, axis=0, keepdims=True)
        ddtraw = jnp.where(lane < SH, ddt * _sigmoid(dtpre), 0.0)
        dvec_ref[0:1, :] += jnp.sum(ddtraw, axis=0, keepdims=True)
        ddt_ref[...] = ddtraw.astype(BF16)
        dxs = dxs_skip + dxh * dtx
        dact = jnp.concatenate([dxs] + db_parts + dc_parts, axis=1)
        dxc = dact * (sact * (1.0 + xc * (1.0 - sact)))

        dcb_ref[...] += jnp.sum(dxc, axis=0, keepdims=True)
        for kk in range(CW):
            dcw_ref[kk:kk + 1, :] += jnp.sum(dxc * wins[kk], axis=0, keepdims=True)
        dext[0:CH, :] = dxc
        cw = cw_ref[...]
        dxr = cw[CW - 1:CW, :] * dxc
        for kk in range(CW - 1):
            dxr = dxr + cw[kk:kk + 1, :] * dext[CW - 1 - kk:CW - 1 - kk + CH, :]
        dxbc_ref[...] = dxr.astype(BF16)
        dext[CH:CH + 8, :] = dxc[0:8, :]

        @pl.when(r == nc - 1)
        def _():
            dvec_ref[1:2, :] = dvec_ref[1:2, :] * arow
            dvec_ref[2:3, :] = _nt_x(jnp.broadcast_to(dskacc[...], (8, SW)), e16, 3)[0:1, :]

    rev = lambda n: pl.BlockSpec((CH, n), lambda r: (nc - 1 - r, 0))
    return pl.pallas_call(
        body, name="ssd_bwd", grid=(nc,),
        in_specs=[rev(CC), pl.BlockSpec((8, CC), lambda r: (jnp.maximum((nc - 1 - r) * (CH // 8) - 1, 0), 0)),
                  rev(128), rev(SW), rev(SW), pl.BlockSpec((None, SN, SW), lambda r: (nc - 1 - r, 0, 0)), rev(SW),
                  _const((CW, CC)), _const((1, CC)), _const((1, 128)), _const((1, 128)), _const((1, SW)), _const((1, SW)),
                  _const((CH, CH)), _const((CH, CH)), _const((128, SW))],
        out_specs=[rev(CC), rev(128), rev(SW), _full((CW, CC)), _full((1, CC)), _full((8, 128)), _full((1, SW))],
        out_shape=[jax.ShapeDtypeStruct((S, CC), BF16), jax.ShapeDtypeStruct((S, 128), BF16), jax.ShapeDtypeStruct((S, SW), BF16),
                   jax.ShapeDtypeStruct((CW, CC), F32), jax.ShapeDtypeStruct((1, CC), F32),
                   jax.ShapeDtypeStruct((8, 128), F32), jax.ShapeDtypeStruct((1, SW), F32)],
        scratch_shapes=[pltpu.VMEM((SN, SW), F32), pltpu.VMEM((8 + CH, CC), F32), pltpu.VMEM((CH + 8, CC), F32),
                        pltpu.VMEM((1, SW), F32)],
        compiler_params=_cp(("arbitrary",)),
    )(xbc, xbc, dtraw, zs, y, htp, dossm, conv_w, conv_b, dtb, alog, dskx, gssm, tri, triu, e16)


def _mla_bwd(dq, dk, dv, qlat, ckv, qg, kvg, wq, wk, wv, pos, invf):
    S = qlat.shape[0]
    tm = min(TQ, S)

    def body(dq_ref, dk_ref, dv_ref, ql_ref, ckv_ref, qg_ref, kvg_ref, wq_ref, wk_ref, wv_ref, pos_ref, invf_ref,
             dql_ref, dckv_ref, dkr_ref, dwq_ref, dwk_ref, dwv_ref, dqg_ref, dkvg_ref):
        i = pl.program_id(0)

        @pl.when(i == 0)
        def _():
            dwq_ref[...] = jnp.zeros_like(dwq_ref)
            dwk_ref[...] = jnp.zeros_like(dwk_ref)
            dwv_ref[...] = jnp.zeros_like(dwv_ref)
            dqg_ref[...] = jnp.zeros_like(dqg_ref)
            dkvg_ref[...] = jnp.zeros_like(dkvg_ref)

        ang = pos_ref[...].astype(F32) * invf_ref[...]
        cs = jnp.cos(ang)
        sn = jnp.sin(ang)

        def rms_bwd(v, g, dn, dg_ref):
            r = lax.rsqrt(jnp.mean(v * v, axis=-1, keepdims=True) + RMS_EPS)
            vh = v * r
            dg_ref[...] += jnp.sum(dn * vh, axis=0, keepdims=True)
            dvh = dn * g
            return vh, r * (dvh - vh * jnp.mean(dvh * vh, axis=-1, keepdims=True))

        pieces = []
        for h in range(NH):
            dqh = dq_ref[h]
            pieces.append(dqh[:, 0:NOPE] * SCALE)
            pieces.append(_rope_t(dqh[:, NOPE:HP], cs, sn) * SCALE)
        dqf = jnp.concatenate(pieces, axis=1).astype(BF16)
        ql = ql_ref[...]
        g = qg_ref[...]
        dqn = _nt(dqf, wq_ref[...])
        qh, dql = rms_bwd(ql, g, dqn, dqg_ref)
        dwq_ref[...] += _tn((qh * g).astype(BF16), dqf)
        dql_ref[...] = dql.astype(BF16)

        dkn_p = jnp.concatenate([dk_ref[h, :, 0:NOPE] for h in range(NH)], axis=1).astype(BF16)
        dvf = jnp.concatenate([dv_ref[h] for h in range(NH)], axis=1).astype(BF16)
        dkr = dk_ref[0, :, NOPE:HP]
        for h in range(1, NH):
            dkr = dkr + dk_ref[h, :, NOPE:HP]
        lane = lax.broadcasted_iota(jnp.int32, dkr.shape, 1)
        dkr_ref[...] = jnp.where(lane < ROPE, _rope_t(dkr, cs, sn), 0.0).astype(BF16)
        cv = ckv_ref[...]
        gk = kvg_ref[...]
        dkn = _nt(dkn_p, wk_ref[...]) + _nt(dvf, wv_ref[...])
        kh, dckv = rms_bwd(cv, gk, dkn, dkvg_ref)
        knb = (kh * gk).astype(BF16)
        dwk_ref[...] += _tn(knb, dkn_p)
        dwv_ref[...] += _tn(knb, dvf)
        dckv_ref[...] = dckv.astype(BF16)

    row = lambda n: pl.BlockSpec((tm, n), lambda i: (i, 0))
    heads = lambda n: pl.BlockSpec((NH, tm, n), lambda i: (0, i, 0))
    return pl.pallas_call(
        body, name="mla_bwd", grid=(S // tm,),
        in_specs=[heads(HP), heads(HP), heads(VD), row(QL), row(KVL), _const((1, QL)), _const((1, KVL)),
                  _const((QL, NH * HP)), _const((KVL, NH * NOPE)), _const((KVL, NH * VD)), row(1), _const((1, 128))],
        out_specs=[row(QL), row(KVL), row(128), _full((QL, NH * HP)), _full((KVL, NH * NOPE)), _full((KVL, NH * VD)),
                   _full((1, QL)), _full((1, KVL))],
        out_shape=[jax.ShapeDtypeStruct((S, QL), BF16), jax.ShapeDtypeStruct((S, KVL), BF16), jax.ShapeDtypeStruct((S, 128), BF16),
                   jax.ShapeDtypeStruct((QL, NH * HP), F32), jax.ShapeDtypeStruct((KVL, NH * NOPE), F32),
                   jax.ShapeDtypeStruct((KVL, NH * VD), F32), jax.ShapeDtypeStruct((1, QL), F32), jax.ShapeDtypeStruct((1, KVL), F32)],
        compiler_params=_cp(("arbitrary",)),
    )(dq, dk, dv, qlat, ckv, qg, kvg, wq, wk, wv, pos, invf)


def _inproj_bwd(x, gx1, mod, win, dql, dckv, dza, dxbc, dzs, dkr, ddt):
    S = x.shape[0]
    tm = min(TM, S)

    def body(x_ref, gx1_ref, mod_ref, win_ref, dql_ref, dckv_ref, dza_ref, dxbc_ref, dzs_ref, dkr_ref, ddt_ref,
             gx_ref, dw_ref, vec_ref):
        i = pl.program_id(0)

        @pl.when(i == 0)
        def _():
            dw_ref[...] = jnp.zeros_like(dw_ref)
            vec_ref[...] = jnp.zeros_like(vec_ref)

        shift = mod_ref[0:1, 0:D]
        scale = mod_ref[0:1, D:2 * D]
        xv = x_ref[...]
        ut = (xv * (1.0 + scale) + shift).T.astype(BF16)
        pieces = (dql_ref, dckv_ref, dza_ref, dxbc_ref, dzs_ref, dkr_ref, ddt_ref)
        du = jnp.zeros((tm, D), F32)
        lo = 0
        for p_ref in pieces:
            n = p_ref.shape[1]
            dp = p_ref[...]
            du = du + _nt(dp, win_ref[:, lo:lo + n])
            dw_ref[:, lo:lo + n] += _mm(ut, dp)
            lo += n
        vec_ref[0:1, :] += jnp.sum(du, axis=0, keepdims=True)
        vec_ref[1:2, :] += jnp.sum(du * xv, axis=0, keepdims=True)
        gx_ref[...] = gx1_ref[...] + du * (1.0 + scale)

    row = lambda n: pl.BlockSpec((tm, n), lambda i: (i, 0))
    return pl.pallas_call(
        body, name="inproj_bwd", grid=(S // tm,),
        in_specs=[row(D), row(D), _const((8, 3 * D)), _const((D, IN_P)), row(QL), row(KVL), row(D), row(CC), row(D),
                  row(128), row(128)],
        out_specs=[row(D), pl.BlockSpec((D, IN_P), lambda i: (0, 0), pipeline_mode=pl.Buffered(1)), _full((8, D))],
        out_shape=[jax.ShapeDtypeStruct((S, D), F32), jax.ShapeDtypeStruct((D, IN_P), F32), jax.ShapeDtypeStruct((8, D), F32)],
        compiler_params=_cp(("arbitrary",)),
    )(x, gx1, mod, win, dql, dckv, dza, dxbc, dzs, dkr, ddt)


def _ada_bwd(callt, dmods):
    w = dmods.shape[1]

    def body(c_ref, d_ref, o_ref):
        acc = c_ref[:, 0:1] * d_ref[0:1, :]
        for s in range(1, 8):
            acc = acc + c_ref[:, s:s + 1] * d_ref[s:s + 1, :]
        o_ref[0] = acc

    return pl.pallas_call(body, name="ada_bwd", out_shape=jax.ShapeDtypeStruct((1, D, w), F32),
                          compiler_params=_cp())(callt, dmods)


def _adamw(name, parts, w, m, v):
    rows, ncol = w.shape
    tr = min(rows, 128)
    nparts = parts.shape[0]

    def body(p_ref, w_ref, m_ref, v_ref, g_ref, d_ref, nm_ref, nv_ref):
        g = p_ref[0].astype(F32)
        for s in range(1, nparts):
            g = g + p_ref[s].astype(F32)
        g_ref[...] = g
        nm = B1 * m_ref[...] + (1.0 - B1) * g
        nv = B2 * v_ref[...] + (1.0 - B2) * (g * g)
        nm_ref[...] = nm
        nv_ref[...] = nv
        m_hat = nm / (1.0 - B1 ** STEP)
        v_hat = nv / (1.0 - B2 ** STEP)
        d_ref[...] = -LR * (m_hat / (jnp.sqrt(v_hat) + EPS) + WD * w_ref[...])

    row = pl.BlockSpec((tr, ncol), lambda i: (i, 0))
    sd = jax.ShapeDtypeStruct((rows, ncol), F32)
    return pl.pallas_call(
        body, name="adamw_" + name, grid=(rows // tr,),
        in_specs=[pl.BlockSpec((nparts, tr, ncol), lambda i: (0, i, 0)), row, row, row],
        out_specs=[row, row, row, row], out_shape=[sd, sd, sd, sd],
        compiler_params=_cp(("arbitrary",)),
    )(parts, w, m, v)


_SMALL = (("b_ada", 3 * D), ("conv_w", CW * CC // 4), ("conv_b", CC), ("ssm_norm_g", SW), ("ln_g", D), ("ln_b", D),
          ("q_norm_g", QL), ("kv_norm_g", KVL), ("dt_bias", SH), ("a_log", SH), ("d_skip", SH), ("loss", 128))


def _pack_small(d, lead):
    flat = [d[name].reshape(d[name].shape[:lead] + (-1,)) for name, _ in _SMALL]
    used = sum(f.shape[lead] for f in flat)
    pad = jnp.zeros(flat[0].shape[:lead] + (R_SMALL * 1024 - used,), F32)
    return jnp.concatenate(flat + [pad], axis=lead).reshape(flat[0].shape[:lead] + (R_SMALL, 1024))


def _unpack_small(p):
    flat = p.reshape(-1)
    out, r = {}, 0
    for name, n in _SMALL:
        out[name] = flat[r:r + n]
        r += n
    return out


def _in_to_padded(w):
    z = lambda n: jnp.zeros((w.shape[0], n), w.dtype)
    return jnp.concatenate([w[:, 0:384], w[:, 384:640], w[:, 704:1728], w[:, 1728:3264], w[:, 3280:4304],
                            w[:, 640:704], z(64), w[:, 3264:3280], z(112)], axis=1)


def _in_from_padded(g):
    return jnp.concatenate([g[:, 0:384], g[:, 384:640], g[:, P_KR[0]:P_KR[0] + 64], g[:, 640:1664], g[:, 1664:3200],
                            g[:, P_DT[0]:P_DT[0] + 16], g[:, 3200:4224]], axis=1)


def kernel(x, c, positions, w_ada, b_ada, w_in, q_norm_g, w_qb, kv_norm_g, w_kvb, conv_w, conv_b, dt_bias, a_log, d_skip, ssm_norm_g, w_out, ln_g, ln_b, loss_target, m_w_ada, m_b_ada, m_w_in, m_q_norm_g, m_w_qb, m_kv_norm_g, m_w_kvb, m_conv_w, m_conv_b, m_dt_bias, m_a_log, m_d_skip, m_ssm_norm_g, m_w_out, m_ln_g, m_ln_b, v_w_ada, v_b_ada, v_w_in, v_q_norm_g, v_w_qb, v_kv_norm_g, v_w_kvb, v_conv_w, v_conv_b, v_dt_bias, v_a_log, v_d_skip, v_ssm_norm_g, v_w_out, v_ln_g, v_ln_b):
    S = x.shape[1]
    xv = x[0]
    tgt = loss_target[0]

    cw16 = jnp.concatenate([conv_w[0], jnp.zeros((16 - CW, CC // 4), F32)], axis=0)
    f_in, f_qb, f_kvb, f_out, f_cw = _gather_weights(
        [w_in[0].astype(BF16), w_qb[0].astype(BF16), w_kvb[0].astype(BF16), w_out[0].astype(BF16), cw16])
    cat1 = lambda f: f.transpose(1, 0, 2).reshape(f.shape[1], 4 * f.shape[2])
    win = _in_to_padded(cat1(f_in))
    wqb = cat1(f_qb).reshape(QL, NH, QKD)
    wq = jnp.concatenate([wqb, jnp.zeros((QL, NH, HP - QKD), BF16)], axis=2).reshape(QL, NH * HP)
    wkvb = cat1(f_kvb).reshape(KVL, NH, NOPE + VD)
    wk = wkvb[:, :, 0:NOPE].reshape(KVL, NH * NOPE)
    wv = wkvb[:, :, NOPE:].reshape(KVL, NH * VD)
    wout = f_out.reshape(MIX, D)
    cwf = cat1(f_cw[:, 0:CW, :])

    half = ROPE // 2
    invf = 1.0 / (ROPE_THETA ** (jnp.arange(half, dtype=F32) / half))
    invf = jnp.concatenate([invf, invf, jnp.zeros((128 - ROPE,), F32)]).reshape(1, 128)
    pos = positions.reshape(S, 1)
    pad128 = lambda a: jnp.concatenate([a.reshape(1, SH), jnp.zeros((1, 128 - SH), F32)], axis=1)
    dtb, alog = pad128(dt_bias), pad128(a_log)
    dskx = jnp.repeat(d_skip.reshape(SH), SP).reshape(1, SW)

    my_c = lax.axis_index("c")
    (call,) = _exchange("gather_c", [jnp.broadcast_to(c.reshape(1, 1, D), (4, 1, D))])
    call = call.reshape(8, D)
    mods = _ada(call, w_ada[0])
    (mrows,) = _exchange("scatter_mod", [mods.reshape(4, 2, 3 * D // 4)])
    mine = lax.dynamic_index_in_dim(mrows.reshape(4, 2, 2, 3 * D // 4)[:, 0], my_c, axis=1, keepdims=False)
    mod = jnp.broadcast_to(mine.reshape(1, 3 * D) + b_ada, (8, 3 * D))
    qlat, ckv, za, xbc, zs, dtraw, q, k, v, kt, vt = _inproj(xv, mod, win, q_norm_g, kv_norm_g, wq, wk, wv, wk.T, wv.T, pos, invf)
    o, lse = _attn_fwd(q, k, vt)
    y, htp, ossm = _ssd_fwd(xbc, dtraw, zs, cwf, conv_b, dtb, alog, dskx, ssm_norm_g)
    gx1, do, dza, dossm, delta, dwout, vec_o = _outproj(o, za, ossm, xv, tgt, wout, mod, ln_g, ln_b)
    loss_part = jnp.zeros((128,), F32).at[0].set(0.5 / D * jnp.sum(vec_o[0]))

    dk, dv, dq = _attn_bwd(q, k, kt, v, do, lse, delta)
    dxbc, ddt, dzs, dcw, dcb, dvec, dgssm = _ssd_bwd(xbc, dtraw, zs, y, htp, dossm, cwf, conv_b, dtb, alog, dskx, ssm_norm_g)
    dql, dckv, dkr, dwq, dwk, dwv, dqg, dkvg = _mla_bwd(dq, dk, dv, qlat, ckv, q_norm_g, kv_norm_g, wq, wk, wv, pos, invf)
    gx, dwin, vec_i = _inproj_bwd(xv, gx1, mod, win, dql, dckv, dza, dxbc, dzs, dkr, ddt)
    dmod = jnp.concatenate([vec_i[0:1], vec_i[1:2], vec_o[3:4]], axis=1)

    cols = lambda g: g.reshape(g.shape[0], 4, g.shape[1] // 4).transpose(1, 0, 2)
    g_in = cols(_in_from_padded(dwin)).astype(BF16)
    g_qb = cols(dwq.reshape(QL, NH, HP)[:, :, 0:QKD].reshape(QL, NH * QKD)).astype(BF16)
    g_kvb = cols(jnp.concatenate([dwk.reshape(KVL, NH, NOPE), dwv.reshape(KVL, NH, VD)], axis=2)
                 .reshape(KVL, NH * (NOPE + VD))).astype(BF16)
    g_out = dwout.reshape(4, MIX // 4, D).astype(BF16)
    small = {"b_ada": dmod, "conv_b": dcb, "ssm_norm_g": dgssm, "ln_g": vec_o[1:2], "ln_b": vec_o[2:3],
             "q_norm_g": dqg, "kv_norm_g": dkvg, "dt_bias": dvec[0:1, 0:SH], "a_log": dvec[1:2, 0:SH], "d_skip": dvec[2:3, 0:SH],
             "loss": loss_part}
    small = {n: jnp.broadcast_to(a.reshape(1, -1), (4, a.size)) for n, a in small.items()}
    small["conv_w"] = cols(dcw).reshape(4, CW * CC // 4)
    gsmall = _pack_small(small, 1)

    r_in, r_qb, r_kvb, r_out, rs, dmods = _exchange(
        "exchange_grads", [g_in, g_qb, g_kvb, g_out, gsmall, jnp.broadcast_to(dmod.reshape(1, 1, 3 * D), (4, 1, 3 * D))])
    chip = 2 * lax.axis_index("x") + lax.axis_index("y")
    dmods = lax.dynamic_slice_in_dim(dmods.reshape(8, 3 * D), chip * (3 * D // 4), 3 * D // 4, axis=1)
    g_ada = _ada_bwd(call.T, dmods)
    res = dict(w_ada=_adamw("w_ada", g_ada, w_ada[0], m_w_ada[0], v_w_ada[0]),
               w_in=_adamw("w_in", r_in, w_in[0], m_w_in[0], v_w_in[0]),
               w_qb=_adamw("w_qb", r_qb, w_qb[0], m_w_qb[0], v_w_qb[0]),
               w_kvb=_adamw("w_kvb", r_kvb, w_kvb[0], m_w_kvb[0], v_w_kvb[0]),
               w_out=_adamw("w_out", r_out, w_out[0], m_w_out[0], v_w_out[0]))
    wsm = _pack_small(dict(b_ada=b_ada, conv_w=conv_w, conv_b=conv_b, ssm_norm_g=ssm_norm_g, ln_g=ln_g, ln_b=ln_b,
                           q_norm_g=q_norm_g, kv_norm_g=kv_norm_g, dt_bias=dt_bias, a_log=a_log, d_skip=d_skip, loss=jnp.zeros((128,), F32)), 0)
    msm = _pack_small(dict(b_ada=m_b_ada, conv_w=m_conv_w, conv_b=m_conv_b, ssm_norm_g=m_ssm_norm_g, ln_g=m_ln_g, ln_b=m_ln_b,
                           q_norm_g=m_q_norm_g, kv_norm_g=m_kv_norm_g, dt_bias=m_dt_bias, a_log=m_a_log, d_skip=m_d_skip, loss=jnp.zeros((128,), F32)), 0)
    vsm = _pack_small(dict(b_ada=v_b_ada, conv_w=v_conv_w, conv_b=v_conv_b, ssm_norm_g=v_ssm_norm_g, ln_g=v_ln_g, ln_b=v_ln_b,
                           q_norm_g=v_q_norm_g, kv_norm_g=v_kv_norm_g, dt_bias=v_dt_bias, a_log=v_a_log, d_skip=v_d_skip, loss=jnp.zeros((128,), F32)), 0)
    sm = _adamw("small", rs, wsm, msm, vsm)

    order = ["w_ada", "b_ada", "w_in", "q_norm_g", "w_qb", "kv_norm_g", "w_kvb", "conv_w", "conv_b", "dt_bias", "a_log",
             "d_skip", "ssm_norm_g", "w_out", "ln_g", "ln_b"]
    shapes = dict(w_ada=w_ada.shape, b_ada=b_ada.shape, w_in=w_in.shape, q_norm_g=q_norm_g.shape, w_qb=w_qb.shape,
                  kv_norm_g=kv_norm_g.shape, w_kvb=w_kvb.shape, conv_w=conv_w.shape, conv_b=conv_b.shape, dt_bias=dt_bias.shape,
                  a_log=a_log.shape, d_skip=d_skip.shape, ssm_norm_g=ssm_norm_g.shape, w_out=w_out.shape, ln_g=ln_g.shape,
                  ln_b=ln_b.shape)
    outs = []
    for kind in range(4):
        d = _unpack_small(sm[kind])
        d.update({n: r[kind] for n, r in res.items()})
        outs.extend(d[n].reshape(shapes[n]) for n in order)
    loss = _unpack_small(sm[0])["loss"][0]
    return (loss, gx.reshape(x.shape), *outs)
```

```python
import functools
import math

import numpy as np
import jax
import jax.numpy as jnp
from jax import lax
from jax.experimental import pallas as pl
from jax.experimental.pallas import tpu as pltpu

F32 = jnp.float32
BF16 = jnp.bfloat16
MESH_ID = pl.DeviceIdType.MESH

D = 1024
NH = 8
NOPE = 128
ROPE = 64
VD = 128
VDP = 144
QKD = NOPE + ROPE
HP = 256
QL = 384
KVL = 256
ROPE_THETA = 10000.0
SH = 16
SP = 64
SG = 2
SN = 128
CW = 4
CH = 128
SW = SH * SP
CC = SW + 2 * SG * SN
GW = SW // SG
MIX = 2 * D
IN_W = 4304
ALPHA = 2.0 ** 0.25
RMS_EPS = 1e-6
LN_EPS = 1e-5
SCALE = QKD ** -0.5
LN2 = math.log(2.0)
QSCALE = SCALE / LN2
LR, B1, B2, EPS, WD, STEP = 0.001, 0.9, 0.999, 1e-08, 0.01, 10

P_Q = (0, 384)
P_KV = (384, 640)
P_ZA = (640, 1664)
P_XBC = (1664, 3200)
P_ZS = (3200, 4224)
P_KR = (4224, 4352)
P_DT = (4352, 4480)
IN_P = 4480

R_SMALL = 16

TM = 256
TQ = 512
NSP = 2
TQF = 1024
VMEM_LIMIT = 56 * 1024 * 1024


def _cp(sem=None):
    return pltpu.CompilerParams(dimension_semantics=sem, vmem_limit_bytes=VMEM_LIMIT)


def _mm(a, b):
    return jnp.dot(a, b, preferred_element_type=F32)


def _nt(a, b):
    return lax.dot_general(a, b, (((1,), (1,)), ((), ())), preferred_element_type=F32)


def _tn(a, b):
    return lax.dot_general(a, b, (((0,), (0,)), ((), ())), preferred_element_type=F32)


def _split(a, terms):
    parts = []
    for t in range(terms):
        p = a.astype(BF16)
        parts.append(p)
        if t + 1 < terms:
            a = a - p.astype(F32)
    return parts


def _mm_x(a, ones, terms):
    parts = _split(a, terms)
    out = _mm(parts[0], ones)
    for p in parts[1:]:
        out = out + _mm(p, ones)
    return out


def _xmm(ones, a, terms):
    parts = _split(a, terms)
    out = _mm(ones, parts[0])
    for p in parts[1:]:
        out = out + _mm(ones, p)
    return out


def _nt_x(a, ones, terms):
    parts = _split(a, terms)
    out = _nt(parts[0], ones)
    for p in parts[1:]:
        out = out + _nt(p, ones)
    return out


def _sigmoid(z):
    return 1.0 / (1.0 + jnp.exp(-z))


def _softplus(z):
    return jnp.maximum(z, 0.0) + jnp.log1p(jnp.exp(-jnp.abs(z)))


def _rope(t, cs, sn):
    lane = lax.broadcasted_iota(jnp.int32, t.shape, 1)
    rot = jnp.where(lane < ROPE // 2, -pltpu.roll(t, 128 - ROPE // 2, 1), pltpu.roll(t, ROPE // 2, 1))
    return t * cs + rot * sn


def _rope_t(t, cs, sn):
    lane = lax.broadcasted_iota(jnp.int32, t.shape, 1)
    y = t * sn
    rot = jnp.where(lane < ROPE // 2, -pltpu.roll(y, 128 - ROPE // 2, 1), pltpu.roll(y, ROPE // 2, 1))
    return t * cs - rot


def _full(shape):
    n = len(shape)
    return pl.BlockSpec(shape, lambda *_: (0,) * n)


def _const(shape):
    n = len(shape)
    return pl.BlockSpec(shape, lambda *_: (0,) * n, pipeline_mode=pl.Buffered(1))


def _gather_weights(shards):
    n = len(shards)
    halves = [a.shape[0] // 2 for a in shards]

    def body(*refs):
        srcs, dsts = refs[:n], refs[n:2 * n]
        send_sems, recv_sems, local_sems = refs[2 * n:]
        x, y, c = lax.axis_index("x"), lax.axis_index("y"), lax.axis_index("c")
        me = 2 * x + y
        sibling = (x, y, 1 - c)
        chips = [(1 - x, y), (x, 1 - y), (1 - x, 1 - y)]

        def rows(a, pc):
            return pl.ds(pl.multiple_of(pc * halves[a], halves[a]), halves[a])

        def copy(a, k, src, slot, pc, to):
            return pltpu.make_async_remote_copy(
                src_ref=src, dst_ref=dsts[a].at[slot, rows(a, pc)], send_sem=send_sems.at[a, k],
                recv_sem=recv_sems.at[a, k], device_id=to, device_id_type=MESH_ID)

        local = [pltpu.make_async_copy(srcs[a], dsts[a].at[me], local_sems.at[a]) for a in range(n)]
        for cp in local:
            cp.start()
        sends = [copy(a, j, srcs[a].at[rows(a, c)], me, c, (px, py, c)) for a in range(n) for j, (px, py) in enumerate(chips)]
        for cp in sends:
            cp.start()
        passed = []
        for a in range(n):
            for j, (px, py) in enumerate(chips):
                k = 2 * px + py
                copy(a, j, srcs[a].at[rows(a, c)], k, c, (x, y, c)).wait_recv()
                fwd = copy(a, 3 + j, dsts[a].at[k, rows(a, c)], k, c, sibling)
                fwd.start()
                passed.append(fwd)
        for a in range(n):
            for j, (px, py) in enumerate(chips):
                copy(a, 3 + j, srcs[a].at[rows(a, c)], 2 * px + py, 1 - c, (x, y, c)).wait_recv()
        for cp in sends + passed:
            cp.wait_send()
        for cp in local:
            cp.wait()

    hbm = pl.BlockSpec(memory_space=pltpu.HBM)
    return pl.pallas_call(
        body, name="gather_weights",
        out_shape=tuple(jax.ShapeDtypeStruct((4,) + a.shape, a.dtype) for a in shards),
        in_specs=[hbm] * n, out_specs=tuple([hbm] * n),
        scratch_shapes=[pltpu.SemaphoreType.DMA((n, 6)), pltpu.SemaphoreType.DMA((n, 6)), pltpu.SemaphoreType.DMA((n,))],
    )(*shards)


def _exchange(name, slabs):
    n = len(slabs)

    def body(*refs):
        srcs, dsts = refs[:n], refs[n:2 * n]
        send_sems, recv_sems, local_sems = refs[2 * n:]
        x, y, c = lax.axis_index("x"), lax.axis_index("y"), lax.axis_index("c")
        chip = 2 * x + y
        sibling = (x, y, 1 - c)
        chips = [(1 - x, y), (x, 1 - y), (1 - x, 1 - y)]

        def slot(px, py, pc):
            return 4 * px + 2 * py + pc

        def copy(a, k, src, s, to):
            return pltpu.make_async_remote_copy(
                src_ref=src, dst_ref=dsts[a].at[s], send_sem=send_sems.at[a, k], recv_sem=recv_sems.at[a, k],
                device_id=to, device_id_type=MESH_ID)

        mine = slot(x, y, c)
        local = [pltpu.make_async_copy(srcs[a].at[chip], dsts[a].at[mine], local_sems.at[a]) for a in range(n)]
        for cp in local:
            cp.start()
        first = []
        for a in range(n):
            first.append(copy(a, 0, srcs[a].at[chip], mine, sibling))
            for j, (px, py) in enumerate(chips):
                first.append(copy(a, 1 + j, srcs[a].at[2 * px + py], mine, (px, py, c)))
        for cp in first:
            cp.start()
        passed = []
        for a in range(n):
            for j, (px, py) in enumerate(chips):
                s = slot(px, py, c)
                copy(a, 1 + j, srcs[a].at[chip], s, (x, y, c)).wait_recv()
                fwd = copy(a, 4 + j, dsts[a].at[s], s, sibling)
                fwd.start()
                passed.append(fwd)
        for a in range(n):
            copy(a, 0, srcs[a].at[chip], slot(x, y, 1 - c), (x, y, c)).wait_recv()
            for j, (px, py) in enumerate(chips):
                copy(a, 4 + j, srcs[a].at[chip], slot(px, py, 1 - c), (x, y, c)).wait_recv()
        for cp in first + passed:
            cp.wait_send()
        for cp in local:
            cp.wait()

    hbm = pl.BlockSpec(memory_space=pltpu.HBM)
    return pl.pallas_call(
        body, name=name,
        out_shape=tuple(jax.ShapeDtypeStruct((8,) + a.shape[1:], a.dtype) for a in slabs),
        in_specs=[hbm] * n, out_specs=tuple([hbm] * n),
        scratch_shapes=[pltpu.SemaphoreType.DMA((n, 7)), pltpu.SemaphoreType.DMA((n, 7)), pltpu.SemaphoreType.DMA((n,))],
    )(*slabs)


def _ada(call, w_shard):
    def body(c_ref, w_ref, o_ref):
        o_ref[...] = _mm(c_ref[...].astype(BF16), w_ref[...].astype(BF16))

    return pl.pallas_call(body, name="ada", out_shape=jax.ShapeDtypeStruct((8, w_shard.shape[1]), F32),
                          compiler_params=_cp())(call, w_shard)


def _inproj(x, mod, win, qg, kvg, wq, wk, wv, wkt, wvt, pos, invf):
    S = x.shape[0]
    tm = min(TM, S)

    def body(x_ref, mod_ref, win_ref, qg_ref, kvg_ref, wq_ref, wk_ref, wv_ref, wkt_ref, wvt_ref, pos_ref, invf_ref,
             qlat_ref, ckv_ref, za_ref, xbc_ref, zs_ref, dt_ref, q_ref, k_ref, v_ref, kt_ref, vt_ref):
        shift = mod_ref[0:1, 0:D]
        scale = mod_ref[0:1, D:2 * D]
        u = (x_ref[...] * (1.0 + scale) + shift).astype(BF16)

        def proj(p):
            return _mm(u, win_ref[:, p[0]:p[1]])

        ql = proj(P_Q)
        ckv = proj(P_KV)
        qlat_ref[...] = ql
        ckv_ref[...] = ckv
        za_ref[...] = proj(P_ZA)
        xbc_ref[...] = proj(P_XBC)
        zs_ref[...] = proj(P_ZS)
        dt_ref[...] = proj(P_DT)
        kr = proj(P_KR)

        ang = pos_ref[...].astype(F32) * invf_ref[...]
        cs = jnp.cos(ang)
        sn = jnp.sin(ang)

        rq = lax.rsqrt(jnp.mean(ql * ql, axis=-1, keepdims=True) + RMS_EPS)
        qn = (ql * rq * qg_ref[...]).astype(BF16)
        for h in range(NH):
            qh = _mm(qn, wq_ref[:, h * HP:(h + 1) * HP])
            q_ref[h, :, 0:NOPE] = (qh[:, 0:NOPE] * QSCALE).astype(BF16)
            q_ref[h, :, NOPE:HP] = (_rope(qh[:, NOPE:HP], cs, sn) * QSCALE).astype(BF16)

        rk = lax.rsqrt(jnp.mean(ckv * ckv, axis=-1, keepdims=True) + RMS_EPS)
        kn = (ckv * rk * kvg_ref[...]).astype(BF16)
        knope = _mm(kn, wk_ref[...])
        vall = _mm(kn, wv_ref[...])
        krf = _rope(kr, cs, sn)
        krr = krf.astype(BF16)
        krt = krf.T.astype(BF16)
        ones_rows = jnp.where(lax.broadcasted_iota(jnp.int32, (VDP - VD, tm), 0) == 0, 1.0, 0.0).astype(BF16)
        for h in range(NH):
            k_ref[h, :, 0:NOPE] = knope[:, h * NOPE:(h + 1) * NOPE].astype(BF16)
            k_ref[h, :, NOPE:HP] = krr
            v_ref[h] = vall[:, h * VD:(h + 1) * VD].astype(BF16)
            kt_ref[h, 0:NOPE, :] = _nt(wkt_ref[h * NOPE:(h + 1) * NOPE, :], kn).astype(BF16)
            kt_ref[h, NOPE:HP, :] = krt
            vt_ref[h, 0:VD, :] = _nt(wvt_ref[h * VD:(h + 1) * VD, :], kn).astype(BF16)
            vt_ref[h, VD:VDP, :] = ones_rows

    row = lambda n: pl.BlockSpec((tm, n), lambda i: (i, 0))
    heads = lambda n: pl.BlockSpec((NH, tm, n), lambda i: (0, i, 0))
    heads_t = lambda n: pl.BlockSpec((NH, None, n, tm), lambda i: (0, i, 0, 0))
    sd = lambda n: jax.ShapeDtypeStruct((S, n), F32)
    hd = lambda n: jax.ShapeDtypeStruct((NH, S, n), BF16)
    ht = lambda n: jax.ShapeDtypeStruct((NH, S // tm, n, tm), BF16)
    return pl.pallas_call(
        body, name="inproj", grid=(S // tm,),
        in_specs=[row(D), _const((8, 3 * D)), _const((D, IN_P)), _const((1, QL)), _const((1, KVL)),
                  _const((QL, NH * HP)), _const((KVL, NH * NOPE)), _const((KVL, NH * VD)),
                  _const((NH * NOPE, KVL)), _const((NH * VD, KVL)), row(1), _const((1, 128))],
        out_specs=[row(QL), row(KVL), row(D), row(CC), row(D), row(128), heads(HP), heads(HP), heads(VD),
                   heads_t(HP), heads_t(VDP)],
        out_shape=[sd(QL), sd(KVL), sd(D), sd(CC), sd(D), sd(128), hd(HP), hd(HP), hd(VD), ht(HP), ht(VDP)],
        compiler_params=_cp(("arbitrary",)),
    )(x, mod, win, qg, kvg, wq, wk, wv, wkt, wvt, pos, invf)


def _attn_fwd(q, k, vt):
    _, S, _ = q.shape
    tq = min(TQF, S)
    nq = S // tq
    half = tq // NSP
    tb = vt.shape[3]
    nsb = half // tb

    def body(q_ref, k_ref, vt_ref, o_ref, lse_ref):
        i = pl.program_id(1)
        qb = q_ref[...]

        def scores(j, hb):
            off = pl.multiple_of(j * tq + hb * half, half)
            return _nt(k_ref[pl.ds(off, half), :], qb)

        def update(j, hb, s, carry):
            m, acc = carry
            m_new = jnp.maximum(m, jnp.max(s, axis=0, keepdims=True))
            a = jnp.exp2(m - m_new)
            pb = jnp.exp2(s - m_new).astype(BF16)
            acc = a * acc
            for sb in range(nsb):
                acc = acc + _mm(vt_ref[(NSP * j + hb) * nsb + sb], pb[sb * tb:(sb + 1) * tb, :])
            return m_new, acc

        def trip(j, carry, masked):
            s = [scores(j, hb) for hb in range(NSP)]
            if masked:
                r = lax.broadcasted_iota(jnp.int32, s[0].shape, 0)
                cidx = lax.broadcasted_iota(jnp.int32, s[0].shape, 1)
                s = [jnp.where(cidx >= r + hb * half, s[hb], -1e30) for hb in range(NSP)]
            for hb in range(NSP):
                carry = update(j, hb, s[hb], carry)
            return carry

        def finish(carry):
            m, acc = carry
            l = acc[VD:VD + 1, :]
            o_ref[...] = (acc[0:VD, :] / l).T
            lse_ref[...] = m + jnp.log2(l)

        init = (jnp.full((1, tq), -1e30, F32), jnp.zeros((VDP, tq), F32))
        carry = lax.fori_loop(0, i >> 1, lambda t, cr: trip(2 * t + 1, trip(2 * t, cr, False), False), init)

        @pl.when((i & 1) == 0)
        def _():
            finish(trip(i, carry, True))

        @pl.when((i & 1) == 1)
        def _():
            finish(trip(i, trip(i - 1, carry, False), True))

    return pl.pallas_call(
        body, name="attn_fwd", grid=(NH, nq),
        in_specs=[pl.BlockSpec((None, tq, HP), lambda h, i: (h, i, 0)),
                  pl.BlockSpec((None, S, HP), lambda h, i: (h, 0, 0)),
                  pl.BlockSpec((None, S // tb, VDP, tb), lambda h, i: (h, 0, 0, 0))],
        out_specs=[pl.BlockSpec((tq, VD), lambda h, i: (i, h)),
                   pl.BlockSpec((None, None, 1, tq), lambda h, i: (h, i, 0, 0))],
        out_shape=[jax.ShapeDtypeStruct((S, NH * VD), F32), jax.ShapeDtypeStruct((NH, nq, 1, tq), F32)],
        compiler_params=_cp(("arbitrary", "arbitrary")),
    )(q, k, vt)


def _ssd_consts():
    tri = np.tril(np.ones((CH, CH), np.float32))
    e16 = np.zeros((128, SW), np.float32)
    for h in range(SH):
        e16[h, h * SP:(h + 1) * SP] = 1.0
    return jnp.asarray(tri, BF16), jnp.asarray(tri.T.copy(), BF16), jnp.asarray(e16, BF16)


def _ssd_conv(xbc_ref, halo_ref, cw_ref, cb_ref, ext, first):
    ext[0:8, :] = jnp.where(first, 0.0, halo_ref[...])
    ext[8:8 + CH, :] = xbc_ref[...]
    cw = cw_ref[...]
    xc = cb_ref[...] + cw[0:1, :] * ext[5:5 + CH, :]
    for kk in range(1, CW):
        xc = xc + cw[kk:kk + 1, :] * ext[5 + kk:5 + kk + CH, :]
    return xc


def _ssd_chunk_common(xc, dtraw_ref, dtb_ref, alog_ref, tri_ref, e16_ref):
    sact = _sigmoid(xc)
    act = xc * sact
    lane = lax.broadcasted_iota(jnp.int32, (1, 128), 1)
    arow = jnp.where(lane < SH, -jnp.exp(alog_ref[...]), 0.0)
    dtpre = dtraw_ref[...] + dtb_ref[...]
    dt = _softplus(dtpre)
    a = dt * arow
    cum = _xmm(tri_ref[...], a, 3)
    cumx = _mm_x(cum, e16_ref[...], 3)
    dtx = _mm_x(dt, e16_ref[...], 2)
    return sact, act, arow, dtpre, dt, cum, cumx, dtx


def _ssd_fwd(xbc, dtraw, zs, conv_w, conv_b, dtb, alog, dskx, gssm):
    S = xbc.shape[0]
    nc = S // CH
    tri, _, e16 = _ssd_consts()

    def body(xbc_ref, halo_ref, dtraw_ref, zs_ref, cw_ref, cb_ref, dtb_ref, alog_ref, dsk_ref, g_ref, tri_ref, e16_ref,
             xc_ref, y_ref, htp_ref, o_ref, ht, ext):
        i = pl.program_id(0)

        @pl.when(i == 0)
        def _():
            ht[...] = jnp.zeros_like(ht)

        xc = _ssd_conv(xbc_ref, halo_ref, cw_ref, cb_ref, ext, i == 0)
        xc_ref[...] = xc
        sact, act, arow, dtpre, dt, cum, cumx, dtx = _ssd_chunk_common(xc, dtraw_ref, dtb_ref, alog_ref, tri_ref, e16_ref)
        cum_t = cum.T
        xs = act[:, 0:SW]
        lastx = cumx[CH - 1:CH, :]
        xh = xs * dtx
        eexp = jnp.exp(cumx)
        dte = jnp.exp(lastx - cumx)
        cdx = jnp.exp(lastx)
        htp = ht[...]
        htp_ref[...] = htp
        xw = (xh * dte).astype(BF16)
        xb = xh.astype(BF16)
        trim = tri_ref[...].astype(F32) > 0.5
        lane = lax.broadcasted_iota(jnp.int32, (CH, 128), 1)
        parts = []
        for g in range(SG):
            gl = slice(g * GW, (g + 1) * GW)
            bg = act[:, SW + g * SN:SW + (g + 1) * SN].astype(BF16)
            cg = act[:, SW + SG * SN + g * SN:SW + SG * SN + (g + 1) * SN].astype(BF16)
            cbm = _nt(cg, bg)
            yoff = eexp[:, gl] * _mm(cg, htp[:, gl].astype(BF16))
            ht[:, gl] = htp[:, gl] * cdx[:, gl] + _tn(bg, xw[:, gl])
            for pr in range(GW // 128):
                h0 = g * (SH // SG) + 2 * pr
                lo = g * GW + pr * 128
                xp = xb[:, lo:lo + 128]
                res = []
                for hh in (h0, h0 + 1):
                    seg = cum[:, hh:hh + 1] - cum_t[hh:hh + 1, :]
                    mh = jnp.where(trim, cbm * jnp.exp(seg), 0.0).astype(BF16)
                    res.append(_mm(mh, xp))
                parts.append(jnp.where(lane < SP, res[0], res[1]) + yoff[:, pr * 128:(pr + 1) * 128])
        y = jnp.concatenate(parts, axis=1) + xs * dsk_ref[...]
        y_ref[...] = y
        z = zs_ref[...]
        hf = y * (z * _sigmoid(z))
        outs = []
        for g in range(SG):
            hg = hf[:, g * GW:(g + 1) * GW]
            rs = lax.rsqrt(jnp.mean(hg * hg, axis=-1, keepdims=True) + RMS_EPS)
            outs.append(hg * rs)
        o_ref[...] = (jnp.concatenate(outs, axis=1) * g_ref[...]).astype(BF16)

    row = lambda n: pl.BlockSpec((CH, n), lambda i: (i, 0))
    return pl.pallas_call(
        body, name="ssd_fwd", grid=(nc,),
        in_specs=[row(CC), pl.BlockSpec((8, CC), lambda i: (jnp.maximum(i * (CH // 8) - 1, 0), 0)), row(128), row(SW),
                  _const((CW, CC)), _const((1, CC)), _const((1, 128)), _const((1, 128)), _const((1, SW)), _const((1, SW)),
                  _const((CH, CH)), _const((128, SW))],
        out_specs=[row(CC), row(SW), pl.BlockSpec((None, SN, SW), lambda i: (i, 0, 0)), row(SW)],
        out_shape=[jax.ShapeDtypeStruct((S, CC), F32), jax.ShapeDtypeStruct((S, SW), F32),
                   jax.ShapeDtypeStruct((nc, SN, SW), F32), jax.ShapeDtypeStruct((S, SW), BF16)],
        scratch_shapes=[pltpu.VMEM((SN, SW), F32), pltpu.VMEM((8 + CH, CC), F32)],
        compiler_params=_cp(("arbitrary",)),
    )(xbc, xbc, dtraw, zs, conv_w, conv_b, dtb, alog, dskx, gssm, tri, e16)


def _outproj(o, za, ossm, x, tgt, wout, mod, ln_g, ln_b):
    S = x.shape[0]
    tm = min(TM, S)

    e8 = np.zeros((D, 128), np.float32)
    for h in range(NH):
        e8[h * VD:(h + 1) * VD, h] = 1.0
    e8 = jnp.asarray(e8, BF16)

    def body(o_ref, za_ref, os_ref, x_ref, t_ref, w_ref, mod_ref, g_ref, b_ref, e8_ref,
             gx_ref, do_ref, dza_ref, dos_ref, delta_ref, dw_ref, vec_ref):
        i = pl.program_id(0)

        @pl.when(i == 0)
        def _():
            dw_ref[...] = jnp.zeros_like(dw_ref)
            vec_ref[...] = jnp.zeros_like(vec_ref)

        gate = mod_ref[0:1, 2 * D:3 * D]
        ov = o_ref[...]
        z = za_ref[...]
        sz = _sigmoid(z)
        silz = z * sz
        a = (ov * silz).astype(BF16)
        osb = os_ref[...]
        mixed = _mm(a, w_ref[0:D, :]) + _mm(osb, w_ref[D:MIX, :])
        xv = x_ref[...]
        hres = ALPHA * xv + gate * mixed
        mu = jnp.mean(hres, axis=-1, keepdims=True)
        hc = hres - mu
        var = jnp.mean(hc * hc, axis=-1, keepdims=True)
        rstd = lax.rsqrt(var + LN_EPS)
        xhat = hc * rstd
        g = g_ref[...]
        yv = xhat * g + b_ref[...]
        err = yv - t_ref[...]
        dy = err * (1.0 / D)
        vec_ref[0:1, :] += jnp.sum(err * err, axis=0, keepdims=True)
        vec_ref[1:2, :] += jnp.sum(dy * xhat, axis=0, keepdims=True)
        vec_ref[2:3, :] += jnp.sum(dy, axis=0, keepdims=True)
        dxh = dy * g
        dh = rstd * (dxh - jnp.mean(dxh, axis=-1, keepdims=True) - xhat * jnp.mean(dxh * xhat, axis=-1, keepdims=True))
        gx_ref[...] = ALPHA * dh
        vec_ref[3:4, :] += jnp.sum(dh * mixed, axis=0, keepdims=True)
        dmixed = (gate * dh).astype(BF16)
        dw_ref[0:D, :] += _tn(a, dmixed)
        dw_ref[D:MIX, :] += _tn(osb, dmixed)
        da = _nt(dmixed, w_ref[0:D, :])
        dos_ref[...] = _nt(dmixed, w_ref[D:MIX, :])
        dov = da * silz
        do_ref[...] = dov.astype(BF16)
        dza_ref[...] = (da * ov * (sz * (1.0 + z * (1.0 - sz)))).astype(BF16)
        delta_ref[:, 0, :] = _mm_x(dov * ov, e8_ref[...], 2).T[0:NH, :]

    row = lambda n: pl.BlockSpec((tm, n), lambda i: (i, 0))
    return pl.pallas_call(
        body, name="outproj", grid=(S // tm,),
        in_specs=[row(D), row(D), row(D), row(D), row(D), _const((MIX, D)), _const((8, 3 * D)), _const((1, D)), _const((1, D)),
                  _const((D, 128))],
        out_specs=[row(D), row(D), row(D), row(D), pl.BlockSpec((NH, None, 1, tm), lambda i: (0, i, 0, 0)),
                   _full((MIX, D)), _full((8, D))],
        out_shape=[jax.ShapeDtypeStruct((S, D), F32), jax.ShapeDtypeStruct((S, D), BF16), jax.ShapeDtypeStruct((S, D), BF16),
                   jax.ShapeDtypeStruct((S, D), F32), jax.ShapeDtypeStruct((NH, S // tm, 1, tm), F32),
                   jax.ShapeDtypeStruct((MIX, D), F32), jax.ShapeDtypeStruct((8, D), F32)],
        compiler_params=_cp(("arbitrary",)),
    )(o, za, ossm, x, tgt, wout, mod, ln_g, ln_b, e8)


def _attn_bwd(q, k, kt, v, do, lse, delta):
    _, S, _ = q.shape
    tk = min(TQ, S // 2)
    nk = S // tk
    tq = 2 * tk
    nq = S // tq
    tb = kt.shape[3]
    nsb = tk // tb

    def body(k_ref, kt_ref, v_ref, q_ref, do_ref, lse_ref, dl_ref, dk_ref, dv_ref, dq_ref, dqt_ref):
        j = pl.program_id(1)
        kb = k_ref[...]
        vb = v_ref[...]

        @pl.when(j == 0)
        def _():
            dqt_ref[...] = jnp.zeros_like(dqt_ref)

        dk_ref[...] = jnp.zeros_like(dk_ref)
        dv_ref[...] = jnp.zeros_like(dv_ref)

        def step(i, masked, lo=0):
            off = pl.multiple_of(i * tq + lo, tk)
            qb = q_ref[pl.ds(off, tq - lo), :]
            dob = do_ref[pl.ds(off, tq - lo), :]
            pt = jnp.exp2(_nt(kb, qb) - lse_ref[i][:, lo:tq])
            if masked:
                r = lax.broadcasted_iota(jnp.int32, pt.shape, 0)
                cidx = lax.broadcasted_iota(jnp.int32, pt.shape, 1)
                pt = jnp.where(i * tq + lo + cidx >= j * tk + r, pt, 0.0)
            dv_ref[...] += _mm(pt.astype(BF16), dob)
            dsb = (pt * (_nt(vb, dob) - dl_ref[i][:, lo:tq])).astype(BF16)
            dk_ref[...] += _mm(dsb, qb)
            acc = dqt_ref[i, :, lo:tq]
            for sb in range(nsb):
                acc = acc + _mm(kt_ref[sb], dsb[sb * tb:(sb + 1) * tb, :])
            dqt_ref[i, :, lo:tq] = acc

        first = j >> 1

        @pl.when((j & 1) == 0)
        def _():
            step(first, True)

        @pl.when((j & 1) == 1)
        def _():
            step(first, True, tk)
            dq_ref[...] = dqt_ref[first].T

        def loop_body(i, carry):
            step(i, False)
            return carry

        lax.fori_loop(first + 1, nq, loop_body, 0)
        dk_ref[...] = dk_ref[...] * LN2

    return pl.pallas_call(
        body, name="attn_bwd", grid=(NH, nk),
        in_specs=[pl.BlockSpec((None, tk, HP), lambda h, j: (h, j, 0)),
                  pl.BlockSpec((None, nsb, HP, tb), lambda h, j: (h, j, 0, 0)),
                  pl.BlockSpec((None, tk, VD), lambda h, j: (h, j, 0)),
                  pl.BlockSpec((None, S, HP), lambda h, j: (h, 0, 0)),
                  pl.BlockSpec((S, VD), lambda h, j: (0, h)),
                  pl.BlockSpec((None, nq, 1, tq), lambda h, j: (h, 0, 0, 0)),
                  pl.BlockSpec((None, nq, 1, tq), lambda h, j: (h, 0, 0, 0))],
        out_specs=[pl.BlockSpec((None, tk, HP), lambda h, j: (h, j, 0)),
                   pl.BlockSpec((None, tk, VD), lambda h, j: (h, j, 0)),
                   pl.BlockSpec((None, tq, HP), lambda h, j: (h, j >> 1, 0))],
        out_shape=[jax.ShapeDtypeStruct((NH, S, HP), F32), jax.ShapeDtypeStruct((NH, S, VD), F32),
                   jax.ShapeDtypeStruct((NH, S, HP), F32)],
        scratch_shapes=[pltpu.VMEM((nq, HP, tq), F32)],
        compiler_params=_cp(("arbitrary", "arbitrary")),
    )(k, kt, v, q, do, lse.reshape(NH, nq, 1, tq), delta.reshape(NH, nq, 1, tq))


def _ssd_bwd(xbc, xc, dtraw, zs, y, htp, dossm, conv_w, dtb, alog, dskx, gssm):
    S = xbc.shape[0]
    nc = S // CH
    tri, triu, e16 = _ssd_consts()

    def body(xbc_ref, xc_ref, dtraw_ref, zs_ref, y_ref, htp_ref, dos_ref,
             cw_ref, dtb_ref, alog_ref, dsk_ref, g_ref, tri_ref, triu_ref, e16_ref,
             dxbc_ref, ddt_ref, dzs_ref, dcw_ref, dcb_ref, dvec_ref, dg_ref,
             dht, dext, dskacc):
        r = pl.program_id(0)

        @pl.when(r == 0)
        def _():
            dht[...] = jnp.zeros_like(dht)
            dext[CH:CH + 8, :] = jnp.zeros((8, CC), F32)
            dskacc[...] = jnp.zeros_like(dskacc)
            dcw_ref[...] = jnp.zeros_like(dcw_ref)
            dcb_ref[...] = jnp.zeros_like(dcb_ref)
            dvec_ref[...] = jnp.zeros_like(dvec_ref)
            dg_ref[...] = jnp.zeros_like(dg_ref)

        xc = xc_ref[...]
        sact, act, arow, dtpre, dt, cum, cumx, dtx = _ssd_chunk_common(xc, dtraw_ref, dtb_ref, alog_ref, tri_ref, e16_ref)
        cum_t = cum.T
        xs = act[:, 0:SW]
        lastx = cumx[CH - 1:CH, :]
        xh = xs * dtx
        eexp = jnp.exp(cumx)
        dte = jnp.exp(lastx - cumx)
        cdx = jnp.exp(lastx)
        trim = tri_ref[...].astype(F32) > 0.5
        lane = lax.broadcasted_iota(jnp.int32, (CH, 128), 1)
        rowi = lax.broadcasted_iota(jnp.int32, (CH, 128), 0)

        yv = y_ref[...]
        z = zs_ref[...]
        sz = _sigmoid(z)
        silz = z * sz
        hf = yv * silz
        dn = dos_ref[...] * g_ref[...]
        dhf_parts, nrm_parts = [], []
        for g in range(SG):
            gl = slice(g * GW, (g + 1) * GW)
            hg = hf[:, gl]
            rs = lax.rsqrt(jnp.mean(hg * hg, axis=-1, keepdims=True) + RMS_EPS)
            ng = hg * rs
            dng = dn[:, gl]
            dhf_parts.append(rs * (dng - ng * jnp.mean(dng * ng, axis=-1, keepdims=True)))
            nrm_parts.append(ng)
        nrm = jnp.concatenate(nrm_parts, axis=1)
        dhf = jnp.concatenate(dhf_parts, axis=1)
        dg_ref[...] += jnp.sum(dos_ref[...] * nrm, axis=0, keepdims=True)
        dyv = dhf * silz
        dzs_ref[...] = (dhf * yv * (sz * (1.0 + z * (1.0 - sz)))).astype(BF16)
        dskacc[...] += jnp.sum(dyv * xs, axis=0, keepdims=True)
        dxs_skip = dyv * dsk_ref[...]

        dhtn = dht[...]
        hp = htp_ref[...]
        dlastx = jnp.sum(dhtn * hp, axis=0, keepdims=True) * cdx
        xb = xh.astype(BF16)
        xwf = xh * dte
        dcum = jnp.zeros((CH, 128), F32)
        dcum_t = jnp.zeros((128, CH), F32)
        dxh_parts, dcumx_parts, dlast_parts, db_parts, dc_parts = [], [], [], [], []
        for g in range(SG):
            gl = slice(g * GW, (g + 1) * GW)
            bg = act[:, SW + g * SN:SW + (g + 1) * SN].astype(BF16)
            cg = act[:, SW + SG * SN + g * SN:SW + SG * SN + (g + 1) * SN].astype(BF16)
            hpg = hp[:, gl].astype(BF16)
            dhn = dhtn[:, gl].astype(BF16)
            dyg = dyv[:, gl]
            dz = (dyg * eexp[:, gl]).astype(BF16)
            dcg = _nt(dz, hpg)
            dht[:, gl] = dhtn[:, gl] * cdx[:, gl] + _tn(cg, dz)
            yoff = eexp[:, gl] * _mm(cg, hpg)
            dcumx_g = dyg * yoff
            dbg = _nt(xwf[:, gl].astype(BF16), dhn)
            dxw = _mm(bg, dhn)
            ddte = dxw * xwf[:, gl]
            dcumx_parts.append(dcumx_g - ddte)
            dlast_parts.append(jnp.sum(ddte, axis=0, keepdims=True))
            dxh_g = dxw * dte[:, gl]
            cbm = _nt(cg, bg)
            dcb = jnp.zeros((CH, CH), F32)
            dxp_parts = []
            for pr in range(GW // 128):
                h0 = g * (SH // SG) + 2 * pr
                lo = g * GW + pr * 128
                xp = xb[:, lo:lo + 128]
                dyp = dyv[:, lo:lo + 128]
                dxp = jnp.zeros((CH, 128), F32)
                for idx, hh in enumerate((h0, h0 + 1)):
                    decay = jnp.where(trim, jnp.exp(cum[:, hh:hh + 1] - cum_t[hh:hh + 1, :]), 0.0)
                    mh = cbm * decay
                    keep = (lane < SP) if idx == 0 else (lane >= SP)
                    dym = jnp.where(keep, dyp, 0.0).astype(BF16)
                    dm = _nt(dym, xp)
                    dxp = dxp + _tn(mh.astype(BF16), dym)
                    gm = dm * mh
                    dcum = dcum + jnp.where(lane == hh, jnp.sum(gm, axis=1, keepdims=True), 0.0)
                    dcum_t = dcum_t - jnp.where(rowi == hh, jnp.sum(gm, axis=0, keepdims=True), 0.0)
                    dcb = dcb + dm * decay
                dxp_parts.append(dxp)
            dxh_parts.append(dxh_g + jnp.concatenate(dxp_parts, axis=1))
            dcbb = dcb.astype(BF16)
            dc_parts.append(dcg + _mm(dcbb, bg))
            db_parts.append(dbg + _tn(dcbb, cg))
        dxh = jnp.concatenate(dxh_parts, axis=1)
        dcumx = jnp.concatenate(dcumx_parts, axis=1)
        dlastx = dlastx + jnp.concatenate(dlast_parts, axis=1)
        e16 = e16_ref[...]
        dlast128 = _nt_x(jnp.broadcast_to(dlastx, (8, SW)), e16, 2)[0:1, :]
        dcum = dcum + dcum_t.T + _nt_x(dcumx, e16, 2) + jnp.where(rowi == CH - 1, dlast128, 0.0)
        da = _xmm(triu_ref[...], dcum, 2)
        ddt = da * arow + _nt_x(dxh * xs, e16, 2)
        dvec_ref[1:2, :] += jnp.sum(da * dt, axis=0, keepdims=True)
        ddtraw = jnp.where(lane < SH, ddt * _sigmoid(dtpre), 0.0)
        dvec_ref[0:1, :] += jnp.sum(ddtraw, axis=0, keepdims=True)
        ddt_ref[...] = ddtraw.astype(BF16)
        dxs = dxs_skip + dxh * dtx
        dact = jnp.concatenate([dxs] + db_parts + dc_parts, axis=1)
        dxc = dact * (sact * (1.0 + xc * (1.0 - sact)))

        dcb_ref[...] += jnp.sum(dxc, axis=0, keepdims=True)
        dext[0:CH, :] = dxc
        cw = cw_ref[...]
        xraw = xbc_ref[...]
        dxr = cw[CW - 1:CW, :] * dxc
        dcw_ref[CW - 1:CW, :] += jnp.sum(dxc * xraw, axis=0, keepdims=True)
        for kk in range(CW - 1):
            dwin = dext[CW - 1 - kk:CW - 1 - kk + CH, :]
            dxr = dxr + cw[kk:kk + 1, :] * dwin
            dcw_ref[kk:kk + 1, :] += jnp.sum(dwin * xraw, axis=0, keepdims=True)
        dxbc_ref[...] = dxr.astype(BF16)
        dext[CH:CH + 8, :] = dxc[0:8, :]

        @pl.when(r == nc - 1)
        def _():
            dvec_ref[1:2, :] = dvec_ref[1:2, :] * arow
            dvec_ref[2:3, :] = _nt_x(jnp.broadcast_to(dskacc[...], (8, SW)), e16, 3)[0:1, :]

    rev = lambda n: pl.BlockSpec((CH, n), lambda r: (nc - 1 - r, 0))
    return pl.pallas_call(
        body, name="ssd_bwd", grid=(nc,),
        in_specs=[rev(CC), rev(CC),
                  rev(128), rev(SW), rev(SW), pl.BlockSpec((None, SN, SW), lambda r: (nc - 1 - r, 0, 0)), rev(SW),
                  _const((CW, CC)), _const((1, 128)), _const((1, 128)), _const((1, SW)), _const((1, SW)),
                  _const((CH, CH)), _const((CH, CH)), _const((128, SW))],
        out_specs=[rev(CC), rev(128), rev(SW), _full((CW, CC)), _full((1, CC)), _full((8, 128)), _full((1, SW))],
        out_shape=[jax.ShapeDtypeStruct((S, CC), BF16), jax.ShapeDtypeStruct((S, 128), BF16), jax.ShapeDtypeStruct((S, SW), BF16),
                   jax.ShapeDtypeStruct((CW, CC), F32), jax.ShapeDtypeStruct((1, CC), F32),
                   jax.ShapeDtypeStruct((8, 128), F32), jax.ShapeDtypeStruct((1, SW), F32)],
        scratch_shapes=[pltpu.VMEM((SN, SW), F32), pltpu.VMEM((CH + 8, CC), F32), pltpu.VMEM((1, SW), F32)],
        compiler_params=_cp(("arbitrary",)),
    )(xbc, xc, dtraw, zs, y, htp, dossm, conv_w, dtb, alog, dskx, gssm, tri, triu, e16)


def _mla_bwd(dq, dk, dv, qlat, ckv, qg, kvg, wq, wk, wv, pos, invf):
    S = qlat.shape[0]
    tm = min(TQ, S)

    def body(dq_ref, dk_ref, dv_ref, ql_ref, ckv_ref, qg_ref, kvg_ref, wq_ref, wk_ref, wv_ref, pos_ref, invf_ref,
             dql_ref, dckv_ref, dkr_ref, dwq_ref, dwk_ref, dwv_ref, dqg_ref, dkvg_ref):
        i = pl.program_id(0)

        @pl.when(i == 0)
        def _():
            dwq_ref[...] = jnp.zeros_like(dwq_ref)
            dwk_ref[...] = jnp.zeros_like(dwk_ref)
            dwv_ref[...] = jnp.zeros_like(dwv_ref)
            dqg_ref[...] = jnp.zeros_like(dqg_ref)
            dkvg_ref[...] = jnp.zeros_like(dkvg_ref)

        ang = pos_ref[...].astype(F32) * invf_ref[...]
        cs = jnp.cos(ang)
        sn = jnp.sin(ang)

        def rms_bwd(v, g, dn, dg_ref):
            r = lax.rsqrt(jnp.mean(v * v, axis=-1, keepdims=True) + RMS_EPS)
            vh = v * r
            dg_ref[...] += jnp.sum(dn * vh, axis=0, keepdims=True)
            dvh = dn * g
            return vh, r * (dvh - vh * jnp.mean(dvh * vh, axis=-1, keepdims=True))

        pieces = []
        for h in range(NH):
            dqh = dq_ref[h]
            pieces.append(dqh[:, 0:NOPE] * SCALE)
            pieces.append(_rope_t(dqh[:, NOPE:HP], cs, sn) * SCALE)
        dqf = jnp.concatenate(pieces, axis=1).astype(BF16)
        ql = ql_ref[...]
        g = qg_ref[...]
        dqn = _nt(dqf, wq_ref[...])
        qh, dql = rms_bwd(ql, g, dqn, dqg_ref)
        dwq_ref[...] += _tn((qh * g).astype(BF16), dqf)
        dql_ref[...] = dql.astype(BF16)

        dkn_p = jnp.concatenate([dk_ref[h, :, 0:NOPE] for h in range(NH)], axis=1).astype(BF16)
        dvf = jnp.concatenate([dv_ref[h] for h in range(NH)], axis=1).astype(BF16)
        dkr = dk_ref[0, :, NOPE:HP]
        for h in range(1, NH):
            dkr = dkr + dk_ref[h, :, NOPE:HP]
        lane = lax.broadcasted_iota(jnp.int32, dkr.shape, 1)
        dkr_ref[...] = jnp.where(lane < ROPE, _rope_t(dkr, cs, sn), 0.0).astype(BF16)
        cv = ckv_ref[...]
        gk = kvg_ref[...]
        dkn = _nt(dkn_p, wk_ref[...]) + _nt(dvf, wv_ref[...])
        kh, dckv = rms_bwd(cv, gk, dkn, dkvg_ref)
        knb = (kh * gk).astype(BF16)
        dwk_ref[...] += _tn(knb, dkn_p)
        dwv_ref[...] += _tn(knb, dvf)
        dckv_ref[...] = dckv.astype(BF16)

    row = lambda n: pl.BlockSpec((tm, n), lambda i: (i, 0))
    heads = lambda n: pl.BlockSpec((NH, tm, n), lambda i: (0, i, 0))
    return pl.pallas_call(
        body, name="mla_bwd", grid=(S // tm,),
        in_specs=[heads(HP), heads(HP), heads(VD), row(QL), row(KVL), _const((1, QL)), _const((1, KVL)),
                  _const((QL, NH * HP)), _const((KVL, NH * NOPE)), _const((KVL, NH * VD)), row(1), _const((1, 128))],
        out_specs=[row(QL), row(KVL), row(128), _full((QL, NH * HP)), _full((KVL, NH * NOPE)), _full((KVL, NH * VD)),
                   _full((1, QL)), _full((1, KVL))],
        out_shape=[jax.ShapeDtypeStruct((S, QL), BF16), jax.ShapeDtypeStruct((S, KVL), BF16), jax.ShapeDtypeStruct((S, 128), BF16),
                   jax.ShapeDtypeStruct((QL, NH * HP), F32), jax.ShapeDtypeStruct((KVL, NH * NOPE), F32),
                   jax.ShapeDtypeStruct((KVL, NH * VD), F32), jax.ShapeDtypeStruct((1, QL), F32), jax.ShapeDtypeStruct((1, KVL), F32)],
        compiler_params=_cp(("arbitrary",)),
    )(dq, dk, dv, qlat, ckv, qg, kvg, wq, wk, wv, pos, invf)


def _inproj_bwd(x, gx1, mod, win, dql, dckv, dza, dxbc, dzs, dkr, ddt):
    S = x.shape[0]
    tm = min(TM, S)

    def body(x_ref, gx1_ref, mod_ref, win_ref, dql_ref, dckv_ref, dza_ref, dxbc_ref, dzs_ref, dkr_ref, ddt_ref,
             gx_ref, dw_ref, vec_ref):
        i = pl.program_id(0)

        @pl.when(i == 0)
        def _():
            dw_ref[...] = jnp.zeros_like(dw_ref)
            vec_ref[...] = jnp.zeros_like(vec_ref)

        shift = mod_ref[0:1, 0:D]
        scale = mod_ref[0:1, D:2 * D]
        xv = x_ref[...]
        ut = (xv * (1.0 + scale) + shift).T.astype(BF16)
        pieces = (dql_ref, dckv_ref, dza_ref, dxbc_ref, dzs_ref, dkr_ref, ddt_ref)
        du = jnp.zeros((tm, D), F32)
        lo = 0
        for p_ref in pieces:
            n = p_ref.shape[1]
            dp = p_ref[...]
            du = du + _nt(dp, win_ref[:, lo:lo + n])
            dw_ref[:, lo:lo + n] += _mm(ut, dp)
            lo += n
        vec_ref[0:1, :] += jnp.sum(du, axis=0, keepdims=True)
        vec_ref[1:2, :] += jnp.sum(du * xv, axis=0, keepdims=True)
        gx_ref[...] = gx1_ref[...] + du * (1.0 + scale)

    row = lambda n: pl.BlockSpec((tm, n), lambda i: (i, 0))
    return pl.pallas_call(
        body, name="inproj_bwd", grid=(S // tm,),
        in_specs=[row(D), row(D), _const((8, 3 * D)), _const((D, IN_P)), row(QL), row(KVL), row(D), row(CC), row(D),
                  row(128), row(128)],
        out_specs=[row(D), pl.BlockSpec((D, IN_P), lambda i: (0, 0), pipeline_mode=pl.Buffered(1)), _full((8, D))],
        out_shape=[jax.ShapeDtypeStruct((S, D), F32), jax.ShapeDtypeStruct((D, IN_P), F32), jax.ShapeDtypeStruct((8, D), F32)],
        compiler_params=_cp(("arbitrary",)),
    )(x, gx1, mod, win, dql, dckv, dza, dxbc, dzs, dkr, ddt)


def _ada_bwd(callt, dmods):
    w = dmods.shape[1]

    def body(c_ref, d_ref, o_ref):
        acc = c_ref[:, 0:1] * d_ref[0:1, :]
        for s in range(1, 8):
            acc = acc + c_ref[:, s:s + 1] * d_ref[s:s + 1, :]
        o_ref[0] = acc

    return pl.pallas_call(body, name="ada_bwd", out_shape=jax.ShapeDtypeStruct((1, D, w), F32),
                          compiler_params=_cp())(callt, dmods)


def _adamw(name, parts, w, m, v):
    rows, ncol = w.shape
    tr = min(rows, 128)
    nparts = parts.shape[0]

    def body(p_ref, w_ref, m_ref, v_ref, g_ref, d_ref, nm_ref, nv_ref):
        g = p_ref[0].astype(F32)
        for s in range(1, nparts):
            g = g + p_ref[s].astype(F32)
        g_ref[...] = g
        nm = B1 * m_ref[...] + (1.0 - B1) * g
        nv = B2 * v_ref[...] + (1.0 - B2) * (g * g)
        nm_ref[...] = nm
        nv_ref[...] = nv
        m_hat = nm / (1.0 - B1 ** STEP)
        v_hat = nv / (1.0 - B2 ** STEP)
        d_ref[...] = -LR * (m_hat / (jnp.sqrt(v_hat) + EPS) + WD * w_ref[...])

    row = pl.BlockSpec((tr, ncol), lambda i: (i, 0))
    sd = jax.ShapeDtypeStruct((rows, ncol), F32)
    return pl.pallas_call(
        body, name="adamw_" + name, grid=(rows // tr,),
        in_specs=[pl.BlockSpec((nparts, tr, ncol), lambda i: (0, i, 0)), row, row, row],
        out_specs=[row, row, row, row], out_shape=[sd, sd, sd, sd],
        compiler_params=_cp(("arbitrary",)),
    )(parts, w, m, v)


_SMALL = (("b_ada", 3 * D), ("conv_w", CW * CC // 4), ("conv_b", CC), ("ssm_norm_g", SW), ("ln_g", D), ("ln_b", D),
          ("q_norm_g", QL), ("kv_norm_g", KVL), ("dt_bias", SH), ("a_log", SH), ("d_skip", SH), ("loss", 128))


def _pack_small(d, lead):
    flat = [d[name].reshape(d[name].shape[:lead] + (-1,)) for name, _ in _SMALL]
    used = sum(f.shape[lead] for f in flat)
    pad = jnp.zeros(flat[0].shape[:lead] + (R_SMALL * 1024 - used,), F32)
    return jnp.concatenate(flat + [pad], axis=lead).reshape(flat[0].shape[:lead] + (R_SMALL, 1024))


def _unpack_small(p):
    flat = p.reshape(-1)
    out, r = {}, 0
    for name, n in _SMALL:
        out[name] = flat[r:r + n]
        r += n
    return out


def _in_to_padded(w):
    z = lambda n: jnp.zeros((w.shape[0], n), w.dtype)
    return jnp.concatenate([w[:, 0:384], w[:, 384:640], w[:, 704:1728], w[:, 1728:3264], w[:, 3280:4304],
                            w[:, 640:704], z(64), w[:, 3264:3280], z(112)], axis=1)


def _in_from_padded(g):
    return jnp.concatenate([g[:, 0:384], g[:, 384:640], g[:, P_KR[0]:P_KR[0] + 64], g[:, 640:1664], g[:, 1664:3200],
                            g[:, P_DT[0]:P_DT[0] + 16], g[:, 3200:4224]], axis=1)


def kernel(x, c, positions, w_ada, b_ada, w_in, q_norm_g, w_qb, kv_norm_g, w_kvb, conv_w, conv_b, dt_bias, a_log, d_skip, ssm_norm_g, w_out, ln_g, ln_b, loss_target, m_w_ada, m_b_ada, m_w_in, m_q_norm_g, m_w_qb, m_kv_norm_g, m_w_kvb, m_conv_w, m_conv_b, m_dt_bias, m_a_log, m_d_skip, m_ssm_norm_g, m_w_out, m_ln_g, m_ln_b, v_w_ada, v_b_ada, v_w_in, v_q_norm_g, v_w_qb, v_kv_norm_g, v_w_kvb, v_conv_w, v_conv_b, v_dt_bias, v_a_log, v_d_skip, v_ssm_norm_g, v_w_out, v_ln_g, v_ln_b):
    S = x.shape[1]
    xv = x[0]
    tgt = loss_target[0]

    cw16 = jnp.concatenate([conv_w[0], jnp.zeros((16 - CW, CC // 4), F32)], axis=0)
    f_in, f_qb, f_kvb, f_out, f_cw = _gather_weights(
        [w_in[0].astype(BF16), w_qb[0].astype(BF16), w_kvb[0].astype(BF16), w_out[0].astype(BF16), cw16])
    cat1 = lambda f: f.transpose(1, 0, 2).reshape(f.shape[1], 4 * f.shape[2])
    win = _in_to_padded(cat1(f_in))
    wqb = cat1(f_qb).reshape(QL, NH, QKD)
    wq = jnp.concatenate([wqb, jnp.zeros((QL, NH, HP - QKD), BF16)], axis=2).reshape(QL, NH * HP)
    wkvb = cat1(f_kvb).reshape(KVL, NH, NOPE + VD)
    wk = wkvb[:, :, 0:NOPE].reshape(KVL, NH * NOPE)
    wv = wkvb[:, :, NOPE:].reshape(KVL, NH * VD)
    wout = f_out.reshape(MIX, D)
    cwf = cat1(f_cw[:, 0:CW, :])

    half = ROPE // 2
    invf = 1.0 / (ROPE_THETA ** (jnp.arange(half, dtype=F32) / half))
    invf = jnp.concatenate([invf, invf, jnp.zeros((128 - ROPE,), F32)]).reshape(1, 128)
    pos = positions.reshape(S, 1)
    pad128 = lambda a: jnp.concatenate([a.reshape(1, SH), jnp.zeros((1, 128 - SH), F32)], axis=1)
    dtb, alog = pad128(dt_bias), pad128(a_log)
    dskx = jnp.repeat(d_skip.reshape(SH), SP).reshape(1, SW)

    my_c = lax.axis_index("c")
    (call,) = _exchange("gather_c", [jnp.broadcast_to(c.reshape(1, 1, D), (4, 1, D))])
    call = call.reshape(8, D)
    mods = _ada(call, w_ada[0])
    (mrows,) = _exchange("scatter_mod", [mods.reshape(4, 2, 3 * D // 4)])
    mine = lax.dynamic_index_in_dim(mrows.reshape(4, 2, 2, 3 * D // 4)[:, 0], my_c, axis=1, keepdims=False)
    mod = jnp.broadcast_to(mine.reshape(1, 3 * D) + b_ada, (8, 3 * D))
    qlat, ckv, za, xbc, zs, dtraw, q, k, v, kt, vt = _inproj(xv, mod, win, q_norm_g, kv_norm_g, wq, wk, wv, wk.T, wv.T, pos, invf)
    o, lse = _attn_fwd(q, k, vt)
    xc, y, htp, ossm = _ssd_fwd(xbc, dtraw, zs, cwf, conv_b, dtb, alog, dskx, ssm_norm_g)
    gx1, do, dza, dossm, delta, dwout, vec_o = _outproj(o, za, ossm, xv, tgt, wout, mod, ln_g, ln_b)
    loss_part = jnp.zeros((128,), F32).at[0].set(0.5 / D * jnp.sum(vec_o[0]))

    dk, dv, dq = _attn_bwd(q, k, kt, v, do, lse, delta)
    dxbc, ddt, dzs, dcw, dcb, dvec, dgssm = _ssd_bwd(xbc, xc, dtraw, zs, y, htp, dossm, cwf, dtb, alog, dskx, ssm_norm_g)
    dql, dckv, dkr, dwq, dwk, dwv, dqg, dkvg = _mla_bwd(dq, dk, dv, qlat, ckv, q_norm_g, kv_norm_g, wq, wk, wv, pos, invf)
    gx, dwin, vec_i = _inproj_bwd(xv, gx1, mod, win, dql, dckv, dza, dxbc, dzs, dkr, ddt)
    dmod = jnp.concatenate([vec_i[0:1], vec_i[1:2], vec_o[3:4]], axis=1)

    cols = lambda g: g.reshape(g.shape[0], 4, g.shape[1] // 4).transpose(1, 0, 2)
    g_in = cols(_in_from_padded(dwin)).astype(BF16)
    g_qb = cols(dwq.reshape(QL, NH, HP)[:, :, 0:QKD].reshape(QL, NH * QKD)).astype(BF16)
    g_kvb = cols(jnp.concatenate([dwk.reshape(KVL, NH, NOPE), dwv.reshape(KVL, NH, VD)], axis=2)
                 .reshape(KVL, NH * (NOPE + VD))).astype(BF16)
    g_out = dwout.reshape(4, MIX // 4, D).astype(BF16)
    small = {"b_ada": dmod, "conv_b": dcb, "ssm_norm_g": dgssm, "ln_g": vec_o[1:2], "ln_b": vec_o[2:3],
             "q_norm_g": dqg, "kv_norm_g": dkvg, "dt_bias": dvec[0:1, 0:SH], "a_log": dvec[1:2, 0:SH], "d_skip": dvec[2:3, 0:SH],
             "loss": loss_part}
    small = {n: jnp.broadcast_to(a.reshape(1, -1), (4, a.size)) for n, a in small.items()}
    small["conv_w"] = cols(dcw).reshape(4, CW * CC // 4)
    gsmall = _pack_small(small, 1)

    r_in, r_qb, r_kvb, r_out, rs, dmods = _exchange(
        "exchange_grads", [g_in, g_qb, g_kvb, g_out, gsmall, jnp.broadcast_to(dmod.reshape(1, 1, 3 * D), (4, 1, 3 * D))])
    chip = 2 * lax.axis_index("x") + lax.axis_index("y")
    dmods = lax.dynamic_slice_in_dim(dmods.reshape(8, 3 * D), chip * (3 * D // 4), 3 * D // 4, axis=1)
    g_ada = _ada_bwd(call.T, dmods)
    res = dict(w_ada=_adamw("w_ada", g_ada, w_ada[0], m_w_ada[0], v_w_ada[0]),
               w_in=_adamw("w_in", r_in, w_in[0], m_w_in[0], v_w_in[0]),
               w_qb=_adamw("w_qb", r_qb, w_qb[0], m_w_qb[0], v_w_qb[0]),
               w_kvb=_adamw("w_kvb", r_kvb, w_kvb[0], m_w_kvb[0], v_w_kvb[0]),
               w_out=_adamw("w_out", r_out, w_out[0], m_w_out[0], v_w_out[0]))
    wsm = _pack_small(dict(b_ada=b_ada, conv_w=conv_w, conv_b=conv_b, ssm_norm_g=ssm_norm_g, ln_g=ln_g, ln_b=ln_b,
                           q_norm_g=q_norm_g, kv_norm_g=kv_norm_g, dt_bias=dt_bias, a_log=a_log, d_skip=d_skip, loss=jnp.zeros((128,), F32)), 0)
    msm = _pack_small(dict(b_ada=m_b_ada, conv_w=m_conv_w, conv_b=m_conv_b, ssm_norm_g=m_ssm_norm_g, ln_g=m_ln_g, ln_b=m_ln_b,
                           q_norm_g=m_q_norm_g, kv_norm_g=m_kv_norm_g, dt_bias=m_dt_bias, a_log=m_a_log, d_skip=m_d_skip, loss=jnp.zeros((128,), F32)), 0)
    vsm = _pack_small(dict(b_ada=v_b_ada, conv_w=v_conv_w, conv_b=v_conv_b, ssm_norm_g=v_ssm_norm_g, ln_g=v_ln_g, ln_b=v_ln_b,
                           q_norm_g=v_q_norm_g, kv_norm_g=v_kv_norm_g, dt_bias=v_dt_bias, a_log=v_a_log, d_skip=v_d_skip, loss=jnp.zeros((128,), F32)), 0)
    sm = _adamw("small", rs, wsm, msm, vsm)

    order = ["w_ada", "b_ada", "w_in", "q_norm_g", "w_qb", "kv_norm_g", "w_kvb", "conv_w", "conv_b", "dt_bias", "a_log",
             "d_skip", "ssm_norm_g", "w_out", "ln_g", "ln_b"]
    shapes = dict(w_ada=w_ada.shape, b_ada=b_ada.shape, w_in=w_in.shape, q_norm_g=q_norm_g.shape, w_qb=w_qb.shape,
                  kv_norm_g=kv_norm_g.shape, w_kvb=w_kvb.shape, conv_w=conv_w.shape, conv_b=conv_b.shape, dt_bias=dt_bias.shape,
                  a_log=a_log.shape, d_skip=d_skip.shape, ssm_norm_g=ssm_norm_g.shape, w_out=w_out.shape, ln_g=ln_g.shape,
                  ln_b=ln_b.shape)
    outs = []
    for kind in range(4):
        d = _unpack_small(sm[kind])
        d.update({n: r[kind] for n, r in res.items()})
        outs.extend(d[n].reshape(shapes[n]) for n in order)
    loss = _unpack_small(sm[0])["loss"][0]
    return (loss, gx.reshape(x.shape), *outs)
```

```python
import functools
import math

import numpy as np
import jax
import jax.numpy as jnp
from jax import lax
from jax.experimental import pallas as pl
from jax.experimental.pallas import tpu as pltpu

F32 = jnp.float32
BF16 = jnp.bfloat16
MESH_ID = pl.DeviceIdType.MESH

D = 1024
NH = 8
NOPE = 128
ROPE = 64
VD = 128
VDP = 144
QKD = NOPE + ROPE
HP = 256
QL = 384
KVL = 256
ROPE_THETA = 10000.0
SH = 16
SP = 64
SG = 2
SN = 128
CW = 4
CH = 128
SW = SH * SP
CC = SW + 2 * SG * SN
GW = SW // SG
MIX = 2 * D
IN_W = 4304
ALPHA = 2.0 ** 0.25
RMS_EPS = 1e-6
LN_EPS = 1e-5
SCALE = QKD ** -0.5
LN2 = math.log(2.0)
QSCALE = SCALE / LN2
LR, B1, B2, EPS, WD, STEP = 0.001, 0.9, 0.999, 1e-08, 0.01, 10

P_Q = (0, 384)
P_KV = (384, 640)
P_ZA = (640, 1664)
P_XBC = (1664, 3200)
P_ZS = (3200, 4224)
P_KR = (4224, 4352)
P_DT = (4352, 4480)
IN_P = 4480

R_SMALL = 16

TM = 256
TQ = 512
NSP = 2
TQF = 1024
VMEM_LIMIT = 56 * 1024 * 1024


def _cp(sem=None):
    return pltpu.CompilerParams(dimension_semantics=sem, vmem_limit_bytes=VMEM_LIMIT)


def _mm(a, b):
    return jnp.dot(a, b, preferred_element_type=F32)


def _nt(a, b):
    return lax.dot_general(a, b, (((1,), (1,)), ((), ())), preferred_element_type=F32)


def _tn(a, b):
    return lax.dot_general(a, b, (((0,), (0,)), ((), ())), preferred_element_type=F32)


def _split(a, terms):
    parts = []
    for t in range(terms):
        p = a.astype(BF16)
        parts.append(p)
        if t + 1 < terms:
            a = a - p.astype(F32)
    return parts


def _mm_x(a, ones, terms):
    parts = _split(a, terms)
    out = _mm(parts[0], ones)
    for p in parts[1:]:
        out = out + _mm(p, ones)
    return out


def _xmm(ones, a, terms):
    parts = _split(a, terms)
    out = _mm(ones, parts[0])
    for p in parts[1:]:
        out = out + _mm(ones, p)
    return out


def _nt_x(a, ones, terms):
    parts = _split(a, terms)
    out = _nt(parts[0], ones)
    for p in parts[1:]:
        out = out + _nt(p, ones)
    return out


def _sigmoid(z):
    return 1.0 / (1.0 + jnp.exp(-z))


def _softplus(z):
    return jnp.maximum(z, 0.0) + jnp.log1p(jnp.exp(-jnp.abs(z)))


def _rope(t, cs, sn):
    lane = lax.broadcasted_iota(jnp.int32, t.shape, 1)
    rot = jnp.where(lane < ROPE // 2, -pltpu.roll(t, 128 - ROPE // 2, 1), pltpu.roll(t, ROPE // 2, 1))
    return t * cs + rot * sn


def _rope_t(t, cs, sn):
    lane = lax.broadcasted_iota(jnp.int32, t.shape, 1)
    y = t * sn
    rot = jnp.where(lane < ROPE // 2, -pltpu.roll(y, 128 - ROPE // 2, 1), pltpu.roll(y, ROPE // 2, 1))
    return t * cs - rot


def _full(shape):
    n = len(shape)
    return pl.BlockSpec(shape, lambda *_: (0,) * n)


def _const(shape):
    n = len(shape)
    return pl.BlockSpec(shape, lambda *_: (0,) * n, pipeline_mode=pl.Buffered(1))


def _gather_weights(shards):
    n = len(shards)
    halves = [a.shape[0] // 2 for a in shards]

    def body(*refs):
        srcs, dsts = refs[:n], refs[n:2 * n]
        send_sems, recv_sems, local_sems = refs[2 * n:]
        x, y, c = lax.axis_index("x"), lax.axis_index("y"), lax.axis_index("c")
        me = 2 * x + y
        sibling = (x, y, 1 - c)
        chips = [(1 - x, y), (x, 1 - y), (1 - x, 1 - y)]

        def rows(a, pc):
            return pl.ds(pl.multiple_of(pc * halves[a], halves[a]), halves[a])

        def copy(a, k, src, slot, pc, to):
            return pltpu.make_async_remote_copy(
                src_ref=src, dst_ref=dsts[a].at[slot, rows(a, pc)], send_sem=send_sems.at[a, k],
                recv_sem=recv_sems.at[a, k], device_id=to, device_id_type=MESH_ID)

        local = [pltpu.make_async_copy(srcs[a], dsts[a].at[me], local_sems.at[a]) for a in range(n)]
        for cp in local:
            cp.start()
        sends = [copy(a, j, srcs[a].at[rows(a, c)], me, c, (px, py, c)) for a in range(n) for j, (px, py) in enumerate(chips)]
        for cp in sends:
            cp.start()
        passed = []
        for a in range(n):
            for j, (px, py) in enumerate(chips):
                k = 2 * px + py
                copy(a, j, srcs[a].at[rows(a, c)], k, c, (x, y, c)).wait_recv()
                fwd = copy(a, 3 + j, dsts[a].at[k, rows(a, c)], k, c, sibling)
                fwd.start()
                passed.append(fwd)
        for a in range(n):
            for j, (px, py) in enumerate(chips):
                copy(a, 3 + j, srcs[a].at[rows(a, c)], 2 * px + py, 1 - c, (x, y, c)).wait_recv()
        for cp in sends + passed:
            cp.wait_send()
        for cp in local:
            cp.wait()

    hbm = pl.BlockSpec(memory_space=pltpu.HBM)
    return pl.pallas_call(
        body, name="gather_weights",
        out_shape=tuple(jax.ShapeDtypeStruct((4,) + a.shape, a.dtype) for a in shards),
        in_specs=[hbm] * n, out_specs=tuple([hbm] * n),
        scratch_shapes=[pltpu.SemaphoreType.DMA((n, 6)), pltpu.SemaphoreType.DMA((n, 6)), pltpu.SemaphoreType.DMA((n,))],
    )(*shards)


def _exchange(name, slabs):
    n = len(slabs)

    def body(*refs):
        srcs, dsts = refs[:n], refs[n:2 * n]
        send_sems, recv_sems, local_sems = refs[2 * n:]
        x, y, c = lax.axis_index("x"), lax.axis_index("y"), lax.axis_index("c")
        chip = 2 * x + y
        sibling = (x, y, 1 - c)
        chips = [(1 - x, y), (x, 1 - y), (1 - x, 1 - y)]

        def slot(px, py, pc):
            return 4 * px + 2 * py + pc

        def copy(a, k, src, s, to):
            return pltpu.make_async_remote_copy(
                src_ref=src, dst_ref=dsts[a].at[s], send_sem=send_sems.at[a, k], recv_sem=recv_sems.at[a, k],
                device_id=to, device_id_type=MESH_ID)

        mine = slot(x, y, c)
        local = [pltpu.make_async_copy(srcs[a].at[chip], dsts[a].at[mine], local_sems.at[a]) for a in range(n)]
        for cp in local:
            cp.start()
        first = []
        for a in range(n):
            first.append(copy(a, 0, srcs[a].at[chip], mine, sibling))
            for j, (px, py) in enumerate(chips):
                first.append(copy(a, 1 + j, srcs[a].at[2 * px + py], mine, (px, py, c)))
        for cp in first:
            cp.start()
        passed = []
        for a in range(n):
            for j, (px, py) in enumerate(chips):
                s = slot(px, py, c)
                copy(a, 1 + j, srcs[a].at[chip], s, (x, y, c)).wait_recv()
                fwd = copy(a, 4 + j, dsts[a].at[s], s, sibling)
                fwd.start()
                passed.append(fwd)
        for a in range(n):
            copy(a, 0, srcs[a].at[chip], slot(x, y, 1 - c), (x, y, c)).wait_recv()
            for j, (px, py) in enumerate(chips):
                copy(a, 4 + j, srcs[a].at[chip], slot(px, py, 1 - c), (x, y, c)).wait_recv()
        for cp in first + passed:
            cp.wait_send()
        for cp in local:
            cp.wait()

    hbm = pl.BlockSpec(memory_space=pltpu.HBM)
    return pl.pallas_call(
        body, name=name,
        out_shape=tuple(jax.ShapeDtypeStruct((8,) + a.shape[1:], a.dtype) for a in slabs),
        in_specs=[hbm] * n, out_specs=tuple([hbm] * n),
        scratch_shapes=[pltpu.SemaphoreType.DMA((n, 7)), pltpu.SemaphoreType.DMA((n, 7)), pltpu.SemaphoreType.DMA((n,))],
    )(*slabs)


def _ada(call, w_shard):
    def body(c_ref, w_ref, o_ref):
        o_ref[...] = _mm(c_ref[...].astype(BF16), w_ref[...].astype(BF16))

    return pl.pallas_call(body, name="ada", out_shape=jax.ShapeDtypeStruct((8, w_shard.shape[1]), F32),
                          compiler_params=_cp())(call, w_shard)


def _inproj(x, mod, win, qg, kvg, wq, wk, wv, wvt, pos, invf):
    S = x.shape[0]
    tm = min(TM, S)

    def body(x_ref, mod_ref, win_ref, qg_ref, kvg_ref, wq_ref, wk_ref, wv_ref, wvt_ref, pos_ref, invf_ref,
             qlat_ref, ckv_ref, za_ref, xbc_ref, zs_ref, dt_ref, q_ref, k_ref, v_ref, vt_ref):
        shift = mod_ref[0:1, 0:D]
        scale = mod_ref[0:1, D:2 * D]
        u = (x_ref[...] * (1.0 + scale) + shift).astype(BF16)

        def proj(p):
            return _mm(u, win_ref[:, p[0]:p[1]])

        ql = proj(P_Q)
        ckv = proj(P_KV)
        qlat_ref[...] = ql
        ckv_ref[...] = ckv
        za_ref[...] = proj(P_ZA)
        xbc_ref[...] = proj(P_XBC)
        zs_ref[...] = proj(P_ZS)
        dt_ref[...] = proj(P_DT)
        kr = proj(P_KR)

        ang = pos_ref[...].astype(F32) * invf_ref[...]
        cs = jnp.cos(ang)
        sn = jnp.sin(ang)

        rq = lax.rsqrt(jnp.mean(ql * ql, axis=-1, keepdims=True) + RMS_EPS)
        qn = (ql * rq * qg_ref[...]).astype(BF16)
        for h in range(NH):
            qh = _mm(qn, wq_ref[:, h * HP:(h + 1) * HP])
            q_ref[h, :, 0:NOPE] = (qh[:, 0:NOPE] * QSCALE).astype(BF16)
            q_ref[h, :, NOPE:HP] = (_rope(qh[:, NOPE:HP], cs, sn) * QSCALE).astype(BF16)

        rk = lax.rsqrt(jnp.mean(ckv * ckv, axis=-1, keepdims=True) + RMS_EPS)
        kn = (ckv * rk * kvg_ref[...]).astype(BF16)
        knope = _mm(kn, wk_ref[...])
        vall = _mm(kn, wv_ref[...])
        krf = _rope(kr, cs, sn)
        krr = krf.astype(BF16)
        ones_rows = jnp.where(lax.broadcasted_iota(jnp.int32, (VDP - VD, tm), 0) == 0, 1.0, 0.0).astype(BF16)
        for h in range(NH):
            k_ref[h, :, 0:NOPE] = knope[:, h * NOPE:(h + 1) * NOPE].astype(BF16)
            k_ref[h, :, NOPE:HP] = krr
            v_ref[h] = vall[:, h * VD:(h + 1) * VD].astype(BF16)
            vt_ref[h, 0:VD, :] = _nt(wvt_ref[h * VD:(h + 1) * VD, :], kn).astype(BF16)
            vt_ref[h, VD:VDP, :] = ones_rows

    row = lambda n: pl.BlockSpec((tm, n), lambda i: (i, 0))
    heads = lambda n: pl.BlockSpec((NH, tm, n), lambda i: (0, i, 0))
    heads_t = lambda n: pl.BlockSpec((NH, None, n, tm), lambda i: (0, i, 0, 0))
    sd = lambda n: jax.ShapeDtypeStruct((S, n), F32)
    hd = lambda n: jax.ShapeDtypeStruct((NH, S, n), BF16)
    ht = lambda n: jax.ShapeDtypeStruct((NH, S // tm, n, tm), BF16)
    return pl.pallas_call(
        body, name="inproj", grid=(S // tm,),
        in_specs=[row(D), _const((8, 3 * D)), _const((D, IN_P)), _const((1, QL)), _const((1, KVL)),
                  _const((QL, NH * HP)), _const((KVL, NH * NOPE)), _const((KVL, NH * VD)),
                  _const((NH * VD, KVL)), row(1), _const((1, 128))],
        out_specs=[row(QL), row(KVL), row(D), row(CC), row(D), row(128), heads(HP), heads(HP), heads(VD),
                   heads_t(VDP)],
        out_shape=[sd(QL), sd(KVL), sd(D), sd(CC), sd(D), sd(128), hd(HP), hd(HP), hd(VD), ht(VDP)],
        compiler_params=_cp(("arbitrary",)),
    )(x, mod, win, qg, kvg, wq, wk, wv, wvt, pos, invf)


def _attn_fwd(q, k, vt):
    _, S, _ = q.shape
    tq = min(TQF, S)
    nq = S // tq
    half = tq // NSP
    tb = vt.shape[3]
    nsb = half // tb

    def body(q_ref, k_ref, vt_ref, o_ref, lse_ref):
        i = pl.program_id(1)
        qb = q_ref[...]

        def scores(j, hb):
            off = pl.multiple_of(j * tq + hb * half, half)
            return _nt(k_ref[pl.ds(off, half), :], qb)

        def update(j, hb, s, carry):
            m, acc = carry
            m_new = jnp.maximum(m, jnp.max(s, axis=0, keepdims=True))
            a = jnp.exp2(m - m_new)
            pb = jnp.exp2(s - m_new).astype(BF16)
            acc = a * acc
            for sb in range(nsb):
                acc = acc + _mm(vt_ref[(NSP * j + hb) * nsb + sb], pb[sb * tb:(sb + 1) * tb, :])
            return m_new, acc

        def trip(j, carry, masked):
            s = [scores(j, hb) for hb in range(NSP)]
            if masked:
                r = lax.broadcasted_iota(jnp.int32, s[0].shape, 0)
                cidx = lax.broadcasted_iota(jnp.int32, s[0].shape, 1)
                s = [jnp.where(cidx >= r + hb * half, s[hb], -1e30) for hb in range(NSP)]
            for hb in range(NSP):
                carry = update(j, hb, s[hb], carry)
            return carry

        def finish(carry):
            m, acc = carry
            l = acc[VD:VD + 1, :]
            o_ref[...] = (acc[0:VD, :] / l).T
            lse_ref[...] = m + jnp.log2(l)

        init = (jnp.full((1, tq), -1e30, F32), jnp.zeros((VDP, tq), F32))
        carry = lax.fori_loop(0, i >> 1, lambda t, cr: trip(2 * t + 1, trip(2 * t, cr, False), False), init)

        @pl.when((i & 1) == 0)
        def _():
            finish(trip(i, carry, True))

        @pl.when((i & 1) == 1)
        def _():
            finish(trip(i, trip(i - 1, carry, False), True))

    return pl.pallas_call(
        body, name="attn_fwd", grid=(NH, nq),
        in_specs=[pl.BlockSpec((None, tq, HP), lambda h, i: (h, i, 0)),
                  pl.BlockSpec((None, S, HP), lambda h, i: (h, 0, 0)),
                  pl.BlockSpec((None, S // tb, VDP, tb), lambda h, i: (h, 0, 0, 0))],
        out_specs=[pl.BlockSpec((tq, VD), lambda h, i: (i, h)),
                   pl.BlockSpec((None, None, 1, tq), lambda h, i: (h, i, 0, 0))],
        out_shape=[jax.ShapeDtypeStruct((S, NH * VD), F32), jax.ShapeDtypeStruct((NH, nq, 1, tq), F32)],
        compiler_params=_cp(("arbitrary", "arbitrary")),
    )(q, k, vt)


def _ssd_consts():
    tri = np.tril(np.ones((CH, CH), np.float32))
    e16 = np.zeros((128, SW), np.float32)
    for h in range(SH):
        e16[h, h * SP:(h + 1) * SP] = 1.0
    return jnp.asarray(tri, BF16), jnp.asarray(tri.T.copy(), BF16), jnp.asarray(e16, BF16)


def _ssd_conv(xbc_ref, halo_ref, cw_ref, cb_ref, ext, first):
    ext[0:8, :] = jnp.where(first, 0.0, halo_ref[...])
    ext[8:8 + CH, :] = xbc_ref[...]
    cw = cw_ref[...]
    xc = cb_ref[...] + cw[0:1, :] * ext[5:5 + CH, :]
    for kk in range(1, CW):
        xc = xc + cw[kk:kk + 1, :] * ext[5 + kk:5 + kk + CH, :]
    return xc


def _ssd_chunk_common(xc, dtraw_ref, dtb_ref, alog_ref, tri_ref, e16_ref):
    sact = _sigmoid(xc)
    act = xc * sact
    lane = lax.broadcasted_iota(jnp.int32, (1, 128), 1)
    arow = jnp.where(lane < SH, -jnp.exp(alog_ref[...]), 0.0)
    dtpre = dtraw_ref[...] + dtb_ref[...]
    dt = _softplus(dtpre)
    a = dt * arow
    cum = _xmm(tri_ref[...], a, 3)
    cumx = _mm_x(cum, e16_ref[...], 3)
    dtx = _mm_x(dt, e16_ref[...], 2)
    return sact, act, arow, dtpre, dt, cum, cumx, dtx


def _ssd_fwd(xbc, dtraw, zs, conv_w, conv_b, dtb, alog, dskx, gssm):
    S = xbc.shape[0]
    nc = S // CH
    tri, _, e16 = _ssd_consts()

    def body(xbc_ref, halo_ref, dtraw_ref, zs_ref, cw_ref, cb_ref, dtb_ref, alog_ref, dsk_ref, g_ref, tri_ref, e16_ref,
             xc_ref, y_ref, htp_ref, o_ref, ht, ext):
        i = pl.program_id(0)

        @pl.when(i == 0)
        def _():
            ht[...] = jnp.zeros_like(ht)

        xc = _ssd_conv(xbc_ref, halo_ref, cw_ref, cb_ref, ext, i == 0)
        xc_ref[...] = xc
        sact, act, arow, dtpre, dt, cum, cumx, dtx = _ssd_chunk_common(xc, dtraw_ref, dtb_ref, alog_ref, tri_ref, e16_ref)
        cum_t = cum.T
        xs = act[:, 0:SW]
        lastx = cumx[CH - 1:CH, :]
        xh = xs * dtx
        eexp = jnp.exp(cumx)
        dte = jnp.exp(lastx - cumx)
        cdx = jnp.exp(lastx)
        htp = ht[...]
        htp_ref[...] = htp
        xw = (xh * dte).astype(BF16)
        xb = xh.astype(BF16)
        trim = tri_ref[...].astype(F32) > 0.5
        lane = lax.broadcasted_iota(jnp.int32, (CH, 128), 1)
        parts = []
        for g in range(SG):
            gl = slice(g * GW, (g + 1) * GW)
            bg = act[:, SW + g * SN:SW + (g + 1) * SN].astype(BF16)
            cg = act[:, SW + SG * SN + g * SN:SW + SG * SN + (g + 1) * SN].astype(BF16)
            cbm = _nt(cg, bg)
            yoff = eexp[:, gl] * _mm(cg, htp[:, gl].astype(BF16))
            ht[:, gl] = htp[:, gl] * cdx[:, gl] + _tn(bg, xw[:, gl])
            for pr in range(GW // 128):
                h0 = g * (SH // SG) + 2 * pr
                lo = g * GW + pr * 128
                xp = xb[:, lo:lo + 128]
                res = []
                for hh in (h0, h0 + 1):
                    seg = cum[:, hh:hh + 1] - cum_t[hh:hh + 1, :]
                    mh = jnp.where(trim, cbm * jnp.exp(seg), 0.0).astype(BF16)
                    res.append(_mm(mh, xp))
                parts.append(jnp.where(lane < SP, res[0], res[1]) + yoff[:, pr * 128:(pr + 1) * 128])
        y = jnp.concatenate(parts, axis=1) + xs * dsk_ref[...]
        y_ref[...] = y
        z = zs_ref[...]
        hf = y * (z * _sigmoid(z))
        outs = []
        for g in range(SG):
            hg = hf[:, g * GW:(g + 1) * GW]
            rs = lax.rsqrt(jnp.mean(hg * hg, axis=-1, keepdims=True) + RMS_EPS)
            outs.append(hg * rs)
        o_ref[...] = (jnp.concatenate(outs, axis=1) * g_ref[...]).astype(BF16)

    row = lambda n: pl.BlockSpec((CH, n), lambda i: (i, 0))
    return pl.pallas_call(
        body, name="ssd_fwd", grid=(nc,),
        in_specs=[row(CC), pl.BlockSpec((8, CC), lambda i: (jnp.maximum(i * (CH // 8) - 1, 0), 0)), row(128), row(SW),
                  _const((CW, CC)), _const((1, CC)), _const((1, 128)), _const((1, 128)), _const((1, SW)), _const((1, SW)),
                  _const((CH, CH)), _const((128, SW))],
        out_specs=[row(CC), row(SW), pl.BlockSpec((None, SN, SW), lambda i: (i, 0, 0)), row(SW)],
        out_shape=[jax.ShapeDtypeStruct((S, CC), F32), jax.ShapeDtypeStruct((S, SW), F32),
                   jax.ShapeDtypeStruct((nc, SN, SW), F32), jax.ShapeDtypeStruct((S, SW), BF16)],
        scratch_shapes=[pltpu.VMEM((SN, SW), F32), pltpu.VMEM((8 + CH, CC), F32)],
        compiler_params=_cp(("arbitrary",)),
    )(xbc, xbc, dtraw, zs, conv_w, conv_b, dtb, alog, dskx, gssm, tri, e16)


def _outproj(o, za, ossm, x, tgt, wout, mod, ln_g, ln_b):
    S = x.shape[0]
    tm = min(TM, S)

    e8 = np.zeros((D, 128), np.float32)
    for h in range(NH):
        e8[h * VD:(h + 1) * VD, h] = 1.0
    e8 = jnp.asarray(e8, BF16)

    def body(o_ref, za_ref, os_ref, x_ref, t_ref, w_ref, mod_ref, g_ref, b_ref, e8_ref,
             gx_ref, do_ref, dza_ref, dos_ref, delta_ref, dw_ref, vec_ref):
        i = pl.program_id(0)

        @pl.when(i == 0)
        def _():
            dw_ref[...] = jnp.zeros_like(dw_ref)
            vec_ref[...] = jnp.zeros_like(vec_ref)

        gate = mod_ref[0:1, 2 * D:3 * D]
        ov = o_ref[...]
        z = za_ref[...]
        sz = _sigmoid(z)
        silz = z * sz
        a = (ov * silz).astype(BF16)
        osb = os_ref[...]
        mixed = _mm(a, w_ref[0:D, :]) + _mm(osb, w_ref[D:MIX, :])
        xv = x_ref[...]
        hres = ALPHA * xv + gate * mixed
        mu = jnp.mean(hres, axis=-1, keepdims=True)
        hc = hres - mu
        var = jnp.mean(hc * hc, axis=-1, keepdims=True)
        rstd = lax.rsqrt(var + LN_EPS)
        xhat = hc * rstd
        g = g_ref[...]
        yv = xhat * g + b_ref[...]
        err = yv - t_ref[...]
        dy = err * (1.0 / D)
        vec_ref[0:1, :] += jnp.sum(err * err, axis=0, keepdims=True)
        vec_ref[1:2, :] += jnp.sum(dy * xhat, axis=0, keepdims=True)
        vec_ref[2:3, :] += jnp.sum(dy, axis=0, keepdims=True)
        dxh = dy * g
        dh = rstd * (dxh - jnp.mean(dxh, axis=-1, keepdims=True) - xhat * jnp.mean(dxh * xhat, axis=-1, keepdims=True))
        gx_ref[...] = ALPHA * dh
        vec_ref[3:4, :] += jnp.sum(dh * mixed, axis=0, keepdims=True)
        dmixed = (gate * dh).astype(BF16)
        dw_ref[0:D, :] += _tn(a, dmixed)
        dw_ref[D:MIX, :] += _tn(osb, dmixed)
        da = _nt(dmixed, w_ref[0:D, :])
        dos_ref[...] = _nt(dmixed, w_ref[D:MIX, :])
        dov = da * silz
        do_ref[...] = dov.astype(BF16)
        dza_ref[...] = (da * ov * (sz * (1.0 + z * (1.0 - sz)))).astype(BF16)
        delta_ref[:, 0, :] = _mm_x(dov * ov, e8_ref[...], 2).T[0:NH, :]

    row = lambda n: pl.BlockSpec((tm, n), lambda i: (i, 0))
    return pl.pallas_call(
        body, name="outproj", grid=(S // tm,),
        in_specs=[row(D), row(D), row(D), row(D), row(D), _const((MIX, D)), _const((8, 3 * D)), _const((1, D)), _const((1, D)),
                  _const((D, 128))],
        out_specs=[row(D), row(D), row(D), row(D), pl.BlockSpec((NH, None, 1, tm), lambda i: (0, i, 0, 0)),
                   _full((MIX, D)), _full((8, D))],
        out_shape=[jax.ShapeDtypeStruct((S, D), F32), jax.ShapeDtypeStruct((S, D), BF16), jax.ShapeDtypeStruct((S, D), BF16),
                   jax.ShapeDtypeStruct((S, D), F32), jax.ShapeDtypeStruct((NH, S // tm, 1, tm), F32),
                   jax.ShapeDtypeStruct((MIX, D), F32), jax.ShapeDtypeStruct((8, D), F32)],
        compiler_params=_cp(("arbitrary",)),
    )(o, za, ossm, x, tgt, wout, mod, ln_g, ln_b, e8)


def _attn_bwd(q, k, v, do, lse, delta):
    _, S, _ = q.shape
    tk = min(TQ, S // 2)
    nk = S // tk
    tq = 2 * tk
    nq = S // tq

    def body(k_ref, v_ref, q_ref, do_ref, lse_ref, dl_ref, dk_ref, dv_ref, dq_ref, dqt_ref):
        j = pl.program_id(1)
        kb = k_ref[...]
        ktb = kb.T
        vb = v_ref[...]

        @pl.when(j == 0)
        def _():
            dqt_ref[...] = jnp.zeros_like(dqt_ref)

        dk_ref[...] = jnp.zeros_like(dk_ref)
        dv_ref[...] = jnp.zeros_like(dv_ref)

        def step(i, masked, lo=0):
            off = pl.multiple_of(i * tq + lo, tk)
            qb = q_ref[pl.ds(off, tq - lo), :]
            dob = do_ref[pl.ds(off, tq - lo), :]
            pt = jnp.exp2(_nt(kb, qb) - lse_ref[i][:, lo:tq])
            if masked:
                r = lax.broadcasted_iota(jnp.int32, pt.shape, 0)
                cidx = lax.broadcasted_iota(jnp.int32, pt.shape, 1)
                pt = jnp.where(i * tq + lo + cidx >= j * tk + r, pt, 0.0)
            dv_ref[...] += _mm(pt.astype(BF16), dob)
            dsb = (pt * (_nt(vb, dob) - dl_ref[i][:, lo:tq])).astype(BF16)
            dk_ref[...] += _mm(dsb, qb)
            dqt_ref[i, :, lo:tq] += _mm(ktb, dsb)

        first = j >> 1

        @pl.when((j & 1) == 0)
        def _():
            step(first, True)

        @pl.when((j & 1) == 1)
        def _():
            step(first, True, tk)
            dq_ref[...] = dqt_ref[first].T

        def loop_body(i, carry):
            step(i, False)
            return carry

        lax.fori_loop(first + 1, nq, loop_body, 0)
        dk_ref[...] = dk_ref[...] * LN2

    return pl.pallas_call(
        body, name="attn_bwd", grid=(NH, nk),
        in_specs=[pl.BlockSpec((None, tk, HP), lambda h, j: (h, j, 0)),
                  pl.BlockSpec((None, tk, VD), lambda h, j: (h, j, 0)),
                  pl.BlockSpec((None, S, HP), lambda h, j: (h, 0, 0)),
                  pl.BlockSpec((S, VD), lambda h, j: (0, h)),
                  pl.BlockSpec((None, nq, 1, tq), lambda h, j: (h, 0, 0, 0)),
                  pl.BlockSpec((None, nq, 1, tq), lambda h, j: (h, 0, 0, 0))],
        out_specs=[pl.BlockSpec((None, tk, HP), lambda h, j: (h, j, 0)),
                   pl.BlockSpec((None, tk, VD), lambda h, j: (h, j, 0)),
                   pl.BlockSpec((None, tq, HP), lambda h, j: (h, j >> 1, 0))],
        out_shape=[jax.ShapeDtypeStruct((NH, S, HP), F32), jax.ShapeDtypeStruct((NH, S, VD), F32),
                   jax.ShapeDtypeStruct((NH, S, HP), F32)],
        scratch_shapes=[pltpu.VMEM((nq, HP, tq), F32)],
        compiler_params=_cp(("arbitrary", "arbitrary")),
    )(k, v, q, do, lse.reshape(NH, nq, 1, tq), delta.reshape(NH, nq, 1, tq))


def _ssd_bwd(xbc, xc, dtraw, zs, y, htp, dossm, conv_w, dtb, alog, dskx, gssm):
    S = xbc.shape[0]
    nc = S // CH
    tri, triu, e16 = _ssd_consts()

    def body(xbc_ref, xc_ref, dtraw_ref, zs_ref, y_ref, htp_ref, dos_ref,
             cw_ref, dtb_ref, alog_ref, dsk_ref, g_ref, tri_ref, triu_ref, e16_ref,
             dxbc_ref, ddt_ref, dzs_ref, dcw_ref, dcb_ref, dvec_ref, dg_ref,
             dht, dext, dskacc):
        r = pl.program_id(0)

        @pl.when(r == 0)
        def _():
            dht[...] = jnp.zeros_like(dht)
            dext[CH:CH + 8, :] = jnp.zeros((8, CC), F32)
            dskacc[...] = jnp.zeros_like(dskacc)
            dcw_ref[...] = jnp.zeros_like(dcw_ref)
            dcb_ref[...] = jnp.zeros_like(dcb_ref)
            dvec_ref[...] = jnp.zeros_like(dvec_ref)
            dg_ref[...] = jnp.zeros_like(dg_ref)

        xc = xc_ref[...]
        sact, act, arow, dtpre, dt, cum, cumx, dtx = _ssd_chunk_common(xc, dtraw_ref, dtb_ref, alog_ref, tri_ref, e16_ref)
        cum_t = cum.T
        xs = act[:, 0:SW]
        lastx = cumx[CH - 1:CH, :]
        xh = xs * dtx
        eexp = jnp.exp(cumx)
        dte = jnp.exp(lastx - cumx)
        cdx = jnp.exp(lastx)
        trim = tri_ref[...].astype(F32) > 0.5
        lane = lax.broadcasted_iota(jnp.int32, (CH, 128), 1)
        rowi = lax.broadcasted_iota(jnp.int32, (CH, 128), 0)

        yv = y_ref[...]
        z = zs_ref[...]
        sz = _sigmoid(z)
        silz = z * sz
        hf = yv * silz
        dn = dos_ref[...] * g_ref[...]
        dhf_parts, nrm_parts = [], []
        for g in range(SG):
            gl = slice(g * GW, (g + 1) * GW)
            hg = hf[:, gl]
            rs = lax.rsqrt(jnp.mean(hg * hg, axis=-1, keepdims=True) + RMS_EPS)
            ng = hg * rs
            dng = dn[:, gl]
            dhf_parts.append(rs * (dng - ng * jnp.mean(dng * ng, axis=-1, keepdims=True)))
            nrm_parts.append(ng)
        nrm = jnp.concatenate(nrm_parts, axis=1)
        dhf = jnp.concatenate(dhf_parts, axis=1)
        dg_ref[...] += jnp.sum(dos_ref[...] * nrm, axis=0, keepdims=True)
        dyv = dhf * silz
        dzs_ref[...] = (dhf * yv * (sz * (1.0 + z * (1.0 - sz)))).astype(BF16)
        dskacc[...] += jnp.sum(dyv * xs, axis=0, keepdims=True)
        dxs_skip = dyv * dsk_ref[...]

        dhtn = dht[...]
        hp = htp_ref[...]
        dlastx = jnp.sum(dhtn * hp, axis=0, keepdims=True) * cdx
        xb = xh.astype(BF16)
        xwf = xh * dte
        dcum = jnp.zeros((CH, 128), F32)
        dcum_t = jnp.zeros((128, CH), F32)
        dxh_parts, dcumx_parts, dlast_parts, db_parts, dc_parts = [], [], [], [], []
        for g in range(SG):
            gl = slice(g * GW, (g + 1) * GW)
            bg = act[:, SW + g * SN:SW + (g + 1) * SN].astype(BF16)
            cg = act[:, SW + SG * SN + g * SN:SW + SG * SN + (g + 1) * SN].astype(BF16)
            hpg = hp[:, gl].astype(BF16)
            dhn = dhtn[:, gl].astype(BF16)
            dyg = dyv[:, gl]
            dz = (dyg * eexp[:, gl]).astype(BF16)
            dcg = _nt(dz, hpg)
            dht[:, gl] = dhtn[:, gl] * cdx[:, gl] + _tn(cg, dz)
            yoff = eexp[:, gl] * _mm(cg, hpg)
            dcumx_g = dyg * yoff
            dbg = _nt(xwf[:, gl].astype(BF16), dhn)
            dxw = _mm(bg, dhn)
            ddte = dxw * xwf[:, gl]
            dcumx_parts.append(dcumx_g - ddte)
            dlast_parts.append(jnp.sum(ddte, axis=0, keepdims=True))
            dxh_g = dxw * dte[:, gl]
            cbm = _nt(cg, bg)
            dcb = jnp.zeros((CH, CH), F32)
            dxp_parts = []
            for pr in range(GW // 128):
                h0 = g * (SH // SG) + 2 * pr
                lo = g * GW + pr * 128
                xp = xb[:, lo:lo + 128]
                dyp = dyv[:, lo:lo + 128]
                dxp = jnp.zeros((CH, 128), F32)
                for idx, hh in enumerate((h0, h0 + 1)):
                    decay = jnp.where(trim, jnp.exp(cum[:, hh:hh + 1] - cum_t[hh:hh + 1, :]), 0.0)
                    mh = cbm * decay
                    keep = (lane < SP) if idx == 0 else (lane >= SP)
                    dym = jnp.where(keep, dyp, 0.0).astype(BF16)
                    dm = _nt(dym, xp)
                    dxp = dxp + _tn(mh.astype(BF16), dym)
                    gm = dm * mh
                    dcum = dcum + jnp.where(lane == hh, jnp.sum(gm, axis=1, keepdims=True), 0.0)
                    dcum_t = dcum_t - jnp.where(rowi == hh, jnp.sum(gm, axis=0, keepdims=True), 0.0)
                    dcb = dcb + dm * decay
                dxp_parts.append(dxp)
            dxh_parts.append(dxh_g + jnp.concatenate(dxp_parts, axis=1))
            dcbb = dcb.astype(BF16)
            dc_parts.append(dcg + _mm(dcbb, bg))
            db_parts.append(dbg + _tn(dcbb, cg))
        dxh = jnp.concatenate(dxh_parts, axis=1)
        dcumx = jnp.concatenate(dcumx_parts, axis=1)
        dlastx = dlastx + jnp.concatenate(dlast_parts, axis=1)
        e16 = e16_ref[...]
        dlast128 = _nt_x(jnp.broadcast_to(dlastx, (8, SW)), e16, 2)[0:1, :]
        dcum = dcum + dcum_t.T + _nt_x(dcumx, e16, 2) + jnp.where(rowi == CH - 1, dlast128, 0.0)
        da = _xmm(triu_ref[...], dcum, 2)
        ddt = da * arow + _nt_x(dxh * xs, e16, 2)
        dvec_ref[1:2, :] += jnp.sum(da * dt, axis=0, keepdims=True)
        ddtraw = jnp.where(lane < SH, ddt * _sigmoid(dtpre), 0.0)
        dvec_ref[0:1, :] += jnp.sum(ddtraw, axis=0, keepdims=True)
        ddt_ref[...] = ddtraw.astype(BF16)
        dxs = dxs_skip + dxh * dtx
        dact = jnp.concatenate([dxs] + db_parts + dc_parts, axis=1)
        dxc = dact * (sact * (1.0 + xc * (1.0 - sact)))

        dcb_ref[...] += jnp.sum(dxc, axis=0, keepdims=True)
        dext[0:CH, :] = dxc
        cw = cw_ref[...]
        xraw = xbc_ref[...]
        dxr = cw[CW - 1:CW, :] * dxc
        dcw_ref[CW - 1:CW, :] += jnp.sum(dxc * xraw, axis=0, keepdims=True)
        for kk in range(CW - 1):
            dwin = dext[CW - 1 - kk:CW - 1 - kk + CH, :]
            dxr = dxr + cw[kk:kk + 1, :] * dwin
            dcw_ref[kk:kk + 1, :] += jnp.sum(dwin * xraw, axis=0, keepdims=True)
        dxbc_ref[...] = dxr.astype(BF16)
        dext[CH:CH + 8, :] = dxc[0:8, :]

        @pl.when(r == nc - 1)
        def _():
            dvec_ref[1:2, :] = dvec_ref[1:2, :] * arow
            dvec_ref[2:3, :] = _nt_x(jnp.broadcast_to(dskacc[...], (8, SW)), e16, 3)[0:1, :]

    rev = lambda n: pl.BlockSpec((CH, n), lambda r: (nc - 1 - r, 0))
    return pl.pallas_call(
        body, name="ssd_bwd", grid=(nc,),
        in_specs=[rev(CC), rev(CC),
                  rev(128), rev(SW), rev(SW), pl.BlockSpec((None, SN, SW), lambda r: (nc - 1 - r, 0, 0)), rev(SW),
                  _const((CW, CC)), _const((1, 128)), _const((1, 128)), _const((1, SW)), _const((1, SW)),
                  _const((CH, CH)), _const((CH, CH)), _const((128, SW))],
        out_specs=[rev(CC), rev(128), rev(SW), _full((CW, CC)), _full((1, CC)), _full((8, 128)), _full((1, SW))],
        out_shape=[jax.ShapeDtypeStruct((S, CC), BF16), jax.ShapeDtypeStruct((S, 128), BF16), jax.ShapeDtypeStruct((S, SW), BF16),
                   jax.ShapeDtypeStruct((CW, CC), F32), jax.ShapeDtypeStruct((1, CC), F32),
                   jax.ShapeDtypeStruct((8, 128), F32), jax.ShapeDtypeStruct((1, SW), F32)],
        scratch_shapes=[pltpu.VMEM((SN, SW), F32), pltpu.VMEM((CH + 8, CC), F32), pltpu.VMEM((1, SW), F32)],
        compiler_params=_cp(("arbitrary",)),
    )(xbc, xc, dtraw, zs, y, htp, dossm, conv_w, dtb, alog, dskx, gssm, tri, triu, e16)


def _mla_bwd(dq, dk, dv, qlat, ckv, qg, kvg, wq, wk, wv, pos, invf):
    S = qlat.shape[0]
    tm = min(TQ, S)

    def body(dq_ref, dk_ref, dv_ref, ql_ref, ckv_ref, qg_ref, kvg_ref, wq_ref, wk_ref, wv_ref, pos_ref, invf_ref,
             dql_ref, dckv_ref, dkr_ref, dwq_ref, dwk_ref, dwv_ref, dqg_ref, dkvg_ref):
        i = pl.program_id(0)

        @pl.when(i == 0)
        def _():
            dwq_ref[...] = jnp.zeros_like(dwq_ref)
            dwk_ref[...] = jnp.zeros_like(dwk_ref)
            dwv_ref[...] = jnp.zeros_like(dwv_ref)
            dqg_ref[...] = jnp.zeros_like(dqg_ref)
            dkvg_ref[...] = jnp.zeros_like(dkvg_ref)

        ang = pos_ref[...].astype(F32) * invf_ref[...]
        cs = jnp.cos(ang)
        sn = jnp.sin(ang)

        def rms_bwd(v, g, dn, dg_ref):
            r = lax.rsqrt(jnp.mean(v * v, axis=-1, keepdims=True) + RMS_EPS)
            vh = v * r
            dg_ref[...] += jnp.sum(dn * vh, axis=0, keepdims=True)
            dvh = dn * g
            return vh, r * (dvh - vh * jnp.mean(dvh * vh, axis=-1, keepdims=True))

        pieces = []
        for h in range(NH):
            dqh = dq_ref[h]
            pieces.append(dqh[:, 0:NOPE] * SCALE)
            pieces.append(_rope_t(dqh[:, NOPE:HP], cs, sn) * SCALE)
        dqf = jnp.concatenate(pieces, axis=1).astype(BF16)
        ql = ql_ref[...]
        g = qg_ref[...]
        dqn = _nt(dqf, wq_ref[...])
        qh, dql = rms_bwd(ql, g, dqn, dqg_ref)
        dwq_ref[...] += _tn((qh * g).astype(BF16), dqf)
        dql_ref[...] = dql.astype(BF16)

        dkn_p = jnp.concatenate([dk_ref[h, :, 0:NOPE] for h in range(NH)], axis=1).astype(BF16)
        dvf = jnp.concatenate([dv_ref[h] for h in range(NH)], axis=1).astype(BF16)
        dkr = dk_ref[0, :, NOPE:HP]
        for h in range(1, NH):
            dkr = dkr + dk_ref[h, :, NOPE:HP]
        lane = lax.broadcasted_iota(jnp.int32, dkr.shape, 1)
        dkr_ref[...] = jnp.where(lane < ROPE, _rope_t(dkr, cs, sn), 0.0).astype(BF16)
        cv = ckv_ref[...]
        gk = kvg_ref[...]
        dkn = _nt(dkn_p, wk_ref[...]) + _nt(dvf, wv_ref[...])
        kh, dckv = rms_bwd(cv, gk, dkn, dkvg_ref)
        knb = (kh * gk).astype(BF16)
        dwk_ref[...] += _tn(knb, dkn_p)
        dwv_ref[...] += _tn(knb, dvf)
        dckv_ref[...] = dckv.astype(BF16)

    row = lambda n: pl.BlockSpec((tm, n), lambda i: (i, 0))
    heads = lambda n: pl.BlockSpec((NH, tm, n), lambda i: (0, i, 0))
    return pl.pallas_call(
        body, name="mla_bwd", grid=(S // tm,),
        in_specs=[heads(HP), heads(HP), heads(VD), row(QL), row(KVL), _const((1, QL)), _const((1, KVL)),
                  _const((QL, NH * HP)), _const((KVL, NH * NOPE)), _const((KVL, NH * VD)), row(1), _const((1, 128))],
        out_specs=[row(QL), row(KVL), row(128), _full((QL, NH * HP)), _full((KVL, NH * NOPE)), _full((KVL, NH * VD)),
                   _full((1, QL)), _full((1, KVL))],
        out_shape=[jax.ShapeDtypeStruct((S, QL), BF16), jax.ShapeDtypeStruct((S, KVL), BF16), jax.ShapeDtypeStruct((S, 128), BF16),
                   jax.ShapeDtypeStruct((QL, NH * HP), F32), jax.ShapeDtypeStruct((KVL, NH * NOPE), F32),
                   jax.ShapeDtypeStruct((KVL, NH * VD), F32), jax.ShapeDtypeStruct((1, QL), F32), jax.ShapeDtypeStruct((1, KVL), F32)],
        compiler_params=_cp(("arbitrary",)),
    )(dq, dk, dv, qlat, ckv, qg, kvg, wq, wk, wv, pos, invf)


def _inproj_bwd(x, gx1, mod, win, dql, dckv, dza, dxbc, dzs, dkr, ddt):
    S = x.shape[0]
    tm = min(TM, S)

    def body(x_ref, gx1_ref, mod_ref, win_ref, dql_ref, dckv_ref, dza_ref, dxbc_ref, dzs_ref, dkr_ref, ddt_ref,
             gx_ref, dw_ref, vec_ref):
        i = pl.program_id(0)

        @pl.when(i == 0)
        def _():
            dw_ref[...] = jnp.zeros_like(dw_ref)
            vec_ref[...] = jnp.zeros_like(vec_ref)

        shift = mod_ref[0:1, 0:D]
        scale = mod_ref[0:1, D:2 * D]
        xv = x_ref[...]
        ut = (xv * (1.0 + scale) + shift).T.astype(BF16)
        pieces = (dql_ref, dckv_ref, dza_ref, dxbc_ref, dzs_ref, dkr_ref, ddt_ref)
        du = jnp.zeros((tm, D), F32)
        lo = 0
        for p_ref in pieces:
            n = p_ref.shape[1]
            dp = p_ref[...]
            du = du + _nt(dp, win_ref[:, lo:lo + n])
            dw_ref[:, lo:lo + n] += _mm(ut, dp)
            lo += n
        vec_ref[0:1, :] += jnp.sum(du, axis=0, keepdims=True)
        vec_ref[1:2, :] += jnp.sum(du * xv, axis=0, keepdims=True)
        gx_ref[...] = gx1_ref[...] + du * (1.0 + scale)

    row = lambda n: pl.BlockSpec((tm, n), lambda i: (i, 0))
    return pl.pallas_call(
        body, name="inproj_bwd", grid=(S // tm,),
        in_specs=[row(D), row(D), _const((8, 3 * D)), _const((D, IN_P)), row(QL), row(KVL), row(D), row(CC), row(D),
                  row(128), row(128)],
        out_specs=[row(D), pl.BlockSpec((D, IN_P), lambda i: (0, 0), pipeline_mode=pl.Buffered(1)), _full((8, D))],
        out_shape=[jax.ShapeDtypeStruct((S, D), F32), jax.ShapeDtypeStruct((D, IN_P), F32), jax.ShapeDtypeStruct((8, D), F32)],
        compiler_params=_cp(("arbitrary",)),
    )(x, gx1, mod, win, dql, dckv, dza, dxbc, dzs, dkr, ddt)


def _ada_bwd(callt, dmods):
    w = dmods.shape[1]

    def body(c_ref, d_ref, o_ref):
        acc = c_ref[:, 0:1] * d_ref[0:1, :]
        for s in range(1, 8):
            acc = acc + c_ref[:, s:s + 1] * d_ref[s:s + 1, :]
        o_ref[0] = acc

    return pl.pallas_call(body, name="ada_bwd", out_shape=jax.ShapeDtypeStruct((1, D, w), F32),
                          compiler_params=_cp())(callt, dmods)


def _adamw(name, parts, w, m, v):
    rows, ncol = w.shape
    tr = min(rows, 128)
    nparts = parts.shape[0]

    def body(p_ref, w_ref, m_ref, v_ref, g_ref, d_ref, nm_ref, nv_ref):
        g = p_ref[0].astype(F32)
        for s in range(1, nparts):
            g = g + p_ref[s].astype(F32)
        g_ref[...] = g
        nm = B1 * m_ref[...] + (1.0 - B1) * g
        nv = B2 * v_ref[...] + (1.0 - B2) * (g * g)
        nm_ref[...] = nm
        nv_ref[...] = nv
        m_hat = nm / (1.0 - B1 ** STEP)
        v_hat = nv / (1.0 - B2 ** STEP)
        d_ref[...] = -LR * (m_hat / (jnp.sqrt(v_hat) + EPS) + WD * w_ref[...])

    row = pl.BlockSpec((tr, ncol), lambda i: (i, 0))
    sd = jax.ShapeDtypeStruct((rows, ncol), F32)
    return pl.pallas_call(
        body, name="adamw_" + name, grid=(rows // tr,),
        in_specs=[pl.BlockSpec((nparts, tr, ncol), lambda i: (0, i, 0)), row, row, row],
        out_specs=[row, row, row, row], out_shape=[sd, sd, sd, sd],
        compiler_params=_cp(("arbitrary",)),
    )(parts, w, m, v)


_SMALL = (("b_ada", 3 * D), ("conv_w", CW * CC // 4), ("conv_b", CC), ("ssm_norm_g", SW), ("ln_g", D), ("ln_b", D),
          ("q_norm_g", QL), ("kv_norm_g", KVL), ("dt_bias", SH), ("a_log", SH), ("d_skip", SH), ("loss", 128))


def _pack_small(d, lead):
    flat = [d[name].reshape(d[name].shape[:lead] + (-1,)) for name, _ in _SMALL]
    used = sum(f.shape[lead] for f in flat)
    pad = jnp.zeros(flat[0].shape[:lead] + (R_SMALL * 1024 - used,), F32)
    return jnp.concatenate(flat + [pad], axis=lead).reshape(flat[0].shape[:lead] + (R_SMALL, 1024))


def _unpack_small(p):
    flat = p.reshape(-1)
    out, r = {}, 0
    for name, n in _SMALL:
        out[name] = flat[r:r + n]
        r += n
    return out


def _in_to_padded(w):
    z = lambda n: jnp.zeros((w.shape[0], n), w.dtype)
    return jnp.concatenate([w[:, 0:384], w[:, 384:640], w[:, 704:1728], w[:, 1728:3264], w[:, 3280:4304],
                            w[:, 640:704], z(64), w[:, 3264:3280], z(112)], axis=1)


def _in_from_padded(g):
    return jnp.concatenate([g[:, 0:384], g[:, 384:640], g[:, P_KR[0]:P_KR[0] + 64], g[:, 640:1664], g[:, 1664:3200],
                            g[:, P_DT[0]:P_DT[0] + 16], g[:, 3200:4224]], axis=1)


def kernel(x, c, positions, w_ada, b_ada, w_in, q_norm_g, w_qb, kv_norm_g, w_kvb, conv_w, conv_b, dt_bias, a_log, d_skip, ssm_norm_g, w_out, ln_g, ln_b, loss_target, m_w_ada, m_b_ada, m_w_in, m_q_norm_g, m_w_qb, m_kv_norm_g, m_w_kvb, m_conv_w, m_conv_b, m_dt_bias, m_a_log, m_d_skip, m_ssm_norm_g, m_w_out, m_ln_g, m_ln_b, v_w_ada, v_b_ada, v_w_in, v_q_norm_g, v_w_qb, v_kv_norm_g, v_w_kvb, v_conv_w, v_conv_b, v_dt_bias, v_a_log, v_d_skip, v_ssm_norm_g, v_w_out, v_ln_g, v_ln_b):
    S = x.shape[1]
    xv = x[0]
    tgt = loss_target[0]

    cw16 = jnp.concatenate([conv_w[0], jnp.zeros((16 - CW, CC // 4), F32)], axis=0)
    f_in, f_qb, f_kvb, f_out, f_cw = _gather_weights(
        [w_in[0].astype(BF16), w_qb[0].astype(BF16), w_kvb[0].astype(BF16), w_out[0].astype(BF16), cw16])
    cat1 = lambda f: f.transpose(1, 0, 2).reshape(f.shape[1], 4 * f.shape[2])
    win = _in_to_padded(cat1(f_in))
    wqb = cat1(f_qb).reshape(QL, NH, QKD)
    wq = jnp.concatenate([wqb, jnp.zeros((QL, NH, HP - QKD), BF16)], axis=2).reshape(QL, NH * HP)
    wkvb = cat1(f_kvb).reshape(KVL, NH, NOPE + VD)
    wk = wkvb[:, :, 0:NOPE].reshape(KVL, NH * NOPE)
    wv = wkvb[:, :, NOPE:].reshape(KVL, NH * VD)
    wout = f_out.reshape(MIX, D)
    cwf = cat1(f_cw[:, 0:CW, :])

    half = ROPE // 2
    invf = 1.0 / (ROPE_THETA ** (jnp.arange(half, dtype=F32) / half))
    invf = jnp.concatenate([invf, invf, jnp.zeros((128 - ROPE,), F32)]).reshape(1, 128)
    pos = positions.reshape(S, 1)
    pad128 = lambda a: jnp.concatenate([a.reshape(1, SH), jnp.zeros((1, 128 - SH), F32)], axis=1)
    dtb, alog = pad128(dt_bias), pad128(a_log)
    dskx = jnp.repeat(d_skip.reshape(SH), SP).reshape(1, SW)

    my_c = lax.axis_index("c")
    (call,) = _exchange("gather_c", [jnp.broadcast_to(c.reshape(1, 1, D), (4, 1, D))])
    call = call.reshape(8, D)
    mods = _ada(call, w_ada[0])
    (mrows,) = _exchange("scatter_mod", [mods.reshape(4, 2, 3 * D // 4)])
    mine = lax.dynamic_index_in_dim(mrows.reshape(4, 2, 2, 3 * D // 4)[:, 0], my_c, axis=1, keepdims=False)
    mod = jnp.broadcast_to(mine.reshape(1, 3 * D) + b_ada, (8, 3 * D))
    qlat, ckv, za, xbc, zs, dtraw, q, k, v, vt = _inproj(xv, mod, win, q_norm_g, kv_norm_g, wq, wk, wv, wv.T, pos, invf)
    o, lse = _attn_fwd(q, k, vt)
    xc, y, htp, ossm = _ssd_fwd(xbc, dtraw, zs, cwf, conv_b, dtb, alog, dskx, ssm_norm_g)
    gx1, do, dza, dossm, delta, dwout, vec_o = _outproj(o, za, ossm, xv, tgt, wout, mod, ln_g, ln_b)
    loss_part = jnp.zeros((128,), F32).at[0].set(0.5 / D * jnp.sum(vec_o[0]))

    dk, dv, dq = _attn_bwd(q, k, v, do, lse, delta)
    dxbc, ddt, dzs, dcw, dcb, dvec, dgssm = _ssd_bwd(xbc, xc, dtraw, zs, y, htp, dossm, cwf, dtb, alog, dskx, ssm_norm_g)
    dql, dckv, dkr, dwq, dwk, dwv, dqg, dkvg = _mla_bwd(dq, dk, dv, qlat, ckv, q_norm_g, kv_norm_g, wq, wk, wv, pos, invf)
    gx, dwin, vec_i = _inproj_bwd(xv, gx1, mod, win, dql, dckv, dza, dxbc, dzs, dkr, ddt)
    dmod = jnp.concatenate([vec_i[0:1], vec_i[1:2], vec_o[3:4]], axis=1)

    cols = lambda g: g.reshape(g.shape[0], 4, g.shape[1] // 4).transpose(1, 0, 2)
    g_in = cols(_in_from_padded(dwin)).astype(BF16)
    g_qb = cols(dwq.reshape(QL, NH, HP)[:, :, 0:QKD].reshape(QL, NH * QKD)).astype(BF16)
    g_kvb = cols(jnp.concatenate([dwk.reshape(KVL, NH, NOPE), dwv.reshape(KVL, NH, VD)], axis=2)
                 .reshape(KVL, NH * (NOPE + VD))).astype(BF16)
    g_out = dwout.reshape(4, MIX // 4, D).astype(BF16)
    small = {"b_ada": dmod, "conv_b": dcb, "ssm_norm_g": dgssm, "ln_g": vec_o[1:2], "ln_b": vec_o[2:3],
             "q_norm_g": dqg, "kv_norm_g": dkvg, "dt_bias": dvec[0:1, 0:SH], "a_log": dvec[1:2, 0:SH], "d_skip": dvec[2:3, 0:SH],
             "loss": loss_part}
    small = {n: jnp.broadcast_to(a.reshape(1, -1), (4, a.size)) for n, a in small.items()}
    small["conv_w"] = cols(dcw).reshape(4, CW * CC // 4)
    gsmall = _pack_small(small, 1)

    r_in, r_qb, r_kvb, r_out, rs, dmods = _exchange(
        "exchange_grads", [g_in, g_qb, g_kvb, g_out, gsmall, jnp.broadcast_to(dmod.reshape(1, 1, 3 * D), (4, 1, 3 * D))])
    chip = 2 * lax.axis_index("x") + lax.axis_index("y")
    dmods = lax.dynamic_slice_in_dim(dmods.reshape(8, 3 * D), chip * (3 * D // 4), 3 * D // 4, axis=1)
    g_ada = _ada_bwd(call.T, dmods)
    res = dict(w_ada=_adamw("w_ada", g_ada, w_ada[0], m_w_ada[0], v_w_ada[0]),
               w_in=_adamw("w_in", r_in, w_in[0], m_w_in[0], v_w_in[0]),
               w_qb=_adamw("w_qb", r_qb, w_qb[0], m_w_qb[0], v_w_qb[0]),
               w_kvb=_adamw("w_kvb", r_kvb, w_kvb[0], m_w_kvb[0], v_w_kvb[0]),
               w_out=_adamw("w_out", r_out, w_out[0], m_w_out[0], v_w_out[0]))
    wsm = _pack_small(dict(b_ada=b_ada, conv_w=conv_w, conv_b=conv_b, ssm_norm_g=ssm_norm_g, ln_g=ln_g, ln_b=ln_b,
                           q_norm_g=q_norm_g, kv_norm_g=kv_norm_g, dt_bias=dt_bias, a_log=a_log, d_skip=d_skip, loss=jnp.zeros((128,), F32)), 0)
    msm = _pack_small(dict(b_ada=m_b_ada, conv_w=m_conv_w, conv_b=m_conv_b, ssm_norm_g=m_ssm_norm_g, ln_g=m_ln_g, ln_b=m_ln_b,
                           q_norm_g=m_q_norm_g, kv_norm_g=m_kv_norm_g, dt_bias=m_dt_bias, a_log=m_a_log, d_skip=m_d_skip, loss=jnp.zeros((128,), F32)), 0)
    vsm = _pack_small(dict(b_ada=v_b_ada, conv_w=v_conv_w, conv_b=v_conv_b, ssm_norm_g=v_ssm_norm_g, ln_g=v_ln_g, ln_b=v_ln_b,
                           q_norm_g=v_q_norm_g, kv_norm_g=v_kv_norm_g, dt_bias=v_dt_bias, a_log=v_a_log, d_skip=v_d_skip, loss=jnp.zeros((128,), F32)), 0)
    sm = _adamw("small", rs, wsm, msm, vsm)

    order = ["w_ada", "b_ada", "w_in", "q_norm_g", "w_qb", "kv_norm_g", "w_kvb", "conv_w", "conv_b", "dt_bias", "a_log",
             "d_skip", "ssm_norm_g", "w_out", "ln_g", "ln_b"]
    shapes = dict(w_ada=w_ada.shape, b_ada=b_ada.shape, w_in=w_in.shape, q_norm_g=q_norm_g.shape, w_qb=w_qb.shape,
                  kv_norm_g=kv_norm_g.shape, w_kvb=w_kvb.shape, conv_w=conv_w.shape, conv_b=conv_b.shape, dt_bias=dt_bias.shape,
                  a_log=a_log.shape, d_skip=d_skip.shape, ssm_norm_g=ssm_norm_g.shape, w_out=w_out.shape, ln_g=ln_g.shape,
                  ln_b=ln_b.shape)
    outs = []
    for kind in range(4):
        d = _unpack_small(sm[kind])
        d.update({n: r[kind] for n, r in res.items()})
        outs.extend(d[n].reshape(shapes[n]) for n in order)
    loss = _unpack_small(sm[0])["loss"][0]
    return (loss, gx.reshape(x.shape), *outs)
```

```python
import functools
import math

import numpy as np
import jax
import jax.numpy as jnp
from jax import lax
from jax.experimental import pallas as pl
from jax.experimental.pallas import tpu as pltpu

F32 = jnp.float32
BF16 = jnp.bfloat16
MESH_ID = pl.DeviceIdType.MESH

D = 1024
NH = 8
NOPE = 128
ROPE = 64
VD = 128
VDP = 144
QKD = NOPE + ROPE
HP = 256
QL = 384
KVL = 256
ROPE_THETA = 10000.0
SH = 16
SP = 64
SG = 2
SN = 128
CW = 4
CH = 128
SW = SH * SP
CC = SW + 2 * SG * SN
GW = SW // SG
MIX = 2 * D
IN_W = 4304
ALPHA = 2.0 ** 0.25
RMS_EPS = 1e-6
LN_EPS = 1e-5
SCALE = QKD ** -0.5
LN2 = math.log(2.0)
QSCALE = SCALE / LN2
LR, B1, B2, EPS, WD, STEP = 0.001, 0.9, 0.999, 1e-08, 0.01, 10

P_Q = (0, 384)
P_KV = (384, 640)
P_ZA = (640, 1664)
P_XBC = (1664, 3200)
P_ZS = (3200, 4224)
P_KR = (4224, 4352)
P_DT = (4352, 4480)
IN_P = 4480

R_SMALL = 16

TM = 256
TQ = 512
NSP = 2
TQF = 1024
VMEM_LIMIT = 56 * 1024 * 1024


def _cp(sem=None):
    return pltpu.CompilerParams(dimension_semantics=sem, vmem_limit_bytes=VMEM_LIMIT)


def _mm(a, b):
    return jnp.dot(a, b, preferred_element_type=F32)


def _nt(a, b):
    return lax.dot_general(a, b, (((1,), (1,)), ((), ())), preferred_element_type=F32)


def _tn(a, b):
    return lax.dot_general(a, b, (((0,), (0,)), ((), ())), preferred_element_type=F32)


def _split(a, terms):
    parts = []
    for t in range(terms):
        p = a.astype(BF16)
        parts.append(p)
        if t + 1 < terms:
            a = a - p.astype(F32)
    return parts


def _mm_x(a, ones, terms):
    parts = _split(a, terms)
    out = _mm(parts[0], ones)
    for p in parts[1:]:
        out = out + _mm(p, ones)
    return out


def _xmm(ones, a, terms):
    parts = _split(a, terms)
    out = _mm(ones, parts[0])
    for p in parts[1:]:
        out = out + _mm(ones, p)
    return out


def _nt_x(a, ones, terms):
    parts = _split(a, terms)
    out = _nt(parts[0], ones)
    for p in parts[1:]:
        out = out + _nt(p, ones)
    return out


def _sigmoid(z):
    return 1.0 / (1.0 + jnp.exp(-z))


def _softplus(z):
    return jnp.maximum(z, 0.0) + jnp.log1p(jnp.exp(-jnp.abs(z)))


def _rope(t, cs, sn):
    lane = lax.broadcasted_iota(jnp.int32, t.shape, 1)
    rot = jnp.where(lane < ROPE // 2, -pltpu.roll(t, 128 - ROPE // 2, 1), pltpu.roll(t, ROPE // 2, 1))
    return t * cs + rot * sn


def _rope_t(t, cs, sn):
    lane = lax.broadcasted_iota(jnp.int32, t.shape, 1)
    y = t * sn
    rot = jnp.where(lane < ROPE // 2, -pltpu.roll(y, 128 - ROPE // 2, 1), pltpu.roll(y, ROPE // 2, 1))
    return t * cs - rot


def _full(shape):
    n = len(shape)
    return pl.BlockSpec(shape, lambda *_: (0,) * n)


def _const(shape):
    n = len(shape)
    return pl.BlockSpec(shape, lambda *_: (0,) * n, pipeline_mode=pl.Buffered(1))


def _gather_weights(shards):
    n = len(shards)
    halves = [a.shape[0] // 2 for a in shards]

    def body(*refs):
        srcs, dsts = refs[:n], refs[n:2 * n]
        send_sems, recv_sems, local_sems = refs[2 * n:]
        x, y, c = lax.axis_index("x"), lax.axis_index("y"), lax.axis_index("c")
        me = 2 * x + y
        sibling = (x, y, 1 - c)
        chips = [(1 - x, y), (x, 1 - y), (1 - x, 1 - y)]

        def rows(a, pc):
            return pl.ds(pl.multiple_of(pc * halves[a], halves[a]), halves[a])

        def copy(a, k, src, slot, pc, to):
            return pltpu.make_async_remote_copy(
                src_ref=src, dst_ref=dsts[a].at[slot, rows(a, pc)], send_sem=send_sems.at[a, k],
                recv_sem=recv_sems.at[a, k], device_id=to, device_id_type=MESH_ID)

        local = [pltpu.make_async_copy(srcs[a], dsts[a].at[me], local_sems.at[a]) for a in range(n)]
        for cp in local:
            cp.start()
        sends = [copy(a, j, srcs[a].at[rows(a, c)], me, c, (px, py, c)) for a in range(n) for j, (px, py) in enumerate(chips)]
        for cp in sends:
            cp.start()
        passed = []
        for a in range(n):
            for j, (px, py) in enumerate(chips):
                k = 2 * px + py
                copy(a, j, srcs[a].at[rows(a, c)], k, c, (x, y, c)).wait_recv()
                fwd = copy(a, 3 + j, dsts[a].at[k, rows(a, c)], k, c, sibling)
                fwd.start()
                passed.append(fwd)
        for a in range(n):
            for j, (px, py) in enumerate(chips):
                copy(a, 3 + j, srcs[a].at[rows(a, c)], 2 * px + py, 1 - c, (x, y, c)).wait_recv()
        for cp in sends + passed:
            cp.wait_send()
        for cp in local:
            cp.wait()

    hbm = pl.BlockSpec(memory_space=pltpu.HBM)
    return pl.pallas_call(
        body, name="gather_weights",
        out_shape=tuple(jax.ShapeDtypeStruct((4,) + a.shape, a.dtype) for a in shards),
        in_specs=[hbm] * n, out_specs=tuple([hbm] * n),
        scratch_shapes=[pltpu.SemaphoreType.DMA((n, 6)), pltpu.SemaphoreType.DMA((n, 6)), pltpu.SemaphoreType.DMA((n,))],
    )(*shards)


def _exchange(name, slabs):
    n = len(slabs)

    def body(*refs):
        srcs, dsts = refs[:n], refs[n:2 * n]
        send_sems, recv_sems, local_sems = refs[2 * n:]
        x, y, c = lax.axis_index("x"), lax.axis_index("y"), lax.axis_index("c")
        chip = 2 * x + y
        sibling = (x, y, 1 - c)
        chips = [(1 - x, y), (x, 1 - y), (1 - x, 1 - y)]

        def slot(px, py, pc):
            return 4 * px + 2 * py + pc

        def copy(a, k, src, s, to):
            return pltpu.make_async_remote_copy(
                src_ref=src, dst_ref=dsts[a].at[s], send_sem=send_sems.at[a, k], recv_sem=recv_sems.at[a, k],
                device_id=to, device_id_type=MESH_ID)

        mine = slot(x, y, c)
        local = [pltpu.make_async_copy(srcs[a].at[chip], dsts[a].at[mine], local_sems.at[a]) for a in range(n)]
        for cp in local:
            cp.start()
        first = []
        for a in range(n):
            first.append(copy(a, 0, srcs[a].at[chip], mine, sibling))
            for j, (px, py) in enumerate(chips):
                first.append(copy(a, 1 + j, srcs[a].at[2 * px + py], mine, (px, py, c)))
        for cp in first:
            cp.start()
        passed = []
        for a in range(n):
            for j, (px, py) in enumerate(chips):
                s = slot(px, py, c)
                copy(a, 1 + j, srcs[a].at[chip], s, (x, y, c)).wait_recv()
                fwd = copy(a, 4 + j, dsts[a].at[s], s, sibling)
                fwd.start()
                passed.append(fwd)
        for a in range(n):
            copy(a, 0, srcs[a].at[chip], slot(x, y, 1 - c), (x, y, c)).wait_recv()
            for j, (px, py) in enumerate(chips):
                copy(a, 4 + j, srcs[a].at[chip], slot(px, py, 1 - c), (x, y, c)).wait_recv()
        for cp in first + passed:
            cp.wait_send()
        for cp in local:
            cp.wait()

    hbm = pl.BlockSpec(memory_space=pltpu.HBM)
    return pl.pallas_call(
        body, name=name,
        out_shape=tuple(jax.ShapeDtypeStruct((8,) + a.shape[1:], a.dtype) for a in slabs),
        in_specs=[hbm] * n, out_specs=tuple([hbm] * n),
        scratch_shapes=[pltpu.SemaphoreType.DMA((n, 7)), pltpu.SemaphoreType.DMA((n, 7)), pltpu.SemaphoreType.DMA((n,))],
    )(*slabs)


def _ada(call, w_shard):
    def body(c_ref, w_ref, o_ref):
        o_ref[...] = _mm(c_ref[...].astype(BF16), w_ref[...].astype(BF16))

    return pl.pallas_call(body, name="ada", out_shape=jax.ShapeDtypeStruct((8, w_shard.shape[1]), F32),
                          compiler_params=_cp())(call, w_shard)


def _inproj(x, mod, win, qg, kvg, wq, wk, wv, pos, invf):
    S = x.shape[0]
    tm = min(TM, S)

    def body(x_ref, mod_ref, win_ref, qg_ref, kvg_ref, wq_ref, wk_ref, wv_ref, pos_ref, invf_ref,
             qlat_ref, ckv_ref, za_ref, xbc_ref, zs_ref, dt_ref, q_ref, k_ref, v_ref):
        shift = mod_ref[0:1, 0:D]
        scale = mod_ref[0:1, D:2 * D]
        u = (x_ref[...] * (1.0 + scale) + shift).astype(BF16)

        def proj(p):
            return _mm(u, win_ref[:, p[0]:p[1]])

        ql = proj(P_Q)
        ckv = proj(P_KV)
        qlat_ref[...] = ql
        ckv_ref[...] = ckv
        za_ref[...] = proj(P_ZA)
        xbc_ref[...] = proj(P_XBC)
        zs_ref[...] = proj(P_ZS)
        dt_ref[...] = proj(P_DT)
        kr = proj(P_KR)

        ang = pos_ref[...].astype(F32) * invf_ref[...]
        cs = jnp.cos(ang)
        sn = jnp.sin(ang)

        rq = lax.rsqrt(jnp.mean(ql * ql, axis=-1, keepdims=True) + RMS_EPS)
        qn = (ql * rq * qg_ref[...]).astype(BF16)
        for h in range(NH):
            qh = _mm(qn, wq_ref[:, h * HP:(h + 1) * HP])
            q_ref[h, :, 0:NOPE] = (qh[:, 0:NOPE] * QSCALE).astype(BF16)
            q_ref[h, :, NOPE:HP] = (_rope(qh[:, NOPE:HP], cs, sn) * QSCALE).astype(BF16)

        rk = lax.rsqrt(jnp.mean(ckv * ckv, axis=-1, keepdims=True) + RMS_EPS)
        kn = (ckv * rk * kvg_ref[...]).astype(BF16)
        knope = _mm(kn, wk_ref[...])
        vall = _mm(kn, wv_ref[...])
        krf = _rope(kr, cs, sn)
        krr = krf.astype(BF16)
        for h in range(NH):
            k_ref[h, :, 0:NOPE] = knope[:, h * NOPE:(h + 1) * NOPE].astype(BF16)
            k_ref[h, :, NOPE:HP] = krr
            v_ref[h] = vall[:, h * VD:(h + 1) * VD].astype(BF16)

    row = lambda n: pl.BlockSpec((tm, n), lambda i: (i, 0))
    heads = lambda n: pl.BlockSpec((NH, tm, n), lambda i: (0, i, 0))
    sd = lambda n: jax.ShapeDtypeStruct((S, n), F32)
    hd = lambda n: jax.ShapeDtypeStruct((NH, S, n), BF16)
    return pl.pallas_call(
        body, name="inproj", grid=(S // tm,),
        in_specs=[row(D), _const((8, 3 * D)), _const((D, IN_P)), _const((1, QL)), _const((1, KVL)),
                  _const((QL, NH * HP)), _const((KVL, NH * NOPE)), _const((KVL, NH * VD)),
                  row(1), _const((1, 128))],
        out_specs=[row(QL), row(KVL), row(D), row(CC), row(D), row(128), heads(HP), heads(HP), heads(VD)],
        out_shape=[sd(QL), sd(KVL), sd(D), sd(CC), sd(D), sd(128), hd(HP), hd(HP), hd(VD)],
        compiler_params=_cp(("arbitrary",)),
    )(x, mod, win, qg, kvg, wq, wk, wv, pos, invf)


def _attn_fwd(q, k, v):
    _, S, _ = q.shape
    tq = min(TQF, S)
    nq = S // tq
    half = tq // NSP
    tb = min(256, half)
    nsb = half // tb

    def body(q_ref, k_ref, v_ref, o_ref, lse_ref, vt_ref):
        i = pl.program_id(1)
        qb = q_ref[...]

        @pl.when(i == 0)
        def _():
            ones_rows = jnp.where(lax.broadcasted_iota(jnp.int32, (VDP - VD, tb), 0) == 0, 1.0, 0.0).astype(BF16)

            def fill(blk, carry):
                off = pl.multiple_of(blk * tb, tb)
                vt_ref[blk, 0:VD, :] = v_ref[pl.ds(off, tb), :].T
                vt_ref[blk, VD:VDP, :] = ones_rows
                return carry

            lax.fori_loop(0, S // tb, fill, 0)

        def scores(j, hb):
            off = pl.multiple_of(j * tq + hb * half, half)
            return _nt(k_ref[pl.ds(off, half), :], qb)

        def update(j, hb, s, carry):
            m, acc = carry
            m_new = jnp.maximum(m, jnp.max(s, axis=0, keepdims=True))
            a = jnp.exp2(m - m_new)
            pb = jnp.exp2(s - m_new).astype(BF16)
            acc = a * acc
            for sb in range(nsb):
                acc = acc + _mm(vt_ref[(NSP * j + hb) * nsb + sb], pb[sb * tb:(sb + 1) * tb, :])
            return m_new, acc

        def trip(j, carry, masked):
            s = [scores(j, hb) for hb in range(NSP)]
            if masked:
                r = lax.broadcasted_iota(jnp.int32, s[0].shape, 0)
                cidx = lax.broadcasted_iota(jnp.int32, s[0].shape, 1)
                s = [jnp.where(cidx >= r + hb * half, s[hb], -1e30) for hb in range(NSP)]
            for hb in range(NSP):
                carry = update(j, hb, s[hb], carry)
            return carry

        def finish(carry):
            m, acc = carry
            l = acc[VD:VD + 1, :]
            o_ref[...] = (acc[0:VD, :] / l).T
            lse_ref[...] = m + jnp.log2(l)

        init = (jnp.full((1, tq), -1e30, F32), jnp.zeros((VDP, tq), F32))
        carry = lax.fori_loop(0, i >> 1, lambda t, cr: trip(2 * t + 1, trip(2 * t, cr, False), False), init)

        @pl.when((i & 1) == 0)
        def _():
            finish(trip(i, carry, True))

        @pl.when((i & 1) == 1)
        def _():
            finish(trip(i, trip(i - 1, carry, False), True))

    return pl.pallas_call(
        body, name="attn_fwd", grid=(NH, nq),
        in_specs=[pl.BlockSpec((None, tq, HP), lambda h, i: (h, i, 0)),
                  pl.BlockSpec((None, S, HP), lambda h, i: (h, 0, 0)),
                  pl.BlockSpec((None, S, VD), lambda h, i: (h, 0, 0))],
        out_specs=[pl.BlockSpec((tq, VD), lambda h, i: (i, h)),
                   pl.BlockSpec((None, None, 1, tq), lambda h, i: (h, i, 0, 0))],
        out_shape=[jax.ShapeDtypeStruct((S, NH * VD), F32), jax.ShapeDtypeStruct((NH, nq, 1, tq), F32)],
        scratch_shapes=[pltpu.VMEM((S // tb, VDP, tb), BF16)],
        compiler_params=_cp(("arbitrary", "arbitrary")),
    )(q, k, v)


def _ssd_consts():
    tri = np.tril(np.ones((CH, CH), np.float32))
    e16 = np.zeros((128, SW), np.float32)
    for h in range(SH):
        e16[h, h * SP:(h + 1) * SP] = 1.0
    return jnp.asarray(tri, BF16), jnp.asarray(tri.T.copy(), BF16), jnp.asarray(e16, BF16)


def _ssd_conv(xbc_ref, halo_ref, cw_ref, cb_ref, ext, first):
    ext[0:8, :] = jnp.where(first, 0.0, halo_ref[...])
    ext[8:8 + CH, :] = xbc_ref[...]
    cw = cw_ref[...]
    xc = cb_ref[...] + cw[0:1, :] * ext[5:5 + CH, :]
    for kk in range(1, CW):
        xc = xc + cw[kk:kk + 1, :] * ext[5 + kk:5 + kk + CH, :]
    return xc


def _ssd_chunk_common(xc, dtraw_ref, dtb_ref, alog_ref, tri_ref, e16_ref):
    sact = _sigmoid(xc)
    act = xc * sact
    lane = lax.broadcasted_iota(jnp.int32, (1, 128), 1)
    arow = jnp.where(lane < SH, -jnp.exp(alog_ref[...]), 0.0)
    dtpre = dtraw_ref[...] + dtb_ref[...]
    dt = _softplus(dtpre)
    a = dt * arow
    cum = _xmm(tri_ref[...], a, 3)
    cumx = _mm_x(cum, e16_ref[...], 3)
    dtx = _mm_x(dt, e16_ref[...], 2)
    return sact, act, arow, dtpre, dt, cum, cumx, dtx


def _ssd_fwd(xbc, dtraw, zs, conv_w, conv_b, dtb, alog, dskx, gssm):
    S = xbc.shape[0]
    nc = S // CH
    tri, _, e16 = _ssd_consts()

    def body(xbc_ref, halo_ref, dtraw_ref, zs_ref, cw_ref, cb_ref, dtb_ref, alog_ref, dsk_ref, g_ref, tri_ref, e16_ref,
             xc_ref, y_ref, htp_ref, o_ref, ht, ext):
        i = pl.program_id(0)

        @pl.when(i == 0)
        def _():
            ht[...] = jnp.zeros_like(ht)

        xc = _ssd_conv(xbc_ref, halo_ref, cw_ref, cb_ref, ext, i == 0)
        xc_ref[...] = xc
        sact, act, arow, dtpre, dt, cum, cumx, dtx = _ssd_chunk_common(xc, dtraw_ref, dtb_ref, alog_ref, tri_ref, e16_ref)
        cum_t = cum.T
        xs = act[:, 0:SW]
        lastx = cumx[CH - 1:CH, :]
        xh = xs * dtx
        eexp = jnp.exp(cumx)
        dte = jnp.exp(lastx - cumx)
        cdx = jnp.exp(lastx)
        htp = ht[...]
        htp_ref[...] = htp
        xw = (xh * dte).astype(BF16)
        xb = xh.astype(BF16)
        trim = tri_ref[...].astype(F32) > 0.5
        lane = lax.broadcasted_iota(jnp.int32, (CH, 128), 1)
        parts = []
        for g in range(SG):
            gl = slice(g * GW, (g + 1) * GW)
            bg = act[:, SW + g * SN:SW + (g + 1) * SN].astype(BF16)
            cg = act[:, SW + SG * SN + g * SN:SW + SG * SN + (g + 1) * SN].astype(BF16)
            cbm = _nt(cg, bg)
            yoff = eexp[:, gl] * _mm(cg, htp[:, gl].astype(BF16))
            ht[:, gl] = htp[:, gl] * cdx[:, gl] + _tn(bg, xw[:, gl])
            for pr in range(GW // 128):
                h0 = g * (SH // SG) + 2 * pr
                lo = g * GW + pr * 128
                xp = xb[:, lo:lo + 128]
                res = []
                for hh in (h0, h0 + 1):
                    seg = cum[:, hh:hh + 1] - cum_t[hh:hh + 1, :]
                    mh = jnp.where(trim, cbm * jnp.exp(seg), 0.0).astype(BF16)
                    res.append(_mm(mh, xp))
                parts.append(jnp.where(lane < SP, res[0], res[1]) + yoff[:, pr * 128:(pr + 1) * 128])
        y = jnp.concatenate(parts, axis=1) + xs * dsk_ref[...]
        y_ref[...] = y
        z = zs_ref[...]
        hf = y * (z * _sigmoid(z))
        outs = []
        for g in range(SG):
            hg = hf[:, g * GW:(g + 1) * GW]
            rs = lax.rsqrt(jnp.mean(hg * hg, axis=-1, keepdims=True) + RMS_EPS)
            outs.append(hg * rs)
        o_ref[...] = (jnp.concatenate(outs, axis=1) * g_ref[...]).astype(BF16)

    row = lambda n: pl.BlockSpec((CH, n), lambda i: (i, 0))
    return pl.pallas_call(
        body, name="ssd_fwd", grid=(nc,),
        in_specs=[row(CC), pl.BlockSpec((8, CC), lambda i: (jnp.maximum(i * (CH // 8) - 1, 0), 0)), row(128), row(SW),
                  _const((CW, CC)), _const((1, CC)), _const((1, 128)), _const((1, 128)), _const((1, SW)), _const((1, SW)),
                  _const((CH, CH)), _const((128, SW))],
        out_specs=[row(CC), row(SW), pl.BlockSpec((None, SN, SW), lambda i: (i, 0, 0)), row(SW)],
        out_shape=[jax.ShapeDtypeStruct((S, CC), F32), jax.ShapeDtypeStruct((S, SW), F32),
                   jax.ShapeDtypeStruct((nc, SN, SW), F32), jax.ShapeDtypeStruct((S, SW), BF16)],
        scratch_shapes=[pltpu.VMEM((SN, SW), F32), pltpu.VMEM((8 + CH, CC), F32)],
        compiler_params=_cp(("arbitrary",)),
    )(xbc, xbc, dtraw, zs, conv_w, conv_b, dtb, alog, dskx, gssm, tri, e16)


def _outproj(o, za, ossm, x, tgt, wout, mod, ln_g, ln_b):
    S = x.shape[0]
    tm = min(TM, S)

    e8 = np.zeros((D, 128), np.float32)
    for h in range(NH):
        e8[h * VD:(h + 1) * VD, h] = 1.0
    e8 = jnp.asarray(e8, BF16)

    def body(o_ref, za_ref, os_ref, x_ref, t_ref, w_ref, mod_ref, g_ref, b_ref, e8_ref,
             gx_ref, do_ref, dza_ref, dos_ref, delta_ref, dw_ref, vec_ref):
        i = pl.program_id(0)

        @pl.when(i == 0)
        def _():
            dw_ref[...] = jnp.zeros_like(dw_ref)
            vec_ref[...] = jnp.zeros_like(vec_ref)

        gate = mod_ref[0:1, 2 * D:3 * D]
        ov = o_ref[...]
        z = za_ref[...]
        sz = _sigmoid(z)
        silz = z * sz
        a = (ov * silz).astype(BF16)
        osb = os_ref[...]
        mixed = _mm(a, w_ref[0:D, :]) + _mm(osb, w_ref[D:MIX, :])
        xv = x_ref[...]
        hres = ALPHA * xv + gate * mixed
        mu = jnp.mean(hres, axis=-1, keepdims=True)
        hc = hres - mu
        var = jnp.mean(hc * hc, axis=-1, keepdims=True)
        rstd = lax.rsqrt(var + LN_EPS)
        xhat = hc * rstd
        g = g_ref[...]
        yv = xhat * g + b_ref[...]
        err = yv - t_ref[...]
        dy = err * (1.0 / D)
        vec_ref[0:1, :] += jnp.sum(err * err, axis=0, keepdims=True)
        vec_ref[1:2, :] += jnp.sum(dy * xhat, axis=0, keepdims=True)
        vec_ref[2:3, :] += jnp.sum(dy, axis=0, keepdims=True)
        dxh = dy * g
        dh = rstd * (dxh - jnp.mean(dxh, axis=-1, keepdims=True) - xhat * jnp.mean(dxh * xhat, axis=-1, keepdims=True))
        gx_ref[...] = ALPHA * dh
        vec_ref[3:4, :] += jnp.sum(dh * mixed, axis=0, keepdims=True)
        dmixed = (gate * dh).astype(BF16)
        dw_ref[0:D, :] += _tn(a, dmixed)
        dw_ref[D:MIX, :] += _tn(osb, dmixed)
        da = _nt(dmixed, w_ref[0:D, :])
        dos_ref[...] = _nt(dmixed, w_ref[D:MIX, :])
        dov = da * silz
        do_ref[...] = dov.astype(BF16)
        dza_ref[...] = (da * ov * (sz * (1.0 + z * (1.0 - sz)))).astype(BF16)
        delta_ref[:, 0, :] = _mm_x(dov * ov, e8_ref[...], 2).T[0:NH, :]

    row = lambda n: pl.BlockSpec((tm, n), lambda i: (i, 0))
    return pl.pallas_call(
        body, name="outproj", grid=(S // tm,),
        in_specs=[row(D), row(D), row(D), row(D), row(D), _const((MIX, D)), _const((8, 3 * D)), _const((1, D)), _const((1, D)),
                  _const((D, 128))],
        out_specs=[row(D), row(D), row(D), row(D), pl.BlockSpec((NH, None, 1, tm), lambda i: (0, i, 0, 0)),
                   _full((MIX, D)), _full((8, D))],
        out_shape=[jax.ShapeDtypeStruct((S, D), F32), jax.ShapeDtypeStruct((S, D), BF16), jax.ShapeDtypeStruct((S, D), BF16),
                   jax.ShapeDtypeStruct((S, D), F32), jax.ShapeDtypeStruct((NH, S // tm, 1, tm), F32),
                   jax.ShapeDtypeStruct((MIX, D), F32), jax.ShapeDtypeStruct((8, D), F32)],
        compiler_params=_cp(("arbitrary",)),
    )(o, za, ossm, x, tgt, wout, mod, ln_g, ln_b, e8)


def _attn_bwd(q, k, v, do, lse, delta):
    _, S, _ = q.shape
    tk = min(TQ, S // 2)
    nk = S // tk
    tq = 2 * tk
    nq = S // tq

    def body(k_ref, v_ref, q_ref, do_ref, lse_ref, dl_ref, dk_ref, dv_ref, dq_ref, dqt_ref):
        j = pl.program_id(1)
        kb = k_ref[...]
        ktb = kb.T
        vb = v_ref[...]

        @pl.when(j == 0)
        def _():
            dqt_ref[...] = jnp.zeros_like(dqt_ref)

        dk_ref[...] = jnp.zeros_like(dk_ref)
        dv_ref[...] = jnp.zeros_like(dv_ref)

        def step(i, masked, lo=0):
            off = pl.multiple_of(i * tq + lo, tk)
            qb = q_ref[pl.ds(off, tq - lo), :]
            dob = do_ref[pl.ds(off, tq - lo), :]
            pt = jnp.exp2(_nt(kb, qb) - lse_ref[i][:, lo:tq])
            if masked:
                r = lax.broadcasted_iota(jnp.int32, pt.shape, 0)
                cidx = lax.broadcasted_iota(jnp.int32, pt.shape, 1)
                pt = jnp.where(i * tq + lo + cidx >= j * tk + r, pt, 0.0)
            dv_ref[...] += _mm(pt.astype(BF16), dob)
            dsb = (pt * (_nt(vb, dob) - dl_ref[i][:, lo:tq])).astype(BF16)
            dk_ref[...] += _mm(dsb, qb)
            dqt_ref[i, :, lo:tq] += _mm(ktb, dsb)

        first = j >> 1

        @pl.when((j & 1) == 0)
        def _():
            step(first, True)

        @pl.when((j & 1) == 1)
        def _():
            step(first, True, tk)
            dq_ref[...] = dqt_ref[first].T

        def loop_body(i, carry):
            step(i, False)
            return carry

        lax.fori_loop(first + 1, nq, loop_body, 0)
        dk_ref[...] = dk_ref[...] * LN2

    return pl.pallas_call(
        body, name="attn_bwd", grid=(NH, nk),
        in_specs=[pl.BlockSpec((None, tk, HP), lambda h, j: (h, j, 0)),
                  pl.BlockSpec((None, tk, VD), lambda h, j: (h, j, 0)),
                  pl.BlockSpec((None, S, HP), lambda h, j: (h, 0, 0)),
                  pl.BlockSpec((S, VD), lambda h, j: (0, h)),
                  pl.BlockSpec((None, nq, 1, tq), lambda h, j: (h, 0, 0, 0)),
                  pl.BlockSpec((None, nq, 1, tq), lambda h, j: (h, 0, 0, 0))],
        out_specs=[pl.BlockSpec((None, tk, HP), lambda h, j: (h, j, 0)),
                   pl.BlockSpec((None, tk, VD), lambda h, j: (h, j, 0)),
                   pl.BlockSpec((None, tq, HP), lambda h, j: (h, j >> 1, 0))],
        out_shape=[jax.ShapeDtypeStruct((NH, S, HP), F32), jax.ShapeDtypeStruct((NH, S, VD), F32),
                   jax.ShapeDtypeStruct((NH, S, HP), F32)],
        scratch_shapes=[pltpu.VMEM((nq, HP, tq), F32)],
        compiler_params=_cp(("arbitrary", "arbitrary")),
    )(k, v, q, do, lse.reshape(NH, nq, 1, tq), delta.reshape(NH, nq, 1, tq))


def _ssd_bwd(xbc, xc, dtraw, zs, y, htp, dossm, conv_w, dtb, alog, dskx, gssm):
    S = xbc.shape[0]
    nc = S // CH
    tri, triu, e16 = _ssd_consts()

    def body(xbc_ref, xc_ref, dtraw_ref, zs_ref, y_ref, htp_ref, dos_ref,
             cw_ref, dtb_ref, alog_ref, dsk_ref, g_ref, tri_ref, triu_ref, e16_ref,
             dxbc_ref, ddt_ref, dzs_ref, dcw_ref, dcb_ref, dvec_ref, dg_ref,
             dht, dext, dskacc):
        r = pl.program_id(0)

        @pl.when(r == 0)
        def _():
            dht[...] = jnp.zeros_like(dht)
            dext[CH:CH + 8, :] = jnp.zeros((8, CC), F32)
            dskacc[...] = jnp.zeros_like(dskacc)
            dcw_ref[...] = jnp.zeros_like(dcw_ref)
            dcb_ref[...] = jnp.zeros_like(dcb_ref)
            dvec_ref[...] = jnp.zeros_like(dvec_ref)
            dg_ref[...] = jnp.zeros_like(dg_ref)

        xc = xc_ref[...]
        sact, act, arow, dtpre, dt, cum, cumx, dtx = _ssd_chunk_common(xc, dtraw_ref, dtb_ref, alog_ref, tri_ref, e16_ref)
        cum_t = cum.T
        xs = act[:, 0:SW]
        lastx = cumx[CH - 1:CH, :]
        xh = xs * dtx
        eexp = jnp.exp(cumx)
        dte = jnp.exp(lastx - cumx)
        cdx = jnp.exp(lastx)
        trim = tri_ref[...].astype(F32) > 0.5
        lane = lax.broadcasted_iota(jnp.int32, (CH, 128), 1)
        rowi = lax.broadcasted_iota(jnp.int32, (CH, 128), 0)

        yv = y_ref[...]
        z = zs_ref[...]
        sz = _sigmoid(z)
        silz = z * sz
        hf = yv * silz
        dn = dos_ref[...] * g_ref[...]
        dhf_parts, nrm_parts = [], []
        for g in range(SG):
            gl = slice(g * GW, (g + 1) * GW)
            hg = hf[:, gl]
            rs = lax.rsqrt(jnp.mean(hg * hg, axis=-1, keepdims=True) + RMS_EPS)
            ng = hg * rs
            dng = dn[:, gl]
            dhf_parts.append(rs * (dng - ng * jnp.mean(dng * ng, axis=-1, keepdims=True)))
            nrm_parts.append(ng)
        nrm = jnp.concatenate(nrm_parts, axis=1)
        dhf = jnp.concatenate(dhf_parts, axis=1)
        dg_ref[...] += jnp.sum(dos_ref[...] * nrm, axis=0, keepdims=True)
        dyv = dhf * silz
        dzs_ref[...] = (dhf * yv * (sz * (1.0 + z * (1.0 - sz)))).astype(BF16)
        dskacc[...] += jnp.sum(dyv * xs, axis=0, keepdims=True)
        dxs_skip = dyv * dsk_ref[...]

        dhtn = dht[...]
        hp = htp_ref[...]
        dlastx = jnp.sum(dhtn * hp, axis=0, keepdims=True) * cdx
        xb = xh.astype(BF16)
        xwf = xh * dte
        dcum = jnp.zeros((CH, 128), F32)
        dcum_t = jnp.zeros((128, CH), F32)
        dxh_parts, dcumx_parts, dlast_parts, db_parts, dc_parts = [], [], [], [], []
        for g in range(SG):
            gl = slice(g * GW, (g + 1) * GW)
            bg = act[:, SW + g * SN:SW + (g + 1) * SN].astype(BF16)
            cg = act[:, SW + SG * SN + g * SN:SW + SG * SN + (g + 1) * SN].astype(BF16)
            hpg = hp[:, gl].astype(BF16)
            dhn = dhtn[:, gl].astype(BF16)
            dyg = dyv[:, gl]
            dz = (dyg * eexp[:, gl]).astype(BF16)
            dcg = _nt(dz, hpg)
            dht[:, gl] = dhtn[:, gl] * cdx[:, gl] + _tn(cg, dz)
            yoff = eexp[:, gl] * _mm(cg, hpg)
            dcumx_g = dyg * yoff
            dbg = _nt(xwf[:, gl].astype(BF16), dhn)
            dxw = _mm(bg, dhn)
            ddte = dxw * xwf[:, gl]
            dcumx_parts.append(dcumx_g - ddte)
            dlast_parts.append(jnp.sum(ddte, axis=0, keepdims=True))
            dxh_g = dxw * dte[:, gl]
            cbm = _nt(cg, bg)
            dcb = jnp.zeros((CH, CH), F32)
            dxp_parts = []
            for pr in range(GW // 128):
                h0 = g * (SH // SG) + 2 * pr
                lo = g * GW + pr * 128
                xp = xb[:, lo:lo + 128]
                dyp = dyv[:, lo:lo + 128]
                dxp = jnp.zeros((CH, 128), F32)
                for idx, hh in enumerate((h0, h0 + 1)):
                    decay = jnp.where(trim, jnp.exp(cum[:, hh:hh + 1] - cum_t[hh:hh + 1, :]), 0.0)
                    mh = cbm * decay
                    keep = (lane < SP) if idx == 0 else (lane >= SP)
                    dym = jnp.where(keep, dyp, 0.0).astype(BF16)
                    dm = _nt(dym, xp)
                    dxp = dxp + _tn(mh.astype(BF16), dym)
                    gm = dm * mh
                    dcum = dcum + jnp.where(lane == hh, jnp.sum(gm, axis=1, keepdims=True), 0.0)
                    dcum_t = dcum_t - jnp.where(rowi == hh, jnp.sum(gm, axis=0, keepdims=True), 0.0)
                    dcb = dcb + dm * decay
                dxp_parts.append(dxp)
            dxh_parts.append(dxh_g + jnp.concatenate(dxp_parts, axis=1))
            dcbb = dcb.astype(BF16)
            dc_parts.append(dcg + _mm(dcbb, bg))
            db_parts.append(dbg + _tn(dcbb, cg))
        dxh = jnp.concatenate(dxh_parts, axis=1)
        dcumx = jnp.concatenate(dcumx_parts, axis=1)
        dlastx = dlastx + jnp.concatenate(dlast_parts, axis=1)
        e16 = e16_ref[...]
        dlast128 = _nt_x(jnp.broadcast_to(dlastx, (8, SW)), e16, 2)[0:1, :]
        dcum = dcum + dcum_t.T + _nt_x(dcumx, e16, 2) + jnp.where(rowi == CH - 1, dlast128, 0.0)
        da = _xmm(triu_ref[...], dcum, 2)
        ddt = da * arow + _nt_x(dxh * xs, e16, 2)
        dvec_ref[1:2, :] += jnp.sum(da * dt, axis=0, keepdims=True)
        ddtraw = jnp.where(lane < SH, ddt * _sigmoid(dtpre), 0.0)
        dvec_ref[0:1, :] += jnp.sum(ddtraw, axis=0, keepdims=True)
        ddt_ref[...] = ddtraw.astype(BF16)
        dxs = dxs_skip + dxh * dtx
        dact = jnp.concatenate([dxs] + db_parts + dc_parts, axis=1)
        dxc = dact * (sact * (1.0 + xc * (1.0 - sact)))

        dcb_ref[...] += jnp.sum(dxc, axis=0, keepdims=True)
        dext[0:CH, :] = dxc
        cw = cw_ref[...]
        xraw = xbc_ref[...]
        dxr = cw[CW - 1:CW, :] * dxc
        dcw_ref[CW - 1:CW, :] += jnp.sum(dxc * xraw, axis=0, keepdims=True)
        for kk in range(CW - 1):
            dwin = dext[CW - 1 - kk:CW - 1 - kk + CH, :]
            dxr = dxr + cw[kk:kk + 1, :] * dwin
            dcw_ref[kk:kk + 1, :] += jnp.sum(dwin * xraw, axis=0, keepdims=True)
        dxbc_ref[...] = dxr.astype(BF16)
        dext[CH:CH + 8, :] = dxc[0:8, :]

        @pl.when(r == nc - 1)
        def _():
            dvec_ref[1:2, :] = dvec_ref[1:2, :] * arow
            dvec_ref[2:3, :] = _nt_x(jnp.broadcast_to(dskacc[...], (8, SW)), e16, 3)[0:1, :]

    rev = lambda n: pl.BlockSpec((CH, n), lambda r: (nc - 1 - r, 0))
    return pl.pallas_call(
        body, name="ssd_bwd", grid=(nc,),
        in_specs=[rev(CC), rev(CC),
                  rev(128), rev(SW), rev(SW), pl.BlockSpec((None, SN, SW), lambda r: (nc - 1 - r, 0, 0)), rev(SW),
                  _const((CW, CC)), _const((1, 128)), _const((1, 128)), _const((1, SW)), _const((1, SW)),
                  _const((CH, CH)), _const((CH, CH)), _const((128, SW))],
        out_specs=[rev(CC), rev(128), rev(SW), _full((CW, CC)), _full((1, CC)), _full((8, 128)), _full((1, SW))],
        out_shape=[jax.ShapeDtypeStruct((S, CC), BF16), jax.ShapeDtypeStruct((S, 128), BF16), jax.ShapeDtypeStruct((S, SW), BF16),
                   jax.ShapeDtypeStruct((CW, CC), F32), jax.ShapeDtypeStruct((1, CC), F32),
                   jax.ShapeDtypeStruct((8, 128), F32), jax.ShapeDtypeStruct((1, SW), F32)],
        scratch_shapes=[pltpu.VMEM((SN, SW), F32), pltpu.VMEM((CH + 8, CC), F32), pltpu.VMEM((1, SW), F32)],
        compiler_params=_cp(("arbitrary",)),
    )(xbc, xc, dtraw, zs, y, htp, dossm, conv_w, dtb, alog, dskx, gssm, tri, triu, e16)


def _mla_bwd(dq, dk, dv, qlat, ckv, qg, kvg, wq, wk, wv, pos, invf):
    S = qlat.shape[0]
    tm = min(TQ, S)

    def body(dq_ref, dk_ref, dv_ref, ql_ref, ckv_ref, qg_ref, kvg_ref, wq_ref, wk_ref, wv_ref, pos_ref, invf_ref,
             dql_ref, dckv_ref, dkr_ref, dwq_ref, dwk_ref, dwv_ref, dqg_ref, dkvg_ref):
        i = pl.program_id(0)

        @pl.when(i == 0)
        def _():
            dwq_ref[...] = jnp.zeros_like(dwq_ref)
            dwk_ref[...] = jnp.zeros_like(dwk_ref)
            dwv_ref[...] = jnp.zeros_like(dwv_ref)
            dqg_ref[...] = jnp.zeros_like(dqg_ref)
            dkvg_ref[...] = jnp.zeros_like(dkvg_ref)

        ang = pos_ref[...].astype(F32) * invf_ref[...]
        cs = jnp.cos(ang)
        sn = jnp.sin(ang)

        def rms_bwd(v, g, dn, dg_ref):
            r = lax.rsqrt(jnp.mean(v * v, axis=-1, keepdims=True) + RMS_EPS)
            vh = v * r
            dg_ref[...] += jnp.sum(dn * vh, axis=0, keepdims=True)
            dvh = dn * g
            return vh, r * (dvh - vh * jnp.mean(dvh * vh, axis=-1, keepdims=True))

        pieces = []
        for h in range(NH):
            dqh = dq_ref[h]
            pieces.append(dqh[:, 0:NOPE] * SCALE)
            pieces.append(_rope_t(dqh[:, NOPE:HP], cs, sn) * SCALE)
        dqf = jnp.concatenate(pieces, axis=1).astype(BF16)
        ql = ql_ref[...]
        g = qg_ref[...]
        dqn = _nt(dqf, wq_ref[...])
        qh, dql = rms_bwd(ql, g, dqn, dqg_ref)
        dwq_ref[...] += _tn((qh * g).astype(BF16), dqf)
        dql_ref[...] = dql.astype(BF16)

        dkn_p = jnp.concatenate([dk_ref[h, :, 0:NOPE] for h in range(NH)], axis=1).astype(BF16)
        dvf = jnp.concatenate([dv_ref[h] for h in range(NH)], axis=1).astype(BF16)
        dkr = dk_ref[0, :, NOPE:HP]
        for h in range(1, NH):
            dkr = dkr + dk_ref[h, :, NOPE:HP]
        lane = lax.broadcasted_iota(jnp.int32, dkr.shape, 1)
        dkr_ref[...] = jnp.where(lane < ROPE, _rope_t(dkr, cs, sn), 0.0).astype(BF16)
        cv = ckv_ref[...]
        gk = kvg_ref[...]
        dkn = _nt(dkn_p, wk_ref[...]) + _nt(dvf, wv_ref[...])
        kh, dckv = rms_bwd(cv, gk, dkn, dkvg_ref)
        knb = (kh * gk).astype(BF16)
        dwk_ref[...] += _tn(knb, dkn_p)
        dwv_ref[...] += _tn(knb, dvf)
        dckv_ref[...] = dckv.astype(BF16)

    row = lambda n: pl.BlockSpec((tm, n), lambda i: (i, 0))
    heads = lambda n: pl.BlockSpec((NH, tm, n), lambda i: (0, i, 0))
    return pl.pallas_call(
        body, name="mla_bwd", grid=(S // tm,),
        in_specs=[heads(HP), heads(HP), heads(VD), row(QL), row(KVL), _const((1, QL)), _const((1, KVL)),
                  _const((QL, NH * HP)), _const((KVL, NH * NOPE)), _const((KVL, NH * VD)), row(1), _const((1, 128))],
        out_specs=[row(QL), row(KVL), row(128), _full((QL, NH * HP)), _full((KVL, NH * NOPE)), _full((KVL, NH * VD)),
                   _full((1, QL)), _full((1, KVL))],
        out_shape=[jax.ShapeDtypeStruct((S, QL), BF16), jax.ShapeDtypeStruct((S, KVL), BF16), jax.ShapeDtypeStruct((S, 128), BF16),
                   jax.ShapeDtypeStruct((QL, NH * HP), F32), jax.ShapeDtypeStruct((KVL, NH * NOPE), F32),
                   jax.ShapeDtypeStruct((KVL, NH * VD), F32), jax.ShapeDtypeStruct((1, QL), F32), jax.ShapeDtypeStruct((1, KVL), F32)],
        compiler_params=_cp(("arbitrary",)),
    )(dq, dk, dv, qlat, ckv, qg, kvg, wq, wk, wv, pos, invf)


def _inproj_bwd(x, gx1, mod, win, dql, dckv, dza, dxbc, dzs, dkr, ddt):
    S = x.shape[0]
    tm = min(TM, S)

    def body(x_ref, gx1_ref, mod_ref, win_ref, dql_ref, dckv_ref, dza_ref, dxbc_ref, dzs_ref, dkr_ref, ddt_ref,
             gx_ref, dw_ref, vec_ref):
        i = pl.program_id(0)

        @pl.when(i == 0)
        def _():
            dw_ref[...] = jnp.zeros_like(dw_ref)
            vec_ref[...] = jnp.zeros_like(vec_ref)

        shift = mod_ref[0:1, 0:D]
        scale = mod_ref[0:1, D:2 * D]
        xv = x_ref[...]
        ut = (xv * (1.0 + scale) + shift).T.astype(BF16)
        pieces = (dql_ref, dckv_ref, dza_ref, dxbc_ref, dzs_ref, dkr_ref, ddt_ref)
        du = jnp.zeros((tm, D), F32)
        lo = 0
        for p_ref in pieces:
            n = p_ref.shape[1]
            dp = p_ref[...]
            du = du + _nt(dp, win_ref[:, lo:lo + n])
            dw_ref[:, lo:lo + n] += _mm(ut, dp)
            lo += n
        vec_ref[0:1, :] += jnp.sum(du, axis=0, keepdims=True)
        vec_ref[1:2, :] += jnp.sum(du * xv, axis=0, keepdims=True)
        gx_ref[...] = gx1_ref[...] + du * (1.0 + scale)

    row = lambda n: pl.BlockSpec((tm, n), lambda i: (i, 0))
    return pl.pallas_call(
        body, name="inproj_bwd", grid=(S // tm,),
        in_specs=[row(D), row(D), _const((8, 3 * D)), _const((D, IN_P)), row(QL), row(KVL), row(D), row(CC), row(D),
                  row(128), row(128)],
        out_specs=[row(D), pl.BlockSpec((D, IN_P), lambda i: (0, 0), pipeline_mode=pl.Buffered(1)), _full((8, D))],
        out_shape=[jax.ShapeDtypeStruct((S, D), F32), jax.ShapeDtypeStruct((D, IN_P), F32), jax.ShapeDtypeStruct((8, D), F32)],
        compiler_params=_cp(("arbitrary",)),
    )(x, gx1, mod, win, dql, dckv, dza, dxbc, dzs, dkr, ddt)


def _ada_bwd(callt, dmods):
    w = dmods.shape[1]

    def body(c_ref, d_ref, o_ref):
        acc = c_ref[:, 0:1] * d_ref[0:1, :]
        for s in range(1, 8):
            acc = acc + c_ref[:, s:s + 1] * d_ref[s:s + 1, :]
        o_ref[0] = acc

    return pl.pallas_call(body, name="ada_bwd", out_shape=jax.ShapeDtypeStruct((1, D, w), F32),
                          compiler_params=_cp())(callt, dmods)


def _adamw(name, parts, w, m, v):
    rows, ncol = w.shape
    tr = min(rows, 128)
    nparts = parts.shape[0]

    def body(p_ref, w_ref, m_ref, v_ref, g_ref, d_ref, nm_ref, nv_ref):
        g = p_ref[0].astype(F32)
        for s in range(1, nparts):
            g = g + p_ref[s].astype(F32)
        g_ref[...] = g
        nm = B1 * m_ref[...] + (1.0 - B1) * g
        nv = B2 * v_ref[...] + (1.0 - B2) * (g * g)
        nm_ref[...] = nm
        nv_ref[...] = nv
        m_hat = nm / (1.0 - B1 ** STEP)
        v_hat = nv / (1.0 - B2 ** STEP)
        d_ref[...] = -LR * (m_hat / (jnp.sqrt(v_hat) + EPS) + WD * w_ref[...])

    row = pl.BlockSpec((tr, ncol), lambda i: (i, 0))
    sd = jax.ShapeDtypeStruct((rows, ncol), F32)
    return pl.pallas_call(
        body, name="adamw_" + name, grid=(rows // tr,),
        in_specs=[pl.BlockSpec((nparts, tr, ncol), lambda i: (0, i, 0)), row, row, row],
        out_specs=[row, row, row, row], out_shape=[sd, sd, sd, sd],
        compiler_params=_cp(("arbitrary",)),
    )(parts, w, m, v)


_SMALL = (("b_ada", 3 * D), ("conv_w", CW * CC // 4), ("conv_b", CC), ("ssm_norm_g", SW), ("ln_g", D), ("ln_b", D),
          ("q_norm_g", QL), ("kv_norm_g", KVL), ("dt_bias", SH), ("a_log", SH), ("d_skip", SH), ("loss", 128))


def _pack_small(d, lead):
    flat = [d[name].reshape(d[name].shape[:lead] + (-1,)) for name, _ in _SMALL]
    used = sum(f.shape[lead] for f in flat)
    pad = jnp.zeros(flat[0].shape[:lead] + (R_SMALL * 1024 - used,), F32)
    return jnp.concatenate(flat + [pad], axis=lead).reshape(flat[0].shape[:lead] + (R_SMALL, 1024))


def _unpack_small(p):
    flat = p.reshape(-1)
    out, r = {}, 0
    for name, n in _SMALL:
        out[name] = flat[r:r + n]
        r += n
    return out


def _in_to_padded(w):
    z = lambda n: jnp.zeros((w.shape[0], n), w.dtype)
    return jnp.concatenate([w[:, 0:384], w[:, 384:640], w[:, 704:1728], w[:, 1728:3264], w[:, 3280:4304],
                            w[:, 640:704], z(64), w[:, 3264:3280], z(112)], axis=1)


def _in_from_padded(g):
    return jnp.concatenate([g[:, 0:384], g[:, 384:640], g[:, P_KR[0]:P_KR[0] + 64], g[:, 640:1664], g[:, 1664:3200],
                            g[:, P_DT[0]:P_DT[0] + 16], g[:, 3200:4224]], axis=1)


def kernel(x, c, positions, w_ada, b_ada, w_in, q_norm_g, w_qb, kv_norm_g, w_kvb, conv_w, conv_b, dt_bias, a_log, d_skip, ssm_norm_g, w_out, ln_g, ln_b, loss_target, m_w_ada, m_b_ada, m_w_in, m_q_norm_g, m_w_qb, m_kv_norm_g, m_w_kvb, m_conv_w, m_conv_b, m_dt_bias, m_a_log, m_d_skip, m_ssm_norm_g, m_w_out, m_ln_g, m_ln_b, v_w_ada, v_b_ada, v_w_in, v_q_norm_g, v_w_qb, v_kv_norm_g, v_w_kvb, v_conv_w, v_conv_b, v_dt_bias, v_a_log, v_d_skip, v_ssm_norm_g, v_w_out, v_ln_g, v_ln_b):
    S = x.shape[1]
    xv = x[0]
    tgt = loss_target[0]

    cw16 = jnp.concatenate([conv_w[0], jnp.zeros((16 - CW, CC // 4), F32)], axis=0)
    f_in, f_qb, f_kvb, f_out, f_cw = _gather_weights(
        [w_in[0].astype(BF16), w_qb[0].astype(BF16), w_kvb[0].astype(BF16), w_out[0].astype(BF16), cw16])
    cat1 = lambda f: f.transpose(1, 0, 2).reshape(f.shape[1], 4 * f.shape[2])
    win = _in_to_padded(cat1(f_in))
    wqb = cat1(f_qb).reshape(QL, NH, QKD)
    wq = jnp.concatenate([wqb, jnp.zeros((QL, NH, HP - QKD), BF16)], axis=2).reshape(QL, NH * HP)
    wkvb = cat1(f_kvb).reshape(KVL, NH, NOPE + VD)
    wk = wkvb[:, :, 0:NOPE].reshape(KVL, NH * NOPE)
    wv = wkvb[:, :, NOPE:].reshape(KVL, NH * VD)
    wout = f_out.reshape(MIX, D)
    cwf = cat1(f_cw[:, 0:CW, :])

    half = ROPE // 2
    invf = 1.0 / (ROPE_THETA ** (jnp.arange(half, dtype=F32) / half))
    invf = jnp.concatenate([invf, invf, jnp.zeros((128 - ROPE,), F32)]).reshape(1, 128)
    pos = positions.reshape(S, 1)
    pad128 = lambda a: jnp.concatenate([a.reshape(1, SH), jnp.zeros((1, 128 - SH), F32)], axis=1)
    dtb, alog = pad128(dt_bias), pad128(a_log)
    dskx = jnp.repeat(d_skip.reshape(SH), SP).reshape(1, SW)

    my_c = lax.axis_index("c")
    (call,) = _exchange("gather_c", [jnp.broadcast_to(c.reshape(1, 1, D), (4, 1, D))])
    call = call.reshape(8, D)
    mods = _ada(call, w_ada[0])
    (mrows,) = _exchange("scatter_mod", [mods.reshape(4, 2, 3 * D // 4)])
    mine = lax.dynamic_index_in_dim(mrows.reshape(4, 2, 2, 3 * D // 4)[:, 0], my_c, axis=1, keepdims=False)
    mod = jnp.broadcast_to(mine.reshape(1, 3 * D) + b_ada, (8, 3 * D))
    qlat, ckv, za, xbc, zs, dtraw, q, k, v = _inproj(xv, mod, win, q_norm_g, kv_norm_g, wq, wk, wv, pos, invf)
    o, lse = _attn_fwd(q, k, v)
    xc, y, htp, ossm = _ssd_fwd(xbc, dtraw, zs, cwf, conv_b, dtb, alog, dskx, ssm_norm_g)
    gx1, do, dza, dossm, delta, dwout, vec_o = _outproj(o, za, ossm, xv, tgt, wout, mod, ln_g, ln_b)
    loss_part = jnp.zeros((128,), F32).at[0].set(0.5 / D * jnp.sum(vec_o[0]))

    dk, dv, dq = _attn_bwd(q, k, v, do, lse, delta)
    dxbc, ddt, dzs, dcw, dcb, dvec, dgssm = _ssd_bwd(xbc, xc, dtraw, zs, y, htp, dossm, cwf, dtb, alog, dskx, ssm_norm_g)
    dql, dckv, dkr, dwq, dwk, dwv, dqg, dkvg = _mla_bwd(dq, dk, dv, qlat, ckv, q_norm_g, kv_norm_g, wq, wk, wv, pos, invf)
    gx, dwin, vec_i = _inproj_bwd(xv, gx1, mod, win, dql, dckv, dza, dxbc, dzs, dkr, ddt)
    dmod = jnp.concatenate([vec_i[0:1], vec_i[1:2], vec_o[3:4]], axis=1)

    cols = lambda g: g.reshape(g.shape[0], 4, g.shape[1] // 4).transpose(1, 0, 2)
    g_in = cols(_in_from_padded(dwin)).astype(BF16)
    g_qb = cols(dwq.reshape(QL, NH, HP)[:, :, 0:QKD].reshape(QL, NH * QKD)).astype(BF16)
    g_kvb = cols(jnp.concatenate([dwk.reshape(KVL, NH, NOPE), dwv.reshape(KVL, NH, VD)], axis=2)
                 .reshape(KVL, NH * (NOPE + VD))).astype(BF16)
    g_out = dwout.reshape(4, MIX // 4, D).astype(BF16)
    small = {"b_ada": dmod, "conv_b": dcb, "ssm_norm_g": dgssm, "ln_g": vec_o[1:2], "ln_b": vec_o[2:3],
             "q_norm_g": dqg, "kv_norm_g": dkvg, "dt_bias": dvec[0:1, 0:SH], "a_log": dvec[1:2, 0:SH], "d_skip": dvec[2:3, 0:SH],
             "loss": loss_part}
    small = {n: jnp.broadcast_to(a.reshape(1, -1), (4, a.size)) for n, a in small.items()}
    small["conv_w"] = cols(dcw).reshape(4, CW * CC // 4)
    gsmall = _pack_small(small, 1)

    r_in, r_qb, r_kvb, r_out, rs, dmods = _exchange(
        "exchange_grads", [g_in, g_qb, g_kvb, g_out, gsmall, jnp.broadcast_to(dmod.reshape(1, 1, 3 * D), (4, 1, 3 * D))])
    chip = 2 * lax.axis_index("x") + lax.axis_index("y")
    dmods = lax.dynamic_slice_in_dim(dmods.reshape(8, 3 * D), chip * (3 * D // 4), 3 * D // 4, axis=1)
    g_ada = _ada_bwd(call.T, dmods)
    res = dict(w_ada=_adamw("w_ada", g_ada, w_ada[0], m_w_ada[0], v_w_ada[0]),
               w_in=_adamw("w_in", r_in, w_in[0], m_w_in[0], v_w_in[0]),
               w_qb=_adamw("w_qb", r_qb, w_qb[0], m_w_qb[0], v_w_qb[0]),
               w_kvb=_adamw("w_kvb", r_kvb, w_kvb[0], m_w_kvb[0], v_w_kvb[0]),
               w_out=_adamw("w_out", r_out, w_out[0], m_w_out[0], v_w_out[0]))
    wsm = _pack_small(dict(b_ada=b_ada, conv_w=conv_w, conv_b=conv_b, ssm_norm_g=ssm_norm_g, ln_g=ln_g, ln_b=ln_b,
                           q_norm_g=q_norm_g, kv_norm_g=kv_norm_g, dt_bias=dt_bias, a_log=a_log, d_skip=d_skip, loss=jnp.zeros((128,), F32)), 0)
    msm = _pack_small(dict(b_ada=m_b_ada, conv_w=m_conv_w, conv_b=m_conv_b, ssm_norm_g=m_ssm_norm_g, ln_g=m_ln_g, ln_b=m_ln_b,
                           q_norm_g=m_q_norm_g, kv_norm_g=m_kv_norm_g, dt_bias=m_dt_bias, a_log=m_a_log, d_skip=m_d_skip, loss=jnp.zeros((128,), F32)), 0)
    vsm = _pack_small(dict(b_ada=v_b_ada, conv_w=v_conv_w, conv_b=v_conv_b, ssm_norm_g=v_ssm_norm_g, ln_g=v_ln_g, ln_b=v_ln_b,
                           q_norm_g=v_q_norm_g, kv_norm_g=v_kv_norm_g, dt_bias=v_dt_bias, a_log=v_a_log, d_skip=v_d_skip, loss=jnp.zeros((128,), F32)), 0)
    sm = _adamw("small", rs, wsm, msm, vsm)

    order = ["w_ada", "b_ada", "w_in", "q_norm_g", "w_qb", "kv_norm_g", "w_kvb", "conv_w", "conv_b", "dt_bias", "a_log",
             "d_skip", "ssm_norm_g", "w_out", "ln_g", "ln_b"]
    shapes = dict(w_ada=w_ada.shape, b_ada=b_ada.shape, w_in=w_in.shape, q_norm_g=q_norm_g.shape, w_qb=w_qb.shape,
                  kv_norm_g=kv_norm_g.shape, w_kvb=w_kvb.shape, conv_w=conv_w.shape, conv_b=conv_b.shape, dt_bias=dt_bias.shape,
                  a_log=a_log.shape, d_skip=d_skip.shape, ssm_norm_g=ssm_norm_g.shape, w_out=w_out.shape, ln_g=ln_g.shape,
                  ln_b=ln_b.shape)
    outs = []
    for kind in range(4):
        d = _unpack_small(sm[kind])
        d.update({n: r[kind] for n, r in res.items()})
        outs.extend(d[n].reshape(shapes[n]) for n in order)
    loss = _unpack_small(sm[0])["loss"][0]
    return (loss, gx.reshape(x.shape), *outs)
```

```python
import functools
import math

import numpy as np
import jax
import jax.numpy as jnp
from jax import lax
from jax.experimental import pallas as pl
from jax.experimental.pallas import tpu as pltpu

F32 = jnp.float32
BF16 = jnp.bfloat16
MESH_ID = pl.DeviceIdType.MESH

D = 1024
NH = 8
NOPE = 128
ROPE = 64
VD = 128
VDP = 144
QKD = NOPE + ROPE
HP = 256
QL = 384
KVL = 256
ROPE_THETA = 10000.0
SH = 16
SP = 64
SG = 2
SN = 128
CW = 4
CH = 128
SW = SH * SP
CC = SW + 2 * SG * SN
GW = SW // SG
MIX = 2 * D
IN_W = 4304
ALPHA = 2.0 ** 0.25
RMS_EPS = 1e-6
LN_EPS = 1e-5
SCALE = QKD ** -0.5
LN2 = math.log(2.0)
QSCALE = SCALE / LN2
LR, B1, B2, EPS, WD, STEP = 0.001, 0.9, 0.999, 1e-08, 0.01, 10

P_Q = (0, 384)
P_KV = (384, 640)
P_ZA = (640, 1664)
P_XBC = (1664, 3200)
P_ZS = (3200, 4224)
P_KR = (4224, 4352)
P_DT = (4352, 4480)
IN_P = 4480

R_SMALL = 16

TM = 256
TQ = 512
NSP = 2
TQF = 1024
VMEM_LIMIT = 56 * 1024 * 1024


def _cp(sem=None):
    return pltpu.CompilerParams(dimension_semantics=sem, vmem_limit_bytes=VMEM_LIMIT)


def _mm(a, b):
    return jnp.dot(a, b, preferred_element_type=F32)


def _nt(a, b):
    return lax.dot_general(a, b, (((1,), (1,)), ((), ())), preferred_element_type=F32)


def _tn(a, b):
    return lax.dot_general(a, b, (((0,), (0,)), ((), ())), preferred_element_type=F32)


def _split(a, terms):
    parts = []
    for t in range(terms):
        p = a.astype(BF16)
        parts.append(p)
        if t + 1 < terms:
            a = a - p.astype(F32)
    return parts


def _mm_x(a, ones, terms):
    parts = _split(a, terms)
    out = _mm(parts[0], ones)
    for p in parts[1:]:
        out = out + _mm(p, ones)
    return out


def _xmm(ones, a, terms):
    parts = _split(a, terms)
    out = _mm(ones, parts[0])
    for p in parts[1:]:
        out = out + _mm(ones, p)
    return out


def _nt_x(a, ones, terms):
    parts = _split(a, terms)
    out = _nt(parts[0], ones)
    for p in parts[1:]:
        out = out + _nt(p, ones)
    return out


def _sigmoid(z):
    return 1.0 / (1.0 + jnp.exp(-z))


def _softplus(z):
    return jnp.maximum(z, 0.0) + jnp.log1p(jnp.exp(-jnp.abs(z)))


def _rope(t, cs, sn):
    lane = lax.broadcasted_iota(jnp.int32, t.shape, 1)
    rot = jnp.where(lane < ROPE // 2, -pltpu.roll(t, 128 - ROPE // 2, 1), pltpu.roll(t, ROPE // 2, 1))
    return t * cs + rot * sn


def _rope_t(t, cs, sn):
    lane = lax.broadcasted_iota(jnp.int32, t.shape, 1)
    y = t * sn
    rot = jnp.where(lane < ROPE // 2, -pltpu.roll(y, 128 - ROPE // 2, 1), pltpu.roll(y, ROPE // 2, 1))
    return t * cs - rot


def _full(shape):
    n = len(shape)
    return pl.BlockSpec(shape, lambda *_: (0,) * n)


def _const(shape):
    n = len(shape)
    return pl.BlockSpec(shape, lambda *_: (0,) * n, pipeline_mode=pl.Buffered(1))


def _gather_weights(shards):
    n = len(shards)
    halves = [a.shape[0] // 2 for a in shards]

    def body(*refs):
        srcs, dsts = refs[:n], refs[n:2 * n]
        send_sems, recv_sems, local_sems = refs[2 * n:]
        x, y, c = lax.axis_index("x"), lax.axis_index("y"), lax.axis_index("c")
        me = 2 * x + y
        sibling = (x, y, 1 - c)
        chips = [(1 - x, y), (x, 1 - y), (1 - x, 1 - y)]

        def rows(a, pc):
            return pl.ds(pl.multiple_of(pc * halves[a], halves[a]), halves[a])

        def copy(a, k, src, slot, pc, to):
            return pltpu.make_async_remote_copy(
                src_ref=src, dst_ref=dsts[a].at[slot, rows(a, pc)], send_sem=send_sems.at[a, k],
                recv_sem=recv_sems.at[a, k], device_id=to, device_id_type=MESH_ID)

        local = [pltpu.make_async_copy(srcs[a], dsts[a].at[me], local_sems.at[a]) for a in range(n)]
        for cp in local:
            cp.start()
        sends = [copy(a, j, srcs[a].at[rows(a, c)], me, c, (px, py, c)) for a in range(n) for j, (px, py) in enumerate(chips)]
        for cp in sends:
            cp.start()
        passed = []
        for a in range(n):
            for j, (px, py) in enumerate(chips):
                k = 2 * px + py
                copy(a, j, srcs[a].at[rows(a, c)], k, c, (x, y, c)).wait_recv()
                fwd = copy(a, 3 + j, dsts[a].at[k, rows(a, c)], k, c, sibling)
                fwd.start()
                passed.append(fwd)
        for a in range(n):
            for j, (px, py) in enumerate(chips):
                copy(a, 3 + j, srcs[a].at[rows(a, c)], 2 * px + py, 1 - c, (x, y, c)).wait_recv()
        for cp in sends + passed:
            cp.wait_send()
        for cp in local:
            cp.wait()

    hbm = pl.BlockSpec(memory_space=pltpu.HBM)
    return pl.pallas_call(
        body, name="gather_weights",
        out_shape=tuple(jax.ShapeDtypeStruct((4,) + a.shape, a.dtype) for a in shards),
        in_specs=[hbm] * n, out_specs=tuple([hbm] * n),
        scratch_shapes=[pltpu.SemaphoreType.DMA((n, 6)), pltpu.SemaphoreType.DMA((n, 6)), pltpu.SemaphoreType.DMA((n,))],
    )(*shards)


def _exchange(name, slabs):
    n = len(slabs)

    def body(*refs):
        srcs, dsts = refs[:n], refs[n:2 * n]
        send_sems, recv_sems, local_sems = refs[2 * n:]
        x, y, c = lax.axis_index("x"), lax.axis_index("y"), lax.axis_index("c")
        chip = 2 * x + y
        sibling = (x, y, 1 - c)
        chips = [(1 - x, y), (x, 1 - y), (1 - x, 1 - y)]

        def slot(px, py, pc):
            return 4 * px + 2 * py + pc

        def copy(a, k, src, s, to):
            return pltpu.make_async_remote_copy(
                src_ref=src, dst_ref=dsts[a].at[s], send_sem=send_sems.at[a, k], recv_sem=recv_sems.at[a, k],
                device_id=to, device_id_type=MESH_ID)

        mine = slot(x, y, c)
        local = [pltpu.make_async_copy(srcs[a].at[chip], dsts[a].at[mine], local_sems.at[a]) for a in range(n)]
        for cp in local:
            cp.start()
        first = []
        for a in range(n):
            first.append(copy(a, 0, srcs[a].at[chip], mine, sibling))
            for j, (px, py) in enumerate(chips):
                first.append(copy(a, 1 + j, srcs[a].at[2 * px + py], mine, (px, py, c)))
        for cp in first:
            cp.start()
        passed = []
        for a in range(n):
            for j, (px, py) in enumerate(chips):
                s = slot(px, py, c)
                copy(a, 1 + j, srcs[a].at[chip], s, (x, y, c)).wait_recv()
                fwd = copy(a, 4 + j, dsts[a].at[s], s, sibling)
                fwd.start()
                passed.append(fwd)
        for a in range(n):
            copy(a, 0, srcs[a].at[chip], slot(x, y, 1 - c), (x, y, c)).wait_recv()
            for j, (px, py) in enumerate(chips):
                copy(a, 4 + j, srcs[a].at[chip], slot(px, py, 1 - c), (x, y, c)).wait_recv()
        for cp in first + passed:
            cp.wait_send()
        for cp in local:
            cp.wait()

    hbm = pl.BlockSpec(memory_space=pltpu.HBM)
    return pl.pallas_call(
        body, name=name,
        out_shape=tuple(jax.ShapeDtypeStruct((8,) + a.shape[1:], a.dtype) for a in slabs),
        in_specs=[hbm] * n, out_specs=tuple([hbm] * n),
        scratch_shapes=[pltpu.SemaphoreType.DMA((n, 7)), pltpu.SemaphoreType.DMA((n, 7)), pltpu.SemaphoreType.DMA((n,))],
    )(*slabs)


def _ada(call, w_shard):
    def body(c_ref, w_ref, o_ref):
        o_ref[...] = _mm(c_ref[...].astype(BF16), w_ref[...].astype(BF16))

    return pl.pallas_call(body, name="ada", out_shape=jax.ShapeDtypeStruct((8, w_shard.shape[1]), F32),
                          compiler_params=_cp())(call, w_shard)


def _inproj(x, mod, win, qg, kvg, wq, wk, wv, pos, invf):
    S = x.shape[0]
    tm = min(TM, S)

    def body(x_ref, mod_ref, win_ref, qg_ref, kvg_ref, wq_ref, wk_ref, wv_ref, pos_ref, invf_ref,
             qlat_ref, ckv_ref, za_ref, xbc_ref, zs_ref, dt_ref, q_ref, k_ref, v_ref):
        shift = mod_ref[0:1, 0:D]
        scale = mod_ref[0:1, D:2 * D]
        u = (x_ref[...] * (1.0 + scale) + shift).astype(BF16)

        def proj(p):
            return _mm(u, win_ref[:, p[0]:p[1]])

        ql = proj(P_Q)
        ckv = proj(P_KV)
        qlat_ref[...] = ql
        ckv_ref[...] = ckv
        za_ref[...] = proj(P_ZA)
        xbc_ref[...] = proj(P_XBC)
        zs_ref[...] = proj(P_ZS)
        dt_ref[...] = proj(P_DT)
        kr = proj(P_KR)

        ang = pos_ref[...].astype(F32) * invf_ref[...]
        cs = jnp.cos(ang)
        sn = jnp.sin(ang)

        rq = lax.rsqrt(jnp.mean(ql * ql, axis=-1, keepdims=True) + RMS_EPS)
        qn = (ql * rq * qg_ref[...]).astype(BF16)
        for h in range(NH):
            qh = _mm(qn, wq_ref[:, h * HP:(h + 1) * HP])
            q_ref[h, :, 0:NOPE] = (qh[:, 0:NOPE] * QSCALE).astype(BF16)
            q_ref[h, :, NOPE:HP] = (_rope(qh[:, NOPE:HP], cs, sn) * QSCALE).astype(BF16)

        rk = lax.rsqrt(jnp.mean(ckv * ckv, axis=-1, keepdims=True) + RMS_EPS)
        kn = (ckv * rk * kvg_ref[...]).astype(BF16)
        knope = _mm(kn, wk_ref[...])
        vall = _mm(kn, wv_ref[...])
        krf = _rope(kr, cs, sn)
        krr = krf.astype(BF16)
        for h in range(NH):
            k_ref[h, :, 0:NOPE] = knope[:, h * NOPE:(h + 1) * NOPE].astype(BF16)
            k_ref[h, :, NOPE:HP] = krr
            v_ref[h] = vall[:, h * VD:(h + 1) * VD].astype(BF16)

    row = lambda n: pl.BlockSpec((tm, n), lambda i: (i, 0))
    heads = lambda n: pl.BlockSpec((NH, tm, n), lambda i: (0, i, 0))
    sd = lambda n: jax.ShapeDtypeStruct((S, n), F32)
    hd = lambda n: jax.ShapeDtypeStruct((NH, S, n), BF16)
    return pl.pallas_call(
        body, name="inproj", grid=(S // tm,),
        in_specs=[row(D), _const((8, 3 * D)), _const((D, IN_P)), _const((1, QL)), _const((1, KVL)),
                  _const((QL, NH * HP)), _const((KVL, NH * NOPE)), _const((KVL, NH * VD)),
                  row(1), _const((1, 128))],
        out_specs=[row(QL), row(KVL), row(D), row(CC), row(D), row(128), heads(HP), heads(HP), heads(VD)],
        out_shape=[sd(QL), sd(KVL), sd(D), sd(CC), sd(D), sd(128), hd(HP), hd(HP), hd(VD)],
        compiler_params=_cp(("arbitrary",)),
    )(x, mod, win, qg, kvg, wq, wk, wv, pos, invf)


def _attn_fwd(q, k, v):
    _, S, _ = q.shape
    tq = min(TQF, S)
    nq = S // tq
    half = tq // NSP
    tb = min(256, half)
    nsb = half // tb

    def body(q_ref, k_ref, v_ref, o_ref, lse_ref, vt_ref):
        i = pl.program_id(1)
        qb = q_ref[...]

        @pl.when(i == 0)
        def _():
            ones_rows = jnp.where(lax.broadcasted_iota(jnp.int32, (VDP - VD, tb), 0) == 0, 1.0, 0.0).astype(BF16)

            def fill(blk, carry):
                off = pl.multiple_of(blk * tb, tb)
                vt_ref[blk, 0:VD, :] = v_ref[pl.ds(off, tb), :].T
                vt_ref[blk, VD:VDP, :] = ones_rows
                return carry

            lax.fori_loop(0, S // tb, fill, 0)

        def scores(j, hb):
            off = pl.multiple_of(j * tq + hb * half, half)
            return _nt(k_ref[pl.ds(off, half), :], qb)

        def update(j, hb, s, carry):
            m, acc = carry
            m_new = jnp.maximum(m, jnp.max(s, axis=0, keepdims=True))
            a = jnp.exp2(m - m_new)
            pb = jnp.exp2(s - m_new).astype(BF16)
            acc = a * acc
            for sb in range(nsb):
                acc = acc + _mm(vt_ref[(NSP * j + hb) * nsb + sb], pb[sb * tb:(sb + 1) * tb, :])
            return m_new, acc

        def trip(j, carry, masked):
            s = [scores(j, hb) for hb in range(NSP)]
            if masked:
                r = lax.broadcasted_iota(jnp.int32, s[0].shape, 0)
                cidx = lax.broadcasted_iota(jnp.int32, s[0].shape, 1)
                s = [jnp.where(cidx >= r + hb * half, s[hb], -1e30) for hb in range(NSP)]
            for hb in range(NSP):
                carry = update(j, hb, s[hb], carry)
            return carry

        def finish(carry):
            m, acc = carry
            l = acc[VD:VD + 1, :]
            o_ref[...] = (acc[0:VD, :] / l).T
            lse_ref[...] = m + jnp.log2(l)

        init = (jnp.full((1, tq), -1e30, F32), jnp.zeros((VDP, tq), F32))
        carry = lax.fori_loop(0, i >> 1, lambda t, cr: trip(2 * t + 1, trip(2 * t, cr, False), False), init)

        @pl.when((i & 1) == 0)
        def _():
            finish(trip(i, carry, True))

        @pl.when((i & 1) == 1)
        def _():
            finish(trip(i, trip(i - 1, carry, False), True))

    return pl.pallas_call(
        body, name="attn_fwd", grid=(NH, nq),
        in_specs=[pl.BlockSpec((None, tq, HP), lambda h, i: (h, i, 0)),
                  pl.BlockSpec((None, S, HP), lambda h, i: (h, 0, 0)),
                  pl.BlockSpec((None, S, VD), lambda h, i: (h, 0, 0))],
        out_specs=[pl.BlockSpec((tq, VD), lambda h, i: (i, h)),
                   pl.BlockSpec((None, None, 1, tq), lambda h, i: (h, i, 0, 0))],
        out_shape=[jax.ShapeDtypeStruct((S, NH * VD), F32), jax.ShapeDtypeStruct((NH, nq, 1, tq), F32)],
        scratch_shapes=[pltpu.VMEM((S // tb, VDP, tb), BF16)],
        compiler_params=_cp(("arbitrary", "arbitrary")),
    )(q, k, v)


def _ssd_consts():
    tri = np.tril(np.ones((CH, CH), np.float32))
    e16 = np.zeros((128, SW), np.float32)
    for h in range(SH):
        e16[h, h * SP:(h + 1) * SP] = 1.0
    return jnp.asarray(tri, BF16), jnp.asarray(tri.T.copy(), BF16), jnp.asarray(e16, BF16)


def _ssd_conv(xbc_ref, halo_ref, cw_ref, cb_ref, ext, first):
    ext[0:8, :] = jnp.where(first, 0.0, halo_ref[...])
    ext[8:8 + CH, :] = xbc_ref[...]
    cw = cw_ref[...]
    xc = cb_ref[...] + cw[0:1, :] * ext[5:5 + CH, :]
    for kk in range(1, CW):
        xc = xc + cw[kk:kk + 1, :] * ext[5 + kk:5 + kk + CH, :]
    return xc


def _ssd_chunk_common(xc, dtraw_ref, dtb_ref, alog_ref, tri_ref, e16_ref):
    sact = _sigmoid(xc)
    act = xc * sact
    lane = lax.broadcasted_iota(jnp.int32, (1, 128), 1)
    arow = jnp.where(lane < SH, -jnp.exp(alog_ref[...]), 0.0)
    dtpre = dtraw_ref[...] + dtb_ref[...]
    dt = _softplus(dtpre)
    a = dt * arow
    cum = _xmm(tri_ref[...], a, 3)
    cumx = _mm_x(cum, e16_ref[...], 3)
    dtx = _mm_x(dt, e16_ref[...], 2)
    return sact, act, arow, dtpre, dt, cum, cumx, dtx


def _ssd_fwd(xbc, dtraw, zs, conv_w, conv_b, dtb, alog, dskx, gssm):
    S = xbc.shape[0]
    nc = S // CH
    tri, _, e16 = _ssd_consts()

    def body(xbc_ref, halo_ref, dtraw_ref, zs_ref, cw_ref, cb_ref, dtb_ref, alog_ref, dsk_ref, g_ref, tri_ref, e16_ref,
             xc_ref, y_ref, htp_ref, o_ref, ht, ext):
        i = pl.program_id(0)

        @pl.when(i == 0)
        def _():
            ht[...] = jnp.zeros_like(ht)

        xc = _ssd_conv(xbc_ref, halo_ref, cw_ref, cb_ref, ext, i == 0)
        xc_ref[...] = xc
        sact, act, arow, dtpre, dt, cum, cumx, dtx = _ssd_chunk_common(xc, dtraw_ref, dtb_ref, alog_ref, tri_ref, e16_ref)
        cum_t = cum.T
        xs = act[:, 0:SW]
        lastx = cumx[CH - 1:CH, :]
        xh = xs * dtx
        eexp = jnp.exp(cumx)
        dte = jnp.exp(lastx - cumx)
        cdx = jnp.exp(lastx)
        htp = ht[...]
        htp_ref[...] = htp
        xw = (xh * dte).astype(BF16)
        xb = xh.astype(BF16)
        trim = tri_ref[...].astype(F32) > 0.5
        lane = lax.broadcasted_iota(jnp.int32, (CH, 128), 1)
        parts = []
        for g in range(SG):
            gl = slice(g * GW, (g + 1) * GW)
            bg = act[:, SW + g * SN:SW + (g + 1) * SN].astype(BF16)
            cg = act[:, SW + SG * SN + g * SN:SW + SG * SN + (g + 1) * SN].astype(BF16)
            cbm = _nt(cg, bg)
            yoff = eexp[:, gl] * _mm(cg, htp[:, gl].astype(BF16))
            ht[:, gl] = htp[:, gl] * cdx[:, gl] + _tn(bg, xw[:, gl])
            for pr in range(GW // 128):
                h0 = g * (SH // SG) + 2 * pr
                lo = g * GW + pr * 128
                xp = xb[:, lo:lo + 128]
                res = []
                for hh in (h0, h0 + 1):
                    seg = cum[:, hh:hh + 1] - cum_t[hh:hh + 1, :]
                    mh = jnp.where(trim, cbm * jnp.exp(seg), 0.0).astype(BF16)
                    res.append(_mm(mh, xp))
                parts.append(jnp.where(lane < SP, res[0], res[1]) + yoff[:, pr * 128:(pr + 1) * 128])
        y = jnp.concatenate(parts, axis=1) + xs * dsk_ref[...]
        y_ref[...] = y
        z = zs_ref[...]
        hf = y * (z * _sigmoid(z))
        outs = []
        for g in range(SG):
            hg = hf[:, g * GW:(g + 1) * GW]
            rs = lax.rsqrt(jnp.mean(hg * hg, axis=-1, keepdims=True) + RMS_EPS)
            outs.append(hg * rs)
        o_ref[...] = (jnp.concatenate(outs, axis=1) * g_ref[...]).astype(BF16)

    row = lambda n: pl.BlockSpec((CH, n), lambda i: (i, 0))
    return pl.pallas_call(
        body, name="ssd_fwd", grid=(nc,),
        in_specs=[row(CC), pl.BlockSpec((8, CC), lambda i: (jnp.maximum(i * (CH // 8) - 1, 0), 0)), row(128), row(SW),
                  _const((CW, CC)), _const((1, CC)), _const((1, 128)), _const((1, 128)), _const((1, SW)), _const((1, SW)),
                  _const((CH, CH)), _const((128, SW))],
        out_specs=[row(CC), row(SW), pl.BlockSpec((None, SN, SW), lambda i: (i, 0, 0)), row(SW)],
        out_shape=[jax.ShapeDtypeStruct((S, CC), F32), jax.ShapeDtypeStruct((S, SW), F32),
                   jax.ShapeDtypeStruct((nc, SN, SW), F32), jax.ShapeDtypeStruct((S, SW), BF16)],
        scratch_shapes=[pltpu.VMEM((SN, SW), F32), pltpu.VMEM((8 + CH, CC), F32)],
        compiler_params=_cp(("arbitrary",)),
    )(xbc, xbc, dtraw, zs, conv_w, conv_b, dtb, alog, dskx, gssm, tri, e16)


def _outproj(o, za, ossm, x, tgt, wout, mod, ln_g, ln_b):
    S = x.shape[0]
    tm = min(TM, S)

    e8 = np.zeros((D, 128), np.float32)
    for h in range(NH):
        e8[h * VD:(h + 1) * VD, h] = 1.0
    e8 = jnp.asarray(e8, BF16)

    def body(o_ref, za_ref, os_ref, x_ref, t_ref, w_ref, mod_ref, g_ref, b_ref, e8_ref,
             gx_ref, do_ref, dza_ref, dos_ref, delta_ref, dw_ref, vec_ref):
        i = pl.program_id(0)

        @pl.when(i == 0)
        def _():
            dw_ref[...] = jnp.zeros_like(dw_ref)
            vec_ref[...] = jnp.zeros_like(vec_ref)

        gate = mod_ref[0:1, 2 * D:3 * D]
        ov = o_ref[...]
        z = za_ref[...]
        sz = _sigmoid(z)
        silz = z * sz
        a = (ov * silz).astype(BF16)
        osb = os_ref[...]
        mixed = _mm(a, w_ref[0:D, :]) + _mm(osb, w_ref[D:MIX, :])
        xv = x_ref[...]
        hres = ALPHA * xv + gate * mixed
        mu = jnp.mean(hres, axis=-1, keepdims=True)
        hc = hres - mu
        var = jnp.mean(hc * hc, axis=-1, keepdims=True)
        rstd = lax.rsqrt(var + LN_EPS)
        xhat = hc * rstd
        g = g_ref[...]
        yv = xhat * g + b_ref[...]
        err = yv - t_ref[...]
        dy = err * (1.0 / D)
        vec_ref[0:1, :] += jnp.sum(err * err, axis=0, keepdims=True)
        vec_ref[1:2, :] += jnp.sum(dy * xhat, axis=0, keepdims=True)
        vec_ref[2:3, :] += jnp.sum(dy, axis=0, keepdims=True)
        dxh = dy * g
        dh = rstd * (dxh - jnp.mean(dxh, axis=-1, keepdims=True) - xhat * jnp.mean(dxh * xhat, axis=-1, keepdims=True))
        gx_ref[...] = ALPHA * dh
        vec_ref[3:4, :] += jnp.sum(dh * mixed, axis=0, keepdims=True)
        dmixed = (gate * dh).astype(BF16)
        dw_ref[0:D, :] += _tn(a, dmixed)
        dw_ref[D:MIX, :] += _tn(osb, dmixed)
        da = _nt(dmixed, w_ref[0:D, :])
        dos_ref[...] = _nt(dmixed, w_ref[D:MIX, :])
        dov = da * silz
        do_ref[...] = dov.astype(BF16)
        dza_ref[...] = (da * ov * (sz * (1.0 + z * (1.0 - sz)))).astype(BF16)
        delta_ref[:, 0, :] = _mm_x(dov * ov, e8_ref[...], 2).T[0:NH, :]

    row = lambda n: pl.BlockSpec((tm, n), lambda i: (i, 0))
    return pl.pallas_call(
        body, name="outproj", grid=(S // tm,),
        in_specs=[row(D), row(D), row(D), row(D), row(D), _const((MIX, D)), _const((8, 3 * D)), _const((1, D)), _const((1, D)),
                  _const((D, 128))],
        out_specs=[row(D), row(D), row(D), row(D), pl.BlockSpec((NH, None, 1, tm), lambda i: (0, i, 0, 0)),
                   _full((MIX, D)), _full((8, D))],
        out_shape=[jax.ShapeDtypeStruct((S, D), F32), jax.ShapeDtypeStruct((S, D), BF16), jax.ShapeDtypeStruct((S, D), BF16),
                   jax.ShapeDtypeStruct((S, D), F32), jax.ShapeDtypeStruct((NH, S // tm, 1, tm), F32),
                   jax.ShapeDtypeStruct((MIX, D), F32), jax.ShapeDtypeStruct((8, D), F32)],
        compiler_params=_cp(("arbitrary",)),
    )(o, za, ossm, x, tgt, wout, mod, ln_g, ln_b, e8)


def _attn_bwd(q, k, v, do, lse, delta):
    _, S, _ = q.shape
    tk = min(TQ, S // 2)
    nk = S // tk
    tq = 2 * tk
    nq = S // tq

    def body(k_ref, v_ref, q_ref, do_ref, lse_ref, dl_ref, dk_ref, dv_ref, dq_ref, dqt_ref):
        j = pl.program_id(1)
        kb = k_ref[...]
        ktb = kb.T
        vb = v_ref[...]

        @pl.when(j == 0)
        def _():
            dqt_ref[...] = jnp.zeros_like(dqt_ref)

        dk_ref[...] = jnp.zeros_like(dk_ref)
        dv_ref[...] = jnp.zeros_like(dv_ref)

        def step(i, masked, lo=0):
            off = pl.multiple_of(i * tq + lo, tk)
            qb = q_ref[pl.ds(off, tq - lo), :]
            dob = do_ref[pl.ds(off, tq - lo), :]
            pt = jnp.exp2(_nt(kb, qb) - lse_ref[i][:, lo:tq])
            if masked:
                r = lax.broadcasted_iota(jnp.int32, pt.shape, 0)
                cidx = lax.broadcasted_iota(jnp.int32, pt.shape, 1)
                pt = jnp.where(i * tq + lo + cidx >= j * tk + r, pt, 0.0)
            dv_ref[...] += _mm(pt.astype(BF16), dob)
            dsb = (pt * (_nt(vb, dob) - dl_ref[i][:, lo:tq])).astype(BF16)
            dk_ref[...] += _mm(dsb, qb)
            dqt_ref[i, :, lo:tq] += _mm(ktb, dsb)

        first = j >> 1

        @pl.when((j & 1) == 0)
        def _():
            step(first, True)

        @pl.when((j & 1) == 1)
        def _():
            step(first, True, tk)
            dq_ref[...] = dqt_ref[first].T

        def loop_body(t, carry):
            step(first + 1 + 2 * t, False)
            step(first + 2 + 2 * t, False)
            return carry

        rest = nq - 1 - first
        lax.fori_loop(0, rest >> 1, loop_body, 0)

        @pl.when((rest & 1) == 1)
        def _():
            step(nq - 1, False)

        dk_ref[...] = dk_ref[...] * LN2

    return pl.pallas_call(
        body, name="attn_bwd", grid=(NH, nk),
        in_specs=[pl.BlockSpec((None, tk, HP), lambda h, j: (h, j, 0)),
                  pl.BlockSpec((None, tk, VD), lambda h, j: (h, j, 0)),
                  pl.BlockSpec((None, S, HP), lambda h, j: (h, 0, 0)),
                  pl.BlockSpec((S, VD), lambda h, j: (0, h)),
                  pl.BlockSpec((None, nq, 1, tq), lambda h, j: (h, 0, 0, 0)),
                  pl.BlockSpec((None, nq, 1, tq), lambda h, j: (h, 0, 0, 0))],
        out_specs=[pl.BlockSpec((None, tk, HP), lambda h, j: (h, j, 0)),
                   pl.BlockSpec((None, tk, VD), lambda h, j: (h, j, 0)),
                   pl.BlockSpec((None, tq, HP), lambda h, j: (h, j >> 1, 0))],
        out_shape=[jax.ShapeDtypeStruct((NH, S, HP), F32), jax.ShapeDtypeStruct((NH, S, VD), F32),
                   jax.ShapeDtypeStruct((NH, S, HP), F32)],
        scratch_shapes=[pltpu.VMEM((nq, HP, tq), F32)],
        compiler_params=_cp(("arbitrary", "arbitrary")),
    )(k, v, q, do, lse.reshape(NH, nq, 1, tq), delta.reshape(NH, nq, 1, tq))


def _ssd_bwd(xbc, xc, dtraw, zs, y, htp, dossm, conv_w, dtb, alog, dskx, gssm):
    S = xbc.shape[0]
    nc = S // CH
    tri, triu, e16 = _ssd_consts()

    def body(xbc_ref, xc_ref, dtraw_ref, zs_ref, y_ref, htp_ref, dos_ref,
             cw_ref, dtb_ref, alog_ref, dsk_ref, g_ref, tri_ref, triu_ref, e16_ref,
             dxbc_ref, ddt_ref, dzs_ref, dcw_ref, dcb_ref, dvec_ref, dg_ref,
             dht, dext, dskacc):
        r = pl.program_id(0)

        @pl.when(r == 0)
        def _():
            dht[...] = jnp.zeros_like(dht)
            dext[CH:CH + 8, :] = jnp.zeros((8, CC), F32)
            dskacc[...] = jnp.zeros_like(dskacc)
            dcw_ref[...] = jnp.zeros_like(dcw_ref)
            dcb_ref[...] = jnp.zeros_like(dcb_ref)
            dvec_ref[...] = jnp.zeros_like(dvec_ref)
            dg_ref[...] = jnp.zeros_like(dg_ref)

        xc = xc_ref[...]
        sact, act, arow, dtpre, dt, cum, cumx, dtx = _ssd_chunk_common(xc, dtraw_ref, dtb_ref, alog_ref, tri_ref, e16_ref)
        cum_t = cum.T
        xs = act[:, 0:SW]
        lastx = cumx[CH - 1:CH, :]
        xh = xs * dtx
        eexp = jnp.exp(cumx)
        dte = jnp.exp(lastx - cumx)
        cdx = jnp.exp(lastx)
        trim = tri_ref[...].astype(F32) > 0.5
        lane = lax.broadcasted_iota(jnp.int32, (CH, 128), 1)
        rowi = lax.broadcasted_iota(jnp.int32, (CH, 128), 0)

        yv = y_ref[...]
        z = zs_ref[...]
        sz = _sigmoid(z)
        silz = z * sz
        hf = yv * silz
        dn = dos_ref[...] * g_ref[...]
        dhf_parts, nrm_parts = [], []
        for g in range(SG):
            gl = slice(g * GW, (g + 1) * GW)
            hg = hf[:, gl]
            rs = lax.rsqrt(jnp.mean(hg * hg, axis=-1, keepdims=True) + RMS_EPS)
            ng = hg * rs
            dng = dn[:, gl]
            dhf_parts.append(rs * (dng - ng * jnp.mean(dng * ng, axis=-1, keepdims=True)))
            nrm_parts.append(ng)
        nrm = jnp.concatenate(nrm_parts, axis=1)
        dhf = jnp.concatenate(dhf_parts, axis=1)
        dg_ref[...] += jnp.sum(dos_ref[...] * nrm, axis=0, keepdims=True)
        dyv = dhf * silz
        dzs_ref[...] = (dhf * yv * (sz * (1.0 + z * (1.0 - sz)))).astype(BF16)
        dskacc[...] += jnp.sum(dyv * xs, axis=0, keepdims=True)
        dxs_skip = dyv * dsk_ref[...]

        dhtn = dht[...]
        hp = htp_ref[...]
        dlastx = jnp.sum(dhtn * hp, axis=0, keepdims=True) * cdx
        xb = xh.astype(BF16)
        xwf = xh * dte
        dcum = jnp.zeros((CH, 128), F32)
        dcum_t = jnp.zeros((128, CH), F32)
        dxh_parts, dcumx_parts, dlast_parts, db_parts, dc_parts = [], [], [], [], []
        for g in range(SG):
            gl = slice(g * GW, (g + 1) * GW)
            bg = act[:, SW + g * SN:SW + (g + 1) * SN].astype(BF16)
            cg = act[:, SW + SG * SN + g * SN:SW + SG * SN + (g + 1) * SN].astype(BF16)
            hpg = hp[:, gl].astype(BF16)
            dhn = dhtn[:, gl].astype(BF16)
            dyg = dyv[:, gl]
            dz = (dyg * eexp[:, gl]).astype(BF16)
            dcg = _nt(dz, hpg)
            dht[:, gl] = dhtn[:, gl] * cdx[:, gl] + _tn(cg, dz)
            yoff = eexp[:, gl] * _mm(cg, hpg)
            dcumx_g = dyg * yoff
            dbg = _nt(xwf[:, gl].astype(BF16), dhn)
            dxw = _mm(bg, dhn)
            ddte = dxw * xwf[:, gl]
            dcumx_parts.append(dcumx_g - ddte)
            dlast_parts.append(jnp.sum(ddte, axis=0, keepdims=True))
            dxh_g = dxw * dte[:, gl]
            cbm = _nt(cg, bg)
            dcb = jnp.zeros((CH, CH), F32)
            dxp_parts = []
            for pr in range(GW // 128):
                h0 = g * (SH // SG) + 2 * pr
                lo = g * GW + pr * 128
                xp = xb[:, lo:lo + 128]
                dyp = dyv[:, lo:lo + 128]
                dxp = jnp.zeros((CH, 128), F32)
                for idx, hh in enumerate((h0, h0 + 1)):
                    decay = jnp.where(trim, jnp.exp(cum[:, hh:hh + 1] - cum_t[hh:hh + 1, :]), 0.0)
                    mh = cbm * decay
                    keep = (lane < SP) if idx == 0 else (lane >= SP)
                    dym = jnp.where(keep, dyp, 0.0).astype(BF16)
                    dm = _nt(dym, xp)
                    dxp = dxp + _tn(mh.astype(BF16), dym)
                    gm = dm * mh
                    dcum = dcum + jnp.where(lane == hh, jnp.sum(gm, axis=1, keepdims=True), 0.0)
                    dcum_t = dcum_t - jnp.where(rowi == hh, jnp.sum(gm, axis=0, keepdims=True), 0.0)
                    dcb = dcb + dm * decay
                dxp_parts.append(dxp)
            dxh_parts.append(dxh_g + jnp.concatenate(dxp_parts, axis=1))
            dcbb = dcb.astype(BF16)
            dc_parts.append(dcg + _mm(dcbb, bg))
            db_parts.append(dbg + _tn(dcbb, cg))
        dxh = jnp.concatenate(dxh_parts, axis=1)
        dcumx = jnp.concatenate(dcumx_parts, axis=1)
        dlastx = dlastx + jnp.concatenate(dlast_parts, axis=1)
        e16 = e16_ref[...]
        dlast128 = _nt_x(jnp.broadcast_to(dlastx, (8, SW)), e16, 2)[0:1, :]
        dcum = dcum + dcum_t.T + _nt_x(dcumx, e16, 2) + jnp.where(rowi == CH - 1, dlast128, 0.0)
        da = _xmm(triu_ref[...], dcum, 2)
        ddt = da * arow + _nt_x(dxh * xs, e16, 2)
        dvec_ref[1:2, :] += jnp.sum(da * dt, axis=0, keepdims=True)
        ddtraw = jnp.where(lane < SH, ddt * _sigmoid(dtpre), 0.0)
        dvec_ref[0:1, :] += jnp.sum(ddtraw, axis=0, keepdims=True)
        ddt_ref[...] = ddtraw.astype(BF16)
        dxs = dxs_skip + dxh * dtx
        dact = jnp.concatenate([dxs] + db_parts + dc_parts, axis=1)
        dxc = dact * (sact * (1.0 + xc * (1.0 - sact)))

        dcb_ref[...] += jnp.sum(dxc, axis=0, keepdims=True)
        dext[0:CH, :] = dxc
        cw = cw_ref[...]
        xraw = xbc_ref[...]
        dxr = cw[CW - 1:CW, :] * dxc
        dcw_ref[CW - 1:CW, :] += jnp.sum(dxc * xraw, axis=0, keepdims=True)
        for kk in range(CW - 1):
            dwin = dext[CW - 1 - kk:CW - 1 - kk + CH, :]
            dxr = dxr + cw[kk:kk + 1, :] * dwin
            dcw_ref[kk:kk + 1, :] += jnp.sum(dwin * xraw, axis=0, keepdims=True)
        dxbc_ref[...] = dxr.astype(BF16)
        dext[CH:CH + 8, :] = dxc[0:8, :]

        @pl.when(r == nc - 1)
        def _():
            dvec_ref[1:2, :] = dvec_ref[1:2, :] * arow
            dvec_ref[2:3, :] = _nt_x(jnp.broadcast_to(dskacc[...], (8, SW)), e16, 3)[0:1, :]

    rev = lambda n: pl.BlockSpec((CH, n), lambda r: (nc - 1 - r, 0))
    return pl.pallas_call(
        body, name="ssd_bwd", grid=(nc,),
        in_specs=[rev(CC), rev(CC),
                  rev(128), rev(SW), rev(SW), pl.BlockSpec((None, SN, SW), lambda r: (nc - 1 - r, 0, 0)), rev(SW),
                  _const((CW, CC)), _const((1, 128)), _const((1, 128)), _const((1, SW)), _const((1, SW)),
                  _const((CH, CH)), _const((CH, CH)), _const((128, SW))],
        out_specs=[rev(CC), rev(128), rev(SW), _full((CW, CC)), _full((1, CC)), _full((8, 128)), _full((1, SW))],
        out_shape=[jax.ShapeDtypeStruct((S, CC), BF16), jax.ShapeDtypeStruct((S, 128), BF16), jax.ShapeDtypeStruct((S, SW), BF16),
                   jax.ShapeDtypeStruct((CW, CC), F32), jax.ShapeDtypeStruct((1, CC), F32),
                   jax.ShapeDtypeStruct((8, 128), F32), jax.ShapeDtypeStruct((1, SW), F32)],
        scratch_shapes=[pltpu.VMEM((SN, SW), F32), pltpu.VMEM((CH + 8, CC), F32), pltpu.VMEM((1, SW), F32)],
        compiler_params=_cp(("arbitrary",)),
    )(xbc, xc, dtraw, zs, y, htp, dossm, conv_w, dtb, alog, dskx, gssm, tri, triu, e16)


def _mla_bwd(dq, dk, dv, qlat, ckv, qg, kvg, wq, wk, wv, pos, invf):
    S = qlat.shape[0]
    tm = min(TQ, S)

    def body(dq_ref, dk_ref, dv_ref, ql_ref, ckv_ref, qg_ref, kvg_ref, wq_ref, wk_ref, wv_ref, pos_ref, invf_ref,
             dql_ref, dckv_ref, dkr_ref, dwq_ref, dwk_ref, dwv_ref, dqg_ref, dkvg_ref):
        i = pl.program_id(0)

        @pl.when(i == 0)
        def _():
            dwq_ref[...] = jnp.zeros_like(dwq_ref)
            dwk_ref[...] = jnp.zeros_like(dwk_ref)
            dwv_ref[...] = jnp.zeros_like(dwv_ref)
            dqg_ref[...] = jnp.zeros_like(dqg_ref)
            dkvg_ref[...] = jnp.zeros_like(dkvg_ref)

        ang = pos_ref[...].astype(F32) * invf_ref[...]
        cs = jnp.cos(ang)
        sn = jnp.sin(ang)

        def rms_bwd(v, g, dn, dg_ref):
            r = lax.rsqrt(jnp.mean(v * v, axis=-1, keepdims=True) + RMS_EPS)
            vh = v * r
            dg_ref[...] += jnp.sum(dn * vh, axis=0, keepdims=True)
            dvh = dn * g
            return vh, r * (dvh - vh * jnp.mean(dvh * vh, axis=-1, keepdims=True))

        pieces = []
        for h in range(NH):
            dqh = dq_ref[h]
            pieces.append(dqh[:, 0:NOPE] * SCALE)
            pieces.append(_rope_t(dqh[:, NOPE:HP], cs, sn) * SCALE)
        dqf = jnp.concatenate(pieces, axis=1).astype(BF16)
        ql = ql_ref[...]
        g = qg_ref[...]
        dqn = _nt(dqf, wq_ref[...])
        qh, dql = rms_bwd(ql, g, dqn, dqg_ref)
        dwq_ref[...] += _tn((qh * g).astype(BF16), dqf)
        dql_ref[...] = dql.astype(BF16)

        dkn_p = jnp.concatenate([dk_ref[h, :, 0:NOPE] for h in range(NH)], axis=1).astype(BF16)
        dvf = jnp.concatenate([dv_ref[h] for h in range(NH)], axis=1).astype(BF16)
        dkr = dk_ref[0, :, NOPE:HP]
        for h in range(1, NH):
            dkr = dkr + dk_ref[h, :, NOPE:HP]
        lane = lax.broadcasted_iota(jnp.int32, dkr.shape, 1)
        dkr_ref[...] = jnp.where(lane < ROPE, _rope_t(dkr, cs, sn), 0.0).astype(BF16)
        cv = ckv_ref[...]
        gk = kvg_ref[...]
        dkn = _nt(dkn_p, wk_ref[...]) + _nt(dvf, wv_ref[...])
        kh, dckv = rms_bwd(cv, gk, dkn, dkvg_ref)
        knb = (kh * gk).astype(BF16)
        dwk_ref[...] += _tn(knb, dkn_p)
        dwv_ref[...] += _tn(knb, dvf)
        dckv_ref[...] = dckv.astype(BF16)

    row = lambda n: pl.BlockSpec((tm, n), lambda i: (i, 0))
    heads = lambda n: pl.BlockSpec((NH, tm, n), lambda i: (0, i, 0))
    return pl.pallas_call(
        body, name="mla_bwd", grid=(S // tm,),
        in_specs=[heads(HP), heads(HP), heads(VD), row(QL), row(KVL), _const((1, QL)), _const((1, KVL)),
                  _const((QL, NH * HP)), _const((KVL, NH * NOPE)), _const((KVL, NH * VD)), row(1), _const((1, 128))],
        out_specs=[row(QL), row(KVL), row(128), _full((QL, NH * HP)), _full((KVL, NH * NOPE)), _full((KVL, NH * VD)),
                   _full((1, QL)), _full((1, KVL))],
        out_shape=[jax.ShapeDtypeStruct((S, QL), BF16), jax.ShapeDtypeStruct((S, KVL), BF16), jax.ShapeDtypeStruct((S, 128), BF16),
                   jax.ShapeDtypeStruct((QL, NH * HP), F32), jax.ShapeDtypeStruct((KVL, NH * NOPE), F32),
                   jax.ShapeDtypeStruct((KVL, NH * VD), F32), jax.ShapeDtypeStruct((1, QL), F32), jax.ShapeDtypeStruct((1, KVL), F32)],
        compiler_params=_cp(("arbitrary",)),
    )(dq, dk, dv, qlat, ckv, qg, kvg, wq, wk, wv, pos, invf)


def _inproj_bwd(x, gx1, mod, win, dql, dckv, dza, dxbc, dzs, dkr, ddt):
    S = x.shape[0]
    tm = min(TM, S)

    def body(x_ref, gx1_ref, mod_ref, win_ref, dql_ref, dckv_ref, dza_ref, dxbc_ref, dzs_ref, dkr_ref, ddt_ref,
             gx_ref, dw_ref, vec_ref):
        i = pl.program_id(0)

        @pl.when(i == 0)
        def _():
            dw_ref[...] = jnp.zeros_like(dw_ref)
            vec_ref[...] = jnp.zeros_like(vec_ref)

        shift = mod_ref[0:1, 0:D]
        scale = mod_ref[0:1, D:2 * D]
        xv = x_ref[...]
        ut = (xv * (1.0 + scale) + shift).T.astype(BF16)
        pieces = (dql_ref, dckv_ref, dza_ref, dxbc_ref, dzs_ref, dkr_ref, ddt_ref)
        du = jnp.zeros((tm, D), F32)
        lo = 0
        for p_ref in pieces:
            n = p_ref.shape[1]
            dp = p_ref[...]
            du = du + _nt(dp, win_ref[:, lo:lo + n])
            dw_ref[:, lo:lo + n] += _mm(ut, dp)
            lo += n
        vec_ref[0:1, :] += jnp.sum(du, axis=0, keepdims=True)
        vec_ref[1:2, :] += jnp.sum(du * xv, axis=0, keepdims=True)
        gx_ref[...] = gx1_ref[...] + du * (1.0 + scale)

    row = lambda n: pl.BlockSpec((tm, n), lambda i: (i, 0))
    return pl.pallas_call(
        body, name="inproj_bwd", grid=(S // tm,),
        in_specs=[row(D), row(D), _const((8, 3 * D)), _const((D, IN_P)), row(QL), row(KVL), row(D), row(CC), row(D),
                  row(128), row(128)],
        out_specs=[row(D), pl.BlockSpec((D, IN_P), lambda i: (0, 0), pipeline_mode=pl.Buffered(1)), _full((8, D))],
        out_shape=[jax.ShapeDtypeStruct((S, D), F32), jax.ShapeDtypeStruct((D, IN_P), F32), jax.ShapeDtypeStruct((8, D), F32)],
        compiler_params=_cp(("arbitrary",)),
    )(x, gx1, mod, win, dql, dckv, dza, dxbc, dzs, dkr, ddt)


def _ada_bwd(callt, dmods):
    w = dmods.shape[1]

    def body(c_ref, d_ref, o_ref):
        acc = c_ref[:, 0:1] * d_ref[0:1, :]
        for s in range(1, 8):
            acc = acc + c_ref[:, s:s + 1] * d_ref[s:s + 1, :]
        o_ref[0] = acc

    return pl.pallas_call(body, name="ada_bwd", out_shape=jax.ShapeDtypeStruct((1, D, w), F32),
                          compiler_params=_cp())(callt, dmods)


def _adamw(name, parts, w, m, v):
    rows, ncol = w.shape
    tr = min(rows, 128)
    nparts = parts.shape[0]

    def body(p_ref, w_ref, m_ref, v_ref, g_ref, d_ref, nm_ref, nv_ref):
        g = p_ref[0].astype(F32)
        for s in range(1, nparts):
            g = g + p_ref[s].astype(F32)
        g_ref[...] = g
        nm = B1 * m_ref[...] + (1.0 - B1) * g
        nv = B2 * v_ref[...] + (1.0 - B2) * (g * g)
        nm_ref[...] = nm
        nv_ref[...] = nv
        m_hat = nm / (1.0 - B1 ** STEP)
        v_hat = nv / (1.0 - B2 ** STEP)
        d_ref[...] = -LR * (m_hat / (jnp.sqrt(v_hat) + EPS) + WD * w_ref[...])

    row = pl.BlockSpec((tr, ncol), lambda i: (i, 0))
    sd = jax.ShapeDtypeStruct((rows, ncol), F32)
    return pl.pallas_call(
        body, name="adamw_" + name, grid=(rows // tr,),
        in_specs=[pl.BlockSpec((nparts, tr, ncol), lambda i: (0, i, 0)), row, row, row],
        out_specs=[row, row, row, row], out_shape=[sd, sd, sd, sd],
        compiler_params=_cp(("arbitrary",)),
    )(parts, w, m, v)


_SMALL = (("b_ada", 3 * D), ("conv_w", CW * CC // 4), ("conv_b", CC), ("ssm_norm_g", SW), ("ln_g", D), ("ln_b", D),
          ("q_norm_g", QL), ("kv_norm_g", KVL), ("dt_bias", SH), ("a_log", SH), ("d_skip", SH), ("loss", 128))


def _pack_small(d, lead):
    flat = [d[name].reshape(d[name].shape[:lead] + (-1,)) for name, _ in _SMALL]
    used = sum(f.shape[lead] for f in flat)
    pad = jnp.zeros(flat[0].shape[:lead] + (R_SMALL * 1024 - used,), F32)
    return jnp.concatenate(flat + [pad], axis=lead).reshape(flat[0].shape[:lead] + (R_SMALL, 1024))


def _unpack_small(p):
    flat = p.reshape(-1)
    out, r = {}, 0
    for name, n in _SMALL:
        out[name] = flat[r:r + n]
        r += n
    return out


def _in_to_padded(w):
    z = lambda n: jnp.zeros((w.shape[0], n), w.dtype)
    return jnp.concatenate([w[:, 0:384], w[:, 384:640], w[:, 704:1728], w[:, 1728:3264], w[:, 3280:4304],
                            w[:, 640:704], z(64), w[:, 3264:3280], z(112)], axis=1)


def _in_from_padded(g):
    return jnp.concatenate([g[:, 0:384], g[:, 384:640], g[:, P_KR[0]:P_KR[0] + 64], g[:, 640:1664], g[:, 1664:3200],
                            g[:, P_DT[0]:P_DT[0] + 16], g[:, 3200:4224]], axis=1)


def kernel(x, c, positions, w_ada, b_ada, w_in, q_norm_g, w_qb, kv_norm_g, w_kvb, conv_w, conv_b, dt_bias, a_log, d_skip, ssm_norm_g, w_out, ln_g, ln_b, loss_target, m_w_ada, m_b_ada, m_w_in, m_q_norm_g, m_w_qb, m_kv_norm_g, m_w_kvb, m_conv_w, m_conv_b, m_dt_bias, m_a_log, m_d_skip, m_ssm_norm_g, m_w_out, m_ln_g, m_ln_b, v_w_ada, v_b_ada, v_w_in, v_q_norm_g, v_w_qb, v_kv_norm_g, v_w_kvb, v_conv_w, v_conv_b, v_dt_bias, v_a_log, v_d_skip, v_ssm_norm_g, v_w_out, v_ln_g, v_ln_b):
    S = x.shape[1]
    xv = x[0]
    tgt = loss_target[0]

    cw16 = jnp.concatenate([conv_w[0], jnp.zeros((16 - CW, CC // 4), F32)], axis=0)
    f_in, f_qb, f_kvb, f_out, f_cw = _gather_weights(
        [w_in[0].astype(BF16), w_qb[0].astype(BF16), w_kvb[0].astype(BF16), w_out[0].astype(BF16), cw16])
    cat1 = lambda f: f.transpose(1, 0, 2).reshape(f.shape[1], 4 * f.shape[2])
    win = _in_to_padded(cat1(f_in))
    wqb = cat1(f_qb).reshape(QL, NH, QKD)
    wq = jnp.concatenate([wqb, jnp.zeros((QL, NH, HP - QKD), BF16)], axis=2).reshape(QL, NH * HP)
    wkvb = cat1(f_kvb).reshape(KVL, NH, NOPE + VD)
    wk = wkvb[:, :, 0:NOPE].reshape(KVL, NH * NOPE)
    wv = wkvb[:, :, NOPE:].reshape(KVL, NH * VD)
    wout = f_out.reshape(MIX, D)
    cwf = cat1(f_cw[:, 0:CW, :])

    half = ROPE // 2
    invf = 1.0 / (ROPE_THETA ** (jnp.arange(half, dtype=F32) / half))
    invf = jnp.concatenate([invf, invf, jnp.zeros((128 - ROPE,), F32)]).reshape(1, 128)
    pos = positions.reshape(S, 1)
    pad128 = lambda a: jnp.concatenate([a.reshape(1, SH), jnp.zeros((1, 128 - SH), F32)], axis=1)
    dtb, alog = pad128(dt_bias), pad128(a_log)
    dskx = jnp.repeat(d_skip.reshape(SH), SP).reshape(1, SW)

    my_c = lax.axis_index("c")
    (call,) = _exchange("gather_c", [jnp.broadcast_to(c.reshape(1, 1, D), (4, 1, D))])
    call = call.reshape(8, D)
    mods = _ada(call, w_ada[0])
    (mrows,) = _exchange("scatter_mod", [mods.reshape(4, 2, 3 * D // 4)])
    mine = lax.dynamic_index_in_dim(mrows.reshape(4, 2, 2, 3 * D // 4)[:, 0], my_c, axis=1, keepdims=False)
    mod = jnp.broadcast_to(mine.reshape(1, 3 * D) + b_ada, (8, 3 * D))
    qlat, ckv, za, xbc, zs, dtraw, q, k, v = _inproj(xv, mod, win, q_norm_g, kv_norm_g, wq, wk, wv, pos, invf)
    o, lse = _attn_fwd(q, k, v)
    xc, y, htp, ossm = _ssd_fwd(xbc, dtraw, zs, cwf, conv_b, dtb, alog, dskx, ssm_norm_g)
    gx1, do, dza, dossm, delta, dwout, vec_o = _outproj(o, za, ossm, xv, tgt, wout, mod, ln_g, ln_b)
    loss_part = jnp.zeros((128,), F32).at[0].set(0.5 / D * jnp.sum(vec_o[0]))

    dk, dv, dq = _attn_bwd(q, k, v, do, lse, delta)
    dxbc, ddt, dzs, dcw, dcb, dvec, dgssm = _ssd_bwd(xbc, xc, dtraw, zs, y, htp, dossm, cwf, dtb, alog, dskx, ssm_norm_g)
    dql, dckv, dkr, dwq, dwk, dwv, dqg, dkvg = _mla_bwd(dq, dk, dv, qlat, ckv, q_norm_g, kv_norm_g, wq, wk, wv, pos, invf)
    gx, dwin, vec_i = _inproj_bwd(xv, gx1, mod, win, dql, dckv, dza, dxbc, dzs, dkr, ddt)
    dmod = jnp.concatenate([vec_i[0:1], vec_i[1:2], vec_o[3:4]], axis=1)

    cols = lambda g: g.reshape(g.shape[0], 4, g.shape[1] // 4).transpose(1, 0, 2)
    g_in = cols(_in_from_padded(dwin)).astype(BF16)
    g_qb = cols(dwq.reshape(QL, NH, HP)[:, :, 0:QKD].reshape(QL, NH * QKD)).astype(BF16)
    g_kvb = cols(jnp.concatenate([dwk.reshape(KVL, NH, NOPE), dwv.reshape(KVL, NH, VD)], axis=2)
                 .reshape(KVL, NH * (NOPE + VD))).astype(BF16)
    g_out = dwout.reshape(4, MIX // 4, D).astype(BF16)
    small = {"b_ada": dmod, "conv_b": dcb, "ssm_norm_g": dgssm, "ln_g": vec_o[1:2], "ln_b": vec_o[2:3],
             "q_norm_g": dqg, "kv_norm_g": dkvg, "dt_bias": dvec[0:1, 0:SH], "a_log": dvec[1:2, 0:SH], "d_skip": dvec[2:3, 0:SH],
             "loss": loss_part}
    small = {n: jnp.broadcast_to(a.reshape(1, -1), (4, a.size)) for n, a in small.items()}
    small["conv_w"] = cols(dcw).reshape(4, CW * CC // 4)
    gsmall = _pack_small(small, 1)

    r_in, r_qb, r_kvb, r_out, rs, dmods = _exchange(
        "exchange_grads", [g_in, g_qb, g_kvb, g_out, gsmall, jnp.broadcast_to(dmod.reshape(1, 1, 3 * D), (4, 1, 3 * D))])
    chip = 2 * lax.axis_index("x") + lax.axis_index("y")
    dmods = lax.dynamic_slice_in_dim(dmods.reshape(8, 3 * D), chip * (3 * D // 4), 3 * D // 4, axis=1)
    g_ada = _ada_bwd(call.T, dmods)
    res = dict(w_ada=_adamw("w_ada", g_ada, w_ada[0], m_w_ada[0], v_w_ada[0]),
               w_in=_adamw("w_in", r_in, w_in[0], m_w_in[0], v_w_in[0]),
               w_qb=_adamw("w_qb", r_qb, w_qb[0], m_w_qb[0], v_w_qb[0]),
               w_kvb=_adamw("w_kvb", r_kvb, w_kvb[0], m_w_kvb[0], v_w_kvb[0]),
               w_out=_adamw("w_out", r_out, w_out[0], m_w_out[0], v_w_out[0]))
    wsm = _pack_small(dict(b_ada=b_ada, conv_w=conv_w, conv_b=conv_b, ssm_norm_g=ssm_norm_g, ln_g=ln_g, ln_b=ln_b,
                           q_norm_g=q_norm_g, kv_norm_g=kv_norm_g, dt_bias=dt_bias, a_log=a_log, d_skip=d_skip, loss=jnp.zeros((128,), F32)), 0)
    msm = _pack_small(dict(b_ada=m_b_ada, conv_w=m_conv_w, conv_b=m_conv_b, ssm_norm_g=m_ssm_norm_g, ln_g=m_ln_g, ln_b=m_ln_b,
                           q_norm_g=m_q_norm_g, kv_norm_g=m_kv_norm_g, dt_bias=m_dt_bias, a_log=m_a_log, d_skip=m_d_skip, loss=jnp.zeros((128,), F32)), 0)
    vsm = _pack_small(dict(b_ada=v_b_ada, conv_w=v_conv_w, conv_b=v_conv_b, ssm_norm_g=v_ssm_norm_g, ln_g=v_ln_g, ln_b=v_ln_b,
                           q_norm_g=v_q_norm_g, kv_norm_g=v_kv_norm_g, dt_bias=v_dt_bias, a_log=v_a_log, d_skip=v_d_skip, loss=jnp.zeros((128,), F32)), 0)
    sm = _adamw("small", rs, wsm, msm, vsm)

    order = ["w_ada", "b_ada", "w_in", "q_norm_g", "w_qb", "kv_norm_g", "w_kvb", "conv_w", "conv_b", "dt_bias", "a_log",
             "d_skip", "ssm_norm_g", "w_out", "ln_g", "ln_b"]
    shapes = dict(w_ada=w_ada.shape, b_ada=b_ada.shape, w_in=w_in.shape, q_norm_g=q_norm_g.shape, w_qb=w_qb.shape,
                  kv_norm_g=kv_norm_g.shape, w_kvb=w_kvb.shape, conv_w=conv_w.shape, conv_b=conv_b.shape, dt_bias=dt_bias.shape,
                  a_log=a_log.shape, d_skip=d_skip.shape, ssm_norm_g=ssm_norm_g.shape, w_out=w_out.shape, ln_g=ln_g.shape,
                  ln_b=ln_b.shape)
    outs = []
    for kind in range(4):
        d = _unpack_small(sm[kind])
        d.update({n: r[kind] for n, r in res.items()})
        outs.extend(d[n].reshape(shapes[n]) for n in order)
    loss = _unpack_small(sm[0])["loss"][0]
    return (loss, gx.reshape(x.shape), *outs)
```

```python
import functools
import math

import numpy as np
import jax
import jax.numpy as jnp
from jax import lax
from jax.experimental import pallas as pl
from jax.experimental.pallas import tpu as pltpu

F32 = jnp.float32
BF16 = jnp.bfloat16
MESH_ID = pl.DeviceIdType.MESH

D = 1024
NH = 8
NOPE = 128
ROPE = 64
VD = 128
VDP = 144
QKD = NOPE + ROPE
HP = 256
QL = 384
KVL = 256
ROPE_THETA = 10000.0
SH = 16
SP = 64
SG = 2
SN = 128
CW = 4
CH = 128
SW = SH * SP
CC = SW + 2 * SG * SN
GW = SW // SG
MIX = 2 * D
IN_W = 4304
ALPHA = 2.0 ** 0.25
RMS_EPS = 1e-6
LN_EPS = 1e-5
SCALE = QKD ** -0.5
LN2 = math.log(2.0)
QSCALE = SCALE / LN2
LR, B1, B2, EPS, WD, STEP = 0.001, 0.9, 0.999, 1e-08, 0.01, 10

P_Q = (0, 384)
P_KV = (384, 640)
P_ZA = (640, 1664)
P_XBC = (1664, 3200)
P_ZS = (3200, 4224)
P_KR = (4224, 4352)
P_DT = (4352, 4480)
IN_P = 4480

R_SMALL = 16

TM = 256
TQ = 512
NSP = 2
TQF = 1024
VMEM_LIMIT = 56 * 1024 * 1024


def _cp(sem=None):
    return pltpu.CompilerParams(dimension_semantics=sem, vmem_limit_bytes=VMEM_LIMIT)


def _mm(a, b):
    return jnp.dot(a, b, preferred_element_type=F32)


def _nt(a, b):
    return lax.dot_general(a, b, (((1,), (1,)), ((), ())), preferred_element_type=F32)


def _tn(a, b):
    return lax.dot_general(a, b, (((0,), (0,)), ((), ())), preferred_element_type=F32)


def _split(a, terms):
    parts = []
    for t in range(terms):
        p = a.astype(BF16)
        parts.append(p)
        if t + 1 < terms:
            a = a - p.astype(F32)
    return parts


def _mm_x(a, ones, terms):
    parts = _split(a, terms)
    out = _mm(parts[0], ones)
    for p in parts[1:]:
        out = out + _mm(p, ones)
    return out


def _xmm(ones, a, terms):
    parts = _split(a, terms)
    out = _mm(ones, parts[0])
    for p in parts[1:]:
        out = out + _mm(ones, p)
    return out


def _nt_x(a, ones, terms):
    parts = _split(a, terms)
    out = _nt(parts[0], ones)
    for p in parts[1:]:
        out = out + _nt(p, ones)
    return out


def _sigmoid(z):
    return 1.0 / (1.0 + jnp.exp(-z))


def _softplus(z):
    return jnp.maximum(z, 0.0) + jnp.log1p(jnp.exp(-jnp.abs(z)))


def _rope(t, cs, sn):
    lane = lax.broadcasted_iota(jnp.int32, t.shape, 1)
    rot = jnp.where(lane < ROPE // 2, -pltpu.roll(t, 128 - ROPE // 2, 1), pltpu.roll(t, ROPE // 2, 1))
    return t * cs + rot * sn


def _rope_t(t, cs, sn):
    lane = lax.broadcasted_iota(jnp.int32, t.shape, 1)
    y = t * sn
    rot = jnp.where(lane < ROPE // 2, -pltpu.roll(y, 128 - ROPE // 2, 1), pltpu.roll(y, ROPE // 2, 1))
    return t * cs - rot


def _full(shape):
    n = len(shape)
    return pl.BlockSpec(shape, lambda *_: (0,) * n)


def _const(shape):
    n = len(shape)
    return pl.BlockSpec(shape, lambda *_: (0,) * n, pipeline_mode=pl.Buffered(1))


def _gather_weights(shards):
    n = len(shards)
    halves = [a.shape[0] // 2 for a in shards]

    def body(*refs):
        srcs, dsts = refs[:n], refs[n:2 * n]
        send_sems, recv_sems, local_sems = refs[2 * n:]
        x, y, c = lax.axis_index("x"), lax.axis_index("y"), lax.axis_index("c")
        me = 2 * x + y
        sibling = (x, y, 1 - c)
        chips = [(1 - x, y), (x, 1 - y), (1 - x, 1 - y)]

        def rows(a, pc):
            return pl.ds(pl.multiple_of(pc * halves[a], halves[a]), halves[a])

        def copy(a, k, src, slot, pc, to):
            return pltpu.make_async_remote_copy(
                src_ref=src, dst_ref=dsts[a].at[slot, rows(a, pc)], send_sem=send_sems.at[a, k],
                recv_sem=recv_sems.at[a, k], device_id=to, device_id_type=MESH_ID)

        local = [pltpu.make_async_copy(srcs[a], dsts[a].at[me], local_sems.at[a]) for a in range(n)]
        for cp in local:
            cp.start()
        sends = [copy(a, j, srcs[a].at[rows(a, c)], me, c, (px, py, c)) for a in range(n) for j, (px, py) in enumerate(chips)]
        for cp in sends:
            cp.start()
        passed = []
        for a in range(n):
            for j, (px, py) in enumerate(chips):
                k = 2 * px + py
                copy(a, j, srcs[a].at[rows(a, c)], k, c, (x, y, c)).wait_recv()
                fwd = copy(a, 3 + j, dsts[a].at[k, rows(a, c)], k, c, sibling)
                fwd.start()
                passed.append(fwd)
        for a in range(n):
            for j, (px, py) in enumerate(chips):
                copy(a, 3 + j, srcs[a].at[rows(a, c)], 2 * px + py, 1 - c, (x, y, c)).wait_recv()
        for cp in sends + passed:
            cp.wait_send()
        for cp in local:
            cp.wait()

    hbm = pl.BlockSpec(memory_space=pltpu.HBM)
    return pl.pallas_call(
        body, name="gather_weights",
        out_shape=tuple(jax.ShapeDtypeStruct((4,) + a.shape, a.dtype) for a in shards),
        in_specs=[hbm] * n, out_specs=tuple([hbm] * n),
        scratch_shapes=[pltpu.SemaphoreType.DMA((n, 6)), pltpu.SemaphoreType.DMA((n, 6)), pltpu.SemaphoreType.DMA((n,))],
    )(*shards)


def _exchange(name, slabs):
    n = len(slabs)

    def body(*refs):
        srcs, dsts = refs[:n], refs[n:2 * n]
        send_sems, recv_sems, local_sems = refs[2 * n:]
        x, y, c = lax.axis_index("x"), lax.axis_index("y"), lax.axis_index("c")
        chip = 2 * x + y
        sibling = (x, y, 1 - c)
        chips = [(1 - x, y), (x, 1 - y), (1 - x, 1 - y)]

        def slot(px, py, pc):
            return 4 * px + 2 * py + pc

        def copy(a, k, src, s, to):
            return pltpu.make_async_remote_copy(
                src_ref=src, dst_ref=dsts[a].at[s], send_sem=send_sems.at[a, k], recv_sem=recv_sems.at[a, k],
                device_id=to, device_id_type=MESH_ID)

        mine = slot(x, y, c)
        local = [pltpu.make_async_copy(srcs[a].at[chip], dsts[a].at[mine], local_sems.at[a]) for a in range(n)]
        for cp in local:
            cp.start()
        first = []
        for a in range(n):
            first.append(copy(a, 0, srcs[a].at[chip], mine, sibling))
            for j, (px, py) in enumerate(chips):
                first.append(copy(a, 1 + j, srcs[a].at[2 * px + py], mine, (px, py, c)))
        for cp in first:
            cp.start()
        passed = []
        for a in range(n):
            for j, (px, py) in enumerate(chips):
                s = slot(px, py, c)
                copy(a, 1 + j, srcs[a].at[chip], s, (x, y, c)).wait_recv()
                fwd = copy(a, 4 + j, dsts[a].at[s], s, sibling)
                fwd.start()
                passed.append(fwd)
        for a in range(n):
            copy(a, 0, srcs[a].at[chip], slot(x, y, 1 - c), (x, y, c)).wait_recv()
            for j, (px, py) in enumerate(chips):
                copy(a, 4 + j, srcs[a].at[chip], slot(px, py, 1 - c), (x, y, c)).wait_recv()
        for cp in first + passed:
            cp.wait_send()
        for cp in local:
            cp.wait()

    hbm = pl.BlockSpec(memory_space=pltpu.HBM)
    return pl.pallas_call(
        body, name=name,
        out_shape=tuple(jax.ShapeDtypeStruct((8,) + a.shape[1:], a.dtype) for a in slabs),
        in_specs=[hbm] * n, out_specs=tuple([hbm] * n),
        scratch_shapes=[pltpu.SemaphoreType.DMA((n, 7)), pltpu.SemaphoreType.DMA((n, 7)), pltpu.SemaphoreType.DMA((n,))],
    )(*slabs)


def _ada(call, w_shard):
    def body(c_ref, w_ref, o_ref):
        o_ref[...] = _mm(c_ref[...].astype(BF16), w_ref[...].astype(BF16))

    return pl.pallas_call(body, name="ada", out_shape=jax.ShapeDtypeStruct((8, w_shard.shape[1]), F32),
                          compiler_params=_cp())(call, w_shard)


def _inproj(x, mod, win, qg, kvg, wq, wk, wv, pos, invf):
    S = x.shape[0]
    tm = min(TM, S)

    def body(x_ref, mod_ref, win_ref, qg_ref, kvg_ref, wq_ref, wk_ref, wv_ref, pos_ref, invf_ref,
             qlat_ref, ckv_ref, za_ref, xbc_ref, zs_ref, dt_ref, q_ref, k_ref, v_ref):
        shift = mod_ref[0:1, 0:D]
        scale = mod_ref[0:1, D:2 * D]
        u = (x_ref[...] * (1.0 + scale) + shift).astype(BF16)

        def proj(p):
            return _mm(u, win_ref[:, p[0]:p[1]])

        ql = proj(P_Q)
        ckv = proj(P_KV)
        qlat_ref[...] = ql
        ckv_ref[...] = ckv
        za_ref[...] = proj(P_ZA)
        xbc_ref[...] = proj(P_XBC)
        zs_ref[...] = proj(P_ZS)
        dt_ref[...] = proj(P_DT)
        kr = proj(P_KR)

        ang = pos_ref[...].astype(F32) * invf_ref[...]
        cs = jnp.cos(ang)
        sn = jnp.sin(ang)

        rq = lax.rsqrt(jnp.mean(ql * ql, axis=-1, keepdims=True) + RMS_EPS)
        qn = (ql * rq * qg_ref[...]).astype(BF16)
        for h in range(NH):
            qh = _mm(qn, wq_ref[:, h * HP:(h + 1) * HP])
            q_ref[h, :, 0:NOPE] = (qh[:, 0:NOPE] * QSCALE).astype(BF16)
            q_ref[h, :, NOPE:HP] = (_rope(qh[:, NOPE:HP], cs, sn) * QSCALE).astype(BF16)

        rk = lax.rsqrt(jnp.mean(ckv * ckv, axis=-1, keepdims=True) + RMS_EPS)
        kn = (ckv * rk * kvg_ref[...]).astype(BF16)
        knope = _mm(kn, wk_ref[...])
        vall = _mm(kn, wv_ref[...])
        krf = _rope(kr, cs, sn)
        krr = krf.astype(BF16)
        for h in range(NH):
            k_ref[h, :, 0:NOPE] = knope[:, h * NOPE:(h + 1) * NOPE].astype(BF16)
            k_ref[h, :, NOPE:HP] = krr
            v_ref[h] = vall[:, h * VD:(h + 1) * VD].astype(BF16)

    row = lambda n: pl.BlockSpec((tm, n), lambda i: (i, 0))
    heads = lambda n: pl.BlockSpec((NH, tm, n), lambda i: (0, i, 0))
    sd = lambda n: jax.ShapeDtypeStruct((S, n), F32)
    hd = lambda n: jax.ShapeDtypeStruct((NH, S, n), BF16)
    return pl.pallas_call(
        body, name="inproj", grid=(S // tm,),
        in_specs=[row(D), _const((8, 3 * D)), _const((D, IN_P)), _const((1, QL)), _const((1, KVL)),
                  _const((QL, NH * HP)), _const((KVL, NH * NOPE)), _const((KVL, NH * VD)),
                  row(1), _const((1, 128))],
        out_specs=[row(QL), row(KVL), row(D), row(CC), row(D), row(128), heads(HP), heads(HP), heads(VD)],
        out_shape=[sd(QL), sd(KVL), sd(D), sd(CC), sd(D), sd(128), hd(HP), hd(HP), hd(VD)],
        compiler_params=_cp(("arbitrary",)),
    )(x, mod, win, qg, kvg, wq, wk, wv, pos, invf)


def _attn_fwd(q, k, v):
    _, S, _ = q.shape
    tq = min(TQF, S)
    nq = S // tq
    half = tq // NSP
    tb = min(256, half)
    nsb = half // tb

    def body(q_ref, k_ref, v_ref, o_ref, lse_ref, vt_ref):
        i = pl.program_id(1)
        qb = q_ref[...]

        @pl.when(i == 0)
        def _():
            ones_rows = jnp.where(lax.broadcasted_iota(jnp.int32, (VDP - VD, tb), 0) == 0, 1.0, 0.0).astype(BF16)

            def fill(blk, carry):
                off = pl.multiple_of(blk * tb, tb)
                vt_ref[blk, 0:VD, :] = v_ref[pl.ds(off, tb), :].T
                vt_ref[blk, VD:VDP, :] = ones_rows
                return carry

            lax.fori_loop(0, S // tb, fill, 0)

        def scores(j, hb):
            off = pl.multiple_of(j * tq + hb * half, half)
            return _nt(k_ref[pl.ds(off, half), :], qb)

        def update(j, hb, s, carry):
            m, acc = carry
            m_new = jnp.maximum(m, jnp.max(s, axis=0, keepdims=True))
            a = jnp.exp2(m - m_new)
            pb = jnp.exp2(s - m_new).astype(BF16)
            acc = a * acc
            for sb in range(nsb):
                acc = acc + _mm(vt_ref[(NSP * j + hb) * nsb + sb], pb[sb * tb:(sb + 1) * tb, :])
            return m_new, acc

        def trip(j, carry, masked):
            s = [scores(j, hb) for hb in range(NSP)]
            if masked:
                r = lax.broadcasted_iota(jnp.int32, s[0].shape, 0)
                cidx = lax.broadcasted_iota(jnp.int32, s[0].shape, 1)
                s = [jnp.where(cidx >= r + hb * half, s[hb], -1e30) for hb in range(NSP)]
            for hb in range(NSP):
                carry = update(j, hb, s[hb], carry)
            return carry

        def finish(carry):
            m, acc = carry
            l = acc[VD:VD + 1, :]
            o_ref[...] = (acc[0:VD, :] / l).T
            lse_ref[...] = m + jnp.log2(l)

        def quad(t, cr):
            for u in range(4):
                cr = trip(4 * t + u, cr, False)
            return cr

        init = (jnp.full((1, tq), -1e30, F32), jnp.zeros((VDP, tq), F32))
        carry = lax.fori_loop(0, i >> 2, quad, init)
        carry = lax.fori_loop(i - (i & 3), i, lambda j, cr: trip(j, cr, False), carry)
        finish(trip(i, carry, True))

    return pl.pallas_call(
        body, name="attn_fwd", grid=(NH, nq),
        in_specs=[pl.BlockSpec((None, tq, HP), lambda h, i: (h, i, 0)),
                  pl.BlockSpec((None, S, HP), lambda h, i: (h, 0, 0)),
                  pl.BlockSpec((None, S, VD), lambda h, i: (h, 0, 0))],
        out_specs=[pl.BlockSpec((tq, VD), lambda h, i: (i, h)),
                   pl.BlockSpec((None, None, 1, tq), lambda h, i: (h, i, 0, 0))],
        out_shape=[jax.ShapeDtypeStruct((S, NH * VD), F32), jax.ShapeDtypeStruct((NH, nq, 1, tq), F32)],
        scratch_shapes=[pltpu.VMEM((S // tb, VDP, tb), BF16)],
        compiler_params=_cp(("arbitrary", "arbitrary")),
    )(q, k, v)


def _ssd_consts():
    tri = np.tril(np.ones((CH, CH), np.float32))
    e16 = np.zeros((128, SW), np.float32)
    for h in range(SH):
        e16[h, h * SP:(h + 1) * SP] = 1.0
    return jnp.asarray(tri, BF16), jnp.asarray(tri.T.copy(), BF16), jnp.asarray(e16, BF16)


def _ssd_conv(xbc_ref, halo_ref, cw_ref, cb_ref, ext, first):
    ext[0:8, :] = jnp.where(first, 0.0, halo_ref[...])
    ext[8:8 + CH, :] = xbc_ref[...]
    cw = cw_ref[...]
    xc = cb_ref[...] + cw[0:1, :] * ext[5:5 + CH, :]
    for kk in range(1, CW):
        xc = xc + cw[kk:kk + 1, :] * ext[5 + kk:5 + kk + CH, :]
    return xc


def _ssd_chunk_common(xc, dtraw_ref, dtb_ref, alog_ref, tri_ref, e16_ref):
    sact = _sigmoid(xc)
    act = xc * sact
    lane = lax.broadcasted_iota(jnp.int32, (1, 128), 1)
    arow = jnp.where(lane < SH, -jnp.exp(alog_ref[...]), 0.0)
    dtpre = dtraw_ref[...] + dtb_ref[...]
    dt = _softplus(dtpre)
    a = dt * arow
    cum = _xmm(tri_ref[...], a, 3)
    cumx = _mm_x(cum, e16_ref[...], 3)
    dtx = _mm_x(dt, e16_ref[...], 2)
    return sact, act, arow, dtpre, dt, cum, cumx, dtx


def _ssd_fwd(xbc, dtraw, zs, conv_w, conv_b, dtb, alog, dskx, gssm):
    S = xbc.shape[0]
    nc = S // CH
    tri, _, e16 = _ssd_consts()

    def body(xbc_ref, halo_ref, dtraw_ref, zs_ref, cw_ref, cb_ref, dtb_ref, alog_ref, dsk_ref, g_ref, tri_ref, e16_ref,
             xc_ref, y_ref, htp_ref, o_ref, ht, ext):
        i = pl.program_id(0)

        @pl.when(i == 0)
        def _():
            ht[...] = jnp.zeros_like(ht)

        xc = _ssd_conv(xbc_ref, halo_ref, cw_ref, cb_ref, ext, i == 0)
        xc_ref[...] = xc
        sact, act, arow, dtpre, dt, cum, cumx, dtx = _ssd_chunk_common(xc, dtraw_ref, dtb_ref, alog_ref, tri_ref, e16_ref)
        cum_t = cum.T
        xs = act[:, 0:SW]
        lastx = cumx[CH - 1:CH, :]
        xh = xs * dtx
        eexp = jnp.exp(cumx)
        dte = jnp.exp(lastx - cumx)
        cdx = jnp.exp(lastx)
        htp = ht[...]
        htp_ref[...] = htp
        xw = (xh * dte).astype(BF16)
        xb = xh.astype(BF16)
        trim = tri_ref[...].astype(F32) > 0.5
        lane = lax.broadcasted_iota(jnp.int32, (CH, 128), 1)
        parts = []
        for g in range(SG):
            gl = slice(g * GW, (g + 1) * GW)
            bg = act[:, SW + g * SN:SW + (g + 1) * SN].astype(BF16)
            cg = act[:, SW + SG * SN + g * SN:SW + SG * SN + (g + 1) * SN].astype(BF16)
            cbm = _nt(cg, bg)
            yoff = eexp[:, gl] * _mm(cg, htp[:, gl].astype(BF16))
            ht[:, gl] = htp[:, gl] * cdx[:, gl] + _tn(bg, xw[:, gl])
            for pr in range(GW // 128):
                h0 = g * (SH // SG) + 2 * pr
                lo = g * GW + pr * 128
                xp = xb[:, lo:lo + 128]
                res = []
                for hh in (h0, h0 + 1):
                    seg = cum[:, hh:hh + 1] - cum_t[hh:hh + 1, :]
                    mh = jnp.where(trim, cbm * jnp.exp(seg), 0.0).astype(BF16)
                    res.append(_mm(mh, xp))
                parts.append(jnp.where(lane < SP, res[0], res[1]) + yoff[:, pr * 128:(pr + 1) * 128])
        y = jnp.concatenate(parts, axis=1) + xs * dsk_ref[...]
        y_ref[...] = y
        z = zs_ref[...]
        hf = y * (z * _sigmoid(z))
        outs = []
        for g in range(SG):
            hg = hf[:, g * GW:(g + 1) * GW]
            rs = lax.rsqrt(jnp.mean(hg * hg, axis=-1, keepdims=True) + RMS_EPS)
            outs.append(hg * rs)
        o_ref[...] = (jnp.concatenate(outs, axis=1) * g_ref[...]).astype(BF16)

    row = lambda n: pl.BlockSpec((CH, n), lambda i: (i, 0))
    return pl.pallas_call(
        body, name="ssd_fwd", grid=(nc,),
        in_specs=[row(CC), pl.BlockSpec((8, CC), lambda i: (jnp.maximum(i * (CH // 8) - 1, 0), 0)), row(128), row(SW),
                  _const((CW, CC)), _const((1, CC)), _const((1, 128)), _const((1, 128)), _const((1, SW)), _const((1, SW)),
                  _const((CH, CH)), _const((128, SW))],
        out_specs=[row(CC), row(SW), pl.BlockSpec((None, SN, SW), lambda i: (i, 0, 0)), row(SW)],
        out_shape=[jax.ShapeDtypeStruct((S, CC), F32), jax.ShapeDtypeStruct((S, SW), F32),
                   jax.ShapeDtypeStruct((nc, SN, SW), F32), jax.ShapeDtypeStruct((S, SW), BF16)],
        scratch_shapes=[pltpu.VMEM((SN, SW), F32), pltpu.VMEM((8 + CH, CC), F32)],
        compiler_params=_cp(("arbitrary",)),
    )(xbc, xbc, dtraw, zs, conv_w, conv_b, dtb, alog, dskx, gssm, tri, e16)


def _outproj(o, za, ossm, x, tgt, wout, mod, ln_g, ln_b):
    S = x.shape[0]
    tm = min(TM, S)

    e8 = np.zeros((D, 128), np.float32)
    for h in range(NH):
        e8[h * VD:(h + 1) * VD, h] = 1.0
    e8 = jnp.asarray(e8, BF16)

    def body(o_ref, za_ref, os_ref, x_ref, t_ref, w_ref, mod_ref, g_ref, b_ref, e8_ref,
             gx_ref, do_ref, dza_ref, dos_ref, delta_ref, dw_ref, vec_ref):
        i = pl.program_id(0)

        @pl.when(i == 0)
        def _():
            dw_ref[...] = jnp.zeros_like(dw_ref)
            vec_ref[...] = jnp.zeros_like(vec_ref)

        gate = mod_ref[0:1, 2 * D:3 * D]
        ov = o_ref[...]
        z = za_ref[...]
        sz = _sigmoid(z)
        silz = z * sz
        a = (ov * silz).astype(BF16)
        osb = os_ref[...]
        mixed = _mm(a, w_ref[0:D, :]) + _mm(osb, w_ref[D:MIX, :])
        xv = x_ref[...]
        hres = ALPHA * xv + gate * mixed
        mu = jnp.mean(hres, axis=-1, keepdims=True)
        hc = hres - mu
        var = jnp.mean(hc * hc, axis=-1, keepdims=True)
        rstd = lax.rsqrt(var + LN_EPS)
        xhat = hc * rstd
        g = g_ref[...]
        yv = xhat * g + b_ref[...]
        err = yv - t_ref[...]
        dy = err * (1.0 / D)
        vec_ref[0:1, :] += jnp.sum(err * err, axis=0, keepdims=True)
        vec_ref[1:2, :] += jnp.sum(dy * xhat, axis=0, keepdims=True)
        vec_ref[2:3, :] += jnp.sum(dy, axis=0, keepdims=True)
        dxh = dy * g
        dh = rstd * (dxh - jnp.mean(dxh, axis=-1, keepdims=True) - xhat * jnp.mean(dxh * xhat, axis=-1, keepdims=True))
        gx_ref[...] = ALPHA * dh
        vec_ref[3:4, :] += jnp.sum(dh * mixed, axis=0, keepdims=True)
        dmixed = (gate * dh).astype(BF16)
        dw_ref[0:D, :] += _tn(a, dmixed)
        dw_ref[D:MIX, :] += _tn(osb, dmixed)
        da = _nt(dmixed, w_ref[0:D, :])
        dos_ref[...] = _nt(dmixed, w_ref[D:MIX, :])
        dov = da * silz
        do_ref[...] = dov.astype(BF16)
        dza_ref[...] = (da * ov * (sz * (1.0 + z * (1.0 - sz)))).astype(BF16)
        delta_ref[:, 0, :] = _mm_x(dov * ov, e8_ref[...], 2).T[0:NH, :]

    row = lambda n: pl.BlockSpec((tm, n), lambda i: (i, 0))
    return pl.pallas_call(
        body, name="outproj", grid=(S // tm,),
        in_specs=[row(D), row(D), row(D), row(D), row(D), _const((MIX, D)), _const((8, 3 * D)), _const((1, D)), _const((1, D)),
                  _const((D, 128))],
        out_specs=[row(D), row(D), row(D), row(D), pl.BlockSpec((NH, None, 1, tm), lambda i: (0, i, 0, 0)),
                   _full((MIX, D)), _full((8, D))],
        out_shape=[jax.ShapeDtypeStruct((S, D), F32), jax.ShapeDtypeStruct((S, D), BF16), jax.ShapeDtypeStruct((S, D), BF16),
                   jax.ShapeDtypeStruct((S, D), F32), jax.ShapeDtypeStruct((NH, S // tm, 1, tm), F32),
                   jax.ShapeDtypeStruct((MIX, D), F32), jax.ShapeDtypeStruct((8, D), F32)],
        compiler_params=_cp(("arbitrary",)),
    )(o, za, ossm, x, tgt, wout, mod, ln_g, ln_b, e8)


def _attn_bwd(q, k, v, do, lse, delta):
    _, S, _ = q.shape
    tk = min(TQ, S // 2)
    nk = S // tk
    tq = 2 * tk
    nq = S // tq

    def body(k_ref, v_ref, q_ref, do_ref, lse_ref, dl_ref, dk_ref, dv_ref, dq_ref, dqt_ref):
        j = pl.program_id(1)
        kb = k_ref[...]
        ktb = kb.T
        vb = v_ref[...]

        @pl.when(j == 0)
        def _():
            dqt_ref[...] = jnp.zeros_like(dqt_ref)

        dk_ref[...] = jnp.zeros_like(dk_ref)
        dv_ref[...] = jnp.zeros_like(dv_ref)

        def step(i, masked, lo=0):
            off = pl.multiple_of(i * tq + lo, tk)
            qb = q_ref[pl.ds(off, tq - lo), :]
            dob = do_ref[pl.ds(off, tq - lo), :]
            pt = jnp.exp2(_nt(kb, qb) - lse_ref[i][:, lo:tq])
            if masked:
                r = lax.broadcasted_iota(jnp.int32, pt.shape, 0)
                cidx = lax.broadcasted_iota(jnp.int32, pt.shape, 1)
                pt = jnp.where(i * tq + lo + cidx >= j * tk + r, pt, 0.0)
            dv_ref[...] += _mm(pt.astype(BF16), dob)
            dsb = (pt * (_nt(vb, dob) - dl_ref[i][:, lo:tq])).astype(BF16)
            dk_ref[...] += _mm(dsb, qb)
            dqt_ref[i, :, lo:tq] += _mm(ktb, dsb)

        first = j >> 1

        @pl.when((j & 1) == 0)
        def _():
            step(first, True)

        @pl.when((j & 1) == 1)
        def _():
            step(first, True, tk)
            dq_ref[...] = dqt_ref[first].T

        def loop_body(t, carry):
            for u in range(4):
                step(first + 1 + 4 * t + u, False)
            return carry

        def tail_body(i, carry):
            step(i, False)
            return carry

        rest = nq - 1 - first
        lax.fori_loop(0, rest >> 2, loop_body, 0)
        lax.fori_loop(nq - (rest & 3), nq, tail_body, 0)

        dk_ref[...] = dk_ref[...] * LN2

    return pl.pallas_call(
        body, name="attn_bwd", grid=(NH, nk),
        in_specs=[pl.BlockSpec((None, tk, HP), lambda h, j: (h, j, 0)),
                  pl.BlockSpec((None, tk, VD), lambda h, j: (h, j, 0)),
                  pl.BlockSpec((None, S, HP), lambda h, j: (h, 0, 0)),
                  pl.BlockSpec((S, VD), lambda h, j: (0, h)),
                  pl.BlockSpec((None, nq, 1, tq), lambda h, j: (h, 0, 0, 0)),
                  pl.BlockSpec((None, nq, 1, tq), lambda h, j: (h, 0, 0, 0))],
        out_specs=[pl.BlockSpec((None, tk, HP), lambda h, j: (h, j, 0)),
                   pl.BlockSpec((None, tk, VD), lambda h, j: (h, j, 0)),
                   pl.BlockSpec((None, tq, HP), lambda h, j: (h, j >> 1, 0))],
        out_shape=[jax.ShapeDtypeStruct((NH, S, HP), F32), jax.ShapeDtypeStruct((NH, S, VD), F32),
                   jax.ShapeDtypeStruct((NH, S, HP), F32)],
        scratch_shapes=[pltpu.VMEM((nq, HP, tq), F32)],
        compiler_params=_cp(("arbitrary", "arbitrary")),
    )(k, v, q, do, lse.reshape(NH, nq, 1, tq), delta.reshape(NH, nq, 1, tq))


def _ssd_bwd(xbc, xc, dtraw, zs, y, htp, dossm, conv_w, dtb, alog, dskx, gssm):
    S = xbc.shape[0]
    nc = S // CH
    tri, triu, e16 = _ssd_consts()

    def body(xbc_ref, xc_ref, dtraw_ref, zs_ref, y_ref, htp_ref, dos_ref,
             cw_ref, dtb_ref, alog_ref, dsk_ref, g_ref, tri_ref, triu_ref, e16_ref,
             dxbc_ref, ddt_ref, dzs_ref, dcw_ref, dcb_ref, dvec_ref, dg_ref,
             dht, dext, dskacc):
        r = pl.program_id(0)

        @pl.when(r == 0)
        def _():
            dht[...] = jnp.zeros_like(dht)
            dext[CH:CH + 8, :] = jnp.zeros((8, CC), F32)
            dskacc[...] = jnp.zeros_like(dskacc)
            dcw_ref[...] = jnp.zeros_like(dcw_ref)
            dcb_ref[...] = jnp.zeros_like(dcb_ref)
            dvec_ref[...] = jnp.zeros_like(dvec_ref)
            dg_ref[...] = jnp.zeros_like(dg_ref)

        xc = xc_ref[...]
        sact, act, arow, dtpre, dt, cum, cumx, dtx = _ssd_chunk_common(xc, dtraw_ref, dtb_ref, alog_ref, tri_ref, e16_ref)
        cum_t = cum.T
        xs = act[:, 0:SW]
        lastx = cumx[CH - 1:CH, :]
        xh = xs * dtx
        eexp = jnp.exp(cumx)
        dte = jnp.exp(lastx - cumx)
        cdx = jnp.exp(lastx)
        trim = tri_ref[...].astype(F32) > 0.5
        lane = lax.broadcasted_iota(jnp.int32, (CH, 128), 1)
        rowi = lax.broadcasted_iota(jnp.int32, (CH, 128), 0)

        yv = y_ref[...]
        z = zs_ref[...]
        sz = _sigmoid(z)
        silz = z * sz
        hf = yv * silz
        dn = dos_ref[...] * g_ref[...]
        dhf_parts, nrm_parts = [], []
        for g in range(SG):
            gl = slice(g * GW, (g + 1) * GW)
            hg = hf[:, gl]
            rs = lax.rsqrt(jnp.mean(hg * hg, axis=-1, keepdims=True) + RMS_EPS)
            ng = hg * rs
            dng = dn[:, gl]
            dhf_parts.append(rs * (dng - ng * jnp.mean(dng * ng, axis=-1, keepdims=True)))
            nrm_parts.append(ng)
        nrm = jnp.concatenate(nrm_parts, axis=1)
        dhf = jnp.concatenate(dhf_parts, axis=1)
        dg_ref[...] += jnp.sum(dos_ref[...] * nrm, axis=0, keepdims=True)
        dyv = dhf * silz
        dzs_ref[...] = (dhf * yv * (sz * (1.0 + z * (1.0 - sz)))).astype(BF16)
        dskacc[...] += jnp.sum(dyv * xs, axis=0, keepdims=True)
        dxs_skip = dyv * dsk_ref[...]

        dhtn = dht[...]
        hp = htp_ref[...]
        dlastx = jnp.sum(dhtn * hp, axis=0, keepdims=True) * cdx
        xb = xh.astype(BF16)
        xwf = xh * dte
        dcum = jnp.zeros((CH, 128), F32)
        dcum_t = jnp.zeros((128, CH), F32)
        dxh_parts, dcumx_parts, dlast_parts, db_parts, dc_parts = [], [], [], [], []
        for g in range(SG):
            gl = slice(g * GW, (g + 1) * GW)
            bg = act[:, SW + g * SN:SW + (g + 1) * SN].astype(BF16)
            cg = act[:, SW + SG * SN + g * SN:SW + SG * SN + (g + 1) * SN].astype(BF16)
            hpg = hp[:, gl].astype(BF16)
            dhn = dhtn[:, gl].astype(BF16)
            dyg = dyv[:, gl]
            dz = (dyg * eexp[:, gl]).astype(BF16)
            dcg = _nt(dz, hpg)
            dht[:, gl] = dhtn[:, gl] * cdx[:, gl] + _tn(cg, dz)
            yoff = eexp[:, gl] * _mm(cg, hpg)
            dcumx_g = dyg * yoff
            dbg = _nt(xwf[:, gl].astype(BF16), dhn)
            dxw = _mm(bg, dhn)
            ddte = dxw * xwf[:, gl]
            dcumx_parts.append(dcumx_g - ddte)
            dlast_parts.append(jnp.sum(ddte, axis=0, keepdims=True))
            dxh_g = dxw * dte[:, gl]
            cbm = _nt(cg, bg)
            dcb = jnp.zeros((CH, CH), F32)
            dxp_parts = []
            for pr in range(GW // 128):
                h0 = g * (SH // SG) + 2 * pr
                lo = g * GW + pr * 128
                xp = xb[:, lo:lo + 128]
                dyp = dyv[:, lo:lo + 128]
                dxp = jnp.zeros((CH, 128), F32)
                for idx, hh in enumerate((h0, h0 + 1)):
                    decay = jnp.where(trim, jnp.exp(cum[:, hh:hh + 1] - cum_t[hh:hh + 1, :]), 0.0)
                    mh = cbm * decay
                    keep = (lane < SP) if idx == 0 else (lane >= SP)
                    dym = jnp.where(keep, dyp, 0.0).astype(BF16)
                    dm = _nt(dym, xp)
                    dxp = dxp + _tn(mh.astype(BF16), dym)
                    gm = dm * mh
                    dcum = dcum + jnp.where(lane == hh, jnp.sum(gm, axis=1, keepdims=True), 0.0)
                    dcum_t = dcum_t - jnp.where(rowi == hh, jnp.sum(gm, axis=0, keepdims=True), 0.0)
                    dcb = dcb + dm * decay
                dxp_parts.append(dxp)
            dxh_parts.append(dxh_g + jnp.concatenate(dxp_parts, axis=1))
            dcbb = dcb.astype(BF16)
            dc_parts.append(dcg + _mm(dcbb, bg))
            db_parts.append(dbg + _tn(dcbb, cg))
        dxh = jnp.concatenate(dxh_parts, axis=1)
        dcumx = jnp.concatenate(dcumx_parts, axis=1)
        dlastx = dlastx + jnp.concatenate(dlast_parts, axis=1)
        e16 = e16_ref[...]
        dlast128 = _nt_x(jnp.broadcast_to(dlastx, (8, SW)), e16, 2)[0:1, :]
        dcum = dcum + dcum_t.T + _nt_x(dcumx, e16, 2) + jnp.where(rowi == CH - 1, dlast128, 0.0)
        da = _xmm(triu_ref[...], dcum, 2)
        ddt = da * arow + _nt_x(dxh * xs, e16, 2)
        dvec_ref[1:2, :] += jnp.sum(da * dt, axis=0, keepdims=True)
        ddtraw = jnp.where(lane < SH, ddt * _sigmoid(dtpre), 0.0)
        dvec_ref[0:1, :] += jnp.sum(ddtraw, axis=0, keepdims=True)
        ddt_ref[...] = ddtraw.astype(BF16)
        dxs = dxs_skip + dxh * dtx
        dact = jnp.concatenate([dxs] + db_parts + dc_parts, axis=1)
        dxc = dact * (sact * (1.0 + xc * (1.0 - sact)))

        dcb_ref[...] += jnp.sum(dxc, axis=0, keepdims=True)
        dext[0:CH, :] = dxc
        cw = cw_ref[...]
        xraw = xbc_ref[...]
        dxr = cw[CW - 1:CW, :] * dxc
        dcw_ref[CW - 1:CW, :] += jnp.sum(dxc * xraw, axis=0, keepdims=True)
        for kk in range(CW - 1):
            dwin = dext[CW - 1 - kk:CW - 1 - kk + CH, :]
            dxr = dxr + cw[kk:kk + 1, :] * dwin
            dcw_ref[kk:kk + 1, :] += jnp.sum(dwin * xraw, axis=0, keepdims=True)
        dxbc_ref[...] = dxr.astype(BF16)
        dext[CH:CH + 8, :] = dxc[0:8, :]

        @pl.when(r == nc - 1)
        def _():
            dvec_ref[1:2, :] = dvec_ref[1:2, :] * arow
            dvec_ref[2:3, :] = _nt_x(jnp.broadcast_to(dskacc[...], (8, SW)), e16, 3)[0:1, :]

    rev = lambda n: pl.BlockSpec((CH, n), lambda r: (nc - 1 - r, 0))
    return pl.pallas_call(
        body, name="ssd_bwd", grid=(nc,),
        in_specs=[rev(CC), rev(CC),
                  rev(128), rev(SW), rev(SW), pl.BlockSpec((None, SN, SW), lambda r: (nc - 1 - r, 0, 0)), rev(SW),
                  _const((CW, CC)), _const((1, 128)), _const((1, 128)), _const((1, SW)), _const((1, SW)),
                  _const((CH, CH)), _const((CH, CH)), _const((128, SW))],
        out_specs=[rev(CC), rev(128), rev(SW), _full((CW, CC)), _full((1, CC)), _full((8, 128)), _full((1, SW))],
        out_shape=[jax.ShapeDtypeStruct((S, CC), BF16), jax.ShapeDtypeStruct((S, 128), BF16), jax.ShapeDtypeStruct((S, SW), BF16),
                   jax.ShapeDtypeStruct((CW, CC), F32), jax.ShapeDtypeStruct((1, CC), F32),
                   jax.ShapeDtypeStruct((8, 128), F32), jax.ShapeDtypeStruct((1, SW), F32)],
        scratch_shapes=[pltpu.VMEM((SN, SW), F32), pltpu.VMEM((CH + 8, CC), F32), pltpu.VMEM((1, SW), F32)],
        compiler_params=_cp(("arbitrary",)),
    )(xbc, xc, dtraw, zs, y, htp, dossm, conv_w, dtb, alog, dskx, gssm, tri, triu, e16)


def _mla_bwd(dq, dk, dv, qlat, ckv, qg, kvg, wq, wk, wv, pos, invf):
    S = qlat.shape[0]
    tm = min(TQ, S)

    def body(dq_ref, dk_ref, dv_ref, ql_ref, ckv_ref, qg_ref, kvg_ref, wq_ref, wk_ref, wv_ref, pos_ref, invf_ref,
             dql_ref, dckv_ref, dkr_ref, dwq_ref, dwk_ref, dwv_ref, dqg_ref, dkvg_ref):
        i = pl.program_id(0)

        @pl.when(i == 0)
        def _():
            dwq_ref[...] = jnp.zeros_like(dwq_ref)
            dwk_ref[...] = jnp.zeros_like(dwk_ref)
            dwv_ref[...] = jnp.zeros_like(dwv_ref)
            dqg_ref[...] = jnp.zeros_like(dqg_ref)
            dkvg_ref[...] = jnp.zeros_like(dkvg_ref)

        ang = pos_ref[...].astype(F32) * invf_ref[...]
        cs = jnp.cos(ang)
        sn = jnp.sin(ang)

        def rms_bwd(v, g, dn, dg_ref):
            r = lax.rsqrt(jnp.mean(v * v, axis=-1, keepdims=True) + RMS_EPS)
            vh = v * r
            dg_ref[...] += jnp.sum(dn * vh, axis=0, keepdims=True)
            dvh = dn * g
            return vh, r * (dvh - vh * jnp.mean(dvh * vh, axis=-1, keepdims=True))

        pieces = []
        for h in range(NH):
            dqh = dq_ref[h]
            pieces.append(dqh[:, 0:NOPE] * SCALE)
            pieces.append(_rope_t(dqh[:, NOPE:HP], cs, sn) * SCALE)
        dqf = jnp.concatenate(pieces, axis=1).astype(BF16)
        ql = ql_ref[...]
        g = qg_ref[...]
        dqn = _nt(dqf, wq_ref[...])
        qh, dql = rms_bwd(ql, g, dqn, dqg_ref)
        dwq_ref[...] += _tn((qh * g).astype(BF16), dqf)
        dql_ref[...] = dql.astype(BF16)

        dkn_p = jnp.concatenate([dk_ref[h, :, 0:NOPE] for h in range(NH)], axis=1).astype(BF16)
        dvf = jnp.concatenate([dv_ref[h] for h in range(NH)], axis=1).astype(BF16)
        dkr = dk_ref[0, :, NOPE:HP]
        for h in range(1, NH):
            dkr = dkr + dk_ref[h, :, NOPE:HP]
        lane = lax.broadcasted_iota(jnp.int32, dkr.shape, 1)
        dkr_ref[...] = jnp.where(lane < ROPE, _rope_t(dkr, cs, sn), 0.0).astype(BF16)
        cv = ckv_ref[...]
        gk = kvg_ref[...]
        dkn = _nt(dkn_p, wk_ref[...]) + _nt(dvf, wv_ref[...])
        kh, dckv = rms_bwd(cv, gk, dkn, dkvg_ref)
        knb = (kh * gk).astype(BF16)
        dwk_ref[...] += _tn(knb, dkn_p)
        dwv_ref[...] += _tn(knb, dvf)
        dckv_ref[...] = dckv.astype(BF16)

    row = lambda n: pl.BlockSpec((tm, n), lambda i: (i, 0))
    heads = lambda n: pl.BlockSpec((NH, tm, n), lambda i: (0, i, 0))
    return pl.pallas_call(
        body, name="mla_bwd", grid=(S // tm,),
        in_specs=[heads(HP), heads(HP), heads(VD), row(QL), row(KVL), _const((1, QL)), _const((1, KVL)),
                  _const((QL, NH * HP)), _const((KVL, NH * NOPE)), _const((KVL, NH * VD)), row(1), _const((1, 128))],
        out_specs=[row(QL), row(KVL), row(128), _full((QL, NH * HP)), _full((KVL, NH * NOPE)), _full((KVL, NH * VD)),
                   _full((1, QL)), _full((1, KVL))],
        out_shape=[jax.ShapeDtypeStruct((S, QL), BF16), jax.ShapeDtypeStruct((S, KVL), BF16), jax.ShapeDtypeStruct((S, 128), BF16),
                   jax.ShapeDtypeStruct((QL, NH * HP), F32), jax.ShapeDtypeStruct((KVL, NH * NOPE), F32),
                   jax.ShapeDtypeStruct((KVL, NH * VD), F32), jax.ShapeDtypeStruct((1, QL), F32), jax.ShapeDtypeStruct((1, KVL), F32)],
        compiler_params=_cp(("arbitrary",)),
    )(dq, dk, dv, qlat, ckv, qg, kvg, wq, wk, wv, pos, invf)


def _inproj_bwd(x, gx1, mod, win, dql, dckv, dza, dxbc, dzs, dkr, ddt):
    S = x.shape[0]
    tm = min(TM, S)

    def body(x_ref, gx1_ref, mod_ref, win_ref, dql_ref, dckv_ref, dza_ref, dxbc_ref, dzs_ref, dkr_ref, ddt_ref,
             gx_ref, dw_ref, vec_ref):
        i = pl.program_id(0)

        @pl.when(i == 0)
        def _():
            dw_ref[...] = jnp.zeros_like(dw_ref)
            vec_ref[...] = jnp.zeros_like(vec_ref)

        shift = mod_ref[0:1, 0:D]
        scale = mod_ref[0:1, D:2 * D]
        xv = x_ref[...]
        ut = (xv * (1.0 + scale) + shift).T.astype(BF16)
        pieces = (dql_ref, dckv_ref, dza_ref, dxbc_ref, dzs_ref, dkr_ref, ddt_ref)
        du = jnp.zeros((tm, D), F32)
        lo = 0
        for p_ref in pieces:
            n = p_ref.shape[1]
            dp = p_ref[...]
            du = du + _nt(dp, win_ref[:, lo:lo + n])
            dw_ref[:, lo:lo + n] += _mm(ut, dp)
            lo += n
        vec_ref[0:1, :] += jnp.sum(du, axis=0, keepdims=True)
        vec_ref[1:2, :] += jnp.sum(du * xv, axis=0, keepdims=True)
        gx_ref[...] = gx1_ref[...] + du * (1.0 + scale)

    row = lambda n: pl.BlockSpec((tm, n), lambda i: (i, 0))
    return pl.pallas_call(
        body, name="inproj_bwd", grid=(S // tm,),
        in_specs=[row(D), row(D), _const((8, 3 * D)), _const((D, IN_P)), row(QL), row(KVL), row(D), row(CC), row(D),
                  row(128), row(128)],
        out_specs=[row(D), pl.BlockSpec((D, IN_P), lambda i: (0, 0), pipeline_mode=pl.Buffered(1)), _full((8, D))],
        out_shape=[jax.ShapeDtypeStruct((S, D), F32), jax.ShapeDtypeStruct((D, IN_P), F32), jax.ShapeDtypeStruct((8, D), F32)],
        compiler_params=_cp(("arbitrary",)),
    )(x, gx1, mod, win, dql, dckv, dza, dxbc, dzs, dkr, ddt)


def _ada_bwd(callt, dmods):
    w = dmods.shape[1]

    def body(c_ref, d_ref, o_ref):
        acc = c_ref[:, 0:1] * d_ref[0:1, :]
        for s in range(1, 8):
            acc = acc + c_ref[:, s:s + 1] * d_ref[s:s + 1, :]
        o_ref[0] = acc

    return pl.pallas_call(body, name="ada_bwd", out_shape=jax.ShapeDtypeStruct((1, D, w), F32),
                          compiler_params=_cp())(callt, dmods)


def _adamw(name, parts, w, m, v):
    rows, ncol = w.shape
    tr = min(rows, 128)
    nparts = parts.shape[0]

    def body(p_ref, w_ref, m_ref, v_ref, g_ref, d_ref, nm_ref, nv_ref):
        g = p_ref[0].astype(F32)
        for s in range(1, nparts):
            g = g + p_ref[s].astype(F32)
        g_ref[...] = g
        nm = B1 * m_ref[...] + (1.0 - B1) * g
        nv = B2 * v_ref[...] + (1.0 - B2) * (g * g)
        nm_ref[...] = nm
        nv_ref[...] = nv
        m_hat = nm / (1.0 - B1 ** STEP)
        v_hat = nv / (1.0 - B2 ** STEP)
        d_ref[...] = -LR * (m_hat / (jnp.sqrt(v_hat) + EPS) + WD * w_ref[...])

    row = pl.BlockSpec((tr, ncol), lambda i: (i, 0))
    sd = jax.ShapeDtypeStruct((rows, ncol), F32)
    return pl.pallas_call(
        body, name="adamw_" + name, grid=(rows // tr,),
        in_specs=[pl.BlockSpec((nparts, tr, ncol), lambda i: (0, i, 0)), row, row, row],
        out_specs=[row, row, row, row], out_shape=[sd, sd, sd, sd],
        compiler_params=_cp(("arbitrary",)),
    )(parts, w, m, v)


_SMALL = (("b_ada", 3 * D), ("conv_w", CW * CC // 4), ("conv_b", CC), ("ssm_norm_g", SW), ("ln_g", D), ("ln_b", D),
          ("q_norm_g", QL), ("kv_norm_g", KVL), ("dt_bias", SH), ("a_log", SH), ("d_skip", SH), ("loss", 128))


def _pack_small(d, lead):
    flat = [d[name].reshape(d[name].shape[:lead] + (-1,)) for name, _ in _SMALL]
    used = sum(f.shape[lead] for f in flat)
    pad = jnp.zeros(flat[0].shape[:lead] + (R_SMALL * 1024 - used,), F32)
    return jnp.concatenate(flat + [pad], axis=lead).reshape(flat[0].shape[:lead] + (R_SMALL, 1024))


def _unpack_small(p):
    flat = p.reshape(-1)
    out, r = {}, 0
    for name, n in _SMALL:
        out[name] = flat[r:r + n]
        r += n
    return out


def _in_to_padded(w):
    z = lambda n: jnp.zeros((w.shape[0], n), w.dtype)
    return jnp.concatenate([w[:, 0:384], w[:, 384:640], w[:, 704:1728], w[:, 1728:3264], w[:, 3280:4304],
                            w[:, 640:704], z(64), w[:, 3264:3280], z(112)], axis=1)


def _in_from_padded(g):
    return jnp.concatenate([g[:, 0:384], g[:, 384:640], g[:, P_KR[0]:P_KR[0] + 64], g[:, 640:1664], g[:, 1664:3200],
                            g[:, P_DT[0]:P_DT[0] + 16], g[:, 3200:4224]], axis=1)


def kernel(x, c, positions, w_ada, b_ada, w_in, q_norm_g, w_qb, kv_norm_g, w_kvb, conv_w, conv_b, dt_bias, a_log, d_skip, ssm_norm_g, w_out, ln_g, ln_b, loss_target, m_w_ada, m_b_ada, m_w_in, m_q_norm_g, m_w_qb, m_kv_norm_g, m_w_kvb, m_conv_w, m_conv_b, m_dt_bias, m_a_log, m_d_skip, m_ssm_norm_g, m_w_out, m_ln_g, m_ln_b, v_w_ada, v_b_ada, v_w_in, v_q_norm_g, v_w_qb, v_kv_norm_g, v_w_kvb, v_conv_w, v_conv_b, v_dt_bias, v_a_log, v_d_skip, v_ssm_norm_g, v_w_out, v_ln_g, v_ln_b):
    S = x.shape[1]
    xv = x[0]
    tgt = loss_target[0]

    cw16 = jnp.concatenate([conv_w[0], jnp.zeros((16 - CW, CC // 4), F32)], axis=0)
    f_in, f_qb, f_kvb, f_out, f_cw = _gather_weights(
        [w_in[0].astype(BF16), w_qb[0].astype(BF16), w_kvb[0].astype(BF16), w_out[0].astype(BF16), cw16])
    cat1 = lambda f: jnp.concatenate([f[k] for k in range(4)], axis=1)
    win = _in_to_padded(cat1(f_in))
    wqb = cat1(f_qb).reshape(QL, NH, QKD)
    wq = jnp.concatenate([wqb, jnp.zeros((QL, NH, HP - QKD), BF16)], axis=2).reshape(QL, NH * HP)
    wkvb = cat1(f_kvb).reshape(KVL, NH, NOPE + VD)
    wk = wkvb[:, :, 0:NOPE].reshape(KVL, NH * NOPE)
    wv = wkvb[:, :, NOPE:].reshape(KVL, NH * VD)
    wout = f_out.reshape(MIX, D)
    cwf = cat1(f_cw[:, 0:CW, :])

    half = ROPE // 2
    invf = 1.0 / (ROPE_THETA ** (jnp.arange(half, dtype=F32) / half))
    invf = jnp.concatenate([invf, invf, jnp.zeros((128 - ROPE,), F32)]).reshape(1, 128)
    pos = positions.reshape(S, 1)
    pad128 = lambda a: jnp.concatenate([a.reshape(1, SH), jnp.zeros((1, 128 - SH), F32)], axis=1)
    dtb, alog = pad128(dt_bias), pad128(a_log)
    dskx = jnp.repeat(d_skip.reshape(SH), SP).reshape(1, SW)

    my_c = lax.axis_index("c")
    (call,) = _exchange("gather_c", [jnp.broadcast_to(c.reshape(1, 1, D), (4, 1, D))])
    call = call.reshape(8, D)
    mods = _ada(call, w_ada[0])
    (mrows,) = _exchange("scatter_mod", [mods.reshape(4, 2, 3 * D // 4)])
    mine = lax.dynamic_index_in_dim(mrows.reshape(4, 2, 2, 3 * D // 4)[:, 0], my_c, axis=1, keepdims=False)
    mod = jnp.broadcast_to(mine.reshape(1, 3 * D) + b_ada, (8, 3 * D))
    qlat, ckv, za, xbc, zs, dtraw, q, k, v = _inproj(xv, mod, win, q_norm_g, kv_norm_g, wq, wk, wv, pos, invf)
    o, lse = _attn_fwd(q, k, v)
    xc, y, htp, ossm = _ssd_fwd(xbc, dtraw, zs, cwf, conv_b, dtb, alog, dskx, ssm_norm_g)
    gx1, do, dza, dossm, delta, dwout, vec_o = _outproj(o, za, ossm, xv, tgt, wout, mod, ln_g, ln_b)
    loss_part = jnp.zeros((128,), F32).at[0].set(0.5 / D * jnp.sum(vec_o[0]))

    dk, dv, dq = _attn_bwd(q, k, v, do, lse, delta)
    dxbc, ddt, dzs, dcw, dcb, dvec, dgssm = _ssd_bwd(xbc, xc, dtraw, zs, y, htp, dossm, cwf, dtb, alog, dskx, ssm_norm_g)
    dql, dckv, dkr, dwq, dwk, dwv, dqg, dkvg = _mla_bwd(dq, dk, dv, qlat, ckv, q_norm_g, kv_norm_g, wq, wk, wv, pos, invf)
    gx, dwin, vec_i = _inproj_bwd(xv, gx1, mod, win, dql, dckv, dza, dxbc, dzs, dkr, ddt)
    dmod = jnp.concatenate([vec_i[0:1], vec_i[1:2], vec_o[3:4]], axis=1)

    cols = lambda g: g.reshape(g.shape[0], 4, g.shape[1] // 4).transpose(1, 0, 2)
    g_in = cols(_in_from_padded(dwin)).astype(BF16)
    g_qb = cols(dwq.reshape(QL, NH, HP)[:, :, 0:QKD].reshape(QL, NH * QKD)).astype(BF16)
    g_kvb = cols(jnp.concatenate([dwk.reshape(KVL, NH, NOPE), dwv.reshape(KVL, NH, VD)], axis=2)
                 .reshape(KVL, NH * (NOPE + VD))).astype(BF16)
    g_out = dwout.reshape(4, MIX // 4, D).astype(BF16)
    small = {"b_ada": dmod, "conv_b": dcb, "ssm_norm_g": dgssm, "ln_g": vec_o[1:2], "ln_b": vec_o[2:3],
             "q_norm_g": dqg, "kv_norm_g": dkvg, "dt_bias": dvec[0:1, 0:SH], "a_log": dvec[1:2, 0:SH], "d_skip": dvec[2:3, 0:SH],
             "loss": loss_part}
    small = {n: jnp.broadcast_to(a.reshape(1, -1), (4, a.size)) for n, a in small.items()}
    small["conv_w"] = cols(dcw).reshape(4, CW * CC // 4)
    gsmall = _pack_small(small, 1)

    r_in, r_qb, r_kvb, r_out, rs, dmods = _exchange(
        "exchange_grads", [g_in, g_qb, g_kvb, g_out, gsmall, jnp.broadcast_to(dmod.reshape(1, 1, 3 * D), (4, 1, 3 * D))])
    chip = 2 * lax.axis_index("x") + lax.axis_index("y")
    dmods = lax.dynamic_slice_in_dim(dmods.reshape(8, 3 * D), chip * (3 * D // 4), 3 * D // 4, axis=1)
    g_ada = _ada_bwd(call.T, dmods)
    res = dict(w_ada=_adamw("w_ada", g_ada, w_ada[0], m_w_ada[0], v_w_ada[0]),
               w_in=_adamw("w_in", r_in, w_in[0], m_w_in[0], v_w_in[0]),
               w_qb=_adamw("w_qb", r_qb, w_qb[0], m_w_qb[0], v_w_qb[0]),
               w_kvb=_adamw("w_kvb", r_kvb, w_kvb[0], m_w_kvb[0], v_w_kvb[0]),
               w_out=_adamw("w_out", r_out, w_out[0], m_w_out[0], v_w_out[0]))
    wsm = _pack_small(dict(b_ada=b_ada, conv_w=conv_w, conv_b=conv_b, ssm_norm_g=ssm_norm_g, ln_g=ln_g, ln_b=ln_b,
                           q_norm_g=q_norm_g, kv_norm_g=kv_norm_g, dt_bias=dt_bias, a_log=a_log, d_skip=d_skip, loss=jnp.zeros((128,), F32)), 0)
    msm = _pack_small(dict(b_ada=m_b_ada, conv_w=m_conv_w, conv_b=m_conv_b, ssm_norm_g=m_ssm_norm_g, ln_g=m_ln_g, ln_b=m_ln_b,
                           q_norm_g=m_q_norm_g, kv_norm_g=m_kv_norm_g, dt_bias=m_dt_bias, a_log=m_a_log, d_skip=m_d_skip, loss=jnp.zeros((128,), F32)), 0)
    vsm = _pack_small(dict(b_ada=v_b_ada, conv_w=v_conv_w, conv_b=v_conv_b, ssm_norm_g=v_ssm_norm_g, ln_g=v_ln_g, ln_b=v_ln_b,
                           q_norm_g=v_q_norm_g, kv_norm_g=v_kv_norm_g, dt_bias=v_dt_bias, a_log=v_a_log, d_skip=v_d_skip, loss=jnp.zeros((128,), F32)), 0)
    sm = _adamw("small", rs, wsm, msm, vsm)

    order = ["w_ada", "b_ada", "w_in", "q_norm_g", "w_qb", "kv_norm_g", "w_kvb", "conv_w", "conv_b", "dt_bias", "a_log",
             "d_skip", "ssm_norm_g", "w_out", "ln_g", "ln_b"]
    shapes = dict(w_ada=w_ada.shape, b_ada=b_ada.shape, w_in=w_in.shape, q_norm_g=q_norm_g.shape, w_qb=w_qb.shape,
                  kv_norm_g=kv_norm_g.shape, w_kvb=w_kvb.shape, conv_w=conv_w.shape, conv_b=conv_b.shape, dt_bias=dt_bias.shape,
                  a_log=a_log.shape, d_skip=d_skip.shape, ssm_norm_g=ssm_norm_g.shape, w_out=w_out.shape, ln_g=ln_g.shape,
                  ln_b=ln_b.shape)
    outs = []
    for kind in range(4):
        d = _unpack_small(sm[kind])
        d.update({n: r[kind] for n, r in res.items()})
        outs.extend(d[n].reshape(shapes[n]) for n in order)
    loss = _unpack_small(sm[0])["loss"][0]
    return (loss, gx.reshape(x.shape), *outs)
```

```python
import functools
import math

import numpy as np
import jax
import jax.numpy as jnp
from jax import lax
from jax.experimental import pallas as pl
from jax.experimental.pallas import tpu as pltpu

F32 = jnp.float32
BF16 = jnp.bfloat16
MESH_ID = pl.DeviceIdType.MESH

D = 1024
NH = 8
NOPE = 128
ROPE = 64
VD = 128
VDP = 144
QKD = NOPE + ROPE
HP = 256
QL = 384
KVL = 256
ROPE_THETA = 10000.0
SH = 16
SP = 64
SG = 2
SN = 128
CW = 4
CH = 128
SW = SH * SP
CC = SW + 2 * SG * SN
GW = SW // SG
MIX = 2 * D
IN_W = 4304
ALPHA = 2.0 ** 0.25
RMS_EPS = 1e-6
LN_EPS = 1e-5
SCALE = QKD ** -0.5
LN2 = math.log(2.0)
QSCALE = SCALE / LN2
LR, B1, B2, EPS, WD, STEP = 0.001, 0.9, 0.999, 1e-08, 0.01, 10

P_Q = (0, 384)
P_KV = (384, 640)
P_ZA = (640, 1664)
P_XBC = (1664, 3200)
P_ZS = (3200, 4224)
P_KR = (4224, 4352)
P_DT = (4352, 4480)
IN_P = 4480

R_SMALL = 16

TM = 256
TQ = 512
NSP = 2
NCK = 2
TQF = 1024
VMEM_LIMIT = 56 * 1024 * 1024


def _cp(sem=None):
    return pltpu.CompilerParams(dimension_semantics=sem, vmem_limit_bytes=VMEM_LIMIT)


def _mm(a, b):
    return jnp.dot(a, b, preferred_element_type=F32)


def _nt(a, b):
    return lax.dot_general(a, b, (((1,), (1,)), ((), ())), preferred_element_type=F32)


def _tn(a, b):
    return lax.dot_general(a, b, (((0,), (0,)), ((), ())), preferred_element_type=F32)


def _split(a, terms):
    parts = []
    for t in range(terms):
        p = a.astype(BF16)
        parts.append(p)
        if t + 1 < terms:
            a = a - p.astype(F32)
    return parts


def _mm_x(a, ones, terms):
    parts = _split(a, terms)
    out = _mm(parts[0], ones)
    for p in parts[1:]:
        out = out + _mm(p, ones)
    return out


def _xmm(ones, a, terms):
    parts = _split(a, terms)
    out = _mm(ones, parts[0])
    for p in parts[1:]:
        out = out + _mm(ones, p)
    return out


def _nt_x(a, ones, terms):
    parts = _split(a, terms)
    out = _nt(parts[0], ones)
    for p in parts[1:]:
        out = out + _nt(p, ones)
    return out


def _sigmoid(z):
    return 1.0 / (1.0 + jnp.exp(-z))


def _softplus(z):
    return jnp.maximum(z, 0.0) + jnp.log1p(jnp.exp(-jnp.abs(z)))


def _rope(t, cs, sn):
    lane = lax.broadcasted_iota(jnp.int32, t.shape, 1)
    rot = jnp.where(lane < ROPE // 2, -pltpu.roll(t, 128 - ROPE // 2, 1), pltpu.roll(t, ROPE // 2, 1))
    return t * cs + rot * sn


def _rope_t(t, cs, sn):
    lane = lax.broadcasted_iota(jnp.int32, t.shape, 1)
    y = t * sn
    rot = jnp.where(lane < ROPE // 2, -pltpu.roll(y, 128 - ROPE // 2, 1), pltpu.roll(y, ROPE // 2, 1))
    return t * cs - rot


def _full(shape):
    n = len(shape)
    return pl.BlockSpec(shape, lambda *_: (0,) * n)


def _const(shape):
    n = len(shape)
    return pl.BlockSpec(shape, lambda *_: (0,) * n, pipeline_mode=pl.Buffered(1))


def _gather_weights(shards):
    n = len(shards)
    halves = [a.shape[0] // 2 for a in shards]

    def body(*refs):
        srcs, dsts = refs[:n], refs[n:2 * n]
        send_sems, recv_sems, local_sems = refs[2 * n:]
        x, y, c = lax.axis_index("x"), lax.axis_index("y"), lax.axis_index("c")
        me = 2 * x + y
        sibling = (x, y, 1 - c)
        chips = [(1 - x, y), (x, 1 - y), (1 - x, 1 - y)]

        def rows(a, pc):
            return pl.ds(pl.multiple_of(pc * halves[a], halves[a]), halves[a])

        def copy(a, k, src, slot, pc, to):
            return pltpu.make_async_remote_copy(
                src_ref=src, dst_ref=dsts[a].at[slot, rows(a, pc)], send_sem=send_sems.at[a, k],
                recv_sem=recv_sems.at[a, k], device_id=to, device_id_type=MESH_ID)

        local = [pltpu.make_async_copy(srcs[a], dsts[a].at[me], local_sems.at[a]) for a in range(n)]
        for cp in local:
            cp.start()
        sends = [copy(a, j, srcs[a].at[rows(a, c)], me, c, (px, py, c)) for a in range(n) for j, (px, py) in enumerate(chips)]
        for cp in sends:
            cp.start()
        passed = []
        for a in range(n):
            for j, (px, py) in enumerate(chips):
                k = 2 * px + py
                copy(a, j, srcs[a].at[rows(a, c)], k, c, (x, y, c)).wait_recv()
                fwd = copy(a, 3 + j, dsts[a].at[k, rows(a, c)], k, c, sibling)
                fwd.start()
                passed.append(fwd)
        for a in range(n):
            for j, (px, py) in enumerate(chips):
                copy(a, 3 + j, srcs[a].at[rows(a, c)], 2 * px + py, 1 - c, (x, y, c)).wait_recv()
        for cp in sends + passed:
            cp.wait_send()
        for cp in local:
            cp.wait()

    hbm = pl.BlockSpec(memory_space=pltpu.HBM)
    return pl.pallas_call(
        body, name="gather_weights",
        out_shape=tuple(jax.ShapeDtypeStruct((4,) + a.shape, a.dtype) for a in shards),
        in_specs=[hbm] * n, out_specs=tuple([hbm] * n),
        scratch_shapes=[pltpu.SemaphoreType.DMA((n, 6)), pltpu.SemaphoreType.DMA((n, 6)), pltpu.SemaphoreType.DMA((n,))],
    )(*shards)


def _exchange(name, slabs):
    n = len(slabs)

    def body(*refs):
        srcs, dsts = refs[:n], refs[n:2 * n]
        send_sems, recv_sems, local_sems = refs[2 * n:]
        x, y, c = lax.axis_index("x"), lax.axis_index("y"), lax.axis_index("c")
        chip = 2 * x + y
        sibling = (x, y, 1 - c)
        chips = [(1 - x, y), (x, 1 - y), (1 - x, 1 - y)]

        def slot(px, py, pc):
            return 4 * px + 2 * py + pc

        def copy(a, k, src, s, to):
            return pltpu.make_async_remote_copy(
                src_ref=src, dst_ref=dsts[a].at[s], send_sem=send_sems.at[a, k], recv_sem=recv_sems.at[a, k],
                device_id=to, device_id_type=MESH_ID)

        mine = slot(x, y, c)
        local = [pltpu.make_async_copy(srcs[a].at[chip], dsts[a].at[mine], local_sems.at[a]) for a in range(n)]
        for cp in local:
            cp.start()
        first = []
        for a in range(n):
            first.append(copy(a, 0, srcs[a].at[chip], mine, sibling))
            for j, (px, py) in enumerate(chips):
                first.append(copy(a, 1 + j, srcs[a].at[2 * px + py], mine, (px, py, c)))
        for cp in first:
            cp.start()
        passed = []
        for a in range(n):
            for j, (px, py) in enumerate(chips):
                s = slot(px, py, c)
                copy(a, 1 + j, srcs[a].at[chip], s, (x, y, c)).wait_recv()
                fwd = copy(a, 4 + j, dsts[a].at[s], s, sibling)
                fwd.start()
                passed.append(fwd)
        for a in range(n):
            copy(a, 0, srcs[a].at[chip], slot(x, y, 1 - c), (x, y, c)).wait_recv()
            for j, (px, py) in enumerate(chips):
                copy(a, 4 + j, srcs[a].at[chip], slot(px, py, 1 - c), (x, y, c)).wait_recv()
        for cp in first + passed:
            cp.wait_send()
        for cp in local:
            cp.wait()

    hbm = pl.BlockSpec(memory_space=pltpu.HBM)
    return pl.pallas_call(
        body, name=name,
        out_shape=tuple(jax.ShapeDtypeStruct((8,) + a.shape[1:], a.dtype) for a in slabs),
        in_specs=[hbm] * n, out_specs=tuple([hbm] * n),
        scratch_shapes=[pltpu.SemaphoreType.DMA((n, 7)), pltpu.SemaphoreType.DMA((n, 7)), pltpu.SemaphoreType.DMA((n,))],
    )(*slabs)


def _ada(call, w_shard):
    def body(c_ref, w_ref, o_ref):
        o_ref[...] = _mm(c_ref[...].astype(BF16), w_ref[...].astype(BF16))

    return pl.pallas_call(body, name="ada", out_shape=jax.ShapeDtypeStruct((8, w_shard.shape[1]), F32),
                          compiler_params=_cp())(call, w_shard)


def _inproj(x, mod, win, qg, kvg, wq, wk, wv, pos, invf):
    S = x.shape[0]
    tm = min(TM, S)

    def body(x_ref, mod_ref, win_ref, qg_ref, kvg_ref, wq_ref, wk_ref, wv_ref, pos_ref, invf_ref,
             qlat_ref, ckv_ref, za_ref, xbc_ref, zs_ref, dt_ref, q_ref, k_ref, v_ref):
        shift = mod_ref[0:1, 0:D]
        scale = mod_ref[0:1, D:2 * D]
        u = (x_ref[...] * (1.0 + scale) + shift).astype(BF16)

        def proj(p):
            return _mm(u, win_ref[:, p[0]:p[1]])

        ql = proj(P_Q)
        ckv = proj(P_KV)
        qlat_ref[...] = ql
        ckv_ref[...] = ckv
        za_ref[...] = proj(P_ZA)
        xbc_ref[...] = proj(P_XBC)
        zs_ref[...] = proj(P_ZS)
        dt_ref[...] = proj(P_DT)
        kr = proj(P_KR)

        ang = pos_ref[...].astype(F32) * invf_ref[...]
        cs = jnp.cos(ang)
        sn = jnp.sin(ang)

        rq = lax.rsqrt(jnp.mean(ql * ql, axis=-1, keepdims=True) + RMS_EPS)
        qn = (ql * rq * qg_ref[...]).astype(BF16)
        for h in range(NH):
            qh = _mm(qn, wq_ref[:, h * HP:(h + 1) * HP])
            q_ref[h, :, 0:NOPE] = (qh[:, 0:NOPE] * QSCALE).astype(BF16)
            q_ref[h, :, NOPE:HP] = (_rope(qh[:, NOPE:HP], cs, sn) * QSCALE).astype(BF16)

        rk = lax.rsqrt(jnp.mean(ckv * ckv, axis=-1, keepdims=True) + RMS_EPS)
        kn = (ckv * rk * kvg_ref[...]).astype(BF16)
        knope = _mm(kn, wk_ref[...])
        vall = _mm(kn, wv_ref[...])
        krf = _rope(kr, cs, sn)
        krr = krf.astype(BF16)
        for h in range(NH):
            k_ref[h, :, 0:NOPE] = knope[:, h * NOPE:(h + 1) * NOPE].astype(BF16)
            k_ref[h, :, NOPE:HP] = krr
            v_ref[h] = vall[:, h * VD:(h + 1) * VD].astype(BF16)

    row = lambda n: pl.BlockSpec((tm, n), lambda i: (i, 0))
    heads = lambda n: pl.BlockSpec((NH, tm, n), lambda i: (0, i, 0))
    sd = lambda n: jax.ShapeDtypeStruct((S, n), F32)
    hd = lambda n: jax.ShapeDtypeStruct((NH, S, n), BF16)
    return pl.pallas_call(
        body, name="inproj", grid=(S // tm,),
        in_specs=[row(D), _const((8, 3 * D)), _const((D, IN_P)), _const((1, QL)), _const((1, KVL)),
                  _const((QL, NH * HP)), _const((KVL, NH * NOPE)), _const((KVL, NH * VD)),
                  row(1), _const((1, 128))],
        out_specs=[row(QL), row(KVL), row(D), row(CC), row(D), row(128), heads(HP), heads(HP), heads(VD)],
        out_shape=[sd(QL), sd(KVL), sd(D), sd(CC), sd(D), sd(128), hd(HP), hd(HP), hd(VD)],
        compiler_params=_cp(("arbitrary",)),
    )(x, mod, win, qg, kvg, wq, wk, wv, pos, invf)


def _attn_fwd(q, k, v):
    _, S, _ = q.shape
    tq = min(TQF, S)
    nq = S // tq
    half = tq // NSP
    tb = min(256, half)
    nsb = half // tb

    def body(q_ref, k_ref, v_ref, o_ref, lse_ref, vt_ref):
        i = pl.program_id(1)
        qb = q_ref[...]

        @pl.when(i == 0)
        def _():
            ones_rows = jnp.where(lax.broadcasted_iota(jnp.int32, (VDP - VD, tb), 0) == 0, 1.0, 0.0).astype(BF16)

            def fill(blk, carry):
                off = pl.multiple_of(blk * tb, tb)
                vt_ref[blk, 0:VD, :] = v_ref[pl.ds(off, tb), :].T
                vt_ref[blk, VD:VDP, :] = ones_rows
                return carry

            lax.fori_loop(0, S // tb, fill, 0)

        def scores(j, hb):
            off = pl.multiple_of(j * tq + hb * half, half)
            return _nt(k_ref[pl.ds(off, half), :], qb)

        def update(j, hb, s, carry):
            m, acc = carry
            m_new = jnp.maximum(m, jnp.max(s, axis=0, keepdims=True))
            a = jnp.exp2(m - m_new)
            pb = jnp.exp2(s - m_new).astype(BF16)
            acc = a * acc
            for sb in range(nsb):
                acc = acc + _mm(vt_ref[(NSP * j + hb) * nsb + sb], pb[sb * tb:(sb + 1) * tb, :])
            return m_new, acc

        def trip(j, carry, masked):
            s = [scores(j, hb) for hb in range(NSP)]
            if masked:
                r = lax.broadcasted_iota(jnp.int32, s[0].shape, 0)
                cidx = lax.broadcasted_iota(jnp.int32, s[0].shape, 1)
                s = [jnp.where(cidx >= r + hb * half, s[hb], -1e30) for hb in range(NSP)]
            for hb in range(NSP):
                carry = update(j, hb, s[hb], carry)
            return carry

        def finish(carry):
            m, acc = carry
            l = acc[VD:VD + 1, :]
            o_ref[...] = (acc[0:VD, :] / l).T
            lse_ref[...] = m + jnp.log2(l)

        def quad(t, cr):
            for u in range(4):
                cr = trip(4 * t + u, cr, False)
            return cr

        init = (jnp.full((1, tq), -1e30, F32), jnp.zeros((VDP, tq), F32))
        carry = lax.fori_loop(0, i >> 2, quad, init)
        carry = lax.fori_loop(i - (i & 3), i, lambda j, cr: trip(j, cr, False), carry)
        finish(trip(i, carry, True))

    return pl.pallas_call(
        body, name="attn_fwd", grid=(NH, nq),
        in_specs=[pl.BlockSpec((None, tq, HP), lambda h, i: (h, i, 0)),
                  pl.BlockSpec((None, S, HP), lambda h, i: (h, 0, 0)),
                  pl.BlockSpec((None, S, VD), lambda h, i: (h, 0, 0))],
        out_specs=[pl.BlockSpec((tq, VD), lambda h, i: (i, h)),
                   pl.BlockSpec((None, None, 1, tq), lambda h, i: (h, i, 0, 0))],
        out_shape=[jax.ShapeDtypeStruct((S, NH * VD), F32), jax.ShapeDtypeStruct((NH, nq, 1, tq), F32)],
        scratch_shapes=[pltpu.VMEM((S // tb, VDP, tb), BF16)],
        compiler_params=_cp(("arbitrary", "arbitrary")),
    )(q, k, v)


def _ssd_consts():
    tri = np.tril(np.ones((CH, CH), np.float32))
    e16 = np.zeros((128, SW), np.float32)
    for h in range(SH):
        e16[h, h * SP:(h + 1) * SP] = 1.0
    return jnp.asarray(tri, BF16), jnp.asarray(tri.T.copy(), BF16), jnp.asarray(e16, BF16)


def _ssd_conv(xraw, halo, cw_ref, cb_ref, ext):
    ext[0:8, :] = halo
    ext[8:8 + CH, :] = xraw
    cw = cw_ref[...]
    xc = cb_ref[...] + cw[0:1, :] * ext[5:5 + CH, :]
    for kk in range(1, CW):
        xc = xc + cw[kk:kk + 1, :] * ext[5 + kk:5 + kk + CH, :]
    return xc


def _ssd_chunk_common(xc, dtraw_ref, dtb_ref, alog_ref, tri_ref, e16_ref):
    sact = _sigmoid(xc)
    act = xc * sact
    lane = lax.broadcasted_iota(jnp.int32, (1, 128), 1)
    arow = jnp.where(lane < SH, -jnp.exp(alog_ref[...]), 0.0)
    dtpre = dtraw_ref[...] + dtb_ref[...]
    dt = _softplus(dtpre)
    a = dt * arow
    cum = _xmm(tri_ref[...], a, 3)
    cumx = _mm_x(cum, e16_ref[...], 3)
    dtx = _mm_x(dt, e16_ref[...], 2)
    return sact, act, arow, dtpre, dt, cum, cumx, dtx


def _ssd_fwd(xbc, dtraw, zs, conv_w, conv_b, dtb, alog, dskx, gssm):
    S = xbc.shape[0]
    nc = S // CH
    tri, _, e16 = _ssd_consts()

    def body(xbc_ref, halo_ref, dtraw_ref, zs_ref, cw_ref, cb_ref, dtb_ref, alog_ref, dsk_ref, g_ref, tri_ref, e16_ref,
             xc_ref, y_ref, htp_ref, o_ref, ht, ext):
        i = pl.program_id(0)

        @pl.when(i == 0)
        def _():
            ht[...] = jnp.zeros_like(ht)

        for c in range(NCK):
            chunk(c, i, xbc_ref, halo_ref, dtraw_ref, zs_ref, cw_ref, cb_ref, dtb_ref, alog_ref, dsk_ref, g_ref, tri_ref,
                  e16_ref, xc_ref, y_ref, htp_ref, o_ref, ht, ext)

    def chunk(c, i, xbc_ref, halo_ref, dtraw_ref, zs_ref, cw_ref, cb_ref, dtb_ref, alog_ref, dsk_ref, g_ref, tri_ref, e16_ref,
              xc_ref, y_ref, htp_ref, o_ref, ht, ext):
        rows = slice(c * CH, (c + 1) * CH)
        halo = jnp.where(i == 0, 0.0, halo_ref[...]) if c == 0 else xbc_ref[c * CH - 8:c * CH, :]
        xc = _ssd_conv(xbc_ref[rows, :], halo, cw_ref, cb_ref, ext)
        xc_ref[rows, :] = xc
        sact, act, arow, dtpre, dt, cum, cumx, dtx = _ssd_chunk_common(xc, dtraw_ref.at[rows, :], dtb_ref, alog_ref, tri_ref, e16_ref)
        cum_t = cum.T
        xs = act[:, 0:SW]
        lastx = cumx[CH - 1:CH, :]
        xh = xs * dtx
        eexp = jnp.exp(cumx)
        dte = jnp.exp(lastx - cumx)
        cdx = jnp.exp(lastx)
        htp = ht[...]
        htp_ref[c] = htp
        xw = (xh * dte).astype(BF16)
        xb = xh.astype(BF16)
        trim = tri_ref[...].astype(F32) > 0.5
        lane = lax.broadcasted_iota(jnp.int32, (CH, 128), 1)
        parts = []
        for g in range(SG):
            gl = slice(g * GW, (g + 1) * GW)
            bg = act[:, SW + g * SN:SW + (g + 1) * SN].astype(BF16)
            cg = act[:, SW + SG * SN + g * SN:SW + SG * SN + (g + 1) * SN].astype(BF16)
            cbm = _nt(cg, bg)
            yoff = eexp[:, gl] * _mm(cg, htp[:, gl].astype(BF16))
            ht[:, gl] = htp[:, gl] * cdx[:, gl] + _tn(bg, xw[:, gl])
            for pr in range(GW // 128):
                h0 = g * (SH // SG) + 2 * pr
                lo = g * GW + pr * 128
                xp = xb[:, lo:lo + 128]
                res = []
                for hh in (h0, h0 + 1):
                    seg = cum[:, hh:hh + 1] - cum_t[hh:hh + 1, :]
                    mh = jnp.where(trim, cbm * jnp.exp(seg), 0.0).astype(BF16)
                    res.append(_mm(mh, xp))
                parts.append(jnp.where(lane < SP, res[0], res[1]) + yoff[:, pr * 128:(pr + 1) * 128])
        y = jnp.concatenate(parts, axis=1) + xs * dsk_ref[...]
        y_ref[rows, :] = y
        z = zs_ref[rows, :]
        hf = y * (z * _sigmoid(z))
        outs = []
        for g in range(SG):
            hg = hf[:, g * GW:(g + 1) * GW]
            rs = lax.rsqrt(jnp.mean(hg * hg, axis=-1, keepdims=True) + RMS_EPS)
            outs.append(hg * rs)
        o_ref[rows, :] = (jnp.concatenate(outs, axis=1) * g_ref[...]).astype(BF16)

    row = lambda n: pl.BlockSpec((NCK * CH, n), lambda i: (i, 0))
    return pl.pallas_call(
        body, name="ssd_fwd", grid=(nc // NCK,),
        in_specs=[row(CC), pl.BlockSpec((8, CC), lambda i: (jnp.maximum(i * (NCK * CH // 8) - 1, 0), 0)), row(128), row(SW),
                  _const((CW, CC)), _const((1, CC)), _const((1, 128)), _const((1, 128)), _const((1, SW)), _const((1, SW)),
                  _const((CH, CH)), _const((128, SW))],
        out_specs=[row(CC), row(SW), pl.BlockSpec((NCK, SN, SW), lambda i: (i, 0, 0)), row(SW)],
        out_shape=[jax.ShapeDtypeStruct((S, CC), F32), jax.ShapeDtypeStruct((S, SW), F32),
                   jax.ShapeDtypeStruct((nc, SN, SW), F32), jax.ShapeDtypeStruct((S, SW), BF16)],
        scratch_shapes=[pltpu.VMEM((SN, SW), F32), pltpu.VMEM((8 + CH, CC), F32)],
        compiler_params=_cp(("arbitrary",)),
    )(xbc, xbc, dtraw, zs, conv_w, conv_b, dtb, alog, dskx, gssm, tri, e16)


def _outproj(o, za, ossm, x, tgt, wout, mod, ln_g, ln_b):
    S = x.shape[0]
    tm = min(TM, S)

    e8 = np.zeros((D, 128), np.float32)
    for h in range(NH):
        e8[h * VD:(h + 1) * VD, h] = 1.0
    e8 = jnp.asarray(e8, BF16)

    def body(o_ref, za_ref, os_ref, x_ref, t_ref, w_ref, mod_ref, g_ref, b_ref, e8_ref,
             gx_ref, do_ref, dza_ref, dos_ref, delta_ref, dw_ref, vec_ref):
        i = pl.program_id(0)

        @pl.when(i == 0)
        def _():
            dw_ref[...] = jnp.zeros_like(dw_ref)
            vec_ref[...] = jnp.zeros_like(vec_ref)

        gate = mod_ref[0:1, 2 * D:3 * D]
        ov = o_ref[...]
        z = za_ref[...]
        sz = _sigmoid(z)
        silz = z * sz
        a = (ov * silz).astype(BF16)
        osb = os_ref[...]
        mixed = _mm(a, w_ref[0:D, :]) + _mm(osb, w_ref[D:MIX, :])
        xv = x_ref[...]
        hres = ALPHA * xv + gate * mixed
        mu = jnp.mean(hres, axis=-1, keepdims=True)
        hc = hres - mu
        var = jnp.mean(hc * hc, axis=-1, keepdims=True)
        rstd = lax.rsqrt(var + LN_EPS)
        xhat = hc * rstd
        g = g_ref[...]
        yv = xhat * g + b_ref[...]
        err = yv - t_ref[...]
        dy = err * (1.0 / D)
        vec_ref[0:1, :] += jnp.sum(err * err, axis=0, keepdims=True)
        vec_ref[1:2, :] += jnp.sum(dy * xhat, axis=0, keepdims=True)
        vec_ref[2:3, :] += jnp.sum(dy, axis=0, keepdims=True)
        dxh = dy * g
        dh = rstd * (dxh - jnp.mean(dxh, axis=-1, keepdims=True) - xhat * jnp.mean(dxh * xhat, axis=-1, keepdims=True))
        gx_ref[...] = ALPHA * dh
        vec_ref[3:4, :] += jnp.sum(dh * mixed, axis=0, keepdims=True)
        dmixed = (gate * dh).astype(BF16)
        dw_ref[0:D, :] += _tn(a, dmixed)
        dw_ref[D:MIX, :] += _tn(osb, dmixed)
        da = _nt(dmixed, w_ref[0:D, :])
        dos_ref[...] = _nt(dmixed, w_ref[D:MIX, :])
        dov = da * silz
        do_ref[...] = dov.astype(BF16)
        dza_ref[...] = (da * ov * (sz * (1.0 + z * (1.0 - sz)))).astype(BF16)
        delta_ref[:, 0, :] = _mm_x(dov * ov, e8_ref[...], 2).T[0:NH, :]

    row = lambda n: pl.BlockSpec((tm, n), lambda i: (i, 0))
    return pl.pallas_call(
        body, name="outproj", grid=(S // tm,),
        in_specs=[row(D), row(D), row(D), row(D), row(D), _const((MIX, D)), _const((8, 3 * D)), _const((1, D)), _const((1, D)),
                  _const((D, 128))],
        out_specs=[row(D), row(D), row(D), row(D), pl.BlockSpec((NH, None, 1, tm), lambda i: (0, i, 0, 0)),
                   _full((MIX, D)), _full((8, D))],
        out_shape=[jax.ShapeDtypeStruct((S, D), F32), jax.ShapeDtypeStruct((S, D), BF16), jax.ShapeDtypeStruct((S, D), BF16),
                   jax.ShapeDtypeStruct((S, D), F32), jax.ShapeDtypeStruct((NH, S // tm, 1, tm), F32),
                   jax.ShapeDtypeStruct((MIX, D), F32), jax.ShapeDtypeStruct((8, D), F32)],
        compiler_params=_cp(("arbitrary",)),
    )(o, za, ossm, x, tgt, wout, mod, ln_g, ln_b, e8)


def _attn_bwd(q, k, v, do, lse, delta):
    _, S, _ = q.shape
    tk = min(TQ, S // 2)
    nk = S // tk
    tq = 2 * tk
    nq = S // tq

    def body(k_ref, v_ref, q_ref, do_ref, lse_ref, dl_ref, dk_ref, dv_ref, dq_ref, dqt_ref):
        j = pl.program_id(1)
        kb = k_ref[...]
        ktb = kb.T
        vb = v_ref[...]

        @pl.when(j == 0)
        def _():
            dqt_ref[...] = jnp.zeros_like(dqt_ref)

        dk_ref[...] = jnp.zeros_like(dk_ref)
        dv_ref[...] = jnp.zeros_like(dv_ref)

        def step(i, masked, lo=0):
            off = pl.multiple_of(i * tq + lo, tk)
            qb = q_ref[pl.ds(off, tq - lo), :]
            dob = do_ref[pl.ds(off, tq - lo), :]
            pt = jnp.exp2(_nt(kb, qb) - lse_ref[i][:, lo:tq])
            if masked:
                r = lax.broadcasted_iota(jnp.int32, pt.shape, 0)
                cidx = lax.broadcasted_iota(jnp.int32, pt.shape, 1)
                pt = jnp.where(i * tq + lo + cidx >= j * tk + r, pt, 0.0)
            dv_ref[...] += _mm(pt.astype(BF16), dob)
            dsb = (pt * (_nt(vb, dob) - dl_ref[i][:, lo:tq])).astype(BF16)
            dk_ref[...] += _mm(dsb, qb)
            dqt_ref[i, :, lo:tq] += _mm(ktb, dsb)

        first = j >> 1

        @pl.when((j & 1) == 0)
        def _():
            step(first, True)

        @pl.when((j & 1) == 1)
        def _():
            step(first, True, tk)
            dq_ref[...] = dqt_ref[first].T

        def loop_body(t, carry):
            for u in range(4):
                step(first + 1 + 4 * t + u, False)
            return carry

        def tail_body(i, carry):
            step(i, False)
            return carry

        rest = nq - 1 - first
        lax.fori_loop(0, rest >> 2, loop_body, 0)
        lax.fori_loop(nq - (rest & 3), nq, tail_body, 0)

        dk_ref[...] = dk_ref[...] * LN2

    return pl.pallas_call(
        body, name="attn_bwd", grid=(NH, nk),
        in_specs=[pl.BlockSpec((None, tk, HP), lambda h, j: (h, j, 0)),
                  pl.BlockSpec((None, tk, VD), lambda h, j: (h, j, 0)),
                  pl.BlockSpec((None, S, HP), lambda h, j: (h, 0, 0)),
                  pl.BlockSpec((S, VD), lambda h, j: (0, h)),
                  pl.BlockSpec((None, nq, 1, tq), lambda h, j: (h, 0, 0, 0)),
                  pl.BlockSpec((None, nq, 1, tq), lambda h, j: (h, 0, 0, 0))],
        out_specs=[pl.BlockSpec((None, tk, HP), lambda h, j: (h, j, 0)),
                   pl.BlockSpec((None, tk, VD), lambda h, j: (h, j, 0)),
                   pl.BlockSpec((None, tq, HP), lambda h, j: (h, j >> 1, 0))],
        out_shape=[jax.ShapeDtypeStruct((NH, S, HP), F32), jax.ShapeDtypeStruct((NH, S, VD), F32),
                   jax.ShapeDtypeStruct((NH, S, HP), F32)],
        scratch_shapes=[pltpu.VMEM((nq, HP, tq), F32)],
        compiler_params=_cp(("arbitrary", "arbitrary")),
    )(k, v, q, do, lse.reshape(NH, nq, 1, tq), delta.reshape(NH, nq, 1, tq))


def _ssd_bwd(xbc, xc, dtraw, zs, y, htp, dossm, conv_w, dtb, alog, dskx, gssm):
    S = xbc.shape[0]
    nc = S // CH
    tri, triu, e16 = _ssd_consts()

    def body(xbc_ref, xc_ref, dtraw_ref, zs_ref, y_ref, htp_ref, dos_ref,
             cw_ref, dtb_ref, alog_ref, dsk_ref, g_ref, tri_ref, triu_ref, e16_ref,
             dxbc_ref, ddt_ref, dzs_ref, dcw_ref, dcb_ref, dvec_ref, dg_ref,
             dht, dext, dskacc):
        r = pl.program_id(0)

        @pl.when(r == 0)
        def _():
            dht[...] = jnp.zeros_like(dht)
            dext[CH:CH + 8, :] = jnp.zeros((8, CC), F32)
            dskacc[...] = jnp.zeros_like(dskacc)
            dcw_ref[...] = jnp.zeros_like(dcw_ref)
            dcb_ref[...] = jnp.zeros_like(dcb_ref)
            dvec_ref[...] = jnp.zeros_like(dvec_ref)
            dg_ref[...] = jnp.zeros_like(dg_ref)

        xc = xc_ref[...]
        sact, act, arow, dtpre, dt, cum, cumx, dtx = _ssd_chunk_common(xc, dtraw_ref, dtb_ref, alog_ref, tri_ref, e16_ref)
        cum_t = cum.T
        xs = act[:, 0:SW]
        lastx = cumx[CH - 1:CH, :]
        xh = xs * dtx
        eexp = jnp.exp(cumx)
        dte = jnp.exp(lastx - cumx)
        cdx = jnp.exp(lastx)
        trim = tri_ref[...].astype(F32) > 0.5
        lane = lax.broadcasted_iota(jnp.int32, (CH, 128), 1)
        rowi = lax.broadcasted_iota(jnp.int32, (CH, 128), 0)

        yv = y_ref[...]
        z = zs_ref[...]
        sz = _sigmoid(z)
        silz = z * sz
        hf = yv * silz
        dn = dos_ref[...] * g_ref[...]
        dhf_parts, nrm_parts = [], []
        for g in range(SG):
            gl = slice(g * GW, (g + 1) * GW)
            hg = hf[:, gl]
            rs = lax.rsqrt(jnp.mean(hg * hg, axis=-1, keepdims=True) + RMS_EPS)
            ng = hg * rs
            dng = dn[:, gl]
            dhf_parts.append(rs * (dng - ng * jnp.mean(dng * ng, axis=-1, keepdims=True)))
            nrm_parts.append(ng)
        nrm = jnp.concatenate(nrm_parts, axis=1)
        dhf = jnp.concatenate(dhf_parts, axis=1)
        dg_ref[...] += jnp.sum(dos_ref[...] * nrm, axis=0, keepdims=True)
        dyv = dhf * silz
        dzs_ref[...] = (dhf * yv * (sz * (1.0 + z * (1.0 - sz)))).astype(BF16)
        dskacc[...] += jnp.sum(dyv * xs, axis=0, keepdims=True)
        dxs_skip = dyv * dsk_ref[...]

        dhtn = dht[...]
        hp = htp_ref[...]
        dlastx = jnp.sum(dhtn * hp, axis=0, keepdims=True) * cdx
        xb = xh.astype(BF16)
        xwf = xh * dte
        dcum = jnp.zeros((CH, 128), F32)
        dcum_t = jnp.zeros((128, CH), F32)
        dxh_parts, dcumx_parts, dlast_parts, db_parts, dc_parts = [], [], [], [], []
        for g in range(SG):
            gl = slice(g * GW, (g + 1) * GW)
            bg = act[:, SW + g * SN:SW + (g + 1) * SN].astype(BF16)
            cg = act[:, SW + SG * SN + g * SN:SW + SG * SN + (g + 1) * SN].astype(BF16)
            hpg = hp[:, gl].astype(BF16)
            dhn = dhtn[:, gl].astype(BF16)
            dyg = dyv[:, gl]
            dz = (dyg * eexp[:, gl]).astype(BF16)
            dcg = _nt(dz, hpg)
            dht[:, gl] = dhtn[:, gl] * cdx[:, gl] + _tn(cg, dz)
            yoff = eexp[:, gl] * _mm(cg, hpg)
            dcumx_g = dyg * yoff
            dbg = _nt(xwf[:, gl].astype(BF16), dhn)
            dxw = _mm(bg, dhn)
            ddte = dxw * xwf[:, gl]
            dcumx_parts.append(dcumx_g - ddte)
            dlast_parts.append(jnp.sum(ddte, axis=0, keepdims=True))
            dxh_g = dxw * dte[:, gl]
            cbm = _nt(cg, bg)
            dcb = jnp.zeros((CH, CH), F32)
            dxp_parts = []
            for pr in range(GW // 128):
                h0 = g * (SH // SG) + 2 * pr
                lo = g * GW + pr * 128
                xp = xb[:, lo:lo + 128]
                dyp = dyv[:, lo:lo + 128]
                dxp = jnp.zeros((CH, 128), F32)
                for idx, hh in enumerate((h0, h0 + 1)):
                    decay = jnp.where(trim, jnp.exp(cum[:, hh:hh + 1] - cum_t[hh:hh + 1, :]), 0.0)
                    mh = cbm * decay
                    keep = (lane < SP) if idx == 0 else (lane >= SP)
                    dym = jnp.where(keep, dyp, 0.0).astype(BF16)
                    dm = _nt(dym, xp)
                    dxp = dxp + _tn(mh.astype(BF16), dym)
                    gm = dm * mh
                    dcum = dcum + jnp.where(lane == hh, jnp.sum(gm, axis=1, keepdims=True), 0.0)
                    dcum_t = dcum_t - jnp.where(rowi == hh, jnp.sum(gm, axis=0, keepdims=True), 0.0)
                    dcb = dcb + dm * decay
                dxp_parts.append(dxp)
            dxh_parts.append(dxh_g + jnp.concatenate(dxp_parts, axis=1))
            dcbb = dcb.astype(BF16)
            dc_parts.append(dcg + _mm(dcbb, bg))
            db_parts.append(dbg + _tn(dcbb, cg))
        dxh = jnp.concatenate(dxh_parts, axis=1)
        dcumx = jnp.concatenate(dcumx_parts, axis=1)
        dlastx = dlastx + jnp.concatenate(dlast_parts, axis=1)
        e16 = e16_ref[...]
        dlast128 = _nt_x(jnp.broadcast_to(dlastx, (8, SW)), e16, 2)[0:1, :]
        dcum = dcum + dcum_t.T + _nt_x(dcumx, e16, 2) + jnp.where(rowi == CH - 1, dlast128, 0.0)
        da = _xmm(triu_ref[...], dcum, 2)
        ddt = da * arow + _nt_x(dxh * xs, e16, 2)
        dvec_ref[1:2, :] += jnp.sum(da * dt, axis=0, keepdims=True)
        ddtraw = jnp.where(lane < SH, ddt * _sigmoid(dtpre), 0.0)
        dvec_ref[0:1, :] += jnp.sum(ddtraw, axis=0, keepdims=True)
        ddt_ref[...] = ddtraw.astype(BF16)
        dxs = dxs_skip + dxh * dtx
        dact = jnp.concatenate([dxs] + db_parts + dc_parts, axis=1)
        dxc = dact * (sact * (1.0 + xc * (1.0 - sact)))

        dcb_ref[...] += jnp.sum(dxc, axis=0, keepdims=True)
        dext[0:CH, :] = dxc
        cw = cw_ref[...]
        xraw = xbc_ref[...]
        dxr = cw[CW - 1:CW, :] * dxc
        dcw_ref[CW - 1:CW, :] += jnp.sum(dxc * xraw, axis=0, keepdims=True)
        for kk in range(CW - 1):
            dwin = dext[CW - 1 - kk:CW - 1 - kk + CH, :]
            dxr = dxr + cw[kk:kk + 1, :] * dwin
            dcw_ref[kk:kk + 1, :] += jnp.sum(dwin * xraw, axis=0, keepdims=True)
        dxbc_ref[...] = dxr.astype(BF16)
        dext[CH:CH + 8, :] = dxc[0:8, :]

        @pl.when(r == nc - 1)
        def _():
            dvec_ref[1:2, :] = dvec_ref[1:2, :] * arow
            dvec_ref[2:3, :] = _nt_x(jnp.broadcast_to(dskacc[...], (8, SW)), e16, 3)[0:1, :]

    rev = lambda n: pl.BlockSpec((CH, n), lambda r: (nc - 1 - r, 0))
    return pl.pallas_call(
        body, name="ssd_bwd", grid=(nc,),
        in_specs=[rev(CC), rev(CC),
                  rev(128), rev(SW), rev(SW), pl.BlockSpec((None, SN, SW), lambda r: (nc - 1 - r, 0, 0)), rev(SW),
                  _const((CW, CC)), _const((1, 128)), _const((1, 128)), _const((1, SW)), _const((1, SW)),
                  _const((CH, CH)), _const((CH, CH)), _const((128, SW))],
        out_specs=[rev(CC), rev(128), rev(SW), _full((CW, CC)), _full((1, CC)), _full((8, 128)), _full((1, SW))],
        out_shape=[jax.ShapeDtypeStruct((S, CC), BF16), jax.ShapeDtypeStruct((S, 128), BF16), jax.ShapeDtypeStruct((S, SW), BF16),
                   jax.ShapeDtypeStruct((CW, CC), F32), jax.ShapeDtypeStruct((1, CC), F32),
                   jax.ShapeDtypeStruct((8, 128), F32), jax.ShapeDtypeStruct((1, SW), F32)],
        scratch_shapes=[pltpu.VMEM((SN, SW), F32), pltpu.VMEM((CH + 8, CC), F32), pltpu.VMEM((1, SW), F32)],
        compiler_params=_cp(("arbitrary",)),
    )(xbc, xc, dtraw, zs, y, htp, dossm, conv_w, dtb, alog, dskx, gssm, tri, triu, e16)


def _mla_bwd(dq, dk, dv, qlat, ckv, qg, kvg, wq, wk, wv, pos, invf):
    S = qlat.shape[0]
    tm = min(TQ, S)

    def body(dq_ref, dk_ref, dv_ref, ql_ref, ckv_ref, qg_ref, kvg_ref, wq_ref, wk_ref, wv_ref, pos_ref, invf_ref,
             dql_ref, dckv_ref, dkr_ref, dwq_ref, dwk_ref, dwv_ref, dqg_ref, dkvg_ref):
        i = pl.program_id(0)

        @pl.when(i == 0)
        def _():
            dwq_ref[...] = jnp.zeros_like(dwq_ref)
            dwk_ref[...] = jnp.zeros_like(dwk_ref)
            dwv_ref[...] = jnp.zeros_like(dwv_ref)
            dqg_ref[...] = jnp.zeros_like(dqg_ref)
            dkvg_ref[...] = jnp.zeros_like(dkvg_ref)

        ang = pos_ref[...].astype(F32) * invf_ref[...]
        cs = jnp.cos(ang)
        sn = jnp.sin(ang)

        def rms_bwd(v, g, dn, dg_ref):
            r = lax.rsqrt(jnp.mean(v * v, axis=-1, keepdims=True) + RMS_EPS)
            vh = v * r
            dg_ref[...] += jnp.sum(dn * vh, axis=0, keepdims=True)
            dvh = dn * g
            return vh, r * (dvh - vh * jnp.mean(dvh * vh, axis=-1, keepdims=True))

        pieces = []
        for h in range(NH):
            dqh = dq_ref[h]
            pieces.append(dqh[:, 0:NOPE] * SCALE)
            pieces.append(_rope_t(dqh[:, NOPE:HP], cs, sn) * SCALE)
        dqf = jnp.concatenate(pieces, axis=1).astype(BF16)
        ql = ql_ref[...]
        g = qg_ref[...]
        dqn = _nt(dqf, wq_ref[...])
        qh, dql = rms_bwd(ql, g, dqn, dqg_ref)
        dwq_ref[...] += _tn((qh * g).astype(BF16), dqf)
        dql_ref[...] = dql.astype(BF16)

        dkn_p = jnp.concatenate([dk_ref[h, :, 0:NOPE] for h in range(NH)], axis=1).astype(BF16)
        dvf = jnp.concatenate([dv_ref[h] for h in range(NH)], axis=1).astype(BF16)
        dkr = dk_ref[0, :, NOPE:HP]
        for h in range(1, NH):
            dkr = dkr + dk_ref[h, :, NOPE:HP]
        lane = lax.broadcasted_iota(jnp.int32, dkr.shape, 1)
        dkr_ref[...] = jnp.where(lane < ROPE, _rope_t(dkr, cs, sn), 0.0).astype(BF16)
        cv = ckv_ref[...]
        gk = kvg_ref[...]
        dkn = _nt(dkn_p, wk_ref[...]) + _nt(dvf, wv_ref[...])
        kh, dckv = rms_bwd(cv, gk, dkn, dkvg_ref)
        knb = (kh * gk).astype(BF16)
        dwk_ref[...] += _tn(knb, dkn_p)
        dwv_ref[...] += _tn(knb, dvf)
        dckv_ref[...] = dckv.astype(BF16)

    row = lambda n: pl.BlockSpec((tm, n), lambda i: (i, 0))
    heads = lambda n: pl.BlockSpec((NH, tm, n), lambda i: (0, i, 0))
    return pl.pallas_call(
        body, name="mla_bwd", grid=(S // tm,),
        in_specs=[heads(HP), heads(HP), heads(VD), row(QL), row(KVL), _const((1, QL)), _const((1, KVL)),
                  _const((QL, NH * HP)), _const((KVL, NH * NOPE)), _const((KVL, NH * VD)), row(1), _const((1, 128))],
        out_specs=[row(QL), row(KVL), row(128), _full((QL, NH * HP)), _full((KVL, NH * NOPE)), _full((KVL, NH * VD)),
                   _full((1, QL)), _full((1, KVL))],
        out_shape=[jax.ShapeDtypeStruct((S, QL), BF16), jax.ShapeDtypeStruct((S, KVL), BF16), jax.ShapeDtypeStruct((S, 128), BF16),
                   jax.ShapeDtypeStruct((QL, NH * HP), F32), jax.ShapeDtypeStruct((KVL, NH * NOPE), F32),
                   jax.ShapeDtypeStruct((KVL, NH * VD), F32), jax.ShapeDtypeStruct((1, QL), F32), jax.ShapeDtypeStruct((1, KVL), F32)],
        compiler_params=_cp(("arbitrary",)),
    )(dq, dk, dv, qlat, ckv, qg, kvg, wq, wk, wv, pos, invf)


def _inproj_bwd(x, gx1, mod, win, dql, dckv, dza, dxbc, dzs, dkr, ddt):
    S = x.shape[0]
    tm = min(TM, S)

    def body(x_ref, gx1_ref, mod_ref, win_ref, dql_ref, dckv_ref, dza_ref, dxbc_ref, dzs_ref, dkr_ref, ddt_ref,
             gx_ref, dw_ref, vec_ref):
        i = pl.program_id(0)

        @pl.when(i == 0)
        def _():
            dw_ref[...] = jnp.zeros_like(dw_ref)
            vec_ref[...] = jnp.zeros_like(vec_ref)

        shift = mod_ref[0:1, 0:D]
        scale = mod_ref[0:1, D:2 * D]
        xv = x_ref[...]
        ut = (xv * (1.0 + scale) + shift).T.astype(BF16)
        pieces = (dql_ref, dckv_ref, dza_ref, dxbc_ref, dzs_ref, dkr_ref, ddt_ref)
        du = jnp.zeros((tm, D), F32)
        lo = 0
        for p_ref in pieces:
            n = p_ref.shape[1]
            dp = p_ref[...]
            du = du + _nt(dp, win_ref[:, lo:lo + n])
            dw_ref[:, lo:lo + n] += _mm(ut, dp)
            lo += n
        vec_ref[0:1, :] += jnp.sum(du, axis=0, keepdims=True)
        vec_ref[1:2, :] += jnp.sum(du * xv, axis=0, keepdims=True)
        gx_ref[...] = gx1_ref[...] + du * (1.0 + scale)

    row = lambda n: pl.BlockSpec((tm, n), lambda i: (i, 0))
    return pl.pallas_call(
        body, name="inproj_bwd", grid=(S // tm,),
        in_specs=[row(D), row(D), _const((8, 3 * D)), _const((D, IN_P)), row(QL), row(KVL), row(D), row(CC), row(D),
                  row(128), row(128)],
        out_specs=[row(D), pl.BlockSpec((D, IN_P), lambda i: (0, 0), pipeline_mode=pl.Buffered(1)), _full((8, D))],
        out_shape=[jax.ShapeDtypeStruct((S, D), F32), jax.ShapeDtypeStruct((D, IN_P), F32), jax.ShapeDtypeStruct((8, D), F32)],
        compiler_params=_cp(("arbitrary",)),
    )(x, gx1, mod, win, dql, dckv, dza, dxbc, dzs, dkr, ddt)


def _ada_bwd(callt, dmods):
    w = dmods.shape[1]

    def body(c_ref, d_ref, o_ref):
        acc = c_ref[:, 0:1] * d_ref[0:1, :]
        for s in range(1, 8):
            acc = acc + c_ref[:, s:s + 1] * d_ref[s:s + 1, :]
        o_ref[0] = acc

    return pl.pallas_call(body, name="ada_bwd", out_shape=jax.ShapeDtypeStruct((1, D, w), F32),
                          compiler_params=_cp())(callt, dmods)


def _adamw(name, parts, w, m, v):
    rows, ncol = w.shape
    tr = min(rows, 128)
    nparts = parts.shape[0]

    def body(p_ref, w_ref, m_ref, v_ref, g_ref, d_ref, nm_ref, nv_ref):
        g = p_ref[0].astype(F32)
        for s in range(1, nparts):
            g = g + p_ref[s].astype(F32)
        g_ref[...] = g
        nm = B1 * m_ref[...] + (1.0 - B1) * g
        nv = B2 * v_ref[...] + (1.0 - B2) * (g * g)
        nm_ref[...] = nm
        nv_ref[...] = nv
        m_hat = nm / (1.0 - B1 ** STEP)
        v_hat = nv / (1.0 - B2 ** STEP)
        d_ref[...] = -LR * (m_hat / (jnp.sqrt(v_hat) + EPS) + WD * w_ref[...])

    row = pl.BlockSpec((tr, ncol), lambda i: (i, 0))
    sd = jax.ShapeDtypeStruct((rows, ncol), F32)
    return pl.pallas_call(
        body, name="adamw_" + name, grid=(rows // tr,),
        in_specs=[pl.BlockSpec((nparts, tr, ncol), lambda i: (0, i, 0)), row, row, row],
        out_specs=[row, row, row, row], out_shape=[sd, sd, sd, sd],
        compiler_params=_cp(("arbitrary",)),
    )(parts, w, m, v)


_SMALL = (("b_ada", 3 * D), ("conv_w", CW * CC // 4), ("conv_b", CC), ("ssm_norm_g", SW), ("ln_g", D), ("ln_b", D),
          ("q_norm_g", QL), ("kv_norm_g", KVL), ("dt_bias", SH), ("a_log", SH), ("d_skip", SH), ("loss", 128))


def _pack_small(d, lead):
    flat = [d[name].reshape(d[name].shape[:lead] + (-1,)) for name, _ in _SMALL]
    used = sum(f.shape[lead] for f in flat)
    pad = jnp.zeros(flat[0].shape[:lead] + (R_SMALL * 1024 - used,), F32)
    return jnp.concatenate(flat + [pad], axis=lead).reshape(flat[0].shape[:lead] + (R_SMALL, 1024))


def _unpack_small(p):
    flat = p.reshape(-1)
    out, r = {}, 0
    for name, n in _SMALL:
        out[name] = flat[r:r + n]
        r += n
    return out


def _in_to_padded(w):
    z = lambda n: jnp.zeros((w.shape[0], n), w.dtype)
    return jnp.concatenate([w[:, 0:384], w[:, 384:640], w[:, 704:1728], w[:, 1728:3264], w[:, 3280:4304],
                            w[:, 640:704], z(64), w[:, 3264:3280], z(112)], axis=1)


def _in_from_padded(g):
    return jnp.concatenate([g[:, 0:384], g[:, 384:640], g[:, P_KR[0]:P_KR[0] + 64], g[:, 640:1664], g[:, 1664:3200],
                            g[:, P_DT[0]:P_DT[0] + 16], g[:, 3200:4224]], axis=1)


def kernel(x, c, positions, w_ada, b_ada, w_in, q_norm_g, w_qb, kv_norm_g, w_kvb, conv_w, conv_b, dt_bias, a_log, d_skip, ssm_norm_g, w_out, ln_g, ln_b, loss_target, m_w_ada, m_b_ada, m_w_in, m_q_norm_g, m_w_qb, m_kv_norm_g, m_w_kvb, m_conv_w, m_conv_b, m_dt_bias, m_a_log, m_d_skip, m_ssm_norm_g, m_w_out, m_ln_g, m_ln_b, v_w_ada, v_b_ada, v_w_in, v_q_norm_g, v_w_qb, v_kv_norm_g, v_w_kvb, v_conv_w, v_conv_b, v_dt_bias, v_a_log, v_d_skip, v_ssm_norm_g, v_w_out, v_ln_g, v_ln_b):
    S = x.shape[1]
    xv = x[0]
    tgt = loss_target[0]

    cw16 = jnp.concatenate([conv_w[0], jnp.zeros((16 - CW, CC // 4), F32)], axis=0)
    f_in, f_qb, f_kvb, f_out, f_cw = _gather_weights(
        [w_in[0].astype(BF16), w_qb[0].astype(BF16), w_kvb[0].astype(BF16), w_out[0].astype(BF16), cw16])
    cat1 = lambda f: jnp.concatenate([f[k] for k in range(4)], axis=1)
    win = _in_to_padded(cat1(f_in))
    wqb = cat1(f_qb).reshape(QL, NH, QKD)
    wq = jnp.concatenate([wqb, jnp.zeros((QL, NH, HP - QKD), BF16)], axis=2).reshape(QL, NH * HP)
    wkvb = cat1(f_kvb).reshape(KVL, NH, NOPE + VD)
    wk = wkvb[:, :, 0:NOPE].reshape(KVL, NH * NOPE)
    wv = wkvb[:, :, NOPE:].reshape(KVL, NH * VD)
    wout = f_out.reshape(MIX, D)
    cwf = cat1(f_cw[:, 0:CW, :])

    half = ROPE // 2
    invf = 1.0 / (ROPE_THETA ** (jnp.arange(half, dtype=F32) / half))
    invf = jnp.concatenate([invf, invf, jnp.zeros((128 - ROPE,), F32)]).reshape(1, 128)
    pos = positions.reshape(S, 1)
    pad128 = lambda a: jnp.concatenate([a.reshape(1, SH), jnp.zeros((1, 128 - SH), F32)], axis=1)
    dtb, alog = pad128(dt_bias), pad128(a_log)
    dskx = jnp.repeat(d_skip.reshape(SH), SP).reshape(1, SW)

    my_c = lax.axis_index("c")
    (call,) = _exchange("gather_c", [jnp.broadcast_to(c.reshape(1, 1, D), (4, 1, D))])
    call = call.reshape(8, D)
    mods = _ada(call, w_ada[0])
    (mrows,) = _exchange("scatter_mod", [mods.reshape(4, 2, 3 * D // 4)])
    mine = lax.dynamic_index_in_dim(mrows.reshape(4, 2, 2, 3 * D // 4)[:, 0], my_c, axis=1, keepdims=False)
    mod = jnp.broadcast_to(mine.reshape(1, 3 * D) + b_ada, (8, 3 * D))
    qlat, ckv, za, xbc, zs, dtraw, q, k, v = _inproj(xv, mod, win, q_norm_g, kv_norm_g, wq, wk, wv, pos, invf)
    o, lse = _attn_fwd(q, k, v)
    xc, y, htp, ossm = _ssd_fwd(xbc, dtraw, zs, cwf, conv_b, dtb, alog, dskx, ssm_norm_g)
    gx1, do, dza, dossm, delta, dwout, vec_o = _outproj(o, za, ossm, xv, tgt, wout, mod, ln_g, ln_b)
    loss_part = jnp.zeros((128,), F32).at[0].set(0.5 / D * jnp.sum(vec_o[0]))

    dk, dv, dq = _attn_bwd(q, k, v, do, lse, delta)
    dxbc, ddt, dzs, dcw, dcb, dvec, dgssm = _ssd_bwd(xbc, xc, dtraw, zs, y, htp, dossm, cwf, dtb, alog, dskx, ssm_norm_g)
    dql, dckv, dkr, dwq, dwk, dwv, dqg, dkvg = _mla_bwd(dq, dk, dv, qlat, ckv, q_norm_g, kv_norm_g, wq, wk, wv, pos, invf)
    gx, dwin, vec_i = _inproj_bwd(xv, gx1, mod, win, dql, dckv, dza, dxbc, dzs, dkr, ddt)
    dmod = jnp.concatenate([vec_i[0:1], vec_i[1:2], vec_o[3:4]], axis=1)

    cols = lambda g: g.reshape(g.shape[0], 4, g.shape[1] // 4).transpose(1, 0, 2)
    g_in = cols(_in_from_padded(dwin)).astype(BF16)
    g_qb = cols(dwq.reshape(QL, NH, HP)[:, :, 0:QKD].reshape(QL, NH * QKD)).astype(BF16)
    g_kvb = cols(jnp.concatenate([dwk.reshape(KVL, NH, NOPE), dwv.reshape(KVL, NH, VD)], axis=2)
                 .reshape(KVL, NH * (NOPE + VD))).astype(BF16)
    g_out = dwout.reshape(4, MIX // 4, D).astype(BF16)
    small = {"b_ada": dmod, "conv_b": dcb, "ssm_norm_g": dgssm, "ln_g": vec_o[1:2], "ln_b": vec_o[2:3],
             "q_norm_g": dqg, "kv_norm_g": dkvg, "dt_bias": dvec[0:1, 0:SH], "a_log": dvec[1:2, 0:SH], "d_skip": dvec[2:3, 0:SH],
             "loss": loss_part}
    small = {n: jnp.broadcast_to(a.reshape(1, -1), (4, a.size)) for n, a in small.items()}
    small["conv_w"] = cols(dcw).reshape(4, CW * CC // 4)
    gsmall = _pack_small(small, 1)

    r_in, r_qb, r_kvb, r_out, rs, dmods = _exchange(
        "exchange_grads", [g_in, g_qb, g_kvb, g_out, gsmall, jnp.broadcast_to(dmod.reshape(1, 1, 3 * D), (4, 1, 3 * D))])
    chip = 2 * lax.axis_index("x") + lax.axis_index("y")
    dmods = lax.dynamic_slice_in_dim(dmods.reshape(8, 3 * D), chip * (3 * D // 4), 3 * D // 4, axis=1)
    g_ada = _ada_bwd(call.T, dmods)
    res = dict(w_ada=_adamw("w_ada", g_ada, w_ada[0], m_w_ada[0], v_w_ada[0]),
               w_in=_adamw("w_in", r_in, w_in[0], m_w_in[0], v_w_in[0]),
               w_qb=_adamw("w_qb", r_qb, w_qb[0], m_w_qb[0], v_w_qb[0]),
               w_kvb=_adamw("w_kvb", r_kvb, w_kvb[0], m_w_kvb[0], v_w_kvb[0]),
               w_out=_adamw("w_out", r_out, w_out[0], m_w_out[0], v_w_out[0]))
    wsm = _pack_small(dict(b_ada=b_ada, conv_w=conv_w, conv_b=conv_b, ssm_norm_g=ssm_norm_g, ln_g=ln_g, ln_b=ln_b,
                           q_norm_g=q_norm_g, kv_norm_g=kv_norm_g, dt_bias=dt_bias, a_log=a_log, d_skip=d_skip, loss=jnp.zeros((128,), F32)), 0)
    msm = _pack_small(dict(b_ada=m_b_ada, conv_w=m_conv_w, conv_b=m_conv_b, ssm_norm_g=m_ssm_norm_g, ln_g=m_ln_g, ln_b=m_ln_b,
                           q_norm_g=m_q_norm_g, kv_norm_g=m_kv_norm_g, dt_bias=m_dt_bias, a_log=m_a_log, d_skip=m_d_skip, loss=jnp.zeros((128,), F32)), 0)
    vsm = _pack_small(dict(b_ada=v_b_ada, conv_w=v_conv_w, conv_b=v_conv_b, ssm_norm_g=v_ssm_norm_g, ln_g=v_ln_g, ln_b=v_ln_b,
                           q_norm_g=v_q_norm_g, kv_norm_g=v_kv_norm_g, dt_bias=v_dt_bias, a_log=v_a_log, d_skip=v_d_skip, loss=jnp.zeros((128,), F32)), 0)
    sm = _adamw("small", rs, wsm, msm, vsm)

    order = ["w_ada", "b_ada", "w_in", "q_norm_g", "w_qb", "kv_norm_g", "w_kvb", "conv_w", "conv_b", "dt_bias", "a_log",
             "d_skip", "ssm_norm_g", "w_out", "ln_g", "ln_b"]
    shapes = dict(w_ada=w_ada.shape, b_ada=b_ada.shape, w_in=w_in.shape, q_norm_g=q_norm_g.shape, w_qb=w_qb.shape,
                  kv_norm_g=kv_norm_g.shape, w_kvb=w_kvb.shape, conv_w=conv_w.shape, conv_b=conv_b.shape, dt_bias=dt_bias.shape,
                  a_log=a_log.shape, d_skip=d_skip.shape, ssm_norm_g=ssm_norm_g.shape, w_out=w_out.shape, ln_g=ln_g.shape,
                  ln_b=ln_b.shape)
    outs = []
    for kind in range(4):
        d = _unpack_small(sm[kind])
        d.update({n: r[kind] for n, r in res.items()})
        outs.extend(d[n].reshape(shapes[n]) for n in order)
    loss = _unpack_small(sm[0])["loss"][0]
    return (loss, gx.reshape(x.shape), *outs)
```

```python
import functools
import math

import numpy as np
import jax
import jax.numpy as jnp
from jax import lax
from jax.experimental import pallas as pl
from jax.experimental.pallas import tpu as pltpu

F32 = jnp.float32
BF16 = jnp.bfloat16
MESH_ID = pl.DeviceIdType.MESH

D = 1024
NH = 8
NOPE = 128
ROPE = 64
VD = 128
VDP = 144
QKD = NOPE + ROPE
HP = 256
QL = 384
KVL = 256
ROPE_THETA = 10000.0
SH = 16
SP = 64
SG = 2
SN = 128
CW = 4
CH = 128
SW = SH * SP
CC = SW + 2 * SG * SN
GW = SW // SG
MIX = 2 * D
IN_W = 4304
ALPHA = 2.0 ** 0.25
RMS_EPS = 1e-6
LN_EPS = 1e-5
SCALE = QKD ** -0.5
LN2 = math.log(2.0)
QSCALE = SCALE / LN2
LR, B1, B2, EPS, WD, STEP = 0.001, 0.9, 0.999, 1e-08, 0.01, 10

P_Q = (0, 384)
P_KV = (384, 640)
P_ZA = (640, 1664)
P_XBC = (1664, 3200)
P_ZS = (3200, 4224)
P_KR = (4224, 4352)
P_DT = (4352, 4480)
IN_P = 4480

R_SMALL = 16

TM = 256
TQ = 512
NSP = 2
NCK = 2
TQF = 1024
VMEM_LIMIT = 56 * 1024 * 1024


def _cp(sem=None):
    return pltpu.CompilerParams(dimension_semantics=sem, vmem_limit_bytes=VMEM_LIMIT)


def _mm(a, b):
    return jnp.dot(a, b, preferred_element_type=F32)


def _nt(a, b):
    return lax.dot_general(a, b, (((1,), (1,)), ((), ())), preferred_element_type=F32)


def _tn(a, b):
    return lax.dot_general(a, b, (((0,), (0,)), ((), ())), preferred_element_type=F32)


def _split(a, terms):
    parts = []
    for t in range(terms):
        p = a.astype(BF16)
        parts.append(p)
        if t + 1 < terms:
            a = a - p.astype(F32)
    return parts


def _mm_x(a, ones, terms):
    parts = _split(a, terms)
    out = _mm(parts[0], ones)
    for p in parts[1:]:
        out = out + _mm(p, ones)
    return out


def _xmm(ones, a, terms):
    parts = _split(a, terms)
    out = _mm(ones, parts[0])
    for p in parts[1:]:
        out = out + _mm(ones, p)
    return out


def _nt_x(a, ones, terms):
    parts = _split(a, terms)
    out = _nt(parts[0], ones)
    for p in parts[1:]:
        out = out + _nt(p, ones)
    return out


def _sigmoid(z):
    return 1.0 / (1.0 + jnp.exp(-z))


def _softplus(z):
    return jnp.maximum(z, 0.0) + jnp.log1p(jnp.exp(-jnp.abs(z)))


def _rope(t, cs, sn):
    lane = lax.broadcasted_iota(jnp.int32, t.shape, 1)
    rot = jnp.where(lane < ROPE // 2, -pltpu.roll(t, 128 - ROPE // 2, 1), pltpu.roll(t, ROPE // 2, 1))
    return t * cs + rot * sn


def _rope_t(t, cs, sn):
    lane = lax.broadcasted_iota(jnp.int32, t.shape, 1)
    y = t * sn
    rot = jnp.where(lane < ROPE // 2, -pltpu.roll(y, 128 - ROPE // 2, 1), pltpu.roll(y, ROPE // 2, 1))
    return t * cs - rot


def _full(shape):
    n = len(shape)
    return pl.BlockSpec(shape, lambda *_: (0,) * n)


def _const(shape):
    n = len(shape)
    return pl.BlockSpec(shape, lambda *_: (0,) * n, pipeline_mode=pl.Buffered(1))


def _gather_weights(shards):
    n = len(shards)
    halves = [a.shape[0] // 2 for a in shards]

    def body(*refs):
        srcs, dsts = refs[:n], refs[n:2 * n]
        send_sems, recv_sems, local_sems = refs[2 * n:]
        x, y, c = lax.axis_index("x"), lax.axis_index("y"), lax.axis_index("c")
        me = 2 * x + y
        sibling = (x, y, 1 - c)
        chips = [(1 - x, y), (x, 1 - y), (1 - x, 1 - y)]

        def rows(a, pc):
            return pl.ds(pl.multiple_of(pc * halves[a], halves[a]), halves[a])

        def copy(a, k, src, slot, pc, to):
            return pltpu.make_async_remote_copy(
                src_ref=src, dst_ref=dsts[a].at[slot, rows(a, pc)], send_sem=send_sems.at[a, k],
                recv_sem=recv_sems.at[a, k], device_id=to, device_id_type=MESH_ID)

        local = [pltpu.make_async_copy(srcs[a], dsts[a].at[me], local_sems.at[a]) for a in range(n)]
        for cp in local:
            cp.start()
        sends = [copy(a, j, srcs[a].at[rows(a, c)], me, c, (px, py, c)) for a in range(n) for j, (px, py) in enumerate(chips)]
        for cp in sends:
            cp.start()
        passed = []
        for a in range(n):
            for j, (px, py) in enumerate(chips):
                k = 2 * px + py
                copy(a, j, srcs[a].at[rows(a, c)], k, c, (x, y, c)).wait_recv()
                fwd = copy(a, 3 + j, dsts[a].at[k, rows(a, c)], k, c, sibling)
                fwd.start()
                passed.append(fwd)
        for a in range(n):
            for j, (px, py) in enumerate(chips):
                copy(a, 3 + j, srcs[a].at[rows(a, c)], 2 * px + py, 1 - c, (x, y, c)).wait_recv()
        for cp in sends + passed:
            cp.wait_send()
        for cp in local:
            cp.wait()

    hbm = pl.BlockSpec(memory_space=pltpu.HBM)
    return pl.pallas_call(
        body, name="gather_weights",
        out_shape=tuple(jax.ShapeDtypeStruct((4,) + a.shape, a.dtype) for a in shards),
        in_specs=[hbm] * n, out_specs=tuple([hbm] * n),
        scratch_shapes=[pltpu.SemaphoreType.DMA((n, 6)), pltpu.SemaphoreType.DMA((n, 6)), pltpu.SemaphoreType.DMA((n,))],
    )(*shards)


def _exchange(name, slabs):
    n = len(slabs)

    def body(*refs):
        srcs, dsts = refs[:n], refs[n:2 * n]
        send_sems, recv_sems, local_sems = refs[2 * n:]
        x, y, c = lax.axis_index("x"), lax.axis_index("y"), lax.axis_index("c")
        chip = 2 * x + y
        sibling = (x, y, 1 - c)
        chips = [(1 - x, y), (x, 1 - y), (1 - x, 1 - y)]

        def slot(px, py, pc):
            return 4 * px + 2 * py + pc

        def copy(a, k, src, s, to):
            return pltpu.make_async_remote_copy(
                src_ref=src, dst_ref=dsts[a].at[s], send_sem=send_sems.at[a, k], recv_sem=recv_sems.at[a, k],
                device_id=to, device_id_type=MESH_ID)

        mine = slot(x, y, c)
        local = [pltpu.make_async_copy(srcs[a].at[chip], dsts[a].at[mine], local_sems.at[a]) for a in range(n)]
        for cp in local:
            cp.start()
        first = []
        for a in range(n):
            first.append(copy(a, 0, srcs[a].at[chip], mine, sibling))
            for j, (px, py) in enumerate(chips):
                first.append(copy(a, 1 + j, srcs[a].at[2 * px + py], mine, (px, py, c)))
        for cp in first:
            cp.start()
        passed = []
        for a in range(n):
            for j, (px, py) in enumerate(chips):
                s = slot(px, py, c)
                copy(a, 1 + j, srcs[a].at[chip], s, (x, y, c)).wait_recv()
                fwd = copy(a, 4 + j, dsts[a].at[s], s, sibling)
                fwd.start()
                passed.append(fwd)
        for a in range(n):
            copy(a, 0, srcs[a].at[chip], slot(x, y, 1 - c), (x, y, c)).wait_recv()
            for j, (px, py) in enumerate(chips):
                copy(a, 4 + j, srcs[a].at[chip], slot(px, py, 1 - c), (x, y, c)).wait_recv()
        for cp in first + passed:
            cp.wait_send()
        for cp in local:
            cp.wait()

    hbm = pl.BlockSpec(memory_space=pltpu.HBM)
    return pl.pallas_call(
        body, name=name,
        out_shape=tuple(jax.ShapeDtypeStruct((8,) + a.shape[1:], a.dtype) for a in slabs),
        in_specs=[hbm] * n, out_specs=tuple([hbm] * n),
        scratch_shapes=[pltpu.SemaphoreType.DMA((n, 7)), pltpu.SemaphoreType.DMA((n, 7)), pltpu.SemaphoreType.DMA((n,))],
    )(*slabs)


def _ada(call, w_shard):
    def body(c_ref, w_ref, o_ref):
        o_ref[...] = _mm(c_ref[...].astype(BF16), w_ref[...].astype(BF16))

    return pl.pallas_call(body, name="ada", out_shape=jax.ShapeDtypeStruct((8, w_shard.shape[1]), F32),
                          compiler_params=_cp())(call, w_shard)


def _inproj(x, mod, win, qg, kvg, wq, wk, wv, pos, invf):
    S = x.shape[0]
    tm = min(TM, S)

    def body(x_ref, mod_ref, win_ref, qg_ref, kvg_ref, wq_ref, wk_ref, wv_ref, pos_ref, invf_ref,
             qlat_ref, ckv_ref, za_ref, xbc_ref, zs_ref, dt_ref, q_ref, k_ref, v_ref):
        shift = mod_ref[0:1, 0:D]
        scale = mod_ref[0:1, D:2 * D]
        u = (x_ref[...] * (1.0 + scale) + shift).astype(BF16)

        def proj(p):
            return _mm(u, win_ref[:, p[0]:p[1]])

        ql = proj(P_Q)
        ckv = proj(P_KV)
        qlat_ref[...] = ql
        ckv_ref[...] = ckv
        za_ref[...] = proj(P_ZA)
        xbc_ref[...] = proj(P_XBC)
        zs_ref[...] = proj(P_ZS)
        dt_ref[...] = proj(P_DT)
        kr = proj(P_KR)

        ang = pos_ref[...].astype(F32) * invf_ref[...]
        cs = jnp.cos(ang)
        sn = jnp.sin(ang)

        rq = lax.rsqrt(jnp.mean(ql * ql, axis=-1, keepdims=True) + RMS_EPS)
        qn = (ql * rq * qg_ref[...]).astype(BF16)
        for h in range(NH):
            qh = _mm(qn, wq_ref[:, h * HP:(h + 1) * HP])
            q_ref[h, :, 0:NOPE] = (qh[:, 0:NOPE] * QSCALE).astype(BF16)
            q_ref[h, :, NOPE:HP] = (_rope(qh[:, NOPE:HP], cs, sn) * QSCALE).astype(BF16)

        rk = lax.rsqrt(jnp.mean(ckv * ckv, axis=-1, keepdims=True) + RMS_EPS)
        kn = (ckv * rk * kvg_ref[...]).astype(BF16)
        knope = _mm(kn, wk_ref[...])
        vall = _mm(kn, wv_ref[...])
        krf = _rope(kr, cs, sn)
        krr = krf.astype(BF16)
        for h in range(NH):
            k_ref[h, :, 0:NOPE] = knope[:, h * NOPE:(h + 1) * NOPE].astype(BF16)
            k_ref[h, :, NOPE:HP] = krr
            v_ref[h] = vall[:, h * VD:(h + 1) * VD].astype(BF16)

    row = lambda n: pl.BlockSpec((tm, n), lambda i: (i, 0))
    heads = lambda n: pl.BlockSpec((NH, tm, n), lambda i: (0, i, 0))
    sd = lambda n: jax.ShapeDtypeStruct((S, n), F32)
    hd = lambda n: jax.ShapeDtypeStruct((NH, S, n), BF16)
    return pl.pallas_call(
        body, name="inproj", grid=(S // tm,),
        in_specs=[row(D), _const((8, 3 * D)), _const((D, IN_P)), _const((1, QL)), _const((1, KVL)),
                  _const((QL, NH * HP)), _const((KVL, NH * NOPE)), _const((KVL, NH * VD)),
                  row(1), _const((1, 128))],
        out_specs=[row(QL), row(KVL), row(D), row(CC), row(D), row(128), heads(HP), heads(HP), heads(VD)],
        out_shape=[sd(QL), sd(KVL), sd(D), sd(CC), sd(D), sd(128), hd(HP), hd(HP), hd(VD)],
        compiler_params=_cp(("arbitrary",)),
    )(x, mod, win, qg, kvg, wq, wk, wv, pos, invf)


def _attn_fwd(q, k, v):
    _, S, _ = q.shape
    tq = min(TQF, S)
    nq = S // tq
    half = tq // NSP
    tb = min(256, half)
    nsb = half // tb

    def body(q_ref, k_ref, v_ref, o_ref, lse_ref, vt_ref):
        i = pl.program_id(1)
        qb = q_ref[...]

        @pl.when(i == 0)
        def _():
            ones_rows = jnp.where(lax.broadcasted_iota(jnp.int32, (VDP - VD, tb), 0) == 0, 1.0, 0.0).astype(BF16)

            def fill(blk, carry):
                off = pl.multiple_of(blk * tb, tb)
                vt_ref[blk, 0:VD, :] = v_ref[pl.ds(off, tb), :].T
                vt_ref[blk, VD:VDP, :] = ones_rows
                return carry

            lax.fori_loop(0, S // tb, fill, 0)

        def scores(j, hb):
            off = pl.multiple_of(j * tq + hb * half, half)
            return _nt(k_ref[pl.ds(off, half), :], qb)

        def update(j, hb, s, carry):
            m, acc = carry
            m_new = jnp.maximum(m, jnp.max(s, axis=0, keepdims=True))
            a = jnp.exp2(m - m_new)
            pb = jnp.exp2(s - m_new).astype(BF16)
            acc = a * acc
            for sb in range(nsb):
                acc = acc + _mm(vt_ref[(NSP * j + hb) * nsb + sb], pb[sb * tb:(sb + 1) * tb, :])
            return m_new, acc

        def trip(j, carry, masked):
            s = [scores(j, hb) for hb in range(NSP)]
            if masked:
                r = lax.broadcasted_iota(jnp.int32, s[0].shape, 0)
                cidx = lax.broadcasted_iota(jnp.int32, s[0].shape, 1)
                s = [jnp.where(cidx >= r + hb * half, s[hb], -1e30) for hb in range(NSP)]
            for hb in range(NSP):
                carry = update(j, hb, s[hb], carry)
            return carry

        def finish(carry):
            m, acc = carry
            l = acc[VD:VD + 1, :]
            o_ref[...] = (acc[0:VD, :] / l).T
            lse_ref[...] = m + jnp.log2(l)

        def quad(t, cr):
            for u in range(4):
                cr = trip(4 * t + u, cr, False)
            return cr

        init = (jnp.full((1, tq), -1e30, F32), jnp.zeros((VDP, tq), F32))
        carry = lax.fori_loop(0, i >> 2, quad, init)
        carry = lax.fori_loop(i - (i & 3), i, lambda j, cr: trip(j, cr, False), carry)
        finish(trip(i, carry, True))

    return pl.pallas_call(
        body, name="attn_fwd", grid=(NH, nq),
        in_specs=[pl.BlockSpec((None, tq, HP), lambda h, i: (h, i, 0)),
                  pl.BlockSpec((None, S, HP), lambda h, i: (h, 0, 0)),
                  pl.BlockSpec((None, S, VD), lambda h, i: (h, 0, 0))],
        out_specs=[pl.BlockSpec((tq, VD), lambda h, i: (i, h)),
                   pl.BlockSpec((None, None, 1, tq), lambda h, i: (h, i, 0, 0))],
        out_shape=[jax.ShapeDtypeStruct((S, NH * VD), F32), jax.ShapeDtypeStruct((NH, nq, 1, tq), F32)],
        scratch_shapes=[pltpu.VMEM((S // tb, VDP, tb), BF16)],
        compiler_params=_cp(("arbitrary", "arbitrary")),
    )(q, k, v)


def _ssd_consts():
    tri = np.tril(np.ones((CH, CH), np.float32))
    e16 = np.zeros((128, SW), np.float32)
    for h in range(SH):
        e16[h, h * SP:(h + 1) * SP] = 1.0
    return jnp.asarray(tri, BF16), jnp.asarray(tri.T.copy(), BF16), jnp.asarray(e16, BF16)


def _ssd_conv(xraw, halo, cw_ref, cb_ref, ext):
    ext[0:8, :] = halo
    ext[8:8 + CH, :] = xraw
    cw = cw_ref[...]
    xc = cb_ref[...] + cw[0:1, :] * ext[5:5 + CH, :]
    for kk in range(1, CW):
        xc = xc + cw[kk:kk + 1, :] * ext[5 + kk:5 + kk + CH, :]
    return xc


def _ssd_chunk_common(xc, dtraw_ref, dtb_ref, alog_ref, tri_ref, e16_ref):
    sact = _sigmoid(xc)
    act = xc * sact
    lane = lax.broadcasted_iota(jnp.int32, (1, 128), 1)
    arow = jnp.where(lane < SH, -jnp.exp(alog_ref[...]), 0.0)
    dtpre = dtraw_ref[...] + dtb_ref[...]
    dt = _softplus(dtpre)
    a = dt * arow
    cum = _xmm(tri_ref[...], a, 3)
    cumx = _mm_x(cum, e16_ref[...], 3)
    dtx = _mm_x(dt, e16_ref[...], 2)
    return sact, act, arow, dtpre, dt, cum, cumx, dtx


def _ssd_fwd(xbc, dtraw, zs, conv_w, conv_b, dtb, alog, dskx, gssm):
    S = xbc.shape[0]
    nc = S // CH
    tri, _, e16 = _ssd_consts()

    def body(xbc_ref, halo_ref, dtraw_ref, zs_ref, cw_ref, cb_ref, dtb_ref, alog_ref, dsk_ref, g_ref, tri_ref, e16_ref,
             xc_ref, y_ref, htp_ref, o_ref, ht, ext):
        i = pl.program_id(0)

        @pl.when(i == 0)
        def _():
            ht[...] = jnp.zeros_like(ht)

        for c in range(NCK):
            chunk(c, i, xbc_ref, halo_ref, dtraw_ref, zs_ref, cw_ref, cb_ref, dtb_ref, alog_ref, dsk_ref, g_ref, tri_ref,
                  e16_ref, xc_ref, y_ref, htp_ref, o_ref, ht, ext)

    def chunk(c, i, xbc_ref, halo_ref, dtraw_ref, zs_ref, cw_ref, cb_ref, dtb_ref, alog_ref, dsk_ref, g_ref, tri_ref, e16_ref,
              xc_ref, y_ref, htp_ref, o_ref, ht, ext):
        rows = slice(c * CH, (c + 1) * CH)
        halo = jnp.where(i == 0, 0.0, halo_ref[...]) if c == 0 else xbc_ref[c * CH - 8:c * CH, :]
        xc = _ssd_conv(xbc_ref[rows, :], halo, cw_ref, cb_ref, ext)
        xc_ref[rows, :] = xc
        sact, act, arow, dtpre, dt, cum, cumx, dtx = _ssd_chunk_common(xc, dtraw_ref.at[rows, :], dtb_ref, alog_ref, tri_ref, e16_ref)
        cum_t = cum.T
        xs = act[:, 0:SW]
        lastx = cumx[CH - 1:CH, :]
        xh = xs * dtx
        eexp = jnp.exp(cumx)
        dte = jnp.exp(lastx - cumx)
        cdx = jnp.exp(lastx)
        htp = ht[...]
        htp_ref[c] = htp
        xw = (xh * dte).astype(BF16)
        xb = xh.astype(BF16)
        trim = tri_ref[...].astype(F32) > 0.5
        lane = lax.broadcasted_iota(jnp.int32, (CH, 128), 1)
        parts = []
        for g in range(SG):
            gl = slice(g * GW, (g + 1) * GW)
            bg = act[:, SW + g * SN:SW + (g + 1) * SN].astype(BF16)
            cg = act[:, SW + SG * SN + g * SN:SW + SG * SN + (g + 1) * SN].astype(BF16)
            cbm = _nt(cg, bg)
            yoff = eexp[:, gl] * _mm(cg, htp[:, gl].astype(BF16))
            ht[:, gl] = htp[:, gl] * cdx[:, gl] + _tn(bg, xw[:, gl])
            for pr in range(GW // 128):
                h0 = g * (SH // SG) + 2 * pr
                lo = g * GW + pr * 128
                xp = xb[:, lo:lo + 128]
                res = []
                for hh in (h0, h0 + 1):
                    seg = cum[:, hh:hh + 1] - cum_t[hh:hh + 1, :]
                    mh = jnp.where(trim, cbm * jnp.exp(seg), 0.0).astype(BF16)
                    res.append(_mm(mh, xp))
                parts.append(jnp.where(lane < SP, res[0], res[1]) + yoff[:, pr * 128:(pr + 1) * 128])
        y = jnp.concatenate(parts, axis=1) + xs * dsk_ref[...]
        y_ref[rows, :] = y
        z = zs_ref[rows, :]
        hf = y * (z * _sigmoid(z))
        outs = []
        for g in range(SG):
            hg = hf[:, g * GW:(g + 1) * GW]
            rs = lax.rsqrt(jnp.mean(hg * hg, axis=-1, keepdims=True) + RMS_EPS)
            outs.append(hg * rs)
        o_ref[rows, :] = (jnp.concatenate(outs, axis=1) * g_ref[...]).astype(BF16)

    row = lambda n: pl.BlockSpec((NCK * CH, n), lambda i: (i, 0))
    return pl.pallas_call(
        body, name="ssd_fwd", grid=(nc // NCK,),
        in_specs=[row(CC), pl.BlockSpec((8, CC), lambda i: (jnp.maximum(i * (NCK * CH // 8) - 1, 0), 0)), row(128), row(SW),
                  _const((CW, CC)), _const((1, CC)), _const((1, 128)), _const((1, 128)), _const((1, SW)), _const((1, SW)),
                  _const((CH, CH)), _const((128, SW))],
        out_specs=[row(CC), row(SW), pl.BlockSpec((NCK, SN, SW), lambda i: (i, 0, 0)), row(SW)],
        out_shape=[jax.ShapeDtypeStruct((S, CC), F32), jax.ShapeDtypeStruct((S, SW), F32),
                   jax.ShapeDtypeStruct((nc, SN, SW), F32), jax.ShapeDtypeStruct((S, SW), BF16)],
        scratch_shapes=[pltpu.VMEM((SN, SW), F32), pltpu.VMEM((8 + CH, CC), F32)],
        compiler_params=_cp(("arbitrary",)),
    )(xbc, xbc, dtraw, zs, conv_w, conv_b, dtb, alog, dskx, gssm, tri, e16)


def _outproj(o, za, ossm, x, tgt, wout, mod, ln_g, ln_b):
    S = x.shape[0]
    tm = min(TM, S)

    e8 = np.zeros((D, 128), np.float32)
    for h in range(NH):
        e8[h * VD:(h + 1) * VD, h] = 1.0
    e8 = jnp.asarray(e8, BF16)

    def body(o_ref, za_ref, os_ref, x_ref, t_ref, w_ref, mod_ref, g_ref, b_ref, e8_ref,
             gx_ref, do_ref, dza_ref, dos_ref, delta_ref, dw_ref, vec_ref):
        i = pl.program_id(0)

        @pl.when(i == 0)
        def _():
            dw_ref[...] = jnp.zeros_like(dw_ref)
            vec_ref[...] = jnp.zeros_like(vec_ref)

        gate = mod_ref[0:1, 2 * D:3 * D]
        ov = o_ref[...]
        z = za_ref[...]
        sz = _sigmoid(z)
        silz = z * sz
        a = (ov * silz).astype(BF16)
        osb = os_ref[...]
        mixed = _mm(a, w_ref[0:D, :]) + _mm(osb, w_ref[D:MIX, :])
        xv = x_ref[...]
        hres = ALPHA * xv + gate * mixed
        mu = jnp.mean(hres, axis=-1, keepdims=True)
        hc = hres - mu
        var = jnp.mean(hc * hc, axis=-1, keepdims=True)
        rstd = lax.rsqrt(var + LN_EPS)
        xhat = hc * rstd
        g = g_ref[...]
        yv = xhat * g + b_ref[...]
        err = yv - t_ref[...]
        dy = err * (1.0 / D)
        vec_ref[0:1, :] += jnp.sum(err * err, axis=0, keepdims=True)
        vec_ref[1:2, :] += jnp.sum(dy * xhat, axis=0, keepdims=True)
        vec_ref[2:3, :] += jnp.sum(dy, axis=0, keepdims=True)
        dxh = dy * g
        dh = rstd * (dxh - jnp.mean(dxh, axis=-1, keepdims=True) - xhat * jnp.mean(dxh * xhat, axis=-1, keepdims=True))
        gx_ref[...] = ALPHA * dh
        vec_ref[3:4, :] += jnp.sum(dh * mixed, axis=0, keepdims=True)
        dmixed = (gate * dh).astype(BF16)
        dw_ref[0:D, :] += _tn(a, dmixed)
        dw_ref[D:MIX, :] += _tn(osb, dmixed)
        da = _nt(dmixed, w_ref[0:D, :])
        dos_ref[...] = _nt(dmixed, w_ref[D:MIX, :])
        dov = da * silz
        do_ref[...] = dov.astype(BF16)
        dza_ref[...] = (da * ov * (sz * (1.0 + z * (1.0 - sz)))).astype(BF16)
        delta_ref[:, 0, :] = _mm_x(dov * ov, e8_ref[...], 2).T[0:NH, :]

    row = lambda n: pl.BlockSpec((tm, n), lambda i: (i, 0))
    return pl.pallas_call(
        body, name="outproj", grid=(S // tm,),
        in_specs=[row(D), row(D), row(D), row(D), row(D), _const((MIX, D)), _const((8, 3 * D)), _const((1, D)), _const((1, D)),
                  _const((D, 128))],
        out_specs=[row(D), row(D), row(D), row(D), pl.BlockSpec((NH, None, 1, tm), lambda i: (0, i, 0, 0)),
                   _full((MIX, D)), _full((8, D))],
        out_shape=[jax.ShapeDtypeStruct((S, D), F32), jax.ShapeDtypeStruct((S, D), BF16), jax.ShapeDtypeStruct((S, D), BF16),
                   jax.ShapeDtypeStruct((S, D), F32), jax.ShapeDtypeStruct((NH, S // tm, 1, tm), F32),
                   jax.ShapeDtypeStruct((MIX, D), F32), jax.ShapeDtypeStruct((8, D), F32)],
        compiler_params=_cp(("arbitrary",)),
    )(o, za, ossm, x, tgt, wout, mod, ln_g, ln_b, e8)


def _attn_bwd(q, k, v, do, lse, delta):
    _, S, _ = q.shape
    tk = min(TQ, S // 2)
    nk = S // tk
    tq = 2 * tk
    nq = S // tq

    def body(k_ref, v_ref, q_ref, do_ref, lse_ref, dl_ref, dk_ref, dv_ref, dq_ref, dqt_ref):
        j = pl.program_id(1)
        kb = k_ref[...]
        ktb = kb.T
        vb = v_ref[...]

        @pl.when(j == 0)
        def _():
            dqt_ref[...] = jnp.zeros_like(dqt_ref)

        dk_ref[...] = jnp.zeros_like(dk_ref)
        dv_ref[...] = jnp.zeros_like(dv_ref)

        def step(i, masked, lo=0):
            off = pl.multiple_of(i * tq + lo, tk)
            qb = q_ref[pl.ds(off, tq - lo), :]
            dob = do_ref[pl.ds(off, tq - lo), :]
            pt = jnp.exp2(_nt(kb, qb) - lse_ref[i][:, lo:tq])
            if masked:
                r = lax.broadcasted_iota(jnp.int32, pt.shape, 0)
                cidx = lax.broadcasted_iota(jnp.int32, pt.shape, 1)
                pt = jnp.where(i * tq + lo + cidx >= j * tk + r, pt, 0.0)
            dv_ref[...] += _mm(pt.astype(BF16), dob)
            dsb = (pt * (_nt(vb, dob) - dl_ref[i][:, lo:tq])).astype(BF16)
            dk_ref[...] += _mm(dsb, qb)
            dqt_ref[i, :, lo:tq] += _mm(ktb, dsb)

        first = j >> 1

        @pl.when((j & 1) == 0)
        def _():
            step(first, True)

        @pl.when((j & 1) == 1)
        def _():
            step(first, True, tk)
            dq_ref[...] = dqt_ref[first].T

        def loop_body(t, carry):
            for u in range(4):
                step(first + 1 + 4 * t + u, False)
            return carry

        def tail_body(i, carry):
            step(i, False)
            return carry

        rest = nq - 1 - first
        lax.fori_loop(0, rest >> 2, loop_body, 0)
        lax.fori_loop(nq - (rest & 3), nq, tail_body, 0)

        dk_ref[...] = dk_ref[...] * LN2

    return pl.pallas_call(
        body, name="attn_bwd", grid=(NH, nk),
        in_specs=[pl.BlockSpec((None, tk, HP), lambda h, j: (h, j, 0)),
                  pl.BlockSpec((None, tk, VD), lambda h, j: (h, j, 0)),
                  pl.BlockSpec((None, S, HP), lambda h, j: (h, 0, 0)),
                  pl.BlockSpec((S, VD), lambda h, j: (0, h)),
                  pl.BlockSpec((None, nq, 1, tq), lambda h, j: (h, 0, 0, 0)),
                  pl.BlockSpec((None, nq, 1, tq), lambda h, j: (h, 0, 0, 0))],
        out_specs=[pl.BlockSpec((None, tk, HP), lambda h, j: (h, j, 0)),
                   pl.BlockSpec((None, tk, VD), lambda h, j: (h, j, 0)),
                   pl.BlockSpec((None, tq, HP), lambda h, j: (h, j >> 1, 0))],
        out_shape=[jax.ShapeDtypeStruct((NH, S, HP), F32), jax.ShapeDtypeStruct((NH, S, VD), F32),
                   jax.ShapeDtypeStruct((NH, S, HP), F32)],
        scratch_shapes=[pltpu.VMEM((nq, HP, tq), F32)],
        compiler_params=_cp(("arbitrary", "arbitrary")),
    )(k, v, q, do, lse.reshape(NH, nq, 1, tq), delta.reshape(NH, nq, 1, tq))


def _ssd_bwd(xbc, xc, dtraw, zs, y, htp, dossm, conv_w, dtb, alog, dskx, gssm):
    S = xbc.shape[0]
    nc = S // CH
    tri, triu, e16 = _ssd_consts()

    def body(xbc_ref, xc_ref, dtraw_ref, zs_ref, y_ref, htp_ref, dos_ref,
             cw_ref, dtb_ref, alog_ref, dsk_ref, g_ref, tri_ref, triu_ref, e16_ref,
             dxbc_ref, ddt_ref, dzs_ref, dcw_ref, dcb_ref, dvec_ref, dg_ref,
             dht, dext, dskacc):
        r = pl.program_id(0)

        @pl.when(r == 0)
        def _():
            dht[...] = jnp.zeros_like(dht)
            dext[CH:CH + 8, :] = jnp.zeros((8, CC), F32)
            dskacc[...] = jnp.zeros_like(dskacc)
            dcw_ref[...] = jnp.zeros_like(dcw_ref)
            dcb_ref[...] = jnp.zeros_like(dcb_ref)
            dvec_ref[...] = jnp.zeros_like(dvec_ref)
            dg_ref[...] = jnp.zeros_like(dg_ref)

        for c in reversed(range(NCK)):
            rows = slice(c * CH, (c + 1) * CH)
            chunk(xbc_ref.at[rows, :], xc_ref.at[rows, :], dtraw_ref.at[rows, :], zs_ref.at[rows, :], y_ref.at[rows, :],
                  htp_ref.at[c], dos_ref.at[rows, :], cw_ref, dtb_ref, alog_ref, dsk_ref, g_ref, tri_ref, triu_ref, e16_ref,
                  dxbc_ref.at[rows, :], ddt_ref.at[rows, :], dzs_ref.at[rows, :], dcw_ref, dcb_ref, dvec_ref, dg_ref,
                  dht, dext, dskacc)

        @pl.when(r == nc // NCK - 1)
        def _():
            lane = lax.broadcasted_iota(jnp.int32, (1, 128), 1)
            arow = jnp.where(lane < SH, -jnp.exp(alog_ref[...]), 0.0)
            dvec_ref[1:2, :] = dvec_ref[1:2, :] * arow
            dvec_ref[2:3, :] = _nt_x(jnp.broadcast_to(dskacc[...], (8, SW)), e16_ref[...], 3)[0:1, :]

    def chunk(xbc_ref, xc_ref, dtraw_ref, zs_ref, y_ref, htp_ref, dos_ref,
              cw_ref, dtb_ref, alog_ref, dsk_ref, g_ref, tri_ref, triu_ref, e16_ref,
              dxbc_ref, ddt_ref, dzs_ref, dcw_ref, dcb_ref, dvec_ref, dg_ref,
              dht, dext, dskacc):
        xc = xc_ref[...]
        sact, act, arow, dtpre, dt, cum, cumx, dtx = _ssd_chunk_common(xc, dtraw_ref, dtb_ref, alog_ref, tri_ref, e16_ref)
        cum_t = cum.T
        xs = act[:, 0:SW]
        lastx = cumx[CH - 1:CH, :]
        xh = xs * dtx
        eexp = jnp.exp(cumx)
        dte = jnp.exp(lastx - cumx)
        cdx = jnp.exp(lastx)
        trim = tri_ref[...].astype(F32) > 0.5
        lane = lax.broadcasted_iota(jnp.int32, (CH, 128), 1)
        rowi = lax.broadcasted_iota(jnp.int32, (CH, 128), 0)

        yv = y_ref[...]
        z = zs_ref[...]
        sz = _sigmoid(z)
        silz = z * sz
        hf = yv * silz
        dn = dos_ref[...] * g_ref[...]
        dhf_parts, nrm_parts = [], []
        for g in range(SG):
            gl = slice(g * GW, (g + 1) * GW)
            hg = hf[:, gl]
            rs = lax.rsqrt(jnp.mean(hg * hg, axis=-1, keepdims=True) + RMS_EPS)
            ng = hg * rs
            dng = dn[:, gl]
            dhf_parts.append(rs * (dng - ng * jnp.mean(dng * ng, axis=-1, keepdims=True)))
            nrm_parts.append(ng)
        nrm = jnp.concatenate(nrm_parts, axis=1)
        dhf = jnp.concatenate(dhf_parts, axis=1)
        dg_ref[...] += jnp.sum(dos_ref[...] * nrm, axis=0, keepdims=True)
        dyv = dhf * silz
        dzs_ref[...] = (dhf * yv * (sz * (1.0 + z * (1.0 - sz)))).astype(BF16)
        dskacc[...] += jnp.sum(dyv * xs, axis=0, keepdims=True)
        dxs_skip = dyv * dsk_ref[...]

        dhtn = dht[...]
        hp = htp_ref[...]
        dlastx = jnp.sum(dhtn * hp, axis=0, keepdims=True) * cdx
        xb = xh.astype(BF16)
        xwf = xh * dte
        dcum = jnp.zeros((CH, 128), F32)
        dcum_t = jnp.zeros((128, CH), F32)
        dxh_parts, dcumx_parts, dlast_parts, db_parts, dc_parts = [], [], [], [], []
        for g in range(SG):
            gl = slice(g * GW, (g + 1) * GW)
            bg = act[:, SW + g * SN:SW + (g + 1) * SN].astype(BF16)
            cg = act[:, SW + SG * SN + g * SN:SW + SG * SN + (g + 1) * SN].astype(BF16)
            hpg = hp[:, gl].astype(BF16)
            dhn = dhtn[:, gl].astype(BF16)
            dyg = dyv[:, gl]
            dz = (dyg * eexp[:, gl]).astype(BF16)
            dcg = _nt(dz, hpg)
            dht[:, gl] = dhtn[:, gl] * cdx[:, gl] + _tn(cg, dz)
            yoff = eexp[:, gl] * _mm(cg, hpg)
            dcumx_g = dyg * yoff
            dbg = _nt(xwf[:, gl].astype(BF16), dhn)
            dxw = _mm(bg, dhn)
            ddte = dxw * xwf[:, gl]
            dcumx_parts.append(dcumx_g - ddte)
            dlast_parts.append(jnp.sum(ddte, axis=0, keepdims=True))
            dxh_g = dxw * dte[:, gl]
            cbm = _nt(cg, bg)
            dcb = jnp.zeros((CH, CH), F32)
            dxp_parts = []
            for pr in range(GW // 128):
                h0 = g * (SH // SG) + 2 * pr
                lo = g * GW + pr * 128
                xp = xb[:, lo:lo + 128]
                dyp = dyv[:, lo:lo + 128]
                dxp = jnp.zeros((CH, 128), F32)
                for idx, hh in enumerate((h0, h0 + 1)):
                    decay = jnp.where(trim, jnp.exp(cum[:, hh:hh + 1] - cum_t[hh:hh + 1, :]), 0.0)
                    mh = cbm * decay
                    keep = (lane < SP) if idx == 0 else (lane >= SP)
                    dym = jnp.where(keep, dyp, 0.0).astype(BF16)
                    dm = _nt(dym, xp)
                    dxp = dxp + _tn(mh.astype(BF16), dym)
                    gm = dm * mh
                    dcum = dcum + jnp.where(lane == hh, jnp.sum(gm, axis=1, keepdims=True), 0.0)
                    dcum_t = dcum_t - jnp.where(rowi == hh, jnp.sum(gm, axis=0, keepdims=True), 0.0)
                    dcb = dcb + dm * decay
                dxp_parts.append(dxp)
            dxh_parts.append(dxh_g + jnp.concatenate(dxp_parts, axis=1))
            dcbb = dcb.astype(BF16)
            dc_parts.append(dcg + _mm(dcbb, bg))
            db_parts.append(dbg + _tn(dcbb, cg))
        dxh = jnp.concatenate(dxh_parts, axis=1)
        dcumx = jnp.concatenate(dcumx_parts, axis=1)
        dlastx = dlastx + jnp.concatenate(dlast_parts, axis=1)
        e16 = e16_ref[...]
        dlast128 = _nt_x(jnp.broadcast_to(dlastx, (8, SW)), e16, 2)[0:1, :]
        dcum = dcum + dcum_t.T + _nt_x(dcumx, e16, 2) + jnp.where(rowi == CH - 1, dlast128, 0.0)
        da = _xmm(triu_ref[...], dcum, 2)
        ddt = da * arow + _nt_x(dxh * xs, e16, 2)
        dvec_ref[1:2, :] += jnp.sum(da * dt, axis=0, keepdims=True)
        ddtraw = jnp.where(lane < SH, ddt * _sigmoid(dtpre), 0.0)
        dvec_ref[0:1, :] += jnp.sum(ddtraw, axis=0, keepdims=True)
        ddt_ref[...] = ddtraw.astype(BF16)
        dxs = dxs_skip + dxh * dtx
        dact = jnp.concatenate([dxs] + db_parts + dc_parts, axis=1)
        dxc = dact * (sact * (1.0 + xc * (1.0 - sact)))

        dcb_ref[...] += jnp.sum(dxc, axis=0, keepdims=True)
        dext[0:CH, :] = dxc
        cw = cw_ref[...]
        xraw = xbc_ref[...]
        dxr = cw[CW - 1:CW, :] * dxc
        dcw_ref[CW - 1:CW, :] += jnp.sum(dxc * xraw, axis=0, keepdims=True)
        for kk in range(CW - 1):
            dwin = dext[CW - 1 - kk:CW - 1 - kk + CH, :]
            dxr = dxr + cw[kk:kk + 1, :] * dwin
            dcw_ref[kk:kk + 1, :] += jnp.sum(dwin * xraw, axis=0, keepdims=True)
        dxbc_ref[...] = dxr.astype(BF16)
        dext[CH:CH + 8, :] = dxc[0:8, :]

    ng = nc // NCK
    rev = lambda n: pl.BlockSpec((NCK * CH, n), lambda r: (ng - 1 - r, 0))
    return pl.pallas_call(
        body, name="ssd_bwd", grid=(ng,),
        in_specs=[rev(CC), rev(CC),
                  rev(128), rev(SW), rev(SW), pl.BlockSpec((NCK, SN, SW), lambda r: (ng - 1 - r, 0, 0)), rev(SW),
                  _const((CW, CC)), _const((1, 128)), _const((1, 128)), _const((1, SW)), _const((1, SW)),
                  _const((CH, CH)), _const((CH, CH)), _const((128, SW))],
        out_specs=[rev(CC), rev(128), rev(SW), _full((CW, CC)), _full((1, CC)), _full((8, 128)), _full((1, SW))],
        out_shape=[jax.ShapeDtypeStruct((S, CC), BF16), jax.ShapeDtypeStruct((S, 128), BF16), jax.ShapeDtypeStruct((S, SW), BF16),
                   jax.ShapeDtypeStruct((CW, CC), F32), jax.ShapeDtypeStruct((1, CC), F32),
                   jax.ShapeDtypeStruct((8, 128), F32), jax.ShapeDtypeStruct((1, SW), F32)],
        scratch_shapes=[pltpu.VMEM((SN, SW), F32), pltpu.VMEM((CH + 8, CC), F32), pltpu.VMEM((1, SW), F32)],
        compiler_params=_cp(("arbitrary",)),
    )(xbc, xc, dtraw, zs, y, htp, dossm, conv_w, dtb, alog, dskx, gssm, tri, triu, e16)


def _mla_bwd(dq, dk, dv, qlat, ckv, qg, kvg, wq, wk, wv, pos, invf):
    S = qlat.shape[0]
    tm = min(TQ, S)

    def body(dq_ref, dk_ref, dv_ref, ql_ref, ckv_ref, qg_ref, kvg_ref, wq_ref, wk_ref, wv_ref, pos_ref, invf_ref,
             dql_ref, dckv_ref, dkr_ref, dwq_ref, dwk_ref, dwv_ref, dqg_ref, dkvg_ref):
        i = pl.program_id(0)

        @pl.when(i == 0)
        def _():
            dwq_ref[...] = jnp.zeros_like(dwq_ref)
            dwk_ref[...] = jnp.zeros_like(dwk_ref)
            dwv_ref[...] = jnp.zeros_like(dwv_ref)
            dqg_ref[...] = jnp.zeros_like(dqg_ref)
            dkvg_ref[...] = jnp.zeros_like(dkvg_ref)

        ang = pos_ref[...].astype(F32) * invf_ref[...]
        cs = jnp.cos(ang)
        sn = jnp.sin(ang)

        def rms_bwd(v, g, dn, dg_ref):
            r = lax.rsqrt(jnp.mean(v * v, axis=-1, keepdims=True) + RMS_EPS)
            vh = v * r
            dg_ref[...] += jnp.sum(dn * vh, axis=0, keepdims=True)
            dvh = dn * g
            return vh, r * (dvh - vh * jnp.mean(dvh * vh, axis=-1, keepdims=True))

        pieces = []
        for h in range(NH):
            dqh = dq_ref[h]
            pieces.append(dqh[:, 0:NOPE] * SCALE)
            pieces.append(_rope_t(dqh[:, NOPE:HP], cs, sn) * SCALE)
        dqf = jnp.concatenate(pieces, axis=1).astype(BF16)
        ql = ql_ref[...]
        g = qg_ref[...]
        dqn = _nt(dqf, wq_ref[...])
        qh, dql = rms_bwd(ql, g, dqn, dqg_ref)
        dwq_ref[...] += _tn((qh * g).astype(BF16), dqf)
        dql_ref[...] = dql.astype(BF16)

        dkn_p = jnp.concatenate([dk_ref[h, :, 0:NOPE] for h in range(NH)], axis=1).astype(BF16)
        dvf = jnp.concatenate([dv_ref[h] for h in range(NH)], axis=1).astype(BF16)
        dkr = dk_ref[0, :, NOPE:HP]
        for h in range(1, NH):
            dkr = dkr + dk_ref[h, :, NOPE:HP]
        lane = lax.broadcasted_iota(jnp.int32, dkr.shape, 1)
        dkr_ref[...] = jnp.where(lane < ROPE, _rope_t(dkr, cs, sn), 0.0).astype(BF16)
        cv = ckv_ref[...]
        gk = kvg_ref[...]
        dkn = _nt(dkn_p, wk_ref[...]) + _nt(dvf, wv_ref[...])
        kh, dckv = rms_bwd(cv, gk, dkn, dkvg_ref)
        knb = (kh * gk).astype(BF16)
        dwk_ref[...] += _tn(knb, dkn_p)
        dwv_ref[...] += _tn(knb, dvf)
        dckv_ref[...] = dckv.astype(BF16)

    row = lambda n: pl.BlockSpec((tm, n), lambda i: (i, 0))
    heads = lambda n: pl.BlockSpec((NH, tm, n), lambda i: (0, i, 0))
    return pl.pallas_call(
        body, name="mla_bwd", grid=(S // tm,),
        in_specs=[heads(HP), heads(HP), heads(VD), row(QL), row(KVL), _const((1, QL)), _const((1, KVL)),
                  _const((QL, NH * HP)), _const((KVL, NH * NOPE)), _const((KVL, NH * VD)), row(1), _const((1, 128))],
        out_specs=[row(QL), row(KVL), row(128), _full((QL, NH * HP)), _full((KVL, NH * NOPE)), _full((KVL, NH * VD)),
                   _full((1, QL)), _full((1, KVL))],
        out_shape=[jax.ShapeDtypeStruct((S, QL), BF16), jax.ShapeDtypeStruct((S, KVL), BF16), jax.ShapeDtypeStruct((S, 128), BF16),
                   jax.ShapeDtypeStruct((QL, NH * HP), F32), jax.ShapeDtypeStruct((KVL, NH * NOPE), F32),
                   jax.ShapeDtypeStruct((KVL, NH * VD), F32), jax.ShapeDtypeStruct((1, QL), F32), jax.ShapeDtypeStruct((1, KVL), F32)],
        compiler_params=_cp(("arbitrary",)),
    )(dq, dk, dv, qlat, ckv, qg, kvg, wq, wk, wv, pos, invf)


def _inproj_bwd(x, gx1, mod, win, dql, dckv, dza, dxbc, dzs, dkr, ddt):
    S = x.shape[0]
    tm = min(TM, S)

    def body(x_ref, gx1_ref, mod_ref, win_ref, dql_ref, dckv_ref, dza_ref, dxbc_ref, dzs_ref, dkr_ref, ddt_ref,
             gx_ref, dw_ref, vec_ref):
        i = pl.program_id(0)

        @pl.when(i == 0)
        def _():
            dw_ref[...] = jnp.zeros_like(dw_ref)
            vec_ref[...] = jnp.zeros_like(vec_ref)

        shift = mod_ref[0:1, 0:D]
        scale = mod_ref[0:1, D:2 * D]
        xv = x_ref[...]
        ut = (xv * (1.0 + scale) + shift).T.astype(BF16)
        pieces = (dql_ref, dckv_ref, dza_ref, dxbc_ref, dzs_ref, dkr_ref, ddt_ref)
        du = jnp.zeros((tm, D), F32)
        lo = 0
        for p_ref in pieces:
            n = p_ref.shape[1]
            dp = p_ref[...]
            du = du + _nt(dp, win_ref[:, lo:lo + n])
            dw_ref[:, lo:lo + n] += _mm(ut, dp)
            lo += n
        vec_ref[0:1, :] += jnp.sum(du, axis=0, keepdims=True)
        vec_ref[1:2, :] += jnp.sum(du * xv, axis=0, keepdims=True)
        gx_ref[...] = gx1_ref[...] + du * (1.0 + scale)

    row = lambda n: pl.BlockSpec((tm, n), lambda i: (i, 0))
    return pl.pallas_call(
        body, name="inproj_bwd", grid=(S // tm,),
        in_specs=[row(D), row(D), _const((8, 3 * D)), _const((D, IN_P)), row(QL), row(KVL), row(D), row(CC), row(D),
                  row(128), row(128)],
        out_specs=[row(D), pl.BlockSpec((D, IN_P), lambda i: (0, 0), pipeline_mode=pl.Buffered(1)), _full((8, D))],
        out_shape=[jax.ShapeDtypeStruct((S, D), F32), jax.ShapeDtypeStruct((D, IN_P), F32), jax.ShapeDtypeStruct((8, D), F32)],
        compiler_params=_cp(("arbitrary",)),
    )(x, gx1, mod, win, dql, dckv, dza, dxbc, dzs, dkr, ddt)


def _ada_bwd(callt, dmods):
    w = dmods.shape[1]

    def body(c_ref, d_ref, o_ref):
        acc = c_ref[:, 0:1] * d_ref[0:1, :]
        for s in range(1, 8):
            acc = acc + c_ref[:, s:s + 1] * d_ref[s:s + 1, :]
        o_ref[0] = acc

    return pl.pallas_call(body, name="ada_bwd", out_shape=jax.ShapeDtypeStruct((1, D, w), F32),
                          compiler_params=_cp())(callt, dmods)


def _adamw(name, parts, w, m, v):
    rows, ncol = w.shape
    tr = min(rows, 128)
    nparts = parts.shape[0]

    def body(p_ref, w_ref, m_ref, v_ref, g_ref, d_ref, nm_ref, nv_ref):
        g = p_ref[0].astype(F32)
        for s in range(1, nparts):
            g = g + p_ref[s].astype(F32)
        g_ref[...] = g
        nm = B1 * m_ref[...] + (1.0 - B1) * g
        nv = B2 * v_ref[...] + (1.0 - B2) * (g * g)
        nm_ref[...] = nm
        nv_ref[...] = nv
        m_hat = nm / (1.0 - B1 ** STEP)
        v_hat = nv / (1.0 - B2 ** STEP)
        d_ref[...] = -LR * (m_hat / (jnp.sqrt(v_hat) + EPS) + WD * w_ref[...])

    row = pl.BlockSpec((tr, ncol), lambda i: (i, 0))
    sd = jax.ShapeDtypeStruct((rows, ncol), F32)
    return pl.pallas_call(
        body, name="adamw_" + name, grid=(rows // tr,),
        in_specs=[pl.BlockSpec((nparts, tr, ncol), lambda i: (0, i, 0)), row, row, row],
        out_specs=[row, row, row, row], out_shape=[sd, sd, sd, sd],
        compiler_params=_cp(("arbitrary",)),
    )(parts, w, m, v)


_SMALL = (("b_ada", 3 * D), ("conv_w", CW * CC // 4), ("conv_b", CC), ("ssm_norm_g", SW), ("ln_g", D), ("ln_b", D),
          ("q_norm_g", QL), ("kv_norm_g", KVL), ("dt_bias", SH), ("a_log", SH), ("d_skip", SH), ("loss", 128))


def _pack_small(d, lead):
    flat = [d[name].reshape(d[name].shape[:lead] + (-1,)) for name, _ in _SMALL]
    used = sum(f.shape[lead] for f in flat)
    pad = jnp.zeros(flat[0].shape[:lead] + (R_SMALL * 1024 - used,), F32)
    return jnp.concatenate(flat + [pad], axis=lead).reshape(flat[0].shape[:lead] + (R_SMALL, 1024))


def _unpack_small(p):
    flat = p.reshape(-1)
    out, r = {}, 0
    for name, n in _SMALL:
        out[name] = flat[r:r + n]
        r += n
    return out


def _in_to_padded(w):
    z = lambda n: jnp.zeros((w.shape[0], n), w.dtype)
    return jnp.concatenate([w[:, 0:384], w[:, 384:640], w[:, 704:1728], w[:, 1728:3264], w[:, 3280:4304],
                            w[:, 640:704], z(64), w[:, 3264:3280], z(112)], axis=1)


def _in_from_padded(g):
    return jnp.concatenate([g[:, 0:384], g[:, 384:640], g[:, P_KR[0]:P_KR[0] + 64], g[:, 640:1664], g[:, 1664:3200],
                            g[:, P_DT[0]:P_DT[0] + 16], g[:, 3200:4224]], axis=1)


def kernel(x, c, positions, w_ada, b_ada, w_in, q_norm_g, w_qb, kv_norm_g, w_kvb, conv_w, conv_b, dt_bias, a_log, d_skip, ssm_norm_g, w_out, ln_g, ln_b, loss_target, m_w_ada, m_b_ada, m_w_in, m_q_norm_g, m_w_qb, m_kv_norm_g, m_w_kvb, m_conv_w, m_conv_b, m_dt_bias, m_a_log, m_d_skip, m_ssm_norm_g, m_w_out, m_ln_g, m_ln_b, v_w_ada, v_b_ada, v_w_in, v_q_norm_g, v_w_qb, v_kv_norm_g, v_w_kvb, v_conv_w, v_conv_b, v_dt_bias, v_a_log, v_d_skip, v_ssm_norm_g, v_w_out, v_ln_g, v_ln_b):
    S = x.shape[1]
    xv = x[0]
    tgt = loss_target[0]

    cw16 = jnp.concatenate([conv_w[0], jnp.zeros((16 - CW, CC // 4), F32)], axis=0)
    f_in, f_qb, f_kvb, f_out, f_cw = _gather_weights(
        [w_in[0].astype(BF16), w_qb[0].astype(BF16), w_kvb[0].astype(BF16), w_out[0].astype(BF16), cw16])
    cat1 = lambda f: jnp.concatenate([f[k] for k in range(4)], axis=1)
    win = _in_to_padded(cat1(f_in))
    wqb = cat1(f_qb).reshape(QL, NH, QKD)
    wq = jnp.concatenate([wqb, jnp.zeros((QL, NH, HP - QKD), BF16)], axis=2).reshape(QL, NH * HP)
    wkvb = cat1(f_kvb).reshape(KVL, NH, NOPE + VD)
    wk = wkvb[:, :, 0:NOPE].reshape(KVL, NH * NOPE)
    wv = wkvb[:, :, NOPE:].reshape(KVL, NH * VD)
    wout = f_out.reshape(MIX, D)
    cwf = cat1(f_cw[:, 0:CW, :])

    half = ROPE // 2
    invf = 1.0 / (ROPE_THETA ** (jnp.arange(half, dtype=F32) / half))
    invf = jnp.concatenate([invf, invf, jnp.zeros((128 - ROPE,), F32)]).reshape(1, 128)
    pos = positions.reshape(S, 1)
    pad128 = lambda a: jnp.concatenate([a.reshape(1, SH), jnp.zeros((1, 128 - SH), F32)], axis=1)
    dtb, alog = pad128(dt_bias), pad128(a_log)
    dskx = jnp.repeat(d_skip.reshape(SH), SP).reshape(1, SW)

    my_c = lax.axis_index("c")
    (call,) = _exchange("gather_c", [jnp.broadcast_to(c.reshape(1, 1, D), (4, 1, D))])
    call = call.reshape(8, D)
    mods = _ada(call, w_ada[0])
    (mrows,) = _exchange("scatter_mod", [mods.reshape(4, 2, 3 * D // 4)])
    mine = lax.dynamic_index_in_dim(mrows.reshape(4, 2, 2, 3 * D // 4)[:, 0], my_c, axis=1, keepdims=False)
    mod = jnp.broadcast_to(mine.reshape(1, 3 * D) + b_ada, (8, 3 * D))
    qlat, ckv, za, xbc, zs, dtraw, q, k, v = _inproj(xv, mod, win, q_norm_g, kv_norm_g, wq, wk, wv, pos, invf)
    o, lse = _attn_fwd(q, k, v)
    xc, y, htp, ossm = _ssd_fwd(xbc, dtraw, zs, cwf, conv_b, dtb, alog, dskx, ssm_norm_g)
    gx1, do, dza, dossm, delta, dwout, vec_o = _outproj(o, za, ossm, xv, tgt, wout, mod, ln_g, ln_b)
    loss_part = jnp.zeros((128,), F32).at[0].set(0.5 / D * jnp.sum(vec_o[0]))

    dk, dv, dq = _attn_bwd(q, k, v, do, lse, delta)
    dxbc, ddt, dzs, dcw, dcb, dvec, dgssm = _ssd_bwd(xbc, xc, dtraw, zs, y, htp, dossm, cwf, dtb, alog, dskx, ssm_norm_g)
    dql, dckv, dkr, dwq, dwk, dwv, dqg, dkvg = _mla_bwd(dq, dk, dv, qlat, ckv, q_norm_g, kv_norm_g, wq, wk, wv, pos, invf)
    gx, dwin, vec_i = _inproj_bwd(xv, gx1, mod, win, dql, dckv, dza, dxbc, dzs, dkr, ddt)
    dmod = jnp.concatenate([vec_i[0:1], vec_i[1:2], vec_o[3:4]], axis=1)

    cols = lambda g: g.reshape(g.shape[0], 4, g.shape[1] // 4).transpose(1, 0, 2)
    g_in = cols(_in_from_padded(dwin)).astype(BF16)
    g_qb = cols(dwq.reshape(QL, NH, HP)[:, :, 0:QKD].reshape(QL, NH * QKD)).astype(BF16)
    g_kvb = cols(jnp.concatenate([dwk.reshape(KVL, NH, NOPE), dwv.reshape(KVL, NH, VD)], axis=2)
                 .reshape(KVL, NH * (NOPE + VD))).astype(BF16)
    g_out = dwout.reshape(4, MIX // 4, D).astype(BF16)
    small = {"b_ada": dmod, "conv_b": dcb, "ssm_norm_g": dgssm, "ln_g": vec_o[1:2], "ln_b": vec_o[2:3],
             "q_norm_g": dqg, "kv_norm_g": dkvg, "dt_bias": dvec[0:1, 0:SH], "a_log": dvec[1:2, 0:SH], "d_skip": dvec[2:3, 0:SH],
             "loss": loss_part}
    small = {n: jnp.broadcast_to(a.reshape(1, -1), (4, a.size)) for n, a in small.items()}
    small["conv_w"] = cols(dcw).reshape(4, CW * CC // 4)
    gsmall = _pack_small(small, 1)

    r_in, r_qb, r_kvb, r_out, rs, dmods = _exchange(
        "exchange_grads", [g_in, g_qb, g_kvb, g_out, gsmall, jnp.broadcast_to(dmod.reshape(1, 1, 3 * D), (4, 1, 3 * D))])
    chip = 2 * lax.axis_index("x") + lax.axis_index("y")
    dmods = lax.dynamic_slice_in_dim(dmods.reshape(8, 3 * D), chip * (3 * D // 4), 3 * D // 4, axis=1)
    g_ada = _ada_bwd(call.T, dmods)
    res = dict(w_ada=_adamw("w_ada", g_ada, w_ada[0], m_w_ada[0], v_w_ada[0]),
               w_in=_adamw("w_in", r_in, w_in[0], m_w_in[0], v_w_in[0]),
               w_qb=_adamw("w_qb", r_qb, w_qb[0], m_w_qb[0], v_w_qb[0]),
               w_kvb=_adamw("w_kvb", r_kvb, w_kvb[0], m_w_kvb[0], v_w_kvb[0]),
               w_out=_adamw("w_out", r_out, w_out[0], m_w_out[0], v_w_out[0]))
    wsm = _pack_small(dict(b_ada=b_ada, conv_w=conv_w, conv_b=conv_b, ssm_norm_g=ssm_norm_g, ln_g=ln_g, ln_b=ln_b,
                           q_norm_g=q_norm_g, kv_norm_g=kv_norm_g, dt_bias=dt_bias, a_log=a_log, d_skip=d_skip, loss=jnp.zeros((128,), F32)), 0)
    msm = _pack_small(dict(b_ada=m_b_ada, conv_w=m_conv_w, conv_b=m_conv_b, ssm_norm_g=m_ssm_norm_g, ln_g=m_ln_g, ln_b=m_ln_b,
                           q_norm_g=m_q_norm_g, kv_norm_g=m_kv_norm_g, dt_bias=m_dt_bias, a_log=m_a_log, d_skip=m_d_skip, loss=jnp.zeros((128,), F32)), 0)
    vsm = _pack_small(dict(b_ada=v_b_ada, conv_w=v_conv_w, conv_b=v_conv_b, ssm_norm_g=v_ssm_norm_g, ln_g=v_ln_g, ln_b=v_ln_b,
                           q_norm_g=v_q_norm_g, kv_norm_g=v_kv_norm_g, dt_bias=v_dt_bias, a_log=v_a_log, d_skip=v_d_skip, loss=jnp.zeros((128,), F32)), 0)
    sm = _adamw("small", rs, wsm, msm, vsm)

    order = ["w_ada", "b_ada", "w_in", "q_norm_g", "w_qb", "kv_norm_g", "w_kvb", "conv_w", "conv_b", "dt_bias", "a_log",
             "d_skip", "ssm_norm_g", "w_out", "ln_g", "ln_b"]
    shapes = dict(w_ada=w_ada.shape, b_ada=b_ada.shape, w_in=w_in.shape, q_norm_g=q_norm_g.shape, w_qb=w_qb.shape,
                  kv_norm_g=kv_norm_g.shape, w_kvb=w_kvb.shape, conv_w=conv_w.shape, conv_b=conv_b.shape, dt_bias=dt_bias.shape,
                  a_log=a_log.shape, d_skip=d_skip.shape, ssm_norm_g=ssm_norm_g.shape, w_out=w_out.shape, ln_g=ln_g.shape,
                  ln_b=ln_b.shape)
    outs = []
    for kind in range(4):
        d = _unpack_small(sm[kind])
        d.update({n: r[kind] for n, r in res.items()})
        outs.extend(d[n].reshape(shapes[n]) for n in order)
    loss = _unpack_small(sm[0])["loss"][0]
    return (loss, gx.reshape(x.shape), *outs)
```

```python
import functools
import math

import numpy as np
import jax
import jax.numpy as jnp
from jax import lax
from jax.experimental import pallas as pl
from jax.experimental.pallas import tpu as pltpu

F32 = jnp.float32
BF16 = jnp.bfloat16
MESH_ID = pl.DeviceIdType.MESH

D = 1024
NH = 8
NOPE = 128
ROPE = 64
VD = 128
VDP = 144
QKD = NOPE + ROPE
HP = 256
QL = 384
KVL = 256
ROPE_THETA = 10000.0
SH = 16
SP = 64
SG = 2
SN = 128
CW = 4
CH = 128
SW = SH * SP
CC = SW + 2 * SG * SN
GW = SW // SG
MIX = 2 * D
IN_W = 4304
ALPHA = 2.0 ** 0.25
RMS_EPS = 1e-6
LN_EPS = 1e-5
SCALE = QKD ** -0.5
LN2 = math.log(2.0)
QSCALE = SCALE / LN2
LR, B1, B2, EPS, WD, STEP = 0.001, 0.9, 0.999, 1e-08, 0.01, 10

P_Q = (0, 384)
P_KV = (384, 640)
P_ZA = (640, 1664)
P_XBC = (1664, 3200)
P_ZS = (3200, 4224)
P_KR = (4224, 4352)
P_DT = (4352, 4480)
IN_P = 4480

R_SMALL = 16

TM = 256
TQ = 512
NSP = 2
NCK = 4
TQF = 1024
VMEM_LIMIT = 56 * 1024 * 1024


def _cp(sem=None):
    return pltpu.CompilerParams(dimension_semantics=sem, vmem_limit_bytes=VMEM_LIMIT)


def _mm(a, b):
    return jnp.dot(a, b, preferred_element_type=F32)


def _nt(a, b):
    return lax.dot_general(a, b, (((1,), (1,)), ((), ())), preferred_element_type=F32)


def _tn(a, b):
    return lax.dot_general(a, b, (((0,), (0,)), ((), ())), preferred_element_type=F32)


def _split(a, terms):
    parts = []
    for t in range(terms):
        p = a.astype(BF16)
        parts.append(p)
        if t + 1 < terms:
            a = a - p.astype(F32)
    return parts


def _mm_x(a, ones, terms):
    parts = _split(a, terms)
    out = _mm(parts[0], ones)
    for p in parts[1:]:
        out = out + _mm(p, ones)
    return out


def _xmm(ones, a, terms):
    parts = _split(a, terms)
    out = _mm(ones, parts[0])
    for p in parts[1:]:
        out = out + _mm(ones, p)
    return out


def _nt_x(a, ones, terms):
    parts = _split(a, terms)
    out = _nt(parts[0], ones)
    for p in parts[1:]:
        out = out + _nt(p, ones)
    return out


def _sigmoid(z):
    return 1.0 / (1.0 + jnp.exp(-z))


def _softplus(z):
    return jnp.maximum(z, 0.0) + jnp.log1p(jnp.exp(-jnp.abs(z)))


def _rope(t, cs, sn):
    lane = lax.broadcasted_iota(jnp.int32, t.shape, 1)
    rot = jnp.where(lane < ROPE // 2, -pltpu.roll(t, 128 - ROPE // 2, 1), pltpu.roll(t, ROPE // 2, 1))
    return t * cs + rot * sn


def _rope_t(t, cs, sn):
    lane = lax.broadcasted_iota(jnp.int32, t.shape, 1)
    y = t * sn
    rot = jnp.where(lane < ROPE // 2, -pltpu.roll(y, 128 - ROPE // 2, 1), pltpu.roll(y, ROPE // 2, 1))
    return t * cs - rot


def _full(shape):
    n = len(shape)
    return pl.BlockSpec(shape, lambda *_: (0,) * n)


def _const(shape):
    n = len(shape)
    return pl.BlockSpec(shape, lambda *_: (0,) * n, pipeline_mode=pl.Buffered(1))


def _gather_weights(shards):
    n = len(shards)
    halves = [a.shape[0] // 2 for a in shards]

    def body(*refs):
        srcs, dsts = refs[:n], refs[n:2 * n]
        send_sems, recv_sems, local_sems = refs[2 * n:]
        x, y, c = lax.axis_index("x"), lax.axis_index("y"), lax.axis_index("c")
        me = 2 * x + y
        sibling = (x, y, 1 - c)
        chips = [(1 - x, y), (x, 1 - y), (1 - x, 1 - y)]

        def rows(a, pc):
            return pl.ds(pl.multiple_of(pc * halves[a], halves[a]), halves[a])

        def copy(a, k, src, slot, pc, to):
            return pltpu.make_async_remote_copy(
                src_ref=src, dst_ref=dsts[a].at[slot, rows(a, pc)], send_sem=send_sems.at[a, k],
                recv_sem=recv_sems.at[a, k], device_id=to, device_id_type=MESH_ID)

        local = [pltpu.make_async_copy(srcs[a], dsts[a].at[me], local_sems.at[a]) for a in range(n)]
        for cp in local:
            cp.start()
        sends = [copy(a, j, srcs[a].at[rows(a, c)], me, c, (px, py, c)) for a in range(n) for j, (px, py) in enumerate(chips)]
        for cp in sends:
            cp.start()
        passed = []
        for a in range(n):
            for j, (px, py) in enumerate(chips):
                k = 2 * px + py
                copy(a, j, srcs[a].at[rows(a, c)], k, c, (x, y, c)).wait_recv()
                fwd = copy(a, 3 + j, dsts[a].at[k, rows(a, c)], k, c, sibling)
                fwd.start()
                passed.append(fwd)
        for a in range(n):
            for j, (px, py) in enumerate(chips):
                copy(a, 3 + j, srcs[a].at[rows(a, c)], 2 * px + py, 1 - c, (x, y, c)).wait_recv()
        for cp in sends + passed:
            cp.wait_send()
        for cp in local:
            cp.wait()

    hbm = pl.BlockSpec(memory_space=pltpu.HBM)
    return pl.pallas_call(
        body, name="gather_weights",
        out_shape=tuple(jax.ShapeDtypeStruct((4,) + a.shape, a.dtype) for a in shards),
        in_specs=[hbm] * n, out_specs=tuple([hbm] * n),
        scratch_shapes=[pltpu.SemaphoreType.DMA((n, 6)), pltpu.SemaphoreType.DMA((n, 6)), pltpu.SemaphoreType.DMA((n,))],
    )(*shards)


def _exchange(name, slabs):
    n = len(slabs)

    def body(*refs):
        srcs, dsts = refs[:n], refs[n:2 * n]
        send_sems, recv_sems, local_sems = refs[2 * n:]
        x, y, c = lax.axis_index("x"), lax.axis_index("y"), lax.axis_index("c")
        chip = 2 * x + y
        sibling = (x, y, 1 - c)
        chips = [(1 - x, y), (x, 1 - y), (1 - x, 1 - y)]

        def slot(px, py, pc):
            return 4 * px + 2 * py + pc

        def copy(a, k, src, s, to):
            return pltpu.make_async_remote_copy(
                src_ref=src, dst_ref=dsts[a].at[s], send_sem=send_sems.at[a, k], recv_sem=recv_sems.at[a, k],
                device_id=to, device_id_type=MESH_ID)

        mine = slot(x, y, c)
        local = [pltpu.make_async_copy(srcs[a].at[chip], dsts[a].at[mine], local_sems.at[a]) for a in range(n)]
        for cp in local:
            cp.start()
        first = []
        for a in range(n):
            first.append(copy(a, 0, srcs[a].at[chip], mine, sibling))
            for j, (px, py) in enumerate(chips):
                first.append(copy(a, 1 + j, srcs[a].at[2 * px + py], mine, (px, py, c)))
        for cp in first:
            cp.start()
        passed = []
        for a in range(n):
            for j, (px, py) in enumerate(chips):
                s = slot(px, py, c)
                copy(a, 1 + j, srcs[a].at[chip], s, (x, y, c)).wait_recv()
                fwd = copy(a, 4 + j, dsts[a].at[s], s, sibling)
                fwd.start()
                passed.append(fwd)
        for a in range(n):
            copy(a, 0, srcs[a].at[chip], slot(x, y, 1 - c), (x, y, c)).wait_recv()
            for j, (px, py) in enumerate(chips):
                copy(a, 4 + j, srcs[a].at[chip], slot(px, py, 1 - c), (x, y, c)).wait_recv()
        for cp in first + passed:
            cp.wait_send()
        for cp in local:
            cp.wait()

    hbm = pl.BlockSpec(memory_space=pltpu.HBM)
    return pl.pallas_call(
        body, name=name,
        out_shape=tuple(jax.ShapeDtypeStruct((8,) + a.shape[1:], a.dtype) for a in slabs),
        in_specs=[hbm] * n, out_specs=tuple([hbm] * n),
        scratch_shapes=[pltpu.SemaphoreType.DMA((n, 7)), pltpu.SemaphoreType.DMA((n, 7)), pltpu.SemaphoreType.DMA((n,))],
    )(*slabs)


def _ada(call, w_shard):
    def body(c_ref, w_ref, o_ref):
        o_ref[...] = _mm(c_ref[...].astype(BF16), w_ref[...].astype(BF16))

    return pl.pallas_call(body, name="ada", out_shape=jax.ShapeDtypeStruct((8, w_shard.shape[1]), F32),
                          compiler_params=_cp())(call, w_shard)


def _inproj(x, mod, win, qg, kvg, wq, wk, wv, pos, invf):
    S = x.shape[0]
    tm = min(TM, S)

    def body(x_ref, mod_ref, win_ref, qg_ref, kvg_ref, wq_ref, wk_ref, wv_ref, pos_ref, invf_ref,
             qlat_ref, ckv_ref, za_ref, xbc_ref, zs_ref, dt_ref, q_ref, k_ref, v_ref):
        shift = mod_ref[0:1, 0:D]
        scale = mod_ref[0:1, D:2 * D]
        u = (x_ref[...] * (1.0 + scale) + shift).astype(BF16)

        def proj(p):
            return _mm(u, win_ref[:, p[0]:p[1]])

        ql = proj(P_Q)
        ckv = proj(P_KV)
        qlat_ref[...] = ql
        ckv_ref[...] = ckv
        za_ref[...] = proj(P_ZA)
        xbc_ref[...] = proj(P_XBC)
        zs_ref[...] = proj(P_ZS)
        dt_ref[...] = proj(P_DT)
        kr = proj(P_KR)

        ang = pos_ref[...].astype(F32) * invf_ref[...]
        cs = jnp.cos(ang)
        sn = jnp.sin(ang)

        rq = lax.rsqrt(jnp.mean(ql * ql, axis=-1, keepdims=True) + RMS_EPS)
        qn = (ql * rq * qg_ref[...]).astype(BF16)
        for h in range(NH):
            qh = _mm(qn, wq_ref[:, h * HP:(h + 1) * HP])
            q_ref[h, :, 0:NOPE] = (qh[:, 0:NOPE] * QSCALE).astype(BF16)
            q_ref[h, :, NOPE:HP] = (_rope(qh[:, NOPE:HP], cs, sn) * QSCALE).astype(BF16)

        rk = lax.rsqrt(jnp.mean(ckv * ckv, axis=-1, keepdims=True) + RMS_EPS)
        kn = (ckv * rk * kvg_ref[...]).astype(BF16)
        knope = _mm(kn, wk_ref[...])
        vall = _mm(kn, wv_ref[...])
        krf = _rope(kr, cs, sn)
        krr = krf.astype(BF16)
        for h in range(NH):
            k_ref[h, :, 0:NOPE] = knope[:, h * NOPE:(h + 1) * NOPE].astype(BF16)
            k_ref[h, :, NOPE:HP] = krr
            v_ref[h] = vall[:, h * VD:(h + 1) * VD].astype(BF16)

    row = lambda n: pl.BlockSpec((tm, n), lambda i: (i, 0))
    heads = lambda n: pl.BlockSpec((NH, tm, n), lambda i: (0, i, 0))
    sd = lambda n: jax.ShapeDtypeStruct((S, n), F32)
    hd = lambda n: jax.ShapeDtypeStruct((NH, S, n), BF16)
    return pl.pallas_call(
        body, name="inproj", grid=(S // tm,),
        in_specs=[row(D), _const((8, 3 * D)), _const((D, IN_P)), _const((1, QL)), _const((1, KVL)),
                  _const((QL, NH * HP)), _const((KVL, NH * NOPE)), _const((KVL, NH * VD)),
                  row(1), _const((1, 128))],
        out_specs=[row(QL), row(KVL), row(D), row(CC), row(D), row(128), heads(HP), heads(HP), heads(VD)],
        out_shape=[sd(QL), sd(KVL), sd(D), sd(CC), sd(D), sd(128), hd(HP), hd(HP), hd(VD)],
        compiler_params=_cp(("arbitrary",)),
    )(x, mod, win, qg, kvg, wq, wk, wv, pos, invf)


def _attn_fwd(q, k, v):
    _, S, _ = q.shape
    tq = min(TQF, S)
    nq = S // tq
    half = tq // NSP
    tb = min(256, half)
    nsb = half // tb

    def body(q_ref, k_ref, v_ref, o_ref, lse_ref, vt_ref):
        i = pl.program_id(1)
        qb = q_ref[...]

        @pl.when(i == 0)
        def _():
            ones_rows = jnp.where(lax.broadcasted_iota(jnp.int32, (VDP - VD, tb), 0) == 0, 1.0, 0.0).astype(BF16)

            def fill(blk, carry):
                off = pl.multiple_of(blk * tb, tb)
                vt_ref[blk, 0:VD, :] = v_ref[pl.ds(off, tb), :].T
                vt_ref[blk, VD:VDP, :] = ones_rows
                return carry

            lax.fori_loop(0, S // tb, fill, 0)

        def scores(j, hb):
            off = pl.multiple_of(j * tq + hb * half, half)
            return _nt(k_ref[pl.ds(off, half), :], qb)

        def update(j, hb, s, carry):
            m, acc = carry
            m_new = jnp.maximum(m, jnp.max(s, axis=0, keepdims=True))
            a = jnp.exp2(m - m_new)
            pb = jnp.exp2(s - m_new).astype(BF16)
            acc = a * acc
            for sb in range(nsb):
                acc = acc + _mm(vt_ref[(NSP * j + hb) * nsb + sb], pb[sb * tb:(sb + 1) * tb, :])
            return m_new, acc

        def trip(j, carry, masked):
            s = [scores(j, hb) for hb in range(NSP)]
            if masked:
                r = lax.broadcasted_iota(jnp.int32, s[0].shape, 0)
                cidx = lax.broadcasted_iota(jnp.int32, s[0].shape, 1)
                s = [jnp.where(cidx >= r + hb * half, s[hb], -1e30) for hb in range(NSP)]
            for hb in range(NSP):
                carry = update(j, hb, s[hb], carry)
            return carry

        def finish(carry):
            m, acc = carry
            l = acc[VD:VD + 1, :]
            o_ref[...] = (acc[0:VD, :] / l).T
            lse_ref[...] = m + jnp.log2(l)

        def quad(t, cr):
            for u in range(4):
                cr = trip(4 * t + u, cr, False)
            return cr

        init = (jnp.full((1, tq), -1e30, F32), jnp.zeros((VDP, tq), F32))
        carry = lax.fori_loop(0, i >> 2, quad, init)
        carry = lax.fori_loop(i - (i & 3), i, lambda j, cr: trip(j, cr, False), carry)
        finish(trip(i, carry, True))

    return pl.pallas_call(
        body, name="attn_fwd", grid=(NH, nq),
        in_specs=[pl.BlockSpec((None, tq, HP), lambda h, i: (h, i, 0)),
                  pl.BlockSpec((None, S, HP), lambda h, i: (h, 0, 0)),
                  pl.BlockSpec((None, S, VD), lambda h, i: (h, 0, 0))],
        out_specs=[pl.BlockSpec((tq, VD), lambda h, i: (i, h)),
                   pl.BlockSpec((None, None, 1, tq), lambda h, i: (h, i, 0, 0))],
        out_shape=[jax.ShapeDtypeStruct((S, NH * VD), F32), jax.ShapeDtypeStruct((NH, nq, 1, tq), F32)],
        scratch_shapes=[pltpu.VMEM((S // tb, VDP, tb), BF16)],
        compiler_params=_cp(("arbitrary", "arbitrary")),
    )(q, k, v)


def _ssd_consts():
    tri = np.tril(np.ones((CH, CH), np.float32))
    e16 = np.zeros((128, SW), np.float32)
    for h in range(SH):
        e16[h, h * SP:(h + 1) * SP] = 1.0
    return jnp.asarray(tri, BF16), jnp.asarray(tri.T.copy(), BF16), jnp.asarray(e16, BF16)


def _ssd_conv(xraw, halo, cw_ref, cb_ref, ext):
    ext[0:8, :] = halo
    ext[8:8 + CH, :] = xraw
    cw = cw_ref[...]
    xc = cb_ref[...] + cw[0:1, :] * ext[5:5 + CH, :]
    for kk in range(1, CW):
        xc = xc + cw[kk:kk + 1, :] * ext[5 + kk:5 + kk + CH, :]
    return xc


def _ssd_chunk_common(xc, dtraw_ref, dtb_ref, alog_ref, tri_ref, e16_ref):
    sact = _sigmoid(xc)
    act = xc * sact
    lane = lax.broadcasted_iota(jnp.int32, (1, 128), 1)
    arow = jnp.where(lane < SH, -jnp.exp(alog_ref[...]), 0.0)
    dtpre = dtraw_ref[...] + dtb_ref[...]
    dt = _softplus(dtpre)
    a = dt * arow
    cum = _xmm(tri_ref[...], a, 3)
    cumx = _mm_x(cum, e16_ref[...], 3)
    dtx = _mm_x(dt, e16_ref[...], 2)
    return sact, act, arow, dtpre, dt, cum, cumx, dtx


def _ssd_fwd(xbc, dtraw, zs, conv_w, conv_b, dtb, alog, dskx, gssm):
    S = xbc.shape[0]
    nc = S // CH
    tri, _, e16 = _ssd_consts()

    def body(xbc_ref, halo_ref, dtraw_ref, zs_ref, cw_ref, cb_ref, dtb_ref, alog_ref, dsk_ref, g_ref, tri_ref, e16_ref,
             xc_ref, y_ref, htp_ref, o_ref, ht, ext):
        i = pl.program_id(0)

        @pl.when(i == 0)
        def _():
            ht[...] = jnp.zeros_like(ht)

        for c in range(NCK):
            chunk(c, i, xbc_ref, halo_ref, dtraw_ref, zs_ref, cw_ref, cb_ref, dtb_ref, alog_ref, dsk_ref, g_ref, tri_ref,
                  e16_ref, xc_ref, y_ref, htp_ref, o_ref, ht, ext)

    def chunk(c, i, xbc_ref, halo_ref, dtraw_ref, zs_ref, cw_ref, cb_ref, dtb_ref, alog_ref, dsk_ref, g_ref, tri_ref, e16_ref,
              xc_ref, y_ref, htp_ref, o_ref, ht, ext):
        rows = slice(c * CH, (c + 1) * CH)
        halo = jnp.where(i == 0, 0.0, halo_ref[...]) if c == 0 else xbc_ref[c * CH - 8:c * CH, :]
        xc = _ssd_conv(xbc_ref[rows, :], halo, cw_ref, cb_ref, ext)
        xc_ref[rows, :] = xc
        sact, act, arow, dtpre, dt, cum, cumx, dtx = _ssd_chunk_common(xc, dtraw_ref.at[rows, :], dtb_ref, alog_ref, tri_ref, e16_ref)
        cum_t = cum.T
        xs = act[:, 0:SW]
        lastx = cumx[CH - 1:CH, :]
        xh = xs * dtx
        eexp = jnp.exp(cumx)
        dte = jnp.exp(lastx - cumx)
        cdx = jnp.exp(lastx)
        htp = ht[...]
        htp_ref[c] = htp
        xw = (xh * dte).astype(BF16)
        xb = xh.astype(BF16)
        trim = tri_ref[...].astype(F32) > 0.5
        lane = lax.broadcasted_iota(jnp.int32, (CH, 128), 1)
        parts = []
        for g in range(SG):
            gl = slice(g * GW, (g + 1) * GW)
            bg = act[:, SW + g * SN:SW + (g + 1) * SN].astype(BF16)
            cg = act[:, SW + SG * SN + g * SN:SW + SG * SN + (g + 1) * SN].astype(BF16)
            cbm = _nt(cg, bg)
            yoff = eexp[:, gl] * _mm(cg, htp[:, gl].astype(BF16))
            ht[:, gl] = htp[:, gl] * cdx[:, gl] + _tn(bg, xw[:, gl])
            for pr in range(GW // 128):
                h0 = g * (SH // SG) + 2 * pr
                lo = g * GW + pr * 128
                xp = xb[:, lo:lo + 128]
                res = []
                for hh in (h0, h0 + 1):
                    seg = cum[:, hh:hh + 1] - cum_t[hh:hh + 1, :]
                    mh = jnp.where(trim, cbm * jnp.exp(seg), 0.0).astype(BF16)
                    res.append(_mm(mh, xp))
                parts.append(jnp.where(lane < SP, res[0], res[1]) + yoff[:, pr * 128:(pr + 1) * 128])
        y = jnp.concatenate(parts, axis=1) + xs * dsk_ref[...]
        y_ref[rows, :] = y
        z = zs_ref[rows, :]
        hf = y * (z * _sigmoid(z))
        outs = []
        for g in range(SG):
            hg = hf[:, g * GW:(g + 1) * GW]
            rs = lax.rsqrt(jnp.mean(hg * hg, axis=-1, keepdims=True) + RMS_EPS)
            outs.append(hg * rs)
        o_ref[rows, :] = (jnp.concatenate(outs, axis=1) * g_ref[...]).astype(BF16)

    row = lambda n: pl.BlockSpec((NCK * CH, n), lambda i: (i, 0))
    return pl.pallas_call(
        body, name="ssd_fwd", grid=(nc // NCK,),
        in_specs=[row(CC), pl.BlockSpec((8, CC), lambda i: (jnp.maximum(i * (NCK * CH // 8) - 1, 0), 0)), row(128), row(SW),
                  _const((CW, CC)), _const((1, CC)), _const((1, 128)), _const((1, 128)), _const((1, SW)), _const((1, SW)),
                  _const((CH, CH)), _const((128, SW))],
        out_specs=[row(CC), row(SW), pl.BlockSpec((NCK, SN, SW), lambda i: (i, 0, 0)), row(SW)],
        out_shape=[jax.ShapeDtypeStruct((S, CC), F32), jax.ShapeDtypeStruct((S, SW), F32),
                   jax.ShapeDtypeStruct((nc, SN, SW), F32), jax.ShapeDtypeStruct((S, SW), BF16)],
        scratch_shapes=[pltpu.VMEM((SN, SW), F32), pltpu.VMEM((8 + CH, CC), F32)],
        compiler_params=_cp(("arbitrary",)),
    )(xbc, xbc, dtraw, zs, conv_w, conv_b, dtb, alog, dskx, gssm, tri, e16)


def _outproj(o, za, ossm, x, tgt, wout, mod, ln_g, ln_b):
    S = x.shape[0]
    tm = min(TM, S)

    e8 = np.zeros((D, 128), np.float32)
    for h in range(NH):
        e8[h * VD:(h + 1) * VD, h] = 1.0
    e8 = jnp.asarray(e8, BF16)

    def body(o_ref, za_ref, os_ref, x_ref, t_ref, w_ref, mod_ref, g_ref, b_ref, e8_ref,
             gx_ref, do_ref, dza_ref, dos_ref, delta_ref, dw_ref, vec_ref):
        i = pl.program_id(0)

        @pl.when(i == 0)
        def _():
            dw_ref[...] = jnp.zeros_like(dw_ref)
            vec_ref[...] = jnp.zeros_like(vec_ref)

        gate = mod_ref[0:1, 2 * D:3 * D]
        ov = o_ref[...]
        z = za_ref[...]
        sz = _sigmoid(z)
        silz = z * sz
        a = (ov * silz).astype(BF16)
        osb = os_ref[...]
        mixed = _mm(a, w_ref[0:D, :]) + _mm(osb, w_ref[D:MIX, :])
        xv = x_ref[...]
        hres = ALPHA * xv + gate * mixed
        mu = jnp.mean(hres, axis=-1, keepdims=True)
        hc = hres - mu
        var = jnp.mean(hc * hc, axis=-1, keepdims=True)
        rstd = lax.rsqrt(var + LN_EPS)
        xhat = hc * rstd
        g = g_ref[...]
        yv = xhat * g + b_ref[...]
        err = yv - t_ref[...]
        dy = err * (1.0 / D)
        vec_ref[0:1, :] += jnp.sum(err * err, axis=0, keepdims=True)
        vec_ref[1:2, :] += jnp.sum(dy * xhat, axis=0, keepdims=True)
        vec_ref[2:3, :] += jnp.sum(dy, axis=0, keepdims=True)
        dxh = dy * g
        dh = rstd * (dxh - jnp.mean(dxh, axis=-1, keepdims=True) - xhat * jnp.mean(dxh * xhat, axis=-1, keepdims=True))
        gx_ref[...] = ALPHA * dh
        vec_ref[3:4, :] += jnp.sum(dh * mixed, axis=0, keepdims=True)
        dmixed = (gate * dh).astype(BF16)
        dw_ref[0:D, :] += _tn(a, dmixed)
        dw_ref[D:MIX, :] += _tn(osb, dmixed)
        da = _nt(dmixed, w_ref[0:D, :])
        dos_ref[...] = _nt(dmixed, w_ref[D:MIX, :])
        dov = da * silz
        do_ref[...] = dov.astype(BF16)
        dza_ref[...] = (da * ov * (sz * (1.0 + z * (1.0 - sz)))).astype(BF16)
        delta_ref[:, 0, :] = _mm_x(dov * ov, e8_ref[...], 2).T[0:NH, :]

    row = lambda n: pl.BlockSpec((tm, n), lambda i: (i, 0))
    return pl.pallas_call(
        body, name="outproj", grid=(S // tm,),
        in_specs=[row(D), row(D), row(D), row(D), row(D), _const((MIX, D)), _const((8, 3 * D)), _const((1, D)), _const((1, D)),
                  _const((D, 128))],
        out_specs=[row(D), row(D), row(D), row(D), pl.BlockSpec((NH, None, 1, tm), lambda i: (0, i, 0, 0)),
                   _full((MIX, D)), _full((8, D))],
        out_shape=[jax.ShapeDtypeStruct((S, D), F32), jax.ShapeDtypeStruct((S, D), BF16), jax.ShapeDtypeStruct((S, D), BF16),
                   jax.ShapeDtypeStruct((S, D), F32), jax.ShapeDtypeStruct((NH, S // tm, 1, tm), F32),
                   jax.ShapeDtypeStruct((MIX, D), F32), jax.ShapeDtypeStruct((8, D), F32)],
        compiler_params=_cp(("arbitrary",)),
    )(o, za, ossm, x, tgt, wout, mod, ln_g, ln_b, e8)


def _attn_bwd(q, k, v, do, lse, delta):
    _, S, _ = q.shape
    tk = min(TQ, S // 2)
    nk = S // tk
    tq = 2 * tk
    nq = S // tq

    def body(k_ref, v_ref, q_ref, do_ref, lse_ref, dl_ref, dk_ref, dv_ref, dq_ref, dqt_ref):
        j = pl.program_id(1)
        kb = k_ref[...]
        ktb = kb.T
        vb = v_ref[...]

        @pl.when(j == 0)
        def _():
            dqt_ref[...] = jnp.zeros_like(dqt_ref)

        dk_ref[...] = jnp.zeros_like(dk_ref)
        dv_ref[...] = jnp.zeros_like(dv_ref)

        def step(i, masked, lo=0):
            off = pl.multiple_of(i * tq + lo, tk)
            qb = q_ref[pl.ds(off, tq - lo), :]
            dob = do_ref[pl.ds(off, tq - lo), :]
            pt = jnp.exp2(_nt(kb, qb) - lse_ref[i][:, lo:tq])
            if masked:
                r = lax.broadcasted_iota(jnp.int32, pt.shape, 0)
                cidx = lax.broadcasted_iota(jnp.int32, pt.shape, 1)
                pt = jnp.where(i * tq + lo + cidx >= j * tk + r, pt, 0.0)
            dv_ref[...] += _mm(pt.astype(BF16), dob)
            dsb = (pt * (_nt(vb, dob) - dl_ref[i][:, lo:tq])).astype(BF16)
            dk_ref[...] += _mm(dsb, qb)
            dqt_ref[i, :, lo:tq] += _mm(ktb, dsb)

        first = j >> 1

        @pl.when((j & 1) == 0)
        def _():
            step(first, True)

        @pl.when((j & 1) == 1)
        def _():
            step(first, True, tk)
            dq_ref[...] = dqt_ref[first].T

        def loop_body(t, carry):
            for u in range(4):
                step(first + 1 + 4 * t + u, False)
            return carry

        def tail_body(i, carry):
            step(i, False)
            return carry

        rest = nq - 1 - first
        lax.fori_loop(0, rest >> 2, loop_body, 0)
        lax.fori_loop(nq - (rest & 3), nq, tail_body, 0)

        dk_ref[...] = dk_ref[...] * LN2

    return pl.pallas_call(
        body, name="attn_bwd", grid=(NH, nk),
        in_specs=[pl.BlockSpec((None, tk, HP), lambda h, j: (h, j, 0)),
                  pl.BlockSpec((None, tk, VD), lambda h, j: (h, j, 0)),
                  pl.BlockSpec((None, S, HP), lambda h, j: (h, 0, 0)),
                  pl.BlockSpec((S, VD), lambda h, j: (0, h)),
                  pl.BlockSpec((None, nq, 1, tq), lambda h, j: (h, 0, 0, 0)),
                  pl.BlockSpec((None, nq, 1, tq), lambda h, j: (h, 0, 0, 0))],
        out_specs=[pl.BlockSpec((None, tk, HP), lambda h, j: (h, j, 0)),
                   pl.BlockSpec((None, tk, VD), lambda h, j: (h, j, 0)),
                   pl.BlockSpec((None, tq, HP), lambda h, j: (h, j >> 1, 0))],
        out_shape=[jax.ShapeDtypeStruct((NH, S, HP), F32), jax.ShapeDtypeStruct((NH, S, VD), F32),
                   jax.ShapeDtypeStruct((NH, S, HP), F32)],
        scratch_shapes=[pltpu.VMEM((nq, HP, tq), F32)],
        compiler_params=_cp(("arbitrary", "arbitrary")),
    )(k, v, q, do, lse.reshape(NH, nq, 1, tq), delta.reshape(NH, nq, 1, tq))


def _ssd_bwd(xbc, xc, dtraw, zs, y, htp, dossm, conv_w, dtb, alog, dskx, gssm):
    S = xbc.shape[0]
    nc = S // CH
    tri, triu, e16 = _ssd_consts()

    def body(xbc_ref, xc_ref, dtraw_ref, zs_ref, y_ref, htp_ref, dos_ref,
             cw_ref, dtb_ref, alog_ref, dsk_ref, g_ref, tri_ref, triu_ref, e16_ref,
             dxbc_ref, ddt_ref, dzs_ref, dcw_ref, dcb_ref, dvec_ref, dg_ref,
             dht, dext, dskacc):
        r = pl.program_id(0)

        @pl.when(r == 0)
        def _():
            dht[...] = jnp.zeros_like(dht)
            dext[CH:CH + 8, :] = jnp.zeros((8, CC), F32)
            dskacc[...] = jnp.zeros_like(dskacc)
            dcw_ref[...] = jnp.zeros_like(dcw_ref)
            dcb_ref[...] = jnp.zeros_like(dcb_ref)
            dvec_ref[...] = jnp.zeros_like(dvec_ref)
            dg_ref[...] = jnp.zeros_like(dg_ref)

        for c in reversed(range(NCK)):
            rows = slice(c * CH, (c + 1) * CH)
            chunk(xbc_ref.at[rows, :], xc_ref.at[rows, :], dtraw_ref.at[rows, :], zs_ref.at[rows, :], y_ref.at[rows, :],
                  htp_ref.at[c], dos_ref.at[rows, :], cw_ref, dtb_ref, alog_ref, dsk_ref, g_ref, tri_ref, triu_ref, e16_ref,
                  dxbc_ref.at[rows, :], ddt_ref.at[rows, :], dzs_ref.at[rows, :], dcw_ref, dcb_ref, dvec_ref, dg_ref,
                  dht, dext, dskacc)

        @pl.when(r == nc // NCK - 1)
        def _():
            lane = lax.broadcasted_iota(jnp.int32, (1, 128), 1)
            arow = jnp.where(lane < SH, -jnp.exp(alog_ref[...]), 0.0)
            dvec_ref[1:2, :] = dvec_ref[1:2, :] * arow
            dvec_ref[2:3, :] = _nt_x(jnp.broadcast_to(dskacc[...], (8, SW)), e16_ref[...], 3)[0:1, :]

    def chunk(xbc_ref, xc_ref, dtraw_ref, zs_ref, y_ref, htp_ref, dos_ref,
              cw_ref, dtb_ref, alog_ref, dsk_ref, g_ref, tri_ref, triu_ref, e16_ref,
              dxbc_ref, ddt_ref, dzs_ref, dcw_ref, dcb_ref, dvec_ref, dg_ref,
              dht, dext, dskacc):
        xc = xc_ref[...]
        sact, act, arow, dtpre, dt, cum, cumx, dtx = _ssd_chunk_common(xc, dtraw_ref, dtb_ref, alog_ref, tri_ref, e16_ref)
        cum_t = cum.T
        xs = act[:, 0:SW]
        lastx = cumx[CH - 1:CH, :]
        xh = xs * dtx
        eexp = jnp.exp(cumx)
        dte = jnp.exp(lastx - cumx)
        cdx = jnp.exp(lastx)
        trim = tri_ref[...].astype(F32) > 0.5
        lane = lax.broadcasted_iota(jnp.int32, (CH, 128), 1)
        rowi = lax.broadcasted_iota(jnp.int32, (CH, 128), 0)

        yv = y_ref[...]
        z = zs_ref[...]
        sz = _sigmoid(z)
        silz = z * sz
        hf = yv * silz
        dn = dos_ref[...] * g_ref[...]
        dhf_parts, nrm_parts = [], []
        for g in range(SG):
            gl = slice(g * GW, (g + 1) * GW)
            hg = hf[:, gl]
            rs = lax.rsqrt(jnp.mean(hg * hg, axis=-1, keepdims=True) + RMS_EPS)
            ng = hg * rs
            dng = dn[:, gl]
            dhf_parts.append(rs * (dng - ng * jnp.mean(dng * ng, axis=-1, keepdims=True)))
            nrm_parts.append(ng)
        nrm = jnp.concatenate(nrm_parts, axis=1)
        dhf = jnp.concatenate(dhf_parts, axis=1)
        dg_ref[...] += jnp.sum(dos_ref[...] * nrm, axis=0, keepdims=True)
        dyv = dhf * silz
        dzs_ref[...] = (dhf * yv * (sz * (1.0 + z * (1.0 - sz)))).astype(BF16)
        dskacc[...] += jnp.sum(dyv * xs, axis=0, keepdims=True)
        dxs_skip = dyv * dsk_ref[...]

        dhtn = dht[...]
        hp = htp_ref[...]
        dlastx = jnp.sum(dhtn * hp, axis=0, keepdims=True) * cdx
        xb = xh.astype(BF16)
        xwf = xh * dte
        dcum = jnp.zeros((CH, 128), F32)
        dcum_t = jnp.zeros((128, CH), F32)
        dxh_parts, dcumx_parts, dlast_parts, db_parts, dc_parts = [], [], [], [], []
        for g in range(SG):
            gl = slice(g * GW, (g + 1) * GW)
            bg = act[:, SW + g * SN:SW + (g + 1) * SN].astype(BF16)
            cg = act[:, SW + SG * SN + g * SN:SW + SG * SN + (g + 1) * SN].astype(BF16)
            hpg = hp[:, gl].astype(BF16)
            dhn = dhtn[:, gl].astype(BF16)
            dyg = dyv[:, gl]
            dz = (dyg * eexp[:, gl]).astype(BF16)
            dcg = _nt(dz, hpg)
            dht[:, gl] = dhtn[:, gl] * cdx[:, gl] + _tn(cg, dz)
            yoff = eexp[:, gl] * _mm(cg, hpg)
            dcumx_g = dyg * yoff
            dbg = _nt(xwf[:, gl].astype(BF16), dhn)
            dxw = _mm(bg, dhn)
            ddte = dxw * xwf[:, gl]
            dcumx_parts.append(dcumx_g - ddte)
            dlast_parts.append(jnp.sum(ddte, axis=0, keepdims=True))
            dxh_g = dxw * dte[:, gl]
            cbm = _nt(cg, bg)
            dcb = jnp.zeros((CH, CH), F32)
            dxp_parts = []
            for pr in range(GW // 128):
                h0 = g * (SH // SG) + 2 * pr
                lo = g * GW + pr * 128
                xp = xb[:, lo:lo + 128]
                dyp = dyv[:, lo:lo + 128]
                dxp = jnp.zeros((CH, 128), F32)
                for idx, hh in enumerate((h0, h0 + 1)):
                    decay = jnp.where(trim, jnp.exp(cum[:, hh:hh + 1] - cum_t[hh:hh + 1, :]), 0.0)
                    mh = cbm * decay
                    keep = (lane < SP) if idx == 0 else (lane >= SP)
                    dym = jnp.where(keep, dyp, 0.0).astype(BF16)
                    dm = _nt(dym, xp)
                    dxp = dxp + _tn(mh.astype(BF16), dym)
                    gm = dm * mh
                    dcum = dcum + jnp.where(lane == hh, jnp.sum(gm, axis=1, keepdims=True), 0.0)
                    dcum_t = dcum_t - jnp.where(rowi == hh, jnp.sum(gm, axis=0, keepdims=True), 0.0)
                    dcb = dcb + dm * decay
                dxp_parts.append(dxp)
            dxh_parts.append(dxh_g + jnp.concatenate(dxp_parts, axis=1))
            dcbb = dcb.astype(BF16)
            dc_parts.append(dcg + _mm(dcbb, bg))
            db_parts.append(dbg + _tn(dcbb, cg))
        dxh = jnp.concatenate(dxh_parts, axis=1)
        dcumx = jnp.concatenate(dcumx_parts, axis=1)
        dlastx = dlastx + jnp.concatenate(dlast_parts, axis=1)
        e16 = e16_ref[...]
        dlast128 = _nt_x(jnp.broadcast_to(dlastx, (8, SW)), e16, 2)[0:1, :]
        dcum = dcum + dcum_t.T + _nt_x(dcumx, e16, 2) + jnp.where(rowi == CH - 1, dlast128, 0.0)
        da = _xmm(triu_ref[...], dcum, 2)
        ddt = da * arow + _nt_x(dxh * xs, e16, 2)
        dvec_ref[1:2, :] += jnp.sum(da * dt, axis=0, keepdims=True)
        ddtraw = jnp.where(lane < SH, ddt * _sigmoid(dtpre), 0.0)
        dvec_ref[0:1, :] += jnp.sum(ddtraw, axis=0, keepdims=True)
        ddt_ref[...] = ddtraw.astype(BF16)
        dxs = dxs_skip + dxh * dtx
        dact = jnp.concatenate([dxs] + db_parts + dc_parts, axis=1)
        dxc = dact * (sact * (1.0 + xc * (1.0 - sact)))

        dcb_ref[...] += jnp.sum(dxc, axis=0, keepdims=True)
        dext[0:CH, :] = dxc
        cw = cw_ref[...]
        xraw = xbc_ref[...]
        dxr = cw[CW - 1:CW, :] * dxc
        dcw_ref[CW - 1:CW, :] += jnp.sum(dxc * xraw, axis=0, keepdims=True)
        for kk in range(CW - 1):
            dwin = dext[CW - 1 - kk:CW - 1 - kk + CH, :]
            dxr = dxr + cw[kk:kk + 1, :] * dwin
            dcw_ref[kk:kk + 1, :] += jnp.sum(dwin * xraw, axis=0, keepdims=True)
        dxbc_ref[...] = dxr.astype(BF16)
        dext[CH:CH + 8, :] = dxc[0:8, :]

    ng = nc // NCK
    rev = lambda n: pl.BlockSpec((NCK * CH, n), lambda r: (ng - 1 - r, 0))
    return pl.pallas_call(
        body, name="ssd_bwd", grid=(ng,),
        in_specs=[rev(CC), rev(CC),
                  rev(128), rev(SW), rev(SW), pl.BlockSpec((NCK, SN, SW), lambda r: (ng - 1 - r, 0, 0)), rev(SW),
                  _const((CW, CC)), _const((1, 128)), _const((1, 128)), _const((1, SW)), _const((1, SW)),
                  _const((CH, CH)), _const((CH, CH)), _const((128, SW))],
        out_specs=[rev(CC), rev(128), rev(SW), _full((CW, CC)), _full((1, CC)), _full((8, 128)), _full((1, SW))],
        out_shape=[jax.ShapeDtypeStruct((S, CC), BF16), jax.ShapeDtypeStruct((S, 128), BF16), jax.ShapeDtypeStruct((S, SW), BF16),
                   jax.ShapeDtypeStruct((CW, CC), F32), jax.ShapeDtypeStruct((1, CC), F32),
                   jax.ShapeDtypeStruct((8, 128), F32), jax.ShapeDtypeStruct((1, SW), F32)],
        scratch_shapes=[pltpu.VMEM((SN, SW), F32), pltpu.VMEM((CH + 8, CC), F32), pltpu.VMEM((1, SW), F32)],
        compiler_params=_cp(("arbitrary",)),
    )(xbc, xc, dtraw, zs, y, htp, dossm, conv_w, dtb, alog, dskx, gssm, tri, triu, e16)


def _mla_bwd(dq, dk, dv, qlat, ckv, qg, kvg, wq, wk, wv, pos, invf):
    S = qlat.shape[0]
    tm = min(TQ, S)

    def body(dq_ref, dk_ref, dv_ref, ql_ref, ckv_ref, qg_ref, kvg_ref, wq_ref, wk_ref, wv_ref, pos_ref, invf_ref,
             dql_ref, dckv_ref, dkr_ref, dwq_ref, dwk_ref, dwv_ref, dqg_ref, dkvg_ref):
        i = pl.program_id(0)

        @pl.when(i == 0)
        def _():
            dwq_ref[...] = jnp.zeros_like(dwq_ref)
            dwk_ref[...] = jnp.zeros_like(dwk_ref)
            dwv_ref[...] = jnp.zeros_like(dwv_ref)
            dqg_ref[...] = jnp.zeros_like(dqg_ref)
            dkvg_ref[...] = jnp.zeros_like(dkvg_ref)

        ang = pos_ref[...].astype(F32) * invf_ref[...]
        cs = jnp.cos(ang)
        sn = jnp.sin(ang)

        def rms_bwd(v, g, dn, dg_ref):
            r = lax.rsqrt(jnp.mean(v * v, axis=-1, keepdims=True) + RMS_EPS)
            vh = v * r
            dg_ref[...] += jnp.sum(dn * vh, axis=0, keepdims=True)
            dvh = dn * g
            return vh, r * (dvh - vh * jnp.mean(dvh * vh, axis=-1, keepdims=True))

        pieces = []
        for h in range(NH):
            dqh = dq_ref[h]
            pieces.append(dqh[:, 0:NOPE] * SCALE)
            pieces.append(_rope_t(dqh[:, NOPE:HP], cs, sn) * SCALE)
        dqf = jnp.concatenate(pieces, axis=1).astype(BF16)
        ql = ql_ref[...]
        g = qg_ref[...]
        dqn = _nt(dqf, wq_ref[...])
        qh, dql = rms_bwd(ql, g, dqn, dqg_ref)
        dwq_ref[...] += _tn((qh * g).astype(BF16), dqf)
        dql_ref[...] = dql.astype(BF16)

        dkn_p = jnp.concatenate([dk_ref[h, :, 0:NOPE] for h in range(NH)], axis=1).astype(BF16)
        dvf = jnp.concatenate([dv_ref[h] for h in range(NH)], axis=1).astype(BF16)
        dkr = dk_ref[0, :, NOPE:HP]
        for h in range(1, NH):
            dkr = dkr + dk_ref[h, :, NOPE:HP]
        lane = lax.broadcasted_iota(jnp.int32, dkr.shape, 1)
        dkr_ref[...] = jnp.where(lane < ROPE, _rope_t(dkr, cs, sn), 0.0).astype(BF16)
        cv = ckv_ref[...]
        gk = kvg_ref[...]
        dkn = _nt(dkn_p, wk_ref[...]) + _nt(dvf, wv_ref[...])
        kh, dckv = rms_bwd(cv, gk, dkn, dkvg_ref)
        knb = (kh * gk).astype(BF16)
        dwk_ref[...] += _tn(knb, dkn_p)
        dwv_ref[...] += _tn(knb, dvf)
        dckv_ref[...] = dckv.astype(BF16)

    row = lambda n: pl.BlockSpec((tm, n), lambda i: (i, 0))
    heads = lambda n: pl.BlockSpec((NH, tm, n), lambda i: (0, i, 0))
    return pl.pallas_call(
        body, name="mla_bwd", grid=(S // tm,),
        in_specs=[heads(HP), heads(HP), heads(VD), row(QL), row(KVL), _const((1, QL)), _const((1, KVL)),
                  _const((QL, NH * HP)), _const((KVL, NH * NOPE)), _const((KVL, NH * VD)), row(1), _const((1, 128))],
        out_specs=[row(QL), row(KVL), row(128), _full((QL, NH * HP)), _full((KVL, NH * NOPE)), _full((KVL, NH * VD)),
                   _full((1, QL)), _full((1, KVL))],
        out_shape=[jax.ShapeDtypeStruct((S, QL), BF16), jax.ShapeDtypeStruct((S, KVL), BF16), jax.ShapeDtypeStruct((S, 128), BF16),
                   jax.ShapeDtypeStruct((QL, NH * HP), F32), jax.ShapeDtypeStruct((KVL, NH * NOPE), F32),
                   jax.ShapeDtypeStruct((KVL, NH * VD), F32), jax.ShapeDtypeStruct((1, QL), F32), jax.ShapeDtypeStruct((1, KVL), F32)],
        compiler_params=_cp(("arbitrary",)),
    )(dq, dk, dv, qlat, ckv, qg, kvg, wq, wk, wv, pos, invf)


def _inproj_bwd(x, gx1, mod, win, dql, dckv, dza, dxbc, dzs, dkr, ddt):
    S = x.shape[0]
    tm = min(TM, S)

    def body(x_ref, gx1_ref, mod_ref, win_ref, dql_ref, dckv_ref, dza_ref, dxbc_ref, dzs_ref, dkr_ref, ddt_ref,
             gx_ref, dw_ref, vec_ref):
        i = pl.program_id(0)

        @pl.when(i == 0)
        def _():
            dw_ref[...] = jnp.zeros_like(dw_ref)
            vec_ref[...] = jnp.zeros_like(vec_ref)

        shift = mod_ref[0:1, 0:D]
        scale = mod_ref[0:1, D:2 * D]
        xv = x_ref[...]
        ut = (xv * (1.0 + scale) + shift).T.astype(BF16)
        pieces = (dql_ref, dckv_ref, dza_ref, dxbc_ref, dzs_ref, dkr_ref, ddt_ref)
        du = jnp.zeros((tm, D), F32)
        lo = 0
        for p_ref in pieces:
            n = p_ref.shape[1]
            dp = p_ref[...]
            du = du + _nt(dp, win_ref[:, lo:lo + n])
            dw_ref[:, lo:lo + n] += _mm(ut, dp)
            lo += n
        vec_ref[0:1, :] += jnp.sum(du, axis=0, keepdims=True)
        vec_ref[1:2, :] += jnp.sum(du * xv, axis=0, keepdims=True)
        gx_ref[...] = gx1_ref[...] + du * (1.0 + scale)

    row = lambda n: pl.BlockSpec((tm, n), lambda i: (i, 0))
    return pl.pallas_call(
        body, name="inproj_bwd", grid=(S // tm,),
        in_specs=[row(D), row(D), _const((8, 3 * D)), _const((D, IN_P)), row(QL), row(KVL), row(D), row(CC), row(D),
                  row(128), row(128)],
        out_specs=[row(D), pl.BlockSpec((D, IN_P), lambda i: (0, 0), pipeline_mode=pl.Buffered(1)), _full((8, D))],
        out_shape=[jax.ShapeDtypeStruct((S, D), F32), jax.ShapeDtypeStruct((D, IN_P), F32), jax.ShapeDtypeStruct((8, D), F32)],
        compiler_params=_cp(("arbitrary",)),
    )(x, gx1, mod, win, dql, dckv, dza, dxbc, dzs, dkr, ddt)


def _ada_bwd(callt, dmods):
    w = dmods.shape[1]

    def body(c_ref, d_ref, o_ref):
        acc = c_ref[:, 0:1] * d_ref[0:1, :]
        for s in range(1, 8):
            acc = acc + c_ref[:, s:s + 1] * d_ref[s:s + 1, :]
        o_ref[0] = acc

    return pl.pallas_call(body, name="ada_bwd", out_shape=jax.ShapeDtypeStruct((1, D, w), F32),
                          compiler_params=_cp())(callt, dmods)


def _adamw(name, parts, w, m, v):
    rows, ncol = w.shape
    tr = min(rows, 128)
    nparts = parts.shape[0]

    def body(p_ref, w_ref, m_ref, v_ref, g_ref, d_ref, nm_ref, nv_ref):
        g = p_ref[0].astype(F32)
        for s in range(1, nparts):
            g = g + p_ref[s].astype(F32)
        g_ref[...] = g
        nm = B1 * m_ref[...] + (1.0 - B1) * g
        nv = B2 * v_ref[...] + (1.0 - B2) * (g * g)
        nm_ref[...] = nm
        nv_ref[...] = nv
        m_hat = nm / (1.0 - B1 ** STEP)
        v_hat = nv / (1.0 - B2 ** STEP)
        d_ref[...] = -LR * (m_hat / (jnp.sqrt(v_hat) + EPS) + WD * w_ref[...])

    row = pl.BlockSpec((tr, ncol), lambda i: (i, 0))
    sd = jax.ShapeDtypeStruct((rows, ncol), F32)
    return pl.pallas_call(
        body, name="adamw_" + name, grid=(rows // tr,),
        in_specs=[pl.BlockSpec((nparts, tr, ncol), lambda i: (0, i, 0)), row, row, row],
        out_specs=[row, row, row, row], out_shape=[sd, sd, sd, sd],
        compiler_params=_cp(("arbitrary",)),
    )(parts, w, m, v)


_SMALL = (("b_ada", 3 * D), ("conv_w", CW * CC // 4), ("conv_b", CC), ("ssm_norm_g", SW), ("ln_g", D), ("ln_b", D),
          ("q_norm_g", QL), ("kv_norm_g", KVL), ("dt_bias", SH), ("a_log", SH), ("d_skip", SH), ("loss", 128))


def _pack_small(d, lead):
    flat = [d[name].reshape(d[name].shape[:lead] + (-1,)) for name, _ in _SMALL]
    used = sum(f.shape[lead] for f in flat)
    pad = jnp.zeros(flat[0].shape[:lead] + (R_SMALL * 1024 - used,), F32)
    return jnp.concatenate(flat + [pad], axis=lead).reshape(flat[0].shape[:lead] + (R_SMALL, 1024))


def _unpack_small(p):
    flat = p.reshape(-1)
    out, r = {}, 0
    for name, n in _SMALL:
        out[name] = flat[r:r + n]
        r += n
    return out


def _in_to_padded(w):
    z = lambda n: jnp.zeros((w.shape[0], n), w.dtype)
    return jnp.concatenate([w[:, 0:384], w[:, 384:640], w[:, 704:1728], w[:, 1728:3264], w[:, 3280:4304],
                            w[:, 640:704], z(64), w[:, 3264:3280], z(112)], axis=1)


def _in_from_padded(g):
    return jnp.concatenate([g[:, 0:384], g[:, 384:640], g[:, P_KR[0]:P_KR[0] + 64], g[:, 640:1664], g[:, 1664:3200],
                            g[:, P_DT[0]:P_DT[0] + 16], g[:, 3200:4224]], axis=1)


def kernel(x, c, positions, w_ada, b_ada, w_in, q_norm_g, w_qb, kv_norm_g, w_kvb, conv_w, conv_b, dt_bias, a_log, d_skip, ssm_norm_g, w_out, ln_g, ln_b, loss_target, m_w_ada, m_b_ada, m_w_in, m_q_norm_g, m_w_qb, m_kv_norm_g, m_w_kvb, m_conv_w, m_conv_b, m_dt_bias, m_a_log, m_d_skip, m_ssm_norm_g, m_w_out, m_ln_g, m_ln_b, v_w_ada, v_b_ada, v_w_in, v_q_norm_g, v_w_qb, v_kv_norm_g, v_w_kvb, v_conv_w, v_conv_b, v_dt_bias, v_a_log, v_d_skip, v_ssm_norm_g, v_w_out, v_ln_g, v_ln_b):
    S = x.shape[1]
    xv = x[0]
    tgt = loss_target[0]

    cw16 = jnp.concatenate([conv_w[0], jnp.zeros((16 - CW, CC // 4), F32)], axis=0)
    f_in, f_qb, f_kvb, f_out, f_cw = _gather_weights(
        [w_in[0].astype(BF16), w_qb[0].astype(BF16), w_kvb[0].astype(BF16), w_out[0].astype(BF16), cw16])
    cat1 = lambda f: jnp.concatenate([f[k] for k in range(4)], axis=1)
    win = _in_to_padded(cat1(f_in))
    wqb = cat1(f_qb).reshape(QL, NH, QKD)
    wq = jnp.concatenate([wqb, jnp.zeros((QL, NH, HP - QKD), BF16)], axis=2).reshape(QL, NH * HP)
    wkvb = cat1(f_kvb).reshape(KVL, NH, NOPE + VD)
    wk = wkvb[:, :, 0:NOPE].reshape(KVL, NH * NOPE)
    wv = wkvb[:, :, NOPE:].reshape(KVL, NH * VD)
    wout = f_out.reshape(MIX, D)
    cwf = cat1(f_cw[:, 0:CW, :])

    half = ROPE // 2
    invf = 1.0 / (ROPE_THETA ** (jnp.arange(half, dtype=F32) / half))
    invf = jnp.concatenate([invf, invf, jnp.zeros((128 - ROPE,), F32)]).reshape(1, 128)
    pos = positions.reshape(S, 1)
    pad128 = lambda a: jnp.concatenate([a.reshape(1, SH), jnp.zeros((1, 128 - SH), F32)], axis=1)
    dtb, alog = pad128(dt_bias), pad128(a_log)
    dskx = jnp.repeat(d_skip.reshape(SH), SP).reshape(1, SW)

    my_c = lax.axis_index("c")
    (call,) = _exchange("gather_c", [jnp.broadcast_to(c.reshape(1, 1, D), (4, 1, D))])
    call = call.reshape(8, D)
    mods = _ada(call, w_ada[0])
    (mrows,) = _exchange("scatter_mod", [mods.reshape(4, 2, 3 * D // 4)])
    mine = lax.dynamic_index_in_dim(mrows.reshape(4, 2, 2, 3 * D // 4)[:, 0], my_c, axis=1, keepdims=False)
    mod = jnp.broadcast_to(mine.reshape(1, 3 * D) + b_ada, (8, 3 * D))
    qlat, ckv, za, xbc, zs, dtraw, q, k, v = _inproj(xv, mod, win, q_norm_g, kv_norm_g, wq, wk, wv, pos, invf)
    o, lse = _attn_fwd(q, k, v)
    xc, y, htp, ossm = _ssd_fwd(xbc, dtraw, zs, cwf, conv_b, dtb, alog, dskx, ssm_norm_g)
    gx1, do, dza, dossm, delta, dwout, vec_o = _outproj(o, za, ossm, xv, tgt, wout, mod, ln_g, ln_b)
    loss_part = jnp.zeros((128,), F32).at[0].set(0.5 / D * jnp.sum(vec_o[0]))

    dk, dv, dq = _attn_bwd(q, k, v, do, lse, delta)
    dxbc, ddt, dzs, dcw, dcb, dvec, dgssm = _ssd_bwd(xbc, xc, dtraw, zs, y, htp, dossm, cwf, dtb, alog, dskx, ssm_norm_g)
    dql, dckv, dkr, dwq, dwk, dwv, dqg, dkvg = _mla_bwd(dq, dk, dv, qlat, ckv, q_norm_g, kv_norm_g, wq, wk, wv, pos, invf)
    gx, dwin, vec_i = _inproj_bwd(xv, gx1, mod, win, dql, dckv, dza, dxbc, dzs, dkr, ddt)
    dmod = jnp.concatenate([vec_i[0:1], vec_i[1:2], vec_o[3:4]], axis=1)

    cols = lambda g: g.reshape(g.shape[0], 4, g.shape[1] // 4).transpose(1, 0, 2)
    g_in = cols(_in_from_padded(dwin)).astype(BF16)
    g_qb = cols(dwq.reshape(QL, NH, HP)[:, :, 0:QKD].reshape(QL, NH * QKD)).astype(BF16)
    g_kvb = cols(jnp.concatenate([dwk.reshape(KVL, NH, NOPE), dwv.reshape(KVL, NH, VD)], axis=2)
                 .reshape(KVL, NH * (NOPE + VD))).astype(BF16)
    g_out = dwout.reshape(4, MIX // 4, D).astype(BF16)
    small = {"b_ada": dmod, "conv_b": dcb, "ssm_norm_g": dgssm, "ln_g": vec_o[1:2], "ln_b": vec_o[2:3],
             "q_norm_g": dqg, "kv_norm_g": dkvg, "dt_bias": dvec[0:1, 0:SH], "a_log": dvec[1:2, 0:SH], "d_skip": dvec[2:3, 0:SH],
             "loss": loss_part}
    small = {n: jnp.broadcast_to(a.reshape(1, -1), (4, a.size)) for n, a in small.items()}
    small["conv_w"] = cols(dcw).reshape(4, CW * CC // 4)
    gsmall = _pack_small(small, 1)

    r_in, r_qb, r_kvb, r_out, rs, dmods = _exchange(
        "exchange_grads", [g_in, g_qb, g_kvb, g_out, gsmall, jnp.broadcast_to(dmod.reshape(1, 1, 3 * D), (4, 1, 3 * D))])
    chip = 2 * lax.axis_index("x") + lax.axis_index("y")
    dmods = lax.dynamic_slice_in_dim(dmods.reshape(8, 3 * D), chip * (3 * D // 4), 3 * D // 4, axis=1)
    g_ada = _ada_bwd(call.T, dmods)
    res = dict(w_ada=_adamw("w_ada", g_ada, w_ada[0], m_w_ada[0], v_w_ada[0]),
               w_in=_adamw("w_in", r_in, w_in[0], m_w_in[0], v_w_in[0]),
               w_qb=_adamw("w_qb", r_qb, w_qb[0], m_w_qb[0], v_w_qb[0]),
               w_kvb=_adamw("w_kvb", r_kvb, w_kvb[0], m_w_kvb[0], v_w_kvb[0]),
               w_out=_adamw("w_out", r_out, w_out[0], m_w_out[0], v_w_out[0]))
    wsm = _pack_small(dict(b_ada=b_ada, conv_w=conv_w, conv_b=conv_b, ssm_norm_g=ssm_norm_g, ln_g=ln_g, ln_b=ln_b,
                           q_norm_g=q_norm_g, kv_norm_g=kv_norm_g, dt_bias=dt_bias, a_log=a_log, d_skip=d_skip, loss=jnp.zeros((128,), F32)), 0)
    msm = _pack_small(dict(b_ada=m_b_ada, conv_w=m_conv_w, conv_b=m_conv_b, ssm_norm_g=m_ssm_norm_g, ln_g=m_ln_g, ln_b=m_ln_b,
                           q_norm_g=m_q_norm_g, kv_norm_g=m_kv_norm_g, dt_bias=m_dt_bias, a_log=m_a_log, d_skip=m_d_skip, loss=jnp.zeros((128,), F32)), 0)
    vsm = _pack_small(dict(b_ada=v_b_ada, conv_w=v_conv_w, conv_b=v_conv_b, ssm_norm_g=v_ssm_norm_g, ln_g=v_ln_g, ln_b=v_ln_b,
                           q_norm_g=v_q_norm_g, kv_norm_g=v_kv_norm_g, dt_bias=v_dt_bias, a_log=v_a_log, d_skip=v_d_skip, loss=jnp.zeros((128,), F32)), 0)
    sm = _adamw("small", rs, wsm, msm, vsm)

    order = ["w_ada", "b_ada", "w_in", "q_norm_g", "w_qb", "kv_norm_g", "w_kvb", "conv_w", "conv_b", "dt_bias", "a_log",
             "d_skip", "ssm_norm_g", "w_out", "ln_g", "ln_b"]
    shapes = dict(w_ada=w_ada.shape, b_ada=b_ada.shape, w_in=w_in.shape, q_norm_g=q_norm_g.shape, w_qb=w_qb.shape,
                  kv_norm_g=kv_norm_g.shape, w_kvb=w_kvb.shape, conv_w=conv_w.shape, conv_b=conv_b.shape, dt_bias=dt_bias.shape,
                  a_log=a_log.shape, d_skip=d_skip.shape, ssm_norm_g=ssm_norm_g.shape, w_out=w_out.shape, ln_g=ln_g.shape,
                  ln_b=ln_b.shape)
    outs = []
    for kind in range(4):
        d = _unpack_small(sm[kind])
        d.update({n: r[kind] for n, r in res.items()})
        outs.extend(d[n].reshape(shapes[n]) for n in order)
    loss = _unpack_small(sm[0])["loss"][0]
    return (loss, gx.reshape(x.shape), *outs)
```

```python
import functools
import math

import numpy as np
import jax
import jax.numpy as jnp
from jax import lax
from jax.experimental import pallas as pl
from jax.experimental.pallas import tpu as pltpu

F32 = jnp.float32
BF16 = jnp.bfloat16
MESH_ID = pl.DeviceIdType.MESH

D = 1024
NH = 8
NOPE = 128
ROPE = 64
VD = 128
VDP = 144
QKD = NOPE + ROPE
HP = 256
QL = 384
KVL = 256
ROPE_THETA = 10000.0
SH = 16
SP = 64
SG = 2
SN = 128
CW = 4
CH = 128
SW = SH * SP
CC = SW + 2 * SG * SN
GW = SW // SG
MIX = 2 * D
IN_W = 4304
ALPHA = 2.0 ** 0.25
RMS_EPS = 1e-6
LN_EPS = 1e-5
SCALE = QKD ** -0.5
LN2 = math.log(2.0)
QSCALE = SCALE / LN2
LR, B1, B2, EPS, WD, STEP = 0.001, 0.9, 0.999, 1e-08, 0.01, 10

P_Q = (0, 384)
P_KV = (384, 640)
P_ZA = (640, 1664)
P_XBC = (1664, 3200)
P_ZS = (3200, 4224)
P_KR = (4224, 4352)
P_DT = (4352, 4480)
IN_P = 4480

R_SMALL = 16

TM = 256
TQ = 512
NSP = 2
NCK = 4
TQF = 1024
VMEM_LIMIT = 56 * 1024 * 1024


def _cp(sem=None):
    return pltpu.CompilerParams(dimension_semantics=sem, vmem_limit_bytes=VMEM_LIMIT)


def _mm(a, b):
    return jnp.dot(a, b, preferred_element_type=F32)


def _nt(a, b):
    return lax.dot_general(a, b, (((1,), (1,)), ((), ())), preferred_element_type=F32)


def _tn(a, b):
    return lax.dot_general(a, b, (((0,), (0,)), ((), ())), preferred_element_type=F32)


def _split(a, terms):
    parts = []
    for t in range(terms):
        p = a.astype(BF16)
        parts.append(p)
        if t + 1 < terms:
            a = a - p.astype(F32)
    return parts


def _mm_x(a, ones, terms):
    parts = _split(a, terms)
    out = _mm(parts[0], ones)
    for p in parts[1:]:
        out = out + _mm(p, ones)
    return out


def _xmm(ones, a, terms):
    parts = _split(a, terms)
    out = _mm(ones, parts[0])
    for p in parts[1:]:
        out = out + _mm(ones, p)
    return out


def _nt_x(a, ones, terms):
    parts = _split(a, terms)
    out = _nt(parts[0], ones)
    for p in parts[1:]:
        out = out + _nt(p, ones)
    return out


def _sigmoid(z):
    return 1.0 / (1.0 + jnp.exp(-z))


def _softplus(z):
    return jnp.maximum(z, 0.0) + jnp.log1p(jnp.exp(-jnp.abs(z)))


def _rope(t, cs, sn):
    lane = lax.broadcasted_iota(jnp.int32, t.shape, 1)
    rot = jnp.where(lane < ROPE // 2, -pltpu.roll(t, 128 - ROPE // 2, 1), pltpu.roll(t, ROPE // 2, 1))
    return t * cs + rot * sn


def _rope_t(t, cs, sn):
    lane = lax.broadcasted_iota(jnp.int32, t.shape, 1)
    y = t * sn
    rot = jnp.where(lane < ROPE // 2, -pltpu.roll(y, 128 - ROPE // 2, 1), pltpu.roll(y, ROPE // 2, 1))
    return t * cs - rot


def _full(shape):
    n = len(shape)
    return pl.BlockSpec(shape, lambda *_: (0,) * n)


def _const(shape):
    n = len(shape)
    return pl.BlockSpec(shape, lambda *_: (0,) * n, pipeline_mode=pl.Buffered(1))


def _gather_weights(shards):
    n = len(shards)
    halves = [a.shape[0] // 2 for a in shards]

    def body(*refs):
        srcs, dsts = refs[:n], refs[n:2 * n]
        send_sems, recv_sems, local_sems = refs[2 * n:]
        x, y, c = lax.axis_index("x"), lax.axis_index("y"), lax.axis_index("c")
        me = 2 * x + y
        sibling = (x, y, 1 - c)
        chips = [(1 - x, y), (x, 1 - y), (1 - x, 1 - y)]

        def rows(a, pc):
            return pl.ds(pl.multiple_of(pc * halves[a], halves[a]), halves[a])

        def copy(a, k, src, slot, pc, to):
            return pltpu.make_async_remote_copy(
                src_ref=src, dst_ref=dsts[a].at[slot, rows(a, pc)], send_sem=send_sems.at[a, k],
                recv_sem=recv_sems.at[a, k], device_id=to, device_id_type=MESH_ID)

        local = [pltpu.make_async_copy(srcs[a], dsts[a].at[me], local_sems.at[a]) for a in range(n)]
        for cp in local:
            cp.start()
        sends = [copy(a, j, srcs[a].at[rows(a, c)], me, c, (px, py, c)) for a in range(n) for j, (px, py) in enumerate(chips)]
        for cp in sends:
            cp.start()
        passed = []
        for a in range(n):
            for j, (px, py) in enumerate(chips):
                k = 2 * px + py
                copy(a, j, srcs[a].at[rows(a, c)], k, c, (x, y, c)).wait_recv()
                fwd = copy(a, 3 + j, dsts[a].at[k, rows(a, c)], k, c, sibling)
                fwd.start()
                passed.append(fwd)
        for a in range(n):
            for j, (px, py) in enumerate(chips):
                copy(a, 3 + j, srcs[a].at[rows(a, c)], 2 * px + py, 1 - c, (x, y, c)).wait_recv()
        for cp in sends + passed:
            cp.wait_send()
        for cp in local:
            cp.wait()

    hbm = pl.BlockSpec(memory_space=pltpu.HBM)
    return pl.pallas_call(
        body, name="gather_weights",
        out_shape=tuple(jax.ShapeDtypeStruct((4,) + a.shape, a.dtype) for a in shards),
        in_specs=[hbm] * n, out_specs=tuple([hbm] * n),
        scratch_shapes=[pltpu.SemaphoreType.DMA((n, 6)), pltpu.SemaphoreType.DMA((n, 6)), pltpu.SemaphoreType.DMA((n,))],
    )(*shards)


def _exchange(name, slabs):
    n = len(slabs)

    def body(*refs):
        srcs, dsts = refs[:n], refs[n:2 * n]
        send_sems, recv_sems, local_sems = refs[2 * n:]
        x, y, c = lax.axis_index("x"), lax.axis_index("y"), lax.axis_index("c")
        chip = 2 * x + y
        sibling = (x, y, 1 - c)
        chips = [(1 - x, y), (x, 1 - y), (1 - x, 1 - y)]

        def slot(px, py, pc):
            return 4 * px + 2 * py + pc

        def copy(a, k, src, s, to):
            return pltpu.make_async_remote_copy(
                src_ref=src, dst_ref=dsts[a].at[s], send_sem=send_sems.at[a, k], recv_sem=recv_sems.at[a, k],
                device_id=to, device_id_type=MESH_ID)

        mine = slot(x, y, c)
        local = [pltpu.make_async_copy(srcs[a].at[chip], dsts[a].at[mine], local_sems.at[a]) for a in range(n)]
        for cp in local:
            cp.start()
        first = []
        for a in range(n):
            first.append(copy(a, 0, srcs[a].at[chip], mine, sibling))
            for j, (px, py) in enumerate(chips):
                first.append(copy(a, 1 + j, srcs[a].at[2 * px + py], mine, (px, py, c)))
        for cp in first:
            cp.start()
        passed = []
        for a in range(n):
            for j, (px, py) in enumerate(chips):
                s = slot(px, py, c)
                copy(a, 1 + j, srcs[a].at[chip], s, (x, y, c)).wait_recv()
                fwd = copy(a, 4 + j, dsts[a].at[s], s, sibling)
                fwd.start()
                passed.append(fwd)
        for a in range(n):
            copy(a, 0, srcs[a].at[chip], slot(x, y, 1 - c), (x, y, c)).wait_recv()
            for j, (px, py) in enumerate(chips):
                copy(a, 4 + j, srcs[a].at[chip], slot(px, py, 1 - c), (x, y, c)).wait_recv()
        for cp in first + passed:
            cp.wait_send()
        for cp in local:
            cp.wait()

    hbm = pl.BlockSpec(memory_space=pltpu.HBM)
    return pl.pallas_call(
        body, name=name,
        out_shape=tuple(jax.ShapeDtypeStruct((8,) + a.shape[1:], a.dtype) for a in slabs),
        in_specs=[hbm] * n, out_specs=tuple([hbm] * n),
        scratch_shapes=[pltpu.SemaphoreType.DMA((n, 7)), pltpu.SemaphoreType.DMA((n, 7)), pltpu.SemaphoreType.DMA((n,))],
    )(*slabs)


def _ada(call, w_shard):
    def body(c_ref, w_ref, o_ref):
        o_ref[...] = _mm(c_ref[...].astype(BF16), w_ref[...].astype(BF16))

    return pl.pallas_call(body, name="ada", out_shape=jax.ShapeDtypeStruct((8, w_shard.shape[1]), F32),
                          compiler_params=_cp())(call, w_shard)


def _inproj(x, mod, win, qg, kvg, wq, wk, wv, pos, invf):
    S = x.shape[0]
    tm = min(TM, S)

    def body(x_ref, mod_ref, win_ref, qg_ref, kvg_ref, wq_ref, wk_ref, wv_ref, pos_ref, invf_ref,
             qlat_ref, ckv_ref, za_ref, xbc_ref, zs_ref, dt_ref, q_ref, k_ref, v_ref):
        shift = mod_ref[0:1, 0:D]
        scale = mod_ref[0:1, D:2 * D]
        u = (x_ref[...] * (1.0 + scale) + shift).astype(BF16)

        def proj(p):
            return _mm(u, win_ref[:, p[0]:p[1]])

        ql = proj(P_Q)
        ckv = proj(P_KV)
        qlat_ref[...] = ql
        ckv_ref[...] = ckv
        za_ref[...] = proj(P_ZA)
        xbc_ref[...] = proj(P_XBC)
        zs_ref[...] = proj(P_ZS)
        dt_ref[...] = proj(P_DT)
        kr = proj(P_KR)

        ang = pos_ref[...].astype(F32) * invf_ref[...]
        cs = jnp.cos(ang)
        sn = jnp.sin(ang)

        rq = lax.rsqrt(jnp.mean(ql * ql, axis=-1, keepdims=True) + RMS_EPS)
        qn = (ql * rq * qg_ref[...]).astype(BF16)
        for h in range(NH):
            qh = _mm(qn, wq_ref[:, h * HP:(h + 1) * HP])
            q_ref[h, :, 0:NOPE] = (qh[:, 0:NOPE] * QSCALE).astype(BF16)
            q_ref[h, :, NOPE:HP] = (_rope(qh[:, NOPE:HP], cs, sn) * QSCALE).astype(BF16)

        rk = lax.rsqrt(jnp.mean(ckv * ckv, axis=-1, keepdims=True) + RMS_EPS)
        kn = (ckv * rk * kvg_ref[...]).astype(BF16)
        knope = _mm(kn, wk_ref[...])
        vall = _mm(kn, wv_ref[...])
        krf = _rope(kr, cs, sn)
        krr = krf.astype(BF16)
        for h in range(NH):
            k_ref[h, :, 0:NOPE] = knope[:, h * NOPE:(h + 1) * NOPE].astype(BF16)
            k_ref[h, :, NOPE:HP] = krr
            v_ref[h] = vall[:, h * VD:(h + 1) * VD].astype(BF16)

    row = lambda n: pl.BlockSpec((tm, n), lambda i: (i, 0))
    heads = lambda n: pl.BlockSpec((NH, tm, n), lambda i: (0, i, 0))
    sd = lambda n: jax.ShapeDtypeStruct((S, n), F32)
    hd = lambda n: jax.ShapeDtypeStruct((NH, S, n), BF16)
    return pl.pallas_call(
        body, name="inproj", grid=(S // tm,),
        in_specs=[row(D), _const((8, 3 * D)), _const((D, IN_P)), _const((1, QL)), _const((1, KVL)),
                  _const((QL, NH * HP)), _const((KVL, NH * NOPE)), _const((KVL, NH * VD)),
                  row(1), _const((1, 128))],
        out_specs=[row(QL), row(KVL), row(D), row(CC), row(D), row(128), heads(HP), heads(HP), heads(VD)],
        out_shape=[sd(QL), sd(KVL), sd(D), sd(CC), sd(D), sd(128), hd(HP), hd(HP), hd(VD)],
        compiler_params=_cp(("arbitrary",)),
    )(x, mod, win, qg, kvg, wq, wk, wv, pos, invf)


def _attn_fwd(q, k, v):
    _, S, _ = q.shape
    tq = min(TQF, S)
    nq = S // tq
    half = tq // NSP
    tb = min(256, half)
    nsb = half // tb

    def body(q_ref, k_ref, v_ref, o_ref, lse_ref, vt_ref):
        i = pl.program_id(1)
        qb = q_ref[...]

        @pl.when(i == 0)
        def _():
            ones_rows = jnp.where(lax.broadcasted_iota(jnp.int32, (VDP - VD, tb), 0) == 0, 1.0, 0.0).astype(BF16)

            def fill(blk, carry):
                off = pl.multiple_of(blk * tb, tb)
                vt_ref[blk, 0:VD, :] = v_ref[pl.ds(off, tb), :].T
                vt_ref[blk, VD:VDP, :] = ones_rows
                return carry

            lax.fori_loop(0, S // tb, fill, 0)

        def scores(j, hb):
            off = pl.multiple_of(j * tq + hb * half, half)
            return _nt(k_ref[pl.ds(off, half), :], qb)

        def update(j, hb, s, carry):
            m, acc = carry
            m_new = jnp.maximum(m, jnp.max(s, axis=0, keepdims=True))
            a = jnp.exp2(m - m_new)
            pb = jnp.exp2(s - m_new).astype(BF16)
            acc = a * acc
            for sb in range(nsb):
                acc = acc + _mm(vt_ref[(NSP * j + hb) * nsb + sb], pb[sb * tb:(sb + 1) * tb, :])
            return m_new, acc

        def trip(j, carry, masked):
            s = [scores(j, hb) for hb in range(NSP)]
            if masked:
                r = lax.broadcasted_iota(jnp.int32, s[0].shape, 0)
                cidx = lax.broadcasted_iota(jnp.int32, s[0].shape, 1)
                s = [jnp.where(cidx >= r + hb * half, s[hb], -1e30) for hb in range(NSP)]
            for hb in range(NSP):
                carry = update(j, hb, s[hb], carry)
            return carry

        def finish(carry):
            m, acc = carry
            l = acc[VD:VD + 1, :]
            o_ref[...] = (acc[0:VD, :] / l).T
            lse_ref[...] = m + jnp.log2(l)

        def quad(t, cr):
            for u in range(4):
                cr = trip(4 * t + u, cr, False)
            return cr

        init = (jnp.full((1, tq), -1e30, F32), jnp.zeros((VDP, tq), F32))
        carry = lax.fori_loop(0, i >> 2, quad, init)
        carry = lax.fori_loop(i - (i & 3), i, lambda j, cr: trip(j, cr, False), carry)
        finish(trip(i, carry, True))

    return pl.pallas_call(
        body, name="attn_fwd", grid=(NH, nq),
        in_specs=[pl.BlockSpec((None, tq, HP), lambda h, i: (h, i, 0)),
                  pl.BlockSpec((None, S, HP), lambda h, i: (h, 0, 0)),
                  pl.BlockSpec((None, S, VD), lambda h, i: (h, 0, 0))],
        out_specs=[pl.BlockSpec((tq, VD), lambda h, i: (i, h)),
                   pl.BlockSpec((None, None, 1, tq), lambda h, i: (h, i, 0, 0))],
        out_shape=[jax.ShapeDtypeStruct((S, NH * VD), F32), jax.ShapeDtypeStruct((NH, nq, 1, tq), F32)],
        scratch_shapes=[pltpu.VMEM((S // tb, VDP, tb), BF16)],
        compiler_params=_cp(("arbitrary", "arbitrary")),
    )(q, k, v)


def _ssd_consts():
    tri = np.tril(np.ones((CH, CH), np.float32))
    e16 = np.zeros((128, SW), np.float32)
    for h in range(SH):
        e16[h, h * SP:(h + 1) * SP] = 1.0
    return jnp.asarray(tri, BF16), jnp.asarray(tri.T.copy(), BF16), jnp.asarray(e16, BF16)


def _ssd_conv(xraw, halo, cw_ref, cb_ref, ext):
    ext[0:8, :] = halo
    ext[8:8 + CH, :] = xraw
    cw = cw_ref[...]
    xc = cb_ref[...] + cw[0:1, :] * ext[5:5 + CH, :]
    for kk in range(1, CW):
        xc = xc + cw[kk:kk + 1, :] * ext[5 + kk:5 + kk + CH, :]
    return xc


def _ssd_chunk_common(xc, dtraw_ref, dtb_ref, alog_ref, tri_ref, e16_ref):
    sact = _sigmoid(xc)
    act = xc * sact
    lane = lax.broadcasted_iota(jnp.int32, (1, 128), 1)
    arow = jnp.where(lane < SH, -jnp.exp(alog_ref[...]), 0.0)
    dtpre = dtraw_ref[...] + dtb_ref[...]
    dt = _softplus(dtpre)
    a = dt * arow
    cum = _xmm(tri_ref[...], a, 3)
    cumx = _mm_x(cum, e16_ref[...], 3)
    dtx = _mm_x(dt, e16_ref[...], 2)
    return sact, act, arow, dtpre, dt, cum, cumx, dtx


def _ssd_fwd(xbc, dtraw, zs, conv_w, conv_b, dtb, alog, dskx, gssm):
    S = xbc.shape[0]
    nc = S // CH
    tri, _, e16 = _ssd_consts()

    def body(xbc_ref, halo_ref, dtraw_ref, zs_ref, cw_ref, cb_ref, dtb_ref, alog_ref, dsk_ref, g_ref, tri_ref, e16_ref,
             xc_ref, y_ref, htp_ref, o_ref, ht, ext):
        i = pl.program_id(0)

        @pl.when(i == 0)
        def _():
            ht[...] = jnp.zeros_like(ht)

        for c in range(NCK):
            chunk(c, i, xbc_ref, halo_ref, dtraw_ref, zs_ref, cw_ref, cb_ref, dtb_ref, alog_ref, dsk_ref, g_ref, tri_ref,
                  e16_ref, xc_ref, y_ref, htp_ref, o_ref, ht, ext)

    def chunk(c, i, xbc_ref, halo_ref, dtraw_ref, zs_ref, cw_ref, cb_ref, dtb_ref, alog_ref, dsk_ref, g_ref, tri_ref, e16_ref,
              xc_ref, y_ref, htp_ref, o_ref, ht, ext):
        rows = slice(c * CH, (c + 1) * CH)
        halo = jnp.where(i == 0, 0.0, halo_ref[...]) if c == 0 else xbc_ref[c * CH - 8:c * CH, :]
        xc = _ssd_conv(xbc_ref[rows, :], halo, cw_ref, cb_ref, ext)
        xc_ref[rows, :] = xc
        sact, act, arow, dtpre, dt, cum, cumx, dtx = _ssd_chunk_common(xc, dtraw_ref.at[rows, :], dtb_ref, alog_ref, tri_ref, e16_ref)
        cum_t = cum.T
        xs = act[:, 0:SW]
        lastx = cumx[CH - 1:CH, :]
        xh = xs * dtx
        eexp = jnp.exp(cumx)
        dte = jnp.exp(lastx - cumx)
        cdx = jnp.exp(lastx)
        htp = ht[...]
        htp_ref[c] = htp
        xw = (xh * dte).astype(BF16)
        xb = xh.astype(BF16)
        trim = tri_ref[...].astype(F32) > 0.5
        lane = lax.broadcasted_iota(jnp.int32, (CH, 128), 1)
        parts = []
        for g in range(SG):
            gl = slice(g * GW, (g + 1) * GW)
            bg = act[:, SW + g * SN:SW + (g + 1) * SN].astype(BF16)
            cg = act[:, SW + SG * SN + g * SN:SW + SG * SN + (g + 1) * SN].astype(BF16)
            cbm = _nt(cg, bg)
            yoff = eexp[:, gl] * _mm(cg, htp[:, gl].astype(BF16))
            ht[:, gl] = htp[:, gl] * cdx[:, gl] + _tn(bg, xw[:, gl])
            for pr in range(GW // 128):
                h0 = g * (SH // SG) + 2 * pr
                lo = g * GW + pr * 128
                xp = xb[:, lo:lo + 128]
                res = []
                for hh in (h0, h0 + 1):
                    seg = cum[:, hh:hh + 1] - cum_t[hh:hh + 1, :]
                    mh = jnp.where(trim, cbm * jnp.exp(seg), 0.0).astype(BF16)
                    res.append(_mm(mh, xp))
                parts.append(jnp.where(lane < SP, res[0], res[1]) + yoff[:, pr * 128:(pr + 1) * 128])
        y = jnp.concatenate(parts, axis=1) + xs * dsk_ref[...]
        y_ref[rows, :] = y
        z = zs_ref[rows, :]
        hf = y * (z * _sigmoid(z))
        outs = []
        for g in range(SG):
            hg = hf[:, g * GW:(g + 1) * GW]
            rs = lax.rsqrt(jnp.mean(hg * hg, axis=-1, keepdims=True) + RMS_EPS)
            outs.append(hg * rs)
        o_ref[rows, :] = (jnp.concatenate(outs, axis=1) * g_ref[...]).astype(BF16)

    row = lambda n: pl.BlockSpec((NCK * CH, n), lambda i: (i, 0))
    return pl.pallas_call(
        body, name="ssd_fwd", grid=(nc // NCK,),
        in_specs=[row(CC), pl.BlockSpec((8, CC), lambda i: (jnp.maximum(i * (NCK * CH // 8) - 1, 0), 0)), row(128), row(SW),
                  _const((CW, CC)), _const((1, CC)), _const((1, 128)), _const((1, 128)), _const((1, SW)), _const((1, SW)),
                  _const((CH, CH)), _const((128, SW))],
        out_specs=[row(CC), row(SW), pl.BlockSpec((NCK, SN, SW), lambda i: (i, 0, 0)), row(SW)],
        out_shape=[jax.ShapeDtypeStruct((S, CC), F32), jax.ShapeDtypeStruct((S, SW), F32),
                   jax.ShapeDtypeStruct((nc, SN, SW), F32), jax.ShapeDtypeStruct((S, SW), BF16)],
        scratch_shapes=[pltpu.VMEM((SN, SW), F32), pltpu.VMEM((8 + CH, CC), F32)],
        compiler_params=_cp(("arbitrary",)),
    )(xbc, xbc, dtraw, zs, conv_w, conv_b, dtb, alog, dskx, gssm, tri, e16)


def _outproj(o, za, ossm, x, tgt, wout, mod, ln_g, ln_b):
    S = x.shape[0]
    tm = min(TM, S)

    e8 = np.zeros((D, 128), np.float32)
    for h in range(NH):
        e8[h * VD:(h + 1) * VD, h] = 1.0
    e8 = jnp.asarray(e8, BF16)

    def body(o_ref, za_ref, os_ref, x_ref, t_ref, w_ref, mod_ref, g_ref, b_ref, e8_ref,
             gx_ref, do_ref, dza_ref, dos_ref, delta_ref, dw_ref, vec_ref):
        i = pl.program_id(0)

        @pl.when(i == 0)
        def _():
            dw_ref[...] = jnp.zeros_like(dw_ref)
            vec_ref[...] = jnp.zeros_like(vec_ref)

        gate = mod_ref[0:1, 2 * D:3 * D]
        ov = o_ref[...]
        z = za_ref[...]
        sz = _sigmoid(z)
        silz = z * sz
        a = (ov * silz).astype(BF16)
        osb = os_ref[...]
        mixed = _mm(a, w_ref[0:D, :]) + _mm(osb, w_ref[D:MIX, :])
        xv = x_ref[...]
        hres = ALPHA * xv + gate * mixed
        mu = jnp.mean(hres, axis=-1, keepdims=True)
        hc = hres - mu
        var = jnp.mean(hc * hc, axis=-1, keepdims=True)
        rstd = lax.rsqrt(var + LN_EPS)
        xhat = hc * rstd
        g = g_ref[...]
        yv = xhat * g + b_ref[...]
        err = yv - t_ref[...]
        dy = err * (1.0 / D)
        vec_ref[0:1, :] += jnp.sum(err * err, axis=0, keepdims=True)
        vec_ref[1:2, :] += jnp.sum(dy * xhat, axis=0, keepdims=True)
        vec_ref[2:3, :] += jnp.sum(dy, axis=0, keepdims=True)
        dxh = dy * g
        dh = rstd * (dxh - jnp.mean(dxh, axis=-1, keepdims=True) - xhat * jnp.mean(dxh * xhat, axis=-1, keepdims=True))
        gx_ref[...] = ALPHA * dh
        vec_ref[3:4, :] += jnp.sum(dh * mixed, axis=0, keepdims=True)
        dmixed = (gate * dh).astype(BF16)
        dw_ref[0:D, :] += _tn(a, dmixed)
        dw_ref[D:MIX, :] += _tn(osb, dmixed)
        da = _nt(dmixed, w_ref[0:D, :])
        dos_ref[...] = _nt(dmixed, w_ref[D:MIX, :])
        dov = da * silz
        do_ref[...] = dov.astype(BF16)
        dza_ref[...] = (da * ov * (sz * (1.0 + z * (1.0 - sz)))).astype(BF16)
        delta_ref[:, 0, :] = _mm_x(dov * ov, e8_ref[...], 2).T[0:NH, :]

    row = lambda n: pl.BlockSpec((tm, n), lambda i: (i, 0))
    return pl.pallas_call(
        body, name="outproj", grid=(S // tm,),
        in_specs=[row(D), row(D), row(D), row(D), row(D), _const((MIX, D)), _const((8, 3 * D)), _const((1, D)), _const((1, D)),
                  _const((D, 128))],
        out_specs=[row(D), row(D), row(D), row(D), pl.BlockSpec((NH, None, 1, tm), lambda i: (0, i, 0, 0)),
                   _full((MIX, D)), _full((8, D))],
        out_shape=[jax.ShapeDtypeStruct((S, D), F32), jax.ShapeDtypeStruct((S, D), BF16), jax.ShapeDtypeStruct((S, D), BF16),
                   jax.ShapeDtypeStruct((S, D), F32), jax.ShapeDtypeStruct((NH, S // tm, 1, tm), F32),
                   jax.ShapeDtypeStruct((MIX, D), F32), jax.ShapeDtypeStruct((8, D), F32)],
        compiler_params=_cp(("arbitrary",)),
    )(o, za, ossm, x, tgt, wout, mod, ln_g, ln_b, e8)


def _attn_bwd(q, k, v, do, lse, delta):
    _, S, _ = q.shape
    tk = min(TQ, S // 2)
    nk = S // tk
    tq = 2 * tk
    nq = S // tq

    def body(k_ref, v_ref, q_ref, do_ref, lse_ref, dl_ref, dk_ref, dv_ref, dq_ref, dqt_ref):
        j = pl.program_id(1)
        kb = k_ref[...]
        ktb = kb.T
        vb = v_ref[...]

        @pl.when(j == 0)
        def _():
            dqt_ref[...] = jnp.zeros_like(dqt_ref)

        dk_ref[...] = jnp.zeros_like(dk_ref)
        dv_ref[...] = jnp.zeros_like(dv_ref)

        def step(i, masked, lo=0):
            off = pl.multiple_of(i * tq + lo, tk)
            qb = q_ref[pl.ds(off, tq - lo), :]
            dob = do_ref[pl.ds(off, tq - lo), :]
            pt = jnp.exp2(_nt(kb, qb) - lse_ref[i][:, lo:tq])
            if masked:
                r = lax.broadcasted_iota(jnp.int32, pt.shape, 0)
                cidx = lax.broadcasted_iota(jnp.int32, pt.shape, 1)
                pt = jnp.where(i * tq + lo + cidx >= j * tk + r, pt, 0.0)
            dv_ref[...] += _mm(pt.astype(BF16), dob)
            dsb = (pt * (_nt(vb, dob) - dl_ref[i][:, lo:tq])).astype(BF16)
            dk_ref[...] += _mm(dsb, qb)
            dqt_ref[i, :, lo:tq] += _mm(ktb, dsb)

        first = j >> 1

        @pl.when((j & 1) == 0)
        def _():
            step(first, True)

        @pl.when((j & 1) == 1)
        def _():
            step(first, True, tk)
            dq_ref[...] = dqt_ref[first].T

        def loop_body(t, carry):
            for u in range(4):
                step(first + 1 + 4 * t + u, False)
            return carry

        def tail_body(i, carry):
            step(i, False)
            return carry

        rest = nq - 1 - first
        lax.fori_loop(0, rest >> 2, loop_body, 0)
        lax.fori_loop(nq - (rest & 3), nq, tail_body, 0)

        dk_ref[...] = dk_ref[...] * LN2

    return pl.pallas_call(
        body, name="attn_bwd", grid=(NH, nk),
        in_specs=[pl.BlockSpec((None, tk, HP), lambda h, j: (h, j, 0)),
                  pl.BlockSpec((None, tk, VD), lambda h, j: (h, j, 0)),
                  pl.BlockSpec((None, S, HP), lambda h, j: (h, 0, 0)),
                  pl.BlockSpec((S, VD), lambda h, j: (0, h)),
                  pl.BlockSpec((None, nq, 1, tq), lambda h, j: (h, 0, 0, 0)),
                  pl.BlockSpec((None, nq, 1, tq), lambda h, j: (h, 0, 0, 0))],
        out_specs=[pl.BlockSpec((None, tk, HP), lambda h, j: (h, j, 0)),
                   pl.BlockSpec((None, tk, VD), lambda h, j: (h, j, 0)),
                   pl.BlockSpec((None, tq, HP), lambda h, j: (h, j >> 1, 0))],
        out_shape=[jax.ShapeDtypeStruct((NH, S, HP), F32), jax.ShapeDtypeStruct((NH, S, VD), F32),
                   jax.ShapeDtypeStruct((NH, S, HP), F32)],
        scratch_shapes=[pltpu.VMEM((nq, HP, tq), F32)],
        compiler_params=_cp(("arbitrary", "arbitrary")),
    )(k, v, q, do, lse.reshape(NH, nq, 1, tq), delta.reshape(NH, nq, 1, tq))


def _ssd_bwd(xbc, xc, dtraw, zs, y, htp, dossm, conv_w, dtb, alog, dskx, gssm):
    S = xbc.shape[0]
    nc = S // CH
    tri, triu, e16 = _ssd_consts()

    def body(xbc_ref, xc_ref, dtraw_ref, zs_ref, y_ref, htp_ref, dos_ref,
             cw_ref, dtb_ref, alog_ref, dsk_ref, g_ref, tri_ref, triu_ref, e16_ref,
             dxbc_ref, ddt_ref, dzs_ref, dcw_ref, dcb_ref, dvec_ref, dg_ref,
             dht, dext, dskacc):
        r = pl.program_id(0)

        @pl.when(r == 0)
        def _():
            dht[...] = jnp.zeros_like(dht)
            dext[CH:CH + 8, :] = jnp.zeros((8, CC), F32)
            dskacc[...] = jnp.zeros_like(dskacc)
            dcw_ref[...] = jnp.zeros_like(dcw_ref)
            dcb_ref[...] = jnp.zeros_like(dcb_ref)
            dvec_ref[...] = jnp.zeros_like(dvec_ref)
            dg_ref[...] = jnp.zeros_like(dg_ref)

        for c in reversed(range(NCK)):
            rows = slice(c * CH, (c + 1) * CH)
            chunk(xbc_ref.at[rows, :], xc_ref.at[rows, :], dtraw_ref.at[rows, :], zs_ref.at[rows, :], y_ref.at[rows, :],
                  htp_ref.at[c], dos_ref.at[rows, :], cw_ref, dtb_ref, alog_ref, dsk_ref, g_ref, tri_ref, triu_ref, e16_ref,
                  dxbc_ref.at[rows, :], ddt_ref.at[rows, :], dzs_ref.at[rows, :], dcw_ref, dcb_ref, dvec_ref, dg_ref,
                  dht, dext, dskacc)

        @pl.when(r == nc // NCK - 1)
        def _():
            lane = lax.broadcasted_iota(jnp.int32, (1, 128), 1)
            arow = jnp.where(lane < SH, -jnp.exp(alog_ref[...]), 0.0)
            dvec_ref[1:2, :] = dvec_ref[1:2, :] * arow
            dvec_ref[2:3, :] = _nt_x(jnp.broadcast_to(dskacc[...], (8, SW)), e16_ref[...], 3)[0:1, :]

    def chunk(xbc_ref, xc_ref, dtraw_ref, zs_ref, y_ref, htp_ref, dos_ref,
              cw_ref, dtb_ref, alog_ref, dsk_ref, g_ref, tri_ref, triu_ref, e16_ref,
              dxbc_ref, ddt_ref, dzs_ref, dcw_ref, dcb_ref, dvec_ref, dg_ref,
              dht, dext, dskacc):
        xc = xc_ref[...]
        sact, act, arow, dtpre, dt, cum, cumx, dtx = _ssd_chunk_common(xc, dtraw_ref, dtb_ref, alog_ref, tri_ref, e16_ref)
        cum_t = cum.T
        xs = act[:, 0:SW]
        lastx = cumx[CH - 1:CH, :]
        xh = xs * dtx
        eexp = jnp.exp(cumx)
        dte = jnp.exp(lastx - cumx)
        cdx = jnp.exp(lastx)
        trim = tri_ref[...].astype(F32) > 0.5
        lane = lax.broadcasted_iota(jnp.int32, (CH, 128), 1)
        rowi = lax.broadcasted_iota(jnp.int32, (CH, 128), 0)

        yv = y_ref[...]
        z = zs_ref[...]
        sz = _sigmoid(z)
        silz = z * sz
        hf = yv * silz
        dn = dos_ref[...] * g_ref[...]
        dhf_parts, nrm_parts = [], []
        for g in range(SG):
            gl = slice(g * GW, (g + 1) * GW)
            hg = hf[:, gl]
            rs = lax.rsqrt(jnp.mean(hg * hg, axis=-1, keepdims=True) + RMS_EPS)
            ng = hg * rs
            dng = dn[:, gl]
            dhf_parts.append(rs * (dng - ng * jnp.mean(dng * ng, axis=-1, keepdims=True)))
            nrm_parts.append(ng)
        nrm = jnp.concatenate(nrm_parts, axis=1)
        dhf = jnp.concatenate(dhf_parts, axis=1)
        dg_ref[...] += jnp.sum(dos_ref[...] * nrm, axis=0, keepdims=True)
        dyv = dhf * silz
        dzs_ref[...] = (dhf * yv * (sz * (1.0 + z * (1.0 - sz)))).astype(BF16)
        dskacc[...] += jnp.sum(dyv * xs, axis=0, keepdims=True)
        dxs_skip = dyv * dsk_ref[...]

        dhtn = dht[...]
        hp = htp_ref[...]
        dlastx = jnp.sum(dhtn * hp, axis=0, keepdims=True) * cdx
        xb = xh.astype(BF16)
        xwf = xh * dte
        dcum = jnp.zeros((CH, 128), F32)
        dcum_t = jnp.zeros((128, CH), F32)
        dxh_parts, dcumx_parts, dlast_parts, db_parts, dc_parts = [], [], [], [], []
        for g in range(SG):
            gl = slice(g * GW, (g + 1) * GW)
            bg = act[:, SW + g * SN:SW + (g + 1) * SN].astype(BF16)
            cg = act[:, SW + SG * SN + g * SN:SW + SG * SN + (g + 1) * SN].astype(BF16)
            hpg = hp[:, gl].astype(BF16)
            dhn = dhtn[:, gl].astype(BF16)
            dyg = dyv[:, gl]
            dz = (dyg * eexp[:, gl]).astype(BF16)
            dcg = _nt(dz, hpg)
            dht[:, gl] = dhtn[:, gl] * cdx[:, gl] + _tn(cg, dz)
            yoff = eexp[:, gl] * _mm(cg, hpg)
            dcumx_g = dyg * yoff
            dbg = _nt(xwf[:, gl].astype(BF16), dhn)
            dxw = _mm(bg, dhn)
            ddte = dxw * xwf[:, gl]
            dcumx_parts.append(dcumx_g - ddte)
            dlast_parts.append(jnp.sum(ddte, axis=0, keepdims=True))
            dxh_g = dxw * dte[:, gl]
            cbm = _nt(cg, bg)
            dcb = jnp.zeros((CH, CH), F32)
            dxp_parts = []
            for pr in range(GW // 128):
                h0 = g * (SH // SG) + 2 * pr
                lo = g * GW + pr * 128
                xp = xb[:, lo:lo + 128]
                dyp = dyv[:, lo:lo + 128]
                dxp = jnp.zeros((CH, 128), F32)
                for idx, hh in enumerate((h0, h0 + 1)):
                    decay = jnp.where(trim, jnp.exp(cum[:, hh:hh + 1] - cum_t[hh:hh + 1, :]), 0.0)
                    mh = cbm * decay
                    keep = (lane < SP) if idx == 0 else (lane >= SP)
                    dym = jnp.where(keep, dyp, 0.0).astype(BF16)
                    dm = _nt(dym, xp)
                    dxp = dxp + _tn(mh.astype(BF16), dym)
                    gm = dm * mh
                    dcum = dcum + jnp.where(lane == hh, jnp.sum(gm, axis=1, keepdims=True), 0.0)
                    dcum_t = dcum_t - jnp.where(rowi == hh, jnp.sum(gm, axis=0, keepdims=True), 0.0)
                    dcb = dcb + dm * decay
                dxp_parts.append(dxp)
            dxh_parts.append(dxh_g + jnp.concatenate(dxp_parts, axis=1))
            dcbb = dcb.astype(BF16)
            dc_parts.append(dcg + _mm(dcbb, bg))
            db_parts.append(dbg + _tn(dcbb, cg))
        dxh = jnp.concatenate(dxh_parts, axis=1)
        dcumx = jnp.concatenate(dcumx_parts, axis=1)
        dlastx = dlastx + jnp.concatenate(dlast_parts, axis=1)
        e16 = e16_ref[...]
        dlast128 = _nt_x(jnp.broadcast_to(dlastx, (8, SW)), e16, 2)[0:1, :]
        dcum = dcum + dcum_t.T + _nt_x(dcumx, e16, 2) + jnp.where(rowi == CH - 1, dlast128, 0.0)
        da = _xmm(triu_ref[...], dcum, 2)
        ddt = da * arow + _nt_x(dxh * xs, e16, 2)
        dvec_ref[1:2, :] += jnp.sum(da * dt, axis=0, keepdims=True)
        ddtraw = jnp.where(lane < SH, ddt * _sigmoid(dtpre), 0.0)
        dvec_ref[0:1, :] += jnp.sum(ddtraw, axis=0, keepdims=True)
        ddt_ref[...] = ddtraw.astype(BF16)
        dxs = dxs_skip + dxh * dtx
        dact = jnp.concatenate([dxs] + db_parts + dc_parts, axis=1)
        dxc = dact * (sact * (1.0 + xc * (1.0 - sact)))

        dcb_ref[...] += jnp.sum(dxc, axis=0, keepdims=True)
        dext[0:CH, :] = dxc
        cw = cw_ref[...]
        xraw = xbc_ref[...]
        dxr = cw[CW - 1:CW, :] * dxc
        dcw_ref[CW - 1:CW, :] += jnp.sum(dxc * xraw, axis=0, keepdims=True)
        for kk in range(CW - 1):
            dwin = dext[CW - 1 - kk:CW - 1 - kk + CH, :]
            dxr = dxr + cw[kk:kk + 1, :] * dwin
            dcw_ref[kk:kk + 1, :] += jnp.sum(dwin * xraw, axis=0, keepdims=True)
        dxbc_ref[...] = dxr.astype(BF16)
        dext[CH:CH + 8, :] = dxc[0:8, :]

    ng = nc // NCK
    rev = lambda n: pl.BlockSpec((NCK * CH, n), lambda r: (ng - 1 - r, 0))
    return pl.pallas_call(
        body, name="ssd_bwd", grid=(ng,),
        in_specs=[rev(CC), rev(CC),
                  rev(128), rev(SW), rev(SW), pl.BlockSpec((NCK, SN, SW), lambda r: (ng - 1 - r, 0, 0)), rev(SW),
                  _const((CW, CC)), _const((1, 128)), _const((1, 128)), _const((1, SW)), _const((1, SW)),
                  _const((CH, CH)), _const((CH, CH)), _const((128, SW))],
        out_specs=[rev(CC), rev(128), rev(SW), _full((CW, CC)), _full((1, CC)), _full((8, 128)), _full((1, SW))],
        out_shape=[jax.ShapeDtypeStruct((S, CC), BF16), jax.ShapeDtypeStruct((S, 128), BF16), jax.ShapeDtypeStruct((S, SW), BF16),
                   jax.ShapeDtypeStruct((CW, CC), F32), jax.ShapeDtypeStruct((1, CC), F32),
                   jax.ShapeDtypeStruct((8, 128), F32), jax.ShapeDtypeStruct((1, SW), F32)],
        scratch_shapes=[pltpu.VMEM((SN, SW), F32), pltpu.VMEM((CH + 8, CC), F32), pltpu.VMEM((1, SW), F32)],
        compiler_params=_cp(("arbitrary",)),
    )(xbc, xc, dtraw, zs, y, htp, dossm, conv_w, dtb, alog, dskx, gssm, tri, triu, e16)


def _mla_bwd(dq, dk, dv, qlat, ckv, qg, kvg, wq, wk, wv, pos, invf):
    S = qlat.shape[0]
    tm = min(TQ, S)

    def body(dq_ref, dk_ref, dv_ref, ql_ref, ckv_ref, qg_ref, kvg_ref, wq_ref, wk_ref, wv_ref, pos_ref, invf_ref,
             dql_ref, dckv_ref, dkr_ref, dwq_ref, dwk_ref, dwv_ref, dqg_ref, dkvg_ref):
        i = pl.program_id(0)

        @pl.when(i == 0)
        def _():
            dwq_ref[...] = jnp.zeros_like(dwq_ref)
            dwk_ref[...] = jnp.zeros_like(dwk_ref)
            dwv_ref[...] = jnp.zeros_like(dwv_ref)
            dqg_ref[...] = jnp.zeros_like(dqg_ref)
            dkvg_ref[...] = jnp.zeros_like(dkvg_ref)

        hm = tm // 2
        for hf in range(2):
            rows = slice(hf * hm, (hf + 1) * hm)
            half(dq_ref.at[:, rows, :], dk_ref.at[:, rows, :], dv_ref.at[:, rows, :], ql_ref.at[rows, :], ckv_ref.at[rows, :],
                 qg_ref, kvg_ref, wq_ref, wk_ref, wv_ref, pos_ref.at[rows, :], invf_ref,
                 dql_ref.at[rows, :], dckv_ref.at[rows, :], dkr_ref.at[rows, :], dwq_ref, dwk_ref, dwv_ref, dqg_ref, dkvg_ref)

    def half(dq_ref, dk_ref, dv_ref, ql_ref, ckv_ref, qg_ref, kvg_ref, wq_ref, wk_ref, wv_ref, pos_ref, invf_ref,
             dql_ref, dckv_ref, dkr_ref, dwq_ref, dwk_ref, dwv_ref, dqg_ref, dkvg_ref):
        ang = pos_ref[...].astype(F32) * invf_ref[...]
        cs = jnp.cos(ang)
        sn = jnp.sin(ang)

        def rms_bwd(v, g, dn, dg_ref):
            r = lax.rsqrt(jnp.mean(v * v, axis=-1, keepdims=True) + RMS_EPS)
            vh = v * r
            dg_ref[...] += jnp.sum(dn * vh, axis=0, keepdims=True)
            dvh = dn * g
            return vh, r * (dvh - vh * jnp.mean(dvh * vh, axis=-1, keepdims=True))

        pieces = []
        for h in range(NH):
            dqh = dq_ref[h]
            pieces.append(dqh[:, 0:NOPE] * SCALE)
            pieces.append(_rope_t(dqh[:, NOPE:HP], cs, sn) * SCALE)
        dqf = jnp.concatenate(pieces, axis=1).astype(BF16)
        ql = ql_ref[...]
        g = qg_ref[...]
        dqn = _nt(dqf, wq_ref[...])
        qh, dql = rms_bwd(ql, g, dqn, dqg_ref)
        dwq_ref[...] += _tn((qh * g).astype(BF16), dqf)
        dql_ref[...] = dql.astype(BF16)

        dkn_p = jnp.concatenate([dk_ref[h, :, 0:NOPE] for h in range(NH)], axis=1).astype(BF16)
        dvf = jnp.concatenate([dv_ref[h] for h in range(NH)], axis=1).astype(BF16)
        dkr = dk_ref[0, :, NOPE:HP]
        for h in range(1, NH):
            dkr = dkr + dk_ref[h, :, NOPE:HP]
        lane = lax.broadcasted_iota(jnp.int32, dkr.shape, 1)
        dkr_ref[...] = jnp.where(lane < ROPE, _rope_t(dkr, cs, sn), 0.0).astype(BF16)
        cv = ckv_ref[...]
        gk = kvg_ref[...]
        dkn = _nt(dkn_p, wk_ref[...]) + _nt(dvf, wv_ref[...])
        kh, dckv = rms_bwd(cv, gk, dkn, dkvg_ref)
        knb = (kh * gk).astype(BF16)
        dwk_ref[...] += _tn(knb, dkn_p)
        dwv_ref[...] += _tn(knb, dvf)
        dckv_ref[...] = dckv.astype(BF16)

    row = lambda n: pl.BlockSpec((tm, n), lambda i: (i, 0))
    heads = lambda n: pl.BlockSpec((NH, tm, n), lambda i: (0, i, 0))
    return pl.pallas_call(
        body, name="mla_bwd", grid=(S // tm,),
        in_specs=[heads(HP), heads(HP), heads(VD), row(QL), row(KVL), _const((1, QL)), _const((1, KVL)),
                  _const((QL, NH * HP)), _const((KVL, NH * NOPE)), _const((KVL, NH * VD)), row(1), _const((1, 128))],
        out_specs=[row(QL), row(KVL), row(128), _full((QL, NH * HP)), _full((KVL, NH * NOPE)), _full((KVL, NH * VD)),
                   _full((1, QL)), _full((1, KVL))],
        out_shape=[jax.ShapeDtypeStruct((S, QL), BF16), jax.ShapeDtypeStruct((S, KVL), BF16), jax.ShapeDtypeStruct((S, 128), BF16),
                   jax.ShapeDtypeStruct((QL, NH * HP), F32), jax.ShapeDtypeStruct((KVL, NH * NOPE), F32),
                   jax.ShapeDtypeStruct((KVL, NH * VD), F32), jax.ShapeDtypeStruct((1, QL), F32), jax.ShapeDtypeStruct((1, KVL), F32)],
        compiler_params=_cp(("arbitrary",)),
    )(dq, dk, dv, qlat, ckv, qg, kvg, wq, wk, wv, pos, invf)


def _inproj_bwd(x, gx1, mod, win, dql, dckv, dza, dxbc, dzs, dkr, ddt):
    S = x.shape[0]
    tm = min(TM, S)

    def body(x_ref, gx1_ref, mod_ref, win_ref, dql_ref, dckv_ref, dza_ref, dxbc_ref, dzs_ref, dkr_ref, ddt_ref,
             gx_ref, dw_ref, vec_ref):
        i = pl.program_id(0)

        @pl.when(i == 0)
        def _():
            dw_ref[...] = jnp.zeros_like(dw_ref)
            vec_ref[...] = jnp.zeros_like(vec_ref)

        shift = mod_ref[0:1, 0:D]
        scale = mod_ref[0:1, D:2 * D]
        xv = x_ref[...]
        ut = (xv * (1.0 + scale) + shift).T.astype(BF16)
        pieces = (dql_ref, dckv_ref, dza_ref, dxbc_ref, dzs_ref, dkr_ref, ddt_ref)
        du = jnp.zeros((tm, D), F32)
        lo = 0
        for p_ref in pieces:
            n = p_ref.shape[1]
            dp = p_ref[...]
            du = du + _nt(dp, win_ref[:, lo:lo + n])
            dw_ref[:, lo:lo + n] += _mm(ut, dp)
            lo += n
        vec_ref[0:1, :] += jnp.sum(du, axis=0, keepdims=True)
        vec_ref[1:2, :] += jnp.sum(du * xv, axis=0, keepdims=True)
        gx_ref[...] = gx1_ref[...] + du * (1.0 + scale)

    row = lambda n: pl.BlockSpec((tm, n), lambda i: (i, 0))
    return pl.pallas_call(
        body, name="inproj_bwd", grid=(S // tm,),
        in_specs=[row(D), row(D), _const((8, 3 * D)), _const((D, IN_P)), row(QL), row(KVL), row(D), row(CC), row(D),
                  row(128), row(128)],
        out_specs=[row(D), pl.BlockSpec((D, IN_P), lambda i: (0, 0), pipeline_mode=pl.Buffered(1)), _full((8, D))],
        out_shape=[jax.ShapeDtypeStruct((S, D), F32), jax.ShapeDtypeStruct((D, IN_P), F32), jax.ShapeDtypeStruct((8, D), F32)],
        compiler_params=_cp(("arbitrary",)),
    )(x, gx1, mod, win, dql, dckv, dza, dxbc, dzs, dkr, ddt)


def _ada_bwd(callt, dmods):
    w = dmods.shape[1]

    def body(c_ref, d_ref, o_ref):
        acc = c_ref[:, 0:1] * d_ref[0:1, :]
        for s in range(1, 8):
            acc = acc + c_ref[:, s:s + 1] * d_ref[s:s + 1, :]
        o_ref[0] = acc

    return pl.pallas_call(body, name="ada_bwd", out_shape=jax.ShapeDtypeStruct((1, D, w), F32),
                          compiler_params=_cp())(callt, dmods)


def _adamw(name, parts, w, m, v):
    rows, ncol = w.shape
    tr = min(rows, 128)
    nparts = parts.shape[0]

    def body(p_ref, w_ref, m_ref, v_ref, g_ref, d_ref, nm_ref, nv_ref):
        g = p_ref[0].astype(F32)
        for s in range(1, nparts):
            g = g + p_ref[s].astype(F32)
        g_ref[...] = g
        nm = B1 * m_ref[...] + (1.0 - B1) * g
        nv = B2 * v_ref[...] + (1.0 - B2) * (g * g)
        nm_ref[...] = nm
        nv_ref[...] = nv
        m_hat = nm / (1.0 - B1 ** STEP)
        v_hat = nv / (1.0 - B2 ** STEP)
        d_ref[...] = -LR * (m_hat / (jnp.sqrt(v_hat) + EPS) + WD * w_ref[...])

    row = pl.BlockSpec((tr, ncol), lambda i: (i, 0))
    sd = jax.ShapeDtypeStruct((rows, ncol), F32)
    return pl.pallas_call(
        body, name="adamw_" + name, grid=(rows // tr,),
        in_specs=[pl.BlockSpec((nparts, tr, ncol), lambda i: (0, i, 0)), row, row, row],
        out_specs=[row, row, row, row], out_shape=[sd, sd, sd, sd],
        compiler_params=_cp(("arbitrary",)),
    )(parts, w, m, v)


_SMALL = (("b_ada", 3 * D), ("conv_w", CW * CC // 4), ("conv_b", CC), ("ssm_norm_g", SW), ("ln_g", D), ("ln_b", D),
          ("q_norm_g", QL), ("kv_norm_g", KVL), ("dt_bias", SH), ("a_log", SH), ("d_skip", SH), ("loss", 128))


def _pack_small(d, lead):
    flat = [d[name].reshape(d[name].shape[:lead] + (-1,)) for name, _ in _SMALL]
    used = sum(f.shape[lead] for f in flat)
    pad = jnp.zeros(flat[0].shape[:lead] + (R_SMALL * 1024 - used,), F32)
    return jnp.concatenate(flat + [pad], axis=lead).reshape(flat[0].shape[:lead] + (R_SMALL, 1024))


def _unpack_small(p):
    flat = p.reshape(-1)
    out, r = {}, 0
    for name, n in _SMALL:
        out[name] = flat[r:r + n]
        r += n
    return out


def _in_to_padded(w):
    z = lambda n: jnp.zeros((w.shape[0], n), w.dtype)
    return jnp.concatenate([w[:, 0:384], w[:, 384:640], w[:, 704:1728], w[:, 1728:3264], w[:, 3280:4304],
                            w[:, 640:704], z(64), w[:, 3264:3280], z(112)], axis=1)


def _in_from_padded(g):
    return jnp.concatenate([g[:, 0:384], g[:, 384:640], g[:, P_KR[0]:P_KR[0] + 64], g[:, 640:1664], g[:, 1664:3200],
                            g[:, P_DT[0]:P_DT[0] + 16], g[:, 3200:4224]], axis=1)


def kernel(x, c, positions, w_ada, b_ada, w_in, q_norm_g, w_qb, kv_norm_g, w_kvb, conv_w, conv_b, dt_bias, a_log, d_skip, ssm_norm_g, w_out, ln_g, ln_b, loss_target, m_w_ada, m_b_ada, m_w_in, m_q_norm_g, m_w_qb, m_kv_norm_g, m_w_kvb, m_conv_w, m_conv_b, m_dt_bias, m_a_log, m_d_skip, m_ssm_norm_g, m_w_out, m_ln_g, m_ln_b, v_w_ada, v_b_ada, v_w_in, v_q_norm_g, v_w_qb, v_kv_norm_g, v_w_kvb, v_conv_w, v_conv_b, v_dt_bias, v_a_log, v_d_skip, v_ssm_norm_g, v_w_out, v_ln_g, v_ln_b):
    S = x.shape[1]
    xv = x[0]
    tgt = loss_target[0]

    cw16 = jnp.concatenate([conv_w[0], jnp.zeros((16 - CW, CC // 4), F32)], axis=0)
    f_in, f_qb, f_kvb, f_out, f_cw = _gather_weights(
        [w_in[0].astype(BF16), w_qb[0].astype(BF16), w_kvb[0].astype(BF16), w_out[0].astype(BF16), cw16])
    cat1 = lambda f: jnp.concatenate([f[k] for k in range(4)], axis=1)
    win = _in_to_padded(cat1(f_in))
    wqb = cat1(f_qb).reshape(QL, NH, QKD)
    wq = jnp.concatenate([wqb, jnp.zeros((QL, NH, HP - QKD), BF16)], axis=2).reshape(QL, NH * HP)
    wkvb = cat1(f_kvb).reshape(KVL, NH, NOPE + VD)
    wk = wkvb[:, :, 0:NOPE].reshape(KVL, NH * NOPE)
    wv = wkvb[:, :, NOPE:].reshape(KVL, NH * VD)
    wout = f_out.reshape(MIX, D)
    cwf = cat1(f_cw[:, 0:CW, :])

    half = ROPE // 2
    invf = 1.0 / (ROPE_THETA ** (jnp.arange(half, dtype=F32) / half))
    invf = jnp.concatenate([invf, invf, jnp.zeros((128 - ROPE,), F32)]).reshape(1, 128)
    pos = positions.reshape(S, 1)
    pad128 = lambda a: jnp.concatenate([a.reshape(1, SH), jnp.zeros((1, 128 - SH), F32)], axis=1)
    dtb, alog = pad128(dt_bias), pad128(a_log)
    dskx = jnp.repeat(d_skip.reshape(SH), SP).reshape(1, SW)

    my_c = lax.axis_index("c")
    (call,) = _exchange("gather_c", [jnp.broadcast_to(c.reshape(1, 1, D), (4, 1, D))])
    call = call.reshape(8, D)
    mods = _ada(call, w_ada[0])
    (mrows,) = _exchange("scatter_mod", [mods.reshape(4, 2, 3 * D // 4)])
    mine = lax.dynamic_index_in_dim(mrows.reshape(4, 2, 2, 3 * D // 4)[:, 0], my_c, axis=1, keepdims=False)
    mod = jnp.broadcast_to(mine.reshape(1, 3 * D) + b_ada, (8, 3 * D))
    qlat, ckv, za, xbc, zs, dtraw, q, k, v = _inproj(xv, mod, win, q_norm_g, kv_norm_g, wq, wk, wv, pos, invf)
    o, lse = _attn_fwd(q, k, v)
    xc, y, htp, ossm = _ssd_fwd(xbc, dtraw, zs, cwf, conv_b, dtb, alog, dskx, ssm_norm_g)
    gx1, do, dza, dossm, delta, dwout, vec_o = _outproj(o, za, ossm, xv, tgt, wout, mod, ln_g, ln_b)
    loss_part = jnp.zeros((128,), F32).at[0].set(0.5 / D * jnp.sum(vec_o[0]))

    dk, dv, dq = _attn_bwd(q, k, v, do, lse, delta)
    dxbc, ddt, dzs, dcw, dcb, dvec, dgssm = _ssd_bwd(xbc, xc, dtraw, zs, y, htp, dossm, cwf, dtb, alog, dskx, ssm_norm_g)
    dql, dckv, dkr, dwq, dwk, dwv, dqg, dkvg = _mla_bwd(dq, dk, dv, qlat, ckv, q_norm_g, kv_norm_g, wq, wk, wv, pos, invf)
    gx, dwin, vec_i = _inproj_bwd(xv, gx1, mod, win, dql, dckv, dza, dxbc, dzs, dkr, ddt)
    dmod = jnp.concatenate([vec_i[0:1], vec_i[1:2], vec_o[3:4]], axis=1)

    cols = lambda g: g.reshape(g.shape[0], 4, g.shape[1] // 4).transpose(1, 0, 2)
    g_in = cols(_in_from_padded(dwin)).astype(BF16)
    g_qb = cols(dwq.reshape(QL, NH, HP)[:, :, 0:QKD].reshape(QL, NH * QKD)).astype(BF16)
    g_kvb = cols(jnp.concatenate([dwk.reshape(KVL, NH, NOPE), dwv.reshape(KVL, NH, VD)], axis=2)
                 .reshape(KVL, NH * (NOPE + VD))).astype(BF16)
    g_out = dwout.reshape(4, MIX // 4, D).astype(BF16)
    small = {"b_ada": dmod, "conv_b": dcb, "ssm_norm_g": dgssm, "ln_g": vec_o[1:2], "ln_b": vec_o[2:3],
             "q_norm_g": dqg, "kv_norm_g": dkvg, "dt_bias": dvec[0:1, 0:SH], "a_log": dvec[1:2, 0:SH], "d_skip": dvec[2:3, 0:SH],
             "loss": loss_part}
    small = {n: jnp.broadcast_to(a.reshape(1, -1), (4, a.size)) for n, a in small.items()}
    small["conv_w"] = cols(dcw).reshape(4, CW * CC // 4)
    gsmall = _pack_small(small, 1)

    r_in, r_qb, r_kvb, r_out, rs, dmods = _exchange(
        "exchange_grads", [g_in, g_qb, g_kvb, g_out, gsmall, jnp.broadcast_to(dmod.reshape(1, 1, 3 * D), (4, 1, 3 * D))])
    chip = 2 * lax.axis_index("x") + lax.axis_index("y")
    dmods = lax.dynamic_slice_in_dim(dmods.reshape(8, 3 * D), chip * (3 * D // 4), 3 * D // 4, axis=1)
    g_ada = _ada_bwd(call.T, dmods)
    res = dict(w_ada=_adamw("w_ada", g_ada, w_ada[0], m_w_ada[0], v_w_ada[0]),
               w_in=_adamw("w_in", r_in, w_in[0], m_w_in[0], v_w_in[0]),
               w_qb=_adamw("w_qb", r_qb, w_qb[0], m_w_qb[0], v_w_qb[0]),
               w_kvb=_adamw("w_kvb", r_kvb, w_kvb[0], m_w_kvb[0], v_w_kvb[0]),
               w_out=_adamw("w_out", r_out, w_out[0], m_w_out[0], v_w_out[0]))
    wsm = _pack_small(dict(b_ada=b_ada, conv_w=conv_w, conv_b=conv_b, ssm_norm_g=ssm_norm_g, ln_g=ln_g, ln_b=ln_b,
                           q_norm_g=q_norm_g, kv_norm_g=kv_norm_g, dt_bias=dt_bias, a_log=a_log, d_skip=d_skip, loss=jnp.zeros((128,), F32)), 0)
    msm = _pack_small(dict(b_ada=m_b_ada, conv_w=m_conv_w, conv_b=m_conv_b, ssm_norm_g=m_ssm_norm_g, ln_g=m_ln_g, ln_b=m_ln_b,
                           q_norm_g=m_q_norm_g, kv_norm_g=m_kv_norm_g, dt_bias=m_dt_bias, a_log=m_a_log, d_skip=m_d_skip, loss=jnp.zeros((128,), F32)), 0)
    vsm = _pack_small(dict(b_ada=v_b_ada, conv_w=v_conv_w, conv_b=v_conv_b, ssm_norm_g=v_ssm_norm_g, ln_g=v_ln_g, ln_b=v_ln_b,
                           q_norm_g=v_q_norm_g, kv_norm_g=v_kv_norm_g, dt_bias=v_dt_bias, a_log=v_a_log, d_skip=v_d_skip, loss=jnp.zeros((128,), F32)), 0)
    sm = _adamw("small", rs, wsm, msm, vsm)

    order = ["w_ada", "b_ada", "w_in", "q_norm_g", "w_qb", "kv_norm_g", "w_kvb", "conv_w", "conv_b", "dt_bias", "a_log",
             "d_skip", "ssm_norm_g", "w_out", "ln_g", "ln_b"]
    shapes = dict(w_ada=w_ada.shape, b_ada=b_ada.shape, w_in=w_in.shape, q_norm_g=q_norm_g.shape, w_qb=w_qb.shape,
                  kv_norm_g=kv_norm_g.shape, w_kvb=w_kvb.shape, conv_w=conv_w.shape, conv_b=conv_b.shape, dt_bias=dt_bias.shape,
                  a_log=a_log.shape, d_skip=d_skip.shape, ssm_norm_g=ssm_norm_g.shape, w_out=w_out.shape, ln_g=ln_g.shape,
                  ln_b=ln_b.shape)
    outs = []
    for kind in range(4):
        d = _unpack_small(sm[kind])
        d.update({n: r[kind] for n, r in res.items()})
        outs.extend(d[n].reshape(shapes[n]) for n in order)
    loss = _unpack_small(sm[0])["loss"][0]
    return (loss, gx.reshape(x.shape), *outs)
```

```python
import functools
import math

import numpy as np
import jax
import jax.numpy as jnp
from jax import lax
from jax.experimental import pallas as pl
from jax.experimental.pallas import tpu as pltpu

F32 = jnp.float32
BF16 = jnp.bfloat16
MESH_ID = pl.DeviceIdType.MESH

D = 1024
NH = 8
NOPE = 128
ROPE = 64
VD = 128
VDP = 144
QKD = NOPE + ROPE
HP = 256
QL = 384
KVL = 256
ROPE_THETA = 10000.0
SH = 16
SP = 64
SG = 2
SN = 128
CW = 4
CH = 128
SW = SH * SP
CC = SW + 2 * SG * SN
GW = SW // SG
MIX = 2 * D
IN_W = 4304
ALPHA = 2.0 ** 0.25
RMS_EPS = 1e-6
LN_EPS = 1e-5
SCALE = QKD ** -0.5
LN2 = math.log(2.0)
QSCALE = SCALE / LN2
LR, B1, B2, EPS, WD, STEP = 0.001, 0.9, 0.999, 1e-08, 0.01, 10

P_Q = (0, 384)
P_KV = (384, 640)
P_ZA = (640, 1664)
P_XBC = (1664, 3200)
P_ZS = (3200, 4224)
P_KR = (4224, 4352)
P_DT = (4352, 4480)
IN_P = 4480

R_SMALL = 16

TM = 256
TQ = 512
NSP = 2
NCK = 4
TQF = 1024
VMEM_LIMIT = 56 * 1024 * 1024


def _cp(sem=None):
    return pltpu.CompilerParams(dimension_semantics=sem, vmem_limit_bytes=VMEM_LIMIT)


def _mm(a, b):
    return jnp.dot(a, b, preferred_element_type=F32)


def _nt(a, b):
    return lax.dot_general(a, b, (((1,), (1,)), ((), ())), preferred_element_type=F32)


def _tn(a, b):
    return lax.dot_general(a, b, (((0,), (0,)), ((), ())), preferred_element_type=F32)


def _split(a, terms):
    parts = []
    for t in range(terms):
        p = a.astype(BF16)
        parts.append(p)
        if t + 1 < terms:
            a = a - p.astype(F32)
    return parts


def _mm_x(a, ones, terms):
    parts = _split(a, terms)
    out = _mm(parts[0], ones)
    for p in parts[1:]:
        out = out + _mm(p, ones)
    return out


def _xmm(ones, a, terms):
    parts = _split(a, terms)
    out = _mm(ones, parts[0])
    for p in parts[1:]:
        out = out + _mm(ones, p)
    return out


def _nt_x(a, ones, terms):
    parts = _split(a, terms)
    out = _nt(parts[0], ones)
    for p in parts[1:]:
        out = out + _nt(p, ones)
    return out


def _sigmoid(z):
    return 1.0 / (1.0 + jnp.exp(-z))


def _softplus(z):
    return jnp.maximum(z, 0.0) + jnp.log1p(jnp.exp(-jnp.abs(z)))


def _rope(t, cs, sn):
    lane = lax.broadcasted_iota(jnp.int32, t.shape, 1)
    rot = jnp.where(lane < ROPE // 2, -pltpu.roll(t, 128 - ROPE // 2, 1), pltpu.roll(t, ROPE // 2, 1))
    return t * cs + rot * sn


def _rope_t(t, cs, sn):
    lane = lax.broadcasted_iota(jnp.int32, t.shape, 1)
    y = t * sn
    rot = jnp.where(lane < ROPE // 2, -pltpu.roll(y, 128 - ROPE // 2, 1), pltpu.roll(y, ROPE // 2, 1))
    return t * cs - rot


def _full(shape):
    n = len(shape)
    return pl.BlockSpec(shape, lambda *_: (0,) * n)


def _const(shape):
    n = len(shape)
    return pl.BlockSpec(shape, lambda *_: (0,) * n, pipeline_mode=pl.Buffered(1))


def _gather_weights(shards):
    n = len(shards)
    halves = [a.shape[0] // 2 for a in shards]

    def body(*refs):
        srcs, dsts = refs[:n], refs[n:2 * n]
        send_sems, recv_sems, local_sems = refs[2 * n:]
        x, y, c = lax.axis_index("x"), lax.axis_index("y"), lax.axis_index("c")
        me = 2 * x + y
        sibling = (x, y, 1 - c)
        chips = [(1 - x, y), (x, 1 - y), (1 - x, 1 - y)]

        def rows(a, pc):
            return pl.ds(pl.multiple_of(pc * halves[a], halves[a]), halves[a])

        def copy(a, k, src, slot, pc, to):
            return pltpu.make_async_remote_copy(
                src_ref=src, dst_ref=dsts[a].at[slot, rows(a, pc)], send_sem=send_sems.at[a, k],
                recv_sem=recv_sems.at[a, k], device_id=to, device_id_type=MESH_ID)

        local = [pltpu.make_async_copy(srcs[a], dsts[a].at[me], local_sems.at[a]) for a in range(n)]
        for cp in local:
            cp.start()
        sends = [copy(a, j, srcs[a].at[rows(a, c)], me, c, (px, py, c)) for a in range(n) for j, (px, py) in enumerate(chips)]
        for cp in sends:
            cp.start()
        passed = []
        for a in range(n):
            for j, (px, py) in enumerate(chips):
                k = 2 * px + py
                copy(a, j, srcs[a].at[rows(a, c)], k, c, (x, y, c)).wait_recv()
                fwd = copy(a, 3 + j, dsts[a].at[k, rows(a, c)], k, c, sibling)
                fwd.start()
                passed.append(fwd)
        for a in range(n):
            for j, (px, py) in enumerate(chips):
                copy(a, 3 + j, srcs[a].at[rows(a, c)], 2 * px + py, 1 - c, (x, y, c)).wait_recv()
        for cp in sends + passed:
            cp.wait_send()
        for cp in local:
            cp.wait()

    hbm = pl.BlockSpec(memory_space=pltpu.HBM)
    return pl.pallas_call(
        body, name="gather_weights",
        out_shape=tuple(jax.ShapeDtypeStruct((4,) + a.shape, a.dtype) for a in shards),
        in_specs=[hbm] * n, out_specs=tuple([hbm] * n),
        scratch_shapes=[pltpu.SemaphoreType.DMA((n, 6)), pltpu.SemaphoreType.DMA((n, 6)), pltpu.SemaphoreType.DMA((n,))],
    )(*shards)


def _exchange(name, slabs):
    n = len(slabs)

    def body(*refs):
        srcs, dsts = refs[:n], refs[n:2 * n]
        send_sems, recv_sems, local_sems = refs[2 * n:]
        x, y, c = lax.axis_index("x"), lax.axis_index("y"), lax.axis_index("c")
        chip = 2 * x + y
        sibling = (x, y, 1 - c)
        chips = [(1 - x, y), (x, 1 - y), (1 - x, 1 - y)]

        def slot(px, py, pc):
            return 4 * px + 2 * py + pc

        def copy(a, k, src, s, to):
            return pltpu.make_async_remote_copy(
                src_ref=src, dst_ref=dsts[a].at[s], send_sem=send_sems.at[a, k], recv_sem=recv_sems.at[a, k],
                device_id=to, device_id_type=MESH_ID)

        mine = slot(x, y, c)
        local = [pltpu.make_async_copy(srcs[a].at[chip], dsts[a].at[mine], local_sems.at[a]) for a in range(n)]
        for cp in local:
            cp.start()
        first = []
        for a in range(n):
            first.append(copy(a, 0, srcs[a].at[chip], mine, sibling))
            for j, (px, py) in enumerate(chips):
                first.append(copy(a, 1 + j, srcs[a].at[2 * px + py], mine, (px, py, c)))
        for cp in first:
            cp.start()
        passed = []
        for a in range(n):
            for j, (px, py) in enumerate(chips):
                s = slot(px, py, c)
                copy(a, 1 + j, srcs[a].at[chip], s, (x, y, c)).wait_recv()
                fwd = copy(a, 4 + j, dsts[a].at[s], s, sibling)
                fwd.start()
                passed.append(fwd)
        for a in range(n):
            copy(a, 0, srcs[a].at[chip], slot(x, y, 1 - c), (x, y, c)).wait_recv()
            for j, (px, py) in enumerate(chips):
                copy(a, 4 + j, srcs[a].at[chip], slot(px, py, 1 - c), (x, y, c)).wait_recv()
        for cp in first + passed:
            cp.wait_send()
        for cp in local:
            cp.wait()

    hbm = pl.BlockSpec(memory_space=pltpu.HBM)
    return pl.pallas_call(
        body, name=name,
        out_shape=tuple(jax.ShapeDtypeStruct((8,) + a.shape[1:], a.dtype) for a in slabs),
        in_specs=[hbm] * n, out_specs=tuple([hbm] * n),
        scratch_shapes=[pltpu.SemaphoreType.DMA((n, 7)), pltpu.SemaphoreType.DMA((n, 7)), pltpu.SemaphoreType.DMA((n,))],
    )(*slabs)


def _ada(call, w_shard):
    def body(c_ref, w_ref, o_ref):
        o_ref[...] = _mm(c_ref[...].astype(BF16), w_ref[...].astype(BF16))

    return pl.pallas_call(body, name="ada", out_shape=jax.ShapeDtypeStruct((8, w_shard.shape[1]), F32),
                          compiler_params=_cp())(call, w_shard)


def _inproj(x, mod, win, qg, kvg, wq, wk, wv, pos, invf):
    S = x.shape[0]
    tm = min(TM, S)

    def body(x_ref, mod_ref, win_ref, qg_ref, kvg_ref, wq_ref, wk_ref, wv_ref, pos_ref, invf_ref,
             qlat_ref, ckv_ref, za_ref, xbc_ref, zs_ref, dt_ref, q_ref, k_ref, v_ref):
        shift = mod_ref[0:1, 0:D]
        scale = mod_ref[0:1, D:2 * D]
        u = (x_ref[...] * (1.0 + scale) + shift).astype(BF16)

        def proj(p):
            return _mm(u, win_ref[:, p[0]:p[1]])

        ql = proj(P_Q)
        ckv = proj(P_KV)
        qlat_ref[...] = ql
        ckv_ref[...] = ckv
        za_ref[...] = proj(P_ZA)
        xbc_ref[...] = proj(P_XBC)
        zs_ref[...] = proj(P_ZS)
        dt_ref[...] = proj(P_DT)
        kr = proj(P_KR)

        ang = pos_ref[...].astype(F32) * invf_ref[...]
        cs = jnp.cos(ang)
        sn = jnp.sin(ang)

        rq = lax.rsqrt(jnp.mean(ql * ql, axis=-1, keepdims=True) + RMS_EPS)
        qn = (ql * rq * qg_ref[...]).astype(BF16)
        for h in range(NH):
            qh = _mm(qn, wq_ref[:, h * HP:(h + 1) * HP])
            q_ref[h, :, 0:NOPE] = (qh[:, 0:NOPE] * QSCALE).astype(BF16)
            q_ref[h, :, NOPE:HP] = (_rope(qh[:, NOPE:HP], cs, sn) * QSCALE).astype(BF16)

        rk = lax.rsqrt(jnp.mean(ckv * ckv, axis=-1, keepdims=True) + RMS_EPS)
        kn = (ckv * rk * kvg_ref[...]).astype(BF16)
        knope = _mm(kn, wk_ref[...])
        vall = _mm(kn, wv_ref[...])
        krf = _rope(kr, cs, sn)
        krr = krf.astype(BF16)
        for h in range(NH):
            k_ref[h, :, 0:NOPE] = knope[:, h * NOPE:(h + 1) * NOPE].astype(BF16)
            k_ref[h, :, NOPE:HP] = krr
            v_ref[h] = vall[:, h * VD:(h + 1) * VD].astype(BF16)

    row = lambda n: pl.BlockSpec((tm, n), lambda i: (i, 0))
    heads = lambda n: pl.BlockSpec((NH, tm, n), lambda i: (0, i, 0))
    sd = lambda n: jax.ShapeDtypeStruct((S, n), F32)
    hd = lambda n: jax.ShapeDtypeStruct((NH, S, n), BF16)
    return pl.pallas_call(
        body, name="inproj", grid=(S // tm,),
        in_specs=[row(D), _const((8, 3 * D)), _const((D, IN_P)), _const((1, QL)), _const((1, KVL)),
                  _const((QL, NH * HP)), _const((KVL, NH * NOPE)), _const((KVL, NH * VD)),
                  row(1), _const((1, 128))],
        out_specs=[row(QL), row(KVL), row(D), row(CC), row(D), row(128), heads(HP), heads(HP), heads(VD)],
        out_shape=[sd(QL), sd(KVL), sd(D), sd(CC), sd(D), sd(128), hd(HP), hd(HP), hd(VD)],
        compiler_params=_cp(("arbitrary",)),
    )(x, mod, win, qg, kvg, wq, wk, wv, pos, invf)


def _attn_fwd(q, k, v):
    _, S, _ = q.shape
    tq = min(TQF, S)
    nq = S // tq
    half = tq // NSP
    tb = min(256, half)
    nsb = half // tb

    def body(q_ref, k_ref, v_ref, o_ref, lse_ref, vt_ref):
        i = pl.program_id(1)
        qb = q_ref[...]

        @pl.when(i == 0)
        def _():
            ones_rows = jnp.where(lax.broadcasted_iota(jnp.int32, (VDP - VD, tb), 0) == 0, 1.0, 0.0).astype(BF16)

            def fill(blk, carry):
                off = pl.multiple_of(blk * tb, tb)
                vt_ref[blk, 0:VD, :] = v_ref[pl.ds(off, tb), :].T
                vt_ref[blk, VD:VDP, :] = ones_rows
                return carry

            lax.fori_loop(0, S // tb, fill, 0)

        def scores(j, hb):
            off = pl.multiple_of(j * tq + hb * half, half)
            return _nt(k_ref[pl.ds(off, half), :], qb)

        def update(j, hb, s, carry):
            m, acc = carry
            m_new = jnp.maximum(m, jnp.max(s, axis=0, keepdims=True))
            a = jnp.exp2(m - m_new)
            pb = jnp.exp2(s - m_new).astype(BF16)
            acc = a * acc
            for sb in range(nsb):
                acc = acc + _mm(vt_ref[(NSP * j + hb) * nsb + sb], pb[sb * tb:(sb + 1) * tb, :])
            return m_new, acc

        def trip(j, carry, masked):
            s = [scores(j, hb) for hb in range(NSP)]
            if masked:
                r = lax.broadcasted_iota(jnp.int32, s[0].shape, 0)
                cidx = lax.broadcasted_iota(jnp.int32, s[0].shape, 1)
                s = [jnp.where(cidx >= r + hb * half, s[hb], -1e30) for hb in range(NSP)]
            for hb in range(NSP):
                carry = update(j, hb, s[hb], carry)
            return carry

        def finish(carry):
            m, acc = carry
            l = acc[VD:VD + 1, :]
            o_ref[...] = (acc[0:VD, :] / l).T
            lse_ref[...] = m + jnp.log2(l)

        def quad(t, cr):
            for u in range(4):
                cr = trip(4 * t + u, cr, False)
            return cr

        init = (jnp.full((1, tq), -1e30, F32), jnp.zeros((VDP, tq), F32))
        carry = lax.fori_loop(0, i >> 2, quad, init)
        carry = lax.fori_loop(i - (i & 3), i, lambda j, cr: trip(j, cr, False), carry)
        finish(trip(i, carry, True))

    return pl.pallas_call(
        body, name="attn_fwd", grid=(NH, nq),
        in_specs=[pl.BlockSpec((None, tq, HP), lambda h, i: (h, i, 0)),
                  pl.BlockSpec((None, S, HP), lambda h, i: (h, 0, 0)),
                  pl.BlockSpec((None, S, VD), lambda h, i: (h, 0, 0))],
        out_specs=[pl.BlockSpec((tq, VD), lambda h, i: (i, h)),
                   pl.BlockSpec((None, None, 1, tq), lambda h, i: (h, i, 0, 0))],
        out_shape=[jax.ShapeDtypeStruct((S, NH * VD), F32), jax.ShapeDtypeStruct((NH, nq, 1, tq), F32)],
        scratch_shapes=[pltpu.VMEM((S // tb, VDP, tb), BF16)],
        compiler_params=_cp(("arbitrary", "arbitrary")),
    )(q, k, v)


def _ssd_consts():
    tri = np.tril(np.ones((CH, CH), np.float32))
    e16 = np.zeros((128, SW), np.float32)
    for h in range(SH):
        e16[h, h * SP:(h + 1) * SP] = 1.0
    return jnp.asarray(tri, BF16), jnp.asarray(tri.T.copy(), BF16), jnp.asarray(e16, BF16)


def _ssd_conv(xraw, halo, cw_ref, cb_ref, ext):
    ext[0:8, :] = halo
    ext[8:8 + CH, :] = xraw
    cw = cw_ref[...]
    xc = cb_ref[...] + cw[0:1, :] * ext[5:5 + CH, :]
    for kk in range(1, CW):
        xc = xc + cw[kk:kk + 1, :] * ext[5 + kk:5 + kk + CH, :]
    return xc


def _ssd_chunk_common(xc, dtraw_ref, dtb_ref, alog_ref, tri_ref, e16_ref):
    sact = _sigmoid(xc)
    act = xc * sact
    lane = lax.broadcasted_iota(jnp.int32, (1, 128), 1)
    arow = jnp.where(lane < SH, -jnp.exp(alog_ref[...]), 0.0)
    dtpre = dtraw_ref[...] + dtb_ref[...]
    dt = _softplus(dtpre)
    a = dt * arow
    cum = _xmm(tri_ref[...], a, 3)
    cumx = _mm_x(cum, e16_ref[...], 3)
    dtx = _mm_x(dt, e16_ref[...], 2)
    return sact, act, arow, dtpre, dt, cum, cumx, dtx


def _ssd_fwd(xbc, dtraw, zs, conv_w, conv_b, dtb, alog, dskx, gssm):
    S = xbc.shape[0]
    nc = S // CH
    tri, _, e16 = _ssd_consts()

    def body(xbc_ref, halo_ref, dtraw_ref, zs_ref, cw_ref, cb_ref, dtb_ref, alog_ref, dsk_ref, g_ref, tri_ref, e16_ref,
             xc_ref, y_ref, htp_ref, o_ref, ht, ext):
        i = pl.program_id(0)

        @pl.when(i == 0)
        def _():
            ht[...] = jnp.zeros_like(ht)

        for c in range(NCK):
            chunk(c, i, xbc_ref, halo_ref, dtraw_ref, zs_ref, cw_ref, cb_ref, dtb_ref, alog_ref, dsk_ref, g_ref, tri_ref,
                  e16_ref, xc_ref, y_ref, htp_ref, o_ref, ht, ext)

    def chunk(c, i, xbc_ref, halo_ref, dtraw_ref, zs_ref, cw_ref, cb_ref, dtb_ref, alog_ref, dsk_ref, g_ref, tri_ref, e16_ref,
              xc_ref, y_ref, htp_ref, o_ref, ht, ext):
        rows = slice(c * CH, (c + 1) * CH)
        halo = jnp.where(i == 0, 0.0, halo_ref[...]) if c == 0 else xbc_ref[c * CH - 8:c * CH, :]
        xc = _ssd_conv(xbc_ref[rows, :], halo, cw_ref, cb_ref, ext)
        xc_ref[rows, :] = xc
        sact, act, arow, dtpre, dt, cum, cumx, dtx = _ssd_chunk_common(xc, dtraw_ref.at[rows, :], dtb_ref, alog_ref, tri_ref, e16_ref)
        cum_t = cum.T
        xs = act[:, 0:SW]
        lastx = cumx[CH - 1:CH, :]
        xh = xs * dtx
        eexp = jnp.exp(cumx)
        dte = jnp.exp(lastx - cumx)
        cdx = jnp.exp(lastx)
        htp = ht[...]
        htp_ref[c] = htp
        xw = (xh * dte).astype(BF16)
        xb = xh.astype(BF16)
        trim = tri_ref[...].astype(F32) > 0.5
        lane = lax.broadcasted_iota(jnp.int32, (CH, 128), 1)
        parts = []
        for g in range(SG):
            gl = slice(g * GW, (g + 1) * GW)
            bg = act[:, SW + g * SN:SW + (g + 1) * SN].astype(BF16)
            cg = act[:, SW + SG * SN + g * SN:SW + SG * SN + (g + 1) * SN].astype(BF16)
            cbm = _nt(cg, bg)
            yoff = eexp[:, gl] * _mm(cg, htp[:, gl].astype(BF16))
            ht[:, gl] = htp[:, gl] * cdx[:, gl] + _tn(bg, xw[:, gl])
            for pr in range(GW // 128):
                h0 = g * (SH // SG) + 2 * pr
                lo = g * GW + pr * 128
                xp = xb[:, lo:lo + 128]
                res = []
                for hh in (h0, h0 + 1):
                    seg = cum[:, hh:hh + 1] - cum_t[hh:hh + 1, :]
                    mh = jnp.where(trim, cbm * jnp.exp(seg), 0.0).astype(BF16)
                    res.append(_mm(mh, xp))
                parts.append(jnp.where(lane < SP, res[0], res[1]) + yoff[:, pr * 128:(pr + 1) * 128])
        y = jnp.concatenate(parts, axis=1) + xs * dsk_ref[...]
        y_ref[rows, :] = y
        z = zs_ref[rows, :]
        hf = y * (z * _sigmoid(z))
        outs = []
        for g in range(SG):
            hg = hf[:, g * GW:(g + 1) * GW]
            rs = lax.rsqrt(jnp.mean(hg * hg, axis=-1, keepdims=True) + RMS_EPS)
            outs.append(hg * rs)
        o_ref[rows, :] = (jnp.concatenate(outs, axis=1) * g_ref[...]).astype(BF16)

    row = lambda n: pl.BlockSpec((NCK * CH, n), lambda i: (i, 0))
    return pl.pallas_call(
        body, name="ssd_fwd", grid=(nc // NCK,),
        in_specs=[row(CC), pl.BlockSpec((8, CC), lambda i: (jnp.maximum(i * (NCK * CH // 8) - 1, 0), 0)), row(128), row(SW),
                  _const((CW, CC)), _const((1, CC)), _const((1, 128)), _const((1, 128)), _const((1, SW)), _const((1, SW)),
                  _const((CH, CH)), _const((128, SW))],
        out_specs=[row(CC), row(SW), pl.BlockSpec((NCK, SN, SW), lambda i: (i, 0, 0)), row(SW)],
        out_shape=[jax.ShapeDtypeStruct((S, CC), F32), jax.ShapeDtypeStruct((S, SW), F32),
                   jax.ShapeDtypeStruct((nc, SN, SW), F32), jax.ShapeDtypeStruct((S, SW), BF16)],
        scratch_shapes=[pltpu.VMEM((SN, SW), F32), pltpu.VMEM((8 + CH, CC), F32)],
        compiler_params=_cp(("arbitrary",)),
    )(xbc, xbc, dtraw, zs, conv_w, conv_b, dtb, alog, dskx, gssm, tri, e16)


def _outproj(o, za, ossm, x, tgt, wout, mod, ln_g, ln_b):
    S = x.shape[0]
    tm = min(TM, S)

    e8 = np.zeros((D, 128), np.float32)
    for h in range(NH):
        e8[h * VD:(h + 1) * VD, h] = 1.0
    e8 = jnp.asarray(e8, BF16)

    def body(o_ref, za_ref, os_ref, x_ref, t_ref, w_ref, mod_ref, g_ref, b_ref, e8_ref,
             gx_ref, do_ref, dza_ref, dos_ref, delta_ref, dw_ref, vec_ref):
        i = pl.program_id(0)

        @pl.when(i == 0)
        def _():
            dw_ref[...] = jnp.zeros_like(dw_ref)
            vec_ref[...] = jnp.zeros_like(vec_ref)

        gate = mod_ref[0:1, 2 * D:3 * D]
        ov = o_ref[...]
        z = za_ref[...]
        sz = _sigmoid(z)
        silz = z * sz
        a = (ov * silz).astype(BF16)
        osb = os_ref[...]
        mixed = _mm(a, w_ref[0:D, :]) + _mm(osb, w_ref[D:MIX, :])
        xv = x_ref[...]
        hres = ALPHA * xv + gate * mixed
        mu = jnp.mean(hres, axis=-1, keepdims=True)
        hc = hres - mu
        var = jnp.mean(hc * hc, axis=-1, keepdims=True)
        rstd = lax.rsqrt(var + LN_EPS)
        xhat = hc * rstd
        g = g_ref[...]
        yv = xhat * g + b_ref[...]
        err = yv - t_ref[...]
        dy = err * (1.0 / D)
        vec_ref[0:1, :] += jnp.sum(err * err, axis=0, keepdims=True)
        vec_ref[1:2, :] += jnp.sum(dy * xhat, axis=0, keepdims=True)
        vec_ref[2:3, :] += jnp.sum(dy, axis=0, keepdims=True)
        dxh = dy * g
        dh = rstd * (dxh - jnp.mean(dxh, axis=-1, keepdims=True) - xhat * jnp.mean(dxh * xhat, axis=-1, keepdims=True))
        gx_ref[...] = ALPHA * dh
        vec_ref[3:4, :] += jnp.sum(dh * mixed, axis=0, keepdims=True)
        dmixed = (gate * dh).astype(BF16)
        dw_ref[0:D, :] += _tn(a, dmixed)
        dw_ref[D:MIX, :] += _tn(osb, dmixed)
        da = _nt(dmixed, w_ref[0:D, :])
        dos_ref[...] = _nt(dmixed, w_ref[D:MIX, :])
        dov = da * silz
        do_ref[...] = dov.astype(BF16)
        dza_ref[...] = (da * ov * (sz * (1.0 + z * (1.0 - sz)))).astype(BF16)
        delta_ref[:, 0, :] = _mm_x(dov * ov, e8_ref[...], 2).T[0:NH, :]

    row = lambda n: pl.BlockSpec((tm, n), lambda i: (i, 0))
    return pl.pallas_call(
        body, name="outproj", grid=(S // tm,),
        in_specs=[row(D), row(D), row(D), row(D), row(D), _const((MIX, D)), _const((8, 3 * D)), _const((1, D)), _const((1, D)),
                  _const((D, 128))],
        out_specs=[row(D), row(D), row(D), row(D), pl.BlockSpec((NH, None, 1, tm), lambda i: (0, i, 0, 0)),
                   _full((MIX, D)), _full((8, D))],
        out_shape=[jax.ShapeDtypeStruct((S, D), F32), jax.ShapeDtypeStruct((S, D), BF16), jax.ShapeDtypeStruct((S, D), BF16),
                   jax.ShapeDtypeStruct((S, D), F32), jax.ShapeDtypeStruct((NH, S // tm, 1, tm), F32),
                   jax.ShapeDtypeStruct((MIX, D), F32), jax.ShapeDtypeStruct((8, D), F32)],
        compiler_params=_cp(("arbitrary",)),
    )(o, za, ossm, x, tgt, wout, mod, ln_g, ln_b, e8)


def _attn_bwd(q, k, v, do, lse, delta):
    _, S, _ = q.shape
    tk = min(TQ, S // 2)
    nk = S // tk
    tq = 2 * tk
    nq = S // tq

    def body(k_ref, v_ref, q_ref, do_ref, lse_ref, dl_ref, dk_ref, dv_ref, dq_ref, dqt_ref, dk_acc, dv_acc):
        j = pl.program_id(1)
        kb = k_ref[...]
        ktb = kb.T
        vb = v_ref[...]

        @pl.when(j == 0)
        def _():
            dqt_ref[...] = jnp.zeros_like(dqt_ref)

        dk_acc[...] = jnp.zeros_like(dk_acc)
        dv_acc[...] = jnp.zeros_like(dv_acc)

        def step(i, masked, lo=0):
            off = pl.multiple_of(i * tq + lo, tk)
            qb = q_ref[pl.ds(off, tq - lo), :]
            dob = do_ref[pl.ds(off, tq - lo), :]
            pt = jnp.exp2(_nt(kb, qb) - lse_ref[i][:, lo:tq])
            if masked:
                r = lax.broadcasted_iota(jnp.int32, pt.shape, 0)
                cidx = lax.broadcasted_iota(jnp.int32, pt.shape, 1)
                pt = jnp.where(i * tq + lo + cidx >= j * tk + r, pt, 0.0)
            dv_acc[...] += _mm(pt.astype(BF16), dob)
            dsb = (pt * (_nt(vb, dob) - dl_ref[i][:, lo:tq])).astype(BF16)
            dk_acc[...] += _mm(dsb, qb)
            dqt_ref[i, :, lo:tq] += _mm(ktb, dsb)

        first = j >> 1

        @pl.when((j & 1) == 0)
        def _():
            step(first, True)

        @pl.when((j & 1) == 1)
        def _():
            step(first, True, tk)
            dq_ref[...] = dqt_ref[first].T.astype(BF16)

        def loop_body(t, carry):
            for u in range(4):
                step(first + 1 + 4 * t + u, False)
            return carry

        def tail_body(i, carry):
            step(i, False)
            return carry

        rest = nq - 1 - first
        lax.fori_loop(0, rest >> 2, loop_body, 0)
        lax.fori_loop(nq - (rest & 3), nq, tail_body, 0)

        dk_ref[...] = (dk_acc[...] * LN2).astype(BF16)
        dv_ref[...] = dv_acc[...].astype(BF16)

    return pl.pallas_call(
        body, name="attn_bwd", grid=(NH, nk),
        in_specs=[pl.BlockSpec((None, tk, HP), lambda h, j: (h, j, 0)),
                  pl.BlockSpec((None, tk, VD), lambda h, j: (h, j, 0)),
                  pl.BlockSpec((None, S, HP), lambda h, j: (h, 0, 0)),
                  pl.BlockSpec((S, VD), lambda h, j: (0, h)),
                  pl.BlockSpec((None, nq, 1, tq), lambda h, j: (h, 0, 0, 0)),
                  pl.BlockSpec((None, nq, 1, tq), lambda h, j: (h, 0, 0, 0))],
        out_specs=[pl.BlockSpec((None, tk, HP), lambda h, j: (h, j, 0)),
                   pl.BlockSpec((None, tk, VD), lambda h, j: (h, j, 0)),
                   pl.BlockSpec((None, tq, HP), lambda h, j: (h, j >> 1, 0))],
        out_shape=[jax.ShapeDtypeStruct((NH, S, HP), BF16), jax.ShapeDtypeStruct((NH, S, VD), BF16),
                   jax.ShapeDtypeStruct((NH, S, HP), BF16)],
        scratch_shapes=[pltpu.VMEM((nq, HP, tq), F32), pltpu.VMEM((tk, HP), F32), pltpu.VMEM((tk, VD), F32)],
        compiler_params=_cp(("arbitrary", "arbitrary")),
    )(k, v, q, do, lse.reshape(NH, nq, 1, tq), delta.reshape(NH, nq, 1, tq))


def _ssd_bwd(xbc, xc, dtraw, zs, y, htp, dossm, conv_w, dtb, alog, dskx, gssm):
    S = xbc.shape[0]
    nc = S // CH
    tri, triu, e16 = _ssd_consts()

    def body(xbc_ref, xc_ref, dtraw_ref, zs_ref, y_ref, htp_ref, dos_ref,
             cw_ref, dtb_ref, alog_ref, dsk_ref, g_ref, tri_ref, triu_ref, e16_ref,
             dxbc_ref, ddt_ref, dzs_ref, dcw_ref, dcb_ref, dvec_ref, dg_ref,
             dht, dext, dskacc):
        r = pl.program_id(0)

        @pl.when(r == 0)
        def _():
            dht[...] = jnp.zeros_like(dht)
            dext[CH:CH + 8, :] = jnp.zeros((8, CC), F32)
            dskacc[...] = jnp.zeros_like(dskacc)
            dcw_ref[...] = jnp.zeros_like(dcw_ref)
            dcb_ref[...] = jnp.zeros_like(dcb_ref)
            dvec_ref[...] = jnp.zeros_like(dvec_ref)
            dg_ref[...] = jnp.zeros_like(dg_ref)

        for c in reversed(range(NCK)):
            rows = slice(c * CH, (c + 1) * CH)
            chunk(xbc_ref.at[rows, :], xc_ref.at[rows, :], dtraw_ref.at[rows, :], zs_ref.at[rows, :], y_ref.at[rows, :],
                  htp_ref.at[c], dos_ref.at[rows, :], cw_ref, dtb_ref, alog_ref, dsk_ref, g_ref, tri_ref, triu_ref, e16_ref,
                  dxbc_ref.at[rows, :], ddt_ref.at[rows, :], dzs_ref.at[rows, :], dcw_ref, dcb_ref, dvec_ref, dg_ref,
                  dht, dext, dskacc)

        @pl.when(r == nc // NCK - 1)
        def _():
            lane = lax.broadcasted_iota(jnp.int32, (1, 128), 1)
            arow = jnp.where(lane < SH, -jnp.exp(alog_ref[...]), 0.0)
            dvec_ref[1:2, :] = dvec_ref[1:2, :] * arow
            dvec_ref[2:3, :] = _nt_x(jnp.broadcast_to(dskacc[...], (8, SW)), e16_ref[...], 3)[0:1, :]

    def chunk(xbc_ref, xc_ref, dtraw_ref, zs_ref, y_ref, htp_ref, dos_ref,
              cw_ref, dtb_ref, alog_ref, dsk_ref, g_ref, tri_ref, triu_ref, e16_ref,
              dxbc_ref, ddt_ref, dzs_ref, dcw_ref, dcb_ref, dvec_ref, dg_ref,
              dht, dext, dskacc):
        xc = xc_ref[...]
        sact, act, arow, dtpre, dt, cum, cumx, dtx = _ssd_chunk_common(xc, dtraw_ref, dtb_ref, alog_ref, tri_ref, e16_ref)
        cum_t = cum.T
        xs = act[:, 0:SW]
        lastx = cumx[CH - 1:CH, :]
        xh = xs * dtx
        eexp = jnp.exp(cumx)
        dte = jnp.exp(lastx - cumx)
        cdx = jnp.exp(lastx)
        trim = tri_ref[...].astype(F32) > 0.5
        lane = lax.broadcasted_iota(jnp.int32, (CH, 128), 1)
        rowi = lax.broadcasted_iota(jnp.int32, (CH, 128), 0)

        yv = y_ref[...]
        z = zs_ref[...]
        sz = _sigmoid(z)
        silz = z * sz
        hf = yv * silz
        dn = dos_ref[...] * g_ref[...]
        dhf_parts, nrm_parts = [], []
        for g in range(SG):
            gl = slice(g * GW, (g + 1) * GW)
            hg = hf[:, gl]
            rs = lax.rsqrt(jnp.mean(hg * hg, axis=-1, keepdims=True) + RMS_EPS)
            ng = hg * rs
            dng = dn[:, gl]
            dhf_parts.append(rs * (dng - ng * jnp.mean(dng * ng, axis=-1, keepdims=True)))
            nrm_parts.append(ng)
        nrm = jnp.concatenate(nrm_parts, axis=1)
        dhf = jnp.concatenate(dhf_parts, axis=1)
        dg_ref[...] += jnp.sum(dos_ref[...] * nrm, axis=0, keepdims=True)
        dyv = dhf * silz
        dzs_ref[...] = (dhf * yv * (sz * (1.0 + z * (1.0 - sz)))).astype(BF16)
        dskacc[...] += jnp.sum(dyv * xs, axis=0, keepdims=True)
        dxs_skip = dyv * dsk_ref[...]

        dhtn = dht[...]
        hp = htp_ref[...]
        dlastx = jnp.sum(dhtn * hp, axis=0, keepdims=True) * cdx
        xb = xh.astype(BF16)
        xwf = xh * dte
        dcum = jnp.zeros((CH, 128), F32)
        dcum_t = jnp.zeros((128, CH), F32)
        dxh_parts, dcumx_parts, dlast_parts, db_parts, dc_parts = [], [], [], [], []
        for g in range(SG):
            gl = slice(g * GW, (g + 1) * GW)
            bg = act[:, SW + g * SN:SW + (g + 1) * SN].astype(BF16)
            cg = act[:, SW + SG * SN + g * SN:SW + SG * SN + (g + 1) * SN].astype(BF16)
            hpg = hp[:, gl].astype(BF16)
            dhn = dhtn[:, gl].astype(BF16)
            dyg = dyv[:, gl]
            dz = (dyg * eexp[:, gl]).astype(BF16)
            dcg = _nt(dz, hpg)
            dht[:, gl] = dhtn[:, gl] * cdx[:, gl] + _tn(cg, dz)
            yoff = eexp[:, gl] * _mm(cg, hpg)
            dcumx_g = dyg * yoff
            dbg = _nt(xwf[:, gl].astype(BF16), dhn)
            dxw = _mm(bg, dhn)
            ddte = dxw * xwf[:, gl]
            dcumx_parts.append(dcumx_g - ddte)
            dlast_parts.append(jnp.sum(ddte, axis=0, keepdims=True))
            dxh_g = dxw * dte[:, gl]
            cbm = _nt(cg, bg)
            dcb = jnp.zeros((CH, CH), F32)
            dxp_parts = []
            for pr in range(GW // 128):
                h0 = g * (SH // SG) + 2 * pr
                lo = g * GW + pr * 128
                xp = xb[:, lo:lo + 128]
                dyp = dyv[:, lo:lo + 128]
                dxp = jnp.zeros((CH, 128), F32)
                for idx, hh in enumerate((h0, h0 + 1)):
                    decay = jnp.where(trim, jnp.exp(cum[:, hh:hh + 1] - cum_t[hh:hh + 1, :]), 0.0)
                    mh = cbm * decay
                    keep = (lane < SP) if idx == 0 else (lane >= SP)
                    dym = jnp.where(keep, dyp, 0.0).astype(BF16)
                    dm = _nt(dym, xp)
                    dxp = dxp + _tn(mh.astype(BF16), dym)
                    gm = dm * mh
                    dcum = dcum + jnp.where(lane == hh, jnp.sum(gm, axis=1, keepdims=True), 0.0)
                    dcum_t = dcum_t - jnp.where(rowi == hh, jnp.sum(gm, axis=0, keepdims=True), 0.0)
                    dcb = dcb + dm * decay
                dxp_parts.append(dxp)
            dxh_parts.append(dxh_g + jnp.concatenate(dxp_parts, axis=1))
            dcbb = dcb.astype(BF16)
            dc_parts.append(dcg + _mm(dcbb, bg))
            db_parts.append(dbg + _tn(dcbb, cg))
        dxh = jnp.concatenate(dxh_parts, axis=1)
        dcumx = jnp.concatenate(dcumx_parts, axis=1)
        dlastx = dlastx + jnp.concatenate(dlast_parts, axis=1)
        e16 = e16_ref[...]
        dlast128 = _nt_x(jnp.broadcast_to(dlastx, (8, SW)), e16, 2)[0:1, :]
        dcum = dcum + dcum_t.T + _nt_x(dcumx, e16, 2) + jnp.where(rowi == CH - 1, dlast128, 0.0)
        da = _xmm(triu_ref[...], dcum, 2)
        ddt = da * arow + _nt_x(dxh * xs, e16, 2)
        dvec_ref[1:2, :] += jnp.sum(da * dt, axis=0, keepdims=True)
        ddtraw = jnp.where(lane < SH, ddt * _sigmoid(dtpre), 0.0)
        dvec_ref[0:1, :] += jnp.sum(ddtraw, axis=0, keepdims=True)
        ddt_ref[...] = ddtraw.astype(BF16)
        dxs = dxs_skip + dxh * dtx
        dact = jnp.concatenate([dxs] + db_parts + dc_parts, axis=1)
        dxc = dact * (sact * (1.0 + xc * (1.0 - sact)))

        dcb_ref[...] += jnp.sum(dxc, axis=0, keepdims=True)
        dext[0:CH, :] = dxc
        cw = cw_ref[...]
        xraw = xbc_ref[...]
        dxr = cw[CW - 1:CW, :] * dxc
        dcw_ref[CW - 1:CW, :] += jnp.sum(dxc * xraw, axis=0, keepdims=True)
        for kk in range(CW - 1):
            dwin = dext[CW - 1 - kk:CW - 1 - kk + CH, :]
            dxr = dxr + cw[kk:kk + 1, :] * dwin
            dcw_ref[kk:kk + 1, :] += jnp.sum(dwin * xraw, axis=0, keepdims=True)
        dxbc_ref[...] = dxr.astype(BF16)
        dext[CH:CH + 8, :] = dxc[0:8, :]

    ng = nc // NCK
    rev = lambda n: pl.BlockSpec((NCK * CH, n), lambda r: (ng - 1 - r, 0))
    return pl.pallas_call(
        body, name="ssd_bwd", grid=(ng,),
        in_specs=[rev(CC), rev(CC),
                  rev(128), rev(SW), rev(SW), pl.BlockSpec((NCK, SN, SW), lambda r: (ng - 1 - r, 0, 0)), rev(SW),
                  _const((CW, CC)), _const((1, 128)), _const((1, 128)), _const((1, SW)), _const((1, SW)),
                  _const((CH, CH)), _const((CH, CH)), _const((128, SW))],
        out_specs=[rev(CC), rev(128), rev(SW), _full((CW, CC)), _full((1, CC)), _full((8, 128)), _full((1, SW))],
        out_shape=[jax.ShapeDtypeStruct((S, CC), BF16), jax.ShapeDtypeStruct((S, 128), BF16), jax.ShapeDtypeStruct((S, SW), BF16),
                   jax.ShapeDtypeStruct((CW, CC), F32), jax.ShapeDtypeStruct((1, CC), F32),
                   jax.ShapeDtypeStruct((8, 128), F32), jax.ShapeDtypeStruct((1, SW), F32)],
        scratch_shapes=[pltpu.VMEM((SN, SW), F32), pltpu.VMEM((CH + 8, CC), F32), pltpu.VMEM((1, SW), F32)],
        compiler_params=_cp(("arbitrary",)),
    )(xbc, xc, dtraw, zs, y, htp, dossm, conv_w, dtb, alog, dskx, gssm, tri, triu, e16)


def _mla_bwd(dq, dk, dv, qlat, ckv, qg, kvg, wq, wk, wv, pos, invf):
    S = qlat.shape[0]
    tm = min(TQ, S)

    def body(dq_ref, dk_ref, dv_ref, ql_ref, ckv_ref, qg_ref, kvg_ref, wq_ref, wk_ref, wv_ref, pos_ref, invf_ref,
             dql_ref, dckv_ref, dkr_ref, dwq_ref, dwk_ref, dwv_ref, dqg_ref, dkvg_ref):
        i = pl.program_id(0)

        @pl.when(i == 0)
        def _():
            dwq_ref[...] = jnp.zeros_like(dwq_ref)
            dwk_ref[...] = jnp.zeros_like(dwk_ref)
            dwv_ref[...] = jnp.zeros_like(dwv_ref)
            dqg_ref[...] = jnp.zeros_like(dqg_ref)
            dkvg_ref[...] = jnp.zeros_like(dkvg_ref)

        hm = tm // 2
        for hf in range(2):
            rows = slice(hf * hm, (hf + 1) * hm)
            half(dq_ref.at[:, rows, :], dk_ref.at[:, rows, :], dv_ref.at[:, rows, :], ql_ref.at[rows, :], ckv_ref.at[rows, :],
                 qg_ref, kvg_ref, wq_ref, wk_ref, wv_ref, pos_ref.at[rows, :], invf_ref,
                 dql_ref.at[rows, :], dckv_ref.at[rows, :], dkr_ref.at[rows, :], dwq_ref, dwk_ref, dwv_ref, dqg_ref, dkvg_ref)

    def half(dq_ref, dk_ref, dv_ref, ql_ref, ckv_ref, qg_ref, kvg_ref, wq_ref, wk_ref, wv_ref, pos_ref, invf_ref,
             dql_ref, dckv_ref, dkr_ref, dwq_ref, dwk_ref, dwv_ref, dqg_ref, dkvg_ref):
        ang = pos_ref[...].astype(F32) * invf_ref[...]
        cs = jnp.cos(ang)
        sn = jnp.sin(ang)

        def rms_bwd(v, g, dn, dg_ref):
            r = lax.rsqrt(jnp.mean(v * v, axis=-1, keepdims=True) + RMS_EPS)
            vh = v * r
            dg_ref[...] += jnp.sum(dn * vh, axis=0, keepdims=True)
            dvh = dn * g
            return vh, r * (dvh - vh * jnp.mean(dvh * vh, axis=-1, keepdims=True))

        pieces = []
        for h in range(NH):
            dqh = dq_ref[h].astype(F32)
            pieces.append(dqh[:, 0:NOPE] * SCALE)
            pieces.append(_rope_t(dqh[:, NOPE:HP], cs, sn) * SCALE)
        dqf = jnp.concatenate(pieces, axis=1).astype(BF16)
        ql = ql_ref[...]
        g = qg_ref[...]
        dqn = _nt(dqf, wq_ref[...])
        qh, dql = rms_bwd(ql, g, dqn, dqg_ref)
        dwq_ref[...] += _tn((qh * g).astype(BF16), dqf)
        dql_ref[...] = dql.astype(BF16)

        dkn_p = jnp.concatenate([dk_ref[h, :, 0:NOPE] for h in range(NH)], axis=1)
        dvf = jnp.concatenate([dv_ref[h] for h in range(NH)], axis=1)
        dkr = dk_ref[0, :, NOPE:HP].astype(F32)
        for h in range(1, NH):
            dkr = dkr + dk_ref[h, :, NOPE:HP].astype(F32)
        lane = lax.broadcasted_iota(jnp.int32, dkr.shape, 1)
        dkr_ref[...] = jnp.where(lane < ROPE, _rope_t(dkr, cs, sn), 0.0).astype(BF16)
        cv = ckv_ref[...]
        gk = kvg_ref[...]
        dkn = _nt(dkn_p, wk_ref[...]) + _nt(dvf, wv_ref[...])
        kh, dckv = rms_bwd(cv, gk, dkn, dkvg_ref)
        knb = (kh * gk).astype(BF16)
        dwk_ref[...] += _tn(knb, dkn_p)
        dwv_ref[...] += _tn(knb, dvf)
        dckv_ref[...] = dckv.astype(BF16)

    row = lambda n: pl.BlockSpec((tm, n), lambda i: (i, 0))
    heads = lambda n: pl.BlockSpec((NH, tm, n), lambda i: (0, i, 0))
    return pl.pallas_call(
        body, name="mla_bwd", grid=(S // tm,),
        in_specs=[heads(HP), heads(HP), heads(VD), row(QL), row(KVL), _const((1, QL)), _const((1, KVL)),
                  _const((QL, NH * HP)), _const((KVL, NH * NOPE)), _const((KVL, NH * VD)), row(1), _const((1, 128))],
        out_specs=[row(QL), row(KVL), row(128), _full((QL, NH * HP)), _full((KVL, NH * NOPE)), _full((KVL, NH * VD)),
                   _full((1, QL)), _full((1, KVL))],
        out_shape=[jax.ShapeDtypeStruct((S, QL), BF16), jax.ShapeDtypeStruct((S, KVL), BF16), jax.ShapeDtypeStruct((S, 128), BF16),
                   jax.ShapeDtypeStruct((QL, NH * HP), F32), jax.ShapeDtypeStruct((KVL, NH * NOPE), F32),
                   jax.ShapeDtypeStruct((KVL, NH * VD), F32), jax.ShapeDtypeStruct((1, QL), F32), jax.ShapeDtypeStruct((1, KVL), F32)],
        compiler_params=_cp(("arbitrary",)),
    )(dq, dk, dv, qlat, ckv, qg, kvg, wq, wk, wv, pos, invf)


def _inproj_bwd(x, gx1, mod, win, dql, dckv, dza, dxbc, dzs, dkr, ddt):
    S = x.shape[0]
    tm = min(TM, S)

    def body(x_ref, gx1_ref, mod_ref, win_ref, dql_ref, dckv_ref, dza_ref, dxbc_ref, dzs_ref, dkr_ref, ddt_ref,
             gx_ref, dw_ref, vec_ref):
        i = pl.program_id(0)

        @pl.when(i == 0)
        def _():
            dw_ref[...] = jnp.zeros_like(dw_ref)
            vec_ref[...] = jnp.zeros_like(vec_ref)

        shift = mod_ref[0:1, 0:D]
        scale = mod_ref[0:1, D:2 * D]
        xv = x_ref[...]
        ut = (xv * (1.0 + scale) + shift).T.astype(BF16)
        pieces = (dql_ref, dckv_ref, dza_ref, dxbc_ref, dzs_ref, dkr_ref, ddt_ref)
        du = jnp.zeros((tm, D), F32)
        lo = 0
        for p_ref in pieces:
            n = p_ref.shape[1]
            dp = p_ref[...]
            du = du + _nt(dp, win_ref[:, lo:lo + n])
            dw_ref[:, lo:lo + n] += _mm(ut, dp)
            lo += n
        vec_ref[0:1, :] += jnp.sum(du, axis=0, keepdims=True)
        vec_ref[1:2, :] += jnp.sum(du * xv, axis=0, keepdims=True)
        gx_ref[...] = gx1_ref[...] + du * (1.0 + scale)

    row = lambda n: pl.BlockSpec((tm, n), lambda i: (i, 0))
    return pl.pallas_call(
        body, name="inproj_bwd", grid=(S // tm,),
        in_specs=[row(D), row(D), _const((8, 3 * D)), _const((D, IN_P)), row(QL), row(KVL), row(D), row(CC), row(D),
                  row(128), row(128)],
        out_specs=[row(D), pl.BlockSpec((D, IN_P), lambda i: (0, 0), pipeline_mode=pl.Buffered(1)), _full((8, D))],
        out_shape=[jax.ShapeDtypeStruct((S, D), F32), jax.ShapeDtypeStruct((D, IN_P), F32), jax.ShapeDtypeStruct((8, D), F32)],
        compiler_params=_cp(("arbitrary",)),
    )(x, gx1, mod, win, dql, dckv, dza, dxbc, dzs, dkr, ddt)


def _ada_bwd(callt, dmods):
    w = dmods.shape[1]

    def body(c_ref, d_ref, o_ref):
        acc = c_ref[:, 0:1] * d_ref[0:1, :]
        for s in range(1, 8):
            acc = acc + c_ref[:, s:s + 1] * d_ref[s:s + 1, :]
        o_ref[0] = acc

    return pl.pallas_call(body, name="ada_bwd", out_shape=jax.ShapeDtypeStruct((1, D, w), F32),
                          compiler_params=_cp())(callt, dmods)


def _adamw(name, parts, w, m, v):
    rows, ncol = w.shape
    tr = min(rows, 128)
    nparts = parts.shape[0]

    def body(p_ref, w_ref, m_ref, v_ref, g_ref, d_ref, nm_ref, nv_ref):
        g = p_ref[0].astype(F32)
        for s in range(1, nparts):
            g = g + p_ref[s].astype(F32)
        g_ref[...] = g
        nm = B1 * m_ref[...] + (1.0 - B1) * g
        nv = B2 * v_ref[...] + (1.0 - B2) * (g * g)
        nm_ref[...] = nm
        nv_ref[...] = nv
        m_hat = nm / (1.0 - B1 ** STEP)
        v_hat = nv / (1.0 - B2 ** STEP)
        d_ref[...] = -LR * (m_hat / (jnp.sqrt(v_hat) + EPS) + WD * w_ref[...])

    row = pl.BlockSpec((tr, ncol), lambda i: (i, 0))
    sd = jax.ShapeDtypeStruct((rows, ncol), F32)
    return pl.pallas_call(
        body, name="adamw_" + name, grid=(rows // tr,),
        in_specs=[pl.BlockSpec((nparts, tr, ncol), lambda i: (0, i, 0)), row, row, row],
        out_specs=[row, row, row, row], out_shape=[sd, sd, sd, sd],
        compiler_params=_cp(("arbitrary",)),
    )(parts, w, m, v)


_SMALL = (("b_ada", 3 * D), ("conv_w", CW * CC // 4), ("conv_b", CC), ("ssm_norm_g", SW), ("ln_g", D), ("ln_b", D),
          ("q_norm_g", QL), ("kv_norm_g", KVL), ("dt_bias", SH), ("a_log", SH), ("d_skip", SH), ("loss", 128))


def _pack_small(d, lead):
    flat = [d[name].reshape(d[name].shape[:lead] + (-1,)) for name, _ in _SMALL]
    used = sum(f.shape[lead] for f in flat)
    pad = jnp.zeros(flat[0].shape[:lead] + (R_SMALL * 1024 - used,), F32)
    return jnp.concatenate(flat + [pad], axis=lead).reshape(flat[0].shape[:lead] + (R_SMALL, 1024))


def _unpack_small(p):
    flat = p.reshape(-1)
    out, r = {}, 0
    for name, n in _SMALL:
        out[name] = flat[r:r + n]
        r += n
    return out


def _in_to_padded(w):
    z = lambda n: jnp.zeros((w.shape[0], n), w.dtype)
    return jnp.concatenate([w[:, 0:384], w[:, 384:640], w[:, 704:1728], w[:, 1728:3264], w[:, 3280:4304],
                            w[:, 640:704], z(64), w[:, 3264:3280], z(112)], axis=1)


def _in_from_padded(g):
    return jnp.concatenate([g[:, 0:384], g[:, 384:640], g[:, P_KR[0]:P_KR[0] + 64], g[:, 640:1664], g[:, 1664:3200],
                            g[:, P_DT[0]:P_DT[0] + 16], g[:, 3200:4224]], axis=1)


def kernel(x, c, positions, w_ada, b_ada, w_in, q_norm_g, w_qb, kv_norm_g, w_kvb, conv_w, conv_b, dt_bias, a_log, d_skip, ssm_norm_g, w_out, ln_g, ln_b, loss_target, m_w_ada, m_b_ada, m_w_in, m_q_norm_g, m_w_qb, m_kv_norm_g, m_w_kvb, m_conv_w, m_conv_b, m_dt_bias, m_a_log, m_d_skip, m_ssm_norm_g, m_w_out, m_ln_g, m_ln_b, v_w_ada, v_b_ada, v_w_in, v_q_norm_g, v_w_qb, v_kv_norm_g, v_w_kvb, v_conv_w, v_conv_b, v_dt_bias, v_a_log, v_d_skip, v_ssm_norm_g, v_w_out, v_ln_g, v_ln_b):
    S = x.shape[1]
    xv = x[0]
    tgt = loss_target[0]

    cw16 = jnp.concatenate([conv_w[0], jnp.zeros((16 - CW, CC // 4), F32)], axis=0)
    f_in, f_qb, f_kvb, f_out, f_cw = _gather_weights(
        [w_in[0].astype(BF16), w_qb[0].astype(BF16), w_kvb[0].astype(BF16), w_out[0].astype(BF16), cw16])
    cat1 = lambda f: jnp.concatenate([f[k] for k in range(4)], axis=1)
    win = _in_to_padded(cat1(f_in))
    wqb = cat1(f_qb).reshape(QL, NH, QKD)
    wq = jnp.concatenate([wqb, jnp.zeros((QL, NH, HP - QKD), BF16)], axis=2).reshape(QL, NH * HP)
    wkvb = cat1(f_kvb).reshape(KVL, NH, NOPE + VD)
    wk = wkvb[:, :, 0:NOPE].reshape(KVL, NH * NOPE)
    wv = wkvb[:, :, NOPE:].reshape(KVL, NH * VD)
    wout = f_out.reshape(MIX, D)
    cwf = cat1(f_cw[:, 0:CW, :])

    half = ROPE // 2
    invf = 1.0 / (ROPE_THETA ** (jnp.arange(half, dtype=F32) / half))
    invf = jnp.concatenate([invf, invf, jnp.zeros((128 - ROPE,), F32)]).reshape(1, 128)
    pos = positions.reshape(S, 1)
    pad128 = lambda a: jnp.concatenate([a.reshape(1, SH), jnp.zeros((1, 128 - SH), F32)], axis=1)
    dtb, alog = pad128(dt_bias), pad128(a_log)
    dskx = jnp.repeat(d_skip.reshape(SH), SP).reshape(1, SW)

    my_c = lax.axis_index("c")
    (call,) = _exchange("gather_c", [jnp.broadcast_to(c.reshape(1, 1, D), (4, 1, D))])
    call = call.reshape(8, D)
    mods = _ada(call, w_ada[0])
    (mrows,) = _exchange("scatter_mod", [mods.reshape(4, 2, 3 * D // 4)])
    mine = lax.dynamic_index_in_dim(mrows.reshape(4, 2, 2, 3 * D // 4)[:, 0], my_c, axis=1, keepdims=False)
    mod = jnp.broadcast_to(mine.reshape(1, 3 * D) + b_ada, (8, 3 * D))
    qlat, ckv, za, xbc, zs, dtraw, q, k, v = _inproj(xv, mod, win, q_norm_g, kv_norm_g, wq, wk, wv, pos, invf)
    o, lse = _attn_fwd(q, k, v)
    xc, y, htp, ossm = _ssd_fwd(xbc, dtraw, zs, cwf, conv_b, dtb, alog, dskx, ssm_norm_g)
    gx1, do, dza, dossm, delta, dwout, vec_o = _outproj(o, za, ossm, xv, tgt, wout, mod, ln_g, ln_b)
    loss_part = jnp.zeros((128,), F32).at[0].set(0.5 / D * jnp.sum(vec_o[0]))

    dk, dv, dq = _attn_bwd(q, k, v, do, lse, delta)
    dxbc, ddt, dzs, dcw, dcb, dvec, dgssm = _ssd_bwd(xbc, xc, dtraw, zs, y, htp, dossm, cwf, dtb, alog, dskx, ssm_norm_g)
    dql, dckv, dkr, dwq, dwk, dwv, dqg, dkvg = _mla_bwd(dq, dk, dv, qlat, ckv, q_norm_g, kv_norm_g, wq, wk, wv, pos, invf)
    gx, dwin, vec_i = _inproj_bwd(xv, gx1, mod, win, dql, dckv, dza, dxbc, dzs, dkr, ddt)
    dmod = jnp.concatenate([vec_i[0:1], vec_i[1:2], vec_o[3:4]], axis=1)

    cols = lambda g: g.reshape(g.shape[0], 4, g.shape[1] // 4).transpose(1, 0, 2)
    g_in = cols(_in_from_padded(dwin)).astype(BF16)
    g_qb = cols(dwq.reshape(QL, NH, HP)[:, :, 0:QKD].reshape(QL, NH * QKD)).astype(BF16)
    g_kvb = cols(jnp.concatenate([dwk.reshape(KVL, NH, NOPE), dwv.reshape(KVL, NH, VD)], axis=2)
                 .reshape(KVL, NH * (NOPE + VD))).astype(BF16)
    g_out = dwout.reshape(4, MIX // 4, D).astype(BF16)
    small = {"b_ada": dmod, "conv_b": dcb, "ssm_norm_g": dgssm, "ln_g": vec_o[1:2], "ln_b": vec_o[2:3],
             "q_norm_g": dqg, "kv_norm_g": dkvg, "dt_bias": dvec[0:1, 0:SH], "a_log": dvec[1:2, 0:SH], "d_skip": dvec[2:3, 0:SH],
             "loss": loss_part}
    small = {n: jnp.broadcast_to(a.reshape(1, -1), (4, a.size)) for n, a in small.items()}
    small["conv_w"] = cols(dcw).reshape(4, CW * CC // 4)
    gsmall = _pack_small(small, 1)

    r_in, r_qb, r_kvb, r_out, rs, dmods = _exchange(
        "exchange_grads", [g_in, g_qb, g_kvb, g_out, gsmall, jnp.broadcast_to(dmod.reshape(1, 1, 3 * D), (4, 1, 3 * D))])
    chip = 2 * lax.axis_index("x") + lax.axis_index("y")
    dmods = lax.dynamic_slice_in_dim(dmods.reshape(8, 3 * D), chip * (3 * D // 4), 3 * D // 4, axis=1)
    g_ada = _ada_bwd(call.T, dmods)
    res = dict(w_ada=_adamw("w_ada", g_ada, w_ada[0], m_w_ada[0], v_w_ada[0]),
               w_in=_adamw("w_in", r_in, w_in[0], m_w_in[0], v_w_in[0]),
               w_qb=_adamw("w_qb", r_qb, w_qb[0], m_w_qb[0], v_w_qb[0]),
               w_kvb=_adamw("w_kvb", r_kvb, w_kvb[0], m_w_kvb[0], v_w_kvb[0]),
               w_out=_adamw("w_out", r_out, w_out[0], m_w_out[0], v_w_out[0]))
    wsm = _pack_small(dict(b_ada=b_ada, conv_w=conv_w, conv_b=conv_b, ssm_norm_g=ssm_norm_g, ln_g=ln_g, ln_b=ln_b,
                           q_norm_g=q_norm_g, kv_norm_g=kv_norm_g, dt_bias=dt_bias, a_log=a_log, d_skip=d_skip, loss=jnp.zeros((128,), F32)), 0)
    msm = _pack_small(dict(b_ada=m_b_ada, conv_w=m_conv_w, conv_b=m_conv_b, ssm_norm_g=m_ssm_norm_g, ln_g=m_ln_g, ln_b=m_ln_b,
                           q_norm_g=m_q_norm_g, kv_norm_g=m_kv_norm_g, dt_bias=m_dt_bias, a_log=m_a_log, d_skip=m_d_skip, loss=jnp.zeros((128,), F32)), 0)
    vsm = _pack_small(dict(b_ada=v_b_ada, conv_w=v_conv_w, conv_b=v_conv_b, ssm_norm_g=v_ssm_norm_g, ln_g=v_ln_g, ln_b=v_ln_b,
                           q_norm_g=v_q_norm_g, kv_norm_g=v_kv_norm_g, dt_bias=v_dt_bias, a_log=v_a_log, d_skip=v_d_skip, loss=jnp.zeros((128,), F32)), 0)
    sm = _adamw("small", rs, wsm, msm, vsm)

    order = ["w_ada", "b_ada", "w_in", "q_norm_g", "w_qb", "kv_norm_g", "w_kvb", "conv_w", "conv_b", "dt_bias", "a_log",
             "d_skip", "ssm_norm_g", "w_out", "ln_g", "ln_b"]
    shapes = dict(w_ada=w_ada.shape, b_ada=b_ada.shape, w_in=w_in.shape, q_norm_g=q_norm_g.shape, w_qb=w_qb.shape,
                  kv_norm_g=kv_norm_g.shape, w_kvb=w_kvb.shape, conv_w=conv_w.shape, conv_b=conv_b.shape, dt_bias=dt_bias.shape,
                  a_log=a_log.shape, d_skip=d_skip.shape, ssm_norm_g=ssm_norm_g.shape, w_out=w_out.shape, ln_g=ln_g.shape,
                  ln_b=ln_b.shape)
    outs = []
    for kind in range(4):
        d = _unpack_small(sm[kind])
        d.update({n: r[kind] for n, r in res.items()})
        outs.extend(d[n].reshape(shapes[n]) for n in order)
    loss = _unpack_small(sm[0])["loss"][0]
    return (loss, gx.reshape(x.shape), *outs)
```
